```python
import jax, jax.numpy as jnp
from jax import lax
import numpy as np

D_MODEL = 1024
BATCH = 4
SEQ = 8192
DEPTH = 2
DEC_BATCH = 8
DEC_SEQ = 32
PAST_LEN = 4096

CHUNK = 64
N_BRANCH = 4
BRANCH_W = D_MODEL // N_BRANCH
S5_GROUP = 16
S5_GROUPS = BRANCH_W // S5_GROUP
S5_STATE = 64
RW_HEAD = 64
RW_HEADS = BRANCH_W // RW_HEAD
RW_DECAY_LORA = 64
RW_AAA_LORA = 64
RW_GATE_LORA = 128
RW_COLS = 3 * BRANCH_W + RW_DECAY_LORA + RW_AAA_LORA + RW_GATE_LORA
RW_SPLITS = (BRANCH_W, 2 * BRANCH_W, 3 * BRANCH_W, 3 * BRANCH_W + RW_DECAY_LORA, 3 * BRANCH_W + RW_DECAY_LORA + RW_AAA_LORA)
RW_EPS = 64e-5
SG_CHUNK = 128
SG_GROUPS = 4
SG_GROUP_W = BRANCH_W // SG_GROUPS
GD_HEAD = 64
GD_HEADS = BRANCH_W // GD_HEAD
GD_CONV = 4
GD_QKV = 3 * BRANCH_W
GD_COLS = GD_QKV + BRANCH_W + 2 * GD_HEADS
GATE_COLS = N_BRANCH * D_MODEL
COL_SPLITS = (BRANCH_W, BRANCH_W + RW_COLS, BRANCH_W + RW_COLS + 2 * BRANCH_W, BRANCH_W + RW_COLS + 2 * BRANCH_W + GD_COLS)
IN_COLS = BRANCH_W + RW_COLS + 2 * BRANCH_W + GD_COLS + GATE_COLS
D_FF = 2816
N_EXPERTS = 8
TOP_K = 2
D_FF_EXPERT = 3584
MOE_BLOCK = 512
N_DENSE = (DEPTH + 1) // 2
N_MOE = DEPTH // 2
NORM_EPS = 1e-6
F32 = jnp.float32

kernel_name = 'hybrid_s5_rwkv7_sgu_gdn_stream_step'


def rmsnorm(x, w):
    xf = x.astype(F32)
    y = xf * lax.rsqrt(jnp.mean(xf * xf, axis=-1, keepdims=True) + NORM_EPS)
    return (y * w).astype(x.dtype)


def l2norm(x):
    return x * lax.rsqrt(jnp.sum(x * x, axis=-1, keepdims=True) + NORM_EPS)


def _complex_affine_combine(e1, e2):
    a1r, a1i, b1r, b1i = e1
    a2r, a2i, b2r, b2i = e2
    return (a1r * a2r - a1i * a2i, a1r * a2i + a1i * a2r,
            a2r * b1r - a2i * b1i + b2r, a2r * b1i + a2i * b1r + b2i)


def s5_mixer(u, h0, lam_re, lam_im, log_dt, b_c, c_c, d_skip, w_glu, b_glu):
    bsz, t, _ = u.shape
    uf = u.astype(F32)
    ug = uf.reshape(bsz, t, S5_GROUPS, S5_GROUP)
    dt = jnp.exp(log_dt.astype(F32))[:, None]
    lr, li = lam_re.astype(F32), lam_im.astype(F32)
    mag = jnp.exp(lr * dt)
    ab_re, ab_im = mag * jnp.cos(li * dt), mag * jnp.sin(li * dt)
    den = lr * lr + li * li
    nr = ab_re - 1.0
    cf_re = (nr * lr + ab_im * li) / den
    cf_im = (ab_im * lr - nr * li) / den
    br, bi = b_c[..., 0], b_c[..., 1]
    bb_re = cf_re[..., None] * br - cf_im[..., None] * bi
    bb_im = cf_re[..., None] * bi + cf_im[..., None] * br
    x_re = jnp.einsum('btgh,gph->btgp', ug, bb_re)
    x_im = jnp.einsum('btgh,gph->btgp', ug, bb_im)
    h0r, h0i = h0[..., 0].astype(F32), h0[..., 1].astype(F32)
    x_re = x_re.at[:, 0].add(ab_re * h0r - ab_im * h0i)
    x_im = x_im.at[:, 0].add(ab_re * h0i + ab_im * h0r)
    a_re = jnp.broadcast_to(ab_re, x_re.shape)
    a_im = jnp.broadcast_to(ab_im, x_im.shape)
    _, _, hr, hi = lax.associative_scan(_complex_affine_combine, (a_re, a_im, x_re, x_im), axis=1)
    cr, ci = c_c[..., 0], c_c[..., 1]
    y = jnp.einsum('btgp,ghp->btgh', hr, cr) - jnp.einsum('btgp,ghp->btgh', hi, ci)
    y = jax.nn.gelu(y.reshape(bsz, t, BRANCH_W) + d_skip * uf)
    y = y * jax.nn.sigmoid(y @ w_glu + b_glu)
    h_last = jnp.stack([hr[:, -1], hi[:, -1]], axis=-1)
    return y.astype(u.dtype), h_last.astype(h0.dtype)


def rwkv7_mixer(z, shift0, s0, mu, w0, w2, a0, a2, g2, k_k, k_a, r_k, ln_w, ln_b):
    bsz, t, _ = z.shape
    zf = z.astype(F32)
    prev = jnp.concatenate([shift0.astype(F32)[:, None], zf[:, :-1]], axis=1)
    zm = zf + (prev - zf) * mu
    r, k, v, w_lo, a_lo, g_lo = jnp.split(zm, RW_SPLITS, axis=-1)
    w_log = -jax.nn.softplus(-(w0 + jnp.tanh(w_lo) @ w2)) - 0.5
    decay = jnp.exp(-jnp.exp(w_log))
    a = jax.nn.sigmoid(a0 + a_lo @ a2)
    g = jax.nn.sigmoid(g_lo) @ g2
    heads = lambda y: y.reshape(bsz, t, RW_HEADS, RW_HEAD)
    kk = l2norm(heads(k * k_k))
    k = k * (1.0 + (a - 1.0) * k_a)
    r_h, k_h, v_h, a_h, w_h = heads(r), heads(k), heads(v), heads(a), heads(decay)
    tm = lambda y: jnp.moveaxis(y, 1, 0)

    def step(s, inp):
        w_t, kk_t, kka_t, k_t, v_t, r_t = inp
        sk = jnp.einsum('bhvk,bhk->bhv', s, kk_t)
        s = s * w_t[:, :, None, :] - sk[..., None] * kka_t[:, :, None, :] + v_t[..., None] * k_t[:, :, None, :]
        return s, jnp.einsum('bhvk,bhk->bhv', s, r_t)

    s_last, y = lax.scan(step, s0.astype(F32), (tm(w_h), tm(kk), tm(kk * a_h), tm(k_h), tm(v_h), tm(r_h)))
    y = jnp.moveaxis(y, 0, 1)
    mean = jnp.mean(y, axis=-1, keepdims=True)
    var = jnp.mean(jnp.square(y - mean), axis=-1, keepdims=True)
    y = ((y - mean) * lax.rsqrt(var + RW_EPS)).reshape(bsz, t, BRANCH_W) * ln_w + ln_b
    bonus = jnp.sum(r_h * k_h * r_k.reshape(RW_HEADS, RW_HEAD), axis=-1, keepdims=True) * v_h
    y = (y + bonus.reshape(bsz, t, BRANCH_W)) * g
    return y.astype(z.dtype), z[:, -1].astype(shift0.dtype), s_last.astype(s0.dtype)


def sgu_mixer(z, ln_w, ln_b, w_s, b_s):
    bsz, t, _ = z.shape
    u, v = jnp.split(jax.nn.gelu(z.astype(F32)), 2, axis=-1)
    mean = jnp.mean(v, axis=-1, keepdims=True)
    var = jnp.mean(jnp.square(v - mean), axis=-1, keepdims=True)
    v = (v - mean) * lax.rsqrt(var + NORM_EPS) * ln_w + ln_b
    blk = min(SG_CHUNK, t)
    n_blk = t // blk
    mask = jnp.tril(jnp.ones((blk, blk), F32))
    wm = w_s[:, :blk, :blk] * mask
    vc = v.reshape(bsz, n_blk, blk, SG_GROUPS, SG_GROUP_W)
    mixed = jnp.einsum('gij,bnjgc->bnigc', wm, vc) + jnp.transpose(b_s[:, :blk])[:, :, None]
    out = u * mixed.reshape(bsz, t, BRANCH_W)
    return out.astype(z.dtype), v.astype(z.dtype)


def chunk_gated_delta(q, k, v, g, beta, s0):
    bsz, t, nh, dh = q.shape
    blk = min(CHUNK, t)
    n_blk = t // blk
    to_blocks = lambda y: y.reshape(bsz, n_blk, blk, nh, -1).transpose(1, 0, 3, 2, 4)
    qc, kc, vc = to_blocks(q), to_blocks(k), to_blocks(v)
    gc = jnp.cumsum(g.reshape(bsz, n_blk, blk, nh).transpose(1, 0, 3, 2), axis=-1)
    bc = beta.reshape(bsz, n_blk, blk, nh).transpose(1, 0, 3, 2)
    tril = jnp.tril(jnp.ones((blk, blk), bool))
    strict = jnp.tril(jnp.ones((blk, blk), bool), k=-1)
    diff = gc[..., :, None] - gc[..., None, :]
    decay = jnp.where(tril, jnp.exp(jnp.where(tril, diff, 0.0)), 0.0)
    kb = kc * bc[..., None]
    vb = vc * bc[..., None]
    lm = jnp.where(strict, jnp.einsum('nbhid,nbhjd->nbhij', kb, kc) * decay, 0.0)
    eye = jnp.eye(blk, dtype=F32)
    tmat = lax.linalg.triangular_solve(lm + eye, jnp.broadcast_to(eye, lm.shape),
                                       left_side=True, lower=True, unit_diagonal=True)
    uc = jnp.einsum('nbhij,nbhjd->nbhid', tmat, vb)
    wc = jnp.einsum('nbhij,nbhjd->nbhid', tmat, kb * jnp.exp(gc)[..., None])
    qk = jnp.einsum('nbhid,nbhjd->nbhij', qc, kc) * decay

    def step(s, inp):
        q_i, k_i, u_i, w_i, g_i, a_i = inp
        v_new = u_i - jnp.einsum('bhld,bhde->bhle', w_i, s)
        o = jnp.einsum('bhld,bhde->bhle', q_i * jnp.exp(g_i)[..., None], s) + jnp.einsum('bhij,bhje->bhie', a_i, v_new)
        g_last = g_i[..., -1]
        s = s * jnp.exp(g_last)[..., None, None] + jnp.einsum('bhld,bhle->bhde', k_i * jnp.exp(g_last[..., None] - g_i)[..., None], v_new)
        return s, o

    s_last, oc = lax.scan(step, s0, (qc, kc, uc, wc, gc, qk))
    o = oc.transpose(1, 0, 3, 2, 4).reshape(bsz, t, nh, dh)
    return o, s_last


def gdn_mixer(z, conv0, s0, conv_w, a_log, dt_bias, norm_w):
    bsz, t, _ = z.shape
    zf = z.astype(F32)
    qkv, gate, a_in, b_in = jnp.split(zf, (GD_QKV, GD_QKV + BRANCH_W, GD_QKV + BRANCH_W + GD_HEADS), axis=-1)
    xp = jnp.concatenate([conv0.astype(F32), qkv], axis=1)
    conv = xp[:, 0:t] * conv_w[0]
    for i in range(1, GD_CONV):
        conv = conv + xp[:, i:i + t] * conv_w[i]
    q, k, v = jnp.split(jax.nn.silu(conv), 3, axis=-1)
    heads = lambda y: y.reshape(bsz, t, GD_HEADS, GD_HEAD)
    q = l2norm(heads(q)) * (GD_HEAD ** -0.5)
    k = l2norm(heads(k))
    v = heads(v)
    beta = jax.nn.sigmoid(b_in)
    g = -jnp.exp(a_log) * jax.nn.softplus(a_in + dt_bias)
    o, s_last = chunk_gated_delta(q, k, v, g, beta, s0.astype(F32))
    o = o * lax.rsqrt(jnp.mean(o * o, axis=-1, keepdims=True) + NORM_EPS) * norm_w * jax.nn.silu(heads(gate))
    return (o.reshape(bsz, t, BRANCH_W).astype(z.dtype), xp[:, -(GD_CONV - 1):].astype(conv0.dtype),
            s_last.astype(s0.dtype))


def swiglu(h, w1, w3, w2):
    return (jax.nn.silu(h @ w1) * (h @ w3)) @ w2


def moe_ffn(x2, w_router, b_router, w1, w3, w2):
    n_tok, dm = x2.shape
    logits = x2.astype(F32) @ w_router.astype(F32) + b_router.astype(F32)
    top_v, top_i = lax.top_k(logits, TOP_K)
    gates = jax.nn.softmax(top_v, axis=-1)
    n_asg = n_tok * TOP_K
    e_flat = top_i.reshape(-1).astype(jnp.int32)
    t_flat = jnp.repeat(jnp.arange(n_tok, dtype=jnp.int32), TOP_K)
    order = jnp.argsort(e_flat, stable=True)
    e_sorted = e_flat[order]
    counts = jnp.zeros((N_EXPERTS,), jnp.int32).at[e_flat].add(1)
    padded = (counts + MOE_BLOCK - 1) // MOE_BLOCK * MOE_BLOCK
    starts = jnp.cumsum(counts) - counts
    pends = jnp.cumsum(padded)
    pstarts = pends - padded
    dest = pstarts[e_sorted] + jnp.arange(n_asg, dtype=jnp.int32) - starts[e_sorted]
    n_blocks = -(-n_asg // MOE_BLOCK) + N_EXPERTS
    n_slots = n_blocks * MOE_BLOCK
    slot_tok = jnp.full((n_slots,), n_tok, jnp.int32).at[dest].set(t_flat[order])
    slot_gate = jnp.zeros((n_slots,), F32).at[dest].set(gates.reshape(-1)[order])
    block_e = jnp.minimum(jnp.searchsorted(pends, jnp.arange(n_blocks, dtype=jnp.int32) * MOE_BLOCK, side='right'),
                          N_EXPERTS - 1)
    x_pad = jnp.concatenate([x2, jnp.zeros((1, dm), x2.dtype)], axis=0)
    xb = x_pad[slot_tok].reshape(n_blocks, MOE_BLOCK, dm)

    def expert_block(args):
        xblk, e = args
        return (jax.nn.silu(xblk @ w1[e]) * (xblk @ w3[e])) @ w2[e]

    yb = lax.map(expert_block, (xb, block_e))
    y = jnp.zeros((n_tok + 1, dm), F32).at[slot_tok].add(yb.reshape(n_slots, dm).astype(F32) * slot_gate[:, None])
    return y[:n_tok].astype(x2.dtype)


def mixer_block(h, p, l, s5_h0, rw_s0, rw_shift0, gd_s0, gd_conv0):
    bsz, t, _ = h.shape
    proj = h @ p['w_in'][l]
    u_s5, z_rw, z_sg, z_gd, z_gate = jnp.split(proj, COL_SPLITS, axis=-1)
    y_a, s5_h = s5_mixer(u_s5, s5_h0, p['s5_lam_re'][l], p['s5_lam_im'][l], p['s5_log_dt'][l], p['s5_b'][l],
                         p['s5_c'][l], p['s5_d'][l], p['s5_w_glu'][l], p['s5_b_glu'][l])
    y_b, rw_shift, rw_s = rwkv7_mixer(z_rw, rw_shift0, rw_s0, p['rw_mu'][l], p['rw_w0'][l], p['rw_w2'][l],
                                      p['rw_a0'][l], p['rw_a2'][l], p['rw_g2'][l], p['rw_k_k'][l], p['rw_k_a'][l],
                                      p['rw_r_k'][l], p['rw_ln_w'][l], p['rw_ln_b'][l])
    y_c, sg_v = sgu_mixer(z_sg, p['sg_ln_w'][l], p['sg_ln_b'][l], p['sg_w_s'][l], p['sg_b_s'][l])
    y_d, gd_conv, gd_s = gdn_mixer(z_gd, gd_conv0, gd_s0, p['gd_conv_w'][l], p['gd_a_log'][l], p['gd_dt_bias'][l],
                                   p['gd_norm_w'][l])
    ys = jnp.stack([y_a, y_b, y_c, y_d], axis=2)
    branch = jnp.einsum('btnc,ncd->btnd', ys, p['w_branch'][l])
    gates = jax.nn.sigmoid(z_gate.reshape(bsz, t, N_BRANCH, D_MODEL))
    merged = jnp.sum(gates * branch, axis=2)
    return merged @ p['w_out'][l], (s5_h, rw_s, rw_shift, gd_s, gd_conv, sg_v)


def run_trunk(x, s5_h, rw_s, rw_shift, gd_s, gd_conv, p):
    new = ([], [], [], [], [], [])
    for l in range(DEPTH):
        mix, states = mixer_block(rmsnorm(x, p['norm1_w'][l]), p, l, s5_h[l], rw_s[l], rw_shift[l], gd_s[l], gd_conv[l])
        x = x + mix
        h2 = rmsnorm(x, p['norm2_w'][l])
        j = l // 2
        if l % 2 == 0:
            x = x + swiglu(h2, p['ffn_w1'][j], p['ffn_w3'][j], p['ffn_w2'][j])
        else:
            x = x + moe_ffn(h2.reshape(-1, D_MODEL), p['moe_router'][j], p['moe_router_b'][j], p['moe_w1'][j],
                            p['moe_w3'][j], p['moe_w2'][j]).reshape(x.shape)
        for lst, s in zip(new, states):
            lst.append(s)
    y = rmsnorm(x, p['final_norm_w'])
    return y, [jnp.stack(lst) for lst in new]


def setup_inputs(seed: int = 0) -> dict:
    key = jax.random.key(seed)
    ks = iter(jax.random.split(key, 64))
    nrm = lambda shape, scale: jax.random.normal(next(ks), shape, F32) * scale
    uni = lambda shape, lo, hi: jax.random.uniform(next(ks), shape, F32, lo, hi)
    gain = lambda shape: 1.0 + nrm(shape, 0.02)
    lam_im = jnp.broadcast_to(jnp.pi * jnp.arange(S5_STATE, dtype=F32), (DEPTH, S5_GROUPS, S5_STATE))
    gd_dt = jnp.exp(uni((DEPTH, GD_HEADS), float(np.log(1e-3)), float(np.log(1e-1))))
    return {
        'x_prompt': nrm((BATCH, SEQ, D_MODEL), 1.0),
        'x_sample': nrm((DEC_BATCH, DEC_SEQ, D_MODEL), 1.0),
        'state_s5': nrm((DEPTH, DEC_BATCH, S5_GROUPS, S5_STATE, 2), 0.5),
        'state_rwkv': nrm((DEPTH, DEC_BATCH, RW_HEADS, RW_HEAD, RW_HEAD), 0.3),
        'state_rwkv_shift': nrm((DEPTH, DEC_BATCH, RW_COLS), 1.0),
        'state_gdn': nrm((DEPTH, DEC_BATCH, GD_HEADS, GD_HEAD, GD_HEAD), 0.1),
        'state_gdn_conv': nrm((DEPTH, DEC_BATCH, GD_CONV - 1, GD_QKV), 1.0),
        'norm1_w': gain((DEPTH, D_MODEL)),
        'w_in': nrm((DEPTH, D_MODEL, IN_COLS), D_MODEL ** -0.5),
        's5_lam_re': -0.5 + uni((DEPTH, S5_GROUPS, S5_STATE), -0.02, 0.02),
        's5_lam_im': lam_im + nrm((DEPTH, S5_GROUPS, S5_STATE), 0.01),
        's5_log_dt': uni((DEPTH, S5_GROUPS), float(np.log(1e-3)), float(np.log(1e-1))),
        's5_b': nrm((DEPTH, S5_GROUPS, S5_STATE, S5_GROUP, 2), S5_GROUP ** -0.5),
        's5_c': nrm((DEPTH, S5_GROUPS, S5_GROUP, S5_STATE, 2), S5_STATE ** -0.5),
        's5_d': nrm((DEPTH, BRANCH_W), 1.0),
        's5_w_glu': nrm((DEPTH, BRANCH_W, BRANCH_W), BRANCH_W ** -0.5),
        's5_b_glu': nrm((DEPTH, BRANCH_W), 0.01),
        'rw_mu': uni((DEPTH, RW_COLS), 0.0, 1.0),
        'rw_w0': uni((DEPTH, BRANCH_W), -6.0, -1.0),
        'rw_w2': nrm((DEPTH, RW_DECAY_LORA, BRANCH_W), 0.5 * RW_DECAY_LORA ** -0.5),
        'rw_a0': nrm((DEPTH, BRANCH_W), 0.1),
        'rw_a2': nrm((DEPTH, RW_AAA_LORA, BRANCH_W), RW_AAA_LORA ** -0.5),
        'rw_g2': nrm((DEPTH, RW_GATE_LORA, BRANCH_W), RW_GATE_LORA ** -0.5),
        'rw_k_k': 0.85 + nrm((DEPTH, BRANCH_W), 0.02),
        'rw_k_a': gain((DEPTH, BRANCH_W)),
        'rw_r_k': nrm((DEPTH, BRANCH_W), 0.05),
        'rw_ln_w': gain((DEPTH, BRANCH_W)),
        'rw_ln_b': nrm((DEPTH, BRANCH_W), 0.01),
        'sg_ln_w': gain((DEPTH, BRANCH_W)),
        'sg_ln_b': nrm((DEPTH, BRANCH_W), 0.01),
        'sg_w_s': nrm((DEPTH, SG_GROUPS, SG_CHUNK, SG_CHUNK), SG_CHUNK ** -0.5),
        'sg_b_s': gain((DEPTH, SG_GROUPS, SG_CHUNK)),
        'gd_conv_w': nrm((DEPTH, GD_CONV, GD_QKV), GD_CONV ** -0.5),
        'gd_a_log': jnp.log(uni((DEPTH, GD_HEADS), 1.0, 16.0)),
        'gd_dt_bias': gd_dt + jnp.log(-jnp.expm1(-gd_dt)),
        'gd_norm_w': gain((DEPTH, GD_HEAD)),
        'w_branch': nrm((DEPTH, N_BRANCH, BRANCH_W, D_MODEL), BRANCH_W ** -0.5),
        'w_out': nrm((DEPTH, D_MODEL, D_MODEL), D_MODEL ** -0.5),
        'norm2_w': gain((DEPTH, D_MODEL)),
        'ffn_w1': nrm((N_DENSE, D_MODEL, D_FF), D_MODEL ** -0.5),
        'ffn_w3': nrm((N_DENSE, D_MODEL, D_FF), D_MODEL ** -0.5),
        'ffn_w2': nrm((N_DENSE, D_FF, D_MODEL), D_FF ** -0.5),
        'moe_router': nrm((N_MOE, D_MODEL, N_EXPERTS), D_MODEL ** -0.5),
        'moe_router_b': nrm((N_MOE, N_EXPERTS), 0.01),
        'moe_w1': nrm((N_MOE, N_EXPERTS, D_MODEL, D_FF_EXPERT), D_MODEL ** -0.5),
        'moe_w3': nrm((N_MOE, N_EXPERTS, D_MODEL, D_FF_EXPERT), D_MODEL ** -0.5),
        'moe_w2': nrm((N_MOE, N_EXPERTS, D_FF_EXPERT, D_MODEL), D_FF_EXPERT ** -0.5),
        'final_norm_w': gain((D_MODEL,)),
    }


def reference(x_prompt, x_sample, state_s5, state_rwkv, state_rwkv_shift, state_gdn, state_gdn_conv,
              norm1_w, w_in, s5_lam_re, s5_lam_im, s5_log_dt, s5_b, s5_c, s5_d, s5_w_glu, s5_b_glu,
              rw_mu, rw_w0, rw_w2, rw_a0, rw_a2, rw_g2, rw_k_k, rw_k_a, rw_r_k, rw_ln_w, rw_ln_b,
              sg_ln_w, sg_ln_b, sg_w_s, sg_b_s, gd_conv_w, gd_a_log, gd_dt_bias, gd_norm_w,
              w_branch, w_out, norm2_w, ffn_w1, ffn_w3, ffn_w2,
              moe_router, moe_router_b, moe_w1, moe_w3, moe_w2, final_norm_w):
    p = {
        'norm1_w': norm1_w, 'w_in': w_in, 's5_lam_re': s5_lam_re, 's5_lam_im': s5_lam_im, 's5_log_dt': s5_log_dt,
        's5_b': s5_b, 's5_c': s5_c, 's5_d': s5_d, 's5_w_glu': s5_w_glu, 's5_b_glu': s5_b_glu,
        'rw_mu': rw_mu, 'rw_w0': rw_w0, 'rw_w2': rw_w2, 'rw_a0': rw_a0, 'rw_a2': rw_a2, 'rw_g2': rw_g2,
        'rw_k_k': rw_k_k, 'rw_k_a': rw_k_a, 'rw_r_k': rw_r_k, 'rw_ln_w': rw_ln_w, 'rw_ln_b': rw_ln_b,
        'sg_ln_w': sg_ln_w, 'sg_ln_b': sg_ln_b, 'sg_w_s': sg_w_s, 'sg_b_s': sg_b_s,
        'gd_conv_w': gd_conv_w, 'gd_a_log': gd_a_log, 'gd_dt_bias': gd_dt_bias, 'gd_norm_w': gd_norm_w,
        'w_branch': w_branch, 'w_out': w_out, 'norm2_w': norm2_w,
        'ffn_w1': ffn_w1, 'ffn_w3': ffn_w3, 'ffn_w2': ffn_w2,
        'moe_router': moe_router, 'moe_router_b': moe_router_b, 'moe_w1': moe_w1, 'moe_w3': moe_w3, 'moe_w2': moe_w2,
        'final_norm_w': final_norm_w,
    }
    bp, dt = x_prompt.shape[0], x_prompt.dtype
    y_prompt, (s5_p, rw_p, rwsh_p, gd_p, gdc_p, _) = run_trunk(
        x_prompt,
        jnp.zeros((DEPTH, bp, S5_GROUPS, S5_STATE, 2), dt),
        jnp.zeros((DEPTH, bp, RW_HEADS, RW_HEAD, RW_HEAD), dt),
        jnp.zeros((DEPTH, bp, RW_COLS), dt),
        jnp.zeros((DEPTH, bp, GD_HEADS, GD_HEAD, GD_HEAD), dt),
        jnp.zeros((DEPTH, bp, GD_CONV - 1, GD_QKV), dt),
        p)
    y_sample, (s5_s, rw_s, rwsh_s, gd_s, gdc_s, sgv_s) = run_trunk(
        x_sample, state_s5, state_rwkv, state_rwkv_shift, state_gdn, state_gdn_conv, p)
    return (y_prompt, y_sample, s5_p, rw_p, rwsh_p, gd_p, gdc_p, s5_s, rw_s, rwsh_s, gd_s, gdc_s, sgv_s)
```

```python
import functools

import jax
import jax.numpy as jnp
from jax import lax
from jax.experimental import pallas as pl
from jax.experimental.pallas import tpu as pltpu

F32 = jnp.float32
BF16 = jnp.bfloat16

D_MODEL = 1024
BRANCH_W = 256
HEADS = 4
HEAD_W = 64
S5_GROUPS = 16
S5_GROUP = 16
S5_STATE = 64
S5_W = S5_GROUPS * S5_STATE
SG_CHUNK = 128
GD_CONV = 4
GD_QKV = 3 * BRANCH_W
RW_COLS = 1024
RW_EPS = 64e-5
NORM_EPS = 1e-6
N_EXPERTS = 8
LANES = 128

COL_GATE = 0
COL_RW = 4096
COL_GD = 5120
COL_SG = 6144
COL_S5 = 6656
COL_AB = 6912
PROJ_COLS = 7040

VMEM_LIMIT = 48 * 1024 * 1024


def _cparams(sem):
    return pltpu.CompilerParams(dimension_semantics=sem, vmem_limit_bytes=VMEM_LIMIT)


def _dot(a, b):
    return jnp.dot(a.astype(BF16), b.astype(BF16), preferred_element_type=F32)


def _dot_nt(a, b):
    return lax.dot_general(a.astype(BF16), b.astype(BF16), (((1,), (1,)), ((), ())),
                           preferred_element_type=F32)


def _split3(a):
    hi = a.astype(BF16)
    r1 = a - hi.astype(F32)
    mid = r1.astype(BF16)
    lo = (r1 - mid.astype(F32)).astype(BF16)
    return hi, mid, lo


def _dot3(a, b_exact):
    hi, mid, lo = _split3(a)
    b = b_exact.astype(BF16)
    return (jnp.dot(hi, b, preferred_element_type=F32) + jnp.dot(mid, b, preferred_element_type=F32)
            + jnp.dot(lo, b, preferred_element_type=F32))


def _dot3_left(b_exact, a):
    hi, mid, lo = _split3(a)
    b = b_exact.astype(BF16)
    return (jnp.dot(b, hi, preferred_element_type=F32) + jnp.dot(b, mid, preferred_element_type=F32)
            + jnp.dot(b, lo, preferred_element_type=F32))


def _dot_hp(a, b):
    a0, a1, a2 = _split3(a)
    b0, b1, b2 = _split3(b)
    d = lambda x, y: jnp.dot(x, y, preferred_element_type=F32)
    return d(a0, b0) + (d(a0, b1) + d(a1, b0)) + (d(a0, b2) + d(a1, b1) + d(a2, b0))


def _iota(shape, axis):
    return lax.broadcasted_iota(jnp.int32, shape, axis)


def _head_ones():
    r = _iota((BRANCH_W, BRANCH_W), 0) // HEAD_W
    c = _iota((BRANCH_W, BRANCH_W), 1) // HEAD_W
    return (r == c).astype(BF16)


def _head_mask(l):
    r = _iota((HEADS * l, BRANCH_W), 0) // l
    c = _iota((HEADS * l, BRANCH_W), 1) // HEAD_W
    return r == c


def _expand(x, mask):
    return jnp.where(mask, jnp.concatenate([x] * HEADS, axis=0), 0.0)


def _fold(xe, l):
    out = xe[0:l]
    for h in range(1, HEADS):
        out = out + xe[h * l:(h + 1) * l]
    return out


def _tri_masks(l):
    n = HEADS * l
    i = _iota((n, n), 0)
    j = _iota((n, n), 1)
    same = (i // l) == (j // l)
    return same & (j < i), same & (j <= i)


def _unit_lower_inverse(a_strict, l):
    n = a_strict.shape[0]
    eye = (_iota((n, n), 0) == _iota((n, n), 1)).astype(F32)
    neg = -a_strict
    t = eye + neg
    p = neg
    k = 2
    while k < l:
        p = _dot(p, p)
        t = t + _dot(t, p)
        k *= 2
    return t


def _cumsum_rows(x, l):
    tri = (_iota((l, l), 1) <= _iota((l, l), 0)).astype(BF16)
    return _dot3_left(tri, x)


def _softplus(x):
    return jnp.maximum(x, 0.0) + jnp.log(1.0 + jnp.exp(-jnp.abs(x)))


def _silu(x):
    return x * jax.nn.sigmoid(x)


def _norm_proj_kernel(x_ref, nw_ref, w_ref, o_ref, h_ref):
    @pl.when(pl.program_id(1) == 0)
    def _():
        x = x_ref[...]
        ms = jnp.mean(x * x, axis=-1, keepdims=True)
        h_ref[...] = (x * lax.rsqrt(ms + NORM_EPS) * nw_ref[...]).astype(BF16)

    o_ref[...] = jnp.dot(h_ref[...], w_ref[...], preferred_element_type=F32)


def _norm_proj(x2, norm_w, w_bf16, tm, tn):
    n = x2.shape[0]
    ncol = w_bf16.shape[1]
    return pl.pallas_call(
        _norm_proj_kernel,
        grid=(n // tm, ncol // tn),
        in_specs=[pl.BlockSpec((tm, D_MODEL), lambda i, j: (i, 0)),
                  pl.BlockSpec((1, D_MODEL), lambda i, j: (0, 0)),
                  pl.BlockSpec((D_MODEL, tn), lambda i, j: (0, j))],
        out_specs=pl.BlockSpec((tm, tn), lambda i, j: (i, j)),
        out_shape=jax.ShapeDtypeStruct((n, ncol), F32),
        scratch_shapes=[pltpu.VMEM((tm, D_MODEL), BF16)],
        compiler_params=_cparams(("parallel", "arbitrary")),
    )(x2, norm_w.reshape(1, D_MODEL), w_bf16)


def _s5_kernel(u_ref, h0_ref, wb_ref, wc_ref, a2k_ref, apow_ref, d_ref, wg_ref, bg_ref,
               y_ref, hl_ref, hs_ref, *, tc):
    c = pl.program_id(1)

    @pl.when(c == 0)
    def _():
        hs_ref[...] = h0_ref[...]

    u = u_ref[...]
    x = _dot(u, wb_ref[...])
    xr, xi = x[:, :S5_W], x[:, S5_W:]
    row = _iota((tc, 1), 0)
    k, d = 0, 1
    while d < tc:
        ar = a2k_ref[k:k + 1, :S5_W]
        ai = a2k_ref[k:k + 1, S5_W:]
        sr = pltpu.roll(xr, d, axis=0)
        si = pltpu.roll(xi, d, axis=0)
        m = row >= d
        xr, xi = (xr + jnp.where(m, ar * sr - ai * si, 0.0),
                  xi + jnp.where(m, ar * si + ai * sr, 0.0))
        k, d = k + 1, d * 2
    hr0, hi0 = hs_ref[:, :S5_W], hs_ref[:, S5_W:]
    pr, pi_ = apow_ref[:, :S5_W], apow_ref[:, S5_W:]
    hr = xr + pr * hr0 - pi_ * hi0
    hi = xi + pr * hi0 + pi_ * hr0
    hs_ref[:, :S5_W] = hr[tc - 1:tc]
    hs_ref[:, S5_W:] = hi[tc - 1:tc]
    y = _dot(hr, wc_ref[:S5_W]) + _dot(hi, wc_ref[S5_W:])
    y = jax.nn.gelu(y + d_ref[...] * u)
    y = y * jax.nn.sigmoid(_dot(y, wg_ref[...]) + bg_ref[...])
    y_ref[...] = y.astype(y_ref.dtype)

    @pl.when(c == pl.num_programs(1) - 1)
    def _():
        hl_ref[...] = hs_ref[...]


def _s5_tables(lam_re, lam_im, log_dt, b_c, c_c, tc):
    dt = jnp.exp(log_dt)[:, None]
    mag = jnp.exp(lam_re * dt)
    ab_re, ab_im = mag * jnp.cos(lam_im * dt), mag * jnp.sin(lam_im * dt)
    den = lam_re * lam_re + lam_im * lam_im
    nr = ab_re - 1.0
    cf_re = (nr * lam_re + ab_im * lam_im) / den
    cf_im = (ab_im * lam_re - nr * lam_im) / den
    br, bi = b_c[..., 0], b_c[..., 1]
    bb_re = cf_re[..., None] * br - cf_im[..., None] * bi
    bb_im = cf_re[..., None] * bi + cf_im[..., None] * br
    eye = jnp.eye(S5_GROUPS, dtype=F32)
    bd_in = lambda m: jnp.einsum('gph,gk->ghkp', m, eye).reshape(BRANCH_W, S5_W)
    wb = jnp.concatenate([bd_in(bb_re), bd_in(bb_im)], axis=1)
    cr, ci = c_c[..., 0], c_c[..., 1]
    bd_out = lambda m: jnp.einsum('ghp,gk->gpkh', m, eye).reshape(S5_W, BRANCH_W)
    wc = jnp.concatenate([bd_out(cr), -bd_out(ci)], axis=0)
    pr, pi_ = ab_re.reshape(1, S5_W), ab_im.reshape(1, S5_W)
    lv_r, lv_i = [], []
    tr, ti = pr, pi_
    d = 1
    while d < tc:
        lv_r.append(pr)
        lv_i.append(pi_)
        tr, ti = (jnp.concatenate([tr, tr * pr - ti * pi_], axis=0),
                  jnp.concatenate([ti, tr * pi_ + ti * pr], axis=0))
        pr, pi_ = pr * pr - pi_ * pi_, 2.0 * pr * pi_
        d *= 2
    n_lv = len(lv_r)
    pad = (-n_lv) % 8
    a2k = jnp.concatenate([jnp.concatenate(lv_r, axis=0), jnp.concatenate(lv_i, axis=0)], axis=1)
    a2k = jnp.pad(a2k, ((0, pad), (0, 0)))
    apow = jnp.concatenate([tr, ti], axis=1)
    return wb.astype(BF16), wc.astype(BF16), a2k, apow


def _s5_mixer(p3, h0, lam_re, lam_im, log_dt, b_c, c_c, d_skip, w_glu, b_glu, tc):
    bsz, t, _ = p3.shape
    wb, wc, a2k, apow = _s5_tables(lam_re, lam_im, log_dt, b_c, c_c, tc)
    h0f = jnp.concatenate([h0[..., 0].reshape(bsz, 1, S5_W), h0[..., 1].reshape(bsz, 1, S5_W)], axis=-1)
    full = lambda a: pl.BlockSpec(a.shape, lambda b, c: (0,) * a.ndim)
    d2, bg2, wg = d_skip.reshape(1, BRANCH_W), b_glu.reshape(1, BRANCH_W), w_glu.astype(BF16)
    y, hl = pl.pallas_call(
        functools.partial(_s5_kernel, tc=tc),
        grid=(bsz, t // tc),
        in_specs=[pl.BlockSpec((None, tc, BRANCH_W), lambda b, c: (b, c, COL_S5 // BRANCH_W)),
                  pl.BlockSpec((None, 1, 2 * S5_W), lambda b, c: (b, 0, 0)),
                  full(wb), full(wc), full(a2k), full(apow), full(d2), full(wg), full(bg2)],
        out_specs=[pl.BlockSpec((None, tc, BRANCH_W), lambda b, c: (b, c, 0)),
                   pl.BlockSpec((None, 1, 2 * S5_W), lambda b, c: (b, 0, 0))],
        out_shape=[jax.ShapeDtypeStruct((bsz, t, BRANCH_W), BF16),
                   jax.ShapeDtypeStruct((bsz, 1, 2 * S5_W), F32)],
        scratch_shapes=[pltpu.VMEM((1, 2 * S5_W), F32)],
        compiler_params=_cparams(("parallel", "arbitrary")),
    )(p3, h0f, wb, wc, a2k, apow, d2, wg, bg2)
    h_last = jnp.stack([hl[:, 0, :S5_W].reshape(bsz, S5_GROUPS, S5_STATE),
                        hl[:, 0, S5_W:].reshape(bsz, S5_GROUPS, S5_STATE)], axis=-1)
    return y, h_last


def _rwkv_kernel(z_ref, sh0_ref, s0_ref, mu_ref, w0_ref, w2_ref, a0_ref, a2_ref, g2_ref,
                 kk_ref, ka_ref, rk_ref, lnw_ref, lnb_ref,
                 y_ref, sl_ref, st_ref, zp_ref, *, l):
    c = pl.program_id(1)

    @pl.when(c == 0)
    def _():
        st_ref[...] = s0_ref[...]
        zp_ref[...] = sh0_ref[...]

    z = z_ref[...]
    row = _iota((l, 1), 0)
    prev = jnp.where(row == 0, zp_ref[...], pltpu.roll(z, 1, axis=0))
    zp_ref[...] = z[l - 1:l]
    zm = z + (prev - z) * mu_ref[...]
    r, k, v = zm[:, 0:256], zm[:, 256:512], zm[:, 512:768]
    lo = zm[:, 768:896]
    g_lo = zm[:, 896:1024]
    w_log = -_softplus(-(w0_ref[...] + _dot(jnp.tanh(lo), w2_ref[...]))) - 0.5
    lw = -jnp.exp(w_log)
    a = jax.nn.sigmoid(a0_ref[...] + _dot(lo, a2_ref[...]))
    g = _dot(jax.nn.sigmoid(g_lo), g2_ref[...])
    ones_h = _head_ones()
    kk = k * kk_ref[...]
    kk = kk * lax.rsqrt(_dot3(kk * kk, ones_h) + NORM_EPS)
    k = k * (1.0 + (a - 1.0) * ka_ref[...])
    kka = kk * a

    cum = _cumsum_rows(lw, l)
    p_incl = jnp.exp(cum)
    p_inv = jnp.exp(-cum)
    hm = _head_mask(l)
    kt_e = _expand(kk * jnp.exp(cum - lw), hm)
    rt_e = _expand(r * p_incl, hm)
    kh = k * p_inv
    ah = kka * p_inv
    kh_e = _expand(kh, hm)
    ah_e = _expand(ah, hm)
    v_e = _expand(v, hm)
    strict, incl = _tri_masks(l)
    a_aa = jnp.where(strict, _dot_nt(kt_e, ah_e), 0.0)
    a_ak = jnp.where(strict, _dot_nt(kt_e, kh_e), 0.0)
    b_ra = jnp.where(incl, _dot_nt(rt_e, ah_e), 0.0)
    b_rk = jnp.where(incl, _dot_nt(rt_e, kh_e), 0.0)
    t_inv = _unit_lower_inverse(a_aa, l)
    st = st_ref[...]
    u = _dot(t_inv, _dot_nt(kt_e, st) + _dot(a_ak, v_e))
    y_e = _dot_nt(rt_e, st) + _dot(b_rk, v_e) - _dot(b_ra, u)
    p_last = p_incl[l - 1:l]
    to_end = jnp.exp(cum[l - 1:l] - cum)
    st_ref[...] = (st * p_last + _dot(v_e.T, _expand(k * to_end, hm)) - _dot(u.T, _expand(kka * to_end, hm)))
    y = _fold(y_e, l)

    inv_w = 1.0 / HEAD_W
    mean = _dot3(y, ones_h) * inv_w
    yc = y - mean
    var = _dot3(yc * yc, ones_h) * inv_w
    y = yc * lax.rsqrt(var + RW_EPS) * lnw_ref[...] + lnb_ref[...]
    bonus = _dot3(r * k * rk_ref[...], ones_h) * v
    y_ref[...] = ((y + bonus) * g).astype(y_ref.dtype)

    @pl.when(c == pl.num_programs(1) - 1)
    def _():
        sl_ref[...] = st_ref[...]


def _block_diag_heads(s):
    bsz = s.shape[0]
    eye = jnp.eye(HEADS, dtype=s.dtype)
    return jnp.einsum('bhij,hg->bhigj', s, eye).reshape(bsz, BRANCH_W, BRANCH_W)


def _diag_blocks(s):
    bsz = s.shape[0]
    s5 = s.reshape(bsz, HEADS, HEAD_W, HEADS, HEAD_W)
    return jnp.stack([s5[:, h, :, h, :] for h in range(HEADS)], axis=1)


def _rwkv_mixer(p3, shift0, s0, mu, w0, w2, a0, a2, g2, k_k, k_a, r_k, ln_w, ln_b, l):
    bsz, t, _ = p3.shape
    row = lambda a: a.reshape(1, -1)
    w2p = jnp.concatenate([w2, jnp.zeros_like(w2)], axis=0).astype(BF16)
    a2p = jnp.concatenate([jnp.zeros_like(a2), a2], axis=0).astype(BF16)
    args = (shift0.reshape(bsz, 1, RW_COLS), _block_diag_heads(s0), row(mu), row(w0), w2p, row(a0), a2p,
            g2.astype(BF16), row(k_k), row(k_a), row(r_k), row(ln_w), row(ln_b))
    full = lambda a: pl.BlockSpec(a.shape, lambda b, c: (0,) * a.ndim)
    y, sl = pl.pallas_call(
        functools.partial(_rwkv_kernel, l=l),
        grid=(bsz, t // l),
        in_specs=[pl.BlockSpec((None, l, RW_COLS), lambda b, c: (b, c, COL_RW // RW_COLS)),
                  pl.BlockSpec((None, 1, RW_COLS), lambda b, c: (b, 0, 0)),
                  pl.BlockSpec((None, BRANCH_W, BRANCH_W), lambda b, c: (b, 0, 0))]
                 + [full(a) for a in args[2:]],
        out_specs=[pl.BlockSpec((None, l, BRANCH_W), lambda b, c: (b, c, 0)),
                   pl.BlockSpec((None, BRANCH_W, BRANCH_W), lambda b, c: (b, 0, 0))],
        out_shape=[jax.ShapeDtypeStruct((bsz, t, BRANCH_W), BF16),
                   jax.ShapeDtypeStruct((bsz, BRANCH_W, BRANCH_W), F32)],
        scratch_shapes=[pltpu.VMEM((BRANCH_W, BRANCH_W), F32), pltpu.VMEM((1, RW_COLS), F32)],
        compiler_params=_cparams(("parallel", "arbitrary")),
    )(p3, *args)
    return y, _diag_blocks(sl)


def _sgu_kernel(z_ref, lnw_ref, lnb_ref, wm_ref, bias_ref, o_ref, v_ref, *, l):
    zg = jax.nn.gelu(z_ref[...])
    u, v = zg[:, :BRANCH_W], zg[:, BRANCH_W:]
    mean = jnp.mean(v, axis=-1, keepdims=True)
    vc = v - mean
    var = jnp.mean(vc * vc, axis=-1, keepdims=True)
    v = vc * lax.rsqrt(var + NORM_EPS) * lnw_ref[...] + lnb_ref[...]
    grp = _iota((l, BRANCH_W), 1) // HEAD_W
    mixed = bias_ref[...]
    for gi in range(HEADS):
        mixed = mixed + _dot(wm_ref[gi], jnp.where(grp == gi, v, 0.0))
    o_ref[...] = (u * mixed).astype(o_ref.dtype)
    v_ref[...] = v


def _sgu_mixer(p3, ln_w, ln_b, w_s, b_s):
    bsz, t, _ = p3.shape
    l = min(SG_CHUNK, t)
    tril = jnp.tril(jnp.ones((l, l), F32))
    wm = (w_s[:, :l, :l] * tril).astype(BF16)
    bias = jnp.repeat(jnp.transpose(b_s[:, :l]), HEAD_W, axis=1)
    row = lambda a: a.reshape(1, -1)
    full = lambda a: pl.BlockSpec(a.shape, lambda b, c: (0,) * a.ndim)
    args = (row(ln_w), row(ln_b), wm, bias)
    return pl.pallas_call(
        functools.partial(_sgu_kernel, l=l),
        grid=(bsz, t // l),
        in_specs=[pl.BlockSpec((None, l, 2 * BRANCH_W), lambda b, c: (b, c, COL_SG // (2 * BRANCH_W)))]
                 + [full(a) for a in args],
        out_specs=[pl.BlockSpec((None, l, BRANCH_W), lambda b, c: (b, c, 0)),
                   pl.BlockSpec((None, l, BRANCH_W), lambda b, c: (b, c, 0))],
        out_shape=[jax.ShapeDtypeStruct((bsz, t, BRANCH_W), BF16),
                   jax.ShapeDtypeStruct((bsz, t, BRANCH_W), F32)],
        compiler_params=_cparams(("parallel", "parallel")),
    )(p3, *args)


def _gdn_kernel(z_ref, ab_ref, cv0_ref, s0_ref, cw_ref, alog_ref, dtb_ref, nw_ref,
                y_ref, sl_ref, st_ref, cv_ref, *, l):
    c = pl.program_id(1)

    @pl.when(c == 0)
    def _():
        st_ref[...] = s0_ref[...]
        cv_ref[...] = cv0_ref[...]

    z = z_ref[...]
    qkv = z[:, :GD_QKV]
    gate = z[:, GD_QKV:]
    carry = cv_ref[...]
    cv_ref[...] = qkv[l - 8:l]
    row8 = _iota((8, 1), 0)
    conv = qkv * cw_ref[GD_CONV - 1:GD_CONV]
    for j in range(1, GD_CONV):
        sh = pltpu.roll(qkv, j, axis=0)
        top = jnp.where(row8 < j, pltpu.roll(carry, j, axis=0), sh[:8])
        sh = jnp.concatenate([top, sh[8:]], axis=0) if l > 8 else top
        conv = conv + sh * cw_ref[GD_CONV - 1 - j:GD_CONV - j]
    conv = _silu(conv)
    q, k, v = conv[:, :256], conv[:, 256:512], conv[:, 512:768]
    ones_h = _head_ones()
    q = q * lax.rsqrt(_dot3(q * q, ones_h) + NORM_EPS) * (HEAD_W ** -0.5)
    k = k * lax.rsqrt(_dot3(k * k, ones_h) + NORM_EPS)
    ab = ab_ref[...]
    lane_h = _iota((l, BRANCH_W), 1) // HEAD_W
    a_in = jnp.zeros((l, BRANCH_W), F32)
    b_in = jnp.zeros((l, BRANCH_W), F32)
    for h in range(HEADS):
        a_in = jnp.where(lane_h == h, ab[:, h:h + 1], a_in)
        b_in = jnp.where(lane_h == h, ab[:, HEADS + h:HEADS + h + 1], b_in)
    beta = jax.nn.sigmoid(b_in)
    g = -jnp.exp(alog_ref[...]) * _softplus(a_in + dtb_ref[...])

    gc = _cumsum_rows(g, l)
    hm = _head_mask(l)
    kb = k * beta
    k_e = _expand(k, hm)
    kb_e = _expand(kb, hm)
    q_e = _expand(q, hm)
    vb_e = _expand(v * beta, hm)
    gc_e = _expand(gc, hm)
    n = HEADS * l
    g_col = jnp.sum(gc_e, axis=1, keepdims=True) * (1.0 / HEAD_W)
    g_row = jnp.broadcast_to(g_col, (n, LANES)).T[0:1]
    strict, incl = _tri_masks(l)
    decay = jnp.where(incl, jnp.exp(jnp.where(incl, g_col - g_row, 0.0)), 0.0)
    lm = jnp.where(strict, _dot_nt(kb_e, k_e) * decay, 0.0)
    t_inv = _unit_lower_inverse(lm, l)
    eg_e = jnp.exp(gc_e)
    uc = _dot(t_inv, vb_e)
    wc = _dot(t_inv, kb_e * eg_e)
    qk = _dot_nt(q_e, k_e) * decay
    st = st_ref[...]
    v_new = uc - _dot(wc, st)
    o_e = _dot(q_e * eg_e, st) + _dot(qk, v_new)
    g_last = gc[l - 1:l]
    k_dec = k_e * jnp.exp(jnp.where(hm, g_last - gc_e, 0.0))
    st_ref[...] = st * jnp.exp(g_last) + _dot(k_dec.T, v_new)
    o = _fold(o_e, l)
    ms = _dot3(o * o, ones_h) * (1.0 / HEAD_W)
    o = o * lax.rsqrt(ms + NORM_EPS) * nw_ref[...] * _silu(gate)
    y_ref[...] = o.astype(y_ref.dtype)

    @pl.when(c == pl.num_programs(1) - 1)
    def _():
        sl_ref[...] = st_ref[...]


def _gdn_mixer(p3, conv0, s0, conv_w, a_log, dt_bias, norm_w, l):
    bsz, t, _ = p3.shape
    cv0 = jnp.pad(conv0, ((0, 0), (8 - (GD_CONV - 1), 0), (0, 0)))
    cw = jnp.pad(conv_w, ((0, 8 - GD_CONV), (0, 0)))
    per_head = lambda a: jnp.repeat(a, HEAD_W).reshape(1, BRANCH_W)
    args = (cv0, _block_diag_heads(s0), cw, per_head(a_log), per_head(dt_bias),
            jnp.tile(norm_w, HEADS).reshape(1, BRANCH_W))
    full = lambda a: pl.BlockSpec(a.shape, lambda b, c: (0,) * a.ndim)
    y, sl = pl.pallas_call(
        functools.partial(_gdn_kernel, l=l),
        grid=(bsz, t // l),
        in_specs=[pl.BlockSpec((None, l, 1024), lambda b, c: (b, c, COL_GD // 1024)),
                  pl.BlockSpec((None, l, LANES), lambda b, c: (b, c, COL_AB // LANES)),
                  pl.BlockSpec((None, 8, GD_QKV), lambda b, c: (b, 0, 0)),
                  pl.BlockSpec((None, BRANCH_W, BRANCH_W), lambda b, c: (b, 0, 0))]
                 + [full(a) for a in args[2:]],
        out_specs=[pl.BlockSpec((None, l, BRANCH_W), lambda b, c: (b, c, 0)),
                   pl.BlockSpec((None, BRANCH_W, BRANCH_W), lambda b, c: (b, 0, 0))],
        out_shape=[jax.ShapeDtypeStruct((bsz, t, BRANCH_W), BF16),
                   jax.ShapeDtypeStruct((bsz, BRANCH_W, BRANCH_W), F32)],
        scratch_shapes=[pltpu.VMEM((BRANCH_W, BRANCH_W), F32), pltpu.VMEM((8, GD_QKV), F32)],
        compiler_params=_cparams(("parallel", "arbitrary")),
    )(p3, p3, *args)
    return y, _diag_blocks(sl)


def _merge_kernel(ya_ref, yb_ref, yc_ref, yd_ref, g_ref, x_ref, wbr_ref, wout_ref, o_ref):
    m = None
    for b, y_ref in enumerate((ya_ref, yb_ref, yc_ref, yd_ref)):
        br = jnp.dot(y_ref[...], wbr_ref[b], preferred_element_type=F32)
        term = jax.nn.sigmoid(g_ref[:, b * D_MODEL:(b + 1) * D_MODEL]) * br
        m = term if m is None else m + term
    o_ref[...] = x_ref[...] + jnp.dot(m.astype(BF16), wout_ref[...], preferred_element_type=F32)


def _merge(ys, p2, x2, w_branch, w_out, tm):
    n = x2.shape[0]
    yspec = pl.BlockSpec((tm, BRANCH_W), lambda i: (i, 0))
    return pl.pallas_call(
        _merge_kernel,
        grid=(n // tm,),
        in_specs=[yspec, yspec, yspec, yspec,
                  pl.BlockSpec((tm, 4 * D_MODEL), lambda i: (i, 0)),
                  pl.BlockSpec((tm, D_MODEL), lambda i: (i, 0)),
                  pl.BlockSpec((4, BRANCH_W, D_MODEL), lambda i: (0, 0, 0)),
                  pl.BlockSpec((D_MODEL, D_MODEL), lambda i: (0, 0))],
        out_specs=pl.BlockSpec((tm, D_MODEL), lambda i: (i, 0)),
        out_shape=jax.ShapeDtypeStruct((n, D_MODEL), F32),
        compiler_params=_cparams(("parallel",)),
    )(*[y.reshape(n, BRANCH_W) for y in ys], p2, x2, w_branch.astype(BF16), w_out.astype(BF16))


def _ffn_kernel(x_ref, nw_ref, w1_ref, w3_ref, w2_ref, o_ref, h_ref):
    j = pl.program_id(1)

    @pl.when(j == 0)
    def _():
        x = x_ref[...]
        ms = jnp.mean(x * x, axis=-1, keepdims=True)
        h_ref[...] = (x * lax.rsqrt(ms + NORM_EPS) * nw_ref[...]).astype(BF16)
        o_ref[...] = x

    h = h_ref[...]
    a = _silu(jnp.dot(h, w1_ref[...], preferred_element_type=F32)) * jnp.dot(h, w3_ref[...], preferred_element_type=F32)
    o_ref[...] += jnp.dot(a.astype(BF16), w2_ref[...], preferred_element_type=F32)


def _ffn(x2, norm_w, w1, w3, w2, tm, tf):
    n = x2.shape[0]
    dff = w1.shape[1]
    return pl.pallas_call(
        _ffn_kernel,
        grid=(n // tm, dff // tf),
        in_specs=[pl.BlockSpec((tm, D_MODEL), lambda i, j: (i, 0)),
                  pl.BlockSpec((1, D_MODEL), lambda i, j: (0, 0)),
                  pl.BlockSpec((D_MODEL, tf), lambda i, j: (0, j)),
                  pl.BlockSpec((D_MODEL, tf), lambda i, j: (0, j)),
                  pl.BlockSpec((tf, D_MODEL), lambda i, j: (j, 0))],
        out_specs=pl.BlockSpec((tm, D_MODEL), lambda i, j: (i, 0)),
        out_shape=jax.ShapeDtypeStruct((n, D_MODEL), F32),
        scratch_shapes=[pltpu.VMEM((tm, D_MODEL), BF16)],
        compiler_params=_cparams(("parallel", "arbitrary")),
    )(x2, norm_w.reshape(1, D_MODEL), w1.astype(BF16), w3.astype(BF16), w2.astype(BF16))


MOE_SUB = 256


def _router_kernel(x_ref, nw_ref, wr_ref, br_ref, h_ref, rcol_ref, gcol_ref, rrow_ref, cnt_ref, *, tm):
    x = x_ref[...]
    ms = jnp.mean(x * x, axis=-1, keepdims=True)
    h = x * lax.rsqrt(ms + NORM_EPS) * nw_ref[...]
    h_ref[...] = h.astype(BF16)
    lane = _iota((tm, LANES), 1)
    logits = jnp.where(lane < N_EXPERTS, _dot_hp(h, wr_ref[...]) + br_ref[...], -jnp.inf)
    m1 = jnp.max(logits, axis=1, keepdims=True)
    i1 = jnp.min(jnp.where(logits == m1, lane, LANES), axis=1, keepdims=True)
    rest = jnp.where(lane == i1, -jnp.inf, logits)
    m2 = jnp.max(rest, axis=1, keepdims=True)
    i2 = jnp.min(jnp.where(rest == m2, lane, LANES), axis=1, keepdims=True)
    e2 = jnp.exp(m2 - m1)
    g1 = 1.0 / (1.0 + e2)
    g2 = e2 / (1.0 + e2)
    sel = (lane == i1) | (lane == i2)
    self32 = sel.astype(F32)
    gcol_ref[...] = jnp.where(lane == i1, g1, 0.0) + jnp.where(lane == i2, g2, 0.0)
    tri = (_iota((tm, tm), 1) < _iota((tm, tm), 0)).astype(BF16)
    rank = jnp.dot(tri, self32.astype(BF16), preferred_element_type=F32)
    r = jnp.where(sel, rank, -1.0)
    rcol_ref[...] = r
    rrow_ref[...] = r.T[:N_EXPERTS]
    cnt_ref[...] = jnp.broadcast_to(jnp.sum(self32, axis=0, keepdims=True), (8, LANES))


def _router(x2, norm_w, w_router, b_router, tm):
    n = x2.shape[0]
    nt = n // tm
    wr = jnp.pad(w_router, ((0, 0), (0, LANES - N_EXPERTS)))
    br = jnp.pad(b_router, (0, LANES - N_EXPERTS)).reshape(1, LANES)
    return pl.pallas_call(
        functools.partial(_router_kernel, tm=tm),
        grid=(nt,),
        in_specs=[pl.BlockSpec((tm, D_MODEL), lambda i: (i, 0)),
                  pl.BlockSpec((1, D_MODEL), lambda i: (0, 0)),
                  pl.BlockSpec((D_MODEL, LANES), lambda i: (0, 0)),
                  pl.BlockSpec((1, LANES), lambda i: (0, 0))],
        out_specs=[pl.BlockSpec((tm, D_MODEL), lambda i: (i, 0)),
                   pl.BlockSpec((tm, LANES), lambda i: (i, 0)),
                   pl.BlockSpec((tm, LANES), lambda i: (i, 0)),
                   pl.BlockSpec((None, N_EXPERTS, tm), lambda i: (i, 0, 0)),
                   pl.BlockSpec((None, 8, LANES), lambda i: (i, 0, 0))],
        out_shape=[jax.ShapeDtypeStruct((n, D_MODEL), BF16),
                   jax.ShapeDtypeStruct((n, LANES), F32),
                   jax.ShapeDtypeStruct((n, LANES), F32),
                   jax.ShapeDtypeStruct((nt, N_EXPERTS, tm), F32),
                   jax.ShapeDtypeStruct((nt, 8, LANES), F32)],
        compiler_params=_cparams(("parallel",)),
    )(x2, norm_w.reshape(1, D_MODEL), wr, br)


def _moe_kernel(cnt_ref, h_ref, rrow_ref, rcol_ref, gcol_ref, w1_ref, w3_ref, w2_ref, o_ref,
                xs_ref, acc_ref, *, tm):
    i, e, c = pl.program_id(0), pl.program_id(1), pl.program_id(2)
    n_sub = (cnt_ref[i * N_EXPERTS + e] + (MOE_SUB - 1)) // MOE_SUB

    @pl.when((e == 0) & (c == 0))
    def _():
        o_ref[...] = jnp.zeros_like(o_ref)

    @pl.when(c == 0)
    def _():
        rrow = rrow_ref[pl.ds(e, 1), :]
        slot = _iota((MOE_SUB, 1), 0).astype(F32)

        def gather(s, carry):
            base = pl.multiple_of(s * MOE_SUB, MOE_SUB)
            pick = (rrow == slot + (s * MOE_SUB).astype(F32)).astype(BF16)
            xs_ref[pl.ds(base, MOE_SUB), :] = jnp.dot(pick, h_ref[...], preferred_element_type=F32).astype(BF16)
            acc_ref[pl.ds(base, MOE_SUB), :] = jnp.zeros((MOE_SUB, D_MODEL), F32)
            return carry

        lax.fori_loop(0, n_sub, gather, 0)

    def expert(s, carry):
        base = pl.multiple_of(s * MOE_SUB, MOE_SUB)
        xb = xs_ref[pl.ds(base, MOE_SUB), :]
        a = (_silu(jnp.dot(xb, w1_ref[...], preferred_element_type=F32))
             * jnp.dot(xb, w3_ref[...], preferred_element_type=F32))
        acc_ref[pl.ds(base, MOE_SUB), :] += jnp.dot(a.astype(BF16), w2_ref[...], preferred_element_type=F32)
        return carry

    lax.fori_loop(0, n_sub, expert, 0)

    @pl.when(c == pl.num_programs(2) - 1)
    def _():
        lane = _iota((tm, LANES), 1)
        rcol = jnp.sum(jnp.where(lane == e, rcol_ref[...], 0.0), axis=1, keepdims=True)
        gate = jnp.sum(jnp.where(lane == e, gcol_ref[...], 0.0), axis=1, keepdims=True)
        slot = _iota((1, MOE_SUB), 1).astype(F32)

        def scatter(s, carry):
            base = pl.multiple_of(s * MOE_SUB, MOE_SUB)
            place = (rcol == slot + (s * MOE_SUB).astype(F32)).astype(BF16)
            a = acc_ref[pl.ds(base, MOE_SUB), :]
            hi = a.astype(BF16)
            lo = (a - hi.astype(F32)).astype(BF16)
            back = (jnp.dot(place, hi, preferred_element_type=F32)
                    + jnp.dot(place, lo, preferred_element_type=F32))
            o_ref[...] += gate * back
            return carry

        lax.fori_loop(0, n_sub, scatter, 0)


def _moe(h2, rrow, rcol, gcol, counts, w1, w3, w2, tm, tf):
    n = h2.shape[0]
    nt = n // tm
    dff = w1.shape[2]
    grid_spec = pltpu.PrefetchScalarGridSpec(
        num_scalar_prefetch=1,
        grid=(nt, N_EXPERTS, dff // tf),
        in_specs=[pl.BlockSpec((tm, D_MODEL), lambda i, e, c, cnt: (i, 0)),
                  pl.BlockSpec((None, N_EXPERTS, tm), lambda i, e, c, cnt: (i, 0, 0)),
                  pl.BlockSpec((tm, LANES), lambda i, e, c, cnt: (i, 0)),
                  pl.BlockSpec((tm, LANES), lambda i, e, c, cnt: (i, 0)),
                  pl.BlockSpec((None, D_MODEL, tf), lambda i, e, c, cnt: (e, 0, c)),
                  pl.BlockSpec((None, D_MODEL, tf), lambda i, e, c, cnt: (e, 0, c)),
                  pl.BlockSpec((None, tf, D_MODEL), lambda i, e, c, cnt: (e, c, 0))],
        out_specs=pl.BlockSpec((tm, D_MODEL), lambda i, e, c, cnt: (i, 0)),
        scratch_shapes=[pltpu.VMEM((tm, D_MODEL), BF16), pltpu.VMEM((tm, D_MODEL), F32)],
    )
    return pl.pallas_call(
        functools.partial(_moe_kernel, tm=tm),
        grid_spec=grid_spec,
        out_shape=jax.ShapeDtypeStruct((n, D_MODEL), F32),
        compiler_params=_cparams(("parallel", "arbitrary", "arbitrary")),
    )(counts, h2, rrow, rcol, gcol, w1, w3, w2)


def _moe_ffn(x2, norm_w, w_router, b_router, w1, w3, w2):
    n = x2.shape[0]
    tm = min(1024, n)
    h2, rcol, gcol, rrow, cnt = _router(x2, norm_w, w_router, b_router, tm)
    counts = cnt[:, 0, :N_EXPERTS].astype(jnp.int32).reshape(-1)
    return _moe(h2, rrow, rcol, gcol, counts, w1.astype(BF16), w3.astype(BF16), w2.astype(BF16), tm, 512)


def _final_kernel(x_ref, m_ref, nw_ref, o_ref):
    x = x_ref[...] + m_ref[...]
    ms = jnp.mean(x * x, axis=-1, keepdims=True)
    o_ref[...] = x * lax.rsqrt(ms + NORM_EPS) * nw_ref[...]


def _final(x2, m2, norm_w, tm):
    n = x2.shape[0]
    spec = pl.BlockSpec((tm, D_MODEL), lambda i: (i, 0))
    return pl.pallas_call(
        _final_kernel,
        grid=(n // tm,),
        in_specs=[spec, spec, pl.BlockSpec((1, D_MODEL), lambda i: (0, 0))],
        out_specs=spec,
        out_shape=jax.ShapeDtypeStruct((n, D_MODEL), F32),
        compiler_params=_cparams(("parallel",)),
    )(x2, m2, norm_w.reshape(1, D_MODEL))


def _permute_w_in(w):
    s5, rw, sg = w[:, 0:256], w[:, 256:1280], w[:, 1280:1792]
    gd, ab, gates = w[:, 1792:2816], w[:, 2816:2824], w[:, 2824:6920]
    pad = jnp.zeros((D_MODEL, LANES - 8), w.dtype)
    return jnp.concatenate([gates, rw, gd, sg, s5, ab, pad], axis=1).astype(BF16)


def _run_trunk(x, s5_h, rw_s, rw_shift, gd_s, gd_conv, p, w_in_perm):
    bsz, t, _ = x.shape
    n = bsz * t
    tm = min(1024, n)
    l = min(64, t)
    tc_s5 = min(128, t)
    new = ([], [], [], [], [], [])
    x2 = x.reshape(n, D_MODEL)
    moe_out = None
    for layer in range(2):
        g = lambda name: p[name][layer]
        proj = _norm_proj(x2, g('norm1_w'), w_in_perm[layer], tm, 640)
        p3 = proj.reshape(bsz, t, PROJ_COLS)
        y_a, s5_new = _s5_mixer(p3, s5_h[layer], g('s5_lam_re'), g('s5_lam_im'), g('s5_log_dt'), g('s5_b'),
                                g('s5_c'), g('s5_d'), g('s5_w_glu'), g('s5_b_glu'), tc_s5)
        y_b, rw_new = _rwkv_mixer(p3, rw_shift[layer], rw_s[layer], g('rw_mu'), g('rw_w0'), g('rw_w2'),
                                  g('rw_a0'), g('rw_a2'), g('rw_g2'), g('rw_k_k'), g('rw_k_a'), g('rw_r_k'),
                                  g('rw_ln_w'), g('rw_ln_b'), l)
        y_c, sg_v = _sgu_mixer(p3, g('sg_ln_w'), g('sg_ln_b'), g('sg_w_s'), g('sg_b_s'))
        y_d, gd_new = _gdn_mixer(p3, gd_conv[layer], gd_s[layer], g('gd_conv_w'), g('gd_a_log'),
                                 g('gd_dt_bias'), g('gd_norm_w'), l)
        shift_new = p3[:, t - 1, COL_RW:COL_RW + RW_COLS]
        conv_new = p3[:, t - (GD_CONV - 1):, COL_GD:COL_GD + GD_QKV]
        x2 = _merge((y_a, y_b, y_c, y_d), proj, x2, g('w_branch'), g('w_out'), min(256, n))
        j = layer // 2
        if layer % 2 == 0:
            x2 = _ffn(x2, g('norm2_w'), p['ffn_w1'][j], p['ffn_w3'][j], p['ffn_w2'][j], min(512, n), 1408)
        else:
            moe_out = _moe_ffn(x2, g('norm2_w'), p['moe_router'][j], p['moe_router_b'][j],
                               p['moe_w1'][j], p['moe_w3'][j], p['moe_w2'][j])
        for lst, s in zip(new, (s5_new, rw_new, shift_new, gd_new, conv_new, sg_v)):
            lst.append(s)
    y = _final(x2, moe_out, p['final_norm_w'], tm).reshape(bsz, t, D_MODEL)
    return y, [jnp.stack(lst) for lst in new]


def kernel(x_prompt, x_sample, state_s5, state_rwkv, state_rwkv_shift, state_gdn, state_gdn_conv, norm1_w, w_in, s5_lam_re, s5_lam_im, s5_log_dt, s5_b, s5_c, s5_d, s5_w_glu, s5_b_glu, rw_mu, rw_w0, rw_w2, rw_a0, rw_a2, rw_g2, rw_k_k, rw_k_a, rw_r_k, rw_ln_w, rw_ln_b, sg_ln_w, sg_ln_b, sg_w_s, sg_b_s, gd_conv_w, gd_a_log, gd_dt_bias, gd_norm_w, w_branch, w_out, norm2_w, ffn_w1, ffn_w3, ffn_w2, moe_router, moe_router_b, moe_w1, moe_w3, moe_w2, final_norm_w):
    p = {
        'norm1_w': norm1_w, 's5_lam_re': s5_lam_re, 's5_lam_im': s5_lam_im, 's5_log_dt': s5_log_dt,
        's5_b': s5_b, 's5_c': s5_c, 's5_d': s5_d, 's5_w_glu': s5_w_glu, 's5_b_glu': s5_b_glu,
        'rw_mu': rw_mu, 'rw_w0': rw_w0, 'rw_w2': rw_w2, 'rw_a0': rw_a0, 'rw_a2': rw_a2, 'rw_g2': rw_g2,
        'rw_k_k': rw_k_k, 'rw_k_a': rw_k_a, 'rw_r_k': rw_r_k, 'rw_ln_w': rw_ln_w, 'rw_ln_b': rw_ln_b,
        'sg_ln_w': sg_ln_w, 'sg_ln_b': sg_ln_b, 'sg_w_s': sg_w_s, 'sg_b_s': sg_b_s,
        'gd_conv_w': gd_conv_w, 'gd_a_log': gd_a_log, 'gd_dt_bias': gd_dt_bias, 'gd_norm_w': gd_norm_w,
        'w_branch': w_branch, 'w_out': w_out, 'norm2_w': norm2_w,
        'ffn_w1': ffn_w1, 'ffn_w3': ffn_w3, 'ffn_w2': ffn_w2,
        'moe_router': moe_router, 'moe_router_b': moe_router_b, 'moe_w1': moe_w1, 'moe_w3': moe_w3, 'moe_w2': moe_w2,
        'final_norm_w': final_norm_w,
    }
    w_in_perm = [_permute_w_in(w_in[layer]) for layer in range(2)]
    bp, dt = x_prompt.shape[0], x_prompt.dtype
    depth = w_in.shape[0]
    y_prompt, (s5_p, rw_p, rwsh_p, gd_p, gdc_p, _) = _run_trunk(
        x_prompt,
        jnp.zeros((depth, bp, S5_GROUPS, S5_STATE, 2), dt),
        jnp.zeros((depth, bp, HEADS, HEAD_W, HEAD_W), dt),
        jnp.zeros((depth, bp, RW_COLS), dt),
        jnp.zeros((depth, bp, HEADS, HEAD_W, HEAD_W), dt),
        jnp.zeros((depth, bp, GD_CONV - 1, GD_QKV), dt),
        p, w_in_perm)
    y_sample, (s5_s, rw_s, rwsh_s, gd_s, gdc_s, sgv_s) = _run_trunk(
        x_sample, state_s5, state_rwkv, state_rwkv_shift, state_gdn, state_gdn_conv, p, w_in_perm)
    return (y_prompt, y_sample, s5_p, rw_p, rwsh_p, gd_p, gdc_p, s5_s, rw_s, rwsh_s, gd_s, gdc_s, sgv_s)
```

```python
import functools

import jax
import jax.numpy as jnp
from jax import lax
from jax.experimental import pallas as pl
from jax.experimental.pallas import tpu as pltpu

F32 = jnp.float32
BF16 = jnp.bfloat16

D_MODEL = 1024
BRANCH_W = 256
HEADS = 4
HEAD_W = 64
S5_GROUPS = 16
S5_GROUP = 16
S5_STATE = 64
S5_W = S5_GROUPS * S5_STATE
SG_CHUNK = 128
GD_CONV = 4
GD_QKV = 3 * BRANCH_W
RW_COLS = 1024
RW_EPS = 64e-5
NORM_EPS = 1e-6
N_EXPERTS = 8
LANES = 128

COL_GATE = 0
COL_RW = 4096
COL_GD = 5120
COL_SG = 6144
COL_S5 = 6656
COL_AB = 6912
PROJ_COLS = 7168

VMEM_LIMIT = 48 * 1024 * 1024


def _cparams(sem):
    return pltpu.CompilerParams(dimension_semantics=sem, vmem_limit_bytes=VMEM_LIMIT)


def _dot(a, b):
    return jnp.dot(a.astype(BF16), b.astype(BF16), preferred_element_type=F32)


def _dot_nt(a, b):
    return lax.dot_general(a.astype(BF16), b.astype(BF16), (((1,), (1,)), ((), ())),
                           preferred_element_type=F32)


def _split3(a):
    hi = a.astype(BF16)
    r1 = a - hi.astype(F32)
    mid = r1.astype(BF16)
    lo = (r1 - mid.astype(F32)).astype(BF16)
    return hi, mid, lo


def _dot3_left(b_exact, a):
    hi, mid, lo = _split3(a)
    b = b_exact.astype(BF16)
    return (jnp.dot(b, hi, preferred_element_type=F32) + jnp.dot(b, mid, preferred_element_type=F32)
            + jnp.dot(b, lo, preferred_element_type=F32))


def _dot_hp(a, b):
    a0, a1, a2 = _split3(a)
    b0, b1, b2 = _split3(b)
    d = lambda x, y: jnp.dot(x, y, preferred_element_type=F32)
    return d(a0, b0) + (d(a0, b1) + d(a1, b0)) + (d(a0, b2) + d(a1, b1) + d(a2, b0))


def _iota(shape, axis):
    return lax.broadcasted_iota(jnp.int32, shape, axis)


def _head_ones():
    r = _iota((BRANCH_W, BRANCH_W), 0) // HEAD_W
    c = _iota((BRANCH_W, BRANCH_W), 1) // HEAD_W
    return (r == c).astype(BF16)


def _head_mask(l):
    r = _iota((HEADS * l, BRANCH_W), 0) // l
    c = _iota((HEADS * l, BRANCH_W), 1) // HEAD_W
    return r == c


def _expand(x, mask):
    return jnp.where(mask, jnp.concatenate([x] * HEADS, axis=0), 0.0)


def _tri_masks(l):
    i = _iota((l, HEADS * l), 0)
    j = _iota((l, HEADS * l), 1) % l
    return j < i, j <= i


def _same_head(l):
    n = HEADS * l
    return (_iota((n, n), 0) // l) == (_iota((n, n), 1) // l)


def _expand_sq(x, same):
    return jnp.where(same, jnp.concatenate([x] * HEADS, axis=0), 0.0)


def _unit_lower_inverse(a_strict, l):
    i = _iota((l, HEADS * l), 0)
    j = _iota((l, HEADS * l), 1) % l
    eye = (i == j).astype(F32)
    same = _same_head(l)
    p = [-a for a in a_strict]
    t = [eye + x for x in p]
    k = 2
    while k < l:
        p = [_dot(x, _expand_sq(x, same)) for x in p]
        t = [y + _dot(y, _expand_sq(x, same)) for y, x in zip(t, p)]
        k *= 2
    return t


def _dot2(a, b_exact):
    hi = a.astype(BF16)
    lo = (a - hi.astype(F32)).astype(BF16)
    b = b_exact.astype(BF16)
    return jnp.dot(hi, b, preferred_element_type=F32) + jnp.dot(lo, b, preferred_element_type=F32)


def _cumsum_rows(x, l, nb=1):
    n = nb * l
    i, j = _iota((n, n), 0), _iota((n, n), 1)
    tri = ((j <= i) & ((i // l) == (j // l))).astype(BF16)
    return _dot3_left(tri, x)


def _softplus(x):
    return jnp.maximum(x, 0.0) + jnp.log(1.0 + jnp.exp(-jnp.abs(x)))


def _silu(x):
    return x * jax.nn.sigmoid(x)


def _norm_proj_kernel(x_ref, nw_ref, w_ref, o_ref, h_ref):
    @pl.when(pl.program_id(1) == 0)
    def _():
        x = x_ref[...]
        ms = jnp.mean(x * x, axis=-1, keepdims=True)
        h_ref[...] = (x * lax.rsqrt(ms + NORM_EPS) * nw_ref[...]).astype(BF16)

    o_ref[...] = jnp.dot(h_ref[...], w_ref[...], preferred_element_type=F32)


def _norm_proj(x2, norm_w, w_bf16, tm, tn):
    n = x2.shape[0]
    ncol = w_bf16.shape[1]
    return pl.pallas_call(
        _norm_proj_kernel,
        grid=(n // tm, ncol // tn),
        in_specs=[pl.BlockSpec((tm, D_MODEL), lambda i, j: (i, 0)),
                  pl.BlockSpec((1, D_MODEL), lambda i, j: (0, 0)),
                  pl.BlockSpec((D_MODEL, tn), lambda i, j: (0, j))],
        out_specs=pl.BlockSpec((tm, tn), lambda i, j: (i, j)),
        out_shape=jax.ShapeDtypeStruct((n, ncol), F32),
        scratch_shapes=[pltpu.VMEM((tm, D_MODEL), BF16)],
        compiler_params=_cparams(("parallel", "arbitrary")),
    )(x2, norm_w.reshape(1, D_MODEL), w_bf16)


def _s5_kernel(u_ref, h0_ref, wb_ref, wc_ref, a2k_ref, apow_ref, d_ref, wg_ref, bg_ref,
               y_ref, hl_ref, hs_ref, *, tc):
    c = pl.program_id(1)

    @pl.when(c == 0)
    def _():
        hs_ref[...] = h0_ref[...]

    u = u_ref[...]
    x = _dot(u, wb_ref[...])
    xr, xi = x[:, :S5_W], x[:, S5_W:]
    row = _iota((tc, 1), 0)
    k, d = 0, 1
    while d < tc:
        ar = a2k_ref[k:k + 1, :S5_W]
        ai = a2k_ref[k:k + 1, S5_W:]
        sr = pltpu.roll(xr, d, axis=0)
        si = pltpu.roll(xi, d, axis=0)
        m = row >= d
        xr, xi = (xr + jnp.where(m, ar * sr - ai * si, 0.0),
                  xi + jnp.where(m, ar * si + ai * sr, 0.0))
        k, d = k + 1, d * 2
    hr0, hi0 = hs_ref[:, :S5_W], hs_ref[:, S5_W:]
    pr, pi_ = apow_ref[:, :S5_W], apow_ref[:, S5_W:]
    hr = xr + pr * hr0 - pi_ * hi0
    hi = xi + pr * hi0 + pi_ * hr0
    hs_ref[:, :S5_W] = hr[tc - 1:tc]
    hs_ref[:, S5_W:] = hi[tc - 1:tc]
    y = _dot(hr, wc_ref[:S5_W]) + _dot(hi, wc_ref[S5_W:])
    y = jax.nn.gelu(y + d_ref[...] * u)
    y = y * jax.nn.sigmoid(_dot(y, wg_ref[...]) + bg_ref[...])
    y_ref[...] = y.astype(y_ref.dtype)

    @pl.when(c == pl.num_programs(1) - 1)
    def _():
        hl_ref[...] = hs_ref[...]


def _s5_tables(lam_re, lam_im, log_dt, b_c, c_c, tc):
    dt = jnp.exp(log_dt)[:, None]
    mag = jnp.exp(lam_re * dt)
    ab_re, ab_im = mag * jnp.cos(lam_im * dt), mag * jnp.sin(lam_im * dt)
    den = lam_re * lam_re + lam_im * lam_im
    nr = ab_re - 1.0
    cf_re = (nr * lam_re + ab_im * lam_im) / den
    cf_im = (ab_im * lam_re - nr * lam_im) / den
    br, bi = b_c[..., 0], b_c[..., 1]
    bb_re = cf_re[..., None] * br - cf_im[..., None] * bi
    bb_im = cf_re[..., None] * bi + cf_im[..., None] * br
    eye = jnp.eye(S5_GROUPS, dtype=F32)
    bd_in = lambda m: jnp.einsum('gph,gk->ghkp', m, eye).reshape(BRANCH_W, S5_W)
    wb = jnp.concatenate([bd_in(bb_re), bd_in(bb_im)], axis=1)
    cr, ci = c_c[..., 0], c_c[..., 1]
    bd_out = lambda m: jnp.einsum('ghp,gk->gpkh', m, eye).reshape(S5_W, BRANCH_W)
    wc = jnp.concatenate([bd_out(cr), -bd_out(ci)], axis=0)
    pr, pi_ = ab_re.reshape(1, S5_W), ab_im.reshape(1, S5_W)
    lv_r, lv_i = [], []
    tr, ti = pr, pi_
    d = 1
    while d < tc:
        lv_r.append(pr)
        lv_i.append(pi_)
        tr, ti = (jnp.concatenate([tr, tr * pr - ti * pi_], axis=0),
                  jnp.concatenate([ti, tr * pi_ + ti * pr], axis=0))
        pr, pi_ = pr * pr - pi_ * pi_, 2.0 * pr * pi_
        d *= 2
    n_lv = len(lv_r)
    pad = (-n_lv) % 8
    a2k = jnp.concatenate([jnp.concatenate(lv_r, axis=0), jnp.concatenate(lv_i, axis=0)], axis=1)
    a2k = jnp.pad(a2k, ((0, pad), (0, 0)))
    apow = jnp.concatenate([tr, ti], axis=1)
    return wb.astype(BF16), wc.astype(BF16), a2k, apow


def _s5_mixer(p3, h0, lam_re, lam_im, log_dt, b_c, c_c, d_skip, w_glu, b_glu, tc):
    bsz, t, _ = p3.shape
    wb, wc, a2k, apow = _s5_tables(lam_re, lam_im, log_dt, b_c, c_c, tc)
    h0f = jnp.concatenate([h0[..., 0].reshape(bsz, 1, S5_W), h0[..., 1].reshape(bsz, 1, S5_W)], axis=-1)
    full = lambda a: pl.BlockSpec(a.shape, lambda b, c: (0,) * a.ndim)
    d2, bg2, wg = d_skip.reshape(1, BRANCH_W), b_glu.reshape(1, BRANCH_W), w_glu.astype(BF16)
    y, hl = pl.pallas_call(
        functools.partial(_s5_kernel, tc=tc),
        grid=(bsz, t // tc),
        in_specs=[pl.BlockSpec((None, tc, BRANCH_W), lambda b, c: (b, c, COL_S5 // BRANCH_W)),
                  pl.BlockSpec((None, 1, 2 * S5_W), lambda b, c: (b, 0, 0)),
                  full(wb), full(wc), full(a2k), full(apow), full(d2), full(wg), full(bg2)],
        out_specs=[pl.BlockSpec((None, tc, BRANCH_W), lambda b, c: (b, c, 0)),
                   pl.BlockSpec((None, 1, 2 * S5_W), lambda b, c: (b, 0, 0))],
        out_shape=[jax.ShapeDtypeStruct((bsz, t, BRANCH_W), BF16),
                   jax.ShapeDtypeStruct((bsz, 1, 2 * S5_W), F32)],
        scratch_shapes=[pltpu.VMEM((1, 2 * S5_W), F32)],
        compiler_params=_cparams(("parallel", "arbitrary")),
    )(p3, h0f, wb, wc, a2k, apow, d2, wg, bg2)
    h_last = jnp.stack([hl[:, 0, :S5_W].reshape(bsz, S5_GROUPS, S5_STATE),
                        hl[:, 0, S5_W:].reshape(bsz, S5_GROUPS, S5_STATE)], axis=-1)
    return y, h_last


def _rwkv_kernel(z_ref, sh0_ref, s0_ref, mu_ref, w0_ref, w2_ref, a0_ref, a2_ref, g2_ref,
                 kk_ref, ka_ref, rk_ref, lnw_ref, lnb_ref,
                 y_ref, sl_ref, st_ref, zp_ref, *, l, bb):
    c = pl.program_id(1)

    @pl.when(c == 0)
    def _():
        st_ref[...] = s0_ref[...]
        zp_ref[...] = sh0_ref[...]

    row = _iota((l, 1), 0)
    zms = []
    for b in range(bb):
        z = z_ref[b]
        prev = jnp.where(row == 0, zp_ref[b], pltpu.roll(z, 1, axis=0))
        zp_ref[b] = z[l - 1:l]
        zms.append(z + (prev - z) * mu_ref[...])
    zm = jnp.concatenate(zms, axis=0)
    r, k, v = zm[:, 0:256], zm[:, 256:512], zm[:, 512:768]
    lo = zm[:, 768:896]
    g_lo = zm[:, 896:1024]
    w_log = -_softplus(-(w0_ref[...] + _dot(jnp.tanh(lo), w2_ref[...]))) - 0.5
    lw = -jnp.exp(w_log)
    a = jax.nn.sigmoid(a0_ref[...] + _dot(lo, a2_ref[...]))
    g = _dot(jax.nn.sigmoid(g_lo), g2_ref[...])
    ones_h = _head_ones()
    kk = k * kk_ref[...]
    kk = kk * lax.rsqrt(_dot2(kk * kk, ones_h) + NORM_EPS)
    k = k * (1.0 + (a - 1.0) * ka_ref[...])
    kka = kk * a

    cum = _cumsum_rows(lw, l, bb)
    p_incl = jnp.exp(cum)
    p_inv = jnp.exp(-cum)
    kt = kk * jnp.exp(cum - lw)
    rt = r * p_incl
    kh = k * p_inv
    ah = kka * p_inv

    hm = _head_mask(l)
    strict, incl = _tri_masks(l)
    streams = range(bb)
    rows = [slice(b * l, (b + 1) * l) for b in streams]
    ex = lambda x: [_expand(x[s], hm) for s in rows]
    kt_b, rt_b, v_b = [kt[s] for s in rows], [rt[s] for s in rows], [v[s] for s in rows]
    kh_e, ah_e, v_e = ex(kh), ex(ah), ex(v)
    a_aa = [jnp.where(strict, _dot_nt(kt_b[b], ah_e[b]), 0.0) for b in streams]
    a_ak = [jnp.where(strict, _dot_nt(kt_b[b], kh_e[b]), 0.0) for b in streams]
    b_ra = [jnp.where(incl, _dot_nt(rt_b[b], ah_e[b]), 0.0) for b in streams]
    b_rk = [jnp.where(incl, _dot_nt(rt_b[b], kh_e[b]), 0.0) for b in streams]
    t_inv = _unit_lower_inverse(a_aa, l)
    st = [st_ref[b] for b in streams]
    rhs = [_dot_nt(kt_b[b], st[b]) + _dot(a_ak[b], v_e[b]) for b in streams]
    yb = [_dot_nt(rt_b[b], st[b]) + _dot(b_rk[b], v_e[b]) for b in streams]
    u = [_dot(t_inv[b], _expand(rhs[b], hm)) for b in streams]
    yb = [yb[b] - _dot(b_ra[b], _expand(u[b], hm)) for b in streams]
    to_end = [jnp.exp(cum[s][l - 1:l] - cum[s]) for s in rows]
    lhs_t = [jnp.concatenate([v_b[b], -u[b]], axis=0).T for b in streams]
    rhs_k = [jnp.concatenate([k[rows[b]] * to_end[b], kka[rows[b]] * to_end[b]], axis=0) for b in streams]
    head_blk = _same_head(HEAD_W)
    for b in streams:
        p_last = p_incl[(b + 1) * l - 1:(b + 1) * l]
        st_ref[b] = st[b] * p_last + jnp.where(head_blk, _dot(lhs_t[b], rhs_k[b]), 0.0)
    y = jnp.concatenate(yb, axis=0)

    inv_w = 1.0 / HEAD_W
    mean = _dot2(y, ones_h) * inv_w
    yc = y - mean
    var = _dot2(yc * yc, ones_h) * inv_w
    y = yc * lax.rsqrt(var + RW_EPS) * lnw_ref[...] + lnb_ref[...]
    bonus = _dot2(r * k * rk_ref[...], ones_h) * v
    y = ((y + bonus) * g).astype(y_ref.dtype)
    for b in range(bb):
        y_ref[b] = y[b * l:(b + 1) * l]

    @pl.when(c == pl.num_programs(1) - 1)
    def _():
        sl_ref[...] = st_ref[...]


STACK_ROWS = 256


def _streams_per_step(bsz, l):
    bb = max(1, min(bsz, STACK_ROWS // l))
    while bsz % bb:
        bb -= 1
    return bb


def _block_diag_heads(s):
    bsz = s.shape[0]
    eye = jnp.eye(HEADS, dtype=s.dtype)
    return jnp.einsum('bhij,hg->bhigj', s, eye).reshape(bsz, BRANCH_W, BRANCH_W)


def _diag_blocks(s):
    bsz = s.shape[0]
    s5 = s.reshape(bsz, HEADS, HEAD_W, HEADS, HEAD_W)
    return jnp.stack([s5[:, h, :, h, :] for h in range(HEADS)], axis=1)


def _rwkv_mixer(p3, shift0, s0, mu, w0, w2, a0, a2, g2, k_k, k_a, r_k, ln_w, ln_b, l):
    bsz, t, _ = p3.shape
    row = lambda a: a.reshape(1, -1)
    w2p = jnp.concatenate([w2, jnp.zeros_like(w2)], axis=0).astype(BF16)
    a2p = jnp.concatenate([jnp.zeros_like(a2), a2], axis=0).astype(BF16)
    args = (shift0.reshape(bsz, 1, RW_COLS), _block_diag_heads(s0), row(mu), row(w0), w2p, row(a0), a2p,
            g2.astype(BF16), row(k_k), row(k_a), row(r_k), row(ln_w), row(ln_b))
    full = lambda a: pl.BlockSpec(a.shape, lambda b, c: (0,) * a.ndim)
    bb = _streams_per_step(bsz, l)
    y, sl = pl.pallas_call(
        functools.partial(_rwkv_kernel, l=l, bb=bb),
        grid=(bsz // bb, t // l),
        in_specs=[pl.BlockSpec((bb, l, RW_COLS), lambda b, c: (b, c, COL_RW // RW_COLS)),
                  pl.BlockSpec((bb, 1, RW_COLS), lambda b, c: (b, 0, 0)),
                  pl.BlockSpec((bb, BRANCH_W, BRANCH_W), lambda b, c: (b, 0, 0))]
                 + [full(a) for a in args[2:]],
        out_specs=[pl.BlockSpec((bb, l, BRANCH_W), lambda b, c: (b, c, 0)),
                   pl.BlockSpec((bb, BRANCH_W, BRANCH_W), lambda b, c: (b, 0, 0))],
        out_shape=[jax.ShapeDtypeStruct((bsz, t, BRANCH_W), BF16),
                   jax.ShapeDtypeStruct((bsz, BRANCH_W, BRANCH_W), F32)],
        scratch_shapes=[pltpu.VMEM((bb, BRANCH_W, BRANCH_W), F32), pltpu.VMEM((bb, 1, RW_COLS), F32)],
        compiler_params=_cparams(("parallel", "arbitrary")),
    )(p3, *args)
    return y, _diag_blocks(sl)


def _sgu_kernel(z_ref, lnw_ref, lnb_ref, wm_ref, bias_ref, o_ref, v_ref, *, l):
    zg = jax.nn.gelu(z_ref[...])
    u, v = zg[:, :BRANCH_W], zg[:, BRANCH_W:]
    mean = jnp.mean(v, axis=-1, keepdims=True)
    vc = v - mean
    var = jnp.mean(vc * vc, axis=-1, keepdims=True)
    v = vc * lax.rsqrt(var + NORM_EPS) * lnw_ref[...] + lnb_ref[...]
    grp = _iota((l, BRANCH_W), 1) // HEAD_W
    mixed = bias_ref[...]
    for gi in range(HEADS):
        mixed = mixed + _dot(wm_ref[gi], jnp.where(grp == gi, v, 0.0))
    o_ref[...] = (u * mixed).astype(o_ref.dtype)
    v_ref[...] = v


def _sgu_mixer(p3, ln_w, ln_b, w_s, b_s):
    bsz, t, _ = p3.shape
    l = min(SG_CHUNK, t)
    tril = jnp.tril(jnp.ones((l, l), F32))
    wm = (w_s[:, :l, :l] * tril).astype(BF16)
    bias = jnp.repeat(jnp.transpose(b_s[:, :l]), HEAD_W, axis=1)
    row = lambda a: a.reshape(1, -1)
    full = lambda a: pl.BlockSpec(a.shape, lambda b, c: (0,) * a.ndim)
    args = (row(ln_w), row(ln_b), wm, bias)
    return pl.pallas_call(
        functools.partial(_sgu_kernel, l=l),
        grid=(bsz, t // l),
        in_specs=[pl.BlockSpec((None, l, 2 * BRANCH_W), lambda b, c: (b, c, COL_SG // (2 * BRANCH_W)))]
                 + [full(a) for a in args],
        out_specs=[pl.BlockSpec((None, l, BRANCH_W), lambda b, c: (b, c, 0)),
                   pl.BlockSpec((None, l, BRANCH_W), lambda b, c: (b, c, 0))],
        out_shape=[jax.ShapeDtypeStruct((bsz, t, BRANCH_W), BF16),
                   jax.ShapeDtypeStruct((bsz, t, BRANCH_W), F32)],
        compiler_params=_cparams(("parallel", "parallel")),
    )(p3, *args)


def _gdn_kernel(z_ref, ab_ref, cv0_ref, s0_ref, cw_ref, alog_ref, dtb_ref, nw_ref,
                y_ref, sl_ref, st_ref, cv_ref, *, l, bb):
    c = pl.program_id(1)

    @pl.when(c == 0)
    def _():
        st_ref[...] = s0_ref[...]
        cv_ref[...] = cv0_ref[...]

    row8 = _iota((8, 1), 0)
    convs, gates = [], []
    for b in range(bb):
        z = z_ref[b]
        qkv = z[:, :GD_QKV]
        gates.append(z[:, GD_QKV:])
        carry = cv_ref[b]
        cv_ref[b] = qkv[l - 8:l]
        conv = qkv * cw_ref[GD_CONV - 1:GD_CONV]
        for j in range(1, GD_CONV):
            sh = pltpu.roll(qkv, j, axis=0)
            top = jnp.where(row8 < j, pltpu.roll(carry, j, axis=0), sh[:8])
            sh = jnp.concatenate([top, sh[8:]], axis=0) if l > 8 else top
            conv = conv + sh * cw_ref[GD_CONV - 1 - j:GD_CONV - j]
        convs.append(conv)
    conv = _silu(jnp.concatenate(convs, axis=0))
    gate = jnp.concatenate(gates, axis=0)
    q, k, v = conv[:, :256], conv[:, 256:512], conv[:, 512:768]
    ones_h = _head_ones()
    q = q * lax.rsqrt(_dot2(q * q, ones_h) + NORM_EPS) * (HEAD_W ** -0.5)
    k = k * lax.rsqrt(_dot2(k * k, ones_h) + NORM_EPS)
    ab = jnp.concatenate([ab_ref[b] for b in range(bb)], axis=0)
    lane_h = _iota((bb * l, BRANCH_W), 1) // HEAD_W
    a_in = jnp.zeros((bb * l, BRANCH_W), F32)
    b_in = jnp.zeros((bb * l, BRANCH_W), F32)
    for h in range(HEADS):
        a_in = jnp.where(lane_h == h, ab[:, h:h + 1], a_in)
        b_in = jnp.where(lane_h == h, ab[:, HEADS + h:HEADS + h + 1], b_in)
    beta = jax.nn.sigmoid(b_in)
    g = -jnp.exp(alog_ref[...]) * _softplus(a_in + dtb_ref[...])
    gc = _cumsum_rows(g, l, bb)
    eg = jnp.exp(gc)
    kb = k * beta
    vb = v * beta
    kbg = kb * eg
    qg = q * eg

    hm = _head_mask(l)
    strict, incl = _tri_masks(l)
    n = HEADS * l
    streams = range(bb)
    rows = [slice(b * l, (b + 1) * l) for b in streams]
    ex = lambda x: [_expand(x[s], hm) for s in rows]
    k_e, vb_e, kbg_e = ex(k), ex(vb), ex(kbg)
    lane_hd = _iota((l, n), 1) // l
    eye = _iota((l, n), 0) == (_iota((l, n), 1) % l)
    decay = []
    for s in rows:
        gi = jnp.zeros((l, n), F32)
        for h in range(HEADS):
            gi = jnp.where(lane_hd == h, gc[s][:, h * HEAD_W:h * HEAD_W + 1], gi)
        gj = jnp.sum(jnp.where(eye, gi, 0.0), axis=0, keepdims=True)
        decay.append(jnp.where(incl, jnp.exp(jnp.where(incl, gi - gj, 0.0)), 0.0))
    lm = [jnp.where(strict, _dot_nt(kb[rows[b]], k_e[b]) * decay[b], 0.0) for b in streams]
    qk = [_dot_nt(q[rows[b]], k_e[b]) * decay[b] for b in streams]
    t_inv = _unit_lower_inverse(lm, l)
    uc = [_dot(t_inv[b], vb_e[b]) for b in streams]
    wc = [_dot(t_inv[b], kbg_e[b]) for b in streams]
    st = [st_ref[b] for b in streams]
    ob = [_dot(qg[rows[b]], st[b]) for b in streams]
    v_new = [uc[b] - _dot(wc[b], st[b]) for b in streams]
    ob = [ob[b] + _dot(qk[b], _expand(v_new[b], hm)) for b in streams]
    g_last = [gc[s][l - 1:l] for s in rows]
    k_dec = [(k[rows[b]] * jnp.exp(g_last[b] - gc[rows[b]])).T for b in streams]
    head_blk = _same_head(HEAD_W)
    for b in streams:
        st_ref[b] = st[b] * jnp.exp(g_last[b]) + jnp.where(head_blk, _dot(k_dec[b], v_new[b]), 0.0)
    o = jnp.concatenate(ob, axis=0)
    ms = _dot2(o * o, ones_h) * (1.0 / HEAD_W)
    o = (o * lax.rsqrt(ms + NORM_EPS) * nw_ref[...] * _silu(gate)).astype(y_ref.dtype)
    for b in range(bb):
        y_ref[b] = o[b * l:(b + 1) * l]

    @pl.when(c == pl.num_programs(1) - 1)
    def _():
        sl_ref[...] = st_ref[...]


def _gdn_mixer(p3, conv0, s0, conv_w, a_log, dt_bias, norm_w, l):
    bsz, t, _ = p3.shape
    cv0 = jnp.pad(conv0, ((0, 0), (8 - (GD_CONV - 1), 0), (0, 0)))
    cw = jnp.pad(conv_w, ((0, 8 - GD_CONV), (0, 0)))
    per_head = lambda a: jnp.repeat(a, HEAD_W).reshape(1, BRANCH_W)
    args = (cv0, _block_diag_heads(s0), cw, per_head(a_log), per_head(dt_bias),
            jnp.tile(norm_w, HEADS).reshape(1, BRANCH_W))
    full = lambda a: pl.BlockSpec(a.shape, lambda b, c: (0,) * a.ndim)
    bb = _streams_per_step(bsz, l)
    y, sl = pl.pallas_call(
        functools.partial(_gdn_kernel, l=l, bb=bb),
        grid=(bsz // bb, t // l),
        in_specs=[pl.BlockSpec((bb, l, 1024), lambda b, c: (b, c, COL_GD // 1024)),
                  pl.BlockSpec((bb, l, LANES), lambda b, c: (b, c, COL_AB // LANES)),
                  pl.BlockSpec((bb, 8, GD_QKV), lambda b, c: (b, 0, 0)),
                  pl.BlockSpec((bb, BRANCH_W, BRANCH_W), lambda b, c: (b, 0, 0))]
                 + [full(a) for a in args[2:]],
        out_specs=[pl.BlockSpec((bb, l, BRANCH_W), lambda b, c: (b, c, 0)),
                   pl.BlockSpec((bb, BRANCH_W, BRANCH_W), lambda b, c: (b, 0, 0))],
        out_shape=[jax.ShapeDtypeStruct((bsz, t, BRANCH_W), BF16),
                   jax.ShapeDtypeStruct((bsz, BRANCH_W, BRANCH_W), F32)],
        scratch_shapes=[pltpu.VMEM((bb, BRANCH_W, BRANCH_W), F32), pltpu.VMEM((bb, 8, GD_QKV), F32)],
        compiler_params=_cparams(("parallel", "arbitrary")),
    )(p3, p3, *args)
    return y, _diag_blocks(sl)


def _merge_kernel(ya_ref, yb_ref, yc_ref, yd_ref, g_ref, x_ref, wbr_ref, wout_ref, o_ref):
    m = None
    for b, y_ref in enumerate((ya_ref, yb_ref, yc_ref, yd_ref)):
        br = jnp.dot(y_ref[...], wbr_ref[b], preferred_element_type=F32)
        term = jax.nn.sigmoid(g_ref[:, b * D_MODEL:(b + 1) * D_MODEL]) * br
        m = term if m is None else m + term
    o_ref[...] = x_ref[...] + jnp.dot(m.astype(BF16), wout_ref[...], preferred_element_type=F32)


def _merge(ys, p2, x2, w_branch, w_out, tm):
    n = x2.shape[0]
    yspec = pl.BlockSpec((tm, BRANCH_W), lambda i: (i, 0))
    return pl.pallas_call(
        _merge_kernel,
        grid=(n // tm,),
        in_specs=[yspec, yspec, yspec, yspec,
                  pl.BlockSpec((tm, 4 * D_MODEL), lambda i: (i, 0)),
                  pl.BlockSpec((tm, D_MODEL), lambda i: (i, 0)),
                  pl.BlockSpec((4, BRANCH_W, D_MODEL), lambda i: (0, 0, 0)),
                  pl.BlockSpec((D_MODEL, D_MODEL), lambda i: (0, 0))],
        out_specs=pl.BlockSpec((tm, D_MODEL), lambda i: (i, 0)),
        out_shape=jax.ShapeDtypeStruct((n, D_MODEL), F32),
        compiler_params=_cparams(("parallel",)),
    )(*[y.reshape(n, BRANCH_W) for y in ys], p2, x2, w_branch.astype(BF16), w_out.astype(BF16))


def _ffn_kernel(x_ref, nw_ref, w1_ref, w3_ref, w2_ref, o_ref, h_ref):
    j = pl.program_id(1)

    @pl.when(j == 0)
    def _():
        x = x_ref[...]
        ms = jnp.mean(x * x, axis=-1, keepdims=True)
        h_ref[...] = (x * lax.rsqrt(ms + NORM_EPS) * nw_ref[...]).astype(BF16)
        o_ref[...] = x

    h = h_ref[...]
    a = _silu(jnp.dot(h, w1_ref[...], preferred_element_type=F32)) * jnp.dot(h, w3_ref[...], preferred_element_type=F32)
    o_ref[...] += jnp.dot(a.astype(BF16), w2_ref[...], preferred_element_type=F32)


def _ffn(x2, norm_w, w1, w3, w2, tm, tf):
    n = x2.shape[0]
    dff = w1.shape[1]
    return pl.pallas_call(
        _ffn_kernel,
        grid=(n // tm, dff // tf),
        in_specs=[pl.BlockSpec((tm, D_MODEL), lambda i, j: (i, 0)),
                  pl.BlockSpec((1, D_MODEL), lambda i, j: (0, 0)),
                  pl.BlockSpec((D_MODEL, tf), lambda i, j: (0, j)),
                  pl.BlockSpec((D_MODEL, tf), lambda i, j: (0, j)),
                  pl.BlockSpec((tf, D_MODEL), lambda i, j: (j, 0))],
        out_specs=pl.BlockSpec((tm, D_MODEL), lambda i, j: (i, 0)),
        out_shape=jax.ShapeDtypeStruct((n, D_MODEL), F32),
        scratch_shapes=[pltpu.VMEM((tm, D_MODEL), BF16)],
        compiler_params=_cparams(("parallel", "arbitrary")),
    )(x2, norm_w.reshape(1, D_MODEL), w1.astype(BF16), w3.astype(BF16), w2.astype(BF16))


MOE_SUB = 256
MOE_TILE = 2048
MOE_PLACE_ROWS = 1024


def _router_kernel(x_ref, nw_ref, wr_ref, br_ref, h_ref, rcol_ref, gcol_ref, rrow_ref, cnt_ref, seen_ref,
                   *, tm, k):
    x = x_ref[...]
    ms = jnp.mean(x * x, axis=-1, keepdims=True)
    h = x * lax.rsqrt(ms + NORM_EPS) * nw_ref[...]
    h_ref[...] = h.astype(BF16)
    lane = _iota((tm, LANES), 1)
    logits = jnp.where(lane < N_EXPERTS, _dot_hp(h, wr_ref[...]) + br_ref[...], -jnp.inf)
    m1 = jnp.max(logits, axis=1, keepdims=True)
    i1 = jnp.min(jnp.where(logits == m1, lane, LANES), axis=1, keepdims=True)
    rest = jnp.where(lane == i1, -jnp.inf, logits)
    m2 = jnp.max(rest, axis=1, keepdims=True)
    i2 = jnp.min(jnp.where(rest == m2, lane, LANES), axis=1, keepdims=True)
    e2 = jnp.exp(m2 - m1)
    g1 = 1.0 / (1.0 + e2)
    g2 = e2 / (1.0 + e2)
    sel = (lane == i1) | (lane == i2)
    self32 = sel.astype(F32)
    gcol_ref[...] = jnp.where(lane == i1, g1, 0.0) + jnp.where(lane == i2, g2, 0.0)
    @pl.when(pl.program_id(0) % k == 0)
    def _():
        seen_ref[...] = jnp.zeros_like(seen_ref)

    seen = seen_ref[...]
    tri = (_iota((tm, tm), 1) < _iota((tm, tm), 0)).astype(BF16)
    rank = jnp.dot(tri, self32.astype(BF16), preferred_element_type=F32) + seen
    r = jnp.where(sel, rank, -1.0)
    rcol_ref[...] = r
    rrow_ref[...] = r.T[:N_EXPERTS]
    seen = seen + jnp.sum(self32, axis=0, keepdims=True)
    seen_ref[...] = seen
    cnt_ref[...] = jnp.broadcast_to(seen, (8, LANES))


def _router(x2, norm_w, w_router, b_router, tm, k):
    n = x2.shape[0]
    nt = n // (tm * k)
    wr = jnp.pad(w_router, ((0, 0), (0, LANES - N_EXPERTS)))
    br = jnp.pad(b_router, (0, LANES - N_EXPERTS)).reshape(1, LANES)
    return pl.pallas_call(
        functools.partial(_router_kernel, tm=tm, k=k),
        grid=(n // tm,),
        in_specs=[pl.BlockSpec((tm, D_MODEL), lambda i: (i, 0)),
                  pl.BlockSpec((1, D_MODEL), lambda i: (0, 0)),
                  pl.BlockSpec((D_MODEL, LANES), lambda i: (0, 0)),
                  pl.BlockSpec((1, LANES), lambda i: (0, 0))],
        out_specs=[pl.BlockSpec((tm, D_MODEL), lambda i: (i, 0)),
                   pl.BlockSpec((tm, LANES), lambda i: (i, 0)),
                   pl.BlockSpec((tm, LANES), lambda i: (i, 0)),
                   pl.BlockSpec((None, N_EXPERTS, tm), lambda i: (i // k, 0, i % k)),
                   pl.BlockSpec((None, 8, LANES), lambda i: (i // k, 0, 0))],
        out_shape=[jax.ShapeDtypeStruct((n, D_MODEL), BF16),
                   jax.ShapeDtypeStruct((n, LANES), F32),
                   jax.ShapeDtypeStruct((n, LANES), F32),
                   jax.ShapeDtypeStruct((nt, N_EXPERTS, tm * k), F32),
                   jax.ShapeDtypeStruct((nt, 8, LANES), F32)],
        scratch_shapes=[pltpu.VMEM((1, LANES), F32)],
        compiler_params=_cparams(("arbitrary",)),
    )(x2, norm_w.reshape(1, D_MODEL), wr, br)


def _moe_kernel(cnt_ref, h_ref, rrow_ref, rcol_ref, gcol_ref, w1_ref, w3_ref, w2_ref, o_ref,
                xs_ref, acc_ref, *, tm):
    i, e, c = pl.program_id(0), pl.program_id(1), pl.program_id(2)
    n_sub = (cnt_ref[i * N_EXPERTS + e] + (MOE_SUB - 1)) // MOE_SUB

    @pl.when((e == 0) & (c == 0))
    def _():
        o_ref[...] = jnp.zeros_like(o_ref)

    @pl.when(c == 0)
    def _():
        rrow = rrow_ref[pl.ds(e, 1), :]
        slot = _iota((MOE_SUB, 1), 0).astype(F32)

        def gather(s, carry):
            base = pl.multiple_of(s * MOE_SUB, MOE_SUB)
            pick = (rrow == slot + (s * MOE_SUB).astype(F32)).astype(BF16)
            xs_ref[pl.ds(base, MOE_SUB), :] = jnp.dot(pick, h_ref[...], preferred_element_type=F32).astype(BF16)
            acc_ref[pl.ds(base, MOE_SUB), :] = jnp.zeros((MOE_SUB, D_MODEL), F32)
            return carry

        lax.fori_loop(0, n_sub, gather, 0)

    def expert(s, carry):
        base = pl.multiple_of(s * MOE_SUB, MOE_SUB)
        xb = xs_ref[pl.ds(base, MOE_SUB), :]
        a = (_silu(jnp.dot(xb, w1_ref[...], preferred_element_type=F32))
             * jnp.dot(xb, w3_ref[...], preferred_element_type=F32))
        acc_ref[pl.ds(base, MOE_SUB), :] += jnp.dot(a.astype(BF16), w2_ref[...], preferred_element_type=F32)
        return carry

    lax.fori_loop(0, n_sub, expert, 0)

    @pl.when(c == pl.num_programs(2) - 1)
    def _():
        tp = min(tm, MOE_PLACE_ROWS)
        lane = _iota((tp, LANES), 1)
        slot = _iota((1, MOE_SUB), 1).astype(F32)

        def scatter(s, carry):
            base = pl.multiple_of(s * MOE_SUB, MOE_SUB)
            a = acc_ref[pl.ds(base, MOE_SUB), :]
            hi = a.astype(BF16)
            lo = (a - hi.astype(F32)).astype(BF16)
            for r0 in range(0, tm, tp):
                rcol = jnp.sum(jnp.where(lane == e, rcol_ref[r0:r0 + tp], 0.0), axis=1, keepdims=True)
                gate = jnp.sum(jnp.where(lane == e, gcol_ref[r0:r0 + tp], 0.0), axis=1, keepdims=True)
                place = (rcol == slot + (s * MOE_SUB).astype(F32)).astype(BF16)
                back = (jnp.dot(place, hi, preferred_element_type=F32)
                        + jnp.dot(place, lo, preferred_element_type=F32))
                o_ref[r0:r0 + tp] += gate * back
            return carry

        lax.fori_loop(0, n_sub, scatter, 0)


def _moe(h2, rrow, rcol, gcol, counts, w1, w3, w2, tm, tf):
    n = h2.shape[0]
    nt = n // tm
    dff = w1.shape[2]
    grid_spec = pltpu.PrefetchScalarGridSpec(
        num_scalar_prefetch=1,
        grid=(nt, N_EXPERTS, dff // tf),
        in_specs=[pl.BlockSpec((tm, D_MODEL), lambda i, e, c, cnt: (i, 0), pipeline_mode=pl.Buffered(1)),
                  pl.BlockSpec((None, N_EXPERTS, tm), lambda i, e, c, cnt: (i, 0, 0)),
                  pl.BlockSpec((tm, LANES), lambda i, e, c, cnt: (i, 0), pipeline_mode=pl.Buffered(1)),
                  pl.BlockSpec((tm, LANES), lambda i, e, c, cnt: (i, 0), pipeline_mode=pl.Buffered(1)),
                  pl.BlockSpec((None, D_MODEL, tf), lambda i, e, c, cnt: (e, 0, c)),
                  pl.BlockSpec((None, D_MODEL, tf), lambda i, e, c, cnt: (e, 0, c)),
                  pl.BlockSpec((None, tf, D_MODEL), lambda i, e, c, cnt: (e, c, 0))],
        out_specs=pl.BlockSpec((tm, D_MODEL), lambda i, e, c, cnt: (i, 0), pipeline_mode=pl.Buffered(1)),
        scratch_shapes=[pltpu.VMEM((tm, D_MODEL), BF16), pltpu.VMEM((tm, D_MODEL), F32)],
    )
    return pl.pallas_call(
        functools.partial(_moe_kernel, tm=tm),
        grid_spec=grid_spec,
        out_shape=jax.ShapeDtypeStruct((n, D_MODEL), F32),
        compiler_params=_cparams(("parallel", "arbitrary", "arbitrary")),
    )(counts, h2, rrow, rcol, gcol, w1, w3, w2)


def _moe_ffn(x2, norm_w, w_router, b_router, w1, w3, w2):
    n = x2.shape[0]
    tm = min(MOE_TILE, n)
    tr = min(MOE_PLACE_ROWS, tm)
    h2, rcol, gcol, rrow, cnt = _router(x2, norm_w, w_router, b_router, tr, tm // tr)
    counts = cnt[:, 0, :N_EXPERTS].astype(jnp.int32).reshape(-1)
    return _moe(h2, rrow, rcol, gcol, counts, w1.astype(BF16), w3.astype(BF16), w2.astype(BF16), tm, 512)


def _final_kernel(x_ref, m_ref, nw_ref, o_ref):
    x = x_ref[...] + m_ref[...]
    ms = jnp.mean(x * x, axis=-1, keepdims=True)
    o_ref[...] = x * lax.rsqrt(ms + NORM_EPS) * nw_ref[...]


def _final(x2, m2, norm_w, tm):
    n = x2.shape[0]
    spec = pl.BlockSpec((tm, D_MODEL), lambda i: (i, 0))
    return pl.pallas_call(
        _final_kernel,
        grid=(n // tm,),
        in_specs=[spec, spec, pl.BlockSpec((1, D_MODEL), lambda i: (0, 0))],
        out_specs=spec,
        out_shape=jax.ShapeDtypeStruct((n, D_MODEL), F32),
        compiler_params=_cparams(("parallel",)),
    )(x2, m2, norm_w.reshape(1, D_MODEL))


def _permute_w_in(w):
    s5, rw, sg = w[:, 0:256], w[:, 256:1280], w[:, 1280:1792]
    gd, ab, gates = w[:, 1792:2816], w[:, 2816:2824], w[:, 2824:6920]
    pad = jnp.zeros((D_MODEL, PROJ_COLS - COL_AB - 8), w.dtype)
    return jnp.concatenate([gates, rw, gd, sg, s5, ab, pad], axis=1).astype(BF16)


def _run_trunk(x, s5_h, rw_s, rw_shift, gd_s, gd_conv, p, w_in_perm):
    bsz, t, _ = x.shape
    n = bsz * t
    tm = min(1024, n)
    l = min(64, t)
    tc_s5 = min(128, t)
    new = ([], [], [], [], [], [])
    x2 = x.reshape(n, D_MODEL)
    moe_out = None
    for layer in range(2):
        g = lambda name: p[name][layer]
        proj = _norm_proj(x2, g('norm1_w'), w_in_perm[layer], tm, 1024)
        p3 = proj.reshape(bsz, t, PROJ_COLS)
        y_a, s5_new = _s5_mixer(p3, s5_h[layer], g('s5_lam_re'), g('s5_lam_im'), g('s5_log_dt'), g('s5_b'),
                                g('s5_c'), g('s5_d'), g('s5_w_glu'), g('s5_b_glu'), tc_s5)
        y_b, rw_new = _rwkv_mixer(p3, rw_shift[layer], rw_s[layer], g('rw_mu'), g('rw_w0'), g('rw_w2'),
                                  g('rw_a0'), g('rw_a2'), g('rw_g2'), g('rw_k_k'), g('rw_k_a'), g('rw_r_k'),
                                  g('rw_ln_w'), g('rw_ln_b'), l)
        y_c, sg_v = _sgu_mixer(p3, g('sg_ln_w'), g('sg_ln_b'), g('sg_w_s'), g('sg_b_s'))
        y_d, gd_new = _gdn_mixer(p3, gd_conv[layer], gd_s[layer], g('gd_conv_w'), g('gd_a_log'),
                                 g('gd_dt_bias'), g('gd_norm_w'), l)
        shift_new = p3[:, t - 1, COL_RW:COL_RW + RW_COLS]
        conv_new = p3[:, t - (GD_CONV - 1):, COL_GD:COL_GD + GD_QKV]
        x2 = _merge((y_a, y_b, y_c, y_d), proj, x2, g('w_branch'), g('w_out'), min(512, n))
        j = layer // 2
        if layer % 2 == 0:
            x2 = _ffn(x2, g('norm2_w'), p['ffn_w1'][j], p['ffn_w3'][j], p['ffn_w2'][j], min(512, n), 1408)
        else:
            moe_out = _moe_ffn(x2, g('norm2_w'), p['moe_router'][j], p['moe_router_b'][j],
                               p['moe_w1'][j], p['moe_w3'][j], p['moe_w2'][j])
        for lst, s in zip(new, (s5_new, rw_new, shift_new, gd_new, conv_new, sg_v)):
            lst.append(s)
    y = _final(x2, moe_out, p['final_norm_w'], tm).reshape(bsz, t, D_MODEL)
    return y, [jnp.stack(lst) for lst in new]


def kernel(x_prompt, x_sample, state_s5, state_rwkv, state_rwkv_shift, state_gdn, state_gdn_conv, norm1_w, w_in, s5_lam_re, s5_lam_im, s5_log_dt, s5_b, s5_c, s5_d, s5_w_glu, s5_b_glu, rw_mu, rw_w0, rw_w2, rw_a0, rw_a2, rw_g2, rw_k_k, rw_k_a, rw_r_k, rw_ln_w, rw_ln_b, sg_ln_w, sg_ln_b, sg_w_s, sg_b_s, gd_conv_w, gd_a_log, gd_dt_bias, gd_norm_w, w_branch, w_out, norm2_w, ffn_w1, ffn_w3, ffn_w2, moe_router, moe_router_b, moe_w1, moe_w3, moe_w2, final_norm_w):
    p = {
        'norm1_w': norm1_w, 's5_lam_re': s5_lam_re, 's5_lam_im': s5_lam_im, 's5_log_dt': s5_log_dt,
        's5_b': s5_b, 's5_c': s5_c, 's5_d': s5_d, 's5_w_glu': s5_w_glu, 's5_b_glu': s5_b_glu,
        'rw_mu': rw_mu, 'rw_w0': rw_w0, 'rw_w2': rw_w2, 'rw_a0': rw_a0, 'rw_a2': rw_a2, 'rw_g2': rw_g2,
        'rw_k_k': rw_k_k, 'rw_k_a': rw_k_a, 'rw_r_k': rw_r_k, 'rw_ln_w': rw_ln_w, 'rw_ln_b': rw_ln_b,
        'sg_ln_w': sg_ln_w, 'sg_ln_b': sg_ln_b, 'sg_w_s': sg_w_s, 'sg_b_s': sg_b_s,
        'gd_conv_w': gd_conv_w, 'gd_a_log': gd_a_log, 'gd_dt_bias': gd_dt_bias, 'gd_norm_w': gd_norm_w,
        'w_branch': w_branch, 'w_out': w_out, 'norm2_w': norm2_w,
        'ffn_w1': ffn_w1, 'ffn_w3': ffn_w3, 'ffn_w2': ffn_w2,
        'moe_router': moe_router, 'moe_router_b': moe_router_b, 'moe_w1': moe_w1, 'moe_w3': moe_w3, 'moe_w2': moe_w2,
        'final_norm_w': final_norm_w,
    }
    w_in_perm = [_permute_w_in(w_in[layer]) for layer in range(2)]
    bp, dt = x_prompt.shape[0], x_prompt.dtype
    depth = w_in.shape[0]
    y_prompt, (s5_p, rw_p, rwsh_p, gd_p, gdc_p, _) = _run_trunk(
        x_prompt,
        jnp.zeros((depth, bp, S5_GROUPS, S5_STATE, 2), dt),
        jnp.zeros((depth, bp, HEADS, HEAD_W, HEAD_W), dt),
        jnp.zeros((depth, bp, RW_COLS), dt),
        jnp.zeros((depth, bp, HEADS, HEAD_W, HEAD_W), dt),
        jnp.zeros((depth, bp, GD_CONV - 1, GD_QKV), dt),
        p, w_in_perm)
    y_sample, (s5_s, rw_s, rwsh_s, gd_s, gdc_s, sgv_s) = _run_trunk(
        x_sample, state_s5, state_rwkv, state_rwkv_shift, state_gdn, state_gdn_conv, p, w_in_perm)
    return (y_prompt, y_sample, s5_p, rw_p, rwsh_p, gd_p, gdc_p, s5_s, rw_s, rwsh_s, gd_s, gdc_s, sgv_s)
```

```python
import functools

import jax
import jax.numpy as jnp
from jax import lax
from jax.experimental import pallas as pl
from jax.experimental.pallas import tpu as pltpu

F32 = jnp.float32
BF16 = jnp.bfloat16

D_MODEL = 1024
BRANCH_W = 256
HEADS = 4
HEAD_W = 64
S5_GROUPS = 16
S5_GROUP = 16
S5_STATE = 64
S5_W = S5_GROUPS * S5_STATE
S5_ROWS = 8
SG_CHUNK = 128
SG_ROWS = 512
GD_CONV = 4
GD_QKV = 3 * BRANCH_W
RW_COLS = 1024
RW_EPS = 64e-5
NORM_EPS = 1e-6
N_EXPERTS = 8
LANES = 128

COL_RW = 0
COL_GD = 1024
COL_SG = 2048
COL_S5 = 2560
COL_AB = 2816
PROJ_COLS = 3072

VMEM_LIMIT = 48 * 1024 * 1024


def _cparams(sem):
    return pltpu.CompilerParams(dimension_semantics=sem, vmem_limit_bytes=VMEM_LIMIT)


def _dot(a, b):
    return jnp.dot(a.astype(BF16), b.astype(BF16), preferred_element_type=F32)


def _dot_nt(a, b):
    return lax.dot_general(a.astype(BF16), b.astype(BF16), (((1,), (1,)), ((), ())),
                           preferred_element_type=F32)


def _split3(a):
    hi = a.astype(BF16)
    r1 = a - hi.astype(F32)
    mid = r1.astype(BF16)
    lo = (r1 - mid.astype(F32)).astype(BF16)
    return hi, mid, lo


def _dot3_left(b_exact, a):
    hi, mid, lo = _split3(a)
    b = b_exact.astype(BF16)
    return (jnp.dot(b, hi, preferred_element_type=F32) + jnp.dot(b, mid, preferred_element_type=F32)
            + jnp.dot(b, lo, preferred_element_type=F32))


def _dot_hp(a, b):
    a0, a1, a2 = _split3(a)
    b0, b1, b2 = _split3(b)
    d = lambda x, y: jnp.dot(x, y, preferred_element_type=F32)
    return d(a0, b0) + (d(a0, b1) + d(a1, b0)) + (d(a0, b2) + d(a1, b1) + d(a2, b0))


def _iota(shape, axis):
    return lax.broadcasted_iota(jnp.int32, shape, axis)


def _head_ones():
    r = _iota((BRANCH_W, BRANCH_W), 0) // HEAD_W
    c = _iota((BRANCH_W, BRANCH_W), 1) // HEAD_W
    return (r == c).astype(BF16)


def _head_mask(l):
    r = _iota((HEADS * l, BRANCH_W), 0) // l
    c = _iota((HEADS * l, BRANCH_W), 1) // HEAD_W
    return r == c


def _expand(x, mask):
    return jnp.where(mask, jnp.concatenate([x] * HEADS, axis=0), 0.0)


def _tri_masks(l):
    i = _iota((l, HEADS * l), 0)
    j = _iota((l, HEADS * l), 1) % l
    return j < i, j <= i


def _same_head(l):
    n = HEADS * l
    return (_iota((n, n), 0) // l) == (_iota((n, n), 1) // l)


def _expand_sq(x, same):
    return jnp.where(same, jnp.concatenate([x] * HEADS, axis=0), 0.0)


def _unit_lower_inverse(a_strict, l):
    i = _iota((l, HEADS * l), 0)
    j = _iota((l, HEADS * l), 1) % l
    eye = (i == j).astype(F32)
    same = _same_head(l)
    p = [-a for a in a_strict]
    t = [eye + x for x in p]
    k = 2
    while k < l:
        p = [_dot(x, _expand_sq(x, same)) for x in p]
        t = [y + _dot(y, _expand_sq(x, same)) for y, x in zip(t, p)]
        k *= 2
    return t


def _dot2(a, b_exact):
    hi = a.astype(BF16)
    lo = (a - hi.astype(F32)).astype(BF16)
    b = b_exact.astype(BF16)
    return jnp.dot(hi, b, preferred_element_type=F32) + jnp.dot(lo, b, preferred_element_type=F32)


def _cumsum_rows(x, l, nb=1):
    n = nb * l
    i, j = _iota((n, n), 0), _iota((n, n), 1)
    tri = ((j <= i) & ((i // l) == (j // l))).astype(BF16)
    return _dot3_left(tri, x)


def _softplus(x):
    return jnp.maximum(x, 0.0) + jnp.log(1.0 + jnp.exp(-jnp.abs(x)))


def _silu(x):
    return x * jax.nn.sigmoid(x)


def _norm_proj_kernel(x_ref, nw_ref, w_ref, o_ref, h_ref):
    @pl.when(pl.program_id(1) == 0)
    def _():
        x = x_ref[...]
        ms = jnp.mean(x * x, axis=-1, keepdims=True)
        h_ref[...] = (x * lax.rsqrt(ms + NORM_EPS) * nw_ref[...]).astype(BF16)

    o_ref[...] = jnp.dot(h_ref[...], w_ref[...], preferred_element_type=F32)


def _norm_proj(x2, norm_w, w_bf16, tm, tn):
    n = x2.shape[0]
    ncol = w_bf16.shape[1]
    return pl.pallas_call(
        _norm_proj_kernel,
        grid=(n // tm, ncol // tn),
        in_specs=[pl.BlockSpec((tm, D_MODEL), lambda i, j: (i, 0)),
                  pl.BlockSpec((1, D_MODEL), lambda i, j: (0, 0)),
                  pl.BlockSpec((D_MODEL, tn), lambda i, j: (0, j))],
        out_specs=pl.BlockSpec((tm, tn), lambda i, j: (i, j)),
        out_shape=jax.ShapeDtypeStruct((n, ncol), F32),
        scratch_shapes=[pltpu.VMEM((tm, D_MODEL), BF16)],
        compiler_params=_cparams(("parallel", "arbitrary")),
    )(x2, norm_w.reshape(1, D_MODEL), w_bf16)


def _s5_kernel(u_ref, h0_ref, wb_ref, wc_ref, a2k_ref, apow_ref, d_ref, wg_ref, bg_ref,
               y_ref, hl_ref, hs_ref, hb_ref, *, tc):
    c = pl.program_id(1)

    @pl.when(c == 0)
    def _():
        hs_ref[...] = h0_ref[...]

    u = u_ref[...]
    x = _dot(u, wb_ref[...])
    xr, xi = x[:, :S5_W], x[:, S5_W:]
    ng = tc // S5_ROWS
    xr = xr.reshape(ng, S5_ROWS, S5_W)
    xi = xi.reshape(ng, S5_ROWS, S5_W)
    row = _iota((S5_ROWS, 1), 0)
    k, d = 0, 1
    while d < S5_ROWS:
        m = row >= d
        ar = jnp.where(m, a2k_ref[k:k + 1, :S5_W], 0.0)
        ai = jnp.where(m, a2k_ref[k:k + 1, S5_W:], 0.0)
        sr = pltpu.roll(xr, d, axis=1)
        si = pltpu.roll(xi, d, axis=1)
        xr, xi = xr + (ar * sr - ai * si), xi + (ar * si + ai * sr)
        k, d = k + 1, d * 2
    hb_ref[:, :S5_W] = xr.reshape(tc, S5_W)
    hb_ref[:, S5_W:] = xi.reshape(tc, S5_W)
    pr, pi_ = apow_ref[:, :S5_W], apow_ref[:, S5_W:]

    def group(gi, carry):
        cr, ci = carry
        rows = pl.ds(pl.multiple_of(gi * S5_ROWS, S5_ROWS), S5_ROWS)
        hr = hb_ref[rows, :S5_W] + (pr * cr - pi_ * ci)
        hi = hb_ref[rows, S5_W:] + (pr * ci + pi_ * cr)
        hb_ref[rows, :S5_W] = hr
        hb_ref[rows, S5_W:] = hi
        return hr[S5_ROWS - 1:S5_ROWS], hi[S5_ROWS - 1:S5_ROWS]

    cr, ci = lax.fori_loop(0, tc // S5_ROWS, group, (hs_ref[:, :S5_W], hs_ref[:, S5_W:]), unroll=4)
    hs_ref[:, :S5_W] = cr
    hs_ref[:, S5_W:] = ci
    y = _dot(hb_ref[:, :S5_W], wc_ref[:S5_W]) + _dot(hb_ref[:, S5_W:], wc_ref[S5_W:])
    y = jax.nn.gelu(y + d_ref[...] * u)
    y = y * jax.nn.sigmoid(_dot(y, wg_ref[...]) + bg_ref[...])
    y_ref[...] = y.astype(y_ref.dtype)

    @pl.when(c == pl.num_programs(1) - 1)
    def _():
        hl_ref[...] = hs_ref[...]


def _s5_tables(lam_re, lam_im, log_dt, b_c, c_c, tc):
    dt = jnp.exp(log_dt)[:, None]
    mag = jnp.exp(lam_re * dt)
    ab_re, ab_im = mag * jnp.cos(lam_im * dt), mag * jnp.sin(lam_im * dt)
    den = lam_re * lam_re + lam_im * lam_im
    nr = ab_re - 1.0
    cf_re = (nr * lam_re + ab_im * lam_im) / den
    cf_im = (ab_im * lam_re - nr * lam_im) / den
    br, bi = b_c[..., 0], b_c[..., 1]
    bb_re = cf_re[..., None] * br - cf_im[..., None] * bi
    bb_im = cf_re[..., None] * bi + cf_im[..., None] * br
    eye = jnp.eye(S5_GROUPS, dtype=F32)
    bd_in = lambda m: jnp.einsum('gph,gk->ghkp', m, eye).reshape(BRANCH_W, S5_W)
    wb = jnp.concatenate([bd_in(bb_re), bd_in(bb_im)], axis=1)
    cr, ci = c_c[..., 0], c_c[..., 1]
    bd_out = lambda m: jnp.einsum('ghp,gk->gpkh', m, eye).reshape(S5_W, BRANCH_W)
    wc = jnp.concatenate([bd_out(cr), -bd_out(ci)], axis=0)
    pr, pi_ = ab_re.reshape(1, S5_W), ab_im.reshape(1, S5_W)
    lv_r, lv_i = [], []
    tr, ti = pr, pi_
    d = 1
    while d < tc:
        lv_r.append(pr)
        lv_i.append(pi_)
        tr, ti = (jnp.concatenate([tr, tr * pr - ti * pi_], axis=0),
                  jnp.concatenate([ti, tr * pi_ + ti * pr], axis=0))
        pr, pi_ = pr * pr - pi_ * pi_, 2.0 * pr * pi_
        d *= 2
    n_lv = len(lv_r)
    pad = (-n_lv) % 8
    a2k = jnp.concatenate([jnp.concatenate(lv_r, axis=0), jnp.concatenate(lv_i, axis=0)], axis=1)
    a2k = jnp.pad(a2k, ((0, pad), (0, 0)))
    apow = jnp.concatenate([tr, ti], axis=1)
    return wb.astype(BF16), wc.astype(BF16), a2k, apow


def _s5_mixer(p3, h0, lam_re, lam_im, log_dt, b_c, c_c, d_skip, w_glu, b_glu, tc):
    bsz, t, _ = p3.shape
    wb, wc, a2k, apow = _s5_tables(lam_re, lam_im, log_dt, b_c, c_c, S5_ROWS)
    h0f = jnp.concatenate([h0[..., 0].reshape(bsz, 1, S5_W), h0[..., 1].reshape(bsz, 1, S5_W)], axis=-1)
    full = lambda a: pl.BlockSpec(a.shape, lambda b, c: (0,) * a.ndim)
    d2, bg2, wg = d_skip.reshape(1, BRANCH_W), b_glu.reshape(1, BRANCH_W), w_glu.astype(BF16)
    y, hl = pl.pallas_call(
        functools.partial(_s5_kernel, tc=tc),
        grid=(bsz, t // tc),
        in_specs=[pl.BlockSpec((None, tc, BRANCH_W), lambda b, c: (b, c, COL_S5 // BRANCH_W)),
                  pl.BlockSpec((None, 1, 2 * S5_W), lambda b, c: (b, 0, 0)),
                  full(wb), full(wc), full(a2k), full(apow), full(d2), full(wg), full(bg2)],
        out_specs=[pl.BlockSpec((None, tc, BRANCH_W), lambda b, c: (b, c, 0)),
                   pl.BlockSpec((None, 1, 2 * S5_W), lambda b, c: (b, 0, 0))],
        out_shape=[jax.ShapeDtypeStruct((bsz, t, BRANCH_W), BF16),
                   jax.ShapeDtypeStruct((bsz, 1, 2 * S5_W), F32)],
        scratch_shapes=[pltpu.VMEM((1, 2 * S5_W), F32), pltpu.VMEM((tc, 2 * S5_W), F32)],
        compiler_params=_cparams(("parallel", "arbitrary")),
    )(p3, h0f, wb, wc, a2k, apow, d2, wg, bg2)
    h_last = jnp.stack([hl[:, 0, :S5_W].reshape(bsz, S5_GROUPS, S5_STATE),
                        hl[:, 0, S5_W:].reshape(bsz, S5_GROUPS, S5_STATE)], axis=-1)
    return y, h_last


def _rwkv_kernel(z_ref, sh0_ref, s0_ref, mu_ref, w0_ref, w2_ref, a0_ref, a2_ref, g2_ref,
                 kk_ref, ka_ref, rk_ref, lnw_ref, lnb_ref,
                 y_ref, sl_ref, st_ref, zp_ref, *, l, bb):
    c = pl.program_id(1)

    @pl.when(c == 0)
    def _():
        st_ref[...] = s0_ref[...]
        zp_ref[...] = sh0_ref[...]

    row = _iota((l, 1), 0)
    zms = []
    for b in range(bb):
        z = z_ref[b]
        prev = jnp.where(row == 0, zp_ref[b], pltpu.roll(z, 1, axis=0))
        zp_ref[b] = z[l - 1:l]
        zms.append(z + (prev - z) * mu_ref[...])
    zm = jnp.concatenate(zms, axis=0)
    r, k, v = zm[:, 0:256], zm[:, 256:512], zm[:, 512:768]
    lo = zm[:, 768:896]
    g_lo = zm[:, 896:1024]
    w_log = -_softplus(-(w0_ref[...] + _dot(jnp.tanh(lo), w2_ref[...]))) - 0.5
    lw = -jnp.exp(w_log)
    a = jax.nn.sigmoid(a0_ref[...] + _dot(lo, a2_ref[...]))
    g = _dot(jax.nn.sigmoid(g_lo), g2_ref[...])
    ones_h = _head_ones()
    kk = k * kk_ref[...]
    kk = kk * lax.rsqrt(_dot2(kk * kk, ones_h) + NORM_EPS)
    k = k * (1.0 + (a - 1.0) * ka_ref[...])
    kka = kk * a

    cum = _cumsum_rows(lw, l, bb)
    p_incl = jnp.exp(cum)
    p_inv = jnp.exp(-cum)
    kt = kk * jnp.exp(cum - lw)
    rt = r * p_incl
    kh = k * p_inv
    ah = kka * p_inv

    hm = _head_mask(l)
    strict, incl = _tri_masks(l)
    streams = range(bb)
    rows = [slice(b * l, (b + 1) * l) for b in streams]
    ex = lambda x: [_expand(x[s], hm) for s in rows]
    kt_b, rt_b, v_b = [kt[s] for s in rows], [rt[s] for s in rows], [v[s] for s in rows]
    kh_e, ah_e, v_e = ex(kh), ex(ah), ex(v)
    a_aa = [jnp.where(strict, _dot_nt(kt_b[b], ah_e[b]), 0.0) for b in streams]
    a_ak = [jnp.where(strict, _dot_nt(kt_b[b], kh_e[b]), 0.0) for b in streams]
    b_ra = [jnp.where(incl, _dot_nt(rt_b[b], ah_e[b]), 0.0) for b in streams]
    b_rk = [jnp.where(incl, _dot_nt(rt_b[b], kh_e[b]), 0.0) for b in streams]
    t_inv = _unit_lower_inverse(a_aa, l)
    st = [st_ref[b] for b in streams]
    rhs = [_dot_nt(kt_b[b], st[b]) + _dot(a_ak[b], v_e[b]) for b in streams]
    yb = [_dot_nt(rt_b[b], st[b]) + _dot(b_rk[b], v_e[b]) for b in streams]
    u = [_dot(t_inv[b], _expand(rhs[b], hm)) for b in streams]
    yb = [yb[b] - _dot(b_ra[b], _expand(u[b], hm)) for b in streams]
    to_end = [jnp.exp(cum[s][l - 1:l] - cum[s]) for s in rows]
    lhs_t = [jnp.concatenate([v_b[b], -u[b]], axis=0).T for b in streams]
    rhs_k = [jnp.concatenate([k[rows[b]] * to_end[b], kka[rows[b]] * to_end[b]], axis=0) for b in streams]
    head_blk = _same_head(HEAD_W)
    for b in streams:
        p_last = p_incl[(b + 1) * l - 1:(b + 1) * l]
        st_ref[b] = st[b] * p_last + jnp.where(head_blk, _dot(lhs_t[b], rhs_k[b]), 0.0)
    y = jnp.concatenate(yb, axis=0)

    inv_w = 1.0 / HEAD_W
    mean = _dot2(y, ones_h) * inv_w
    yc = y - mean
    var = _dot2(yc * yc, ones_h) * inv_w
    y = yc * lax.rsqrt(var + RW_EPS) * lnw_ref[...] + lnb_ref[...]
    bonus = _dot2(r * k * rk_ref[...], ones_h) * v
    y = ((y + bonus) * g).astype(y_ref.dtype)
    for b in range(bb):
        y_ref[b] = y[b * l:(b + 1) * l]

    @pl.when(c == pl.num_programs(1) - 1)
    def _():
        sl_ref[...] = st_ref[...]


STACK_ROWS = 256


def _streams_per_step(bsz, l):
    bb = max(1, min(bsz, STACK_ROWS // l))
    while bsz % bb:
        bb -= 1
    return bb


def _block_diag_heads(s):
    bsz = s.shape[0]
    eye = jnp.eye(HEADS, dtype=s.dtype)
    return jnp.einsum('bhij,hg->bhigj', s, eye).reshape(bsz, BRANCH_W, BRANCH_W)


def _diag_blocks(s):
    bsz = s.shape[0]
    s5 = s.reshape(bsz, HEADS, HEAD_W, HEADS, HEAD_W)
    return jnp.stack([s5[:, h, :, h, :] for h in range(HEADS)], axis=1)


def _rwkv_mixer(p3, shift0, s0, mu, w0, w2, a0, a2, g2, k_k, k_a, r_k, ln_w, ln_b, l):
    bsz, t, _ = p3.shape
    row = lambda a: a.reshape(1, -1)
    w2p = jnp.concatenate([w2, jnp.zeros_like(w2)], axis=0).astype(BF16)
    a2p = jnp.concatenate([jnp.zeros_like(a2), a2], axis=0).astype(BF16)
    args = (shift0.reshape(bsz, 1, RW_COLS), _block_diag_heads(s0), row(mu), row(w0), w2p, row(a0), a2p,
            g2.astype(BF16), row(k_k), row(k_a), row(r_k), row(ln_w), row(ln_b))
    full = lambda a: pl.BlockSpec(a.shape, lambda b, c: (0,) * a.ndim)
    bb = _streams_per_step(bsz, l)
    y, sl = pl.pallas_call(
        functools.partial(_rwkv_kernel, l=l, bb=bb),
        grid=(bsz // bb, t // l),
        in_specs=[pl.BlockSpec((bb, l, RW_COLS), lambda b, c: (b, c, COL_RW // RW_COLS)),
                  pl.BlockSpec((bb, 1, RW_COLS), lambda b, c: (b, 0, 0)),
                  pl.BlockSpec((bb, BRANCH_W, BRANCH_W), lambda b, c: (b, 0, 0))]
                 + [full(a) for a in args[2:]],
        out_specs=[pl.BlockSpec((bb, l, BRANCH_W), lambda b, c: (b, c, 0)),
                   pl.BlockSpec((bb, BRANCH_W, BRANCH_W), lambda b, c: (b, 0, 0))],
        out_shape=[jax.ShapeDtypeStruct((bsz, t, BRANCH_W), BF16),
                   jax.ShapeDtypeStruct((bsz, BRANCH_W, BRANCH_W), F32)],
        scratch_shapes=[pltpu.VMEM((bb, BRANCH_W, BRANCH_W), F32), pltpu.VMEM((bb, 1, RW_COLS), F32)],
        compiler_params=_cparams(("parallel", "arbitrary")),
    )(p3, *args)
    return y, _diag_blocks(sl)


def _sgu_kernel(z_ref, lnw_ref, lnb_ref, wm_ref, bias_ref, o_ref, v_ref, *, l, nc):
    zg = jax.nn.gelu(z_ref[...])
    u, v = zg[:, :BRANCH_W], zg[:, BRANCH_W:]
    mean = jnp.mean(v, axis=-1, keepdims=True)
    vc = v - mean
    var = jnp.mean(vc * vc, axis=-1, keepdims=True)
    v = vc * lax.rsqrt(var + NORM_EPS) * lnw_ref[...] + lnb_ref[...]
    v_ref[...] = v
    hm = _head_mask(l)
    wm = wm_ref[...]
    for i in range(nc):
        rows = slice(i * l, (i + 1) * l)
        mixed = bias_ref[...] + _dot(wm, _expand(v[rows], hm))
        o_ref[rows, :] = (u[rows] * mixed).astype(o_ref.dtype)


def _sgu_mixer(p3, ln_w, ln_b, w_s, b_s):
    bsz, t, _ = p3.shape
    l = min(SG_CHUNK, t)
    nc = max(1, min(SG_ROWS, t) // l)
    tril = jnp.tril(jnp.ones((l, l), F32))
    wm = jnp.transpose(w_s[:, :l, :l] * tril, (1, 0, 2)).reshape(l, HEADS * l).astype(BF16)
    bias = jnp.repeat(jnp.transpose(b_s[:, :l]), HEAD_W, axis=1)
    row = lambda a: a.reshape(1, -1)
    full = lambda a: pl.BlockSpec(a.shape, lambda b, c: (0,) * a.ndim)
    args = (row(ln_w), row(ln_b), wm, bias)
    return pl.pallas_call(
        functools.partial(_sgu_kernel, l=l, nc=nc),
        grid=(bsz, t // (nc * l)),
        in_specs=[pl.BlockSpec((None, nc * l, 2 * BRANCH_W), lambda b, c: (b, c, COL_SG // (2 * BRANCH_W)))]
                 + [full(a) for a in args],
        out_specs=[pl.BlockSpec((None, nc * l, BRANCH_W), lambda b, c: (b, c, 0)),
                   pl.BlockSpec((None, nc * l, BRANCH_W), lambda b, c: (b, c, 0))],
        out_shape=[jax.ShapeDtypeStruct((bsz, t, BRANCH_W), BF16),
                   jax.ShapeDtypeStruct((bsz, t, BRANCH_W), F32)],
        compiler_params=_cparams(("parallel", "parallel")),
    )(p3, *args)


def _gdn_kernel(z_ref, ab_ref, cv0_ref, s0_ref, cw_ref, alog_ref, dtb_ref, nw_ref,
                y_ref, sl_ref, st_ref, cv_ref, *, l, bb):
    c = pl.program_id(1)

    @pl.when(c == 0)
    def _():
        st_ref[...] = s0_ref[...]
        cv_ref[...] = cv0_ref[...]

    row8 = _iota((8, 1), 0)
    convs, gates = [], []
    for b in range(bb):
        z = z_ref[b]
        qkv = z[:, :GD_QKV]
        gates.append(z[:, GD_QKV:])
        carry = cv_ref[b]
        cv_ref[b] = qkv[l - 8:l]
        conv = qkv * cw_ref[GD_CONV - 1:GD_CONV]
        for j in range(1, GD_CONV):
            sh = pltpu.roll(qkv, j, axis=0)
            top = jnp.where(row8 < j, pltpu.roll(carry, j, axis=0), sh[:8])
            sh = jnp.concatenate([top, sh[8:]], axis=0) if l > 8 else top
            conv = conv + sh * cw_ref[GD_CONV - 1 - j:GD_CONV - j]
        convs.append(conv)
    conv = _silu(jnp.concatenate(convs, axis=0))
    gate = jnp.concatenate(gates, axis=0)
    q, k, v = conv[:, :256], conv[:, 256:512], conv[:, 512:768]
    ones_h = _head_ones()
    q = q * lax.rsqrt(_dot2(q * q, ones_h) + NORM_EPS) * (HEAD_W ** -0.5)
    k = k * lax.rsqrt(_dot2(k * k, ones_h) + NORM_EPS)
    ab = jnp.concatenate([ab_ref[b] for b in range(bb)], axis=0)
    lane_h = _iota((bb * l, BRANCH_W), 1) // HEAD_W
    a_in = jnp.zeros((bb * l, BRANCH_W), F32)
    b_in = jnp.zeros((bb * l, BRANCH_W), F32)
    for h in range(HEADS):
        a_in = jnp.where(lane_h == h, ab[:, h:h + 1], a_in)
        b_in = jnp.where(lane_h == h, ab[:, HEADS + h:HEADS + h + 1], b_in)
    beta = jax.nn.sigmoid(b_in)
    g = -jnp.exp(alog_ref[...]) * _softplus(a_in + dtb_ref[...])
    gc = _cumsum_rows(g, l, bb)
    eg = jnp.exp(gc)
    kb = k * beta
    vb = v * beta
    kbg = kb * eg
    qg = q * eg

    hm = _head_mask(l)
    strict, incl = _tri_masks(l)
    n = HEADS * l
    streams = range(bb)
    rows = [slice(b * l, (b + 1) * l) for b in streams]
    ex = lambda x: [_expand(x[s], hm) for s in rows]
    k_e, vb_e, kbg_e = ex(k), ex(vb), ex(kbg)
    lane_hd = _iota((l, n), 1) // l
    eye = _iota((l, n), 0) == (_iota((l, n), 1) % l)
    decay = []
    for s in rows:
        gi = jnp.zeros((l, n), F32)
        for h in range(HEADS):
            gi = jnp.where(lane_hd == h, gc[s][:, h * HEAD_W:h * HEAD_W + 1], gi)
        gj = jnp.sum(jnp.where(eye, gi, 0.0), axis=0, keepdims=True)
        decay.append(jnp.where(incl, jnp.exp(jnp.where(incl, gi - gj, 0.0)), 0.0))
    lm = [jnp.where(strict, _dot_nt(kb[rows[b]], k_e[b]) * decay[b], 0.0) for b in streams]
    qk = [_dot_nt(q[rows[b]], k_e[b]) * decay[b] for b in streams]
    t_inv = _unit_lower_inverse(lm, l)
    uc = [_dot(t_inv[b], vb_e[b]) for b in streams]
    wc = [_dot(t_inv[b], kbg_e[b]) for b in streams]
    st = [st_ref[b] for b in streams]
    ob = [_dot(qg[rows[b]], st[b]) for b in streams]
    v_new = [uc[b] - _dot(wc[b], st[b]) for b in streams]
    ob = [ob[b] + _dot(qk[b], _expand(v_new[b], hm)) for b in streams]
    g_last = [gc[s][l - 1:l] for s in rows]
    k_dec = [(k[rows[b]] * jnp.exp(g_last[b] - gc[rows[b]])).T for b in streams]
    head_blk = _same_head(HEAD_W)
    for b in streams:
        st_ref[b] = st[b] * jnp.exp(g_last[b]) + jnp.where(head_blk, _dot(k_dec[b], v_new[b]), 0.0)
    o = jnp.concatenate(ob, axis=0)
    ms = _dot2(o * o, ones_h) * (1.0 / HEAD_W)
    o = (o * lax.rsqrt(ms + NORM_EPS) * nw_ref[...] * _silu(gate)).astype(y_ref.dtype)
    for b in range(bb):
        y_ref[b] = o[b * l:(b + 1) * l]

    @pl.when(c == pl.num_programs(1) - 1)
    def _():
        sl_ref[...] = st_ref[...]


def _gdn_mixer(p3, conv0, s0, conv_w, a_log, dt_bias, norm_w, l):
    bsz, t, _ = p3.shape
    cv0 = jnp.pad(conv0, ((0, 0), (8 - (GD_CONV - 1), 0), (0, 0)))
    cw = jnp.pad(conv_w, ((0, 8 - GD_CONV), (0, 0)))
    per_head = lambda a: jnp.repeat(a, HEAD_W).reshape(1, BRANCH_W)
    args = (cv0, _block_diag_heads(s0), cw, per_head(a_log), per_head(dt_bias),
            jnp.tile(norm_w, HEADS).reshape(1, BRANCH_W))
    full = lambda a: pl.BlockSpec(a.shape, lambda b, c: (0,) * a.ndim)
    bb = _streams_per_step(bsz, l)
    y, sl = pl.pallas_call(
        functools.partial(_gdn_kernel, l=l, bb=bb),
        grid=(bsz // bb, t // l),
        in_specs=[pl.BlockSpec((bb, l, 1024), lambda b, c: (b, c, COL_GD // 1024)),
                  pl.BlockSpec((bb, l, LANES), lambda b, c: (b, c, COL_AB // LANES)),
                  pl.BlockSpec((bb, 8, GD_QKV), lambda b, c: (b, 0, 0)),
                  pl.BlockSpec((bb, BRANCH_W, BRANCH_W), lambda b, c: (b, 0, 0))]
                 + [full(a) for a in args[2:]],
        out_specs=[pl.BlockSpec((bb, l, BRANCH_W), lambda b, c: (b, c, 0)),
                   pl.BlockSpec((bb, BRANCH_W, BRANCH_W), lambda b, c: (b, 0, 0))],
        out_shape=[jax.ShapeDtypeStruct((bsz, t, BRANCH_W), BF16),
                   jax.ShapeDtypeStruct((bsz, BRANCH_W, BRANCH_W), F32)],
        scratch_shapes=[pltpu.VMEM((bb, BRANCH_W, BRANCH_W), F32), pltpu.VMEM((bb, 8, GD_QKV), F32)],
        compiler_params=_cparams(("parallel", "arbitrary")),
    )(p3, p3, *args)
    return y, _diag_blocks(sl)


def _merge_kernel(ya_ref, yb_ref, yc_ref, yd_ref, x_ref, nw_ref, wg_ref, wbr_ref, wout_ref, o_ref):
    x = x_ref[...]
    ms = jnp.mean(x * x, axis=-1, keepdims=True)
    h = (x * lax.rsqrt(ms + NORM_EPS) * nw_ref[...]).astype(BF16)
    m = None
    for b, y_ref in enumerate((ya_ref, yb_ref, yc_ref, yd_ref)):
        gate = jnp.dot(h, wg_ref[:, b * D_MODEL:(b + 1) * D_MODEL], preferred_element_type=F32)
        br = jnp.dot(y_ref[...], wbr_ref[b], preferred_element_type=F32)
        term = (0.5 * jnp.tanh(0.5 * gate) + 0.5) * br
        m = term if m is None else m + term
    o_ref[...] = x + jnp.dot(m.astype(BF16), wout_ref[...], preferred_element_type=F32)


def _merge(ys, x2, norm_w, w_gate, w_branch, w_out, tm):
    n = x2.shape[0]
    yspec = pl.BlockSpec((tm, BRANCH_W), lambda i: (i, 0))
    once = pl.Buffered(1)
    return pl.pallas_call(
        _merge_kernel,
        grid=(n // tm,),
        in_specs=[yspec, yspec, yspec, yspec,
                  pl.BlockSpec((tm, D_MODEL), lambda i: (i, 0)),
                  pl.BlockSpec((1, D_MODEL), lambda i: (0, 0)),
                  pl.BlockSpec((D_MODEL, 4 * D_MODEL), lambda i: (0, 0), pipeline_mode=once),
                  pl.BlockSpec((4, BRANCH_W, D_MODEL), lambda i: (0, 0, 0), pipeline_mode=once),
                  pl.BlockSpec((D_MODEL, D_MODEL), lambda i: (0, 0), pipeline_mode=once)],
        out_specs=pl.BlockSpec((tm, D_MODEL), lambda i: (i, 0)),
        out_shape=jax.ShapeDtypeStruct((n, D_MODEL), F32),
        compiler_params=_cparams(("parallel",)),
    )(*[y.reshape(n, BRANCH_W) for y in ys], x2, norm_w.reshape(1, D_MODEL), w_gate,
      w_branch.astype(BF16), w_out.astype(BF16))


def _ffn_kernel(x_ref, nw_ref, w1_ref, w3_ref, w2_ref, o_ref, h_ref):
    j = pl.program_id(1)

    @pl.when(j == 0)
    def _():
        x = x_ref[...]
        ms = jnp.mean(x * x, axis=-1, keepdims=True)
        h_ref[...] = (x * lax.rsqrt(ms + NORM_EPS) * nw_ref[...]).astype(BF16)
        o_ref[...] = x

    h = h_ref[...]
    a = _silu(jnp.dot(h, w1_ref[...], preferred_element_type=F32)) * jnp.dot(h, w3_ref[...], preferred_element_type=F32)
    o_ref[...] += jnp.dot(a.astype(BF16), w2_ref[...], preferred_element_type=F32)


def _ffn(x2, norm_w, w1, w3, w2, tm, tf):
    n = x2.shape[0]
    dff = w1.shape[1]
    return pl.pallas_call(
        _ffn_kernel,
        grid=(n // tm, dff // tf),
        in_specs=[pl.BlockSpec((tm, D_MODEL), lambda i, j: (i, 0)),
                  pl.BlockSpec((1, D_MODEL), lambda i, j: (0, 0)),
                  pl.BlockSpec((D_MODEL, tf), lambda i, j: (0, j)),
                  pl.BlockSpec((D_MODEL, tf), lambda i, j: (0, j)),
                  pl.BlockSpec((tf, D_MODEL), lambda i, j: (j, 0))],
        out_specs=pl.BlockSpec((tm, D_MODEL), lambda i, j: (i, 0)),
        out_shape=jax.ShapeDtypeStruct((n, D_MODEL), F32),
        scratch_shapes=[pltpu.VMEM((tm, D_MODEL), BF16)],
        compiler_params=_cparams(("parallel", "arbitrary")),
    )(x2, norm_w.reshape(1, D_MODEL), w1.astype(BF16), w3.astype(BF16), w2.astype(BF16))


MOE_SUB = 256
MOE_TILE = 2048
MOE_PLACE_ROWS = 1024


def _router_kernel(x_ref, nw_ref, wr_ref, br_ref, h_ref, rcol_ref, gcol_ref, rrow_ref, cnt_ref, seen_ref,
                   *, tm, k):
    x = x_ref[...]
    ms = jnp.mean(x * x, axis=-1, keepdims=True)
    h = x * lax.rsqrt(ms + NORM_EPS) * nw_ref[...]
    h_ref[...] = h.astype(BF16)
    lane = _iota((tm, LANES), 1)
    logits = jnp.where(lane < N_EXPERTS, _dot_hp(h, wr_ref[...]) + br_ref[...], -jnp.inf)
    m1 = jnp.max(logits, axis=1, keepdims=True)
    i1 = jnp.min(jnp.where(logits == m1, lane, LANES), axis=1, keepdims=True)
    rest = jnp.where(lane == i1, -jnp.inf, logits)
    m2 = jnp.max(rest, axis=1, keepdims=True)
    i2 = jnp.min(jnp.where(rest == m2, lane, LANES), axis=1, keepdims=True)
    e2 = jnp.exp(m2 - m1)
    g1 = 1.0 / (1.0 + e2)
    g2 = e2 / (1.0 + e2)
    sel = (lane == i1) | (lane == i2)
    self32 = sel.astype(F32)
    gcol_ref[...] = jnp.where(lane == i1, g1, 0.0) + jnp.where(lane == i2, g2, 0.0)
    @pl.when(pl.program_id(0) % k == 0)
    def _():
        seen_ref[...] = jnp.zeros_like(seen_ref)

    seen = seen_ref[...]
    tri = (_iota((tm, tm), 1) < _iota((tm, tm), 0)).astype(BF16)
    rank = jnp.dot(tri, self32.astype(BF16), preferred_element_type=F32) + seen
    r = jnp.where(sel, rank, -1.0)
    rcol_ref[...] = r
    rrow_ref[...] = r.T[:N_EXPERTS]
    seen = seen + jnp.sum(self32, axis=0, keepdims=True)
    seen_ref[...] = seen
    cnt_ref[...] = jnp.broadcast_to(seen, (8, LANES))


def _router(x2, norm_w, w_router, b_router, tm, k):
    n = x2.shape[0]
    nt = n // (tm * k)
    wr = jnp.pad(w_router, ((0, 0), (0, LANES - N_EXPERTS)))
    br = jnp.pad(b_router, (0, LANES - N_EXPERTS)).reshape(1, LANES)
    return pl.pallas_call(
        functools.partial(_router_kernel, tm=tm, k=k),
        grid=(n // tm,),
        in_specs=[pl.BlockSpec((tm, D_MODEL), lambda i: (i, 0)),
                  pl.BlockSpec((1, D_MODEL), lambda i: (0, 0)),
                  pl.BlockSpec((D_MODEL, LANES), lambda i: (0, 0)),
                  pl.BlockSpec((1, LANES), lambda i: (0, 0))],
        out_specs=[pl.BlockSpec((tm, D_MODEL), lambda i: (i, 0)),
                   pl.BlockSpec((tm, LANES), lambda i: (i, 0)),
                   pl.BlockSpec((tm, LANES), lambda i: (i, 0)),
                   pl.BlockSpec((None, N_EXPERTS, tm), lambda i: (i // k, 0, i % k)),
                   pl.BlockSpec((None, 8, LANES), lambda i: (i // k, 0, 0))],
        out_shape=[jax.ShapeDtypeStruct((n, D_MODEL), BF16),
                   jax.ShapeDtypeStruct((n, LANES), F32),
                   jax.ShapeDtypeStruct((n, LANES), F32),
                   jax.ShapeDtypeStruct((nt, N_EXPERTS, tm * k), F32),
                   jax.ShapeDtypeStruct((nt, 8, LANES), F32)],
        scratch_shapes=[pltpu.VMEM((1, LANES), F32)],
        compiler_params=_cparams(("arbitrary",)),
    )(x2, norm_w.reshape(1, D_MODEL), wr, br)


def _moe_kernel(cnt_ref, h_ref, rrow_ref, rcol_ref, gcol_ref, w1_ref, w3_ref, w2_ref, o_ref,
                xs_ref, acc_ref, *, tm):
    i, e, c = pl.program_id(0), pl.program_id(1), pl.program_id(2)
    n_sub = (cnt_ref[i * N_EXPERTS + e] + (MOE_SUB - 1)) // MOE_SUB

    @pl.when((e == 0) & (c == 0))
    def _():
        o_ref[...] = jnp.zeros_like(o_ref)

    @pl.when(c == 0)
    def _():
        rrow = rrow_ref[pl.ds(e, 1), :]
        slot = _iota((MOE_SUB, 1), 0).astype(F32)

        def gather(s, carry):
            base = pl.multiple_of(s * MOE_SUB, MOE_SUB)
            pick = (rrow == slot + (s * MOE_SUB).astype(F32)).astype(BF16)
            xs_ref[pl.ds(base, MOE_SUB), :] = jnp.dot(pick, h_ref[...], preferred_element_type=F32).astype(BF16)
            acc_ref[pl.ds(base, MOE_SUB), :] = jnp.zeros((MOE_SUB, D_MODEL), F32)
            return carry

        lax.fori_loop(0, n_sub, gather, 0)

    def expert(s, carry):
        base = pl.multiple_of(s * MOE_SUB, MOE_SUB)
        xb = xs_ref[pl.ds(base, MOE_SUB), :]
        a = (_silu(jnp.dot(xb, w1_ref[...], preferred_element_type=F32))
             * jnp.dot(xb, w3_ref[...], preferred_element_type=F32))
        acc_ref[pl.ds(base, MOE_SUB), :] += jnp.dot(a.astype(BF16), w2_ref[...], preferred_element_type=F32)
        return carry

    lax.fori_loop(0, n_sub, expert, 0)

    @pl.when(c == pl.num_programs(2) - 1)
    def _():
        tp = min(tm, MOE_PLACE_ROWS)
        lane = _iota((tp, LANES), 1)
        slot = _iota((1, MOE_SUB), 1).astype(F32)

        def scatter(s, carry):
            base = pl.multiple_of(s * MOE_SUB, MOE_SUB)
            a = acc_ref[pl.ds(base, MOE_SUB), :]
            hi = a.astype(BF16)
            lo = (a - hi.astype(F32)).astype(BF16)
            for r0 in range(0, tm, tp):
                rcol = jnp.sum(jnp.where(lane == e, rcol_ref[r0:r0 + tp], 0.0), axis=1, keepdims=True)
                gate = jnp.sum(jnp.where(lane == e, gcol_ref[r0:r0 + tp], 0.0), axis=1, keepdims=True)
                place = (rcol == slot + (s * MOE_SUB).astype(F32)).astype(BF16)
                back = (jnp.dot(place, hi, preferred_element_type=F32)
                        + jnp.dot(place, lo, preferred_element_type=F32))
                o_ref[r0:r0 + tp] += gate * back
            return carry

        lax.fori_loop(0, n_sub, scatter, 0)


def _moe(h2, rrow, rcol, gcol, counts, w1, w3, w2, tm, tf):
    n = h2.shape[0]
    nt = n // tm
    dff = w1.shape[2]
    grid_spec = pltpu.PrefetchScalarGridSpec(
        num_scalar_prefetch=1,
        grid=(nt, N_EXPERTS, dff // tf),
        in_specs=[pl.BlockSpec((tm, D_MODEL), lambda i, e, c, cnt: (i, 0), pipeline_mode=pl.Buffered(1)),
                  pl.BlockSpec((None, N_EXPERTS, tm), lambda i, e, c, cnt: (i, 0, 0)),
                  pl.BlockSpec((tm, LANES), lambda i, e, c, cnt: (i, 0), pipeline_mode=pl.Buffered(1)),
                  pl.BlockSpec((tm, LANES), lambda i, e, c, cnt: (i, 0), pipeline_mode=pl.Buffered(1)),
                  pl.BlockSpec((None, D_MODEL, tf), lambda i, e, c, cnt: (e, 0, c)),
                  pl.BlockSpec((None, D_MODEL, tf), lambda i, e, c, cnt: (e, 0, c)),
                  pl.BlockSpec((None, tf, D_MODEL), lambda i, e, c, cnt: (e, c, 0))],
        out_specs=pl.BlockSpec((tm, D_MODEL), lambda i, e, c, cnt: (i, 0), pipeline_mode=pl.Buffered(1)),
        scratch_shapes=[pltpu.VMEM((tm, D_MODEL), BF16), pltpu.VMEM((tm, D_MODEL), F32)],
    )
    return pl.pallas_call(
        functools.partial(_moe_kernel, tm=tm),
        grid_spec=grid_spec,
        out_shape=jax.ShapeDtypeStruct((n, D_MODEL), F32),
        compiler_params=_cparams(("parallel", "arbitrary", "arbitrary")),
    )(counts, h2, rrow, rcol, gcol, w1, w3, w2)


def _moe_ffn(x2, norm_w, w_router, b_router, w1, w3, w2):
    n = x2.shape[0]
    tm = min(MOE_TILE, n)
    tr = min(MOE_PLACE_ROWS, tm)
    h2, rcol, gcol, rrow, cnt = _router(x2, norm_w, w_router, b_router, tr, tm // tr)
    counts = cnt[:, 0, :N_EXPERTS].astype(jnp.int32).reshape(-1)
    return _moe(h2, rrow, rcol, gcol, counts, w1.astype(BF16), w3.astype(BF16), w2.astype(BF16), tm, 512)


def _final_kernel(x_ref, m_ref, nw_ref, o_ref):
    x = x_ref[...] + m_ref[...]
    ms = jnp.mean(x * x, axis=-1, keepdims=True)
    o_ref[...] = x * lax.rsqrt(ms + NORM_EPS) * nw_ref[...]


def _final(x2, m2, norm_w, tm):
    n = x2.shape[0]
    spec = pl.BlockSpec((tm, D_MODEL), lambda i: (i, 0))
    return pl.pallas_call(
        _final_kernel,
        grid=(n // tm,),
        in_specs=[spec, spec, pl.BlockSpec((1, D_MODEL), lambda i: (0, 0))],
        out_specs=spec,
        out_shape=jax.ShapeDtypeStruct((n, D_MODEL), F32),
        compiler_params=_cparams(("parallel",)),
    )(x2, m2, norm_w.reshape(1, D_MODEL))


def _permute_w_in(w):
    s5, rw, sg = w[:, 0:256], w[:, 256:1280], w[:, 1280:1792]
    gd, ab, gates = w[:, 1792:2816], w[:, 2816:2824], w[:, 2824:6920]
    pad = jnp.zeros((D_MODEL, PROJ_COLS - COL_AB - 8), w.dtype)
    return jnp.concatenate([rw, gd, sg, s5, ab, pad], axis=1).astype(BF16), gates.astype(BF16)


def _run_trunk(x, s5_h, rw_s, rw_shift, gd_s, gd_conv, p, w_in_perm):
    bsz, t, _ = x.shape
    n = bsz * t
    tm = min(1024, n)
    l = min(64, t)
    tc_s5 = min(256, t)
    new = ([], [], [], [], [], [])
    x2 = x.reshape(n, D_MODEL)
    moe_out = None
    for layer in range(2):
        g = lambda name: p[name][layer]
        w_mix, w_gate = w_in_perm[layer]
        proj = _norm_proj(x2, g('norm1_w'), w_mix, tm, 1024)
        p3 = proj.reshape(bsz, t, PROJ_COLS)
        y_a, s5_new = _s5_mixer(p3, s5_h[layer], g('s5_lam_re'), g('s5_lam_im'), g('s5_log_dt'), g('s5_b'),
                                g('s5_c'), g('s5_d'), g('s5_w_glu'), g('s5_b_glu'), tc_s5)
        y_b, rw_new = _rwkv_mixer(p3, rw_shift[layer], rw_s[layer], g('rw_mu'), g('rw_w0'), g('rw_w2'),
                                  g('rw_a0'), g('rw_a2'), g('rw_g2'), g('rw_k_k'), g('rw_k_a'), g('rw_r_k'),
                                  g('rw_ln_w'), g('rw_ln_b'), l)
        y_c, sg_v = _sgu_mixer(p3, g('sg_ln_w'), g('sg_ln_b'), g('sg_w_s'), g('sg_b_s'))
        y_d, gd_new = _gdn_mixer(p3, gd_conv[layer], gd_s[layer], g('gd_conv_w'), g('gd_a_log'),
                                 g('gd_dt_bias'), g('gd_norm_w'), l)
        shift_new = p3[:, t - 1, COL_RW:COL_RW + RW_COLS]
        conv_new = p3[:, t - (GD_CONV - 1):, COL_GD:COL_GD + GD_QKV]
        x2 = _merge((y_a, y_b, y_c, y_d), x2, g('norm1_w'), w_gate, g('w_branch'), g('w_out'), min(512, n))
        j = layer // 2
        if layer % 2 == 0:
            x2 = _ffn(x2, g('norm2_w'), p['ffn_w1'][j], p['ffn_w3'][j], p['ffn_w2'][j], min(512, n), 1408)
        else:
            moe_out = _moe_ffn(x2, g('norm2_w'), p['moe_router'][j], p['moe_router_b'][j],
                               p['moe_w1'][j], p['moe_w3'][j], p['moe_w2'][j])
        for lst, s in zip(new, (s5_new, rw_new, shift_new, gd_new, conv_new, sg_v)):
            lst.append(s)
    y = _final(x2, moe_out, p['final_norm_w'], tm).reshape(bsz, t, D_MODEL)
    return y, [jnp.stack(lst) for lst in new]


def kernel(x_prompt, x_sample, state_s5, state_rwkv, state_rwkv_shift, state_gdn, state_gdn_conv, norm1_w, w_in, s5_lam_re, s5_lam_im, s5_log_dt, s5_b, s5_c, s5_d, s5_w_glu, s5_b_glu, rw_mu, rw_w0, rw_w2, rw_a0, rw_a2, rw_g2, rw_k_k, rw_k_a, rw_r_k, rw_ln_w, rw_ln_b, sg_ln_w, sg_ln_b, sg_w_s, sg_b_s, gd_conv_w, gd_a_log, gd_dt_bias, gd_norm_w, w_branch, w_out, norm2_w, ffn_w1, ffn_w3, ffn_w2, moe_router, moe_router_b, moe_w1, moe_w3, moe_w2, final_norm_w):
    p = {
        'norm1_w': norm1_w, 's5_lam_re': s5_lam_re, 's5_lam_im': s5_lam_im, 's5_log_dt': s5_log_dt,
        's5_b': s5_b, 's5_c': s5_c, 's5_d': s5_d, 's5_w_glu': s5_w_glu, 's5_b_glu': s5_b_glu,
        'rw_mu': rw_mu, 'rw_w0': rw_w0, 'rw_w2': rw_w2, 'rw_a0': rw_a0, 'rw_a2': rw_a2, 'rw_g2': rw_g2,
        'rw_k_k': rw_k_k, 'rw_k_a': rw_k_a, 'rw_r_k': rw_r_k, 'rw_ln_w': rw_ln_w, 'rw_ln_b': rw_ln_b,
        'sg_ln_w': sg_ln_w, 'sg_ln_b': sg_ln_b, 'sg_w_s': sg_w_s, 'sg_b_s': sg_b_s,
        'gd_conv_w': gd_conv_w, 'gd_a_log': gd_a_log, 'gd_dt_bias': gd_dt_bias, 'gd_norm_w': gd_norm_w,
        'w_branch': w_branch, 'w_out': w_out, 'norm2_w': norm2_w,
        'ffn_w1': ffn_w1, 'ffn_w3': ffn_w3, 'ffn_w2': ffn_w2,
        'moe_router': moe_router, 'moe_router_b': moe_router_b, 'moe_w1': moe_w1, 'moe_w3': moe_w3, 'moe_w2': moe_w2,
        'final_norm_w': final_norm_w,
    }
    w_in_perm = [_permute_w_in(w_in[layer]) for layer in range(2)]
    bp, dt = x_prompt.shape[0], x_prompt.dtype
    depth = w_in.shape[0]
    y_prompt, (s5_p, rw_p, rwsh_p, gd_p, gdc_p, _) = _run_trunk(
        x_prompt,
        jnp.zeros((depth, bp, S5_GROUPS, S5_STATE, 2), dt),
        jnp.zeros((depth, bp, HEADS, HEAD_W, HEAD_W), dt),
        jnp.zeros((depth, bp, RW_COLS), dt),
        jnp.zeros((depth, bp, HEADS, HEAD_W, HEAD_W), dt),
        jnp.zeros((depth, bp, GD_CONV - 1, GD_QKV), dt),
        p, w_in_perm)
    y_sample, (s5_s, rw_s, rwsh_s, gd_s, gdc_s, sgv_s) = _run_trunk(
        x_sample, state_s5, state_rwkv, state_rwkv_shift, state_gdn, state_gdn_conv, p, w_in_perm)
    return (y_prompt, y_sample, s5_p, rw_p, rwsh_p, gd_p, gdc_p, s5_s, rw_s, rwsh_s, gd_s, gdc_s, sgv_s)
```

```python
import functools

import jax
import jax.numpy as jnp
from jax import lax
from jax.experimental import pallas as pl
from jax.experimental.pallas import tpu as pltpu

F32 = jnp.float32
BF16 = jnp.bfloat16

D_MODEL = 1024
BRANCH_W = 256
HEADS = 4
HEAD_W = 64
S5_GROUPS = 16
S5_GROUP = 16
S5_STATE = 64
S5_W = S5_GROUPS * S5_STATE
S5_ROWS = 8
SG_CHUNK = 128
SG_ROWS = 512
GD_CONV = 4
GD_QKV = 3 * BRANCH_W
RW_COLS = 1024
RW_EPS = 64e-5
NORM_EPS = 1e-6
N_EXPERTS = 8
LANES = 128

COL_RW = 0
COL_GD = 1024
COL_SG = 2048
COL_S5 = 2560
COL_AB = 2816
PROJ_COLS = 3072

VMEM_LIMIT = 48 * 1024 * 1024


def _cparams(sem):
    return pltpu.CompilerParams(dimension_semantics=sem, vmem_limit_bytes=VMEM_LIMIT)


def _dot(a, b):
    return jnp.dot(a.astype(BF16), b.astype(BF16), preferred_element_type=F32)


def _dot_nt(a, b):
    return lax.dot_general(a.astype(BF16), b.astype(BF16), (((1,), (1,)), ((), ())),
                           preferred_element_type=F32)


def _split3(a):
    hi = a.astype(BF16)
    r1 = a - hi.astype(F32)
    mid = r1.astype(BF16)
    lo = (r1 - mid.astype(F32)).astype(BF16)
    return hi, mid, lo


def _dot3_left(b_exact, a):
    hi, mid, lo = _split3(a)
    b = b_exact.astype(BF16)
    return (jnp.dot(b, hi, preferred_element_type=F32) + jnp.dot(b, mid, preferred_element_type=F32)
            + jnp.dot(b, lo, preferred_element_type=F32))


def _dot_hp(a, b):
    a0, a1, a2 = _split3(a)
    b0, b1, b2 = _split3(b)
    d = lambda x, y: jnp.dot(x, y, preferred_element_type=F32)
    return d(a0, b0) + (d(a0, b1) + d(a1, b0)) + (d(a0, b2) + d(a1, b1) + d(a2, b0))


def _iota(shape, axis):
    return lax.broadcasted_iota(jnp.int32, shape, axis)


def _head_ones():
    r = _iota((BRANCH_W, BRANCH_W), 0) // HEAD_W
    c = _iota((BRANCH_W, BRANCH_W), 1) // HEAD_W
    return (r == c).astype(BF16)


def _head_mask(l):
    r = _iota((HEADS * l, BRANCH_W), 0) // l
    c = _iota((HEADS * l, BRANCH_W), 1) // HEAD_W
    return r == c


def _expand(x, mask):
    return jnp.where(mask, jnp.concatenate([x] * HEADS, axis=0), 0.0)


def _tri_masks(l):
    i = _iota((l, HEADS * l), 0)
    j = _iota((l, HEADS * l), 1) % l
    return j < i, j <= i


def _same_head(l):
    n = HEADS * l
    return (_iota((n, n), 0) // l) == (_iota((n, n), 1) // l)


def _expand_sq(x, same):
    return jnp.where(same, jnp.concatenate([x] * HEADS, axis=0), 0.0)


def _unit_lower_inverse(a_strict, l):
    i = _iota((l, HEADS * l), 0)
    j = _iota((l, HEADS * l), 1) % l
    eye = (i == j).astype(F32)
    same = _same_head(l)
    p = [-a for a in a_strict]
    t = [eye + x for x in p]
    k = 2
    while k < l:
        p = [_dot(x, _expand_sq(x, same)) for x in p]
        t = [y + _dot(y, _expand_sq(x, same)) for y, x in zip(t, p)]
        k *= 2
    return t


def _dot2(a, b_exact):
    hi = a.astype(BF16)
    lo = (a - hi.astype(F32)).astype(BF16)
    b = b_exact.astype(BF16)
    return jnp.dot(hi, b, preferred_element_type=F32) + jnp.dot(lo, b, preferred_element_type=F32)


def _cumsum_rows(x, l, nb=1):
    n = nb * l
    i, j = _iota((n, n), 0), _iota((n, n), 1)
    tri = ((j <= i) & ((i // l) == (j // l))).astype(BF16)
    return _dot3_left(tri, x)


def _softplus(x):
    return jnp.maximum(x, 0.0) + jnp.log(1.0 + jnp.exp(-jnp.abs(x)))


def _silu(x):
    return x * jax.nn.sigmoid(x)


def _norm_proj_kernel(x_ref, nw_ref, w_ref, o_ref, h_ref):
    @pl.when(pl.program_id(1) == 0)
    def _():
        x = x_ref[...]
        ms = jnp.mean(x * x, axis=-1, keepdims=True)
        h_ref[...] = (x * lax.rsqrt(ms + NORM_EPS) * nw_ref[...]).astype(BF16)

    o_ref[...] = jnp.dot(h_ref[...], w_ref[...], preferred_element_type=F32)


def _norm_proj(x2, norm_w, w_bf16, tm, tn):
    n = x2.shape[0]
    ncol = w_bf16.shape[1]
    return pl.pallas_call(
        _norm_proj_kernel,
        grid=(n // tm, ncol // tn),
        in_specs=[pl.BlockSpec((tm, D_MODEL), lambda i, j: (i, 0)),
                  pl.BlockSpec((1, D_MODEL), lambda i, j: (0, 0)),
                  pl.BlockSpec((D_MODEL, tn), lambda i, j: (0, j))],
        out_specs=pl.BlockSpec((tm, tn), lambda i, j: (i, j)),
        out_shape=jax.ShapeDtypeStruct((n, ncol), F32),
        scratch_shapes=[pltpu.VMEM((tm, D_MODEL), BF16)],
        compiler_params=_cparams(("parallel", "arbitrary")),
    )(x2, norm_w.reshape(1, D_MODEL), w_bf16)


def _s5_kernel(u_ref, h0_ref, wb_ref, wc_ref, a2k_ref, apow_ref, d_ref, wg_ref, bg_ref,
               y_ref, hl_ref, hs_ref, hb_ref, *, tc):
    c = pl.program_id(1)

    @pl.when(c == 0)
    def _():
        hs_ref[...] = h0_ref[...]

    u = u_ref[...]
    x = _dot(u, wb_ref[...])
    xr, xi = x[:, :S5_W], x[:, S5_W:]
    ng = tc // S5_ROWS
    xr = xr.reshape(ng, S5_ROWS, S5_W)
    xi = xi.reshape(ng, S5_ROWS, S5_W)
    row = _iota((S5_ROWS, 1), 0)
    k, d = 0, 1
    while d < S5_ROWS:
        m = row >= d
        ar = jnp.where(m, a2k_ref[k:k + 1, :S5_W], 0.0)
        ai = jnp.where(m, a2k_ref[k:k + 1, S5_W:], 0.0)
        sr = pltpu.roll(xr, d, axis=1)
        si = pltpu.roll(xi, d, axis=1)
        xr, xi = xr + (ar * sr - ai * si), xi + (ar * si + ai * sr)
        k, d = k + 1, d * 2
    hb_ref[:, :S5_W] = xr.reshape(tc, S5_W)
    hb_ref[:, S5_W:] = xi.reshape(tc, S5_W)
    pr, pi_ = apow_ref[:, :S5_W], apow_ref[:, S5_W:]

    def group(gi, carry):
        cr, ci = carry
        rows = pl.ds(pl.multiple_of(gi * S5_ROWS, S5_ROWS), S5_ROWS)
        hr = hb_ref[rows, :S5_W] + (pr * cr - pi_ * ci)
        hi = hb_ref[rows, S5_W:] + (pr * ci + pi_ * cr)
        hb_ref[rows, :S5_W] = hr
        hb_ref[rows, S5_W:] = hi
        return hr[S5_ROWS - 1:S5_ROWS], hi[S5_ROWS - 1:S5_ROWS]

    cr, ci = lax.fori_loop(0, tc // S5_ROWS, group, (hs_ref[:, :S5_W], hs_ref[:, S5_W:]), unroll=4)
    hs_ref[:, :S5_W] = cr
    hs_ref[:, S5_W:] = ci
    y = _dot(hb_ref[:, :S5_W], wc_ref[:S5_W]) + _dot(hb_ref[:, S5_W:], wc_ref[S5_W:])
    y = jax.nn.gelu(y + d_ref[...] * u)
    y = y * jax.nn.sigmoid(_dot(y, wg_ref[...]) + bg_ref[...])
    y_ref[...] = y.astype(y_ref.dtype)

    @pl.when(c == pl.num_programs(1) - 1)
    def _():
        hl_ref[...] = hs_ref[...]


def _s5_tables(lam_re, lam_im, log_dt, b_c, c_c, tc):
    dt = jnp.exp(log_dt)[:, None]
    mag = jnp.exp(lam_re * dt)
    ab_re, ab_im = mag * jnp.cos(lam_im * dt), mag * jnp.sin(lam_im * dt)
    den = lam_re * lam_re + lam_im * lam_im
    nr = ab_re - 1.0
    cf_re = (nr * lam_re + ab_im * lam_im) / den
    cf_im = (ab_im * lam_re - nr * lam_im) / den
    br, bi = b_c[..., 0], b_c[..., 1]
    bb_re = cf_re[..., None] * br - cf_im[..., None] * bi
    bb_im = cf_re[..., None] * bi + cf_im[..., None] * br
    eye = jnp.eye(S5_GROUPS, dtype=F32)
    bd_in = lambda m: jnp.einsum('gph,gk->ghkp', m, eye).reshape(BRANCH_W, S5_W)
    wb = jnp.concatenate([bd_in(bb_re), bd_in(bb_im)], axis=1)
    cr, ci = c_c[..., 0], c_c[..., 1]
    bd_out = lambda m: jnp.einsum('ghp,gk->gpkh', m, eye).reshape(S5_W, BRANCH_W)
    wc = jnp.concatenate([bd_out(cr), -bd_out(ci)], axis=0)
    pr, pi_ = ab_re.reshape(1, S5_W), ab_im.reshape(1, S5_W)
    lv_r, lv_i = [], []
    tr, ti = pr, pi_
    d = 1
    while d < tc:
        lv_r.append(pr)
        lv_i.append(pi_)
        tr, ti = (jnp.concatenate([tr, tr * pr - ti * pi_], axis=0),
                  jnp.concatenate([ti, tr * pi_ + ti * pr], axis=0))
        pr, pi_ = pr * pr - pi_ * pi_, 2.0 * pr * pi_
        d *= 2
    n_lv = len(lv_r)
    pad = (-n_lv) % 8
    a2k = jnp.concatenate([jnp.concatenate(lv_r, axis=0), jnp.concatenate(lv_i, axis=0)], axis=1)
    a2k = jnp.pad(a2k, ((0, pad), (0, 0)))
    apow = jnp.concatenate([tr, ti], axis=1)
    return wb.astype(BF16), wc.astype(BF16), a2k, apow


def _s5_mixer(p3, h0, lam_re, lam_im, log_dt, b_c, c_c, d_skip, w_glu, b_glu, tc):
    bsz, t, _ = p3.shape
    wb, wc, a2k, apow = _s5_tables(lam_re, lam_im, log_dt, b_c, c_c, S5_ROWS)
    h0f = jnp.concatenate([h0[..., 0].reshape(bsz, 1, S5_W), h0[..., 1].reshape(bsz, 1, S5_W)], axis=-1)
    full = lambda a: pl.BlockSpec(a.shape, lambda b, c: (0,) * a.ndim)
    d2, bg2, wg = d_skip.reshape(1, BRANCH_W), b_glu.reshape(1, BRANCH_W), w_glu.astype(BF16)
    y, hl = pl.pallas_call(
        functools.partial(_s5_kernel, tc=tc),
        grid=(bsz, t // tc),
        in_specs=[pl.BlockSpec((None, tc, BRANCH_W), lambda b, c: (b, c, COL_S5 // BRANCH_W)),
                  pl.BlockSpec((None, 1, 2 * S5_W), lambda b, c: (b, 0, 0)),
                  full(wb), full(wc), full(a2k), full(apow), full(d2), full(wg), full(bg2)],
        out_specs=[pl.BlockSpec((None, tc, BRANCH_W), lambda b, c: (b, c, 0)),
                   pl.BlockSpec((None, 1, 2 * S5_W), lambda b, c: (b, 0, 0))],
        out_shape=[jax.ShapeDtypeStruct((bsz, t, BRANCH_W), BF16),
                   jax.ShapeDtypeStruct((bsz, 1, 2 * S5_W), F32)],
        scratch_shapes=[pltpu.VMEM((1, 2 * S5_W), F32), pltpu.VMEM((tc, 2 * S5_W), F32)],
        compiler_params=_cparams(("parallel", "arbitrary")),
    )(p3, h0f, wb, wc, a2k, apow, d2, wg, bg2)
    h_last = jnp.stack([hl[:, 0, :S5_W].reshape(bsz, S5_GROUPS, S5_STATE),
                        hl[:, 0, S5_W:].reshape(bsz, S5_GROUPS, S5_STATE)], axis=-1)
    return y, h_last


def _rwkv_kernel(z_ref, sh0_ref, s0_ref, mu_ref, w0_ref, w2_ref, a0_ref, a2_ref, g2_ref,
                 kk_ref, ka_ref, rk_ref, lnw_ref, lnb_ref,
                 y_ref, sl_ref, st_ref, zp_ref, *, l, bb):
    c = pl.program_id(1)

    @pl.when(c == 0)
    def _():
        st_ref[...] = s0_ref[...]
        zp_ref[...] = sh0_ref[...]

    row = _iota((l, 1), 0)
    zms = []
    for b in range(bb):
        z = z_ref[b]
        prev = jnp.where(row == 0, zp_ref[b], pltpu.roll(z, 1, axis=0))
        zp_ref[b] = z[l - 1:l]
        zms.append(z + (prev - z) * mu_ref[...])
    zm = jnp.concatenate(zms, axis=0)
    r, k, v = zm[:, 0:256], zm[:, 256:512], zm[:, 512:768]
    lo = zm[:, 768:896]
    g_lo = zm[:, 896:1024]
    w_log = -_softplus(-(w0_ref[...] + _dot(jnp.tanh(lo), w2_ref[...]))) - 0.5
    lw = -jnp.exp(w_log)
    a = jax.nn.sigmoid(a0_ref[...] + _dot(lo, a2_ref[...]))
    g = _dot(jax.nn.sigmoid(g_lo), g2_ref[...])
    ones_h = _head_ones()
    kk = k * kk_ref[...]
    kk = kk * lax.rsqrt(_dot2(kk * kk, ones_h) + NORM_EPS)
    k = k * (1.0 + (a - 1.0) * ka_ref[...])
    kka = kk * a

    cum = _cumsum_rows(lw, l, bb)
    p_incl = jnp.exp(cum)
    p_inv = jnp.exp(-cum)
    kt = kk * jnp.exp(cum - lw)
    rt = r * p_incl
    kh = k * p_inv
    ah = kka * p_inv

    hm = _head_mask(l)
    strict, incl = _tri_masks(l)
    streams = range(bb)
    rows = [slice(b * l, (b + 1) * l) for b in streams]
    ex = lambda x: [_expand(x[s], hm) for s in rows]
    kt_b, rt_b, v_b = [kt[s] for s in rows], [rt[s] for s in rows], [v[s] for s in rows]
    kh_e, ah_e, v_e = ex(kh), ex(ah), ex(v)
    a_aa = [jnp.where(strict, _dot_nt(kt_b[b], ah_e[b]), 0.0) for b in streams]
    a_ak = [jnp.where(strict, _dot_nt(kt_b[b], kh_e[b]), 0.0) for b in streams]
    b_ra = [jnp.where(incl, _dot_nt(rt_b[b], ah_e[b]), 0.0) for b in streams]
    b_rk = [jnp.where(incl, _dot_nt(rt_b[b], kh_e[b]), 0.0) for b in streams]
    t_inv = _unit_lower_inverse(a_aa, l)
    st = [st_ref[b] for b in streams]
    rhs = [_dot_nt(kt_b[b], st[b]) + _dot(a_ak[b], v_e[b]) for b in streams]
    yb = [_dot_nt(rt_b[b], st[b]) + _dot(b_rk[b], v_e[b]) for b in streams]
    u = [_dot(t_inv[b], _expand(rhs[b], hm)) for b in streams]
    yb = [yb[b] - _dot(b_ra[b], _expand(u[b], hm)) for b in streams]
    to_end = [jnp.exp(cum[s][l - 1:l] - cum[s]) for s in rows]
    lhs_t = [jnp.concatenate([v_b[b], -u[b]], axis=0).T for b in streams]
    rhs_k = [jnp.concatenate([k[rows[b]] * to_end[b], kka[rows[b]] * to_end[b]], axis=0) for b in streams]
    head_blk = _same_head(HEAD_W)
    for b in streams:
        p_last = p_incl[(b + 1) * l - 1:(b + 1) * l]
        st_ref[b] = st[b] * p_last + jnp.where(head_blk, _dot(lhs_t[b], rhs_k[b]), 0.0)
    y = jnp.concatenate(yb, axis=0)

    inv_w = 1.0 / HEAD_W
    mean = _dot2(y, ones_h) * inv_w
    yc = y - mean
    var = _dot2(yc * yc, ones_h) * inv_w
    y = yc * lax.rsqrt(var + RW_EPS) * lnw_ref[...] + lnb_ref[...]
    bonus = _dot2(r * k * rk_ref[...], ones_h) * v
    y = ((y + bonus) * g).astype(y_ref.dtype)
    for b in range(bb):
        y_ref[b] = y[b * l:(b + 1) * l]

    @pl.when(c == pl.num_programs(1) - 1)
    def _():
        sl_ref[...] = st_ref[...]


STACK_ROWS = 256


def _streams_per_step(bsz, l):
    bb = max(1, min(bsz, STACK_ROWS // l))
    while bsz % bb:
        bb -= 1
    return bb


def _block_diag_heads(s):
    bsz = s.shape[0]
    eye = jnp.eye(HEADS, dtype=s.dtype)
    return jnp.einsum('bhij,hg->bhigj', s, eye).reshape(bsz, BRANCH_W, BRANCH_W)


def _diag_blocks(s):
    bsz = s.shape[0]
    s5 = s.reshape(bsz, HEADS, HEAD_W, HEADS, HEAD_W)
    return jnp.stack([s5[:, h, :, h, :] for h in range(HEADS)], axis=1)


def _rwkv_mixer(p3, shift0, s0, mu, w0, w2, a0, a2, g2, k_k, k_a, r_k, ln_w, ln_b, l):
    bsz, t, _ = p3.shape
    row = lambda a: a.reshape(1, -1)
    w2p = jnp.concatenate([w2, jnp.zeros_like(w2)], axis=0).astype(BF16)
    a2p = jnp.concatenate([jnp.zeros_like(a2), a2], axis=0).astype(BF16)
    args = (shift0.reshape(bsz, 1, RW_COLS), _block_diag_heads(s0), row(mu), row(w0), w2p, row(a0), a2p,
            g2.astype(BF16), row(k_k), row(k_a), row(r_k), row(ln_w), row(ln_b))
    full = lambda a: pl.BlockSpec(a.shape, lambda b, c: (0,) * a.ndim)
    bb = _streams_per_step(bsz, l)
    y, sl = pl.pallas_call(
        functools.partial(_rwkv_kernel, l=l, bb=bb),
        grid=(bsz // bb, t // l),
        in_specs=[pl.BlockSpec((bb, l, RW_COLS), lambda b, c: (b, c, COL_RW // RW_COLS)),
                  pl.BlockSpec((bb, 1, RW_COLS), lambda b, c: (b, 0, 0)),
                  pl.BlockSpec((bb, BRANCH_W, BRANCH_W), lambda b, c: (b, 0, 0))]
                 + [full(a) for a in args[2:]],
        out_specs=[pl.BlockSpec((bb, l, BRANCH_W), lambda b, c: (b, c, 0)),
                   pl.BlockSpec((bb, BRANCH_W, BRANCH_W), lambda b, c: (b, 0, 0))],
        out_shape=[jax.ShapeDtypeStruct((bsz, t, BRANCH_W), BF16),
                   jax.ShapeDtypeStruct((bsz, BRANCH_W, BRANCH_W), F32)],
        scratch_shapes=[pltpu.VMEM((bb, BRANCH_W, BRANCH_W), F32), pltpu.VMEM((bb, 1, RW_COLS), F32)],
        compiler_params=_cparams(("parallel", "arbitrary")),
    )(p3, *args)
    return y, _diag_blocks(sl)


def _sgu_kernel(z_ref, lnw_ref, lnb_ref, wm_ref, bias_ref, o_ref, v_ref, *, l, nc):
    zg = jax.nn.gelu(z_ref[...])
    u, v = zg[:, :BRANCH_W], zg[:, BRANCH_W:]
    mean = jnp.mean(v, axis=-1, keepdims=True)
    vc = v - mean
    var = jnp.mean(vc * vc, axis=-1, keepdims=True)
    v = vc * lax.rsqrt(var + NORM_EPS) * lnw_ref[...] + lnb_ref[...]
    v_ref[...] = v
    hm = _head_mask(l)
    wm = wm_ref[...]
    for i in range(nc):
        rows = slice(i * l, (i + 1) * l)
        mixed = bias_ref[...] + _dot(wm, _expand(v[rows], hm))
        o_ref[rows, :] = (u[rows] * mixed).astype(o_ref.dtype)


def _sgu_mixer(p3, ln_w, ln_b, w_s, b_s):
    bsz, t, _ = p3.shape
    l = min(SG_CHUNK, t)
    nc = max(1, min(SG_ROWS, t) // l)
    tril = jnp.tril(jnp.ones((l, l), F32))
    wm = jnp.transpose(w_s[:, :l, :l] * tril, (1, 0, 2)).reshape(l, HEADS * l).astype(BF16)
    bias = jnp.repeat(jnp.transpose(b_s[:, :l]), HEAD_W, axis=1)
    row = lambda a: a.reshape(1, -1)
    full = lambda a: pl.BlockSpec(a.shape, lambda b, c: (0,) * a.ndim)
    args = (row(ln_w), row(ln_b), wm, bias)
    return pl.pallas_call(
        functools.partial(_sgu_kernel, l=l, nc=nc),
        grid=(bsz, t // (nc * l)),
        in_specs=[pl.BlockSpec((None, nc * l, 2 * BRANCH_W), lambda b, c: (b, c, COL_SG // (2 * BRANCH_W)))]
                 + [full(a) for a in args],
        out_specs=[pl.BlockSpec((None, nc * l, BRANCH_W), lambda b, c: (b, c, 0)),
                   pl.BlockSpec((None, nc * l, BRANCH_W), lambda b, c: (b, c, 0))],
        out_shape=[jax.ShapeDtypeStruct((bsz, t, BRANCH_W), BF16),
                   jax.ShapeDtypeStruct((bsz, t, BRANCH_W), F32)],
        compiler_params=_cparams(("parallel", "parallel")),
    )(p3, *args)


def _gdn_kernel(z_ref, ab_ref, cv0_ref, s0_ref, cw_ref, alog_ref, dtb_ref, nw_ref,
                y_ref, sl_ref, st_ref, cv_ref, *, l, bb):
    c = pl.program_id(1)

    @pl.when(c == 0)
    def _():
        st_ref[...] = s0_ref[...]
        cv_ref[...] = cv0_ref[...]

    row8 = _iota((8, 1), 0)
    convs, gates = [], []
    for b in range(bb):
        z = z_ref[b]
        qkv = z[:, :GD_QKV]
        gates.append(z[:, GD_QKV:])
        carry = cv_ref[b]
        cv_ref[b] = qkv[l - 8:l]
        conv = qkv * cw_ref[GD_CONV - 1:GD_CONV]
        for j in range(1, GD_CONV):
            sh = pltpu.roll(qkv, j, axis=0)
            top = jnp.where(row8 < j, pltpu.roll(carry, j, axis=0), sh[:8])
            sh = jnp.concatenate([top, sh[8:]], axis=0) if l > 8 else top
            conv = conv + sh * cw_ref[GD_CONV - 1 - j:GD_CONV - j]
        convs.append(conv)
    conv = _silu(jnp.concatenate(convs, axis=0))
    gate = jnp.concatenate(gates, axis=0)
    q, k, v = conv[:, :256], conv[:, 256:512], conv[:, 512:768]
    ones_h = _head_ones()
    q = q * lax.rsqrt(_dot2(q * q, ones_h) + NORM_EPS) * (HEAD_W ** -0.5)
    k = k * lax.rsqrt(_dot2(k * k, ones_h) + NORM_EPS)
    ab = jnp.concatenate([ab_ref[b] for b in range(bb)], axis=0)
    lane_h = _iota((bb * l, BRANCH_W), 1) // HEAD_W
    a_in = jnp.zeros((bb * l, BRANCH_W), F32)
    b_in = jnp.zeros((bb * l, BRANCH_W), F32)
    for h in range(HEADS):
        a_in = jnp.where(lane_h == h, ab[:, h:h + 1], a_in)
        b_in = jnp.where(lane_h == h, ab[:, HEADS + h:HEADS + h + 1], b_in)
    beta = jax.nn.sigmoid(b_in)
    g = -jnp.exp(alog_ref[...]) * _softplus(a_in + dtb_ref[...])
    gc = _cumsum_rows(g, l, bb)
    eg = jnp.exp(gc)
    kb = k * beta
    vb = v * beta
    kbg = kb * eg
    qg = q * eg

    hm = _head_mask(l)
    strict, incl = _tri_masks(l)
    n = HEADS * l
    streams = range(bb)
    rows = [slice(b * l, (b + 1) * l) for b in streams]
    ex = lambda x: [_expand(x[s], hm) for s in rows]
    k_e, vb_e, kbg_e = ex(k), ex(vb), ex(kbg)
    lane_hd = _iota((l, n), 1) // l
    eye = _iota((l, n), 0) == (_iota((l, n), 1) % l)
    decay = []
    for s in rows:
        gi = jnp.zeros((l, n), F32)
        for h in range(HEADS):
            gi = jnp.where(lane_hd == h, gc[s][:, h * HEAD_W:h * HEAD_W + 1], gi)
        gj = jnp.sum(jnp.where(eye, gi, 0.0), axis=0, keepdims=True)
        decay.append(jnp.where(incl, jnp.exp(jnp.where(incl, gi - gj, 0.0)), 0.0))
    lm = [jnp.where(strict, _dot_nt(kb[rows[b]], k_e[b]) * decay[b], 0.0) for b in streams]
    qk = [_dot_nt(q[rows[b]], k_e[b]) * decay[b] for b in streams]
    t_inv = _unit_lower_inverse(lm, l)
    uc = [_dot(t_inv[b], vb_e[b]) for b in streams]
    wc = [_dot(t_inv[b], kbg_e[b]) for b in streams]
    st = [st_ref[b] for b in streams]
    ob = [_dot(qg[rows[b]], st[b]) for b in streams]
    v_new = [uc[b] - _dot(wc[b], st[b]) for b in streams]
    ob = [ob[b] + _dot(qk[b], _expand(v_new[b], hm)) for b in streams]
    g_last = [gc[s][l - 1:l] for s in rows]
    k_dec = [(k[rows[b]] * jnp.exp(g_last[b] - gc[rows[b]])).T for b in streams]
    head_blk = _same_head(HEAD_W)
    for b in streams:
        st_ref[b] = st[b] * jnp.exp(g_last[b]) + jnp.where(head_blk, _dot(k_dec[b], v_new[b]), 0.0)
    o = jnp.concatenate(ob, axis=0)
    ms = _dot2(o * o, ones_h) * (1.0 / HEAD_W)
    o = (o * lax.rsqrt(ms + NORM_EPS) * nw_ref[...] * _silu(gate)).astype(y_ref.dtype)
    for b in range(bb):
        y_ref[b] = o[b * l:(b + 1) * l]

    @pl.when(c == pl.num_programs(1) - 1)
    def _():
        sl_ref[...] = st_ref[...]


def _gdn_mixer(p3, conv0, s0, conv_w, a_log, dt_bias, norm_w, l):
    bsz, t, _ = p3.shape
    cv0 = jnp.pad(conv0, ((0, 0), (8 - (GD_CONV - 1), 0), (0, 0)))
    cw = jnp.pad(conv_w, ((0, 8 - GD_CONV), (0, 0)))
    per_head = lambda a: jnp.repeat(a, HEAD_W).reshape(1, BRANCH_W)
    args = (cv0, _block_diag_heads(s0), cw, per_head(a_log), per_head(dt_bias),
            jnp.tile(norm_w, HEADS).reshape(1, BRANCH_W))
    full = lambda a: pl.BlockSpec(a.shape, lambda b, c: (0,) * a.ndim)
    bb = _streams_per_step(bsz, l)
    y, sl = pl.pallas_call(
        functools.partial(_gdn_kernel, l=l, bb=bb),
        grid=(bsz // bb, t // l),
        in_specs=[pl.BlockSpec((bb, l, 1024), lambda b, c: (b, c, COL_GD // 1024)),
                  pl.BlockSpec((bb, l, LANES), lambda b, c: (b, c, COL_AB // LANES)),
                  pl.BlockSpec((bb, 8, GD_QKV), lambda b, c: (b, 0, 0)),
                  pl.BlockSpec((bb, BRANCH_W, BRANCH_W), lambda b, c: (b, 0, 0))]
                 + [full(a) for a in args[2:]],
        out_specs=[pl.BlockSpec((bb, l, BRANCH_W), lambda b, c: (b, c, 0)),
                   pl.BlockSpec((bb, BRANCH_W, BRANCH_W), lambda b, c: (b, 0, 0))],
        out_shape=[jax.ShapeDtypeStruct((bsz, t, BRANCH_W), BF16),
                   jax.ShapeDtypeStruct((bsz, BRANCH_W, BRANCH_W), F32)],
        scratch_shapes=[pltpu.VMEM((bb, BRANCH_W, BRANCH_W), F32), pltpu.VMEM((bb, 8, GD_QKV), F32)],
        compiler_params=_cparams(("parallel", "arbitrary")),
    )(p3, p3, *args)
    return y, _diag_blocks(sl)


def _merge_kernel(ya_ref, yb_ref, yc_ref, yd_ref, x_ref, nw_ref, wg_ref, wbr_ref, wout_ref, o_ref):
    x = x_ref[...]
    ms = jnp.mean(x * x, axis=-1, keepdims=True)
    h = (x * lax.rsqrt(ms + NORM_EPS) * nw_ref[...]).astype(BF16)
    m = None
    for b, y_ref in enumerate((ya_ref, yb_ref, yc_ref, yd_ref)):
        gate = jnp.dot(h, wg_ref[:, b * D_MODEL:(b + 1) * D_MODEL], preferred_element_type=F32)
        br = jnp.dot(y_ref[...], wbr_ref[b], preferred_element_type=F32)
        term = (0.5 * jnp.tanh(0.5 * gate) + 0.5) * br
        m = term if m is None else m + term
    o_ref[...] = x + jnp.dot(m.astype(BF16), wout_ref[...], preferred_element_type=F32)


def _merge(ys, x2, norm_w, w_gate, w_branch, w_out, tm):
    n = x2.shape[0]
    yspec = pl.BlockSpec((tm, BRANCH_W), lambda i: (i, 0))
    once = pl.Buffered(1)
    return pl.pallas_call(
        _merge_kernel,
        grid=(n // tm,),
        in_specs=[yspec, yspec, yspec, yspec,
                  pl.BlockSpec((tm, D_MODEL), lambda i: (i, 0)),
                  pl.BlockSpec((1, D_MODEL), lambda i: (0, 0)),
                  pl.BlockSpec((D_MODEL, 4 * D_MODEL), lambda i: (0, 0), pipeline_mode=once),
                  pl.BlockSpec((4, BRANCH_W, D_MODEL), lambda i: (0, 0, 0), pipeline_mode=once),
                  pl.BlockSpec((D_MODEL, D_MODEL), lambda i: (0, 0), pipeline_mode=once)],
        out_specs=pl.BlockSpec((tm, D_MODEL), lambda i: (i, 0)),
        out_shape=jax.ShapeDtypeStruct((n, D_MODEL), F32),
        compiler_params=_cparams(("parallel",)),
    )(*[y.reshape(n, BRANCH_W) for y in ys], x2, norm_w.reshape(1, D_MODEL), w_gate,
      w_branch.astype(BF16), w_out.astype(BF16))


def _ffn_kernel(x_ref, nw_ref, w1_ref, w3_ref, w2_ref, o_ref, h_ref):
    j = pl.program_id(1)

    @pl.when(j == 0)
    def _():
        x = x_ref[...]
        ms = jnp.mean(x * x, axis=-1, keepdims=True)
        h_ref[...] = (x * lax.rsqrt(ms + NORM_EPS) * nw_ref[...]).astype(BF16)
        o_ref[...] = x

    h = h_ref[...]
    a = _silu(jnp.dot(h, w1_ref[...], preferred_element_type=F32)) * jnp.dot(h, w3_ref[...], preferred_element_type=F32)
    o_ref[...] += jnp.dot(a.astype(BF16), w2_ref[...], preferred_element_type=F32)


def _ffn(x2, norm_w, w1, w3, w2, tm, tf):
    n = x2.shape[0]
    dff = w1.shape[1]
    return pl.pallas_call(
        _ffn_kernel,
        grid=(n // tm, dff // tf),
        in_specs=[pl.BlockSpec((tm, D_MODEL), lambda i, j: (i, 0)),
                  pl.BlockSpec((1, D_MODEL), lambda i, j: (0, 0)),
                  pl.BlockSpec((D_MODEL, tf), lambda i, j: (0, j)),
                  pl.BlockSpec((D_MODEL, tf), lambda i, j: (0, j)),
                  pl.BlockSpec((tf, D_MODEL), lambda i, j: (j, 0))],
        out_specs=pl.BlockSpec((tm, D_MODEL), lambda i, j: (i, 0)),
        out_shape=jax.ShapeDtypeStruct((n, D_MODEL), F32),
        scratch_shapes=[pltpu.VMEM((tm, D_MODEL), BF16)],
        compiler_params=_cparams(("parallel", "arbitrary")),
    )(x2, norm_w.reshape(1, D_MODEL), w1.astype(BF16), w3.astype(BF16), w2.astype(BF16))


MOE_SUB = 256
MOE_TILE = 2048
MOE_PLACE_ROWS = 1024
MOE_FF = 512
ROUTER_ROWS = 256


def _router_kernel(x_ref, nw_ref, wr_ref, br_ref, h_ref, rcol_ref, gcol_ref, rrow_ref, cnt_ref, seen_ref,
                   *, tm, k):
    x = x_ref[...]
    ms = jnp.mean(x * x, axis=-1, keepdims=True)
    h = x * lax.rsqrt(ms + NORM_EPS) * nw_ref[...]
    h_ref[...] = h.astype(BF16)
    lane = _iota((tm, LANES), 1)
    logits = jnp.where(lane < N_EXPERTS, _dot_hp(h, wr_ref[...]) + br_ref[...], -jnp.inf)
    m1 = jnp.max(logits, axis=1, keepdims=True)
    i1 = jnp.min(jnp.where(logits == m1, lane, LANES), axis=1, keepdims=True)
    rest = jnp.where(lane == i1, -jnp.inf, logits)
    m2 = jnp.max(rest, axis=1, keepdims=True)
    i2 = jnp.min(jnp.where(rest == m2, lane, LANES), axis=1, keepdims=True)
    e2 = jnp.exp(m2 - m1)
    g1 = 1.0 / (1.0 + e2)
    g2 = e2 / (1.0 + e2)
    sel = (lane == i1) | (lane == i2)
    self32 = sel.astype(F32)
    gcol_ref[...] = jnp.where(lane == i1, g1, 0.0) + jnp.where(lane == i2, g2, 0.0)
    @pl.when(pl.program_id(0) % k == 0)
    def _():
        seen_ref[...] = jnp.zeros_like(seen_ref)

    seen = seen_ref[...]
    tri = (_iota((tm, tm), 1) < _iota((tm, tm), 0)).astype(BF16)
    rank = jnp.dot(tri, self32.astype(BF16), preferred_element_type=F32) + seen
    r = jnp.where(sel, rank, -1.0)
    rcol_ref[...] = r
    rrow_ref[...] = r.T[:N_EXPERTS]
    seen = seen + jnp.sum(self32, axis=0, keepdims=True)
    seen_ref[...] = seen
    cnt_ref[...] = jnp.broadcast_to(seen, (8, LANES))


def _router(x2, norm_w, w_router, b_router, tm, k):
    n = x2.shape[0]
    nt = n // (tm * k)
    wr = jnp.pad(w_router, ((0, 0), (0, LANES - N_EXPERTS)))
    br = jnp.pad(b_router, (0, LANES - N_EXPERTS)).reshape(1, LANES)
    return pl.pallas_call(
        functools.partial(_router_kernel, tm=tm, k=k),
        grid=(n // tm,),
        in_specs=[pl.BlockSpec((tm, D_MODEL), lambda i: (i, 0)),
                  pl.BlockSpec((1, D_MODEL), lambda i: (0, 0)),
                  pl.BlockSpec((D_MODEL, LANES), lambda i: (0, 0)),
                  pl.BlockSpec((1, LANES), lambda i: (0, 0))],
        out_specs=[pl.BlockSpec((tm, D_MODEL), lambda i: (i, 0)),
                   pl.BlockSpec((tm, LANES), lambda i: (i, 0)),
                   pl.BlockSpec((tm, LANES), lambda i: (i, 0)),
                   pl.BlockSpec((None, N_EXPERTS, tm), lambda i: (i // k, 0, i % k)),
                   pl.BlockSpec((None, 8, LANES), lambda i: (i, 0, 0))],
        out_shape=[jax.ShapeDtypeStruct((n, D_MODEL), BF16),
                   jax.ShapeDtypeStruct((n, LANES), F32),
                   jax.ShapeDtypeStruct((n, LANES), F32),
                   jax.ShapeDtypeStruct((nt, N_EXPERTS, tm * k), F32),
                   jax.ShapeDtypeStruct((n // tm, 8, LANES), F32)],
        scratch_shapes=[pltpu.VMEM((1, LANES), F32)],
        compiler_params=_cparams(("arbitrary",)),
    )(x2, norm_w.reshape(1, D_MODEL), wr, br)


def _moe_kernel(cnt_ref, h_ref, rrow_ref, rcol_ref, gcol_ref, w1_ref, w3_ref, w2_ref, o_ref,
                xs_ref, acc_ref, *, tm):
    i, e, c = pl.program_id(0), pl.program_id(1), pl.program_id(2)
    n_sub = (cnt_ref[i * N_EXPERTS + e] + (MOE_SUB - 1)) // MOE_SUB

    @pl.when((e == 0) & (c == 0))
    def _():
        o_ref[...] = jnp.zeros_like(o_ref)

    @pl.when(c == 0)
    def _():
        rrow = rrow_ref[pl.ds(e, 1), :]
        slot = _iota((MOE_SUB, 1), 0).astype(F32)

        def gather(s, carry):
            base = pl.multiple_of(s * MOE_SUB, MOE_SUB)
            pick = (rrow == slot + (s * MOE_SUB).astype(F32)).astype(BF16)
            xs_ref[pl.ds(base, MOE_SUB), :] = jnp.dot(pick, h_ref[...], preferred_element_type=F32).astype(BF16)
            acc_ref[pl.ds(base, MOE_SUB), :] = jnp.zeros((MOE_SUB, D_MODEL), F32)
            return carry

        lax.fori_loop(0, n_sub, gather, 0)

    def expert_rows(base, m):
        xb = xs_ref[pl.ds(base, m), :]
        a = (_silu(jnp.dot(xb, w1_ref[...], preferred_element_type=F32))
             * jnp.dot(xb, w3_ref[...], preferred_element_type=F32))
        acc_ref[pl.ds(base, m), :] += jnp.dot(a.astype(BF16), w2_ref[...], preferred_element_type=F32)

    def expert_pair(p, carry):
        expert_rows(pl.multiple_of(p * (2 * MOE_SUB), 2 * MOE_SUB), 2 * MOE_SUB)
        return carry

    if tm >= 2 * MOE_SUB:
        lax.fori_loop(0, n_sub // 2, expert_pair, 0)

        @pl.when(n_sub % 2 == 1)
        def _():
            expert_rows(pl.multiple_of((n_sub - 1) * MOE_SUB, MOE_SUB), MOE_SUB)
    else:
        @pl.when(n_sub > 0)
        def _():
            expert_rows(0, MOE_SUB)

    @pl.when(c == pl.num_programs(2) - 1)
    def _():
        tp = min(tm, MOE_PLACE_ROWS)
        lane = _iota((tp, LANES), 1)
        slot = _iota((1, MOE_SUB), 1).astype(F32)

        def scatter(s, carry):
            base = pl.multiple_of(s * MOE_SUB, MOE_SUB)
            a = acc_ref[pl.ds(base, MOE_SUB), :]
            hi = a.astype(BF16)
            lo = (a - hi.astype(F32)).astype(BF16)
            for r0 in range(0, tm, tp):
                rcol = jnp.sum(jnp.where(lane == e, rcol_ref[r0:r0 + tp], 0.0), axis=1, keepdims=True)
                gate = jnp.sum(jnp.where(lane == e, gcol_ref[r0:r0 + tp], 0.0), axis=1, keepdims=True)
                place = (rcol == slot + (s * MOE_SUB).astype(F32)).astype(BF16)
                back = (jnp.dot(place, hi, preferred_element_type=F32)
                        + jnp.dot(place, lo, preferred_element_type=F32))
                o_ref[r0:r0 + tp] += gate * back
            return carry

        lax.fori_loop(0, n_sub, scatter, 0)


def _moe(h2, rrow, rcol, gcol, counts, w1, w3, w2, tm, tf):
    n = h2.shape[0]
    nt = n // tm
    dff = w1.shape[2]
    grid_spec = pltpu.PrefetchScalarGridSpec(
        num_scalar_prefetch=1,
        grid=(nt, N_EXPERTS, dff // tf),
        in_specs=[pl.BlockSpec((tm, D_MODEL), lambda i, e, c, cnt: (i, 0), pipeline_mode=pl.Buffered(1)),
                  pl.BlockSpec((None, N_EXPERTS, tm), lambda i, e, c, cnt: (i, 0, 0)),
                  pl.BlockSpec((tm, LANES), lambda i, e, c, cnt: (i, 0), pipeline_mode=pl.Buffered(1)),
                  pl.BlockSpec((tm, LANES), lambda i, e, c, cnt: (i, 0), pipeline_mode=pl.Buffered(1)),
                  pl.BlockSpec((None, D_MODEL, tf), lambda i, e, c, cnt: (e, 0, c)),
                  pl.BlockSpec((None, D_MODEL, tf), lambda i, e, c, cnt: (e, 0, c)),
                  pl.BlockSpec((None, tf, D_MODEL), lambda i, e, c, cnt: (e, c, 0))],
        out_specs=pl.BlockSpec((tm, D_MODEL), lambda i, e, c, cnt: (i, 0), pipeline_mode=pl.Buffered(1)),
        scratch_shapes=[pltpu.VMEM((tm, D_MODEL), BF16), pltpu.VMEM((tm, D_MODEL), F32)],
    )
    return pl.pallas_call(
        functools.partial(_moe_kernel, tm=tm),
        grid_spec=grid_spec,
        out_shape=jax.ShapeDtypeStruct((n, D_MODEL), F32),
        compiler_params=_cparams(("parallel", "arbitrary", "arbitrary")),
    )(counts, h2, rrow, rcol, gcol, w1, w3, w2)


def _moe_ffn(x2, norm_w, w_router, b_router, w1, w3, w2):
    n = x2.shape[0]
    tm = min(MOE_TILE, n)
    k = tm // ROUTER_ROWS
    h2, rcol, gcol, rrow, cnt = _router(x2, norm_w, w_router, b_router, ROUTER_ROWS, k)
    counts = cnt[k - 1::k, 0, :N_EXPERTS].astype(jnp.int32).reshape(-1)
    return _moe(h2, rrow, rcol, gcol, counts, w1.astype(BF16), w3.astype(BF16), w2.astype(BF16), tm, MOE_FF)


def _final_kernel(x_ref, m_ref, nw_ref, o_ref):
    x = x_ref[...] + m_ref[...]
    ms = jnp.mean(x * x, axis=-1, keepdims=True)
    o_ref[...] = x * lax.rsqrt(ms + NORM_EPS) * nw_ref[...]


def _final(x2, m2, norm_w, tm):
    n = x2.shape[0]
    spec = pl.BlockSpec((tm, D_MODEL), lambda i: (i, 0))
    return pl.pallas_call(
        _final_kernel,
        grid=(n // tm,),
        in_specs=[spec, spec, pl.BlockSpec((1, D_MODEL), lambda i: (0, 0))],
        out_specs=spec,
        out_shape=jax.ShapeDtypeStruct((n, D_MODEL), F32),
        compiler_params=_cparams(("parallel",)),
    )(x2, m2, norm_w.reshape(1, D_MODEL))


def _permute_w_in(w):
    s5, rw, sg = w[:, 0:256], w[:, 256:1280], w[:, 1280:1792]
    gd, ab, gates = w[:, 1792:2816], w[:, 2816:2824], w[:, 2824:6920]
    pad = jnp.zeros((D_MODEL, PROJ_COLS - COL_AB - 8), w.dtype)
    return jnp.concatenate([rw, gd, sg, s5, ab, pad], axis=1).astype(BF16), gates.astype(BF16)


def _run_trunk(x, s5_h, rw_s, rw_shift, gd_s, gd_conv, p, w_in_perm):
    bsz, t, _ = x.shape
    n = bsz * t
    tm = min(1024, n)
    l = min(64, t)
    tc_s5 = min(256, t)
    new = ([], [], [], [], [], [])
    x2 = x.reshape(n, D_MODEL)
    moe_out = None
    for layer in range(2):
        g = lambda name: p[name][layer]
        w_mix, w_gate = w_in_perm[layer]
        proj = _norm_proj(x2, g('norm1_w'), w_mix, tm, 1024)
        p3 = proj.reshape(bsz, t, PROJ_COLS)
        y_a, s5_new = _s5_mixer(p3, s5_h[layer], g('s5_lam_re'), g('s5_lam_im'), g('s5_log_dt'), g('s5_b'),
                                g('s5_c'), g('s5_d'), g('s5_w_glu'), g('s5_b_glu'), tc_s5)
        y_b, rw_new = _rwkv_mixer(p3, rw_shift[layer], rw_s[layer], g('rw_mu'), g('rw_w0'), g('rw_w2'),
                                  g('rw_a0'), g('rw_a2'), g('rw_g2'), g('rw_k_k'), g('rw_k_a'), g('rw_r_k'),
                                  g('rw_ln_w'), g('rw_ln_b'), l)
        y_c, sg_v = _sgu_mixer(p3, g('sg_ln_w'), g('sg_ln_b'), g('sg_w_s'), g('sg_b_s'))
        y_d, gd_new = _gdn_mixer(p3, gd_conv[layer], gd_s[layer], g('gd_conv_w'), g('gd_a_log'),
                                 g('gd_dt_bias'), g('gd_norm_w'), l)
        shift_new = p3[:, t - 1, COL_RW:COL_RW + RW_COLS]
        conv_new = p3[:, t - (GD_CONV - 1):, COL_GD:COL_GD + GD_QKV]
        x2 = _merge((y_a, y_b, y_c, y_d), x2, g('norm1_w'), w_gate, g('w_branch'), g('w_out'), min(512, n))
        j = layer // 2
        if layer % 2 == 0:
            x2 = _ffn(x2, g('norm2_w'), p['ffn_w1'][j], p['ffn_w3'][j], p['ffn_w2'][j], min(512, n), 1408)
        else:
            moe_out = _moe_ffn(x2, g('norm2_w'), p['moe_router'][j], p['moe_router_b'][j],
                               p['moe_w1'][j], p['moe_w3'][j], p['moe_w2'][j])
        for lst, s in zip(new, (s5_new, rw_new, shift_new, gd_new, conv_new, sg_v)):
            lst.append(s)
    y = _final(x2, moe_out, p['final_norm_w'], tm).reshape(bsz, t, D_MODEL)
    return y, [jnp.stack(lst) for lst in new]


def kernel(x_prompt, x_sample, state_s5, state_rwkv, state_rwkv_shift, state_gdn, state_gdn_conv, norm1_w, w_in, s5_lam_re, s5_lam_im, s5_log_dt, s5_b, s5_c, s5_d, s5_w_glu, s5_b_glu, rw_mu, rw_w0, rw_w2, rw_a0, rw_a2, rw_g2, rw_k_k, rw_k_a, rw_r_k, rw_ln_w, rw_ln_b, sg_ln_w, sg_ln_b, sg_w_s, sg_b_s, gd_conv_w, gd_a_log, gd_dt_bias, gd_norm_w, w_branch, w_out, norm2_w, ffn_w1, ffn_w3, ffn_w2, moe_router, moe_router_b, moe_w1, moe_w3, moe_w2, final_norm_w):
    p = {
        'norm1_w': norm1_w, 's5_lam_re': s5_lam_re, 's5_lam_im': s5_lam_im, 's5_log_dt': s5_log_dt,
        's5_b': s5_b, 's5_c': s5_c, 's5_d': s5_d, 's5_w_glu': s5_w_glu, 's5_b_glu': s5_b_glu,
        'rw_mu': rw_mu, 'rw_w0': rw_w0, 'rw_w2': rw_w2, 'rw_a0': rw_a0, 'rw_a2': rw_a2, 'rw_g2': rw_g2,
        'rw_k_k': rw_k_k, 'rw_k_a': rw_k_a, 'rw_r_k': rw_r_k, 'rw_ln_w': rw_ln_w, 'rw_ln_b': rw_ln_b,
        'sg_ln_w': sg_ln_w, 'sg_ln_b': sg_ln_b, 'sg_w_s': sg_w_s, 'sg_b_s': sg_b_s,
        'gd_conv_w': gd_conv_w, 'gd_a_log': gd_a_log, 'gd_dt_bias': gd_dt_bias, 'gd_norm_w': gd_norm_w,
        'w_branch': w_branch, 'w_out': w_out, 'norm2_w': norm2_w,
        'ffn_w1': ffn_w1, 'ffn_w3': ffn_w3, 'ffn_w2': ffn_w2,
        'moe_router': moe_router, 'moe_router_b': moe_router_b, 'moe_w1': moe_w1, 'moe_w3': moe_w3, 'moe_w2': moe_w2,
        'final_norm_w': final_norm_w,
    }
    w_in_perm = [_permute_w_in(w_in[layer]) for layer in range(2)]
    bp, dt = x_prompt.shape[0], x_prompt.dtype
    depth = w_in.shape[0]
    y_prompt, (s5_p, rw_p, rwsh_p, gd_p, gdc_p, _) = _run_trunk(
        x_prompt,
        jnp.zeros((depth, bp, S5_GROUPS, S5_STATE, 2), dt),
        jnp.zeros((depth, bp, HEADS, HEAD_W, HEAD_W), dt),
        jnp.zeros((depth, bp, RW_COLS), dt),
        jnp.zeros((depth, bp, HEADS, HEAD_W, HEAD_W), dt),
        jnp.zeros((depth, bp, GD_CONV - 1, GD_QKV), dt),
        p, w_in_perm)
    y_sample, (s5_s, rw_s, rwsh_s, gd_s, gdc_s, sgv_s) = _run_trunk(
        x_sample, state_s5, state_rwkv, state_rwkv_shift, state_gdn, state_gdn_conv, p, w_in_perm)
    return (y_prompt, y_sample, s5_p, rw_p, rwsh_p, gd_p, gdc_p, s5_s, rw_s, rwsh_s, gd_s, gdc_s, sgv_s)
```

```python
import functools

import jax
import jax.numpy as jnp
from jax import lax
from jax.experimental import pallas as pl
from jax.experimental.pallas import tpu as pltpu
from jax.experimental.pallas import tpu_sc as plsc

F32 = jnp.float32
BF16 = jnp.bfloat16

D_MODEL = 1024
BRANCH_W = 256
HEADS = 4
HEAD_W = 64
S5_GROUPS = 16
S5_GROUP = 16
S5_STATE = 64
S5_W = S5_GROUPS * S5_STATE
S5_ROWS = 8
SG_CHUNK = 128
SG_ROWS = 512
GD_CONV = 4
GD_QKV = 3 * BRANCH_W
RW_COLS = 1024
RW_EPS = 64e-5
NORM_EPS = 1e-6
N_EXPERTS = 8
LANES = 128

COL_RW = 0
COL_GD = 1024
COL_SG = 2048
COL_S5 = 2560
COL_AB = 2816
PROJ_COLS = 3072

VMEM_LIMIT = 48 * 1024 * 1024


def _cparams(sem):
    return pltpu.CompilerParams(dimension_semantics=sem, vmem_limit_bytes=VMEM_LIMIT)


def _dot(a, b):
    return jnp.dot(a.astype(BF16), b.astype(BF16), preferred_element_type=F32)


def _dot_nt(a, b):
    return lax.dot_general(a.astype(BF16), b.astype(BF16), (((1,), (1,)), ((), ())),
                           preferred_element_type=F32)


def _split3(a):
    hi = a.astype(BF16)
    r1 = a - hi.astype(F32)
    mid = r1.astype(BF16)
    lo = (r1 - mid.astype(F32)).astype(BF16)
    return hi, mid, lo


def _dot3_left(b_exact, a):
    hi, mid, lo = _split3(a)
    b = b_exact.astype(BF16)
    return (jnp.dot(b, hi, preferred_element_type=F32) + jnp.dot(b, mid, preferred_element_type=F32)
            + jnp.dot(b, lo, preferred_element_type=F32))


def _dot_hp(a, b):
    a0, a1, a2 = _split3(a)
    b0, b1, b2 = _split3(b)
    d = lambda x, y: jnp.dot(x, y, preferred_element_type=F32)
    return d(a0, b0) + (d(a0, b1) + d(a1, b0)) + (d(a0, b2) + d(a1, b1) + d(a2, b0))


def _iota(shape, axis):
    return lax.broadcasted_iota(jnp.int32, shape, axis)


def _head_ones():
    r = _iota((BRANCH_W, BRANCH_W), 0) // HEAD_W
    c = _iota((BRANCH_W, BRANCH_W), 1) // HEAD_W
    return (r == c).astype(BF16)


def _head_mask(l):
    r = _iota((HEADS * l, BRANCH_W), 0) // l
    c = _iota((HEADS * l, BRANCH_W), 1) // HEAD_W
    return r == c


def _expand(x, mask):
    return jnp.where(mask, jnp.concatenate([x] * HEADS, axis=0), 0.0)


def _tri_masks(l):
    i = _iota((l, HEADS * l), 0)
    j = _iota((l, HEADS * l), 1) % l
    return j < i, j <= i


def _same_head(l):
    n = HEADS * l
    return (_iota((n, n), 0) // l) == (_iota((n, n), 1) // l)


def _expand_sq(x, same):
    return jnp.where(same, jnp.concatenate([x] * HEADS, axis=0), 0.0)


def _unit_lower_inverse(a_strict, l):
    i = _iota((l, HEADS * l), 0)
    j = _iota((l, HEADS * l), 1) % l
    eye = (i == j).astype(F32)
    same = _same_head(l)
    p = [-a for a in a_strict]
    t = [eye + x for x in p]
    k = 2
    while k < l:
        p = [_dot(x, _expand_sq(x, same)) for x in p]
        t = [y + _dot(y, _expand_sq(x, same)) for y, x in zip(t, p)]
        k *= 2
    return t


def _dot2(a, b_exact):
    hi = a.astype(BF16)
    lo = (a - hi.astype(F32)).astype(BF16)
    b = b_exact.astype(BF16)
    return jnp.dot(hi, b, preferred_element_type=F32) + jnp.dot(lo, b, preferred_element_type=F32)


def _cumsum_rows(x, l, nb=1):
    n = nb * l
    i, j = _iota((n, n), 0), _iota((n, n), 1)
    tri = ((j <= i) & ((i // l) == (j // l))).astype(BF16)
    return _dot3_left(tri, x)


def _softplus(x):
    return jnp.maximum(x, 0.0) + jnp.log(1.0 + jnp.exp(-jnp.abs(x)))


def _silu(x):
    return x * jax.nn.sigmoid(x)


def _norm_proj_kernel(x_ref, nw_ref, w_ref, o_ref, h_ref):
    @pl.when(pl.program_id(1) == 0)
    def _():
        x = x_ref[...]
        ms = jnp.mean(x * x, axis=-1, keepdims=True)
        h_ref[...] = (x * lax.rsqrt(ms + NORM_EPS) * nw_ref[...]).astype(BF16)

    o_ref[...] = jnp.dot(h_ref[...], w_ref[...], preferred_element_type=F32)


def _norm_proj(x2, norm_w, w_bf16, tm, tn):
    n = x2.shape[0]
    ncol = w_bf16.shape[1]
    return pl.pallas_call(
        _norm_proj_kernel,
        grid=(n // tm, ncol // tn),
        in_specs=[pl.BlockSpec((tm, D_MODEL), lambda i, j: (i, 0)),
                  pl.BlockSpec((1, D_MODEL), lambda i, j: (0, 0)),
                  pl.BlockSpec((D_MODEL, tn), lambda i, j: (0, j))],
        out_specs=pl.BlockSpec((tm, tn), lambda i, j: (i, j)),
        out_shape=jax.ShapeDtypeStruct((n, ncol), F32),
        scratch_shapes=[pltpu.VMEM((tm, D_MODEL), BF16)],
        compiler_params=_cparams(("parallel", "arbitrary")),
    )(x2, norm_w.reshape(1, D_MODEL), w_bf16)


def _s5_kernel(u_ref, h0_ref, wb_ref, wc_ref, a2k_ref, apow_ref, d_ref, wg_ref, bg_ref,
               y_ref, hl_ref, hs_ref, hb_ref, *, tc):
    c = pl.program_id(1)

    @pl.when(c == 0)
    def _():
        hs_ref[...] = h0_ref[...]

    u = u_ref[...]
    x = _dot(u, wb_ref[...])
    xr, xi = x[:, :S5_W], x[:, S5_W:]
    ng = tc // S5_ROWS
    xr = xr.reshape(ng, S5_ROWS, S5_W)
    xi = xi.reshape(ng, S5_ROWS, S5_W)
    row = _iota((S5_ROWS, 1), 0)
    k, d = 0, 1
    while d < S5_ROWS:
        m = row >= d
        ar = jnp.where(m, a2k_ref[k:k + 1, :S5_W], 0.0)
        ai = jnp.where(m, a2k_ref[k:k + 1, S5_W:], 0.0)
        sr = pltpu.roll(xr, d, axis=1)
        si = pltpu.roll(xi, d, axis=1)
        xr, xi = xr + (ar * sr - ai * si), xi + (ar * si + ai * sr)
        k, d = k + 1, d * 2
    hb_ref[:, :S5_W] = xr.reshape(tc, S5_W)
    hb_ref[:, S5_W:] = xi.reshape(tc, S5_W)
    pr, pi_ = apow_ref[:, :S5_W], apow_ref[:, S5_W:]

    def group(gi, carry):
        cr, ci = carry
        rows = pl.ds(pl.multiple_of(gi * S5_ROWS, S5_ROWS), S5_ROWS)
        hr = hb_ref[rows, :S5_W] + (pr * cr - pi_ * ci)
        hi = hb_ref[rows, S5_W:] + (pr * ci + pi_ * cr)
        hb_ref[rows, :S5_W] = hr
        hb_ref[rows, S5_W:] = hi
        return hr[S5_ROWS - 1:S5_ROWS], hi[S5_ROWS - 1:S5_ROWS]

    cr, ci = lax.fori_loop(0, tc // S5_ROWS, group, (hs_ref[:, :S5_W], hs_ref[:, S5_W:]), unroll=4)
    hs_ref[:, :S5_W] = cr
    hs_ref[:, S5_W:] = ci
    y = _dot(hb_ref[:, :S5_W], wc_ref[:S5_W]) + _dot(hb_ref[:, S5_W:], wc_ref[S5_W:])
    y = jax.nn.gelu(y + d_ref[...] * u)
    y = y * jax.nn.sigmoid(_dot(y, wg_ref[...]) + bg_ref[...])
    y_ref[...] = y.astype(y_ref.dtype)

    @pl.when(c == pl.num_programs(1) - 1)
    def _():
        hl_ref[...] = hs_ref[...]


def _s5_tables(lam_re, lam_im, log_dt, b_c, c_c, tc):
    dt = jnp.exp(log_dt)[:, None]
    mag = jnp.exp(lam_re * dt)
    ab_re, ab_im = mag * jnp.cos(lam_im * dt), mag * jnp.sin(lam_im * dt)
    den = lam_re * lam_re + lam_im * lam_im
    nr = ab_re - 1.0
    cf_re = (nr * lam_re + ab_im * lam_im) / den
    cf_im = (ab_im * lam_re - nr * lam_im) / den
    br, bi = b_c[..., 0], b_c[..., 1]
    bb_re = cf_re[..., None] * br - cf_im[..., None] * bi
    bb_im = cf_re[..., None] * bi + cf_im[..., None] * br
    eye = jnp.eye(S5_GROUPS, dtype=F32)
    bd_in = lambda m: jnp.einsum('gph,gk->ghkp', m, eye).reshape(BRANCH_W, S5_W)
    wb = jnp.concatenate([bd_in(bb_re), bd_in(bb_im)], axis=1)
    cr, ci = c_c[..., 0], c_c[..., 1]
    bd_out = lambda m: jnp.einsum('ghp,gk->gpkh', m, eye).reshape(S5_W, BRANCH_W)
    wc = jnp.concatenate([bd_out(cr), -bd_out(ci)], axis=0)
    pr, pi_ = ab_re.reshape(1, S5_W), ab_im.reshape(1, S5_W)
    lv_r, lv_i = [], []
    tr, ti = pr, pi_
    d = 1
    while d < tc:
        lv_r.append(pr)
        lv_i.append(pi_)
        tr, ti = (jnp.concatenate([tr, tr * pr - ti * pi_], axis=0),
                  jnp.concatenate([ti, tr * pi_ + ti * pr], axis=0))
        pr, pi_ = pr * pr - pi_ * pi_, 2.0 * pr * pi_
        d *= 2
    n_lv = len(lv_r)
    pad = (-n_lv) % 8
    a2k = jnp.concatenate([jnp.concatenate(lv_r, axis=0), jnp.concatenate(lv_i, axis=0)], axis=1)
    a2k = jnp.pad(a2k, ((0, pad), (0, 0)))
    apow = jnp.concatenate([tr, ti], axis=1)
    return wb.astype(BF16), wc.astype(BF16), a2k, apow


def _s5_mixer(p3, h0, lam_re, lam_im, log_dt, b_c, c_c, d_skip, w_glu, b_glu, tc):
    bsz, t, _ = p3.shape
    wb, wc, a2k, apow = _s5_tables(lam_re, lam_im, log_dt, b_c, c_c, S5_ROWS)
    h0f = jnp.concatenate([h0[..., 0].reshape(bsz, 1, S5_W), h0[..., 1].reshape(bsz, 1, S5_W)], axis=-1)
    full = lambda a: pl.BlockSpec(a.shape, lambda b, c: (0,) * a.ndim)
    d2, bg2, wg = d_skip.reshape(1, BRANCH_W), b_glu.reshape(1, BRANCH_W), w_glu.astype(BF16)
    y, hl = pl.pallas_call(
        functools.partial(_s5_kernel, tc=tc),
        grid=(bsz, t // tc),
        in_specs=[pl.BlockSpec((None, tc, BRANCH_W), lambda b, c: (b, c, COL_S5 // BRANCH_W)),
                  pl.BlockSpec((None, 1, 2 * S5_W), lambda b, c: (b, 0, 0)),
                  full(wb), full(wc), full(a2k), full(apow), full(d2), full(wg), full(bg2)],
        out_specs=[pl.BlockSpec((None, tc, BRANCH_W), lambda b, c: (b, c, 0)),
                   pl.BlockSpec((None, 1, 2 * S5_W), lambda b, c: (b, 0, 0))],
        out_shape=[jax.ShapeDtypeStruct((bsz, t, BRANCH_W), BF16),
                   jax.ShapeDtypeStruct((bsz, 1, 2 * S5_W), F32)],
        scratch_shapes=[pltpu.VMEM((1, 2 * S5_W), F32), pltpu.VMEM((tc, 2 * S5_W), F32)],
        compiler_params=_cparams(("parallel", "arbitrary")),
    )(p3, h0f, wb, wc, a2k, apow, d2, wg, bg2)
    h_last = jnp.stack([hl[:, 0, :S5_W].reshape(bsz, S5_GROUPS, S5_STATE),
                        hl[:, 0, S5_W:].reshape(bsz, S5_GROUPS, S5_STATE)], axis=-1)
    return y, h_last


def _rwkv_kernel(z_ref, sh0_ref, s0_ref, mu_ref, w0_ref, w2_ref, a0_ref, a2_ref, g2_ref,
                 kk_ref, ka_ref, rk_ref, lnw_ref, lnb_ref,
                 y_ref, sl_ref, st_ref, zp_ref, *, l, bb):
    c = pl.program_id(1)

    @pl.when(c == 0)
    def _():
        st_ref[...] = s0_ref[...]
        zp_ref[...] = sh0_ref[...]

    row = _iota((l, 1), 0)
    zms = []
    for b in range(bb):
        z = z_ref[b]
        prev = jnp.where(row == 0, zp_ref[b], pltpu.roll(z, 1, axis=0))
        zp_ref[b] = z[l - 1:l]
        zms.append(z + (prev - z) * mu_ref[...])
    zm = jnp.concatenate(zms, axis=0)
    r, k, v = zm[:, 0:256], zm[:, 256:512], zm[:, 512:768]
    lo = zm[:, 768:896]
    g_lo = zm[:, 896:1024]
    w_log = -_softplus(-(w0_ref[...] + _dot(jnp.tanh(lo), w2_ref[...]))) - 0.5
    lw = -jnp.exp(w_log)
    a = jax.nn.sigmoid(a0_ref[...] + _dot(lo, a2_ref[...]))
    g = _dot(jax.nn.sigmoid(g_lo), g2_ref[...])
    ones_h = _head_ones()
    kk = k * kk_ref[...]
    kk = kk * lax.rsqrt(_dot2(kk * kk, ones_h) + NORM_EPS)
    k = k * (1.0 + (a - 1.0) * ka_ref[...])
    kka = kk * a

    cum = _cumsum_rows(lw, l, bb)
    p_incl = jnp.exp(cum)
    p_inv = jnp.exp(-cum)
    kt = kk * jnp.exp(cum - lw)
    rt = r * p_incl
    kh = k * p_inv
    ah = kka * p_inv

    hm = _head_mask(l)
    strict, incl = _tri_masks(l)
    streams = range(bb)
    rows = [slice(b * l, (b + 1) * l) for b in streams]
    ex = lambda x: [_expand(x[s], hm) for s in rows]
    kt_b, rt_b, v_b = [kt[s] for s in rows], [rt[s] for s in rows], [v[s] for s in rows]
    kh_e, ah_e, v_e = ex(kh), ex(ah), ex(v)
    a_aa = [jnp.where(strict, _dot_nt(kt_b[b], ah_e[b]), 0.0) for b in streams]
    a_ak = [jnp.where(strict, _dot_nt(kt_b[b], kh_e[b]), 0.0) for b in streams]
    b_ra = [jnp.where(incl, _dot_nt(rt_b[b], ah_e[b]), 0.0) for b in streams]
    b_rk = [jnp.where(incl, _dot_nt(rt_b[b], kh_e[b]), 0.0) for b in streams]
    t_inv = _unit_lower_inverse(a_aa, l)
    st = [st_ref[b] for b in streams]
    rhs = [_dot_nt(kt_b[b], st[b]) + _dot(a_ak[b], v_e[b]) for b in streams]
    yb = [_dot_nt(rt_b[b], st[b]) + _dot(b_rk[b], v_e[b]) for b in streams]
    u = [_dot(t_inv[b], _expand(rhs[b], hm)) for b in streams]
    yb = [yb[b] - _dot(b_ra[b], _expand(u[b], hm)) for b in streams]
    to_end = [jnp.exp(cum[s][l - 1:l] - cum[s]) for s in rows]
    lhs_t = [jnp.concatenate([v_b[b], -u[b]], axis=0).T for b in streams]
    rhs_k = [jnp.concatenate([k[rows[b]] * to_end[b], kka[rows[b]] * to_end[b]], axis=0) for b in streams]
    head_blk = _same_head(HEAD_W)
    for b in streams:
        p_last = p_incl[(b + 1) * l - 1:(b + 1) * l]
        st_ref[b] = st[b] * p_last + jnp.where(head_blk, _dot(lhs_t[b], rhs_k[b]), 0.0)
    y = jnp.concatenate(yb, axis=0)

    inv_w = 1.0 / HEAD_W
    mean = _dot2(y, ones_h) * inv_w
    yc = y - mean
    var = _dot2(yc * yc, ones_h) * inv_w
    y = yc * lax.rsqrt(var + RW_EPS) * lnw_ref[...] + lnb_ref[...]
    bonus = _dot2(r * k * rk_ref[...], ones_h) * v
    y = ((y + bonus) * g).astype(y_ref.dtype)
    for b in range(bb):
        y_ref[b] = y[b * l:(b + 1) * l]

    @pl.when(c == pl.num_programs(1) - 1)
    def _():
        sl_ref[...] = st_ref[...]


STACK_ROWS = 256


def _streams_per_step(bsz, l):
    bb = max(1, min(bsz, STACK_ROWS // l))
    while bsz % bb:
        bb -= 1
    return bb


def _block_diag_heads(s):
    bsz = s.shape[0]
    eye = jnp.eye(HEADS, dtype=s.dtype)
    return jnp.einsum('bhij,hg->bhigj', s, eye).reshape(bsz, BRANCH_W, BRANCH_W)


def _diag_blocks(s):
    bsz = s.shape[0]
    s5 = s.reshape(bsz, HEADS, HEAD_W, HEADS, HEAD_W)
    return jnp.stack([s5[:, h, :, h, :] for h in range(HEADS)], axis=1)


def _rwkv_mixer(p3, shift0, s0, mu, w0, w2, a0, a2, g2, k_k, k_a, r_k, ln_w, ln_b, l):
    bsz, t, _ = p3.shape
    row = lambda a: a.reshape(1, -1)
    w2p = jnp.concatenate([w2, jnp.zeros_like(w2)], axis=0).astype(BF16)
    a2p = jnp.concatenate([jnp.zeros_like(a2), a2], axis=0).astype(BF16)
    args = (shift0.reshape(bsz, 1, RW_COLS), _block_diag_heads(s0), row(mu), row(w0), w2p, row(a0), a2p,
            g2.astype(BF16), row(k_k), row(k_a), row(r_k), row(ln_w), row(ln_b))
    full = lambda a: pl.BlockSpec(a.shape, lambda b, c: (0,) * a.ndim)
    bb = _streams_per_step(bsz, l)
    y, sl = pl.pallas_call(
        functools.partial(_rwkv_kernel, l=l, bb=bb),
        grid=(bsz // bb, t // l),
        in_specs=[pl.BlockSpec((bb, l, RW_COLS), lambda b, c: (b, c, COL_RW // RW_COLS)),
                  pl.BlockSpec((bb, 1, RW_COLS), lambda b, c: (b, 0, 0)),
                  pl.BlockSpec((bb, BRANCH_W, BRANCH_W), lambda b, c: (b, 0, 0))]
                 + [full(a) for a in args[2:]],
        out_specs=[pl.BlockSpec((bb, l, BRANCH_W), lambda b, c: (b, c, 0)),
                   pl.BlockSpec((bb, BRANCH_W, BRANCH_W), lambda b, c: (b, 0, 0))],
        out_shape=[jax.ShapeDtypeStruct((bsz, t, BRANCH_W), BF16),
                   jax.ShapeDtypeStruct((bsz, BRANCH_W, BRANCH_W), F32)],
        scratch_shapes=[pltpu.VMEM((bb, BRANCH_W, BRANCH_W), F32), pltpu.VMEM((bb, 1, RW_COLS), F32)],
        compiler_params=_cparams(("parallel", "arbitrary")),
    )(p3, *args)
    return y, _diag_blocks(sl)


def _sgu_kernel(z_ref, lnw_ref, lnb_ref, wm_ref, bias_ref, o_ref, v_ref, *, l, nc):
    zg = jax.nn.gelu(z_ref[...])
    u, v = zg[:, :BRANCH_W], zg[:, BRANCH_W:]
    mean = jnp.mean(v, axis=-1, keepdims=True)
    vc = v - mean
    var = jnp.mean(vc * vc, axis=-1, keepdims=True)
    v = vc * lax.rsqrt(var + NORM_EPS) * lnw_ref[...] + lnb_ref[...]
    v_ref[...] = v
    hm = _head_mask(l)
    wm = wm_ref[...]
    for i in range(nc):
        rows = slice(i * l, (i + 1) * l)
        mixed = bias_ref[...] + _dot(wm, _expand(v[rows], hm))
        o_ref[rows, :] = (u[rows] * mixed).astype(o_ref.dtype)


def _sgu_mixer(p3, ln_w, ln_b, w_s, b_s):
    bsz, t, _ = p3.shape
    l = min(SG_CHUNK, t)
    nc = max(1, min(SG_ROWS, t) // l)
    tril = jnp.tril(jnp.ones((l, l), F32))
    wm = jnp.transpose(w_s[:, :l, :l] * tril, (1, 0, 2)).reshape(l, HEADS * l).astype(BF16)
    bias = jnp.repeat(jnp.transpose(b_s[:, :l]), HEAD_W, axis=1)
    row = lambda a: a.reshape(1, -1)
    full = lambda a: pl.BlockSpec(a.shape, lambda b, c: (0,) * a.ndim)
    args = (row(ln_w), row(ln_b), wm, bias)
    return pl.pallas_call(
        functools.partial(_sgu_kernel, l=l, nc=nc),
        grid=(bsz, t // (nc * l)),
        in_specs=[pl.BlockSpec((None, nc * l, 2 * BRANCH_W), lambda b, c: (b, c, COL_SG // (2 * BRANCH_W)))]
                 + [full(a) for a in args],
        out_specs=[pl.BlockSpec((None, nc * l, BRANCH_W), lambda b, c: (b, c, 0)),
                   pl.BlockSpec((None, nc * l, BRANCH_W), lambda b, c: (b, c, 0))],
        out_shape=[jax.ShapeDtypeStruct((bsz, t, BRANCH_W), BF16),
                   jax.ShapeDtypeStruct((bsz, t, BRANCH_W), F32)],
        compiler_params=_cparams(("parallel", "parallel")),
    )(p3, *args)


def _gdn_kernel(z_ref, ab_ref, cv0_ref, s0_ref, cw_ref, alog_ref, dtb_ref, nw_ref,
                y_ref, sl_ref, st_ref, cv_ref, *, l, bb):
    c = pl.program_id(1)

    @pl.when(c == 0)
    def _():
        st_ref[...] = s0_ref[...]
        cv_ref[...] = cv0_ref[...]

    row8 = _iota((8, 1), 0)
    convs, gates = [], []
    for b in range(bb):
        z = z_ref[b]
        qkv = z[:, :GD_QKV]
        gates.append(z[:, GD_QKV:])
        carry = cv_ref[b]
        cv_ref[b] = qkv[l - 8:l]
        conv = qkv * cw_ref[GD_CONV - 1:GD_CONV]
        for j in range(1, GD_CONV):
            sh = pltpu.roll(qkv, j, axis=0)
            top = jnp.where(row8 < j, pltpu.roll(carry, j, axis=0), sh[:8])
            sh = jnp.concatenate([top, sh[8:]], axis=0) if l > 8 else top
            conv = conv + sh * cw_ref[GD_CONV - 1 - j:GD_CONV - j]
        convs.append(conv)
    conv = _silu(jnp.concatenate(convs, axis=0))
    gate = jnp.concatenate(gates, axis=0)
    q, k, v = conv[:, :256], conv[:, 256:512], conv[:, 512:768]
    ones_h = _head_ones()
    q = q * lax.rsqrt(_dot2(q * q, ones_h) + NORM_EPS) * (HEAD_W ** -0.5)
    k = k * lax.rsqrt(_dot2(k * k, ones_h) + NORM_EPS)
    ab = jnp.concatenate([ab_ref[b] for b in range(bb)], axis=0)
    lane_h = _iota((bb * l, BRANCH_W), 1) // HEAD_W
    a_in = jnp.zeros((bb * l, BRANCH_W), F32)
    b_in = jnp.zeros((bb * l, BRANCH_W), F32)
    for h in range(HEADS):
        a_in = jnp.where(lane_h == h, ab[:, h:h + 1], a_in)
        b_in = jnp.where(lane_h == h, ab[:, HEADS + h:HEADS + h + 1], b_in)
    beta = jax.nn.sigmoid(b_in)
    g = -jnp.exp(alog_ref[...]) * _softplus(a_in + dtb_ref[...])
    gc = _cumsum_rows(g, l, bb)
    eg = jnp.exp(gc)
    kb = k * beta
    vb = v * beta
    kbg = kb * eg
    qg = q * eg

    hm = _head_mask(l)
    strict, incl = _tri_masks(l)
    n = HEADS * l
    streams = range(bb)
    rows = [slice(b * l, (b + 1) * l) for b in streams]
    ex = lambda x: [_expand(x[s], hm) for s in rows]
    k_e, vb_e, kbg_e = ex(k), ex(vb), ex(kbg)
    lane_hd = _iota((l, n), 1) // l
    eye = _iota((l, n), 0) == (_iota((l, n), 1) % l)
    decay = []
    for s in rows:
        gi = jnp.zeros((l, n), F32)
        for h in range(HEADS):
            gi = jnp.where(lane_hd == h, gc[s][:, h * HEAD_W:h * HEAD_W + 1], gi)
        gj = jnp.sum(jnp.where(eye, gi, 0.0), axis=0, keepdims=True)
        decay.append(jnp.where(incl, jnp.exp(jnp.where(incl, gi - gj, 0.0)), 0.0))
    lm = [jnp.where(strict, _dot_nt(kb[rows[b]], k_e[b]) * decay[b], 0.0) for b in streams]
    qk = [_dot_nt(q[rows[b]], k_e[b]) * decay[b] for b in streams]
    t_inv = _unit_lower_inverse(lm, l)
    uc = [_dot(t_inv[b], vb_e[b]) for b in streams]
    wc = [_dot(t_inv[b], kbg_e[b]) for b in streams]
    st = [st_ref[b] for b in streams]
    ob = [_dot(qg[rows[b]], st[b]) for b in streams]
    v_new = [uc[b] - _dot(wc[b], st[b]) for b in streams]
    ob = [ob[b] + _dot(qk[b], _expand(v_new[b], hm)) for b in streams]
    g_last = [gc[s][l - 1:l] for s in rows]
    k_dec = [(k[rows[b]] * jnp.exp(g_last[b] - gc[rows[b]])).T for b in streams]
    head_blk = _same_head(HEAD_W)
    for b in streams:
        st_ref[b] = st[b] * jnp.exp(g_last[b]) + jnp.where(head_blk, _dot(k_dec[b], v_new[b]), 0.0)
    o = jnp.concatenate(ob, axis=0)
    ms = _dot2(o * o, ones_h) * (1.0 / HEAD_W)
    o = (o * lax.rsqrt(ms + NORM_EPS) * nw_ref[...] * _silu(gate)).astype(y_ref.dtype)
    for b in range(bb):
        y_ref[b] = o[b * l:(b + 1) * l]

    @pl.when(c == pl.num_programs(1) - 1)
    def _():
        sl_ref[...] = st_ref[...]


def _gdn_mixer(p3, conv0, s0, conv_w, a_log, dt_bias, norm_w, l):
    bsz, t, _ = p3.shape
    cv0 = jnp.pad(conv0, ((0, 0), (8 - (GD_CONV - 1), 0), (0, 0)))
    cw = jnp.pad(conv_w, ((0, 8 - GD_CONV), (0, 0)))
    per_head = lambda a: jnp.repeat(a, HEAD_W).reshape(1, BRANCH_W)
    args = (cv0, _block_diag_heads(s0), cw, per_head(a_log), per_head(dt_bias),
            jnp.tile(norm_w, HEADS).reshape(1, BRANCH_W))
    full = lambda a: pl.BlockSpec(a.shape, lambda b, c: (0,) * a.ndim)
    bb = _streams_per_step(bsz, l)
    y, sl = pl.pallas_call(
        functools.partial(_gdn_kernel, l=l, bb=bb),
        grid=(bsz // bb, t // l),
        in_specs=[pl.BlockSpec((bb, l, 1024), lambda b, c: (b, c, COL_GD // 1024)),
                  pl.BlockSpec((bb, l, LANES), lambda b, c: (b, c, COL_AB // LANES)),
                  pl.BlockSpec((bb, 8, GD_QKV), lambda b, c: (b, 0, 0)),
                  pl.BlockSpec((bb, BRANCH_W, BRANCH_W), lambda b, c: (b, 0, 0))]
                 + [full(a) for a in args[2:]],
        out_specs=[pl.BlockSpec((bb, l, BRANCH_W), lambda b, c: (b, c, 0)),
                   pl.BlockSpec((bb, BRANCH_W, BRANCH_W), lambda b, c: (b, 0, 0))],
        out_shape=[jax.ShapeDtypeStruct((bsz, t, BRANCH_W), BF16),
                   jax.ShapeDtypeStruct((bsz, BRANCH_W, BRANCH_W), F32)],
        scratch_shapes=[pltpu.VMEM((bb, BRANCH_W, BRANCH_W), F32), pltpu.VMEM((bb, 8, GD_QKV), F32)],
        compiler_params=_cparams(("parallel", "arbitrary")),
    )(p3, p3, *args)
    return y, _diag_blocks(sl)


def _merge_kernel(ya_ref, yb_ref, yc_ref, yd_ref, x_ref, nw_ref, wg_ref, wbr_ref, wout_ref, o_ref):
    x = x_ref[...]
    ms = jnp.mean(x * x, axis=-1, keepdims=True)
    h = (x * lax.rsqrt(ms + NORM_EPS) * nw_ref[...]).astype(BF16)
    m = None
    for b, y_ref in enumerate((ya_ref, yb_ref, yc_ref, yd_ref)):
        gate = jnp.dot(h, wg_ref[:, b * D_MODEL:(b + 1) * D_MODEL], preferred_element_type=F32)
        br = jnp.dot(y_ref[...], wbr_ref[b], preferred_element_type=F32)
        term = (0.5 * jnp.tanh(0.5 * gate) + 0.5) * br
        m = term if m is None else m + term
    o_ref[...] = x + jnp.dot(m.astype(BF16), wout_ref[...], preferred_element_type=F32)


def _merge(ys, x2, norm_w, w_gate, w_branch, w_out, tm):
    n = x2.shape[0]
    yspec = pl.BlockSpec((tm, BRANCH_W), lambda i: (i, 0))
    once = pl.Buffered(1)
    return pl.pallas_call(
        _merge_kernel,
        grid=(n // tm,),
        in_specs=[yspec, yspec, yspec, yspec,
                  pl.BlockSpec((tm, D_MODEL), lambda i: (i, 0)),
                  pl.BlockSpec((1, D_MODEL), lambda i: (0, 0)),
                  pl.BlockSpec((D_MODEL, 4 * D_MODEL), lambda i: (0, 0), pipeline_mode=once),
                  pl.BlockSpec((4, BRANCH_W, D_MODEL), lambda i: (0, 0, 0), pipeline_mode=once),
                  pl.BlockSpec((D_MODEL, D_MODEL), lambda i: (0, 0), pipeline_mode=once)],
        out_specs=pl.BlockSpec((tm, D_MODEL), lambda i: (i, 0)),
        out_shape=jax.ShapeDtypeStruct((n, D_MODEL), F32),
        compiler_params=_cparams(("parallel",)),
    )(*[y.reshape(n, BRANCH_W) for y in ys], x2, norm_w.reshape(1, D_MODEL), w_gate,
      w_branch.astype(BF16), w_out.astype(BF16))


def _ffn_kernel(x_ref, nw_ref, w1_ref, w3_ref, w2_ref, o_ref, h_ref):
    j = pl.program_id(1)

    @pl.when(j == 0)
    def _():
        x = x_ref[...]
        ms = jnp.mean(x * x, axis=-1, keepdims=True)
        h_ref[...] = (x * lax.rsqrt(ms + NORM_EPS) * nw_ref[...]).astype(BF16)
        o_ref[...] = x

    h = h_ref[...]
    a = _silu(jnp.dot(h, w1_ref[...], preferred_element_type=F32)) * jnp.dot(h, w3_ref[...], preferred_element_type=F32)
    o_ref[...] += jnp.dot(a.astype(BF16), w2_ref[...], preferred_element_type=F32)


def _ffn(x2, norm_w, w1, w3, w2, tm, tf):
    n = x2.shape[0]
    dff = w1.shape[1]
    return pl.pallas_call(
        _ffn_kernel,
        grid=(n // tm, dff // tf),
        in_specs=[pl.BlockSpec((tm, D_MODEL), lambda i, j: (i, 0)),
                  pl.BlockSpec((1, D_MODEL), lambda i, j: (0, 0)),
                  pl.BlockSpec((D_MODEL, tf), lambda i, j: (0, j)),
                  pl.BlockSpec((D_MODEL, tf), lambda i, j: (0, j)),
                  pl.BlockSpec((tf, D_MODEL), lambda i, j: (j, 0))],
        out_specs=pl.BlockSpec((tm, D_MODEL), lambda i, j: (i, 0)),
        out_shape=jax.ShapeDtypeStruct((n, D_MODEL), F32),
        scratch_shapes=[pltpu.VMEM((tm, D_MODEL), BF16)],
        compiler_params=_cparams(("parallel", "arbitrary")),
    )(x2, norm_w.reshape(1, D_MODEL), w1.astype(BF16), w3.astype(BF16), w2.astype(BF16))


MOE_SUB = 256
MOE_TILE = 2048
MOE_PLACE_ROWS = 1024
MOE_FF = 512
ROUTER_ROWS = 256


def _router_kernel(x_ref, nw_ref, wr_ref, br_ref, h_ref, rcol_ref, gcol_ref, rrow_ref, cnt_ref, seen_ref,
                   *, tm, k):
    x = x_ref[...]
    ms = jnp.mean(x * x, axis=-1, keepdims=True)
    h = x * lax.rsqrt(ms + NORM_EPS) * nw_ref[...]
    h_ref[...] = h.astype(BF16)
    lane = _iota((tm, LANES), 1)
    logits = jnp.where(lane < N_EXPERTS, _dot_hp(h, wr_ref[...]) + br_ref[...], -jnp.inf)
    m1 = jnp.max(logits, axis=1, keepdims=True)
    i1 = jnp.min(jnp.where(logits == m1, lane, LANES), axis=1, keepdims=True)
    rest = jnp.where(lane == i1, -jnp.inf, logits)
    m2 = jnp.max(rest, axis=1, keepdims=True)
    i2 = jnp.min(jnp.where(rest == m2, lane, LANES), axis=1, keepdims=True)
    e2 = jnp.exp(m2 - m1)
    g1 = 1.0 / (1.0 + e2)
    g2 = e2 / (1.0 + e2)
    sel = (lane == i1) | (lane == i2)
    self32 = sel.astype(F32)
    gcol_ref[...] = jnp.where(lane == i1, g1, 0.0) + jnp.where(lane == i2, g2, 0.0)
    @pl.when(pl.program_id(0) % k == 0)
    def _():
        seen_ref[...] = jnp.zeros_like(seen_ref)

    seen = seen_ref[...]
    tri = (_iota((tm, tm), 1) < _iota((tm, tm), 0)).astype(BF16)
    rank = jnp.dot(tri, self32.astype(BF16), preferred_element_type=F32) + seen
    r = jnp.where(sel, rank, -1.0)
    rcol_ref[...] = r
    rrow_ref[...] = r.T[:N_EXPERTS]
    seen = seen + jnp.sum(self32, axis=0, keepdims=True)
    seen_ref[...] = seen
    cnt_ref[...] = jnp.broadcast_to(seen, (8, LANES))


def _router(x2, norm_w, w_router, b_router, tm, k):
    n = x2.shape[0]
    nt = n // (tm * k)
    wr = jnp.pad(w_router, ((0, 0), (0, LANES - N_EXPERTS)))
    br = jnp.pad(b_router, (0, LANES - N_EXPERTS)).reshape(1, LANES)
    return pl.pallas_call(
        functools.partial(_router_kernel, tm=tm, k=k),
        grid=(n // tm,),
        in_specs=[pl.BlockSpec((tm, D_MODEL), lambda i: (i, 0)),
                  pl.BlockSpec((1, D_MODEL), lambda i: (0, 0)),
                  pl.BlockSpec((D_MODEL, LANES), lambda i: (0, 0)),
                  pl.BlockSpec((1, LANES), lambda i: (0, 0))],
        out_specs=[pl.BlockSpec((tm, D_MODEL), lambda i: (i, 0)),
                   pl.BlockSpec((tm, LANES), lambda i: (i, 0)),
                   pl.BlockSpec((tm, LANES), lambda i: (i, 0)),
                   pl.BlockSpec((None, N_EXPERTS, tm), lambda i: (i // k, 0, i % k)),
                   pl.BlockSpec((None, 8, LANES), lambda i: (i, 0, 0))],
        out_shape=[jax.ShapeDtypeStruct((n, D_MODEL), BF16),
                   jax.ShapeDtypeStruct((n, LANES), F32),
                   jax.ShapeDtypeStruct((n, LANES), F32),
                   jax.ShapeDtypeStruct((nt, N_EXPERTS, tm * k), F32),
                   jax.ShapeDtypeStruct((n // tm, 8, LANES), F32)],
        scratch_shapes=[pltpu.VMEM((1, LANES), F32)],
        compiler_params=_cparams(("arbitrary",)),
    )(x2, norm_w.reshape(1, D_MODEL), wr, br)


def _moe_kernel(cnt_ref, h_ref, rrow_ref, rcol_ref, gcol_ref, w1_ref, w3_ref, w2_ref, o_ref,
                xs_ref, acc_ref, *, tm):
    i, e, c = pl.program_id(0), pl.program_id(1), pl.program_id(2)
    n_sub = (cnt_ref[i * N_EXPERTS + e] + (MOE_SUB - 1)) // MOE_SUB

    @pl.when((e == 0) & (c == 0))
    def _():
        o_ref[...] = jnp.zeros_like(o_ref)

    @pl.when(c == 0)
    def _():
        rrow = rrow_ref[pl.ds(e, 1), :]
        slot = _iota((MOE_SUB, 1), 0).astype(F32)

        def gather(s, carry):
            base = pl.multiple_of(s * MOE_SUB, MOE_SUB)
            pick = (rrow == slot + (s * MOE_SUB).astype(F32)).astype(BF16)
            xs_ref[pl.ds(base, MOE_SUB), :] = jnp.dot(pick, h_ref[...], preferred_element_type=F32).astype(BF16)
            acc_ref[pl.ds(base, MOE_SUB), :] = jnp.zeros((MOE_SUB, D_MODEL), F32)
            return carry

        lax.fori_loop(0, n_sub, gather, 0)

    def expert_rows(base, m):
        xb = xs_ref[pl.ds(base, m), :]
        a = (_silu(jnp.dot(xb, w1_ref[...], preferred_element_type=F32))
             * jnp.dot(xb, w3_ref[...], preferred_element_type=F32))
        acc_ref[pl.ds(base, m), :] += jnp.dot(a.astype(BF16), w2_ref[...], preferred_element_type=F32)

    def expert_pair(p, carry):
        expert_rows(pl.multiple_of(p * (2 * MOE_SUB), 2 * MOE_SUB), 2 * MOE_SUB)
        return carry

    if tm >= 2 * MOE_SUB:
        lax.fori_loop(0, n_sub // 2, expert_pair, 0)

        @pl.when(n_sub % 2 == 1)
        def _():
            expert_rows(pl.multiple_of((n_sub - 1) * MOE_SUB, MOE_SUB), MOE_SUB)
    else:
        @pl.when(n_sub > 0)
        def _():
            expert_rows(0, MOE_SUB)

    @pl.when(c == pl.num_programs(2) - 1)
    def _():
        tp = min(tm, MOE_PLACE_ROWS)
        lane = _iota((tp, LANES), 1)
        slot = _iota((1, MOE_SUB), 1).astype(F32)

        def scatter(s, carry):
            base = pl.multiple_of(s * MOE_SUB, MOE_SUB)
            a = acc_ref[pl.ds(base, MOE_SUB), :]
            hi = a.astype(BF16)
            lo = (a - hi.astype(F32)).astype(BF16)
            for r0 in range(0, tm, tp):
                rcol = jnp.sum(jnp.where(lane == e, rcol_ref[r0:r0 + tp], 0.0), axis=1, keepdims=True)
                gate = jnp.sum(jnp.where(lane == e, gcol_ref[r0:r0 + tp], 0.0), axis=1, keepdims=True)
                place = (rcol == slot + (s * MOE_SUB).astype(F32)).astype(BF16)
                back = (jnp.dot(place, hi, preferred_element_type=F32)
                        + jnp.dot(place, lo, preferred_element_type=F32))
                o_ref[r0:r0 + tp] += gate * back
            return carry

        lax.fori_loop(0, n_sub, scatter, 0)


def _moe(h2, rrow, rcol, gcol, counts, w1, w3, w2, tm, tf):
    n = h2.shape[0]
    nt = n // tm
    dff = w1.shape[2]
    grid_spec = pltpu.PrefetchScalarGridSpec(
        num_scalar_prefetch=1,
        grid=(nt, N_EXPERTS, dff // tf),
        in_specs=[pl.BlockSpec((tm, D_MODEL), lambda i, e, c, cnt: (i, 0), pipeline_mode=pl.Buffered(1)),
                  pl.BlockSpec((None, N_EXPERTS, tm), lambda i, e, c, cnt: (i, 0, 0)),
                  pl.BlockSpec((tm, LANES), lambda i, e, c, cnt: (i, 0), pipeline_mode=pl.Buffered(1)),
                  pl.BlockSpec((tm, LANES), lambda i, e, c, cnt: (i, 0), pipeline_mode=pl.Buffered(1)),
                  pl.BlockSpec((None, D_MODEL, tf), lambda i, e, c, cnt: (e, 0, c)),
                  pl.BlockSpec((None, D_MODEL, tf), lambda i, e, c, cnt: (e, 0, c)),
                  pl.BlockSpec((None, tf, D_MODEL), lambda i, e, c, cnt: (e, c, 0))],
        out_specs=pl.BlockSpec((tm, D_MODEL), lambda i, e, c, cnt: (i, 0), pipeline_mode=pl.Buffered(1)),
        scratch_shapes=[pltpu.VMEM((tm, D_MODEL), BF16), pltpu.VMEM((tm, D_MODEL), F32)],
    )
    return pl.pallas_call(
        functools.partial(_moe_kernel, tm=tm),
        grid_spec=grid_spec,
        out_shape=jax.ShapeDtypeStruct((n, D_MODEL), F32),
        compiler_params=_cparams(("parallel", "arbitrary", "arbitrary")),
    )(counts, h2, rrow, rcol, gcol, w1, w3, w2)


def _moe_ffn(x2, norm_w, w_router, b_router, w1, w3, w2):
    n = x2.shape[0]
    tm = min(MOE_TILE, n)
    k = tm // ROUTER_ROWS
    h2, rcol, gcol, rrow, cnt = _router(x2, norm_w, w_router, b_router, ROUTER_ROWS, k)
    counts = cnt[k - 1::k, 0, :N_EXPERTS].astype(jnp.int32).reshape(-1)
    return _moe(h2, rrow, rcol, gcol, counts, w1.astype(BF16), w3.astype(BF16), w2.astype(BF16), tm, MOE_FF)


SC_CORES = 2
SC_SUBCORES = 16
SC_WINDOW = 128
SC_PIECE = 256
MOE_BLOCK = 512
HALF = D_MODEL // 2


HIGH16 = -65536


def _pack_bf16_pairs(h):
    bits = lax.bitcast_convert_type(h.astype(BF16).astype(F32), jnp.int32)
    return lax.shift_right_logical(bits[:, :HALF], 16) | (bits[:, HALF:] & HIGH16)


def _unpack_bf16_pairs(w):
    lo = lax.bitcast_convert_type(lax.shift_left(w, 16), F32)
    hi = lax.bitcast_convert_type(w & HIGH16, F32)
    return jnp.concatenate([lo, hi], axis=1).astype(BF16)


def _route_kernel(x_ref, nw_ref, wr_ref, br_ref, hp_ref, tok_ref, cnt_ref, seen_ref, *, tm):
    @pl.when(pl.program_id(0) == 0)
    def _():
        seen_ref[...] = jnp.zeros_like(seen_ref)

    x = x_ref[...]
    ms = jnp.mean(x * x, axis=-1, keepdims=True)
    h = x * lax.rsqrt(ms + NORM_EPS) * nw_ref[...]
    hp_ref[...] = _pack_bf16_pairs(h)
    lane = _iota((tm, LANES), 1)
    logits = jnp.where(lane < N_EXPERTS, _dot_hp(h, wr_ref[...]) + br_ref[...], -jnp.inf)
    m1 = jnp.max(logits, axis=1, keepdims=True)
    i1 = jnp.min(jnp.where(logits == m1, lane, LANES), axis=1, keepdims=True)
    rest = jnp.where(lane == i1, -jnp.inf, logits)
    m2 = jnp.max(rest, axis=1, keepdims=True)
    i2 = jnp.min(jnp.where(rest == m2, lane, LANES), axis=1, keepdims=True)
    e2 = jnp.exp(m2 - m1)
    g1 = 1.0 / (1.0 + e2)
    g2 = e2 / (1.0 + e2)
    self32 = ((lane == i1) | (lane == i2)).astype(F32)
    seen = seen_ref[...]
    tri = (_iota((tm, tm), 1) < _iota((tm, tm), 0)).astype(BF16)
    rank = jnp.dot(tri, self32.astype(BF16), preferred_element_type=F32) + seen
    r1 = jnp.sum(jnp.where(lane == i1, rank, 0.0), axis=1, keepdims=True)
    r2 = jnp.sum(jnp.where(lane == i2, rank, 0.0), axis=1, keepdims=True)
    cols = (i1.astype(F32), i2.astype(F32), r1, r2, g1, g2)
    tok = jnp.zeros((tm, LANES), F32)
    for j, col in enumerate(cols):
        tok = jnp.where(lane == j, col, tok)
    tok_ref[...] = tok
    seen = seen + jnp.sum(self32, axis=0, keepdims=True)
    seen_ref[...] = seen
    cnt_ref[...] = jnp.broadcast_to(seen, (8, LANES))


def _route(x2, norm_w, w_router, b_router, tm):
    n = x2.shape[0]
    wr = jnp.pad(w_router, ((0, 0), (0, LANES - N_EXPERTS)))
    br = jnp.pad(b_router, (0, LANES - N_EXPERTS)).reshape(1, LANES)
    return pl.pallas_call(
        functools.partial(_route_kernel, tm=tm),
        grid=(n // tm,),
        in_specs=[pl.BlockSpec((tm, D_MODEL), lambda i: (i, 0)),
                  pl.BlockSpec((1, D_MODEL), lambda i: (0, 0)),
                  pl.BlockSpec((D_MODEL, LANES), lambda i: (0, 0)),
                  pl.BlockSpec((1, LANES), lambda i: (0, 0))],
        out_specs=[pl.BlockSpec((tm, HALF), lambda i: (i, 0)),
                   pl.BlockSpec((tm, LANES), lambda i: (i, 0)),
                   pl.BlockSpec((8, LANES), lambda i: (0, 0))],
        out_shape=[jax.ShapeDtypeStruct((n, HALF), jnp.int32),
                   jax.ShapeDtypeStruct((n, LANES), F32),
                   jax.ShapeDtypeStruct((8, LANES), F32)],
        scratch_shapes=[pltpu.VMEM((1, LANES), F32)],
        compiler_params=_cparams(("arbitrary",)),
    )(x2, norm_w.reshape(1, D_MODEL), wr, br)


def _sc_rows_multiple(d):
    return SC_CORES * SC_SUBCORES * SC_WINDOW * SC_PIECE // d


def _sc_gather_rows(table, idx):
    b, d = idx.shape[0], table.shape[1]
    f = d // SC_PIECE
    assert b % _sc_rows_multiple(d) == 0
    out = _sc_gather_pieces(table.reshape(table.shape[0] * f, SC_PIECE),
                            (idx[:, None] * f + jnp.arange(f, dtype=jnp.int32)).reshape(-1))
    return out.reshape(b, d)


def _sc_gather_pieces(table, idx):
    bp = idx.shape[0]
    d = table.shape[1]
    window = SC_WINDOW
    idx2 = idx.reshape(1, bp)
    mesh = plsc.VectorSubcoreMesh(core_axis_name="core", subcore_axis_name="subcore")

    @functools.partial(pl.kernel, out_type=jax.ShapeDtypeStruct((bp, d), table.dtype), mesh=mesh)
    def gather(x_hbm, i_hbm, o_hbm):
        def body(i_vmem, o_vmem):
            pltpu.sync_copy(x_hbm.at[i_vmem.at[0]], o_vmem)

        pltpu.emit_pipeline(
            body,
            grid=(bp // window,),
            in_specs=[pl.BlockSpec((1, window), index_map=lambda i: (0, i))],
            out_specs=[pl.BlockSpec((window, d), index_map=lambda i: (i, 0))],
            core_axis_name=("core", "subcore"),
            dimension_semantics=(pltpu.PARALLEL,),
        )(i_hbm, o_hbm)

    return gather(table, idx2)


def _experts_kernel(be_ref, nb_ref, xs_ref, w1_ref, w3_ref, w2_ref, o_ref):
    g, c = pl.program_id(0), pl.program_id(1)

    @pl.when(c == 0)
    def _():
        o_ref[...] = jnp.zeros_like(o_ref)

    @pl.when(g < nb_ref[0])
    def _():
        xb = _unpack_bf16_pairs(xs_ref[...])
        a = (_silu(jnp.dot(xb, w1_ref[...], preferred_element_type=F32))
             * jnp.dot(xb, w3_ref[...], preferred_element_type=F32))
        o_ref[...] += jnp.dot(a.astype(BF16), w2_ref[...], preferred_element_type=F32)


def _experts(xs, block_expert, n_blocks, w1, w3, w2, blk, tf):
    n_slots = xs.shape[0]
    dff = w1.shape[2]
    grid_spec = pltpu.PrefetchScalarGridSpec(
        num_scalar_prefetch=2,
        grid=(n_slots // blk, dff // tf),
        in_specs=[pl.BlockSpec((blk, HALF), lambda g, c, be, nb: (g, 0)),
                  pl.BlockSpec((None, D_MODEL, tf), lambda g, c, be, nb: (be[g], 0, c)),
                  pl.BlockSpec((None, D_MODEL, tf), lambda g, c, be, nb: (be[g], 0, c)),
                  pl.BlockSpec((None, tf, D_MODEL), lambda g, c, be, nb: (be[g], c, 0))],
        out_specs=pl.BlockSpec((blk, D_MODEL), lambda g, c, be, nb: (g, 0)),
    )
    return pl.pallas_call(
        _experts_kernel,
        grid_spec=grid_spec,
        out_shape=jax.ShapeDtypeStruct((n_slots, D_MODEL), F32),
        compiler_params=_cparams(("parallel", "arbitrary")),
    )(block_expert, n_blocks, xs, w1, w3, w2)


def _combine_kernel(x_ref, y1_ref, y2_ref, tok_ref, nw_ref, o_ref):
    tok = tok_ref[...]
    x = x_ref[...] + tok[:, 4:5] * y1_ref[...] + tok[:, 5:6] * y2_ref[...]
    ms = jnp.mean(x * x, axis=-1, keepdims=True)
    o_ref[...] = x * lax.rsqrt(ms + NORM_EPS) * nw_ref[...]


def _combine(x2, ys2, tok, norm_w, tm):
    n = x2.shape[0]
    spec = pl.BlockSpec((tm, D_MODEL), lambda i: (i, 0))
    return pl.pallas_call(
        _combine_kernel,
        grid=(n // tm,),
        in_specs=[spec, spec, pl.BlockSpec((tm, D_MODEL), lambda i: (i + n // tm, 0)),
                  pl.BlockSpec((tm, LANES), lambda i: (i, 0)), pl.BlockSpec((1, D_MODEL), lambda i: (0, 0))],
        out_specs=spec,
        out_shape=jax.ShapeDtypeStruct((n, D_MODEL), F32),
        compiler_params=_cparams(("parallel",)),
    )(x2, ys2, ys2, tok, norm_w.reshape(1, D_MODEL))


def _moe_final(x2, norm_w, w_router, b_router, w1, w3, w2, final_norm_w):
    n = x2.shape[0]
    blk = min(MOE_BLOCK, n)
    hp, tok, cnt = _route(x2, norm_w, w_router, b_router, ROUTER_ROWS)
    counts = cnt[0, :N_EXPERTS].astype(jnp.int32)
    padded = (counts + blk - 1) // blk * blk
    ends = jnp.cumsum(padded)
    base = ends - padded
    round_up = lambda a, m: -(-a // m) * m
    n_slots = round_up((-(-2 * n // blk) + N_EXPERTS) * blk, _sc_rows_multiple(HALF))
    n_blocks = n_slots // blk
    i12 = tok[:, 0:2].astype(jnp.int32)
    pos = (base[i12] + tok[:, 2:4].astype(jnp.int32)).T.reshape(-1)
    rows = jnp.tile(jnp.arange(n, dtype=jnp.int32), 2)
    src = jnp.zeros((n_slots,), jnp.int32).at[pos].set(rows)
    block_expert = jnp.minimum(
        jnp.searchsorted(ends, jnp.arange(n_blocks, dtype=jnp.int32) * blk, side='right'), N_EXPERTS - 1
    ).astype(jnp.int32)
    xs = _sc_gather_rows(hp, src)
    ys = _experts(xs, block_expert, (ends[-1:] // blk).astype(jnp.int32),
                  w1.astype(BF16), w3.astype(BF16), w2.astype(BF16), blk, MOE_FF)
    pos = jnp.pad(pos, (0, round_up(2 * n, _sc_rows_multiple(D_MODEL)) - 2 * n))
    ys2 = _sc_gather_rows(ys, pos)
    return _combine(x2, ys2, tok, final_norm_w, min(1024, n))


def _final_kernel(x_ref, m_ref, nw_ref, o_ref):
    x = x_ref[...] + m_ref[...]
    ms = jnp.mean(x * x, axis=-1, keepdims=True)
    o_ref[...] = x * lax.rsqrt(ms + NORM_EPS) * nw_ref[...]


def _final(x2, m2, norm_w, tm):
    n = x2.shape[0]
    spec = pl.BlockSpec((tm, D_MODEL), lambda i: (i, 0))
    return pl.pallas_call(
        _final_kernel,
        grid=(n // tm,),
        in_specs=[spec, spec, pl.BlockSpec((1, D_MODEL), lambda i: (0, 0))],
        out_specs=spec,
        out_shape=jax.ShapeDtypeStruct((n, D_MODEL), F32),
        compiler_params=_cparams(("parallel",)),
    )(x2, m2, norm_w.reshape(1, D_MODEL))


def _permute_w_in(w):
    s5, rw, sg = w[:, 0:256], w[:, 256:1280], w[:, 1280:1792]
    gd, ab, gates = w[:, 1792:2816], w[:, 2816:2824], w[:, 2824:6920]
    pad = jnp.zeros((D_MODEL, PROJ_COLS - COL_AB - 8), w.dtype)
    return jnp.concatenate([rw, gd, sg, s5, ab, pad], axis=1).astype(BF16), gates.astype(BF16)


def _run_trunk(x, s5_h, rw_s, rw_shift, gd_s, gd_conv, p, w_in_perm):
    bsz, t, _ = x.shape
    n = bsz * t
    tm = min(1024, n)
    l = min(64, t)
    tc_s5 = min(256, t)
    new = ([], [], [], [], [], [])
    x2 = x.reshape(n, D_MODEL)
    moe_out = None
    for layer in range(2):
        g = lambda name: p[name][layer]
        w_mix, w_gate = w_in_perm[layer]
        proj = _norm_proj(x2, g('norm1_w'), w_mix, tm, 1024)
        p3 = proj.reshape(bsz, t, PROJ_COLS)
        y_a, s5_new = _s5_mixer(p3, s5_h[layer], g('s5_lam_re'), g('s5_lam_im'), g('s5_log_dt'), g('s5_b'),
                                g('s5_c'), g('s5_d'), g('s5_w_glu'), g('s5_b_glu'), tc_s5)
        y_b, rw_new = _rwkv_mixer(p3, rw_shift[layer], rw_s[layer], g('rw_mu'), g('rw_w0'), g('rw_w2'),
                                  g('rw_a0'), g('rw_a2'), g('rw_g2'), g('rw_k_k'), g('rw_k_a'), g('rw_r_k'),
                                  g('rw_ln_w'), g('rw_ln_b'), l)
        y_c, sg_v = _sgu_mixer(p3, g('sg_ln_w'), g('sg_ln_b'), g('sg_w_s'), g('sg_b_s'))
        y_d, gd_new = _gdn_mixer(p3, gd_conv[layer], gd_s[layer], g('gd_conv_w'), g('gd_a_log'),
                                 g('gd_dt_bias'), g('gd_norm_w'), l)
        shift_new = p3[:, t - 1, COL_RW:COL_RW + RW_COLS]
        conv_new = p3[:, t - (GD_CONV - 1):, COL_GD:COL_GD + GD_QKV]
        x2 = _merge((y_a, y_b, y_c, y_d), x2, g('norm1_w'), w_gate, g('w_branch'), g('w_out'), min(512, n))
        j = layer // 2
        if layer % 2 == 0:
            x2 = _ffn(x2, g('norm2_w'), p['ffn_w1'][j], p['ffn_w3'][j], p['ffn_w2'][j], min(512, n), 1408)
        else:
            y = _moe_final(x2, g('norm2_w'), p['moe_router'][j], p['moe_router_b'][j],
                           p['moe_w1'][j], p['moe_w3'][j], p['moe_w2'][j], p['final_norm_w'])
        for lst, s in zip(new, (s5_new, rw_new, shift_new, gd_new, conv_new, sg_v)):
            lst.append(s)
    return y.reshape(bsz, t, D_MODEL), [jnp.stack(lst) for lst in new]


def kernel(x_prompt, x_sample, state_s5, state_rwkv, state_rwkv_shift, state_gdn, state_gdn_conv, norm1_w, w_in, s5_lam_re, s5_lam_im, s5_log_dt, s5_b, s5_c, s5_d, s5_w_glu, s5_b_glu, rw_mu, rw_w0, rw_w2, rw_a0, rw_a2, rw_g2, rw_k_k, rw_k_a, rw_r_k, rw_ln_w, rw_ln_b, sg_ln_w, sg_ln_b, sg_w_s, sg_b_s, gd_conv_w, gd_a_log, gd_dt_bias, gd_norm_w, w_branch, w_out, norm2_w, ffn_w1, ffn_w3, ffn_w2, moe_router, moe_router_b, moe_w1, moe_w3, moe_w2, final_norm_w):
    p = {
        'norm1_w': norm1_w, 's5_lam_re': s5_lam_re, 's5_lam_im': s5_lam_im, 's5_log_dt': s5_log_dt,
        's5_b': s5_b, 's5_c': s5_c, 's5_d': s5_d, 's5_w_glu': s5_w_glu, 's5_b_glu': s5_b_glu,
        'rw_mu': rw_mu, 'rw_w0': rw_w0, 'rw_w2': rw_w2, 'rw_a0': rw_a0, 'rw_a2': rw_a2, 'rw_g2': rw_g2,
        'rw_k_k': rw_k_k, 'rw_k_a': rw_k_a, 'rw_r_k': rw_r_k, 'rw_ln_w': rw_ln_w, 'rw_ln_b': rw_ln_b,
        'sg_ln_w': sg_ln_w, 'sg_ln_b': sg_ln_b, 'sg_w_s': sg_w_s, 'sg_b_s': sg_b_s,
        'gd_conv_w': gd_conv_w, 'gd_a_log': gd_a_log, 'gd_dt_bias': gd_dt_bias, 'gd_norm_w': gd_norm_w,
        'w_branch': w_branch, 'w_out': w_out, 'norm2_w': norm2_w,
        'ffn_w1': ffn_w1, 'ffn_w3': ffn_w3, 'ffn_w2': ffn_w2,
        'moe_router': moe_router, 'moe_router_b': moe_router_b, 'moe_w1': moe_w1, 'moe_w3': moe_w3, 'moe_w2': moe_w2,
        'final_norm_w': final_norm_w,
    }
    w_in_perm = [_permute_w_in(w_in[layer]) for layer in range(2)]
    bp, dt = x_prompt.shape[0], x_prompt.dtype
    depth = w_in.shape[0]
    y_prompt, (s5_p, rw_p, rwsh_p, gd_p, gdc_p, _) = _run_trunk(
        x_prompt,
        jnp.zeros((depth, bp, S5_GROUPS, S5_STATE, 2), dt),
        jnp.zeros((depth, bp, HEADS, HEAD_W, HEAD_W), dt),
        jnp.zeros((depth, bp, RW_COLS), dt),
        jnp.zeros((depth, bp, HEADS, HEAD_W, HEAD_W), dt),
        jnp.zeros((depth, bp, GD_CONV - 1, GD_QKV), dt),
        p, w_in_perm)
    y_sample, (s5_s, rw_s, rwsh_s, gd_s, gdc_s, sgv_s) = _run_trunk(
        x_sample, state_s5, state_rwkv, state_rwkv_shift, state_gdn, state_gdn_conv, p, w_in_perm)
    return (y_prompt, y_sample, s5_p, rw_p, rwsh_p, gd_p, gdc_p, s5_s, rw_s, rwsh_s, gd_s, gdc_s, sgv_s)
```

```python
import functools

import jax
import jax.numpy as jnp
from jax import lax
from jax.experimental import pallas as pl
from jax.experimental.pallas import tpu as pltpu
from jax.experimental.pallas import tpu_sc as plsc

F32 = jnp.float32
BF16 = jnp.bfloat16

D_MODEL = 1024
BRANCH_W = 256
HEADS = 4
HEAD_W = 64
S5_GROUPS = 16
S5_GROUP = 16
S5_STATE = 64
S5_W = S5_GROUPS * S5_STATE
S5_ROWS = 8
SG_CHUNK = 128
SG_ROWS = 512
GD_CONV = 4
GD_QKV = 3 * BRANCH_W
RW_COLS = 1024
RW_EPS = 64e-5
NORM_EPS = 1e-6
N_EXPERTS = 8
LANES = 128

COL_RW = 0
COL_GD = 1024
COL_SG = 2048
COL_S5 = 2560
COL_AB = 2816
PROJ_COLS = 3072

VMEM_LIMIT = 48 * 1024 * 1024


def _cparams(sem):
    return pltpu.CompilerParams(dimension_semantics=sem, vmem_limit_bytes=VMEM_LIMIT)


def _dot(a, b):
    return jnp.dot(a.astype(BF16), b.astype(BF16), preferred_element_type=F32)


def _dot_nt(a, b):
    return lax.dot_general(a.astype(BF16), b.astype(BF16), (((1,), (1,)), ((), ())),
                           preferred_element_type=F32)


def _split3(a):
    hi = a.astype(BF16)
    r1 = a - hi.astype(F32)
    mid = r1.astype(BF16)
    lo = (r1 - mid.astype(F32)).astype(BF16)
    return hi, mid, lo


def _dot3_left(b_exact, a):
    hi, mid, lo = _split3(a)
    b = b_exact.astype(BF16)
    return (jnp.dot(b, hi, preferred_element_type=F32) + jnp.dot(b, mid, preferred_element_type=F32)
            + jnp.dot(b, lo, preferred_element_type=F32))


def _dot_hp(a, b):
    a0, a1, a2 = _split3(a)
    b0, b1, b2 = _split3(b)
    d = lambda x, y: jnp.dot(x, y, preferred_element_type=F32)
    return d(a0, b0) + (d(a0, b1) + d(a1, b0)) + (d(a0, b2) + d(a1, b1) + d(a2, b0))


def _iota(shape, axis):
    return lax.broadcasted_iota(jnp.int32, shape, axis)


def _head_ones():
    r = _iota((BRANCH_W, BRANCH_W), 0) // HEAD_W
    c = _iota((BRANCH_W, BRANCH_W), 1) // HEAD_W
    return (r == c).astype(BF16)


def _head_mask(l):
    r = _iota((HEADS * l, BRANCH_W), 0) // l
    c = _iota((HEADS * l, BRANCH_W), 1) // HEAD_W
    return r == c


def _expand(x, mask):
    return jnp.where(mask, jnp.concatenate([x] * HEADS, axis=0), 0.0)


def _tri_masks(l):
    i = _iota((l, HEADS * l), 0)
    j = _iota((l, HEADS * l), 1) % l
    return j < i, j <= i


def _same_head(l):
    n = HEADS * l
    return (_iota((n, n), 0) // l) == (_iota((n, n), 1) // l)


def _expand_sq(x, same):
    return jnp.where(same, jnp.concatenate([x] * HEADS, axis=0), 0.0)


def _unit_lower_inverse(a_strict, l):
    i = _iota((l, HEADS * l), 0)
    j = _iota((l, HEADS * l), 1) % l
    eye = (i == j).astype(F32)
    same = _same_head(l)
    p = [-a for a in a_strict]
    t = [eye + x for x in p]
    k = 2
    while k < l:
        p = [_dot(x, _expand_sq(x, same)) for x in p]
        t = [y + _dot(y, _expand_sq(x, same)) for y, x in zip(t, p)]
        k *= 2
    return t


def _dot2(a, b_exact):
    hi = a.astype(BF16)
    lo = (a - hi.astype(F32)).astype(BF16)
    b = b_exact.astype(BF16)
    return jnp.dot(hi, b, preferred_element_type=F32) + jnp.dot(lo, b, preferred_element_type=F32)


def _cumsum_rows(x, l, nb=1):
    n = nb * l
    i, j = _iota((n, n), 0), _iota((n, n), 1)
    tri = ((j <= i) & ((i // l) == (j // l))).astype(BF16)
    return _dot3_left(tri, x)


def _softplus(x):
    return jnp.maximum(x, 0.0) + jnp.log(1.0 + jnp.exp(-jnp.abs(x)))


def _silu(x):
    return x * jax.nn.sigmoid(x)


def _norm_proj_kernel(x_ref, nw_ref, w_ref, o_ref, h_ref):
    @pl.when(pl.program_id(1) == 0)
    def _():
        x = x_ref[...]
        ms = jnp.mean(x * x, axis=-1, keepdims=True)
        h_ref[...] = (x * lax.rsqrt(ms + NORM_EPS) * nw_ref[...]).astype(BF16)

    o_ref[...] = jnp.dot(h_ref[...], w_ref[...], preferred_element_type=F32)


def _norm_proj(x2, norm_w, w_bf16, tm, tn):
    n = x2.shape[0]
    ncol = w_bf16.shape[1]
    return pl.pallas_call(
        _norm_proj_kernel,
        grid=(n // tm, ncol // tn),
        in_specs=[pl.BlockSpec((tm, D_MODEL), lambda i, j: (i, 0)),
                  pl.BlockSpec((1, D_MODEL), lambda i, j: (0, 0)),
                  pl.BlockSpec((D_MODEL, tn), lambda i, j: (0, j))],
        out_specs=pl.BlockSpec((tm, tn), lambda i, j: (i, j)),
        out_shape=jax.ShapeDtypeStruct((n, ncol), F32),
        scratch_shapes=[pltpu.VMEM((tm, D_MODEL), BF16)],
        compiler_params=_cparams(("parallel", "arbitrary")),
    )(x2, norm_w.reshape(1, D_MODEL), w_bf16)


def _s5_kernel(u_ref, h0_ref, wb_ref, wc_ref, a2k_ref, apow_ref, d_ref, wg_ref, bg_ref,
               y_ref, hl_ref, hs_ref, hb_ref, *, tc):
    c = pl.program_id(1)

    @pl.when(c == 0)
    def _():
        hs_ref[...] = h0_ref[...]

    u = u_ref[...]
    x = _dot(u, wb_ref[...])
    xr, xi = x[:, :S5_W], x[:, S5_W:]
    ng = tc // S5_ROWS
    xr = xr.reshape(ng, S5_ROWS, S5_W)
    xi = xi.reshape(ng, S5_ROWS, S5_W)
    row = _iota((S5_ROWS, 1), 0)
    k, d = 0, 1
    while d < S5_ROWS:
        m = row >= d
        ar = jnp.where(m, a2k_ref[k:k + 1, :S5_W], 0.0)
        ai = jnp.where(m, a2k_ref[k:k + 1, S5_W:], 0.0)
        sr = pltpu.roll(xr, d, axis=1)
        si = pltpu.roll(xi, d, axis=1)
        xr, xi = xr + (ar * sr - ai * si), xi + (ar * si + ai * sr)
        k, d = k + 1, d * 2
    hb_ref[:, :S5_W] = xr.reshape(tc, S5_W)
    hb_ref[:, S5_W:] = xi.reshape(tc, S5_W)
    pr, pi_ = apow_ref[:, :S5_W], apow_ref[:, S5_W:]

    def group(gi, carry):
        cr, ci = carry
        rows = pl.ds(pl.multiple_of(gi * S5_ROWS, S5_ROWS), S5_ROWS)
        hr = hb_ref[rows, :S5_W] + (pr * cr - pi_ * ci)
        hi = hb_ref[rows, S5_W:] + (pr * ci + pi_ * cr)
        hb_ref[rows, :S5_W] = hr
        hb_ref[rows, S5_W:] = hi
        return hr[S5_ROWS - 1:S5_ROWS], hi[S5_ROWS - 1:S5_ROWS]

    cr, ci = lax.fori_loop(0, tc // S5_ROWS, group, (hs_ref[:, :S5_W], hs_ref[:, S5_W:]), unroll=4)
    hs_ref[:, :S5_W] = cr
    hs_ref[:, S5_W:] = ci
    y = _dot(hb_ref[:, :S5_W], wc_ref[:S5_W]) + _dot(hb_ref[:, S5_W:], wc_ref[S5_W:])
    y = jax.nn.gelu(y + d_ref[...] * u)
    y = y * jax.nn.sigmoid(_dot(y, wg_ref[...]) + bg_ref[...])
    y_ref[...] = y.astype(y_ref.dtype)

    @pl.when(c == pl.num_programs(1) - 1)
    def _():
        hl_ref[...] = hs_ref[...]


def _s5_tables(lam_re, lam_im, log_dt, b_c, c_c, tc):
    dt = jnp.exp(log_dt)[:, None]
    mag = jnp.exp(lam_re * dt)
    ab_re, ab_im = mag * jnp.cos(lam_im * dt), mag * jnp.sin(lam_im * dt)
    den = lam_re * lam_re + lam_im * lam_im
    nr = ab_re - 1.0
    cf_re = (nr * lam_re + ab_im * lam_im) / den
    cf_im = (ab_im * lam_re - nr * lam_im) / den
    br, bi = b_c[..., 0], b_c[..., 1]
    bb_re = cf_re[..., None] * br - cf_im[..., None] * bi
    bb_im = cf_re[..., None] * bi + cf_im[..., None] * br
    eye = jnp.eye(S5_GROUPS, dtype=F32)
    bd_in = lambda m: jnp.einsum('gph,gk->ghkp', m, eye).reshape(BRANCH_W, S5_W)
    wb = jnp.concatenate([bd_in(bb_re), bd_in(bb_im)], axis=1)
    cr, ci = c_c[..., 0], c_c[..., 1]
    bd_out = lambda m: jnp.einsum('ghp,gk->gpkh', m, eye).reshape(S5_W, BRANCH_W)
    wc = jnp.concatenate([bd_out(cr), -bd_out(ci)], axis=0)
    pr, pi_ = ab_re.reshape(1, S5_W), ab_im.reshape(1, S5_W)
    lv_r, lv_i = [], []
    tr, ti = pr, pi_
    d = 1
    while d < tc:
        lv_r.append(pr)
        lv_i.append(pi_)
        tr, ti = (jnp.concatenate([tr, tr * pr - ti * pi_], axis=0),
                  jnp.concatenate([ti, tr * pi_ + ti * pr], axis=0))
        pr, pi_ = pr * pr - pi_ * pi_, 2.0 * pr * pi_
        d *= 2
    n_lv = len(lv_r)
    pad = (-n_lv) % 8
    a2k = jnp.concatenate([jnp.concatenate(lv_r, axis=0), jnp.concatenate(lv_i, axis=0)], axis=1)
    a2k = jnp.pad(a2k, ((0, pad), (0, 0)))
    apow = jnp.concatenate([tr, ti], axis=1)
    return wb.astype(BF16), wc.astype(BF16), a2k, apow


def _s5_mixer(p3, h0, lam_re, lam_im, log_dt, b_c, c_c, d_skip, w_glu, b_glu, tc):
    bsz, t, _ = p3.shape
    wb, wc, a2k, apow = _s5_tables(lam_re, lam_im, log_dt, b_c, c_c, S5_ROWS)
    h0f = jnp.concatenate([h0[..., 0].reshape(bsz, 1, S5_W), h0[..., 1].reshape(bsz, 1, S5_W)], axis=-1)
    full = lambda a: pl.BlockSpec(a.shape, lambda b, c: (0,) * a.ndim)
    d2, bg2, wg = d_skip.reshape(1, BRANCH_W), b_glu.reshape(1, BRANCH_W), w_glu.astype(BF16)
    y, hl = pl.pallas_call(
        functools.partial(_s5_kernel, tc=tc),
        grid=(bsz, t // tc),
        in_specs=[pl.BlockSpec((None, tc, BRANCH_W), lambda b, c: (b, c, COL_S5 // BRANCH_W)),
                  pl.BlockSpec((None, 1, 2 * S5_W), lambda b, c: (b, 0, 0)),
                  full(wb), full(wc), full(a2k), full(apow), full(d2), full(wg), full(bg2)],
        out_specs=[pl.BlockSpec((None, tc, BRANCH_W), lambda b, c: (b, c, 0)),
                   pl.BlockSpec((None, 1, 2 * S5_W), lambda b, c: (b, 0, 0))],
        out_shape=[jax.ShapeDtypeStruct((bsz, t, BRANCH_W), BF16),
                   jax.ShapeDtypeStruct((bsz, 1, 2 * S5_W), F32)],
        scratch_shapes=[pltpu.VMEM((1, 2 * S5_W), F32), pltpu.VMEM((tc, 2 * S5_W), F32)],
        compiler_params=_cparams(("parallel", "arbitrary")),
    )(p3, h0f, wb, wc, a2k, apow, d2, wg, bg2)
    h_last = jnp.stack([hl[:, 0, :S5_W].reshape(bsz, S5_GROUPS, S5_STATE),
                        hl[:, 0, S5_W:].reshape(bsz, S5_GROUPS, S5_STATE)], axis=-1)
    return y, h_last


def _rwkv_kernel(z_ref, sh0_ref, s0_ref, mu_ref, w0_ref, w2_ref, a0_ref, a2_ref, g2_ref,
                 kk_ref, ka_ref, rk_ref, lnw_ref, lnb_ref,
                 y_ref, sl_ref, st_ref, zp_ref, *, l, bb):
    c = pl.program_id(1)

    @pl.when(c == 0)
    def _():
        st_ref[...] = s0_ref[...]
        zp_ref[...] = sh0_ref[...]

    row = _iota((l, 1), 0)
    zms = []
    for b in range(bb):
        z = z_ref[b]
        prev = jnp.where(row == 0, zp_ref[b], pltpu.roll(z, 1, axis=0))
        zp_ref[b] = z[l - 1:l]
        zms.append(z + (prev - z) * mu_ref[...])
    zm = jnp.concatenate(zms, axis=0)
    r, k, v = zm[:, 0:256], zm[:, 256:512], zm[:, 512:768]
    lo = zm[:, 768:896]
    g_lo = zm[:, 896:1024]
    w_log = -_softplus(-(w0_ref[...] + _dot(jnp.tanh(lo), w2_ref[...]))) - 0.5
    lw = -jnp.exp(w_log)
    a = jax.nn.sigmoid(a0_ref[...] + _dot(lo, a2_ref[...]))
    g = _dot(jax.nn.sigmoid(g_lo), g2_ref[...])
    ones_h = _head_ones()
    kk = k * kk_ref[...]
    kk = kk * lax.rsqrt(_dot2(kk * kk, ones_h) + NORM_EPS)
    k = k * (1.0 + (a - 1.0) * ka_ref[...])
    kka = kk * a

    cum = _cumsum_rows(lw, l, bb)
    p_incl = jnp.exp(cum)
    p_inv = jnp.exp(-cum)
    kt = kk * jnp.exp(cum - lw)
    rt = r * p_incl
    kh = k * p_inv
    ah = kka * p_inv

    hm = _head_mask(l)
    strict, incl = _tri_masks(l)
    streams = range(bb)
    rows = [slice(b * l, (b + 1) * l) for b in streams]
    ex = lambda x: [_expand(x[s], hm) for s in rows]
    kt_b, rt_b, v_b = [kt[s] for s in rows], [rt[s] for s in rows], [v[s] for s in rows]
    kh_e, ah_e, v_e = ex(kh), ex(ah), ex(v)
    a_aa = [jnp.where(strict, _dot_nt(kt_b[b], ah_e[b]), 0.0) for b in streams]
    a_ak = [jnp.where(strict, _dot_nt(kt_b[b], kh_e[b]), 0.0) for b in streams]
    b_ra = [jnp.where(incl, _dot_nt(rt_b[b], ah_e[b]), 0.0) for b in streams]
    b_rk = [jnp.where(incl, _dot_nt(rt_b[b], kh_e[b]), 0.0) for b in streams]
    t_inv = _unit_lower_inverse(a_aa, l)
    st = [st_ref[b] for b in streams]
    rhs = [_dot_nt(kt_b[b], st[b]) + _dot(a_ak[b], v_e[b]) for b in streams]
    yb = [_dot_nt(rt_b[b], st[b]) + _dot(b_rk[b], v_e[b]) for b in streams]
    u = [_dot(t_inv[b], _expand(rhs[b], hm)) for b in streams]
    yb = [yb[b] - _dot(b_ra[b], _expand(u[b], hm)) for b in streams]
    to_end = [jnp.exp(cum[s][l - 1:l] - cum[s]) for s in rows]
    lhs_t = [jnp.concatenate([v_b[b], -u[b]], axis=0).T for b in streams]
    rhs_k = [jnp.concatenate([k[rows[b]] * to_end[b], kka[rows[b]] * to_end[b]], axis=0) for b in streams]
    head_blk = _same_head(HEAD_W)
    for b in streams:
        p_last = p_incl[(b + 1) * l - 1:(b + 1) * l]
        st_ref[b] = st[b] * p_last + jnp.where(head_blk, _dot(lhs_t[b], rhs_k[b]), 0.0)
    y = jnp.concatenate(yb, axis=0)

    inv_w = 1.0 / HEAD_W
    mean = _dot2(y, ones_h) * inv_w
    yc = y - mean
    var = _dot2(yc * yc, ones_h) * inv_w
    y = yc * lax.rsqrt(var + RW_EPS) * lnw_ref[...] + lnb_ref[...]
    bonus = _dot2(r * k * rk_ref[...], ones_h) * v
    y = ((y + bonus) * g).astype(y_ref.dtype)
    for b in range(bb):
        y_ref[b] = y[b * l:(b + 1) * l]

    @pl.when(c == pl.num_programs(1) - 1)
    def _():
        sl_ref[...] = st_ref[...]


STACK_ROWS = 256


def _streams_per_step(bsz, l):
    bb = max(1, min(bsz, STACK_ROWS // l))
    while bsz % bb:
        bb -= 1
    return bb


def _block_diag_heads(s):
    bsz = s.shape[0]
    eye = jnp.eye(HEADS, dtype=s.dtype)
    return jnp.einsum('bhij,hg->bhigj', s, eye).reshape(bsz, BRANCH_W, BRANCH_W)


def _diag_blocks(s):
    bsz = s.shape[0]
    s5 = s.reshape(bsz, HEADS, HEAD_W, HEADS, HEAD_W)
    return jnp.stack([s5[:, h, :, h, :] for h in range(HEADS)], axis=1)


def _rwkv_mixer(p3, shift0, s0, mu, w0, w2, a0, a2, g2, k_k, k_a, r_k, ln_w, ln_b, l):
    bsz, t, _ = p3.shape
    row = lambda a: a.reshape(1, -1)
    w2p = jnp.concatenate([w2, jnp.zeros_like(w2)], axis=0).astype(BF16)
    a2p = jnp.concatenate([jnp.zeros_like(a2), a2], axis=0).astype(BF16)
    args = (shift0.reshape(bsz, 1, RW_COLS), _block_diag_heads(s0), row(mu), row(w0), w2p, row(a0), a2p,
            g2.astype(BF16), row(k_k), row(k_a), row(r_k), row(ln_w), row(ln_b))
    full = lambda a: pl.BlockSpec(a.shape, lambda b, c: (0,) * a.ndim)
    bb = _streams_per_step(bsz, l)
    y, sl = pl.pallas_call(
        functools.partial(_rwkv_kernel, l=l, bb=bb),
        grid=(bsz // bb, t // l),
        in_specs=[pl.BlockSpec((bb, l, RW_COLS), lambda b, c: (b, c, COL_RW // RW_COLS)),
                  pl.BlockSpec((bb, 1, RW_COLS), lambda b, c: (b, 0, 0)),
                  pl.BlockSpec((bb, BRANCH_W, BRANCH_W), lambda b, c: (b, 0, 0))]
                 + [full(a) for a in args[2:]],
        out_specs=[pl.BlockSpec((bb, l, BRANCH_W), lambda b, c: (b, c, 0)),
                   pl.BlockSpec((bb, BRANCH_W, BRANCH_W), lambda b, c: (b, 0, 0))],
        out_shape=[jax.ShapeDtypeStruct((bsz, t, BRANCH_W), BF16),
                   jax.ShapeDtypeStruct((bsz, BRANCH_W, BRANCH_W), F32)],
        scratch_shapes=[pltpu.VMEM((bb, BRANCH_W, BRANCH_W), F32), pltpu.VMEM((bb, 1, RW_COLS), F32)],
        compiler_params=_cparams(("parallel", "arbitrary")),
    )(p3, *args)
    return y, _diag_blocks(sl)


def _sgu_kernel(z_ref, lnw_ref, lnb_ref, wm_ref, bias_ref, o_ref, v_ref, *, l, nc):
    zg = jax.nn.gelu(z_ref[...])
    u, v = zg[:, :BRANCH_W], zg[:, BRANCH_W:]
    mean = jnp.mean(v, axis=-1, keepdims=True)
    vc = v - mean
    var = jnp.mean(vc * vc, axis=-1, keepdims=True)
    v = vc * lax.rsqrt(var + NORM_EPS) * lnw_ref[...] + lnb_ref[...]
    v_ref[...] = v
    hm = _head_mask(l)
    wm = wm_ref[...]
    for i in range(nc):
        rows = slice(i * l, (i + 1) * l)
        mixed = bias_ref[...] + _dot(wm, _expand(v[rows], hm))
        o_ref[rows, :] = (u[rows] * mixed).astype(o_ref.dtype)


def _sgu_mixer(p3, ln_w, ln_b, w_s, b_s):
    bsz, t, _ = p3.shape
    l = min(SG_CHUNK, t)
    nc = max(1, min(SG_ROWS, t) // l)
    tril = jnp.tril(jnp.ones((l, l), F32))
    wm = jnp.transpose(w_s[:, :l, :l] * tril, (1, 0, 2)).reshape(l, HEADS * l).astype(BF16)
    bias = jnp.repeat(jnp.transpose(b_s[:, :l]), HEAD_W, axis=1)
    row = lambda a: a.reshape(1, -1)
    full = lambda a: pl.BlockSpec(a.shape, lambda b, c: (0,) * a.ndim)
    args = (row(ln_w), row(ln_b), wm, bias)
    return pl.pallas_call(
        functools.partial(_sgu_kernel, l=l, nc=nc),
        grid=(bsz, t // (nc * l)),
        in_specs=[pl.BlockSpec((None, nc * l, 2 * BRANCH_W), lambda b, c: (b, c, COL_SG // (2 * BRANCH_W)))]
                 + [full(a) for a in args],
        out_specs=[pl.BlockSpec((None, nc * l, BRANCH_W), lambda b, c: (b, c, 0)),
                   pl.BlockSpec((None, nc * l, BRANCH_W), lambda b, c: (b, c, 0))],
        out_shape=[jax.ShapeDtypeStruct((bsz, t, BRANCH_W), BF16),
                   jax.ShapeDtypeStruct((bsz, t, BRANCH_W), F32)],
        compiler_params=_cparams(("parallel", "parallel")),
    )(p3, *args)


def _gdn_kernel(z_ref, ab_ref, cv0_ref, s0_ref, cw_ref, alog_ref, dtb_ref, nw_ref,
                y_ref, sl_ref, st_ref, cv_ref, *, l, bb):
    c = pl.program_id(1)

    @pl.when(c == 0)
    def _():
        st_ref[...] = s0_ref[...]
        cv_ref[...] = cv0_ref[...]

    row8 = _iota((8, 1), 0)
    convs, gates = [], []
    for b in range(bb):
        z = z_ref[b]
        qkv = z[:, :GD_QKV]
        gates.append(z[:, GD_QKV:])
        carry = cv_ref[b]
        cv_ref[b] = qkv[l - 8:l]
        conv = qkv * cw_ref[GD_CONV - 1:GD_CONV]
        for j in range(1, GD_CONV):
            sh = pltpu.roll(qkv, j, axis=0)
            top = jnp.where(row8 < j, pltpu.roll(carry, j, axis=0), sh[:8])
            sh = jnp.concatenate([top, sh[8:]], axis=0) if l > 8 else top
            conv = conv + sh * cw_ref[GD_CONV - 1 - j:GD_CONV - j]
        convs.append(conv)
    conv = _silu(jnp.concatenate(convs, axis=0))
    gate = jnp.concatenate(gates, axis=0)
    q, k, v = conv[:, :256], conv[:, 256:512], conv[:, 512:768]
    ones_h = _head_ones()
    q = q * lax.rsqrt(_dot2(q * q, ones_h) + NORM_EPS) * (HEAD_W ** -0.5)
    k = k * lax.rsqrt(_dot2(k * k, ones_h) + NORM_EPS)
    ab = jnp.concatenate([ab_ref[b] for b in range(bb)], axis=0)
    lane_h = _iota((bb * l, BRANCH_W), 1) // HEAD_W
    a_in = jnp.zeros((bb * l, BRANCH_W), F32)
    b_in = jnp.zeros((bb * l, BRANCH_W), F32)
    for h in range(HEADS):
        a_in = jnp.where(lane_h == h, ab[:, h:h + 1], a_in)
        b_in = jnp.where(lane_h == h, ab[:, HEADS + h:HEADS + h + 1], b_in)
    beta = jax.nn.sigmoid(b_in)
    g = -jnp.exp(alog_ref[...]) * _softplus(a_in + dtb_ref[...])
    gc = _cumsum_rows(g, l, bb)
    eg = jnp.exp(gc)
    kb = k * beta
    vb = v * beta
    kbg = kb * eg
    qg = q * eg

    hm = _head_mask(l)
    strict, incl = _tri_masks(l)
    n = HEADS * l
    streams = range(bb)
    rows = [slice(b * l, (b + 1) * l) for b in streams]
    ex = lambda x: [_expand(x[s], hm) for s in rows]
    k_e, vb_e, kbg_e = ex(k), ex(vb), ex(kbg)
    lane_hd = _iota((l, n), 1) // l
    eye = _iota((l, n), 0) == (_iota((l, n), 1) % l)
    decay = []
    for s in rows:
        gi = jnp.zeros((l, n), F32)
        for h in range(HEADS):
            gi = jnp.where(lane_hd == h, gc[s][:, h * HEAD_W:h * HEAD_W + 1], gi)
        gj = jnp.sum(jnp.where(eye, gi, 0.0), axis=0, keepdims=True)
        decay.append(jnp.where(incl, jnp.exp(jnp.where(incl, gi - gj, 0.0)), 0.0))
    lm = [jnp.where(strict, _dot_nt(kb[rows[b]], k_e[b]) * decay[b], 0.0) for b in streams]
    qk = [_dot_nt(q[rows[b]], k_e[b]) * decay[b] for b in streams]
    t_inv = _unit_lower_inverse(lm, l)
    uc = [_dot(t_inv[b], vb_e[b]) for b in streams]
    wc = [_dot(t_inv[b], kbg_e[b]) for b in streams]
    st = [st_ref[b] for b in streams]
    ob = [_dot(qg[rows[b]], st[b]) for b in streams]
    v_new = [uc[b] - _dot(wc[b], st[b]) for b in streams]
    ob = [ob[b] + _dot(qk[b], _expand(v_new[b], hm)) for b in streams]
    g_last = [gc[s][l - 1:l] for s in rows]
    k_dec = [(k[rows[b]] * jnp.exp(g_last[b] - gc[rows[b]])).T for b in streams]
    head_blk = _same_head(HEAD_W)
    for b in streams:
        st_ref[b] = st[b] * jnp.exp(g_last[b]) + jnp.where(head_blk, _dot(k_dec[b], v_new[b]), 0.0)
    o = jnp.concatenate(ob, axis=0)
    ms = _dot2(o * o, ones_h) * (1.0 / HEAD_W)
    o = (o * lax.rsqrt(ms + NORM_EPS) * nw_ref[...] * _silu(gate)).astype(y_ref.dtype)
    for b in range(bb):
        y_ref[b] = o[b * l:(b + 1) * l]

    @pl.when(c == pl.num_programs(1) - 1)
    def _():
        sl_ref[...] = st_ref[...]


def _gdn_mixer(p3, conv0, s0, conv_w, a_log, dt_bias, norm_w, l):
    bsz, t, _ = p3.shape
    cv0 = jnp.pad(conv0, ((0, 0), (8 - (GD_CONV - 1), 0), (0, 0)))
    cw = jnp.pad(conv_w, ((0, 8 - GD_CONV), (0, 0)))
    per_head = lambda a: jnp.repeat(a, HEAD_W).reshape(1, BRANCH_W)
    args = (cv0, _block_diag_heads(s0), cw, per_head(a_log), per_head(dt_bias),
            jnp.tile(norm_w, HEADS).reshape(1, BRANCH_W))
    full = lambda a: pl.BlockSpec(a.shape, lambda b, c: (0,) * a.ndim)
    bb = _streams_per_step(bsz, l)
    y, sl = pl.pallas_call(
        functools.partial(_gdn_kernel, l=l, bb=bb),
        grid=(bsz // bb, t // l),
        in_specs=[pl.BlockSpec((bb, l, 1024), lambda b, c: (b, c, COL_GD // 1024)),
                  pl.BlockSpec((bb, l, LANES), lambda b, c: (b, c, COL_AB // LANES)),
                  pl.BlockSpec((bb, 8, GD_QKV), lambda b, c: (b, 0, 0)),
                  pl.BlockSpec((bb, BRANCH_W, BRANCH_W), lambda b, c: (b, 0, 0))]
                 + [full(a) for a in args[2:]],
        out_specs=[pl.BlockSpec((bb, l, BRANCH_W), lambda b, c: (b, c, 0)),
                   pl.BlockSpec((bb, BRANCH_W, BRANCH_W), lambda b, c: (b, 0, 0))],
        out_shape=[jax.ShapeDtypeStruct((bsz, t, BRANCH_W), BF16),
                   jax.ShapeDtypeStruct((bsz, BRANCH_W, BRANCH_W), F32)],
        scratch_shapes=[pltpu.VMEM((bb, BRANCH_W, BRANCH_W), F32), pltpu.VMEM((bb, 8, GD_QKV), F32)],
        compiler_params=_cparams(("parallel", "arbitrary")),
    )(p3, p3, *args)
    return y, _diag_blocks(sl)


def _merge_kernel(ya_ref, yb_ref, yc_ref, yd_ref, x_ref, nw_ref, wg_ref, wbr_ref, wout_ref, o_ref):
    x = x_ref[...]
    ms = jnp.mean(x * x, axis=-1, keepdims=True)
    h = (x * lax.rsqrt(ms + NORM_EPS) * nw_ref[...]).astype(BF16)
    m = None
    for b, y_ref in enumerate((ya_ref, yb_ref, yc_ref, yd_ref)):
        gate = jnp.dot(h, wg_ref[:, b * D_MODEL:(b + 1) * D_MODEL], preferred_element_type=F32)
        br = jnp.dot(y_ref[...], wbr_ref[b], preferred_element_type=F32)
        term = (0.5 * jnp.tanh(0.5 * gate) + 0.5) * br
        m = term if m is None else m + term
    o_ref[...] = x + jnp.dot(m.astype(BF16), wout_ref[...], preferred_element_type=F32)


def _merge(ys, x2, norm_w, w_gate, w_branch, w_out, tm):
    n = x2.shape[0]
    yspec = pl.BlockSpec((tm, BRANCH_W), lambda i: (i, 0))
    once = pl.Buffered(1)
    return pl.pallas_call(
        _merge_kernel,
        grid=(n // tm,),
        in_specs=[yspec, yspec, yspec, yspec,
                  pl.BlockSpec((tm, D_MODEL), lambda i: (i, 0)),
                  pl.BlockSpec((1, D_MODEL), lambda i: (0, 0)),
                  pl.BlockSpec((D_MODEL, 4 * D_MODEL), lambda i: (0, 0), pipeline_mode=once),
                  pl.BlockSpec((4, BRANCH_W, D_MODEL), lambda i: (0, 0, 0), pipeline_mode=once),
                  pl.BlockSpec((D_MODEL, D_MODEL), lambda i: (0, 0), pipeline_mode=once)],
        out_specs=pl.BlockSpec((tm, D_MODEL), lambda i: (i, 0)),
        out_shape=jax.ShapeDtypeStruct((n, D_MODEL), F32),
        compiler_params=_cparams(("parallel",)),
    )(*[y.reshape(n, BRANCH_W) for y in ys], x2, norm_w.reshape(1, D_MODEL), w_gate,
      w_branch.astype(BF16), w_out.astype(BF16))


def _ffn_kernel(x_ref, nw_ref, w1_ref, w3_ref, w2_ref, o_ref, h_ref):
    j = pl.program_id(1)

    @pl.when(j == 0)
    def _():
        x = x_ref[...]
        ms = jnp.mean(x * x, axis=-1, keepdims=True)
        h_ref[...] = (x * lax.rsqrt(ms + NORM_EPS) * nw_ref[...]).astype(BF16)
        o_ref[...] = x

    h = h_ref[...]
    a = _silu(jnp.dot(h, w1_ref[...], preferred_element_type=F32)) * jnp.dot(h, w3_ref[...], preferred_element_type=F32)
    o_ref[...] += jnp.dot(a.astype(BF16), w2_ref[...], preferred_element_type=F32)


def _ffn(x2, norm_w, w1, w3, w2, tm, tf):
    n = x2.shape[0]
    dff = w1.shape[1]
    return pl.pallas_call(
        _ffn_kernel,
        grid=(n // tm, dff // tf),
        in_specs=[pl.BlockSpec((tm, D_MODEL), lambda i, j: (i, 0)),
                  pl.BlockSpec((1, D_MODEL), lambda i, j: (0, 0)),
                  pl.BlockSpec((D_MODEL, tf), lambda i, j: (0, j)),
                  pl.BlockSpec((D_MODEL, tf), lambda i, j: (0, j)),
                  pl.BlockSpec((tf, D_MODEL), lambda i, j: (j, 0))],
        out_specs=pl.BlockSpec((tm, D_MODEL), lambda i, j: (i, 0)),
        out_shape=jax.ShapeDtypeStruct((n, D_MODEL), F32),
        scratch_shapes=[pltpu.VMEM((tm, D_MODEL), BF16)],
        compiler_params=_cparams(("parallel", "arbitrary")),
    )(x2, norm_w.reshape(1, D_MODEL), w1.astype(BF16), w3.astype(BF16), w2.astype(BF16))


MOE_SUB = 256
MOE_TILE = 2048
MOE_PLACE_ROWS = 1024
MOE_FF = 512
ROUTER_ROWS = 256


def _router_kernel(x_ref, nw_ref, wr_ref, br_ref, h_ref, rcol_ref, gcol_ref, rrow_ref, cnt_ref, seen_ref,
                   *, tm, k):
    x = x_ref[...]
    ms = jnp.mean(x * x, axis=-1, keepdims=True)
    h = x * lax.rsqrt(ms + NORM_EPS) * nw_ref[...]
    h_ref[...] = h.astype(BF16)
    lane = _iota((tm, LANES), 1)
    logits = jnp.where(lane < N_EXPERTS, _dot_hp(h, wr_ref[...]) + br_ref[...], -jnp.inf)
    m1 = jnp.max(logits, axis=1, keepdims=True)
    i1 = jnp.min(jnp.where(logits == m1, lane, LANES), axis=1, keepdims=True)
    rest = jnp.where(lane == i1, -jnp.inf, logits)
    m2 = jnp.max(rest, axis=1, keepdims=True)
    i2 = jnp.min(jnp.where(rest == m2, lane, LANES), axis=1, keepdims=True)
    e2 = jnp.exp(m2 - m1)
    g1 = 1.0 / (1.0 + e2)
    g2 = e2 / (1.0 + e2)
    sel = (lane == i1) | (lane == i2)
    self32 = sel.astype(F32)
    gcol_ref[...] = jnp.where(lane == i1, g1, 0.0) + jnp.where(lane == i2, g2, 0.0)
    @pl.when(pl.program_id(0) % k == 0)
    def _():
        seen_ref[...] = jnp.zeros_like(seen_ref)

    seen = seen_ref[...]
    tri = (_iota((tm, tm), 1) < _iota((tm, tm), 0)).astype(BF16)
    rank = jnp.dot(tri, self32.astype(BF16), preferred_element_type=F32) + seen
    r = jnp.where(sel, rank, -1.0)
    rcol_ref[...] = r
    rrow_ref[...] = r.T[:N_EXPERTS]
    seen = seen + jnp.sum(self32, axis=0, keepdims=True)
    seen_ref[...] = seen
    cnt_ref[...] = jnp.broadcast_to(seen, (8, LANES))


def _router(x2, norm_w, w_router, b_router, tm, k):
    n = x2.shape[0]
    nt = n // (tm * k)
    wr = jnp.pad(w_router, ((0, 0), (0, LANES - N_EXPERTS)))
    br = jnp.pad(b_router, (0, LANES - N_EXPERTS)).reshape(1, LANES)
    return pl.pallas_call(
        functools.partial(_router_kernel, tm=tm, k=k),
        grid=(n // tm,),
        in_specs=[pl.BlockSpec((tm, D_MODEL), lambda i: (i, 0)),
                  pl.BlockSpec((1, D_MODEL), lambda i: (0, 0)),
                  pl.BlockSpec((D_MODEL, LANES), lambda i: (0, 0)),
                  pl.BlockSpec((1, LANES), lambda i: (0, 0))],
        out_specs=[pl.BlockSpec((tm, D_MODEL), lambda i: (i, 0)),
                   pl.BlockSpec((tm, LANES), lambda i: (i, 0)),
                   pl.BlockSpec((tm, LANES), lambda i: (i, 0)),
                   pl.BlockSpec((None, N_EXPERTS, tm), lambda i: (i // k, 0, i % k)),
                   pl.BlockSpec((None, 8, LANES), lambda i: (i, 0, 0))],
        out_shape=[jax.ShapeDtypeStruct((n, D_MODEL), BF16),
                   jax.ShapeDtypeStruct((n, LANES), F32),
                   jax.ShapeDtypeStruct((n, LANES), F32),
                   jax.ShapeDtypeStruct((nt, N_EXPERTS, tm * k), F32),
                   jax.ShapeDtypeStruct((n // tm, 8, LANES), F32)],
        scratch_shapes=[pltpu.VMEM((1, LANES), F32)],
        compiler_params=_cparams(("arbitrary",)),
    )(x2, norm_w.reshape(1, D_MODEL), wr, br)


def _moe_kernel(cnt_ref, h_ref, rrow_ref, rcol_ref, gcol_ref, w1_ref, w3_ref, w2_ref, o_ref,
                xs_ref, acc_ref, *, tm):
    i, e, c = pl.program_id(0), pl.program_id(1), pl.program_id(2)
    n_sub = (cnt_ref[i * N_EXPERTS + e] + (MOE_SUB - 1)) // MOE_SUB

    @pl.when((e == 0) & (c == 0))
    def _():
        o_ref[...] = jnp.zeros_like(o_ref)

    @pl.when(c == 0)
    def _():
        rrow = rrow_ref[pl.ds(e, 1), :]
        slot = _iota((MOE_SUB, 1), 0).astype(F32)

        def gather(s, carry):
            base = pl.multiple_of(s * MOE_SUB, MOE_SUB)
            pick = (rrow == slot + (s * MOE_SUB).astype(F32)).astype(BF16)
            xs_ref[pl.ds(base, MOE_SUB), :] = jnp.dot(pick, h_ref[...], preferred_element_type=F32).astype(BF16)
            acc_ref[pl.ds(base, MOE_SUB), :] = jnp.zeros((MOE_SUB, D_MODEL), F32)
            return carry

        lax.fori_loop(0, n_sub, gather, 0)

    def expert_rows(base, m):
        xb = xs_ref[pl.ds(base, m), :]
        a = (_silu(jnp.dot(xb, w1_ref[...], preferred_element_type=F32))
             * jnp.dot(xb, w3_ref[...], preferred_element_type=F32))
        acc_ref[pl.ds(base, m), :] += jnp.dot(a.astype(BF16), w2_ref[...], preferred_element_type=F32)

    def expert_pair(p, carry):
        expert_rows(pl.multiple_of(p * (2 * MOE_SUB), 2 * MOE_SUB), 2 * MOE_SUB)
        return carry

    if tm >= 2 * MOE_SUB:
        lax.fori_loop(0, n_sub // 2, expert_pair, 0)

        @pl.when(n_sub % 2 == 1)
        def _():
            expert_rows(pl.multiple_of((n_sub - 1) * MOE_SUB, MOE_SUB), MOE_SUB)
    else:
        @pl.when(n_sub > 0)
        def _():
            expert_rows(0, MOE_SUB)

    @pl.when(c == pl.num_programs(2) - 1)
    def _():
        tp = min(tm, MOE_PLACE_ROWS)
        lane = _iota((tp, LANES), 1)
        slot = _iota((1, MOE_SUB), 1).astype(F32)

        def scatter(s, carry):
            base = pl.multiple_of(s * MOE_SUB, MOE_SUB)
            a = acc_ref[pl.ds(base, MOE_SUB), :]
            hi = a.astype(BF16)
            lo = (a - hi.astype(F32)).astype(BF16)
            for r0 in range(0, tm, tp):
                rcol = jnp.sum(jnp.where(lane == e, rcol_ref[r0:r0 + tp], 0.0), axis=1, keepdims=True)
                gate = jnp.sum(jnp.where(lane == e, gcol_ref[r0:r0 + tp], 0.0), axis=1, keepdims=True)
                place = (rcol == slot + (s * MOE_SUB).astype(F32)).astype(BF16)
                back = (jnp.dot(place, hi, preferred_element_type=F32)
                        + jnp.dot(place, lo, preferred_element_type=F32))
                o_ref[r0:r0 + tp] += gate * back
            return carry

        lax.fori_loop(0, n_sub, scatter, 0)


def _moe(h2, rrow, rcol, gcol, counts, w1, w3, w2, tm, tf):
    n = h2.shape[0]
    nt = n // tm
    dff = w1.shape[2]
    grid_spec = pltpu.PrefetchScalarGridSpec(
        num_scalar_prefetch=1,
        grid=(nt, N_EXPERTS, dff // tf),
        in_specs=[pl.BlockSpec((tm, D_MODEL), lambda i, e, c, cnt: (i, 0), pipeline_mode=pl.Buffered(1)),
                  pl.BlockSpec((None, N_EXPERTS, tm), lambda i, e, c, cnt: (i, 0, 0)),
                  pl.BlockSpec((tm, LANES), lambda i, e, c, cnt: (i, 0), pipeline_mode=pl.Buffered(1)),
                  pl.BlockSpec((tm, LANES), lambda i, e, c, cnt: (i, 0), pipeline_mode=pl.Buffered(1)),
                  pl.BlockSpec((None, D_MODEL, tf), lambda i, e, c, cnt: (e, 0, c)),
                  pl.BlockSpec((None, D_MODEL, tf), lambda i, e, c, cnt: (e, 0, c)),
                  pl.BlockSpec((None, tf, D_MODEL), lambda i, e, c, cnt: (e, c, 0))],
        out_specs=pl.BlockSpec((tm, D_MODEL), lambda i, e, c, cnt: (i, 0), pipeline_mode=pl.Buffered(1)),
        scratch_shapes=[pltpu.VMEM((tm, D_MODEL), BF16), pltpu.VMEM((tm, D_MODEL), F32)],
    )
    return pl.pallas_call(
        functools.partial(_moe_kernel, tm=tm),
        grid_spec=grid_spec,
        out_shape=jax.ShapeDtypeStruct((n, D_MODEL), F32),
        compiler_params=_cparams(("parallel", "arbitrary", "arbitrary")),
    )(counts, h2, rrow, rcol, gcol, w1, w3, w2)


def _moe_ffn(x2, norm_w, w_router, b_router, w1, w3, w2):
    n = x2.shape[0]
    tm = min(MOE_TILE, n)
    k = tm // ROUTER_ROWS
    h2, rcol, gcol, rrow, cnt = _router(x2, norm_w, w_router, b_router, ROUTER_ROWS, k)
    counts = cnt[k - 1::k, 0, :N_EXPERTS].astype(jnp.int32).reshape(-1)
    return _moe(h2, rrow, rcol, gcol, counts, w1.astype(BF16), w3.astype(BF16), w2.astype(BF16), tm, MOE_FF)


SC_CORES = 2
SC_SUBCORES = 16
SC_WINDOW = 128
SC_PIECE = 256
MOE_BLOCK = 512
HALF = D_MODEL // 2


HIGH16 = -65536


def _pack_bf16_pairs(h):
    bits = lax.bitcast_convert_type(h.astype(BF16).astype(F32), jnp.int32)
    return lax.shift_right_logical(bits[:, :HALF], 16) | (bits[:, HALF:] & HIGH16)


def _unpack_bf16_pairs(pieces):
    lo = [lax.bitcast_convert_type(lax.shift_left(w, 16), F32) for w in pieces]
    hi = [lax.bitcast_convert_type(w & HIGH16, F32) for w in pieces]
    return jnp.concatenate(lo + hi, axis=1).astype(BF16)


def _route_kernel(x_ref, nw_ref, wr_ref, br_ref, hp_ref, tok_ref, cnt_ref, seen_ref, *, tm):
    @pl.when(pl.program_id(0) == 0)
    def _():
        seen_ref[...] = jnp.zeros_like(seen_ref)

    x = x_ref[...]
    ms = jnp.mean(x * x, axis=-1, keepdims=True)
    h = x * lax.rsqrt(ms + NORM_EPS) * nw_ref[...]
    hp = _pack_bf16_pairs(h)
    for q in range(HALF // SC_PIECE):
        hp_ref[q] = hp[:, q * SC_PIECE:(q + 1) * SC_PIECE]
    lane = _iota((tm, LANES), 1)
    logits = jnp.where(lane < N_EXPERTS, _dot_hp(h, wr_ref[...]) + br_ref[...], -jnp.inf)
    m1 = jnp.max(logits, axis=1, keepdims=True)
    i1 = jnp.min(jnp.where(logits == m1, lane, LANES), axis=1, keepdims=True)
    rest = jnp.where(lane == i1, -jnp.inf, logits)
    m2 = jnp.max(rest, axis=1, keepdims=True)
    i2 = jnp.min(jnp.where(rest == m2, lane, LANES), axis=1, keepdims=True)
    e2 = jnp.exp(m2 - m1)
    g1 = 1.0 / (1.0 + e2)
    g2 = e2 / (1.0 + e2)
    self32 = ((lane == i1) | (lane == i2)).astype(F32)
    seen = seen_ref[...]
    tri = (_iota((tm, tm), 1) < _iota((tm, tm), 0)).astype(BF16)
    rank = jnp.dot(tri, self32.astype(BF16), preferred_element_type=F32) + seen
    r1 = jnp.sum(jnp.where(lane == i1, rank, 0.0), axis=1, keepdims=True)
    r2 = jnp.sum(jnp.where(lane == i2, rank, 0.0), axis=1, keepdims=True)
    cols = (i1.astype(F32), i2.astype(F32), r1, r2, g1, g2)
    tok = jnp.zeros((tm, LANES), F32)
    for j, col in enumerate(cols):
        tok = jnp.where(lane == j, col, tok)
    tok_ref[...] = tok
    seen = seen + jnp.sum(self32, axis=0, keepdims=True)
    seen_ref[...] = seen
    cnt_ref[...] = jnp.broadcast_to(seen, (8, LANES))


def _route(x2, norm_w, w_router, b_router, tm):
    n = x2.shape[0]
    wr = jnp.pad(w_router, ((0, 0), (0, LANES - N_EXPERTS)))
    br = jnp.pad(b_router, (0, LANES - N_EXPERTS)).reshape(1, LANES)
    return pl.pallas_call(
        functools.partial(_route_kernel, tm=tm),
        grid=(n // tm,),
        in_specs=[pl.BlockSpec((tm, D_MODEL), lambda i: (i, 0)),
                  pl.BlockSpec((1, D_MODEL), lambda i: (0, 0)),
                  pl.BlockSpec((D_MODEL, LANES), lambda i: (0, 0)),
                  pl.BlockSpec((1, LANES), lambda i: (0, 0))],
        out_specs=[pl.BlockSpec((HALF // SC_PIECE, tm, SC_PIECE), lambda i: (0, i, 0)),
                   pl.BlockSpec((tm, LANES), lambda i: (i, 0)),
                   pl.BlockSpec((8, LANES), lambda i: (0, 0))],
        out_shape=[jax.ShapeDtypeStruct((HALF // SC_PIECE, n, SC_PIECE), jnp.int32),
                   jax.ShapeDtypeStruct((n, LANES), F32),
                   jax.ShapeDtypeStruct((8, LANES), F32)],
        scratch_shapes=[pltpu.VMEM((1, LANES), F32)],
        compiler_params=_cparams(("arbitrary",)),
    )(x2, norm_w.reshape(1, D_MODEL), wr, br)


def _sc_rows_multiple(d):
    return SC_CORES * SC_SUBCORES * SC_WINDOW * SC_PIECE // d


def _sc_gather_rows(table, idx):
    f, v, _ = table.shape
    b = idx.shape[0]
    assert b % _sc_rows_multiple(f * SC_PIECE) == 0
    idx_all = (idx[None, :] + (jnp.arange(f, dtype=jnp.int32) * v)[:, None]).reshape(-1)
    return _sc_gather_pieces(table.reshape(f * v, SC_PIECE), idx_all).reshape(f, b, SC_PIECE)


def _sc_gather_pieces(table, idx):
    bp = idx.shape[0]
    d = table.shape[1]
    window = SC_WINDOW
    idx2 = idx.reshape(1, bp)
    mesh = plsc.VectorSubcoreMesh(core_axis_name="core", subcore_axis_name="subcore")

    @functools.partial(pl.kernel, out_type=jax.ShapeDtypeStruct((bp, d), table.dtype), mesh=mesh)
    def gather(x_hbm, i_hbm, o_hbm):
        def body(i_vmem, o_vmem):
            pltpu.sync_copy(x_hbm.at[i_vmem.at[0]], o_vmem)

        pltpu.emit_pipeline(
            body,
            grid=(bp // window,),
            in_specs=[pl.BlockSpec((1, window), index_map=lambda i: (0, i))],
            out_specs=[pl.BlockSpec((window, d), index_map=lambda i: (i, 0))],
            core_axis_name=("core", "subcore"),
            dimension_semantics=(pltpu.PARALLEL,),
        )(i_hbm, o_hbm)

    return gather(table, idx2)


def _experts_kernel(be_ref, nb_ref, xs_ref, w1_ref, w3_ref, w2_ref, o_ref):
    g, c = pl.program_id(0), pl.program_id(1)

    @pl.when(c == 0)
    def _():
        o_ref[...] = jnp.zeros_like(o_ref)

    @pl.when(g < nb_ref[0])
    def _():
        xb = _unpack_bf16_pairs([xs_ref[q] for q in range(HALF // SC_PIECE)])
        a = (_silu(jnp.dot(xb, w1_ref[...], preferred_element_type=F32))
             * jnp.dot(xb, w3_ref[...], preferred_element_type=F32))
        y = jnp.dot(a.astype(BF16), w2_ref[...], preferred_element_type=F32)
        for q in range(D_MODEL // SC_PIECE):
            o_ref[q] += y[:, q * SC_PIECE:(q + 1) * SC_PIECE]


def _experts(xs, block_expert, n_blocks, w1, w3, w2, blk, tf):
    n_slots = xs.shape[1]
    dff = w1.shape[2]
    grid_spec = pltpu.PrefetchScalarGridSpec(
        num_scalar_prefetch=2,
        grid=(n_slots // blk, dff // tf),
        in_specs=[pl.BlockSpec((HALF // SC_PIECE, blk, SC_PIECE), lambda g, c, be, nb: (0, g, 0)),
                  pl.BlockSpec((None, D_MODEL, tf), lambda g, c, be, nb: (be[g], 0, c)),
                  pl.BlockSpec((None, D_MODEL, tf), lambda g, c, be, nb: (be[g], 0, c)),
                  pl.BlockSpec((None, tf, D_MODEL), lambda g, c, be, nb: (be[g], c, 0))],
        out_specs=pl.BlockSpec((D_MODEL // SC_PIECE, blk, SC_PIECE), lambda g, c, be, nb: (0, g, 0)),
    )
    return pl.pallas_call(
        _experts_kernel,
        grid_spec=grid_spec,
        out_shape=jax.ShapeDtypeStruct((D_MODEL // SC_PIECE, n_slots, SC_PIECE), F32),
        compiler_params=_cparams(("parallel", "arbitrary")),
    )(block_expert, n_blocks, xs, w1, w3, w2)


def _combine_kernel(x_ref, y1_ref, y2_ref, tok_ref, nw_ref, o_ref):
    tok = tok_ref[...]
    rows = lambda y_ref: jnp.concatenate([y_ref[q] for q in range(D_MODEL // SC_PIECE)], axis=1)
    x = x_ref[...] + tok[:, 4:5] * rows(y1_ref) + tok[:, 5:6] * rows(y2_ref)
    ms = jnp.mean(x * x, axis=-1, keepdims=True)
    o_ref[...] = x * lax.rsqrt(ms + NORM_EPS) * nw_ref[...]


def _combine(x2, ys2, tok, norm_w, tm):
    n = x2.shape[0]
    spec = pl.BlockSpec((tm, D_MODEL), lambda i: (i, 0))
    yspec = lambda off: pl.BlockSpec((D_MODEL // SC_PIECE, tm, SC_PIECE), lambda i: (0, i + off, 0))
    return pl.pallas_call(
        _combine_kernel,
        grid=(n // tm,),
        in_specs=[spec, yspec(0), yspec(n // tm),
                  pl.BlockSpec((tm, LANES), lambda i: (i, 0)), pl.BlockSpec((1, D_MODEL), lambda i: (0, 0))],
        out_specs=spec,
        out_shape=jax.ShapeDtypeStruct((n, D_MODEL), F32),
        compiler_params=_cparams(("parallel",)),
    )(x2, ys2, ys2, tok, norm_w.reshape(1, D_MODEL))


def _moe_final(x2, norm_w, w_router, b_router, w1, w3, w2, final_norm_w):
    n = x2.shape[0]
    blk = min(MOE_BLOCK, n)
    hp, tok, cnt = _route(x2, norm_w, w_router, b_router, ROUTER_ROWS)
    counts = cnt[0, :N_EXPERTS].astype(jnp.int32)
    padded = (counts + blk - 1) // blk * blk
    ends = jnp.cumsum(padded)
    base = ends - padded
    round_up = lambda a, m: -(-a // m) * m
    n_slots = round_up((-(-2 * n // blk) + N_EXPERTS) * blk, _sc_rows_multiple(HALF))
    n_blocks = n_slots // blk
    i12 = tok[:, 0:2].astype(jnp.int32)
    pos = (base[i12] + tok[:, 2:4].astype(jnp.int32)).T.reshape(-1)
    rows = jnp.tile(jnp.arange(n, dtype=jnp.int32), 2)
    src = jnp.zeros((n_slots,), jnp.int32).at[pos].set(rows, unique_indices=True, mode='promise_in_bounds')
    block_expert = jnp.minimum(
        jnp.searchsorted(ends, jnp.arange(n_blocks, dtype=jnp.int32) * blk, side='right'), N_EXPERTS - 1
    ).astype(jnp.int32)
    xs = _sc_gather_rows(hp, src)
    ys = _experts(xs, block_expert, (ends[-1:] // blk).astype(jnp.int32),
                  w1.astype(BF16), w3.astype(BF16), w2.astype(BF16), blk, MOE_FF)
    pos = jnp.pad(pos, (0, round_up(2 * n, _sc_rows_multiple(D_MODEL)) - 2 * n))
    ys2 = _sc_gather_rows(ys, pos)
    return _combine(x2, ys2, tok, final_norm_w, min(1024, n))


def _final_kernel(x_ref, m_ref, nw_ref, o_ref):
    x = x_ref[...] + m_ref[...]
    ms = jnp.mean(x * x, axis=-1, keepdims=True)
    o_ref[...] = x * lax.rsqrt(ms + NORM_EPS) * nw_ref[...]


def _final(x2, m2, norm_w, tm):
    n = x2.shape[0]
    spec = pl.BlockSpec((tm, D_MODEL), lambda i: (i, 0))
    return pl.pallas_call(
        _final_kernel,
        grid=(n // tm,),
        in_specs=[spec, spec, pl.BlockSpec((1, D_MODEL), lambda i: (0, 0))],
        out_specs=spec,
        out_shape=jax.ShapeDtypeStruct((n, D_MODEL), F32),
        compiler_params=_cparams(("parallel",)),
    )(x2, m2, norm_w.reshape(1, D_MODEL))


def _permute_w_in(w):
    s5, rw, sg = w[:, 0:256], w[:, 256:1280], w[:, 1280:1792]
    gd, ab, gates = w[:, 1792:2816], w[:, 2816:2824], w[:, 2824:6920]
    pad = jnp.zeros((D_MODEL, PROJ_COLS - COL_AB - 8), w.dtype)
    return jnp.concatenate([rw, gd, sg, s5, ab, pad], axis=1).astype(BF16), gates.astype(BF16)


def _run_trunk(x, s5_h, rw_s, rw_shift, gd_s, gd_conv, p, w_in_perm):
    bsz, t, _ = x.shape
    n = bsz * t
    tm = min(1024, n)
    l = min(64, t)
    tc_s5 = min(256, t)
    new = ([], [], [], [], [], [])
    x2 = x.reshape(n, D_MODEL)
    moe_out = None
    for layer in range(2):
        g = lambda name: p[name][layer]
        w_mix, w_gate = w_in_perm[layer]
        proj = _norm_proj(x2, g('norm1_w'), w_mix, tm, 1024)
        p3 = proj.reshape(bsz, t, PROJ_COLS)
        y_a, s5_new = _s5_mixer(p3, s5_h[layer], g('s5_lam_re'), g('s5_lam_im'), g('s5_log_dt'), g('s5_b'),
                                g('s5_c'), g('s5_d'), g('s5_w_glu'), g('s5_b_glu'), tc_s5)
        y_b, rw_new = _rwkv_mixer(p3, rw_shift[layer], rw_s[layer], g('rw_mu'), g('rw_w0'), g('rw_w2'),
                                  g('rw_a0'), g('rw_a2'), g('rw_g2'), g('rw_k_k'), g('rw_k_a'), g('rw_r_k'),
                                  g('rw_ln_w'), g('rw_ln_b'), l)
        y_c, sg_v = _sgu_mixer(p3, g('sg_ln_w'), g('sg_ln_b'), g('sg_w_s'), g('sg_b_s'))
        y_d, gd_new = _gdn_mixer(p3, gd_conv[layer], gd_s[layer], g('gd_conv_w'), g('gd_a_log'),
                                 g('gd_dt_bias'), g('gd_norm_w'), l)
        shift_new = p3[:, t - 1, COL_RW:COL_RW + RW_COLS]
        conv_new = p3[:, t - (GD_CONV - 1):, COL_GD:COL_GD + GD_QKV]
        x2 = _merge((y_a, y_b, y_c, y_d), x2, g('norm1_w'), w_gate, g('w_branch'), g('w_out'), min(512, n))
        j = layer // 2
        if layer % 2 == 0:
            x2 = _ffn(x2, g('norm2_w'), p['ffn_w1'][j], p['ffn_w3'][j], p['ffn_w2'][j], min(512, n), 1408)
        else:
            y = _moe_final(x2, g('norm2_w'), p['moe_router'][j], p['moe_router_b'][j],
                           p['moe_w1'][j], p['moe_w3'][j], p['moe_w2'][j], p['final_norm_w'])
        for lst, s in zip(new, (s5_new, rw_new, shift_new, gd_new, conv_new, sg_v)):
            lst.append(s)
    return y.reshape(bsz, t, D_MODEL), [jnp.stack(lst) for lst in new]


def kernel(x_prompt, x_sample, state_s5, state_rwkv, state_rwkv_shift, state_gdn, state_gdn_conv, norm1_w, w_in, s5_lam_re, s5_lam_im, s5_log_dt, s5_b, s5_c, s5_d, s5_w_glu, s5_b_glu, rw_mu, rw_w0, rw_w2, rw_a0, rw_a2, rw_g2, rw_k_k, rw_k_a, rw_r_k, rw_ln_w, rw_ln_b, sg_ln_w, sg_ln_b, sg_w_s, sg_b_s, gd_conv_w, gd_a_log, gd_dt_bias, gd_norm_w, w_branch, w_out, norm2_w, ffn_w1, ffn_w3, ffn_w2, moe_router, moe_router_b, moe_w1, moe_w3, moe_w2, final_norm_w):
    p = {
        'norm1_w': norm1_w, 's5_lam_re': s5_lam_re, 's5_lam_im': s5_lam_im, 's5_log_dt': s5_log_dt,
        's5_b': s5_b, 's5_c': s5_c, 's5_d': s5_d, 's5_w_glu': s5_w_glu, 's5_b_glu': s5_b_glu,
        'rw_mu': rw_mu, 'rw_w0': rw_w0, 'rw_w2': rw_w2, 'rw_a0': rw_a0, 'rw_a2': rw_a2, 'rw_g2': rw_g2,
        'rw_k_k': rw_k_k, 'rw_k_a': rw_k_a, 'rw_r_k': rw_r_k, 'rw_ln_w': rw_ln_w, 'rw_ln_b': rw_ln_b,
        'sg_ln_w': sg_ln_w, 'sg_ln_b': sg_ln_b, 'sg_w_s': sg_w_s, 'sg_b_s': sg_b_s,
        'gd_conv_w': gd_conv_w, 'gd_a_log': gd_a_log, 'gd_dt_bias': gd_dt_bias, 'gd_norm_w': gd_norm_w,
        'w_branch': w_branch, 'w_out': w_out, 'norm2_w': norm2_w,
        'ffn_w1': ffn_w1, 'ffn_w3': ffn_w3, 'ffn_w2': ffn_w2,
        'moe_router': moe_router, 'moe_router_b': moe_router_b, 'moe_w1': moe_w1, 'moe_w3': moe_w3, 'moe_w2': moe_w2,
        'final_norm_w': final_norm_w,
    }
    w_in_perm = [_permute_w_in(w_in[layer]) for layer in range(2)]
    bp, dt = x_prompt.shape[0], x_prompt.dtype
    depth = w_in.shape[0]
    y_prompt, (s5_p, rw_p, rwsh_p, gd_p, gdc_p, _) = _run_trunk(
        x_prompt,
        jnp.zeros((depth, bp, S5_GROUPS, S5_STATE, 2), dt),
        jnp.zeros((depth, bp, HEADS, HEAD_W, HEAD_W), dt),
        jnp.zeros((depth, bp, RW_COLS), dt),
        jnp.zeros((depth, bp, HEADS, HEAD_W, HEAD_W), dt),
        jnp.zeros((depth, bp, GD_CONV - 1, GD_QKV), dt),
        p, w_in_perm)
    y_sample, (s5_s, rw_s, rwsh_s, gd_s, gdc_s, sgv_s) = _run_trunk(
        x_sample, state_s5, state_rwkv, state_rwkv_shift, state_gdn, state_gdn_conv, p, w_in_perm)
    return (y_prompt, y_sample, s5_p, rw_p, rwsh_p, gd_p, gdc_p, s5_s, rw_s, rwsh_s, gd_s, gdc_s, sgv_s)
```

```python
import functools

import jax
import jax.numpy as jnp
from jax import lax
from jax.experimental import pallas as pl
from jax.experimental.pallas import tpu as pltpu
from jax.experimental.pallas import tpu_sc as plsc

F32 = jnp.float32
BF16 = jnp.bfloat16

D_MODEL = 1024
BRANCH_W = 256
HEADS = 4
HEAD_W = 64
S5_GROUPS = 16
S5_GROUP = 16
S5_STATE = 64
S5_W = S5_GROUPS * S5_STATE
S5_ROWS = 8
SG_CHUNK = 128
SG_ROWS = 512
GD_CONV = 4
GD_QKV = 3 * BRANCH_W
RW_COLS = 1024
RW_EPS = 64e-5
NORM_EPS = 1e-6
N_EXPERTS = 8
LANES = 128

COL_RW = 0
COL_GD = 1024
COL_SG = 2048
COL_S5 = 2560
COL_AB = 2816
PROJ_COLS = 3072

VMEM_LIMIT = 48 * 1024 * 1024


def _cparams(sem):
    return pltpu.CompilerParams(dimension_semantics=sem, vmem_limit_bytes=VMEM_LIMIT)


def _dot(a, b):
    return jnp.dot(a.astype(BF16), b.astype(BF16), preferred_element_type=F32)


def _dot_nt(a, b):
    return lax.dot_general(a.astype(BF16), b.astype(BF16), (((1,), (1,)), ((), ())),
                           preferred_element_type=F32)


def _split3(a):
    hi = a.astype(BF16)
    r1 = a - hi.astype(F32)
    mid = r1.astype(BF16)
    lo = (r1 - mid.astype(F32)).astype(BF16)
    return hi, mid, lo


def _dot3_left(b_exact, a):
    hi, mid, lo = _split3(a)
    b = b_exact.astype(BF16)
    return (jnp.dot(b, hi, preferred_element_type=F32) + jnp.dot(b, mid, preferred_element_type=F32)
            + jnp.dot(b, lo, preferred_element_type=F32))


def _dot_hp(a, b):
    a0, a1, a2 = _split3(a)
    b0, b1, b2 = _split3(b)
    d = lambda x, y: jnp.dot(x, y, preferred_element_type=F32)
    return d(a0, b0) + (d(a0, b1) + d(a1, b0)) + (d(a0, b2) + d(a1, b1) + d(a2, b0))


def _iota(shape, axis):
    return lax.broadcasted_iota(jnp.int32, shape, axis)


def _head_ones():
    r = _iota((BRANCH_W, BRANCH_W), 0) // HEAD_W
    c = _iota((BRANCH_W, BRANCH_W), 1) // HEAD_W
    return (r == c).astype(BF16)


def _head_mask(l):
    r = _iota((HEADS * l, BRANCH_W), 0) // l
    c = _iota((HEADS * l, BRANCH_W), 1) // HEAD_W
    return r == c


def _expand(x, mask):
    return jnp.where(mask, jnp.concatenate([x] * HEADS, axis=0), 0.0)


def _tri_masks(l):
    i = _iota((l, HEADS * l), 0)
    j = _iota((l, HEADS * l), 1) % l
    return j < i, j <= i


def _same_head(l):
    n = HEADS * l
    return (_iota((n, n), 0) // l) == (_iota((n, n), 1) // l)


def _expand_sq(x, same):
    return jnp.where(same, jnp.concatenate([x] * HEADS, axis=0), 0.0)


def _unit_lower_inverse(a_strict, l):
    i = _iota((l, HEADS * l), 0)
    j = _iota((l, HEADS * l), 1) % l
    eye = (i == j).astype(F32)
    same = _same_head(l)
    p = [-a for a in a_strict]
    t = [eye + x for x in p]
    k = 2
    while k < l:
        p = [_dot(x, _expand_sq(x, same)) for x in p]
        t = [y + _dot(y, _expand_sq(x, same)) for y, x in zip(t, p)]
        k *= 2
    return t


def _dot2(a, b_exact):
    hi = a.astype(BF16)
    lo = (a - hi.astype(F32)).astype(BF16)
    b = b_exact.astype(BF16)
    return jnp.dot(hi, b, preferred_element_type=F32) + jnp.dot(lo, b, preferred_element_type=F32)


def _cumsum_rows(x, l, nb=1):
    n = nb * l
    i, j = _iota((n, n), 0), _iota((n, n), 1)
    tri = ((j <= i) & ((i // l) == (j // l))).astype(BF16)
    return _dot3_left(tri, x)


def _softplus(x):
    return jnp.maximum(x, 0.0) + jnp.log(1.0 + jnp.exp(-jnp.abs(x)))


def _silu(x):
    return x * jax.nn.sigmoid(x)


def _norm_proj_kernel(x_ref, nw_ref, w_ref, o_ref, h_ref):
    @pl.when(pl.program_id(1) == 0)
    def _():
        x = x_ref[...]
        ms = jnp.mean(x * x, axis=-1, keepdims=True)
        h_ref[...] = (x * lax.rsqrt(ms + NORM_EPS) * nw_ref[...]).astype(BF16)

    o_ref[...] = jnp.dot(h_ref[...], w_ref[...], preferred_element_type=F32)


def _norm_proj(x2, norm_w, w_bf16, tm, tn):
    n = x2.shape[0]
    ncol = w_bf16.shape[1]
    return pl.pallas_call(
        _norm_proj_kernel,
        grid=(n // tm, ncol // tn),
        in_specs=[pl.BlockSpec((tm, D_MODEL), lambda i, j: (i, 0)),
                  pl.BlockSpec((1, D_MODEL), lambda i, j: (0, 0)),
                  pl.BlockSpec((D_MODEL, tn), lambda i, j: (0, j))],
        out_specs=pl.BlockSpec((tm, tn), lambda i, j: (i, j)),
        out_shape=jax.ShapeDtypeStruct((n, ncol), F32),
        scratch_shapes=[pltpu.VMEM((tm, D_MODEL), BF16)],
        compiler_params=_cparams(("parallel", "arbitrary")),
    )(x2, norm_w.reshape(1, D_MODEL), w_bf16)


def _s5_kernel(u_ref, h0_ref, wb_ref, wc_ref, a2k_ref, apow_ref, d_ref, wg_ref, bg_ref,
               y_ref, hl_ref, hs_ref, hb_ref, *, tc):
    c = pl.program_id(1)

    @pl.when(c == 0)
    def _():
        hs_ref[...] = h0_ref[...]

    u = u_ref[...]
    x = _dot(u, wb_ref[...])
    xr, xi = x[:, :S5_W], x[:, S5_W:]
    ng = tc // S5_ROWS
    xr = xr.reshape(ng, S5_ROWS, S5_W)
    xi = xi.reshape(ng, S5_ROWS, S5_W)
    row = _iota((S5_ROWS, 1), 0)
    k, d = 0, 1
    while d < S5_ROWS:
        m = row >= d
        ar = jnp.where(m, a2k_ref[k:k + 1, :S5_W], 0.0)
        ai = jnp.where(m, a2k_ref[k:k + 1, S5_W:], 0.0)
        sr = pltpu.roll(xr, d, axis=1)
        si = pltpu.roll(xi, d, axis=1)
        xr, xi = xr + (ar * sr - ai * si), xi + (ar * si + ai * sr)
        k, d = k + 1, d * 2
    hb_ref[:, :S5_W] = xr.reshape(tc, S5_W)
    hb_ref[:, S5_W:] = xi.reshape(tc, S5_W)
    pr, pi_ = apow_ref[:, :S5_W], apow_ref[:, S5_W:]

    def group(gi, carry):
        cr, ci = carry
        rows = pl.ds(pl.multiple_of(gi * S5_ROWS, S5_ROWS), S5_ROWS)
        hr = hb_ref[rows, :S5_W] + (pr * cr - pi_ * ci)
        hi = hb_ref[rows, S5_W:] + (pr * ci + pi_ * cr)
        hb_ref[rows, :S5_W] = hr
        hb_ref[rows, S5_W:] = hi
        return hr[S5_ROWS - 1:S5_ROWS], hi[S5_ROWS - 1:S5_ROWS]

    cr, ci = lax.fori_loop(0, tc // S5_ROWS, group, (hs_ref[:, :S5_W], hs_ref[:, S5_W:]), unroll=4)
    hs_ref[:, :S5_W] = cr
    hs_ref[:, S5_W:] = ci
    y = _dot(hb_ref[:, :S5_W], wc_ref[:S5_W]) + _dot(hb_ref[:, S5_W:], wc_ref[S5_W:])
    y = jax.nn.gelu(y + d_ref[...] * u)
    y = y * jax.nn.sigmoid(_dot(y, wg_ref[...]) + bg_ref[...])
    y_ref[...] = y.astype(y_ref.dtype)

    @pl.when(c == pl.num_programs(1) - 1)
    def _():
        hl_ref[...] = hs_ref[...]


def _s5_tables(lam_re, lam_im, log_dt, b_c, c_c, tc):
    dt = jnp.exp(log_dt)[:, None]
    mag = jnp.exp(lam_re * dt)
    ab_re, ab_im = mag * jnp.cos(lam_im * dt), mag * jnp.sin(lam_im * dt)
    den = lam_re * lam_re + lam_im * lam_im
    nr = ab_re - 1.0
    cf_re = (nr * lam_re + ab_im * lam_im) / den
    cf_im = (ab_im * lam_re - nr * lam_im) / den
    br, bi = b_c[..., 0], b_c[..., 1]
    bb_re = cf_re[..., None] * br - cf_im[..., None] * bi
    bb_im = cf_re[..., None] * bi + cf_im[..., None] * br
    eye = jnp.eye(S5_GROUPS, dtype=F32)
    bd_in = lambda m: jnp.einsum('gph,gk->ghkp', m, eye).reshape(BRANCH_W, S5_W)
    wb = jnp.concatenate([bd_in(bb_re), bd_in(bb_im)], axis=1)
    cr, ci = c_c[..., 0], c_c[..., 1]
    bd_out = lambda m: jnp.einsum('ghp,gk->gpkh', m, eye).reshape(S5_W, BRANCH_W)
    wc = jnp.concatenate([bd_out(cr), -bd_out(ci)], axis=0)
    pr, pi_ = ab_re.reshape(1, S5_W), ab_im.reshape(1, S5_W)
    lv_r, lv_i = [], []
    tr, ti = pr, pi_
    d = 1
    while d < tc:
        lv_r.append(pr)
        lv_i.append(pi_)
        tr, ti = (jnp.concatenate([tr, tr * pr - ti * pi_], axis=0),
                  jnp.concatenate([ti, tr * pi_ + ti * pr], axis=0))
        pr, pi_ = pr * pr - pi_ * pi_, 2.0 * pr * pi_
        d *= 2
    n_lv = len(lv_r)
    pad = (-n_lv) % 8
    a2k = jnp.concatenate([jnp.concatenate(lv_r, axis=0), jnp.concatenate(lv_i, axis=0)], axis=1)
    a2k = jnp.pad(a2k, ((0, pad), (0, 0)))
    apow = jnp.concatenate([tr, ti], axis=1)
    return wb.astype(BF16), wc.astype(BF16), a2k, apow


def _s5_mixer(p3, h0, lam_re, lam_im, log_dt, b_c, c_c, d_skip, w_glu, b_glu, tc):
    bsz, t, _ = p3.shape
    wb, wc, a2k, apow = _s5_tables(lam_re, lam_im, log_dt, b_c, c_c, S5_ROWS)
    h0f = jnp.concatenate([h0[..., 0].reshape(bsz, 1, S5_W), h0[..., 1].reshape(bsz, 1, S5_W)], axis=-1)
    full = lambda a: pl.BlockSpec(a.shape, lambda b, c: (0,) * a.ndim)
    d2, bg2, wg = d_skip.reshape(1, BRANCH_W), b_glu.reshape(1, BRANCH_W), w_glu.astype(BF16)
    y, hl = pl.pallas_call(
        functools.partial(_s5_kernel, tc=tc),
        grid=(bsz, t // tc),
        in_specs=[pl.BlockSpec((None, tc, BRANCH_W), lambda b, c: (b, c, COL_S5 // BRANCH_W)),
                  pl.BlockSpec((None, 1, 2 * S5_W), lambda b, c: (b, 0, 0)),
                  full(wb), full(wc), full(a2k), full(apow), full(d2), full(wg), full(bg2)],
        out_specs=[pl.BlockSpec((None, tc, BRANCH_W), lambda b, c: (b, c, 0)),
                   pl.BlockSpec((None, 1, 2 * S5_W), lambda b, c: (b, 0, 0))],
        out_shape=[jax.ShapeDtypeStruct((bsz, t, BRANCH_W), BF16),
                   jax.ShapeDtypeStruct((bsz, 1, 2 * S5_W), F32)],
        scratch_shapes=[pltpu.VMEM((1, 2 * S5_W), F32), pltpu.VMEM((tc, 2 * S5_W), F32)],
        compiler_params=_cparams(("parallel", "arbitrary")),
    )(p3, h0f, wb, wc, a2k, apow, d2, wg, bg2)
    h_last = jnp.stack([hl[:, 0, :S5_W].reshape(bsz, S5_GROUPS, S5_STATE),
                        hl[:, 0, S5_W:].reshape(bsz, S5_GROUPS, S5_STATE)], axis=-1)
    return y, h_last


def _rwkv_kernel(z_ref, sh0_ref, s0_ref, mu_ref, w0_ref, w2_ref, a0_ref, a2_ref, g2_ref,
                 kk_ref, ka_ref, rk_ref, lnw_ref, lnb_ref,
                 y_ref, sl_ref, st_ref, zp_ref, *, l, bb, nt):
    c = pl.program_id(1)

    @pl.when(c == 0)
    def _():
        st_ref[...] = s0_ref[...]
        zp_ref[...] = sh0_ref[...]

    tl = nt * l
    row = _iota((tl, 1), 0)
    zms = []
    for b in range(bb):
        z = z_ref[b]
        prev = jnp.where(row == 0, zp_ref[b], pltpu.roll(z, 1, axis=0))
        zp_ref[b] = z[tl - 1:tl]
        zms.append(z + (prev - z) * mu_ref[...])
    zm = jnp.concatenate(zms, axis=0)
    r, k, v = zm[:, 0:256], zm[:, 256:512], zm[:, 512:768]
    lo = zm[:, 768:896]
    g_lo = zm[:, 896:1024]
    w_log = -_softplus(-(w0_ref[...] + _dot(jnp.tanh(lo), w2_ref[...]))) - 0.5
    lw = -jnp.exp(w_log)
    a = jax.nn.sigmoid(a0_ref[...] + _dot(lo, a2_ref[...]))
    g = _dot(jax.nn.sigmoid(g_lo), g2_ref[...])
    ones_h = _head_ones()
    kk = k * kk_ref[...]
    kk = kk * lax.rsqrt(_dot2(kk * kk, ones_h) + NORM_EPS)
    k = k * (1.0 + (a - 1.0) * ka_ref[...])
    kka = kk * a

    cum = _cumsum_rows(lw, l, bb * nt)
    p_incl = jnp.exp(cum)
    p_inv = jnp.exp(-cum)
    kt = kk * jnp.exp(cum - lw)
    rt = r * p_incl
    kh = k * p_inv
    ah = kka * p_inv

    hm = _head_mask(l)
    strict, incl = _tri_masks(l)
    chunks = [(b, j) for j in range(nt) for b in range(bb)]
    rows = {s: slice((s[0] * nt + s[1]) * l, (s[0] * nt + s[1] + 1) * l) for s in chunks}
    ex = lambda x: {s: _expand(x[rows[s]], hm) for s in chunks}
    kh_e, ah_e, v_e = ex(kh), ex(ah), ex(v)
    a_aa = [jnp.where(strict, _dot_nt(kt[rows[s]], ah_e[s]), 0.0) for s in chunks]
    a_ak = {s: jnp.where(strict, _dot_nt(kt[rows[s]], kh_e[s]), 0.0) for s in chunks}
    b_ra = {s: jnp.where(incl, _dot_nt(rt[rows[s]], ah_e[s]), 0.0) for s in chunks}
    b_rk = {s: jnp.where(incl, _dot_nt(rt[rows[s]], kh_e[s]), 0.0) for s in chunks}
    t_inv = dict(zip(chunks, _unit_lower_inverse(a_aa, l)))
    av = {s: _dot(a_ak[s], v_e[s]) for s in chunks}
    bv = {s: _dot(b_rk[s], v_e[s]) for s in chunks}
    to_end = {s: jnp.exp(cum[rows[s]][l - 1:l] - cum[rows[s]]) for s in chunks}
    head_blk = _same_head(HEAD_W)
    st = [st_ref[b] for b in range(bb)]
    yb = {}
    for j in range(nt):
        now = [(b, j) for b in range(bb)]
        rhs = {s: _dot_nt(kt[rows[s]], st[s[0]]) + av[s] for s in now}
        ys = {s: _dot_nt(rt[rows[s]], st[s[0]]) + bv[s] for s in now}
        u = {s: _dot(t_inv[s], _expand(rhs[s], hm)) for s in now}
        for s in now:
            yb[s] = ys[s] - _dot(b_ra[s], _expand(u[s], hm))
        lhs_t = {s: jnp.concatenate([v[rows[s]], -u[s]], axis=0).T for s in now}
        rhs_k = {s: jnp.concatenate([k[rows[s]] * to_end[s], kka[rows[s]] * to_end[s]], axis=0) for s in now}
        for s in now:
            p_last = p_incl[rows[s]][l - 1:l]
            st[s[0]] = st[s[0]] * p_last + jnp.where(head_blk, _dot(lhs_t[s], rhs_k[s]), 0.0)
    for b in range(bb):
        st_ref[b] = st[b]
    y = jnp.concatenate([yb[(b, j)] for b in range(bb) for j in range(nt)], axis=0)

    inv_w = 1.0 / HEAD_W
    mean = _dot2(y, ones_h) * inv_w
    yc = y - mean
    var = _dot2(yc * yc, ones_h) * inv_w
    y = yc * lax.rsqrt(var + RW_EPS) * lnw_ref[...] + lnb_ref[...]
    bonus = _dot2(r * k * rk_ref[...], ones_h) * v
    y = ((y + bonus) * g).astype(y_ref.dtype)
    for b in range(bb):
        y_ref[b] = y[b * tl:(b + 1) * tl]

    @pl.when(c == pl.num_programs(1) - 1)
    def _():
        sl_ref[...] = st_ref[...]


STACK_STREAMS = 4
STACK_ROWS = 512


def _streams_per_step(bsz, t, l):
    bb = min(bsz, STACK_STREAMS)
    while bsz % bb:
        bb -= 1
    nt = max(1, min(t, STACK_ROWS // bb) // l)
    while (t // l) % nt:
        nt -= 1
    return bb, nt


def _block_diag_heads(s):
    bsz = s.shape[0]
    eye = jnp.eye(HEADS, dtype=s.dtype)
    return jnp.einsum('bhij,hg->bhigj', s, eye).reshape(bsz, BRANCH_W, BRANCH_W)


def _diag_blocks(s):
    bsz = s.shape[0]
    s5 = s.reshape(bsz, HEADS, HEAD_W, HEADS, HEAD_W)
    return jnp.stack([s5[:, h, :, h, :] for h in range(HEADS)], axis=1)


def _rwkv_mixer(p3, shift0, s0, mu, w0, w2, a0, a2, g2, k_k, k_a, r_k, ln_w, ln_b, l):
    bsz, t, _ = p3.shape
    row = lambda a: a.reshape(1, -1)
    w2p = jnp.concatenate([w2, jnp.zeros_like(w2)], axis=0).astype(BF16)
    a2p = jnp.concatenate([jnp.zeros_like(a2), a2], axis=0).astype(BF16)
    args = (shift0.reshape(bsz, 1, RW_COLS), _block_diag_heads(s0), row(mu), row(w0), w2p, row(a0), a2p,
            g2.astype(BF16), row(k_k), row(k_a), row(r_k), row(ln_w), row(ln_b))
    full = lambda a: pl.BlockSpec(a.shape, lambda b, c: (0,) * a.ndim)
    bb, nt = _streams_per_step(bsz, t, l)
    y, sl = pl.pallas_call(
        functools.partial(_rwkv_kernel, l=l, bb=bb, nt=nt),
        grid=(bsz // bb, t // (nt * l)),
        in_specs=[pl.BlockSpec((bb, nt * l, RW_COLS), lambda b, c: (b, c, COL_RW // RW_COLS)),
                  pl.BlockSpec((bb, 1, RW_COLS), lambda b, c: (b, 0, 0)),
                  pl.BlockSpec((bb, BRANCH_W, BRANCH_W), lambda b, c: (b, 0, 0))]
                 + [full(a) for a in args[2:]],
        out_specs=[pl.BlockSpec((bb, nt * l, BRANCH_W), lambda b, c: (b, c, 0)),
                   pl.BlockSpec((bb, BRANCH_W, BRANCH_W), lambda b, c: (b, 0, 0))],
        out_shape=[jax.ShapeDtypeStruct((bsz, t, BRANCH_W), BF16),
                   jax.ShapeDtypeStruct((bsz, BRANCH_W, BRANCH_W), F32)],
        scratch_shapes=[pltpu.VMEM((bb, BRANCH_W, BRANCH_W), F32), pltpu.VMEM((bb, 1, RW_COLS), F32)],
        compiler_params=_cparams(("parallel", "arbitrary")),
    )(p3, *args)
    return y, _diag_blocks(sl)


def _sgu_kernel(z_ref, lnw_ref, lnb_ref, wm_ref, bias_ref, o_ref, v_ref, *, l, nc):
    zg = jax.nn.gelu(z_ref[...])
    u, v = zg[:, :BRANCH_W], zg[:, BRANCH_W:]
    mean = jnp.mean(v, axis=-1, keepdims=True)
    vc = v - mean
    var = jnp.mean(vc * vc, axis=-1, keepdims=True)
    v = vc * lax.rsqrt(var + NORM_EPS) * lnw_ref[...] + lnb_ref[...]
    v_ref[...] = v
    hm = _head_mask(l)
    wm = wm_ref[...]
    for i in range(nc):
        rows = slice(i * l, (i + 1) * l)
        mixed = bias_ref[...] + _dot(wm, _expand(v[rows], hm))
        o_ref[rows, :] = (u[rows] * mixed).astype(o_ref.dtype)


def _sgu_mixer(p3, ln_w, ln_b, w_s, b_s):
    bsz, t, _ = p3.shape
    l = min(SG_CHUNK, t)
    nc = max(1, min(SG_ROWS, t) // l)
    tril = jnp.tril(jnp.ones((l, l), F32))
    wm = jnp.transpose(w_s[:, :l, :l] * tril, (1, 0, 2)).reshape(l, HEADS * l).astype(BF16)
    bias = jnp.repeat(jnp.transpose(b_s[:, :l]), HEAD_W, axis=1)
    row = lambda a: a.reshape(1, -1)
    full = lambda a: pl.BlockSpec(a.shape, lambda b, c: (0,) * a.ndim)
    args = (row(ln_w), row(ln_b), wm, bias)
    return pl.pallas_call(
        functools.partial(_sgu_kernel, l=l, nc=nc),
        grid=(bsz, t // (nc * l)),
        in_specs=[pl.BlockSpec((None, nc * l, 2 * BRANCH_W), lambda b, c: (b, c, COL_SG // (2 * BRANCH_W)))]
                 + [full(a) for a in args],
        out_specs=[pl.BlockSpec((None, nc * l, BRANCH_W), lambda b, c: (b, c, 0)),
                   pl.BlockSpec((None, nc * l, BRANCH_W), lambda b, c: (b, c, 0))],
        out_shape=[jax.ShapeDtypeStruct((bsz, t, BRANCH_W), BF16),
                   jax.ShapeDtypeStruct((bsz, t, BRANCH_W), F32)],
        compiler_params=_cparams(("parallel", "parallel")),
    )(p3, *args)


def _gdn_kernel(z_ref, ab_ref, cv0_ref, s0_ref, cw_ref, alog_ref, dtb_ref, nw_ref,
                y_ref, sl_ref, st_ref, cv_ref, *, l, bb, nt):
    c = pl.program_id(1)

    @pl.when(c == 0)
    def _():
        st_ref[...] = s0_ref[...]
        cv_ref[...] = cv0_ref[...]

    tl = nt * l
    row8 = _iota((8, 1), 0)
    convs, gates = [], []
    for b in range(bb):
        z = z_ref[b]
        qkv = z[:, :GD_QKV]
        gates.append(z[:, GD_QKV:])
        carry = cv_ref[b]
        cv_ref[b] = qkv[tl - 8:tl]
        conv = qkv * cw_ref[GD_CONV - 1:GD_CONV]
        for j in range(1, GD_CONV):
            sh = pltpu.roll(qkv, j, axis=0)
            top = jnp.where(row8 < j, pltpu.roll(carry, j, axis=0), sh[:8])
            sh = jnp.concatenate([top, sh[8:]], axis=0) if tl > 8 else top
            conv = conv + sh * cw_ref[GD_CONV - 1 - j:GD_CONV - j]
        convs.append(conv)
    conv = _silu(jnp.concatenate(convs, axis=0))
    gate = jnp.concatenate(gates, axis=0)
    q, k, v = conv[:, :256], conv[:, 256:512], conv[:, 512:768]
    ones_h = _head_ones()
    q = q * lax.rsqrt(_dot2(q * q, ones_h) + NORM_EPS) * (HEAD_W ** -0.5)
    k = k * lax.rsqrt(_dot2(k * k, ones_h) + NORM_EPS)
    ab = jnp.concatenate([ab_ref[b] for b in range(bb)], axis=0)
    lane_h = _iota((bb * tl, BRANCH_W), 1) // HEAD_W
    a_in = jnp.zeros((bb * tl, BRANCH_W), F32)
    b_in = jnp.zeros((bb * tl, BRANCH_W), F32)
    for h in range(HEADS):
        a_in = jnp.where(lane_h == h, ab[:, h:h + 1], a_in)
        b_in = jnp.where(lane_h == h, ab[:, HEADS + h:HEADS + h + 1], b_in)
    beta = jax.nn.sigmoid(b_in)
    g = -jnp.exp(alog_ref[...]) * _softplus(a_in + dtb_ref[...])
    gc = _cumsum_rows(g, l, bb * nt)
    eg = jnp.exp(gc)
    kb = k * beta
    vb = v * beta
    kbg = kb * eg
    qg = q * eg

    hm = _head_mask(l)
    strict, incl = _tri_masks(l)
    n = HEADS * l
    chunks = [(b, j) for j in range(nt) for b in range(bb)]
    rows = {s: slice((s[0] * nt + s[1]) * l, (s[0] * nt + s[1] + 1) * l) for s in chunks}
    ex = lambda x: {s: _expand(x[rows[s]], hm) for s in chunks}
    k_e, vb_e, kbg_e = ex(k), ex(vb), ex(kbg)
    lane_hd = _iota((l, n), 1) // l
    eye = _iota((l, n), 0) == (_iota((l, n), 1) % l)
    decay = {}
    for s in chunks:
        gi = jnp.zeros((l, n), F32)
        for h in range(HEADS):
            gi = jnp.where(lane_hd == h, gc[rows[s]][:, h * HEAD_W:h * HEAD_W + 1], gi)
        gj = jnp.sum(jnp.where(eye, gi, 0.0), axis=0, keepdims=True)
        decay[s] = jnp.where(incl, jnp.exp(jnp.where(incl, gi - gj, 0.0)), 0.0)
    lm = [jnp.where(strict, _dot_nt(kb[rows[s]], k_e[s]) * decay[s], 0.0) for s in chunks]
    qk = {s: _dot_nt(q[rows[s]], k_e[s]) * decay[s] for s in chunks}
    t_inv = dict(zip(chunks, _unit_lower_inverse(lm, l)))
    uc = {s: _dot(t_inv[s], vb_e[s]) for s in chunks}
    wc = {s: _dot(t_inv[s], kbg_e[s]) for s in chunks}
    g_last = {s: gc[rows[s]][l - 1:l] for s in chunks}
    k_dec = {s: (k[rows[s]] * jnp.exp(g_last[s] - gc[rows[s]])).T for s in chunks}
    head_blk = _same_head(HEAD_W)
    st = [st_ref[b] for b in range(bb)]
    ob = {}
    for j in range(nt):
        now = [(b, j) for b in range(bb)]
        o0 = {s: _dot(qg[rows[s]], st[s[0]]) for s in now}
        v_new = {s: uc[s] - _dot(wc[s], st[s[0]]) for s in now}
        for s in now:
            ob[s] = o0[s] + _dot(qk[s], _expand(v_new[s], hm))
        for s in now:
            st[s[0]] = st[s[0]] * jnp.exp(g_last[s]) + jnp.where(head_blk, _dot(k_dec[s], v_new[s]), 0.0)
    for b in range(bb):
        st_ref[b] = st[b]
    o = jnp.concatenate([ob[(b, j)] for b in range(bb) for j in range(nt)], axis=0)
    ms = _dot2(o * o, ones_h) * (1.0 / HEAD_W)
    o = (o * lax.rsqrt(ms + NORM_EPS) * nw_ref[...] * _silu(gate)).astype(y_ref.dtype)
    for b in range(bb):
        y_ref[b] = o[b * tl:(b + 1) * tl]

    @pl.when(c == pl.num_programs(1) - 1)
    def _():
        sl_ref[...] = st_ref[...]


def _gdn_mixer(p3, conv0, s0, conv_w, a_log, dt_bias, norm_w, l):
    bsz, t, _ = p3.shape
    cv0 = jnp.pad(conv0, ((0, 0), (8 - (GD_CONV - 1), 0), (0, 0)))
    cw = jnp.pad(conv_w, ((0, 8 - GD_CONV), (0, 0)))
    per_head = lambda a: jnp.repeat(a, HEAD_W).reshape(1, BRANCH_W)
    args = (cv0, _block_diag_heads(s0), cw, per_head(a_log), per_head(dt_bias),
            jnp.tile(norm_w, HEADS).reshape(1, BRANCH_W))
    full = lambda a: pl.BlockSpec(a.shape, lambda b, c: (0,) * a.ndim)
    bb, nt = _streams_per_step(bsz, t, l)
    y, sl = pl.pallas_call(
        functools.partial(_gdn_kernel, l=l, bb=bb, nt=nt),
        grid=(bsz // bb, t // (nt * l)),
        in_specs=[pl.BlockSpec((bb, nt * l, 1024), lambda b, c: (b, c, COL_GD // 1024)),
                  pl.BlockSpec((bb, nt * l, LANES), lambda b, c: (b, c, COL_AB // LANES)),
                  pl.BlockSpec((bb, 8, GD_QKV), lambda b, c: (b, 0, 0)),
                  pl.BlockSpec((bb, BRANCH_W, BRANCH_W), lambda b, c: (b, 0, 0))]
                 + [full(a) for a in args[2:]],
        out_specs=[pl.BlockSpec((bb, nt * l, BRANCH_W), lambda b, c: (b, c, 0)),
                   pl.BlockSpec((bb, BRANCH_W, BRANCH_W), lambda b, c: (b, 0, 0))],
        out_shape=[jax.ShapeDtypeStruct((bsz, t, BRANCH_W), BF16),
                   jax.ShapeDtypeStruct((bsz, BRANCH_W, BRANCH_W), F32)],
        scratch_shapes=[pltpu.VMEM((bb, BRANCH_W, BRANCH_W), F32), pltpu.VMEM((bb, 8, GD_QKV), F32)],
        compiler_params=_cparams(("parallel", "arbitrary")),
    )(p3, p3, *args)
    return y, _diag_blocks(sl)


def _merge_kernel(ya_ref, yb_ref, yc_ref, yd_ref, x_ref, nw_ref, wg_ref, wbr_ref, wout_ref, o_ref):
    x = x_ref[...]
    ms = jnp.mean(x * x, axis=-1, keepdims=True)
    h = (x * lax.rsqrt(ms + NORM_EPS) * nw_ref[...]).astype(BF16)
    m = None
    for b, y_ref in enumerate((ya_ref, yb_ref, yc_ref, yd_ref)):
        gate = jnp.dot(h, wg_ref[:, b * D_MODEL:(b + 1) * D_MODEL], preferred_element_type=F32)
        br = jnp.dot(y_ref[...], wbr_ref[b], preferred_element_type=F32)
        term = (0.5 * jnp.tanh(0.5 * gate) + 0.5) * br
        m = term if m is None else m + term
    o_ref[...] = x + jnp.dot(m.astype(BF16), wout_ref[...], preferred_element_type=F32)


def _merge(ys, x2, norm_w, w_gate, w_branch, w_out, tm):
    n = x2.shape[0]
    yspec = pl.BlockSpec((tm, BRANCH_W), lambda i: (i, 0))
    once = pl.Buffered(1)
    return pl.pallas_call(
        _merge_kernel,
        grid=(n // tm,),
        in_specs=[yspec, yspec, yspec, yspec,
                  pl.BlockSpec((tm, D_MODEL), lambda i: (i, 0)),
                  pl.BlockSpec((1, D_MODEL), lambda i: (0, 0)),
                  pl.BlockSpec((D_MODEL, 4 * D_MODEL), lambda i: (0, 0), pipeline_mode=once),
                  pl.BlockSpec((4, BRANCH_W, D_MODEL), lambda i: (0, 0, 0), pipeline_mode=once),
                  pl.BlockSpec((D_MODEL, D_MODEL), lambda i: (0, 0), pipeline_mode=once)],
        out_specs=pl.BlockSpec((tm, D_MODEL), lambda i: (i, 0)),
        out_shape=jax.ShapeDtypeStruct((n, D_MODEL), F32),
        compiler_params=_cparams(("parallel",)),
    )(*[y.reshape(n, BRANCH_W) for y in ys], x2, norm_w.reshape(1, D_MODEL), w_gate,
      w_branch.astype(BF16), w_out.astype(BF16))


def _ffn_kernel(x_ref, nw_ref, w1_ref, w3_ref, w2_ref, o_ref, h_ref):
    j = pl.program_id(1)

    @pl.when(j == 0)
    def _():
        x = x_ref[...]
        ms = jnp.mean(x * x, axis=-1, keepdims=True)
        h_ref[...] = (x * lax.rsqrt(ms + NORM_EPS) * nw_ref[...]).astype(BF16)
        o_ref[...] = x

    h = h_ref[...]
    a = _silu(jnp.dot(h, w1_ref[...], preferred_element_type=F32)) * jnp.dot(h, w3_ref[...], preferred_element_type=F32)
    o_ref[...] += jnp.dot(a.astype(BF16), w2_ref[...], preferred_element_type=F32)


def _ffn(x2, norm_w, w1, w3, w2, tm, tf):
    n = x2.shape[0]
    dff = w1.shape[1]
    return pl.pallas_call(
        _ffn_kernel,
        grid=(n // tm, dff // tf),
        in_specs=[pl.BlockSpec((tm, D_MODEL), lambda i, j: (i, 0)),
                  pl.BlockSpec((1, D_MODEL), lambda i, j: (0, 0)),
                  pl.BlockSpec((D_MODEL, tf), lambda i, j: (0, j)),
                  pl.BlockSpec((D_MODEL, tf), lambda i, j: (0, j)),
                  pl.BlockSpec((tf, D_MODEL), lambda i, j: (j, 0))],
        out_specs=pl.BlockSpec((tm, D_MODEL), lambda i, j: (i, 0)),
        out_shape=jax.ShapeDtypeStruct((n, D_MODEL), F32),
        scratch_shapes=[pltpu.VMEM((tm, D_MODEL), BF16)],
        compiler_params=_cparams(("parallel", "arbitrary")),
    )(x2, norm_w.reshape(1, D_MODEL), w1.astype(BF16), w3.astype(BF16), w2.astype(BF16))


MOE_SUB = 256
MOE_TILE = 2048
MOE_PLACE_ROWS = 1024
MOE_FF = 1792
ROUTER_ROWS = 256


def _router_kernel(x_ref, nw_ref, wr_ref, br_ref, h_ref, rcol_ref, gcol_ref, rrow_ref, cnt_ref, seen_ref,
                   *, tm, k):
    x = x_ref[...]
    ms = jnp.mean(x * x, axis=-1, keepdims=True)
    h = x * lax.rsqrt(ms + NORM_EPS) * nw_ref[...]
    h_ref[...] = h.astype(BF16)
    lane = _iota((tm, LANES), 1)
    logits = jnp.where(lane < N_EXPERTS, _dot_hp(h, wr_ref[...]) + br_ref[...], -jnp.inf)
    m1 = jnp.max(logits, axis=1, keepdims=True)
    i1 = jnp.min(jnp.where(logits == m1, lane, LANES), axis=1, keepdims=True)
    rest = jnp.where(lane == i1, -jnp.inf, logits)
    m2 = jnp.max(rest, axis=1, keepdims=True)
    i2 = jnp.min(jnp.where(rest == m2, lane, LANES), axis=1, keepdims=True)
    e2 = jnp.exp(m2 - m1)
    g1 = 1.0 / (1.0 + e2)
    g2 = e2 / (1.0 + e2)
    sel = (lane == i1) | (lane == i2)
    self32 = sel.astype(F32)
    gcol_ref[...] = jnp.where(lane == i1, g1, 0.0) + jnp.where(lane == i2, g2, 0.0)
    @pl.when(pl.program_id(0) % k == 0)
    def _():
        seen_ref[...] = jnp.zeros_like(seen_ref)

    seen = seen_ref[...]
    tri = (_iota((tm, tm), 1) < _iota((tm, tm), 0)).astype(BF16)
    rank = jnp.dot(tri, self32.astype(BF16), preferred_element_type=F32) + seen
    r = jnp.where(sel, rank, -1.0)
    rcol_ref[...] = r
    rrow_ref[...] = r.T[:N_EXPERTS]
    seen = seen + jnp.sum(self32, axis=0, keepdims=True)
    seen_ref[...] = seen
    cnt_ref[...] = jnp.broadcast_to(seen, (8, LANES))


def _router(x2, norm_w, w_router, b_router, tm, k):
    n = x2.shape[0]
    nt = n // (tm * k)
    wr = jnp.pad(w_router, ((0, 0), (0, LANES - N_EXPERTS)))
    br = jnp.pad(b_router, (0, LANES - N_EXPERTS)).reshape(1, LANES)
    return pl.pallas_call(
        functools.partial(_router_kernel, tm=tm, k=k),
        grid=(n // tm,),
        in_specs=[pl.BlockSpec((tm, D_MODEL), lambda i: (i, 0)),
                  pl.BlockSpec((1, D_MODEL), lambda i: (0, 0)),
                  pl.BlockSpec((D_MODEL, LANES), lambda i: (0, 0)),
                  pl.BlockSpec((1, LANES), lambda i: (0, 0))],
        out_specs=[pl.BlockSpec((tm, D_MODEL), lambda i: (i, 0)),
                   pl.BlockSpec((tm, LANES), lambda i: (i, 0)),
                   pl.BlockSpec((tm, LANES), lambda i: (i, 0)),
                   pl.BlockSpec((None, N_EXPERTS, tm), lambda i: (i // k, 0, i % k)),
                   pl.BlockSpec((None, 8, LANES), lambda i: (i, 0, 0))],
        out_shape=[jax.ShapeDtypeStruct((n, D_MODEL), BF16),
                   jax.ShapeDtypeStruct((n, LANES), F32),
                   jax.ShapeDtypeStruct((n, LANES), F32),
                   jax.ShapeDtypeStruct((nt, N_EXPERTS, tm * k), F32),
                   jax.ShapeDtypeStruct((n // tm, 8, LANES), F32)],
        scratch_shapes=[pltpu.VMEM((1, LANES), F32)],
        compiler_params=_cparams(("arbitrary",)),
    )(x2, norm_w.reshape(1, D_MODEL), wr, br)


def _moe_kernel(cnt_ref, h_ref, rrow_ref, rcol_ref, gcol_ref, w1_ref, w3_ref, w2_ref, o_ref,
                xs_ref, acc_ref, *, tm):
    i, e, c = pl.program_id(0), pl.program_id(1), pl.program_id(2)
    n_sub = (cnt_ref[i * N_EXPERTS + e] + (MOE_SUB - 1)) // MOE_SUB

    @pl.when((e == 0) & (c == 0))
    def _():
        o_ref[...] = jnp.zeros_like(o_ref)

    @pl.when(c == 0)
    def _():
        rrow = rrow_ref[pl.ds(e, 1), :]
        slot = _iota((MOE_SUB, 1), 0).astype(F32)

        def gather(s, carry):
            base = pl.multiple_of(s * MOE_SUB, MOE_SUB)
            pick = (rrow == slot + (s * MOE_SUB).astype(F32)).astype(BF16)
            xs_ref[pl.ds(base, MOE_SUB), :] = jnp.dot(pick, h_ref[...], preferred_element_type=F32).astype(BF16)
            acc_ref[pl.ds(base, MOE_SUB), :] = jnp.zeros((MOE_SUB, D_MODEL), F32)
            return carry

        lax.fori_loop(0, n_sub, gather, 0)

    def expert_rows(base, m):
        xb = xs_ref[pl.ds(base, m), :]
        a = (_silu(jnp.dot(xb, w1_ref[...], preferred_element_type=F32))
             * jnp.dot(xb, w3_ref[...], preferred_element_type=F32))
        acc_ref[pl.ds(base, m), :] += jnp.dot(a.astype(BF16), w2_ref[...], preferred_element_type=F32)

    def expert_pair(p, carry):
        expert_rows(pl.multiple_of(p * (2 * MOE_SUB), 2 * MOE_SUB), 2 * MOE_SUB)
        return carry

    if tm >= 2 * MOE_SUB:
        lax.fori_loop(0, n_sub // 2, expert_pair, 0)

        @pl.when(n_sub % 2 == 1)
        def _():
            expert_rows(pl.multiple_of((n_sub - 1) * MOE_SUB, MOE_SUB), MOE_SUB)
    else:
        @pl.when(n_sub > 0)
        def _():
            expert_rows(0, MOE_SUB)

    @pl.when(c == pl.num_programs(2) - 1)
    def _():
        tp = min(tm, MOE_PLACE_ROWS)
        lane = _iota((tp, LANES), 1)
        slot = _iota((1, MOE_SUB), 1).astype(F32)

        def scatter(s, carry):
            base = pl.multiple_of(s * MOE_SUB, MOE_SUB)
            a = acc_ref[pl.ds(base, MOE_SUB), :]
            hi = a.astype(BF16)
            lo = (a - hi.astype(F32)).astype(BF16)
            for r0 in range(0, tm, tp):
                rcol = jnp.sum(jnp.where(lane == e, rcol_ref[r0:r0 + tp], 0.0), axis=1, keepdims=True)
                gate = jnp.sum(jnp.where(lane == e, gcol_ref[r0:r0 + tp], 0.0), axis=1, keepdims=True)
                place = (rcol == slot + (s * MOE_SUB).astype(F32)).astype(BF16)
                back = (jnp.dot(place, hi, preferred_element_type=F32)
                        + jnp.dot(place, lo, preferred_element_type=F32))
                o_ref[r0:r0 + tp] += gate * back
            return carry

        lax.fori_loop(0, n_sub, scatter, 0)


def _moe(h2, rrow, rcol, gcol, counts, w1, w3, w2, tm, tf):
    n = h2.shape[0]
    nt = n // tm
    dff = w1.shape[2]
    grid_spec = pltpu.PrefetchScalarGridSpec(
        num_scalar_prefetch=1,
        grid=(nt, N_EXPERTS, dff // tf),
        in_specs=[pl.BlockSpec((tm, D_MODEL), lambda i, e, c, cnt: (i, 0), pipeline_mode=pl.Buffered(1)),
                  pl.BlockSpec((None, N_EXPERTS, tm), lambda i, e, c, cnt: (i, 0, 0)),
                  pl.BlockSpec((tm, LANES), lambda i, e, c, cnt: (i, 0), pipeline_mode=pl.Buffered(1)),
                  pl.BlockSpec((tm, LANES), lambda i, e, c, cnt: (i, 0), pipeline_mode=pl.Buffered(1)),
                  pl.BlockSpec((None, D_MODEL, tf), lambda i, e, c, cnt: (e, 0, c)),
                  pl.BlockSpec((None, D_MODEL, tf), lambda i, e, c, cnt: (e, 0, c)),
                  pl.BlockSpec((None, tf, D_MODEL), lambda i, e, c, cnt: (e, c, 0))],
        out_specs=pl.BlockSpec((tm, D_MODEL), lambda i, e, c, cnt: (i, 0), pipeline_mode=pl.Buffered(1)),
        scratch_shapes=[pltpu.VMEM((tm, D_MODEL), BF16), pltpu.VMEM((tm, D_MODEL), F32)],
    )
    return pl.pallas_call(
        functools.partial(_moe_kernel, tm=tm),
        grid_spec=grid_spec,
        out_shape=jax.ShapeDtypeStruct((n, D_MODEL), F32),
        compiler_params=_cparams(("parallel", "arbitrary", "arbitrary")),
    )(counts, h2, rrow, rcol, gcol, w1, w3, w2)


def _moe_ffn(x2, norm_w, w_router, b_router, w1, w3, w2):
    n = x2.shape[0]
    tm = min(MOE_TILE, n)
    k = tm // ROUTER_ROWS
    h2, rcol, gcol, rrow, cnt = _router(x2, norm_w, w_router, b_router, ROUTER_ROWS, k)
    counts = cnt[k - 1::k, 0, :N_EXPERTS].astype(jnp.int32).reshape(-1)
    return _moe(h2, rrow, rcol, gcol, counts, w1.astype(BF16), w3.astype(BF16), w2.astype(BF16), tm, MOE_FF)


SC_CORES = 2
SC_SUBCORES = 16
SC_WINDOW = 128
SC_PIECE = 256
MOE_BLOCK = 512
HALF = D_MODEL // 2


HIGH16 = -65536


def _pack_bf16_pairs(h):
    bits = lax.bitcast_convert_type(h.astype(BF16).astype(F32), jnp.int32)
    return lax.shift_right_logical(bits[:, :HALF], 16) | (bits[:, HALF:] & HIGH16)


def _unpack_bf16_pairs(pieces):
    lo = [lax.bitcast_convert_type(lax.shift_left(w, 16), F32) for w in pieces]
    hi = [lax.bitcast_convert_type(w & HIGH16, F32) for w in pieces]
    return jnp.concatenate(lo + hi, axis=1).astype(BF16)


def _route_kernel(x_ref, nw_ref, wr_ref, br_ref, hp_ref, tok_ref, cnt_ref, seen_ref, *, tm):
    @pl.when(pl.program_id(0) == 0)
    def _():
        seen_ref[...] = jnp.zeros_like(seen_ref)

    x = x_ref[...]
    ms = jnp.mean(x * x, axis=-1, keepdims=True)
    h = x * lax.rsqrt(ms + NORM_EPS) * nw_ref[...]
    hp = _pack_bf16_pairs(h)
    for q in range(HALF // SC_PIECE):
        hp_ref[q] = hp[:, q * SC_PIECE:(q + 1) * SC_PIECE]
    lane = _iota((tm, LANES), 1)
    logits = jnp.where(lane < N_EXPERTS, _dot_hp(h, wr_ref[...]) + br_ref[...], -jnp.inf)
    m1 = jnp.max(logits, axis=1, keepdims=True)
    i1 = jnp.min(jnp.where(logits == m1, lane, LANES), axis=1, keepdims=True)
    rest = jnp.where(lane == i1, -jnp.inf, logits)
    m2 = jnp.max(rest, axis=1, keepdims=True)
    i2 = jnp.min(jnp.where(rest == m2, lane, LANES), axis=1, keepdims=True)
    e2 = jnp.exp(m2 - m1)
    g1 = 1.0 / (1.0 + e2)
    g2 = e2 / (1.0 + e2)
    self32 = ((lane == i1) | (lane == i2)).astype(F32)
    seen = seen_ref[...]
    tri = (_iota((tm, tm), 1) < _iota((tm, tm), 0)).astype(BF16)
    rank = jnp.dot(tri, self32.astype(BF16), preferred_element_type=F32) + seen
    r1 = jnp.sum(jnp.where(lane == i1, rank, 0.0), axis=1, keepdims=True)
    r2 = jnp.sum(jnp.where(lane == i2, rank, 0.0), axis=1, keepdims=True)
    cols = (i1.astype(F32), i2.astype(F32), r1, r2, g1, g2)
    tok = jnp.zeros((tm, LANES), F32)
    for j, col in enumerate(cols):
        tok = jnp.where(lane == j, col, tok)
    tok_ref[...] = tok
    seen = seen + jnp.sum(self32, axis=0, keepdims=True)
    seen_ref[...] = seen
    cnt_ref[...] = jnp.broadcast_to(seen, (8, LANES))


def _route(x2, norm_w, w_router, b_router, tm):
    n = x2.shape[0]
    wr = jnp.pad(w_router, ((0, 0), (0, LANES - N_EXPERTS)))
    br = jnp.pad(b_router, (0, LANES - N_EXPERTS)).reshape(1, LANES)
    return pl.pallas_call(
        functools.partial(_route_kernel, tm=tm),
        grid=(n // tm,),
        in_specs=[pl.BlockSpec((tm, D_MODEL), lambda i: (i, 0)),
                  pl.BlockSpec((1, D_MODEL), lambda i: (0, 0)),
                  pl.BlockSpec((D_MODEL, LANES), lambda i: (0, 0)),
                  pl.BlockSpec((1, LANES), lambda i: (0, 0))],
        out_specs=[pl.BlockSpec((HALF // SC_PIECE, tm, SC_PIECE), lambda i: (0, i, 0)),
                   pl.BlockSpec((tm, LANES), lambda i: (i, 0)),
                   pl.BlockSpec((8, LANES), lambda i: (0, 0))],
        out_shape=[jax.ShapeDtypeStruct((HALF // SC_PIECE, n, SC_PIECE), jnp.int32),
                   jax.ShapeDtypeStruct((n, LANES), F32),
                   jax.ShapeDtypeStruct((8, LANES), F32)],
        scratch_shapes=[pltpu.VMEM((1, LANES), F32)],
        compiler_params=_cparams(("arbitrary",)),
    )(x2, norm_w.reshape(1, D_MODEL), wr, br)


def _sc_rows_multiple(d):
    return SC_CORES * SC_SUBCORES * SC_WINDOW * SC_PIECE // d


def _sc_gather_rows(table, idx):
    f, v, _ = table.shape
    b = idx.shape[0]
    assert b % _sc_rows_multiple(f * SC_PIECE) == 0
    idx_all = (idx[None, :] + (jnp.arange(f, dtype=jnp.int32) * v)[:, None]).reshape(-1)
    return _sc_gather_pieces(table.reshape(f * v, SC_PIECE), idx_all).reshape(f, b, SC_PIECE)


def _sc_gather_pieces(table, idx):
    bp = idx.shape[0]
    d = table.shape[1]
    window = SC_WINDOW
    idx2 = idx.reshape(1, bp)
    mesh = plsc.VectorSubcoreMesh(core_axis_name="core", subcore_axis_name="subcore")

    @functools.partial(pl.kernel, out_type=jax.ShapeDtypeStruct((bp, d), table.dtype), mesh=mesh)
    def gather(x_hbm, i_hbm, o_hbm):
        def body(i_vmem, o_vmem):
            pltpu.sync_copy(x_hbm.at[i_vmem.at[0]], o_vmem)

        pltpu.emit_pipeline(
            body,
            grid=(bp // window,),
            in_specs=[pl.BlockSpec((1, window), index_map=lambda i: (0, i))],
            out_specs=[pl.BlockSpec((window, d), index_map=lambda i: (i, 0))],
            core_axis_name=("core", "subcore"),
            dimension_semantics=(pltpu.PARALLEL,),
        )(i_hbm, o_hbm)

    return gather(table, idx2)


def _experts_kernel(be_ref, nb_ref, xs_ref, w1_ref, w3_ref, w2_ref, o_ref):
    g, c = pl.program_id(0), pl.program_id(1)

    @pl.when(c == 0)
    def _():
        o_ref[...] = jnp.zeros_like(o_ref)

    @pl.when(g < nb_ref[0])
    def _():
        xb = _unpack_bf16_pairs([xs_ref[q] for q in range(HALF // SC_PIECE)])
        a = (_silu(jnp.dot(xb, w1_ref[...], preferred_element_type=F32))
             * jnp.dot(xb, w3_ref[...], preferred_element_type=F32))
        y = jnp.dot(a.astype(BF16), w2_ref[...], preferred_element_type=F32)
        for q in range(D_MODEL // SC_PIECE):
            o_ref[q] += y[:, q * SC_PIECE:(q + 1) * SC_PIECE]


def _experts(xs, block_expert, n_blocks, w1, w3, w2, blk, tf):
    n_slots = xs.shape[1]
    dff = w1.shape[2]
    grid_spec = pltpu.PrefetchScalarGridSpec(
        num_scalar_prefetch=2,
        grid=(n_slots // blk, dff // tf),
        in_specs=[pl.BlockSpec((HALF // SC_PIECE, blk, SC_PIECE), lambda g, c, be, nb: (0, g, 0)),
                  pl.BlockSpec((None, D_MODEL, tf), lambda g, c, be, nb: (be[g], 0, c)),
                  pl.BlockSpec((None, D_MODEL, tf), lambda g, c, be, nb: (be[g], 0, c)),
                  pl.BlockSpec((None, tf, D_MODEL), lambda g, c, be, nb: (be[g], c, 0))],
        out_specs=pl.BlockSpec((D_MODEL // SC_PIECE, blk, SC_PIECE), lambda g, c, be, nb: (0, g, 0)),
    )
    return pl.pallas_call(
        _experts_kernel,
        grid_spec=grid_spec,
        out_shape=jax.ShapeDtypeStruct((D_MODEL // SC_PIECE, n_slots, SC_PIECE), F32),
        compiler_params=_cparams(("parallel", "arbitrary")),
    )(block_expert, n_blocks, xs, w1, w3, w2)


def _combine_kernel(x_ref, y1_ref, y2_ref, tok_ref, nw_ref, o_ref):
    tok = tok_ref[...]
    rows = lambda y_ref: jnp.concatenate([y_ref[q] for q in range(D_MODEL // SC_PIECE)], axis=1)
    x = x_ref[...] + tok[:, 4:5] * rows(y1_ref) + tok[:, 5:6] * rows(y2_ref)
    ms = jnp.mean(x * x, axis=-1, keepdims=True)
    o_ref[...] = x * lax.rsqrt(ms + NORM_EPS) * nw_ref[...]


def _combine(x2, ys2, tok, norm_w, tm):
    n = x2.shape[0]
    spec = pl.BlockSpec((tm, D_MODEL), lambda i: (i, 0))
    yspec = lambda off: pl.BlockSpec((D_MODEL // SC_PIECE, tm, SC_PIECE), lambda i: (0, i + off, 0))
    return pl.pallas_call(
        _combine_kernel,
        grid=(n // tm,),
        in_specs=[spec, yspec(0), yspec(n // tm),
                  pl.BlockSpec((tm, LANES), lambda i: (i, 0)), pl.BlockSpec((1, D_MODEL), lambda i: (0, 0))],
        out_specs=spec,
        out_shape=jax.ShapeDtypeStruct((n, D_MODEL), F32),
        compiler_params=_cparams(("parallel",)),
    )(x2, ys2, ys2, tok, norm_w.reshape(1, D_MODEL))


def _moe_final(x2, norm_w, w_router, b_router, w1, w3, w2, final_norm_w):
    n = x2.shape[0]
    blk = min(MOE_BLOCK, n)
    hp, tok, cnt = _route(x2, norm_w, w_router, b_router, ROUTER_ROWS)
    counts = cnt[0, :N_EXPERTS].astype(jnp.int32)
    padded = (counts + blk - 1) // blk * blk
    ends = jnp.cumsum(padded)
    base = ends - padded
    round_up = lambda a, m: -(-a // m) * m
    n_slots = round_up((-(-2 * n // blk) + N_EXPERTS) * blk, _sc_rows_multiple(HALF))
    n_blocks = n_slots // blk
    i12 = tok[:, 0:2].astype(jnp.int32)
    pos = (base[i12] + tok[:, 2:4].astype(jnp.int32)).T.reshape(-1)
    rows = jnp.tile(jnp.arange(n, dtype=jnp.int32), 2)
    src = jnp.zeros((n_slots,), jnp.int32).at[pos].set(rows, unique_indices=True, mode='promise_in_bounds')
    block_expert = jnp.minimum(
        jnp.searchsorted(ends, jnp.arange(n_blocks, dtype=jnp.int32) * blk, side='right'), N_EXPERTS - 1
    ).astype(jnp.int32)
    xs = _sc_gather_rows(hp, src)
    ys = _experts(xs, block_expert, (ends[-1:] // blk).astype(jnp.int32),
                  w1.astype(BF16), w3.astype(BF16), w2.astype(BF16), blk, MOE_FF)
    pos = jnp.pad(pos, (0, round_up(2 * n, _sc_rows_multiple(D_MODEL)) - 2 * n))
    ys2 = _sc_gather_rows(ys, pos)
    return _combine(x2, ys2, tok, final_norm_w, min(1024, n))


def _final_kernel(x_ref, m_ref, nw_ref, o_ref):
    x = x_ref[...] + m_ref[...]
    ms = jnp.mean(x * x, axis=-1, keepdims=True)
    o_ref[...] = x * lax.rsqrt(ms + NORM_EPS) * nw_ref[...]


def _final(x2, m2, norm_w, tm):
    n = x2.shape[0]
    spec = pl.BlockSpec((tm, D_MODEL), lambda i: (i, 0))
    return pl.pallas_call(
        _final_kernel,
        grid=(n // tm,),
        in_specs=[spec, spec, pl.BlockSpec((1, D_MODEL), lambda i: (0, 0))],
        out_specs=spec,
        out_shape=jax.ShapeDtypeStruct((n, D_MODEL), F32),
        compiler_params=_cparams(("parallel",)),
    )(x2, m2, norm_w.reshape(1, D_MODEL))


def _permute_w_in(w):
    s5, rw, sg = w[:, 0:256], w[:, 256:1280], w[:, 1280:1792]
    gd, ab, gates = w[:, 1792:2816], w[:, 2816:2824], w[:, 2824:6920]
    pad = jnp.zeros((D_MODEL, PROJ_COLS - COL_AB - 8), w.dtype)
    return jnp.concatenate([rw, gd, sg, s5, ab, pad], axis=1).astype(BF16), gates.astype(BF16)


def _run_trunk(x, s5_h, rw_s, rw_shift, gd_s, gd_conv, p, w_in_perm):
    bsz, t, _ = x.shape
    n = bsz * t
    tm = min(1024, n)
    l = min(64, t)
    tc_s5 = min(256, t)
    new = ([], [], [], [], [], [])
    x2 = x.reshape(n, D_MODEL)
    moe_out = None
    for layer in range(2):
        g = lambda name: p[name][layer]
        w_mix, w_gate = w_in_perm[layer]
        proj = _norm_proj(x2, g('norm1_w'), w_mix, tm, 1024)
        p3 = proj.reshape(bsz, t, PROJ_COLS)
        y_a, s5_new = _s5_mixer(p3, s5_h[layer], g('s5_lam_re'), g('s5_lam_im'), g('s5_log_dt'), g('s5_b'),
                                g('s5_c'), g('s5_d'), g('s5_w_glu'), g('s5_b_glu'), tc_s5)
        y_b, rw_new = _rwkv_mixer(p3, rw_shift[layer], rw_s[layer], g('rw_mu'), g('rw_w0'), g('rw_w2'),
                                  g('rw_a0'), g('rw_a2'), g('rw_g2'), g('rw_k_k'), g('rw_k_a'), g('rw_r_k'),
                                  g('rw_ln_w'), g('rw_ln_b'), l)
        y_c, sg_v = _sgu_mixer(p3, g('sg_ln_w'), g('sg_ln_b'), g('sg_w_s'), g('sg_b_s'))
        y_d, gd_new = _gdn_mixer(p3, gd_conv[layer], gd_s[layer], g('gd_conv_w'), g('gd_a_log'),
                                 g('gd_dt_bias'), g('gd_norm_w'), l)
        shift_new = p3[:, t - 1, COL_RW:COL_RW + RW_COLS]
        conv_new = p3[:, t - (GD_CONV - 1):, COL_GD:COL_GD + GD_QKV]
        x2 = _merge((y_a, y_b, y_c, y_d), x2, g('norm1_w'), w_gate, g('w_branch'), g('w_out'), min(512, n))
        j = layer // 2
        if layer % 2 == 0:
            x2 = _ffn(x2, g('norm2_w'), p['ffn_w1'][j], p['ffn_w3'][j], p['ffn_w2'][j], min(512, n), 1408)
        else:
            y = _moe_final(x2, g('norm2_w'), p['moe_router'][j], p['moe_router_b'][j],
                           p['moe_w1'][j], p['moe_w3'][j], p['moe_w2'][j], p['final_norm_w'])
        for lst, s in zip(new, (s5_new, rw_new, shift_new, gd_new, conv_new, sg_v)):
            lst.append(s)
    return y.reshape(bsz, t, D_MODEL), [jnp.stack(lst) for lst in new]


def kernel(x_prompt, x_sample, state_s5, state_rwkv, state_rwkv_shift, state_gdn, state_gdn_conv, norm1_w, w_in, s5_lam_re, s5_lam_im, s5_log_dt, s5_b, s5_c, s5_d, s5_w_glu, s5_b_glu, rw_mu, rw_w0, rw_w2, rw_a0, rw_a2, rw_g2, rw_k_k, rw_k_a, rw_r_k, rw_ln_w, rw_ln_b, sg_ln_w, sg_ln_b, sg_w_s, sg_b_s, gd_conv_w, gd_a_log, gd_dt_bias, gd_norm_w, w_branch, w_out, norm2_w, ffn_w1, ffn_w3, ffn_w2, moe_router, moe_router_b, moe_w1, moe_w3, moe_w2, final_norm_w):
    p = {
        'norm1_w': norm1_w, 's5_lam_re': s5_lam_re, 's5_lam_im': s5_lam_im, 's5_log_dt': s5_log_dt,
        's5_b': s5_b, 's5_c': s5_c, 's5_d': s5_d, 's5_w_glu': s5_w_glu, 's5_b_glu': s5_b_glu,
        'rw_mu': rw_mu, 'rw_w0': rw_w0, 'rw_w2': rw_w2, 'rw_a0': rw_a0, 'rw_a2': rw_a2, 'rw_g2': rw_g2,
        'rw_k_k': rw_k_k, 'rw_k_a': rw_k_a, 'rw_r_k': rw_r_k, 'rw_ln_w': rw_ln_w, 'rw_ln_b': rw_ln_b,
        'sg_ln_w': sg_ln_w, 'sg_ln_b': sg_ln_b, 'sg_w_s': sg_w_s, 'sg_b_s': sg_b_s,
        'gd_conv_w': gd_conv_w, 'gd_a_log': gd_a_log, 'gd_dt_bias': gd_dt_bias, 'gd_norm_w': gd_norm_w,
        'w_branch': w_branch, 'w_out': w_out, 'norm2_w': norm2_w,
        'ffn_w1': ffn_w1, 'ffn_w3': ffn_w3, 'ffn_w2': ffn_w2,
        'moe_router': moe_router, 'moe_router_b': moe_router_b, 'moe_w1': moe_w1, 'moe_w3': moe_w3, 'moe_w2': moe_w2,
        'final_norm_w': final_norm_w,
    }
    w_in_perm = [_permute_w_in(w_in[layer]) for layer in range(2)]
    bp, dt = x_prompt.shape[0], x_prompt.dtype
    depth = w_in.shape[0]
    y_prompt, (s5_p, rw_p, rwsh_p, gd_p, gdc_p, _) = _run_trunk(
        x_prompt,
        jnp.zeros((depth, bp, S5_GROUPS, S5_STATE, 2), dt),
        jnp.zeros((depth, bp, HEADS, HEAD_W, HEAD_W), dt),
        jnp.zeros((depth, bp, RW_COLS), dt),
        jnp.zeros((depth, bp, HEADS, HEAD_W, HEAD_W), dt),
        jnp.zeros((depth, bp, GD_CONV - 1, GD_QKV), dt),
        p, w_in_perm)
    y_sample, (s5_s, rw_s, rwsh_s, gd_s, gdc_s, sgv_s) = _run_trunk(
        x_sample, state_s5, state_rwkv, state_rwkv_shift, state_gdn, state_gdn_conv, p, w_in_perm)
    return (y_prompt, y_sample, s5_p, rw_p, rwsh_p, gd_p, gdc_p, s5_s, rw_s, rwsh_s, gd_s, gdc_s, sgv_s)
```

```python
import functools

import jax
import jax.numpy as jnp
from jax import lax
from jax.experimental import pallas as pl
from jax.experimental.pallas import tpu as pltpu
from jax.experimental.pallas import tpu_sc as plsc

F32 = jnp.float32
BF16 = jnp.bfloat16

D_MODEL = 1024
BRANCH_W = 256
HEADS = 4
HEAD_W = 64
S5_GROUPS = 16
S5_GROUP = 16
S5_STATE = 64
S5_W = S5_GROUPS * S5_STATE
S5_ROWS = 8
SG_CHUNK = 128
SG_ROWS = 512
GD_CONV = 4
GD_QKV = 3 * BRANCH_W
RW_COLS = 1024
RW_EPS = 64e-5
NORM_EPS = 1e-6
N_EXPERTS = 8
LANES = 128

COL_RW = 0
COL_GD = 1024
COL_SG = 2048
COL_S5 = 2560
COL_AB = 2816
PROJ_COLS = 3072

VMEM_LIMIT = 48 * 1024 * 1024


def _cparams(sem):
    return pltpu.CompilerParams(dimension_semantics=sem, vmem_limit_bytes=VMEM_LIMIT)


def _dot(a, b):
    return jnp.dot(a.astype(BF16), b.astype(BF16), preferred_element_type=F32)


def _dot_nt(a, b):
    return lax.dot_general(a.astype(BF16), b.astype(BF16), (((1,), (1,)), ((), ())),
                           preferred_element_type=F32)


def _split3(a):
    hi = a.astype(BF16)
    r1 = a - hi.astype(F32)
    mid = r1.astype(BF16)
    lo = (r1 - mid.astype(F32)).astype(BF16)
    return hi, mid, lo


def _dot3_left(b_exact, a):
    hi, mid, lo = _split3(a)
    b = b_exact.astype(BF16)
    return (jnp.dot(b, hi, preferred_element_type=F32) + jnp.dot(b, mid, preferred_element_type=F32)
            + jnp.dot(b, lo, preferred_element_type=F32))


def _dot_hp(a, b):
    a0, a1, _ = _split3(a)
    b0, b1, _ = _split3(b)
    d = lambda x, y: jnp.dot(x, y, preferred_element_type=F32)
    return d(a0, b0) + (d(a0, b1) + d(a1, b0))


def _iota(shape, axis):
    return lax.broadcasted_iota(jnp.int32, shape, axis)


def _head_ones():
    r = _iota((BRANCH_W, BRANCH_W), 0) // HEAD_W
    c = _iota((BRANCH_W, BRANCH_W), 1) // HEAD_W
    return (r == c).astype(BF16)


def _head_mask(l):
    r = _iota((HEADS * l, BRANCH_W), 0) // l
    c = _iota((HEADS * l, BRANCH_W), 1) // HEAD_W
    return r == c


def _expand(x, mask):
    return jnp.where(mask, jnp.concatenate([x] * HEADS, axis=0), 0.0)


def _tri_masks(l):
    i = _iota((l, HEADS * l), 0)
    j = _iota((l, HEADS * l), 1) % l
    return j < i, j <= i


def _same_head(l):
    n = HEADS * l
    return (_iota((n, n), 0) // l) == (_iota((n, n), 1) // l)


def _expand_sq(x, same):
    return jnp.where(same, jnp.concatenate([x] * HEADS, axis=0), 0.0)


def _unit_lower_inverse(a_strict, l):
    i = _iota((l, HEADS * l), 0)
    j = _iota((l, HEADS * l), 1) % l
    eye = (i == j).astype(F32)
    same = _same_head(l)
    p = [-a for a in a_strict]
    t = [eye + x for x in p]
    k = 2
    while k < l:
        p = [_dot(x, _expand_sq(x, same)) for x in p]
        t = [y + _dot(y, _expand_sq(x, same)) for y, x in zip(t, p)]
        k *= 2
    return t


def _dot2(a, b_exact):
    hi = a.astype(BF16)
    lo = (a - hi.astype(F32)).astype(BF16)
    b = b_exact.astype(BF16)
    return jnp.dot(hi, b, preferred_element_type=F32) + jnp.dot(lo, b, preferred_element_type=F32)


def _cumsum_rows(x, l, nb=1):
    n = nb * l
    i, j = _iota((n, n), 0), _iota((n, n), 1)
    tri = ((j <= i) & ((i // l) == (j // l))).astype(BF16)
    return _dot3_left(tri, x)


def _softplus(x):
    return jnp.maximum(x, 0.0) + jnp.log(1.0 + jnp.exp(-jnp.abs(x)))


def _silu(x):
    return x * jax.nn.sigmoid(x)


def _norm_proj_kernel(x_ref, nw_ref, w_ref, o_ref, h_ref):
    @pl.when(pl.program_id(1) == 0)
    def _():
        x = x_ref[...]
        ms = jnp.mean(x * x, axis=-1, keepdims=True)
        h_ref[...] = (x * lax.rsqrt(ms + NORM_EPS) * nw_ref[...]).astype(BF16)

    o_ref[...] = jnp.dot(h_ref[...], w_ref[...], preferred_element_type=F32)


def _norm_proj(x2, norm_w, w_bf16, tm, tn):
    n = x2.shape[0]
    ncol = w_bf16.shape[1]
    return pl.pallas_call(
        _norm_proj_kernel,
        grid=(n // tm, ncol // tn),
        in_specs=[pl.BlockSpec((tm, D_MODEL), lambda i, j: (i, 0)),
                  pl.BlockSpec((1, D_MODEL), lambda i, j: (0, 0)),
                  pl.BlockSpec((D_MODEL, tn), lambda i, j: (0, j))],
        out_specs=pl.BlockSpec((tm, tn), lambda i, j: (i, j)),
        out_shape=jax.ShapeDtypeStruct((n, ncol), F32),
        scratch_shapes=[pltpu.VMEM((tm, D_MODEL), BF16)],
        compiler_params=_cparams(("parallel", "arbitrary")),
    )(x2, norm_w.reshape(1, D_MODEL), w_bf16)


def _s5_kernel(u_ref, h0_ref, wb_ref, wc_ref, a2k_ref, apow_ref, d_ref, wg_ref, bg_ref,
               y_ref, hl_ref, hs_ref, hb_ref, *, tc):
    c = pl.program_id(1)

    @pl.when(c == 0)
    def _():
        hs_ref[...] = h0_ref[...]

    u = u_ref[...]
    x = _dot(u, wb_ref[...])
    xr, xi = x[:, :S5_W], x[:, S5_W:]
    ng = tc // S5_ROWS
    xr = xr.reshape(ng, S5_ROWS, S5_W)
    xi = xi.reshape(ng, S5_ROWS, S5_W)
    row = _iota((S5_ROWS, 1), 0)
    k, d = 0, 1
    while d < S5_ROWS:
        m = row >= d
        ar = jnp.where(m, a2k_ref[k:k + 1, :S5_W], 0.0)
        ai = jnp.where(m, a2k_ref[k:k + 1, S5_W:], 0.0)
        sr = pltpu.roll(xr, d, axis=1)
        si = pltpu.roll(xi, d, axis=1)
        xr, xi = xr + (ar * sr - ai * si), xi + (ar * si + ai * sr)
        k, d = k + 1, d * 2
    hb_ref[:, :S5_W] = xr.reshape(tc, S5_W)
    hb_ref[:, S5_W:] = xi.reshape(tc, S5_W)
    pr, pi_ = apow_ref[:, :S5_W], apow_ref[:, S5_W:]

    def group(gi, carry):
        cr, ci = carry
        rows = pl.ds(pl.multiple_of(gi * S5_ROWS, S5_ROWS), S5_ROWS)
        hr = hb_ref[rows, :S5_W] + (pr * cr - pi_ * ci)
        hi = hb_ref[rows, S5_W:] + (pr * ci + pi_ * cr)
        hb_ref[rows, :S5_W] = hr
        hb_ref[rows, S5_W:] = hi
        return hr[S5_ROWS - 1:S5_ROWS], hi[S5_ROWS - 1:S5_ROWS]

    cr, ci = lax.fori_loop(0, tc // S5_ROWS, group, (hs_ref[:, :S5_W], hs_ref[:, S5_W:]), unroll=4)
    hs_ref[:, :S5_W] = cr
    hs_ref[:, S5_W:] = ci
    y = _dot(hb_ref[:, :S5_W], wc_ref[:S5_W]) + _dot(hb_ref[:, S5_W:], wc_ref[S5_W:])
    y = jax.nn.gelu(y + d_ref[...] * u)
    y = y * jax.nn.sigmoid(_dot(y, wg_ref[...]) + bg_ref[...])
    y_ref[...] = y.astype(y_ref.dtype)

    @pl.when(c == pl.num_programs(1) - 1)
    def _():
        hl_ref[...] = hs_ref[...]


def _s5_tables(lam_re, lam_im, log_dt, b_c, c_c, tc):
    dt = jnp.exp(log_dt)[:, None]
    mag = jnp.exp(lam_re * dt)
    ab_re, ab_im = mag * jnp.cos(lam_im * dt), mag * jnp.sin(lam_im * dt)
    den = lam_re * lam_re + lam_im * lam_im
    nr = ab_re - 1.0
    cf_re = (nr * lam_re + ab_im * lam_im) / den
    cf_im = (ab_im * lam_re - nr * lam_im) / den
    br, bi = b_c[..., 0], b_c[..., 1]
    bb_re = cf_re[..., None] * br - cf_im[..., None] * bi
    bb_im = cf_re[..., None] * bi + cf_im[..., None] * br
    eye = jnp.eye(S5_GROUPS, dtype=F32)
    bd_in = lambda m: jnp.einsum('gph,gk->ghkp', m, eye).reshape(BRANCH_W, S5_W)
    wb = jnp.concatenate([bd_in(bb_re), bd_in(bb_im)], axis=1)
    cr, ci = c_c[..., 0], c_c[..., 1]
    bd_out = lambda m: jnp.einsum('ghp,gk->gpkh', m, eye).reshape(S5_W, BRANCH_W)
    wc = jnp.concatenate([bd_out(cr), -bd_out(ci)], axis=0)
    pr, pi_ = ab_re.reshape(1, S5_W), ab_im.reshape(1, S5_W)
    lv_r, lv_i = [], []
    tr, ti = pr, pi_
    d = 1
    while d < tc:
        lv_r.append(pr)
        lv_i.append(pi_)
        tr, ti = (jnp.concatenate([tr, tr * pr - ti * pi_], axis=0),
                  jnp.concatenate([ti, tr * pi_ + ti * pr], axis=0))
        pr, pi_ = pr * pr - pi_ * pi_, 2.0 * pr * pi_
        d *= 2
    n_lv = len(lv_r)
    pad = (-n_lv) % 8
    a2k = jnp.concatenate([jnp.concatenate(lv_r, axis=0), jnp.concatenate(lv_i, axis=0)], axis=1)
    a2k = jnp.pad(a2k, ((0, pad), (0, 0)))
    apow = jnp.concatenate([tr, ti], axis=1)
    return wb.astype(BF16), wc.astype(BF16), a2k, apow


def _s5_mixer(p3, h0, lam_re, lam_im, log_dt, b_c, c_c, d_skip, w_glu, b_glu, tc):
    bsz, t, _ = p3.shape
    wb, wc, a2k, apow = _s5_tables(lam_re, lam_im, log_dt, b_c, c_c, S5_ROWS)
    h0f = jnp.concatenate([h0[..., 0].reshape(bsz, 1, S5_W), h0[..., 1].reshape(bsz, 1, S5_W)], axis=-1)
    full = lambda a: pl.BlockSpec(a.shape, lambda b, c: (0,) * a.ndim)
    d2, bg2, wg = d_skip.reshape(1, BRANCH_W), b_glu.reshape(1, BRANCH_W), w_glu.astype(BF16)
    y, hl = pl.pallas_call(
        functools.partial(_s5_kernel, tc=tc),
        grid=(bsz, t // tc),
        in_specs=[pl.BlockSpec((None, tc, BRANCH_W), lambda b, c: (b, c, COL_S5 // BRANCH_W)),
                  pl.BlockSpec((None, 1, 2 * S5_W), lambda b, c: (b, 0, 0)),
                  full(wb), full(wc), full(a2k), full(apow), full(d2), full(wg), full(bg2)],
        out_specs=[pl.BlockSpec((None, tc, BRANCH_W), lambda b, c: (b, c, 0)),
                   pl.BlockSpec((None, 1, 2 * S5_W), lambda b, c: (b, 0, 0))],
        out_shape=[jax.ShapeDtypeStruct((bsz, t, BRANCH_W), BF16),
                   jax.ShapeDtypeStruct((bsz, 1, 2 * S5_W), F32)],
        scratch_shapes=[pltpu.VMEM((1, 2 * S5_W), F32), pltpu.VMEM((tc, 2 * S5_W), F32)],
        compiler_params=_cparams(("parallel", "arbitrary")),
    )(p3, h0f, wb, wc, a2k, apow, d2, wg, bg2)
    h_last = jnp.stack([hl[:, 0, :S5_W].reshape(bsz, S5_GROUPS, S5_STATE),
                        hl[:, 0, S5_W:].reshape(bsz, S5_GROUPS, S5_STATE)], axis=-1)
    return y, h_last


def _rwkv_kernel(z_ref, sh0_ref, s0_ref, mu_ref, w0_ref, w2_ref, a0_ref, a2_ref, g2_ref,
                 kk_ref, ka_ref, rk_ref, lnw_ref, lnb_ref,
                 y_ref, sl_ref, st_ref, zp_ref, *, l, bb, nt):
    c = pl.program_id(1)

    @pl.when(c == 0)
    def _():
        st_ref[...] = s0_ref[...]
        zp_ref[...] = sh0_ref[...]

    tl = nt * l
    row = _iota((tl, 1), 0)
    zms = []
    for b in range(bb):
        z = z_ref[b]
        prev = jnp.where(row == 0, zp_ref[b], pltpu.roll(z, 1, axis=0))
        zp_ref[b] = z[tl - 1:tl]
        zms.append(z + (prev - z) * mu_ref[...])
    zm = jnp.concatenate(zms, axis=0)
    r, k, v = zm[:, 0:256], zm[:, 256:512], zm[:, 512:768]
    lo = zm[:, 768:896]
    g_lo = zm[:, 896:1024]
    w_log = -_softplus(-(w0_ref[...] + _dot(jnp.tanh(lo), w2_ref[...]))) - 0.5
    lw = -jnp.exp(w_log)
    a = jax.nn.sigmoid(a0_ref[...] + _dot(lo, a2_ref[...]))
    g = _dot(jax.nn.sigmoid(g_lo), g2_ref[...])
    ones_h = _head_ones()
    kk = k * kk_ref[...]
    kk = kk * lax.rsqrt(_dot2(kk * kk, ones_h) + NORM_EPS)
    k = k * (1.0 + (a - 1.0) * ka_ref[...])
    kka = kk * a

    cum = _cumsum_rows(lw, l, bb * nt)
    p_incl = jnp.exp(cum)
    p_inv = jnp.exp(-cum)
    kt = kk * jnp.exp(cum - lw)
    rt = r * p_incl
    kh = k * p_inv
    ah = kka * p_inv

    hm = _head_mask(l)
    strict, incl = _tri_masks(l)
    chunks = [(b, j) for j in range(nt) for b in range(bb)]
    rows = {s: slice((s[0] * nt + s[1]) * l, (s[0] * nt + s[1] + 1) * l) for s in chunks}
    ex = lambda x: {s: _expand(x[rows[s]], hm) for s in chunks}
    kh_e, ah_e, v_e = ex(kh), ex(ah), ex(v)
    a_aa = [jnp.where(strict, _dot_nt(kt[rows[s]], ah_e[s]), 0.0) for s in chunks]
    a_ak = {s: jnp.where(strict, _dot_nt(kt[rows[s]], kh_e[s]), 0.0) for s in chunks}
    b_ra = {s: jnp.where(incl, _dot_nt(rt[rows[s]], ah_e[s]), 0.0) for s in chunks}
    b_rk = {s: jnp.where(incl, _dot_nt(rt[rows[s]], kh_e[s]), 0.0) for s in chunks}
    t_inv = dict(zip(chunks, _unit_lower_inverse(a_aa, l)))
    av = {s: _dot(a_ak[s], v_e[s]) for s in chunks}
    bv = {s: _dot(b_rk[s], v_e[s]) for s in chunks}
    to_end = {s: jnp.exp(cum[rows[s]][l - 1:l] - cum[rows[s]]) for s in chunks}
    head_blk = _same_head(HEAD_W)
    st = [st_ref[b] for b in range(bb)]
    yb = {}
    for j in range(nt):
        now = [(b, j) for b in range(bb)]
        rhs = {s: _dot_nt(kt[rows[s]], st[s[0]]) + av[s] for s in now}
        ys = {s: _dot_nt(rt[rows[s]], st[s[0]]) + bv[s] for s in now}
        u = {s: _dot(t_inv[s], _expand(rhs[s], hm)) for s in now}
        for s in now:
            yb[s] = ys[s] - _dot(b_ra[s], _expand(u[s], hm))
        lhs_t = {s: jnp.concatenate([v[rows[s]], -u[s]], axis=0).T for s in now}
        rhs_k = {s: jnp.concatenate([k[rows[s]] * to_end[s], kka[rows[s]] * to_end[s]], axis=0) for s in now}
        for s in now:
            p_last = p_incl[rows[s]][l - 1:l]
            st[s[0]] = st[s[0]] * p_last + jnp.where(head_blk, _dot(lhs_t[s], rhs_k[s]), 0.0)
    for b in range(bb):
        st_ref[b] = st[b]
    y = jnp.concatenate([yb[(b, j)] for b in range(bb) for j in range(nt)], axis=0)

    inv_w = 1.0 / HEAD_W
    mean = _dot2(y, ones_h) * inv_w
    yc = y - mean
    var = _dot2(yc * yc, ones_h) * inv_w
    y = yc * lax.rsqrt(var + RW_EPS) * lnw_ref[...] + lnb_ref[...]
    bonus = _dot2(r * k * rk_ref[...], ones_h) * v
    y = ((y + bonus) * g).astype(y_ref.dtype)
    for b in range(bb):
        y_ref[b] = y[b * tl:(b + 1) * tl]

    @pl.when(c == pl.num_programs(1) - 1)
    def _():
        sl_ref[...] = st_ref[...]


STACK_STREAMS = 4
STACK_ROWS = 512


def _streams_per_step(bsz, t, l):
    bb = min(bsz, STACK_STREAMS)
    while bsz % bb:
        bb -= 1
    nt = max(1, min(t, STACK_ROWS // bb) // l)
    while (t // l) % nt:
        nt -= 1
    return bb, nt


def _block_diag_heads(s):
    bsz = s.shape[0]
    eye = jnp.eye(HEADS, dtype=s.dtype)
    return jnp.einsum('bhij,hg->bhigj', s, eye).reshape(bsz, BRANCH_W, BRANCH_W)


def _diag_blocks(s):
    bsz = s.shape[0]
    s5 = s.reshape(bsz, HEADS, HEAD_W, HEADS, HEAD_W)
    return jnp.stack([s5[:, h, :, h, :] for h in range(HEADS)], axis=1)


def _rwkv_mixer(p3, shift0, s0, mu, w0, w2, a0, a2, g2, k_k, k_a, r_k, ln_w, ln_b, l):
    bsz, t, _ = p3.shape
    row = lambda a: a.reshape(1, -1)
    w2p = jnp.concatenate([w2, jnp.zeros_like(w2)], axis=0).astype(BF16)
    a2p = jnp.concatenate([jnp.zeros_like(a2), a2], axis=0).astype(BF16)
    args = (shift0.reshape(bsz, 1, RW_COLS), _block_diag_heads(s0), row(mu), row(w0), w2p, row(a0), a2p,
            g2.astype(BF16), row(k_k), row(k_a), row(r_k), row(ln_w), row(ln_b))
    full = lambda a: pl.BlockSpec(a.shape, lambda b, c: (0,) * a.ndim)
    bb, nt = _streams_per_step(bsz, t, l)
    y, sl = pl.pallas_call(
        functools.partial(_rwkv_kernel, l=l, bb=bb, nt=nt),
        grid=(bsz // bb, t // (nt * l)),
        in_specs=[pl.BlockSpec((bb, nt * l, RW_COLS), lambda b, c: (b, c, COL_RW // RW_COLS)),
                  pl.BlockSpec((bb, 1, RW_COLS), lambda b, c: (b, 0, 0)),
                  pl.BlockSpec((bb, BRANCH_W, BRANCH_W), lambda b, c: (b, 0, 0))]
                 + [full(a) for a in args[2:]],
        out_specs=[pl.BlockSpec((bb, nt * l, BRANCH_W), lambda b, c: (b, c, 0)),
                   pl.BlockSpec((bb, BRANCH_W, BRANCH_W), lambda b, c: (b, 0, 0))],
        out_shape=[jax.ShapeDtypeStruct((bsz, t, BRANCH_W), BF16),
                   jax.ShapeDtypeStruct((bsz, BRANCH_W, BRANCH_W), F32)],
        scratch_shapes=[pltpu.VMEM((bb, BRANCH_W, BRANCH_W), F32), pltpu.VMEM((bb, 1, RW_COLS), F32)],
        compiler_params=_cparams(("parallel", "arbitrary")),
    )(p3, *args)
    return y, _diag_blocks(sl)


def _sgu_kernel(z_ref, lnw_ref, lnb_ref, wm_ref, bias_ref, o_ref, v_ref, *, l, nc):
    zg = jax.nn.gelu(z_ref[...])
    u, v = zg[:, :BRANCH_W], zg[:, BRANCH_W:]
    mean = jnp.mean(v, axis=-1, keepdims=True)
    vc = v - mean
    var = jnp.mean(vc * vc, axis=-1, keepdims=True)
    v = vc * lax.rsqrt(var + NORM_EPS) * lnw_ref[...] + lnb_ref[...]
    v_ref[...] = v
    hm = _head_mask(l)
    wm = wm_ref[...]
    for i in range(nc):
        rows = slice(i * l, (i + 1) * l)
        mixed = bias_ref[...] + _dot(wm, _expand(v[rows], hm))
        o_ref[rows, :] = (u[rows] * mixed).astype(o_ref.dtype)


def _sgu_mixer(p3, ln_w, ln_b, w_s, b_s):
    bsz, t, _ = p3.shape
    l = min(SG_CHUNK, t)
    nc = max(1, min(SG_ROWS, t) // l)
    tril = jnp.tril(jnp.ones((l, l), F32))
    wm = jnp.transpose(w_s[:, :l, :l] * tril, (1, 0, 2)).reshape(l, HEADS * l).astype(BF16)
    bias = jnp.repeat(jnp.transpose(b_s[:, :l]), HEAD_W, axis=1)
    row = lambda a: a.reshape(1, -1)
    full = lambda a: pl.BlockSpec(a.shape, lambda b, c: (0,) * a.ndim)
    args = (row(ln_w), row(ln_b), wm, bias)
    return pl.pallas_call(
        functools.partial(_sgu_kernel, l=l, nc=nc),
        grid=(bsz, t // (nc * l)),
        in_specs=[pl.BlockSpec((None, nc * l, 2 * BRANCH_W), lambda b, c: (b, c, COL_SG // (2 * BRANCH_W)))]
                 + [full(a) for a in args],
        out_specs=[pl.BlockSpec((None, nc * l, BRANCH_W), lambda b, c: (b, c, 0)),
                   pl.BlockSpec((None, nc * l, BRANCH_W), lambda b, c: (b, c, 0))],
        out_shape=[jax.ShapeDtypeStruct((bsz, t, BRANCH_W), BF16),
                   jax.ShapeDtypeStruct((bsz, t, BRANCH_W), F32)],
        compiler_params=_cparams(("parallel", "parallel")),
    )(p3, *args)


def _gdn_kernel(z_ref, ab_ref, cv0_ref, s0_ref, cw_ref, alog_ref, dtb_ref, nw_ref,
                y_ref, sl_ref, st_ref, cv_ref, *, l, bb, nt):
    c = pl.program_id(1)

    @pl.when(c == 0)
    def _():
        st_ref[...] = s0_ref[...]
        cv_ref[...] = cv0_ref[...]

    tl = nt * l
    row8 = _iota((8, 1), 0)
    convs, gates = [], []
    for b in range(bb):
        z = z_ref[b]
        qkv = z[:, :GD_QKV]
        gates.append(z[:, GD_QKV:])
        carry = cv_ref[b]
        cv_ref[b] = qkv[tl - 8:tl]
        conv = qkv * cw_ref[GD_CONV - 1:GD_CONV]
        for j in range(1, GD_CONV):
            sh = pltpu.roll(qkv, j, axis=0)
            top = jnp.where(row8 < j, pltpu.roll(carry, j, axis=0), sh[:8])
            sh = jnp.concatenate([top, sh[8:]], axis=0) if tl > 8 else top
            conv = conv + sh * cw_ref[GD_CONV - 1 - j:GD_CONV - j]
        convs.append(conv)
    conv = _silu(jnp.concatenate(convs, axis=0))
    gate = jnp.concatenate(gates, axis=0)
    q, k, v = conv[:, :256], conv[:, 256:512], conv[:, 512:768]
    ones_h = _head_ones()
    q = q * lax.rsqrt(_dot2(q * q, ones_h) + NORM_EPS) * (HEAD_W ** -0.5)
    k = k * lax.rsqrt(_dot2(k * k, ones_h) + NORM_EPS)
    ab = jnp.concatenate([ab_ref[b] for b in range(bb)], axis=0)
    lane_h = _iota((bb * tl, BRANCH_W), 1) // HEAD_W
    a_in = jnp.zeros((bb * tl, BRANCH_W), F32)
    b_in = jnp.zeros((bb * tl, BRANCH_W), F32)
    for h in range(HEADS):
        a_in = jnp.where(lane_h == h, ab[:, h:h + 1], a_in)
        b_in = jnp.where(lane_h == h, ab[:, HEADS + h:HEADS + h + 1], b_in)
    beta = jax.nn.sigmoid(b_in)
    g = -jnp.exp(alog_ref[...]) * _softplus(a_in + dtb_ref[...])
    gc = _cumsum_rows(g, l, bb * nt)
    eg = jnp.exp(gc)
    kb = k * beta
    vb = v * beta
    kbg = kb * eg
    qg = q * eg

    hm = _head_mask(l)
    strict, incl = _tri_masks(l)
    n = HEADS * l
    chunks = [(b, j) for j in range(nt) for b in range(bb)]
    rows = {s: slice((s[0] * nt + s[1]) * l, (s[0] * nt + s[1] + 1) * l) for s in chunks}
    ex = lambda x: {s: _expand(x[rows[s]], hm) for s in chunks}
    k_e, vb_e, kbg_e = ex(k), ex(vb), ex(kbg)
    lane_hd = _iota((l, n), 1) // l
    eye = _iota((l, n), 0) == (_iota((l, n), 1) % l)
    decay = {}
    for s in chunks:
        gi = jnp.zeros((l, n), F32)
        for h in range(HEADS):
            gi = jnp.where(lane_hd == h, gc[rows[s]][:, h * HEAD_W:h * HEAD_W + 1], gi)
        gj = jnp.sum(jnp.where(eye, gi, 0.0), axis=0, keepdims=True)
        decay[s] = jnp.where(incl, jnp.exp(jnp.where(incl, gi - gj, 0.0)), 0.0)
    lm = [jnp.where(strict, _dot_nt(kb[rows[s]], k_e[s]) * decay[s], 0.0) for s in chunks]
    qk = {s: _dot_nt(q[rows[s]], k_e[s]) * decay[s] for s in chunks}
    t_inv = dict(zip(chunks, _unit_lower_inverse(lm, l)))
    uc = {s: _dot(t_inv[s], vb_e[s]) for s in chunks}
    wc = {s: _dot(t_inv[s], kbg_e[s]) for s in chunks}
    g_last = {s: gc[rows[s]][l - 1:l] for s in chunks}
    k_dec = {s: (k[rows[s]] * jnp.exp(g_last[s] - gc[rows[s]])).T for s in chunks}
    head_blk = _same_head(HEAD_W)
    st = [st_ref[b] for b in range(bb)]
    ob = {}
    for j in range(nt):
        now = [(b, j) for b in range(bb)]
        o0 = {s: _dot(qg[rows[s]], st[s[0]]) for s in now}
        v_new = {s: uc[s] - _dot(wc[s], st[s[0]]) for s in now}
        for s in now:
            ob[s] = o0[s] + _dot(qk[s], _expand(v_new[s], hm))
        for s in now:
            st[s[0]] = st[s[0]] * jnp.exp(g_last[s]) + jnp.where(head_blk, _dot(k_dec[s], v_new[s]), 0.0)
    for b in range(bb):
        st_ref[b] = st[b]
    o = jnp.concatenate([ob[(b, j)] for b in range(bb) for j in range(nt)], axis=0)
    ms = _dot2(o * o, ones_h) * (1.0 / HEAD_W)
    o = (o * lax.rsqrt(ms + NORM_EPS) * nw_ref[...] * _silu(gate)).astype(y_ref.dtype)
    for b in range(bb):
        y_ref[b] = o[b * tl:(b + 1) * tl]

    @pl.when(c == pl.num_programs(1) - 1)
    def _():
        sl_ref[...] = st_ref[...]


def _gdn_mixer(p3, conv0, s0, conv_w, a_log, dt_bias, norm_w, l):
    bsz, t, _ = p3.shape
    cv0 = jnp.pad(conv0, ((0, 0), (8 - (GD_CONV - 1), 0), (0, 0)))
    cw = jnp.pad(conv_w, ((0, 8 - GD_CONV), (0, 0)))
    per_head = lambda a: jnp.repeat(a, HEAD_W).reshape(1, BRANCH_W)
    args = (cv0, _block_diag_heads(s0), cw, per_head(a_log), per_head(dt_bias),
            jnp.tile(norm_w, HEADS).reshape(1, BRANCH_W))
    full = lambda a: pl.BlockSpec(a.shape, lambda b, c: (0,) * a.ndim)
    bb, nt = _streams_per_step(bsz, t, l)
    y, sl = pl.pallas_call(
        functools.partial(_gdn_kernel, l=l, bb=bb, nt=nt),
        grid=(bsz // bb, t // (nt * l)),
        in_specs=[pl.BlockSpec((bb, nt * l, 1024), lambda b, c: (b, c, COL_GD // 1024)),
                  pl.BlockSpec((bb, nt * l, LANES), lambda b, c: (b, c, COL_AB // LANES)),
                  pl.BlockSpec((bb, 8, GD_QKV), lambda b, c: (b, 0, 0)),
                  pl.BlockSpec((bb, BRANCH_W, BRANCH_W), lambda b, c: (b, 0, 0))]
                 + [full(a) for a in args[2:]],
        out_specs=[pl.BlockSpec((bb, nt * l, BRANCH_W), lambda b, c: (b, c, 0)),
                   pl.BlockSpec((bb, BRANCH_W, BRANCH_W), lambda b, c: (b, 0, 0))],
        out_shape=[jax.ShapeDtypeStruct((bsz, t, BRANCH_W), BF16),
                   jax.ShapeDtypeStruct((bsz, BRANCH_W, BRANCH_W), F32)],
        scratch_shapes=[pltpu.VMEM((bb, BRANCH_W, BRANCH_W), F32), pltpu.VMEM((bb, 8, GD_QKV), F32)],
        compiler_params=_cparams(("parallel", "arbitrary")),
    )(p3, p3, *args)
    return y, _diag_blocks(sl)


def _merge_kernel(ya_ref, yb_ref, yc_ref, yd_ref, x_ref, nw_ref, wg_ref, wbr_ref, wout_ref, o_ref):
    x = x_ref[...]
    ms = jnp.mean(x * x, axis=-1, keepdims=True)
    h = (x * lax.rsqrt(ms + NORM_EPS) * nw_ref[...]).astype(BF16)
    m = None
    for b, y_ref in enumerate((ya_ref, yb_ref, yc_ref, yd_ref)):
        gate = jnp.dot(h, wg_ref[:, b * D_MODEL:(b + 1) * D_MODEL], preferred_element_type=F32)
        br = jnp.dot(y_ref[...], wbr_ref[b], preferred_element_type=F32)
        term = (0.5 * jnp.tanh(0.5 * gate) + 0.5) * br
        m = term if m is None else m + term
    o_ref[...] = x + jnp.dot(m.astype(BF16), wout_ref[...], preferred_element_type=F32)


def _merge(ys, x2, norm_w, w_gate, w_branch, w_out, tm):
    n = x2.shape[0]
    yspec = pl.BlockSpec((tm, BRANCH_W), lambda i: (i, 0))
    once = pl.Buffered(1)
    return pl.pallas_call(
        _merge_kernel,
        grid=(n // tm,),
        in_specs=[yspec, yspec, yspec, yspec,
                  pl.BlockSpec((tm, D_MODEL), lambda i: (i, 0)),
                  pl.BlockSpec((1, D_MODEL), lambda i: (0, 0)),
                  pl.BlockSpec((D_MODEL, 4 * D_MODEL), lambda i: (0, 0), pipeline_mode=once),
                  pl.BlockSpec((4, BRANCH_W, D_MODEL), lambda i: (0, 0, 0), pipeline_mode=once),
                  pl.BlockSpec((D_MODEL, D_MODEL), lambda i: (0, 0), pipeline_mode=once)],
        out_specs=pl.BlockSpec((tm, D_MODEL), lambda i: (i, 0)),
        out_shape=jax.ShapeDtypeStruct((n, D_MODEL), F32),
        compiler_params=_cparams(("parallel",)),
    )(*[y.reshape(n, BRANCH_W) for y in ys], x2, norm_w.reshape(1, D_MODEL), w_gate,
      w_branch.astype(BF16), w_out.astype(BF16))


def _ffn_kernel(x_ref, nw_ref, w1_ref, w3_ref, w2_ref, o_ref, h_ref):
    j = pl.program_id(1)

    @pl.when(j == 0)
    def _():
        x = x_ref[...]
        ms = jnp.mean(x * x, axis=-1, keepdims=True)
        h_ref[...] = (x * lax.rsqrt(ms + NORM_EPS) * nw_ref[...]).astype(BF16)
        o_ref[...] = x

    h = h_ref[...]
    a = _silu(jnp.dot(h, w1_ref[...], preferred_element_type=F32)) * jnp.dot(h, w3_ref[...], preferred_element_type=F32)
    o_ref[...] += jnp.dot(a.astype(BF16), w2_ref[...], preferred_element_type=F32)


def _ffn(x2, norm_w, w1, w3, w2, tm, tf):
    n = x2.shape[0]
    dff = w1.shape[1]
    return pl.pallas_call(
        _ffn_kernel,
        grid=(n // tm, dff // tf),
        in_specs=[pl.BlockSpec((tm, D_MODEL), lambda i, j: (i, 0)),
                  pl.BlockSpec((1, D_MODEL), lambda i, j: (0, 0)),
                  pl.BlockSpec((D_MODEL, tf), lambda i, j: (0, j)),
                  pl.BlockSpec((D_MODEL, tf), lambda i, j: (0, j)),
                  pl.BlockSpec((tf, D_MODEL), lambda i, j: (j, 0))],
        out_specs=pl.BlockSpec((tm, D_MODEL), lambda i, j: (i, 0)),
        out_shape=jax.ShapeDtypeStruct((n, D_MODEL), F32),
        scratch_shapes=[pltpu.VMEM((tm, D_MODEL), BF16)],
        compiler_params=_cparams(("parallel", "arbitrary")),
    )(x2, norm_w.reshape(1, D_MODEL), w1.astype(BF16), w3.astype(BF16), w2.astype(BF16))


MOE_SUB = 256
MOE_TILE = 2048
MOE_PLACE_ROWS = 1024
MOE_FF = 1792
ROUTER_ROWS = 256


def _router_kernel(x_ref, nw_ref, wr_ref, br_ref, h_ref, rcol_ref, gcol_ref, rrow_ref, cnt_ref, seen_ref,
                   *, tm, k):
    x = x_ref[...]
    ms = jnp.mean(x * x, axis=-1, keepdims=True)
    h = x * lax.rsqrt(ms + NORM_EPS) * nw_ref[...]
    h_ref[...] = h.astype(BF16)
    lane = _iota((tm, LANES), 1)
    logits = jnp.where(lane < N_EXPERTS, _dot_hp(h, wr_ref[...]) + br_ref[...], -jnp.inf)
    m1 = jnp.max(logits, axis=1, keepdims=True)
    i1 = jnp.min(jnp.where(logits == m1, lane, LANES), axis=1, keepdims=True)
    rest = jnp.where(lane == i1, -jnp.inf, logits)
    m2 = jnp.max(rest, axis=1, keepdims=True)
    i2 = jnp.min(jnp.where(rest == m2, lane, LANES), axis=1, keepdims=True)
    e2 = jnp.exp(m2 - m1)
    g1 = 1.0 / (1.0 + e2)
    g2 = e2 / (1.0 + e2)
    sel = (lane == i1) | (lane == i2)
    self32 = sel.astype(F32)
    gcol_ref[...] = jnp.where(lane == i1, g1, 0.0) + jnp.where(lane == i2, g2, 0.0)
    @pl.when(pl.program_id(0) % k == 0)
    def _():
        seen_ref[...] = jnp.zeros_like(seen_ref)

    seen = seen_ref[...]
    tri = (_iota((tm, tm), 1) < _iota((tm, tm), 0)).astype(BF16)
    rank = jnp.dot(tri, self32.astype(BF16), preferred_element_type=F32) + seen
    r = jnp.where(sel, rank, -1.0)
    rcol_ref[...] = r
    rrow_ref[...] = r.T[:N_EXPERTS]
    seen = seen + jnp.sum(self32, axis=0, keepdims=True)
    seen_ref[...] = seen
    cnt_ref[...] = jnp.broadcast_to(seen, (8, LANES))


def _router(x2, norm_w, w_router, b_router, tm, k):
    n = x2.shape[0]
    nt = n // (tm * k)
    wr = jnp.pad(w_router, ((0, 0), (0, LANES - N_EXPERTS)))
    br = jnp.pad(b_router, (0, LANES - N_EXPERTS)).reshape(1, LANES)
    return pl.pallas_call(
        functools.partial(_router_kernel, tm=tm, k=k),
        grid=(n // tm,),
        in_specs=[pl.BlockSpec((tm, D_MODEL), lambda i: (i, 0)),
                  pl.BlockSpec((1, D_MODEL), lambda i: (0, 0)),
                  pl.BlockSpec((D_MODEL, LANES), lambda i: (0, 0)),
                  pl.BlockSpec((1, LANES), lambda i: (0, 0))],
        out_specs=[pl.BlockSpec((tm, D_MODEL), lambda i: (i, 0)),
                   pl.BlockSpec((tm, LANES), lambda i: (i, 0)),
                   pl.BlockSpec((tm, LANES), lambda i: (i, 0)),
                   pl.BlockSpec((None, N_EXPERTS, tm), lambda i: (i // k, 0, i % k)),
                   pl.BlockSpec((None, 8, LANES), lambda i: (i, 0, 0))],
        out_shape=[jax.ShapeDtypeStruct((n, D_MODEL), BF16),
                   jax.ShapeDtypeStruct((n, LANES), F32),
                   jax.ShapeDtypeStruct((n, LANES), F32),
                   jax.ShapeDtypeStruct((nt, N_EXPERTS, tm * k), F32),
                   jax.ShapeDtypeStruct((n // tm, 8, LANES), F32)],
        scratch_shapes=[pltpu.VMEM((1, LANES), F32)],
        compiler_params=_cparams(("arbitrary",)),
    )(x2, norm_w.reshape(1, D_MODEL), wr, br)


def _moe_kernel(cnt_ref, h_ref, rrow_ref, rcol_ref, gcol_ref, w1_ref, w3_ref, w2_ref, o_ref,
                xs_ref, acc_ref, *, tm):
    i, e, c = pl.program_id(0), pl.program_id(1), pl.program_id(2)
    n_sub = (cnt_ref[i * N_EXPERTS + e] + (MOE_SUB - 1)) // MOE_SUB

    @pl.when((e == 0) & (c == 0))
    def _():
        o_ref[...] = jnp.zeros_like(o_ref)

    @pl.when(c == 0)
    def _():
        rrow = rrow_ref[pl.ds(e, 1), :]
        slot = _iota((MOE_SUB, 1), 0).astype(F32)

        def gather(s, carry):
            base = pl.multiple_of(s * MOE_SUB, MOE_SUB)
            pick = (rrow == slot + (s * MOE_SUB).astype(F32)).astype(BF16)
            xs_ref[pl.ds(base, MOE_SUB), :] = jnp.dot(pick, h_ref[...], preferred_element_type=F32).astype(BF16)
            acc_ref[pl.ds(base, MOE_SUB), :] = jnp.zeros((MOE_SUB, D_MODEL), F32)
            return carry

        lax.fori_loop(0, n_sub, gather, 0)

    def expert_rows(base, m):
        xb = xs_ref[pl.ds(base, m), :]
        a = (_silu(jnp.dot(xb, w1_ref[...], preferred_element_type=F32))
             * jnp.dot(xb, w3_ref[...], preferred_element_type=F32))
        acc_ref[pl.ds(base, m), :] += jnp.dot(a.astype(BF16), w2_ref[...], preferred_element_type=F32)

    def expert_pair(p, carry):
        expert_rows(pl.multiple_of(p * (2 * MOE_SUB), 2 * MOE_SUB), 2 * MOE_SUB)
        return carry

    if tm >= 2 * MOE_SUB:
        lax.fori_loop(0, n_sub // 2, expert_pair, 0)

        @pl.when(n_sub % 2 == 1)
        def _():
            expert_rows(pl.multiple_of((n_sub - 1) * MOE_SUB, MOE_SUB), MOE_SUB)
    else:
        @pl.when(n_sub > 0)
        def _():
            expert_rows(0, MOE_SUB)

    @pl.when(c == pl.num_programs(2) - 1)
    def _():
        tp = min(tm, MOE_PLACE_ROWS)
        lane = _iota((tp, LANES), 1)
        slot = _iota((1, MOE_SUB), 1).astype(F32)

        def scatter(s, carry):
            base = pl.multiple_of(s * MOE_SUB, MOE_SUB)
            a = acc_ref[pl.ds(base, MOE_SUB), :]
            hi = a.astype(BF16)
            lo = (a - hi.astype(F32)).astype(BF16)
            for r0 in range(0, tm, tp):
                rcol = jnp.sum(jnp.where(lane == e, rcol_ref[r0:r0 + tp], 0.0), axis=1, keepdims=True)
                gate = jnp.sum(jnp.where(lane == e, gcol_ref[r0:r0 + tp], 0.0), axis=1, keepdims=True)
                place = (rcol == slot + (s * MOE_SUB).astype(F32)).astype(BF16)
                back = (jnp.dot(place, hi, preferred_element_type=F32)
                        + jnp.dot(place, lo, preferred_element_type=F32))
                o_ref[r0:r0 + tp] += gate * back
            return carry

        lax.fori_loop(0, n_sub, scatter, 0)


def _moe(h2, rrow, rcol, gcol, counts, w1, w3, w2, tm, tf):
    n = h2.shape[0]
    nt = n // tm
    dff = w1.shape[2]
    grid_spec = pltpu.PrefetchScalarGridSpec(
        num_scalar_prefetch=1,
        grid=(nt, N_EXPERTS, dff // tf),
        in_specs=[pl.BlockSpec((tm, D_MODEL), lambda i, e, c, cnt: (i, 0), pipeline_mode=pl.Buffered(1)),
                  pl.BlockSpec((None, N_EXPERTS, tm), lambda i, e, c, cnt: (i, 0, 0)),
                  pl.BlockSpec((tm, LANES), lambda i, e, c, cnt: (i, 0), pipeline_mode=pl.Buffered(1)),
                  pl.BlockSpec((tm, LANES), lambda i, e, c, cnt: (i, 0), pipeline_mode=pl.Buffered(1)),
                  pl.BlockSpec((None, D_MODEL, tf), lambda i, e, c, cnt: (e, 0, c)),
                  pl.BlockSpec((None, D_MODEL, tf), lambda i, e, c, cnt: (e, 0, c)),
                  pl.BlockSpec((None, tf, D_MODEL), lambda i, e, c, cnt: (e, c, 0))],
        out_specs=pl.BlockSpec((tm, D_MODEL), lambda i, e, c, cnt: (i, 0), pipeline_mode=pl.Buffered(1)),
        scratch_shapes=[pltpu.VMEM((tm, D_MODEL), BF16), pltpu.VMEM((tm, D_MODEL), F32)],
    )
    return pl.pallas_call(
        functools.partial(_moe_kernel, tm=tm),
        grid_spec=grid_spec,
        out_shape=jax.ShapeDtypeStruct((n, D_MODEL), F32),
        compiler_params=_cparams(("parallel", "arbitrary", "arbitrary")),
    )(counts, h2, rrow, rcol, gcol, w1, w3, w2)


def _moe_ffn(x2, norm_w, w_router, b_router, w1, w3, w2):
    n = x2.shape[0]
    tm = min(MOE_TILE, n)
    k = tm // ROUTER_ROWS
    h2, rcol, gcol, rrow, cnt = _router(x2, norm_w, w_router, b_router, ROUTER_ROWS, k)
    counts = cnt[k - 1::k, 0, :N_EXPERTS].astype(jnp.int32).reshape(-1)
    return _moe(h2, rrow, rcol, gcol, counts, w1.astype(BF16), w3.astype(BF16), w2.astype(BF16), tm, MOE_FF)


SC_CORES = 2
SC_SUBCORES = 16
SC_WINDOW = 128
SC_PIECE = 256
MOE_BLOCK = 512
HALF = D_MODEL // 2


HIGH16 = -65536


def _pack_bf16_pairs(h):
    bits = lax.bitcast_convert_type(h.astype(BF16).astype(F32), jnp.int32)
    return lax.shift_right_logical(bits[:, :HALF], 16) | (bits[:, HALF:] & HIGH16)


def _unpack_bf16_pairs(pieces):
    lo = [lax.bitcast_convert_type(lax.shift_left(w, 16), F32) for w in pieces]
    hi = [lax.bitcast_convert_type(w & HIGH16, F32) for w in pieces]
    return jnp.concatenate(lo + hi, axis=1).astype(BF16)


def _route_kernel(x_ref, nw_ref, wr_ref, br_ref, hp_ref, tok_ref, cnt_ref, seen_ref, *, tm):
    @pl.when(pl.program_id(0) == 0)
    def _():
        seen_ref[...] = jnp.zeros_like(seen_ref)

    x = x_ref[...]
    ms = jnp.mean(x * x, axis=-1, keepdims=True)
    h = x * lax.rsqrt(ms + NORM_EPS) * nw_ref[...]
    hp = _pack_bf16_pairs(h)
    for q in range(HALF // SC_PIECE):
        hp_ref[q] = hp[:, q * SC_PIECE:(q + 1) * SC_PIECE]
    lane = _iota((tm, LANES), 1)
    logits = jnp.where(lane < N_EXPERTS, _dot_hp(h, wr_ref[...]) + br_ref[...], -jnp.inf)
    m1 = jnp.max(logits, axis=1, keepdims=True)
    i1 = jnp.min(jnp.where(logits == m1, lane, LANES), axis=1, keepdims=True)
    rest = jnp.where(lane == i1, -jnp.inf, logits)
    m2 = jnp.max(rest, axis=1, keepdims=True)
    i2 = jnp.min(jnp.where(rest == m2, lane, LANES), axis=1, keepdims=True)
    e2 = jnp.exp(m2 - m1)
    g1 = 1.0 / (1.0 + e2)
    g2 = e2 / (1.0 + e2)
    self32 = ((lane == i1) | (lane == i2)).astype(F32)
    seen = seen_ref[...]
    tri = (_iota((tm, tm), 1) < _iota((tm, tm), 0)).astype(BF16)
    rank = jnp.dot(tri, self32.astype(BF16), preferred_element_type=F32) + seen
    r1 = jnp.sum(jnp.where(lane == i1, rank, 0.0), axis=1, keepdims=True)
    r2 = jnp.sum(jnp.where(lane == i2, rank, 0.0), axis=1, keepdims=True)
    cols = (i1.astype(F32), i2.astype(F32), r1, r2, g1, g2)
    tok = jnp.zeros((tm, LANES), F32)
    for j, col in enumerate(cols):
        tok = jnp.where(lane == j, col, tok)
    tok_ref[...] = tok
    seen = seen + jnp.sum(self32, axis=0, keepdims=True)
    seen_ref[...] = seen
    cnt_ref[...] = jnp.broadcast_to(seen, (8, LANES))


def _route(x2, norm_w, w_router, b_router, tm):
    n = x2.shape[0]
    wr = jnp.pad(w_router, ((0, 0), (0, LANES - N_EXPERTS)))
    br = jnp.pad(b_router, (0, LANES - N_EXPERTS)).reshape(1, LANES)
    return pl.pallas_call(
        functools.partial(_route_kernel, tm=tm),
        grid=(n // tm,),
        in_specs=[pl.BlockSpec((tm, D_MODEL), lambda i: (i, 0)),
                  pl.BlockSpec((1, D_MODEL), lambda i: (0, 0)),
                  pl.BlockSpec((D_MODEL, LANES), lambda i: (0, 0)),
                  pl.BlockSpec((1, LANES), lambda i: (0, 0))],
        out_specs=[pl.BlockSpec((HALF // SC_PIECE, tm, SC_PIECE), lambda i: (0, i, 0)),
                   pl.BlockSpec((tm, LANES), lambda i: (i, 0)),
                   pl.BlockSpec((8, LANES), lambda i: (0, 0))],
        out_shape=[jax.ShapeDtypeStruct((HALF // SC_PIECE, n, SC_PIECE), jnp.int32),
                   jax.ShapeDtypeStruct((n, LANES), F32),
                   jax.ShapeDtypeStruct((8, LANES), F32)],
        scratch_shapes=[pltpu.VMEM((1, LANES), F32)],
        compiler_params=_cparams(("arbitrary",)),
    )(x2, norm_w.reshape(1, D_MODEL), wr, br)


def _sc_rows_multiple(d):
    return SC_CORES * SC_SUBCORES * SC_WINDOW * SC_PIECE // d


def _sc_gather_rows(table, idx):
    f, v, _ = table.shape
    b = idx.shape[0]
    assert b % _sc_rows_multiple(f * SC_PIECE) == 0
    idx_all = (idx[None, :] + (jnp.arange(f, dtype=jnp.int32) * v)[:, None]).reshape(-1)
    return _sc_gather_pieces(table.reshape(f * v, SC_PIECE), idx_all).reshape(f, b, SC_PIECE)


def _sc_gather_pieces(table, idx):
    bp = idx.shape[0]
    d = table.shape[1]
    window = SC_WINDOW
    idx2 = idx.reshape(1, bp)
    mesh = plsc.VectorSubcoreMesh(core_axis_name="core", subcore_axis_name="subcore")

    @functools.partial(pl.kernel, out_type=jax.ShapeDtypeStruct((bp, d), table.dtype), mesh=mesh)
    def gather(x_hbm, i_hbm, o_hbm):
        def body(i_vmem, o_vmem):
            pltpu.sync_copy(x_hbm.at[i_vmem.at[0]], o_vmem)

        pltpu.emit_pipeline(
            body,
            grid=(bp // window,),
            in_specs=[pl.BlockSpec((1, window), index_map=lambda i: (0, i))],
            out_specs=[pl.BlockSpec((window, d), index_map=lambda i: (i, 0))],
            core_axis_name=("core", "subcore"),
            dimension_semantics=(pltpu.PARALLEL,),
        )(i_hbm, o_hbm)

    return gather(table, idx2)


def _sc_scatter_rows(rows, pos, n_slots):
    f, n0, _ = rows.shape
    n = -(-n0 // _sc_rows_multiple(D_MODEL)) * _sc_rows_multiple(D_MODEL)
    if n != n0:
        rows = jnp.pad(rows, ((0, 0), (0, n - n0), (0, 0)))
        pos = jnp.pad(pos.reshape(2, n0), ((0, 0), (0, n - n0)), constant_values=n_slots - 1)
    nb = n // SC_WINDOW
    assert (f * 2 * nb) % (SC_CORES * SC_SUBCORES) == 0
    idx = (pos.reshape(1, 2 * n) + (jnp.arange(f, dtype=jnp.int32) * n_slots)[:, None]).reshape(1, f * 2 * n)
    mesh = plsc.VectorSubcoreMesh(core_axis_name="core", subcore_axis_name="subcore")

    @functools.partial(pl.kernel, out_type=jax.ShapeDtypeStruct((f * n_slots, SC_PIECE), rows.dtype), mesh=mesh)
    def scatter(x_hbm, i_hbm, o_hbm):
        def body(x_vmem, i_vmem):
            pltpu.sync_copy(x_vmem, o_hbm.at[i_vmem.at[0]])

        pltpu.emit_pipeline(
            body,
            grid=(f * 2 * nb,),
            in_specs=[pl.BlockSpec((SC_WINDOW, SC_PIECE), index_map=lambda i: ((i // (2 * nb)) * nb + i % nb, 0)),
                      pl.BlockSpec((1, SC_WINDOW), index_map=lambda i: (0, i))],
            out_specs=[],
            core_axis_name=("core", "subcore"),
            dimension_semantics=(pltpu.PARALLEL,),
        )(x_hbm, i_hbm)

    return scatter(rows.reshape(f * n, SC_PIECE), idx).reshape(f, n_slots, SC_PIECE)


def _experts_kernel(be_ref, nb_ref, nv_ref, xs_ref, w1_ref, w3_ref, w2_ref, o_ref):
    g, c = pl.program_id(0), pl.program_id(1)

    @pl.when(c == 0)
    def _():
        o_ref[...] = jnp.zeros_like(o_ref)

    @pl.when(g < nb_ref[0])
    def _():
        live = _iota((xs_ref.shape[1], 1), 0) < nv_ref[g]
        xb = _unpack_bf16_pairs([jnp.where(live, xs_ref[q], 0) for q in range(HALF // SC_PIECE)])
        a = (_silu(jnp.dot(xb, w1_ref[...], preferred_element_type=F32))
             * jnp.dot(xb, w3_ref[...], preferred_element_type=F32))
        y = jnp.dot(a.astype(BF16), w2_ref[...], preferred_element_type=F32)
        for q in range(D_MODEL // SC_PIECE):
            o_ref[q] += y[:, q * SC_PIECE:(q + 1) * SC_PIECE]


def _experts(xs, block_expert, n_blocks, block_rows, w1, w3, w2, blk, tf):
    n_slots = xs.shape[1]
    dff = w1.shape[2]
    grid_spec = pltpu.PrefetchScalarGridSpec(
        num_scalar_prefetch=3,
        grid=(n_slots // blk, dff // tf),
        in_specs=[pl.BlockSpec((HALF // SC_PIECE, blk, SC_PIECE), lambda g, c, be, nb, nv: (0, g, 0)),
                  pl.BlockSpec((None, D_MODEL, tf), lambda g, c, be, nb, nv: (be[g], 0, c)),
                  pl.BlockSpec((None, D_MODEL, tf), lambda g, c, be, nb, nv: (be[g], 0, c)),
                  pl.BlockSpec((None, tf, D_MODEL), lambda g, c, be, nb, nv: (be[g], c, 0))],
        out_specs=pl.BlockSpec((D_MODEL // SC_PIECE, blk, SC_PIECE), lambda g, c, be, nb, nv: (0, g, 0)),
    )
    return pl.pallas_call(
        _experts_kernel,
        grid_spec=grid_spec,
        out_shape=jax.ShapeDtypeStruct((D_MODEL // SC_PIECE, n_slots, SC_PIECE), F32),
        compiler_params=_cparams(("parallel", "arbitrary")),
    )(block_expert, n_blocks, block_rows, xs, w1, w3, w2)


def _combine_kernel(x_ref, y1_ref, y2_ref, tok_ref, nw_ref, o_ref):
    tok = tok_ref[...]
    rows = lambda y_ref: jnp.concatenate([y_ref[q] for q in range(D_MODEL // SC_PIECE)], axis=1)
    x = x_ref[...] + tok[:, 4:5] * rows(y1_ref) + tok[:, 5:6] * rows(y2_ref)
    ms = jnp.mean(x * x, axis=-1, keepdims=True)
    o_ref[...] = x * lax.rsqrt(ms + NORM_EPS) * nw_ref[...]


def _combine(x2, ys2, tok, norm_w, tm):
    n = x2.shape[0]
    spec = pl.BlockSpec((tm, D_MODEL), lambda i: (i, 0))
    yspec = lambda off: pl.BlockSpec((D_MODEL // SC_PIECE, tm, SC_PIECE), lambda i: (0, i + off, 0))
    return pl.pallas_call(
        _combine_kernel,
        grid=(n // tm,),
        in_specs=[spec, yspec(0), yspec(n // tm),
                  pl.BlockSpec((tm, LANES), lambda i: (i, 0)), pl.BlockSpec((1, D_MODEL), lambda i: (0, 0))],
        out_specs=spec,
        out_shape=jax.ShapeDtypeStruct((n, D_MODEL), F32),
        compiler_params=_cparams(("parallel",)),
    )(x2, ys2, ys2, tok, norm_w.reshape(1, D_MODEL))


def _moe_final(x2, norm_w, w_router, b_router, w1, w3, w2, final_norm_w):
    n = x2.shape[0]
    blk = min(MOE_BLOCK, n)
    hp, tok, cnt = _route(x2, norm_w, w_router, b_router, ROUTER_ROWS)
    counts = cnt[0, :N_EXPERTS].astype(jnp.int32)
    padded = (counts + blk - 1) // blk * blk
    ends = jnp.cumsum(padded)
    base = ends - padded
    round_up = lambda a, m: -(-a // m) * m
    n_slots = round_up((-(-2 * n // blk) + N_EXPERTS + 1) * blk, _sc_rows_multiple(HALF))
    n_blocks = n_slots // blk
    i12 = tok[:, 0:2].astype(jnp.int32)
    pos = (base[i12] + tok[:, 2:4].astype(jnp.int32)).T.reshape(-1)
    block_start = jnp.arange(n_blocks, dtype=jnp.int32) * blk
    block_expert = jnp.minimum(jnp.searchsorted(ends, block_start, side='right'), N_EXPERTS - 1).astype(jnp.int32)
    block_rows = jnp.clip((base + counts)[block_expert] - block_start, 0, blk).astype(jnp.int32)
    xs = _sc_scatter_rows(hp, pos, n_slots)
    ys = _experts(xs, block_expert, (ends[-1:] // blk).astype(jnp.int32), block_rows,
                  w1.astype(BF16), w3.astype(BF16), w2.astype(BF16), blk, MOE_FF)
    pos = jnp.pad(pos, (0, round_up(2 * n, _sc_rows_multiple(D_MODEL)) - 2 * n))
    ys2 = _sc_gather_rows(ys, pos)
    return _combine(x2, ys2, tok, final_norm_w, min(1024, n))


def _final_kernel(x_ref, m_ref, nw_ref, o_ref):
    x = x_ref[...] + m_ref[...]
    ms = jnp.mean(x * x, axis=-1, keepdims=True)
    o_ref[...] = x * lax.rsqrt(ms + NORM_EPS) * nw_ref[...]


def _final(x2, m2, norm_w, tm):
    n = x2.shape[0]
    spec = pl.BlockSpec((tm, D_MODEL), lambda i: (i, 0))
    return pl.pallas_call(
        _final_kernel,
        grid=(n // tm,),
        in_specs=[spec, spec, pl.BlockSpec((1, D_MODEL), lambda i: (0, 0))],
        out_specs=spec,
        out_shape=jax.ShapeDtypeStruct((n, D_MODEL), F32),
        compiler_params=_cparams(("parallel",)),
    )(x2, m2, norm_w.reshape(1, D_MODEL))


def _permute_w_in(w):
    s5, rw, sg = w[:, 0:256], w[:, 256:1280], w[:, 1280:1792]
    gd, ab, gates = w[:, 1792:2816], w[:, 2816:2824], w[:, 2824:6920]
    pad = jnp.zeros((D_MODEL, PROJ_COLS - COL_AB - 8), w.dtype)
    return jnp.concatenate([rw, gd, sg, s5, ab, pad], axis=1).astype(BF16), gates.astype(BF16)


def _run_trunk(x, s5_h, rw_s, rw_shift, gd_s, gd_conv, p, w_in_perm):
    bsz, t, _ = x.shape
    n = bsz * t
    tm = min(1024, n)
    l = min(64, t)
    tc_s5 = min(256, t)
    new = ([], [], [], [], [], [])
    x2 = x.reshape(n, D_MODEL)
    moe_out = None
    for layer in range(2):
        g = lambda name: p[name][layer]
        w_mix, w_gate = w_in_perm[layer]
        proj = _norm_proj(x2, g('norm1_w'), w_mix, tm, 1024)
        p3 = proj.reshape(bsz, t, PROJ_COLS)
        y_a, s5_new = _s5_mixer(p3, s5_h[layer], g('s5_lam_re'), g('s5_lam_im'), g('s5_log_dt'), g('s5_b'),
                                g('s5_c'), g('s5_d'), g('s5_w_glu'), g('s5_b_glu'), tc_s5)
        y_b, rw_new = _rwkv_mixer(p3, rw_shift[layer], rw_s[layer], g('rw_mu'), g('rw_w0'), g('rw_w2'),
                                  g('rw_a0'), g('rw_a2'), g('rw_g2'), g('rw_k_k'), g('rw_k_a'), g('rw_r_k'),
                                  g('rw_ln_w'), g('rw_ln_b'), l)
        y_c, sg_v = _sgu_mixer(p3, g('sg_ln_w'), g('sg_ln_b'), g('sg_w_s'), g('sg_b_s'))
        y_d, gd_new = _gdn_mixer(p3, gd_conv[layer], gd_s[layer], g('gd_conv_w'), g('gd_a_log'),
                                 g('gd_dt_bias'), g('gd_norm_w'), l)
        shift_new = p3[:, t - 1, COL_RW:COL_RW + RW_COLS]
        conv_new = p3[:, t - (GD_CONV - 1):, COL_GD:COL_GD + GD_QKV]
        x2 = _merge((y_a, y_b, y_c, y_d), x2, g('norm1_w'), w_gate, g('w_branch'), g('w_out'), min(512, n))
        j = layer // 2
        if layer % 2 == 0:
            x2 = _ffn(x2, g('norm2_w'), p['ffn_w1'][j], p['ffn_w3'][j], p['ffn_w2'][j], min(512, n), 1408)
        else:
            y = _moe_final(x2, g('norm2_w'), p['moe_router'][j], p['moe_router_b'][j],
                           p['moe_w1'][j], p['moe_w3'][j], p['moe_w2'][j], p['final_norm_w'])
        for lst, s in zip(new, (s5_new, rw_new, shift_new, gd_new, conv_new, sg_v)):
            lst.append(s)
    return y.reshape(bsz, t, D_MODEL), [jnp.stack(lst) for lst in new]


def kernel(x_prompt, x_sample, state_s5, state_rwkv, state_rwkv_shift, state_gdn, state_gdn_conv, norm1_w, w_in, s5_lam_re, s5_lam_im, s5_log_dt, s5_b, s5_c, s5_d, s5_w_glu, s5_b_glu, rw_mu, rw_w0, rw_w2, rw_a0, rw_a2, rw_g2, rw_k_k, rw_k_a, rw_r_k, rw_ln_w, rw_ln_b, sg_ln_w, sg_ln_b, sg_w_s, sg_b_s, gd_conv_w, gd_a_log, gd_dt_bias, gd_norm_w, w_branch, w_out, norm2_w, ffn_w1, ffn_w3, ffn_w2, moe_router, moe_router_b, moe_w1, moe_w3, moe_w2, final_norm_w):
    p = {
        'norm1_w': norm1_w, 's5_lam_re': s5_lam_re, 's5_lam_im': s5_lam_im, 's5_log_dt': s5_log_dt,
        's5_b': s5_b, 's5_c': s5_c, 's5_d': s5_d, 's5_w_glu': s5_w_glu, 's5_b_glu': s5_b_glu,
        'rw_mu': rw_mu, 'rw_w0': rw_w0, 'rw_w2': rw_w2, 'rw_a0': rw_a0, 'rw_a2': rw_a2, 'rw_g2': rw_g2,
        'rw_k_k': rw_k_k, 'rw_k_a': rw_k_a, 'rw_r_k': rw_r_k, 'rw_ln_w': rw_ln_w, 'rw_ln_b': rw_ln_b,
        'sg_ln_w': sg_ln_w, 'sg_ln_b': sg_ln_b, 'sg_w_s': sg_w_s, 'sg_b_s': sg_b_s,
        'gd_conv_w': gd_conv_w, 'gd_a_log': gd_a_log, 'gd_dt_bias': gd_dt_bias, 'gd_norm_w': gd_norm_w,
        'w_branch': w_branch, 'w_out': w_out, 'norm2_w': norm2_w,
        'ffn_w1': ffn_w1, 'ffn_w3': ffn_w3, 'ffn_w2': ffn_w2,
        'moe_router': moe_router, 'moe_router_b': moe_router_b, 'moe_w1': moe_w1, 'moe_w3': moe_w3, 'moe_w2': moe_w2,
        'final_norm_w': final_norm_w,
    }
    w_in_perm = [_permute_w_in(w_in[layer]) for layer in range(2)]
    bp, dt = x_prompt.shape[0], x_prompt.dtype
    depth = w_in.shape[0]
    y_prompt, (s5_p, rw_p, rwsh_p, gd_p, gdc_p, _) = _run_trunk(
        x_prompt,
        jnp.zeros((depth, bp, S5_GROUPS, S5_STATE, 2), dt),
        jnp.zeros((depth, bp, HEADS, HEAD_W, HEAD_W), dt),
        jnp.zeros((depth, bp, RW_COLS), dt),
        jnp.zeros((depth, bp, HEADS, HEAD_W, HEAD_W), dt),
        jnp.zeros((depth, bp, GD_CONV - 1, GD_QKV), dt),
        p, w_in_perm)
    y_sample, (s5_s, rw_s, rwsh_s, gd_s, gdc_s, sgv_s) = _run_trunk(
        x_sample, state_s5, state_rwkv, state_rwkv_shift, state_gdn, state_gdn_conv, p, w_in_perm)
    return (y_prompt, y_sample, s5_p, rw_p, rwsh_p, gd_p, gdc_p, s5_s, rw_s, rwsh_s, gd_s, gdc_s, sgv_s)
```

```python
import functools

import jax
import jax.numpy as jnp
from jax import lax
from jax.experimental import pallas as pl
from jax.experimental.pallas import tpu as pltpu
from jax.experimental.pallas import tpu_sc as plsc

F32 = jnp.float32
BF16 = jnp.bfloat16

D_MODEL = 1024
BRANCH_W = 256
HEADS = 4
HEAD_W = 64
S5_GROUPS = 16
S5_GROUP = 16
S5_STATE = 64
S5_W = S5_GROUPS * S5_STATE
S5_ROWS = 8
SG_CHUNK = 128
SG_ROWS = 512
GD_CONV = 4
GD_QKV = 3 * BRANCH_W
RW_COLS = 1024
RW_EPS = 64e-5
NORM_EPS = 1e-6
N_EXPERTS = 8
LANES = 128

COL_RW = 0
COL_GD = 1024
COL_SG = 2048
COL_S5 = 2560
COL_AB = 2816
PROJ_COLS = 3072

VMEM_LIMIT = 48 * 1024 * 1024


def _cparams(sem):
    return pltpu.CompilerParams(dimension_semantics=sem, vmem_limit_bytes=VMEM_LIMIT)


def _dot(a, b):
    return jnp.dot(a.astype(BF16), b.astype(BF16), preferred_element_type=F32)


def _dot_nt(a, b):
    return lax.dot_general(a.astype(BF16), b.astype(BF16), (((1,), (1,)), ((), ())),
                           preferred_element_type=F32)


def _split3(a):
    hi = a.astype(BF16)
    r1 = a - hi.astype(F32)
    mid = r1.astype(BF16)
    lo = (r1 - mid.astype(F32)).astype(BF16)
    return hi, mid, lo


def _dot3_left(b_exact, a):
    hi, mid, lo = _split3(a)
    b = b_exact.astype(BF16)
    return (jnp.dot(b, hi, preferred_element_type=F32) + jnp.dot(b, mid, preferred_element_type=F32)
            + jnp.dot(b, lo, preferred_element_type=F32))


def _dot_hp(a, b):
    a0, a1, _ = _split3(a)
    b0, b1, _ = _split3(b)
    d = lambda x, y: jnp.dot(x, y, preferred_element_type=F32)
    return d(a0, b0) + (d(a0, b1) + d(a1, b0))


def _iota(shape, axis):
    return lax.broadcasted_iota(jnp.int32, shape, axis)


def _head_ones():
    r = _iota((BRANCH_W, BRANCH_W), 0) // HEAD_W
    c = _iota((BRANCH_W, BRANCH_W), 1) // HEAD_W
    return (r == c).astype(BF16)


def _head_mask(l):
    r = _iota((HEADS * l, BRANCH_W), 0) // l
    c = _iota((HEADS * l, BRANCH_W), 1) // HEAD_W
    return r == c


def _expand(x, mask):
    return jnp.where(mask, jnp.concatenate([x] * HEADS, axis=0), 0.0)


def _tri_masks(l):
    i = _iota((l, HEADS * l), 0)
    j = _iota((l, HEADS * l), 1) % l
    return j < i, j <= i


def _same_head(l):
    n = HEADS * l
    return (_iota((n, n), 0) // l) == (_iota((n, n), 1) // l)


def _expand_sq(x, same):
    return jnp.where(same, jnp.concatenate([x] * HEADS, axis=0), 0.0)


def _unit_lower_inverse(a_strict, l):
    i = _iota((l, HEADS * l), 0)
    j = _iota((l, HEADS * l), 1) % l
    eye = (i == j).astype(F32)
    same = _same_head(l)
    p = [-a for a in a_strict]
    t = [eye + x for x in p]
    k = 2
    while k < l:
        p = [_dot(x, _expand_sq(x, same)) for x in p]
        t = [y + _dot(y, _expand_sq(x, same)) for y, x in zip(t, p)]
        k *= 2
    return t


def _dot2(a, b_exact):
    hi = a.astype(BF16)
    lo = (a - hi.astype(F32)).astype(BF16)
    b = b_exact.astype(BF16)
    return jnp.dot(hi, b, preferred_element_type=F32) + jnp.dot(lo, b, preferred_element_type=F32)


def _cumsum_rows(x, l, nb=1):
    n = nb * l
    i, j = _iota((n, n), 0), _iota((n, n), 1)
    tri = ((j <= i) & ((i // l) == (j // l))).astype(BF16)
    return _dot3_left(tri, x)


def _softplus(x):
    return jnp.maximum(x, 0.0) + jnp.log(1.0 + jnp.exp(-jnp.abs(x)))


def _silu(x):
    return x * jax.nn.sigmoid(x)


def _norm_proj_kernel(x_ref, nw_ref, w_ref, o_ref, h_ref):
    @pl.when(pl.program_id(1) == 0)
    def _():
        x = x_ref[...]
        ms = jnp.mean(x * x, axis=-1, keepdims=True)
        h_ref[...] = (x * lax.rsqrt(ms + NORM_EPS) * nw_ref[...]).astype(BF16)

    o_ref[...] = jnp.dot(h_ref[...], w_ref[...], preferred_element_type=F32)


def _norm_proj(x2, norm_w, w_bf16, tm, tn):
    n = x2.shape[0]
    ncol = w_bf16.shape[1]
    return pl.pallas_call(
        _norm_proj_kernel,
        grid=(n // tm, ncol // tn),
        in_specs=[pl.BlockSpec((tm, D_MODEL), lambda i, j: (i, 0)),
                  pl.BlockSpec((1, D_MODEL), lambda i, j: (0, 0)),
                  pl.BlockSpec((D_MODEL, tn), lambda i, j: (0, j))],
        out_specs=pl.BlockSpec((tm, tn), lambda i, j: (i, j)),
        out_shape=jax.ShapeDtypeStruct((n, ncol), F32),
        scratch_shapes=[pltpu.VMEM((tm, D_MODEL), BF16)],
        compiler_params=_cparams(("parallel", "arbitrary")),
    )(x2, norm_w.reshape(1, D_MODEL), w_bf16)


def _s5_kernel(u_ref, h0_ref, wb_ref, wc_ref, a2k_ref, apow_ref, d_ref, wg_ref, bg_ref,
               y_ref, hl_ref, hs_ref, hb_ref, *, tc):
    c = pl.program_id(1)

    @pl.when(c == 0)
    def _():
        hs_ref[...] = h0_ref[...]

    u = u_ref[...]
    x = _dot(u, wb_ref[...])
    xr, xi = x[:, :S5_W], x[:, S5_W:]
    ng = tc // S5_ROWS
    xr = xr.reshape(ng, S5_ROWS, S5_W)
    xi = xi.reshape(ng, S5_ROWS, S5_W)
    row = _iota((S5_ROWS, 1), 0)
    k, d = 0, 1
    while d < S5_ROWS:
        m = row >= d
        ar = jnp.where(m, a2k_ref[k:k + 1, :S5_W], 0.0)
        ai = jnp.where(m, a2k_ref[k:k + 1, S5_W:], 0.0)
        sr = pltpu.roll(xr, d, axis=1)
        si = pltpu.roll(xi, d, axis=1)
        xr, xi = xr + (ar * sr - ai * si), xi + (ar * si + ai * sr)
        k, d = k + 1, d * 2
    hb_ref[:, :S5_W] = xr.reshape(tc, S5_W)
    hb_ref[:, S5_W:] = xi.reshape(tc, S5_W)
    pr, pi_ = apow_ref[:, :S5_W], apow_ref[:, S5_W:]

    def group(gi, carry):
        cr, ci = carry
        rows = pl.ds(pl.multiple_of(gi * S5_ROWS, S5_ROWS), S5_ROWS)
        hr = hb_ref[rows, :S5_W] + (pr * cr - pi_ * ci)
        hi = hb_ref[rows, S5_W:] + (pr * ci + pi_ * cr)
        hb_ref[rows, :S5_W] = hr
        hb_ref[rows, S5_W:] = hi
        return hr[S5_ROWS - 1:S5_ROWS], hi[S5_ROWS - 1:S5_ROWS]

    cr, ci = lax.fori_loop(0, tc // S5_ROWS, group, (hs_ref[:, :S5_W], hs_ref[:, S5_W:]), unroll=4)
    hs_ref[:, :S5_W] = cr
    hs_ref[:, S5_W:] = ci
    y = _dot(hb_ref[:, :S5_W], wc_ref[:S5_W]) + _dot(hb_ref[:, S5_W:], wc_ref[S5_W:])
    y = jax.nn.gelu(y + d_ref[...] * u)
    y = y * jax.nn.sigmoid(_dot(y, wg_ref[...]) + bg_ref[...])
    y_ref[...] = y.astype(y_ref.dtype)

    @pl.when(c == pl.num_programs(1) - 1)
    def _():
        hl_ref[...] = hs_ref[...]


def _s5_tables(lam_re, lam_im, log_dt, b_c, c_c, tc):
    dt = jnp.exp(log_dt)[:, None]
    mag = jnp.exp(lam_re * dt)
    ab_re, ab_im = mag * jnp.cos(lam_im * dt), mag * jnp.sin(lam_im * dt)
    den = lam_re * lam_re + lam_im * lam_im
    nr = ab_re - 1.0
    cf_re = (nr * lam_re + ab_im * lam_im) / den
    cf_im = (ab_im * lam_re - nr * lam_im) / den
    br, bi = b_c[..., 0], b_c[..., 1]
    bb_re = cf_re[..., None] * br - cf_im[..., None] * bi
    bb_im = cf_re[..., None] * bi + cf_im[..., None] * br
    eye = jnp.eye(S5_GROUPS, dtype=F32)
    bd_in = lambda m: jnp.einsum('gph,gk->ghkp', m, eye).reshape(BRANCH_W, S5_W)
    wb = jnp.concatenate([bd_in(bb_re), bd_in(bb_im)], axis=1)
    cr, ci = c_c[..., 0], c_c[..., 1]
    bd_out = lambda m: jnp.einsum('ghp,gk->gpkh', m, eye).reshape(S5_W, BRANCH_W)
    wc = jnp.concatenate([bd_out(cr), -bd_out(ci)], axis=0)
    pr, pi_ = ab_re.reshape(1, S5_W), ab_im.reshape(1, S5_W)
    lv_r, lv_i = [], []
    tr, ti = pr, pi_
    d = 1
    while d < tc:
        lv_r.append(pr)
        lv_i.append(pi_)
        tr, ti = (jnp.concatenate([tr, tr * pr - ti * pi_], axis=0),
                  jnp.concatenate([ti, tr * pi_ + ti * pr], axis=0))
        pr, pi_ = pr * pr - pi_ * pi_, 2.0 * pr * pi_
        d *= 2
    n_lv = len(lv_r)
    pad = (-n_lv) % 8
    a2k = jnp.concatenate([jnp.concatenate(lv_r, axis=0), jnp.concatenate(lv_i, axis=0)], axis=1)
    a2k = jnp.pad(a2k, ((0, pad), (0, 0)))
    apow = jnp.concatenate([tr, ti], axis=1)
    return wb.astype(BF16), wc.astype(BF16), a2k, apow


def _s5_mixer(p3, h0, lam_re, lam_im, log_dt, b_c, c_c, d_skip, w_glu, b_glu, tc):
    bsz, t, _ = p3.shape
    wb, wc, a2k, apow = _s5_tables(lam_re, lam_im, log_dt, b_c, c_c, S5_ROWS)
    h0f = jnp.concatenate([h0[..., 0].reshape(bsz, 1, S5_W), h0[..., 1].reshape(bsz, 1, S5_W)], axis=-1)
    full = lambda a: pl.BlockSpec(a.shape, lambda b, c: (0,) * a.ndim)
    d2, bg2, wg = d_skip.reshape(1, BRANCH_W), b_glu.reshape(1, BRANCH_W), w_glu.astype(BF16)
    y, hl = pl.pallas_call(
        functools.partial(_s5_kernel, tc=tc),
        grid=(bsz, t // tc),
        in_specs=[pl.BlockSpec((None, tc, BRANCH_W), lambda b, c: (b, c, COL_S5 // BRANCH_W)),
                  pl.BlockSpec((None, 1, 2 * S5_W), lambda b, c: (b, 0, 0)),
                  full(wb), full(wc), full(a2k), full(apow), full(d2), full(wg), full(bg2)],
        out_specs=[pl.BlockSpec((None, tc, BRANCH_W), lambda b, c: (b, c, 0)),
                   pl.BlockSpec((None, 1, 2 * S5_W), lambda b, c: (b, 0, 0))],
        out_shape=[jax.ShapeDtypeStruct((bsz, t, BRANCH_W), BF16),
                   jax.ShapeDtypeStruct((bsz, 1, 2 * S5_W), F32)],
        scratch_shapes=[pltpu.VMEM((1, 2 * S5_W), F32), pltpu.VMEM((tc, 2 * S5_W), F32)],
        compiler_params=_cparams(("parallel", "arbitrary")),
    )(p3, h0f, wb, wc, a2k, apow, d2, wg, bg2)
    h_last = jnp.stack([hl[:, 0, :S5_W].reshape(bsz, S5_GROUPS, S5_STATE),
                        hl[:, 0, S5_W:].reshape(bsz, S5_GROUPS, S5_STATE)], axis=-1)
    return y, h_last


def _rwkv_kernel(z_ref, sh0_ref, s0_ref, mu_ref, w0_ref, w2_ref, a0_ref, a2_ref, g2_ref,
                 kk_ref, ka_ref, rk_ref, lnw_ref, lnb_ref,
                 y_ref, sl_ref, st_ref, zp_ref, *, l, bb, nt):
    c = pl.program_id(1)

    @pl.when(c == 0)
    def _():
        st_ref[...] = s0_ref[...]
        zp_ref[...] = sh0_ref[...]

    tl = nt * l
    row = _iota((tl, 1), 0)
    zms = []
    for b in range(bb):
        z = z_ref[b]
        prev = jnp.where(row == 0, zp_ref[b], pltpu.roll(z, 1, axis=0))
        zp_ref[b] = z[tl - 1:tl]
        zms.append(z + (prev - z) * mu_ref[...])
    zm = jnp.concatenate(zms, axis=0)
    r, k, v = zm[:, 0:256], zm[:, 256:512], zm[:, 512:768]
    lo = zm[:, 768:896]
    g_lo = zm[:, 896:1024]
    w_log = -_softplus(-(w0_ref[...] + _dot(jnp.tanh(lo), w2_ref[...]))) - 0.5
    lw = -jnp.exp(w_log)
    a = jax.nn.sigmoid(a0_ref[...] + _dot(lo, a2_ref[...]))
    g = _dot(jax.nn.sigmoid(g_lo), g2_ref[...])
    ones_h = _head_ones()
    kk = k * kk_ref[...]
    kk = kk * lax.rsqrt(_dot2(kk * kk, ones_h) + NORM_EPS)
    k = k * (1.0 + (a - 1.0) * ka_ref[...])
    kka = kk * a

    cum = _cumsum_rows(lw, l, bb * nt)
    p_incl = jnp.exp(cum)
    p_inv = jnp.exp(-cum)
    kt = kk * jnp.exp(cum - lw)
    rt = r * p_incl
    kh = k * p_inv
    ah = kka * p_inv

    hm = _head_mask(l)
    strict, incl = _tri_masks(l)
    chunks = [(b, j) for j in range(nt) for b in range(bb)]
    rows = {s: slice((s[0] * nt + s[1]) * l, (s[0] * nt + s[1] + 1) * l) for s in chunks}
    ex = lambda x: {s: _expand(x[rows[s]], hm) for s in chunks}
    kh_e, ah_e, v_e = ex(kh), ex(ah), ex(v)
    a_aa = [jnp.where(strict, _dot_nt(kt[rows[s]], ah_e[s]), 0.0) for s in chunks]
    a_ak = {s: jnp.where(strict, _dot_nt(kt[rows[s]], kh_e[s]), 0.0) for s in chunks}
    b_ra = {s: jnp.where(incl, _dot_nt(rt[rows[s]], ah_e[s]), 0.0) for s in chunks}
    b_rk = {s: jnp.where(incl, _dot_nt(rt[rows[s]], kh_e[s]), 0.0) for s in chunks}
    t_inv = dict(zip(chunks, _unit_lower_inverse(a_aa, l)))
    av = {s: _dot(a_ak[s], v_e[s]) for s in chunks}
    bv = {s: _dot(b_rk[s], v_e[s]) for s in chunks}
    to_end = {s: jnp.exp(cum[rows[s]][l - 1:l] - cum[rows[s]]) for s in chunks}
    head_blk = _same_head(HEAD_W)
    st = [st_ref[b] for b in range(bb)]
    yb = {}
    for j in range(nt):
        now = [(b, j) for b in range(bb)]
        rhs = {s: _dot_nt(kt[rows[s]], st[s[0]]) + av[s] for s in now}
        ys = {s: _dot_nt(rt[rows[s]], st[s[0]]) + bv[s] for s in now}
        u = {s: _dot(t_inv[s], _expand(rhs[s], hm)) for s in now}
        for s in now:
            yb[s] = ys[s] - _dot(b_ra[s], _expand(u[s], hm))
        lhs_t = {s: jnp.concatenate([v[rows[s]], -u[s]], axis=0).T for s in now}
        rhs_k = {s: jnp.concatenate([k[rows[s]] * to_end[s], kka[rows[s]] * to_end[s]], axis=0) for s in now}
        for s in now:
            p_last = p_incl[rows[s]][l - 1:l]
            st[s[0]] = st[s[0]] * p_last + jnp.where(head_blk, _dot(lhs_t[s], rhs_k[s]), 0.0)
    for b in range(bb):
        st_ref[b] = st[b]
    y = jnp.concatenate([yb[(b, j)] for b in range(bb) for j in range(nt)], axis=0)

    inv_w = 1.0 / HEAD_W
    mean = _dot2(y, ones_h) * inv_w
    yc = y - mean
    var = _dot2(yc * yc, ones_h) * inv_w
    y = yc * lax.rsqrt(var + RW_EPS) * lnw_ref[...] + lnb_ref[...]
    bonus = _dot2(r * k * rk_ref[...], ones_h) * v
    y = ((y + bonus) * g).astype(y_ref.dtype)
    for b in range(bb):
        y_ref[b] = y[b * tl:(b + 1) * tl]

    @pl.when(c == pl.num_programs(1) - 1)
    def _():
        sl_ref[...] = st_ref[...]


MIX_CHUNK = 64
STACK_STREAMS = 4
STACK_ROWS = 512


def _streams_per_step(bsz, t, l):
    bb = min(bsz, STACK_STREAMS)
    while bsz % bb:
        bb -= 1
    nt = max(1, min(t, STACK_ROWS // bb) // l)
    while (t // l) % nt:
        nt -= 1
    return bb, nt


def _block_diag_heads(s):
    bsz = s.shape[0]
    eye = jnp.eye(HEADS, dtype=s.dtype)
    return jnp.einsum('bhij,hg->bhigj', s, eye).reshape(bsz, BRANCH_W, BRANCH_W)


def _diag_blocks(s):
    bsz = s.shape[0]
    s5 = s.reshape(bsz, HEADS, HEAD_W, HEADS, HEAD_W)
    return jnp.stack([s5[:, h, :, h, :] for h in range(HEADS)], axis=1)


def _rwkv_mixer(p3, shift0, s0, mu, w0, w2, a0, a2, g2, k_k, k_a, r_k, ln_w, ln_b, l):
    bsz, t, _ = p3.shape
    row = lambda a: a.reshape(1, -1)
    w2p = jnp.concatenate([w2, jnp.zeros_like(w2)], axis=0).astype(BF16)
    a2p = jnp.concatenate([jnp.zeros_like(a2), a2], axis=0).astype(BF16)
    args = (shift0.reshape(bsz, 1, RW_COLS), _block_diag_heads(s0), row(mu), row(w0), w2p, row(a0), a2p,
            g2.astype(BF16), row(k_k), row(k_a), row(r_k), row(ln_w), row(ln_b))
    full = lambda a: pl.BlockSpec(a.shape, lambda b, c: (0,) * a.ndim)
    bb, nt = _streams_per_step(bsz, t, l)
    y, sl = pl.pallas_call(
        functools.partial(_rwkv_kernel, l=l, bb=bb, nt=nt),
        grid=(bsz // bb, t // (nt * l)),
        in_specs=[pl.BlockSpec((bb, nt * l, RW_COLS), lambda b, c: (b, c, COL_RW // RW_COLS)),
                  pl.BlockSpec((bb, 1, RW_COLS), lambda b, c: (b, 0, 0)),
                  pl.BlockSpec((bb, BRANCH_W, BRANCH_W), lambda b, c: (b, 0, 0))]
                 + [full(a) for a in args[2:]],
        out_specs=[pl.BlockSpec((bb, nt * l, BRANCH_W), lambda b, c: (b, c, 0)),
                   pl.BlockSpec((bb, BRANCH_W, BRANCH_W), lambda b, c: (b, 0, 0))],
        out_shape=[jax.ShapeDtypeStruct((bsz, t, BRANCH_W), BF16),
                   jax.ShapeDtypeStruct((bsz, BRANCH_W, BRANCH_W), F32)],
        scratch_shapes=[pltpu.VMEM((bb, BRANCH_W, BRANCH_W), F32), pltpu.VMEM((bb, 1, RW_COLS), F32)],
        compiler_params=_cparams(("parallel", "arbitrary")),
    )(p3, *args)
    return y, _diag_blocks(sl)


def _sgu_kernel(z_ref, lnw_ref, lnb_ref, wm_ref, bias_ref, o_ref, v_ref, *, l, nc):
    zg = jax.nn.gelu(z_ref[...])
    u, v = zg[:, :BRANCH_W], zg[:, BRANCH_W:]
    mean = jnp.mean(v, axis=-1, keepdims=True)
    vc = v - mean
    var = jnp.mean(vc * vc, axis=-1, keepdims=True)
    v = vc * lax.rsqrt(var + NORM_EPS) * lnw_ref[...] + lnb_ref[...]
    v_ref[...] = v
    hm = _head_mask(l)
    wm = wm_ref[...]
    for i in range(nc):
        rows = slice(i * l, (i + 1) * l)
        mixed = bias_ref[...] + _dot(wm, _expand(v[rows], hm))
        o_ref[rows, :] = (u[rows] * mixed).astype(o_ref.dtype)


def _sgu_mixer(p3, ln_w, ln_b, w_s, b_s):
    bsz, t, _ = p3.shape
    l = min(SG_CHUNK, t)
    nc = max(1, min(SG_ROWS, t) // l)
    tril = jnp.tril(jnp.ones((l, l), F32))
    wm = jnp.transpose(w_s[:, :l, :l] * tril, (1, 0, 2)).reshape(l, HEADS * l).astype(BF16)
    bias = jnp.repeat(jnp.transpose(b_s[:, :l]), HEAD_W, axis=1)
    row = lambda a: a.reshape(1, -1)
    full = lambda a: pl.BlockSpec(a.shape, lambda b, c: (0,) * a.ndim)
    args = (row(ln_w), row(ln_b), wm, bias)
    return pl.pallas_call(
        functools.partial(_sgu_kernel, l=l, nc=nc),
        grid=(bsz, t // (nc * l)),
        in_specs=[pl.BlockSpec((None, nc * l, 2 * BRANCH_W), lambda b, c: (b, c, COL_SG // (2 * BRANCH_W)))]
                 + [full(a) for a in args],
        out_specs=[pl.BlockSpec((None, nc * l, BRANCH_W), lambda b, c: (b, c, 0)),
                   pl.BlockSpec((None, nc * l, BRANCH_W), lambda b, c: (b, c, 0))],
        out_shape=[jax.ShapeDtypeStruct((bsz, t, BRANCH_W), BF16),
                   jax.ShapeDtypeStruct((bsz, t, BRANCH_W), F32)],
        compiler_params=_cparams(("parallel", "parallel")),
    )(p3, *args)


def _gdn_kernel(z_ref, ab_ref, cv0_ref, s0_ref, cw_ref, alog_ref, dtb_ref, nw_ref,
                y_ref, sl_ref, st_ref, cv_ref, *, l, bb, nt):
    c = pl.program_id(1)

    @pl.when(c == 0)
    def _():
        st_ref[...] = s0_ref[...]
        cv_ref[...] = cv0_ref[...]

    tl = nt * l
    row8 = _iota((8, 1), 0)
    convs, gates = [], []
    for b in range(bb):
        z = z_ref[b]
        qkv = z[:, :GD_QKV]
        gates.append(z[:, GD_QKV:])
        carry = cv_ref[b]
        cv_ref[b] = qkv[tl - 8:tl]
        conv = qkv * cw_ref[GD_CONV - 1:GD_CONV]
        for j in range(1, GD_CONV):
            sh = pltpu.roll(qkv, j, axis=0)
            top = jnp.where(row8 < j, pltpu.roll(carry, j, axis=0), sh[:8])
            sh = jnp.concatenate([top, sh[8:]], axis=0) if tl > 8 else top
            conv = conv + sh * cw_ref[GD_CONV - 1 - j:GD_CONV - j]
        convs.append(conv)
    conv = _silu(jnp.concatenate(convs, axis=0))
    gate = jnp.concatenate(gates, axis=0)
    q, k, v = conv[:, :256], conv[:, 256:512], conv[:, 512:768]
    ones_h = _head_ones()
    q = q * lax.rsqrt(_dot2(q * q, ones_h) + NORM_EPS) * (HEAD_W ** -0.5)
    k = k * lax.rsqrt(_dot2(k * k, ones_h) + NORM_EPS)
    ab = jnp.concatenate([ab_ref[b] for b in range(bb)], axis=0)
    lane_h = _iota((bb * tl, BRANCH_W), 1) // HEAD_W
    a_in = jnp.zeros((bb * tl, BRANCH_W), F32)
    b_in = jnp.zeros((bb * tl, BRANCH_W), F32)
    for h in range(HEADS):
        a_in = jnp.where(lane_h == h, ab[:, h:h + 1], a_in)
        b_in = jnp.where(lane_h == h, ab[:, HEADS + h:HEADS + h + 1], b_in)
    beta = jax.nn.sigmoid(b_in)
    g = -jnp.exp(alog_ref[...]) * _softplus(a_in + dtb_ref[...])
    gc = _cumsum_rows(g, l, bb * nt)
    eg = jnp.exp(gc)
    kb = k * beta
    vb = v * beta
    kbg = kb * eg
    qg = q * eg

    hm = _head_mask(l)
    strict, incl = _tri_masks(l)
    n = HEADS * l
    chunks = [(b, j) for j in range(nt) for b in range(bb)]
    rows = {s: slice((s[0] * nt + s[1]) * l, (s[0] * nt + s[1] + 1) * l) for s in chunks}
    ex = lambda x: {s: _expand(x[rows[s]], hm) for s in chunks}
    k_e, vb_e, kbg_e = ex(k), ex(vb), ex(kbg)
    lane_hd = _iota((l, n), 1) // l
    eye = _iota((l, n), 0) == (_iota((l, n), 1) % l)
    decay = {}
    for s in chunks:
        gi = jnp.zeros((l, n), F32)
        for h in range(HEADS):
            gi = jnp.where(lane_hd == h, gc[rows[s]][:, h * HEAD_W:h * HEAD_W + 1], gi)
        gj = jnp.sum(jnp.where(eye, gi, 0.0), axis=0, keepdims=True)
        decay[s] = jnp.where(incl, jnp.exp(jnp.where(incl, gi - gj, 0.0)), 0.0)
    lm = [jnp.where(strict, _dot_nt(kb[rows[s]], k_e[s]) * decay[s], 0.0) for s in chunks]
    qk = {s: _dot_nt(q[rows[s]], k_e[s]) * decay[s] for s in chunks}
    t_inv = dict(zip(chunks, _unit_lower_inverse(lm, l)))
    uc = {s: _dot(t_inv[s], vb_e[s]) for s in chunks}
    wc = {s: _dot(t_inv[s], kbg_e[s]) for s in chunks}
    g_last = {s: gc[rows[s]][l - 1:l] for s in chunks}
    k_dec = {s: (k[rows[s]] * jnp.exp(g_last[s] - gc[rows[s]])).T for s in chunks}
    head_blk = _same_head(HEAD_W)
    st = [st_ref[b] for b in range(bb)]
    ob = {}
    for j in range(nt):
        now = [(b, j) for b in range(bb)]
        o0 = {s: _dot(qg[rows[s]], st[s[0]]) for s in now}
        v_new = {s: uc[s] - _dot(wc[s], st[s[0]]) for s in now}
        for s in now:
            ob[s] = o0[s] + _dot(qk[s], _expand(v_new[s], hm))
        for s in now:
            st[s[0]] = st[s[0]] * jnp.exp(g_last[s]) + jnp.where(head_blk, _dot(k_dec[s], v_new[s]), 0.0)
    for b in range(bb):
        st_ref[b] = st[b]
    o = jnp.concatenate([ob[(b, j)] for b in range(bb) for j in range(nt)], axis=0)
    ms = _dot2(o * o, ones_h) * (1.0 / HEAD_W)
    o = (o * lax.rsqrt(ms + NORM_EPS) * nw_ref[...] * _silu(gate)).astype(y_ref.dtype)
    for b in range(bb):
        y_ref[b] = o[b * tl:(b + 1) * tl]

    @pl.when(c == pl.num_programs(1) - 1)
    def _():
        sl_ref[...] = st_ref[...]


def _gdn_mixer(p3, conv0, s0, conv_w, a_log, dt_bias, norm_w, l):
    bsz, t, _ = p3.shape
    cv0 = jnp.pad(conv0, ((0, 0), (8 - (GD_CONV - 1), 0), (0, 0)))
    cw = jnp.pad(conv_w, ((0, 8 - GD_CONV), (0, 0)))
    per_head = lambda a: jnp.repeat(a, HEAD_W).reshape(1, BRANCH_W)
    args = (cv0, _block_diag_heads(s0), cw, per_head(a_log), per_head(dt_bias),
            jnp.tile(norm_w, HEADS).reshape(1, BRANCH_W))
    full = lambda a: pl.BlockSpec(a.shape, lambda b, c: (0,) * a.ndim)
    bb, nt = _streams_per_step(bsz, t, l)
    y, sl = pl.pallas_call(
        functools.partial(_gdn_kernel, l=l, bb=bb, nt=nt),
        grid=(bsz // bb, t // (nt * l)),
        in_specs=[pl.BlockSpec((bb, nt * l, 1024), lambda b, c: (b, c, COL_GD // 1024)),
                  pl.BlockSpec((bb, nt * l, LANES), lambda b, c: (b, c, COL_AB // LANES)),
                  pl.BlockSpec((bb, 8, GD_QKV), lambda b, c: (b, 0, 0)),
                  pl.BlockSpec((bb, BRANCH_W, BRANCH_W), lambda b, c: (b, 0, 0))]
                 + [full(a) for a in args[2:]],
        out_specs=[pl.BlockSpec((bb, nt * l, BRANCH_W), lambda b, c: (b, c, 0)),
                   pl.BlockSpec((bb, BRANCH_W, BRANCH_W), lambda b, c: (b, 0, 0))],
        out_shape=[jax.ShapeDtypeStruct((bsz, t, BRANCH_W), BF16),
                   jax.ShapeDtypeStruct((bsz, BRANCH_W, BRANCH_W), F32)],
        scratch_shapes=[pltpu.VMEM((bb, BRANCH_W, BRANCH_W), F32), pltpu.VMEM((bb, 8, GD_QKV), F32)],
        compiler_params=_cparams(("parallel", "arbitrary")),
    )(p3, p3, *args)
    return y, _diag_blocks(sl)


def _merge_kernel(ya_ref, yb_ref, yc_ref, yd_ref, x_ref, nw_ref, wg_ref, wbr_ref, wout_ref, o_ref):
    x = x_ref[...]
    ms = jnp.mean(x * x, axis=-1, keepdims=True)
    h = (x * lax.rsqrt(ms + NORM_EPS) * nw_ref[...]).astype(BF16)
    m = None
    for b, y_ref in enumerate((ya_ref, yb_ref, yc_ref, yd_ref)):
        gate = jnp.dot(h, wg_ref[:, b * D_MODEL:(b + 1) * D_MODEL], preferred_element_type=F32)
        br = jnp.dot(y_ref[...], wbr_ref[b], preferred_element_type=F32)
        term = (0.5 * jnp.tanh(0.5 * gate) + 0.5) * br
        m = term if m is None else m + term
    o_ref[...] = x + jnp.dot(m.astype(BF16), wout_ref[...], preferred_element_type=F32)


def _merge(ys, x2, norm_w, w_gate, w_branch, w_out, tm):
    n = x2.shape[0]
    yspec = pl.BlockSpec((tm, BRANCH_W), lambda i: (i, 0))
    once = pl.Buffered(1)
    return pl.pallas_call(
        _merge_kernel,
        grid=(n // tm,),
        in_specs=[yspec, yspec, yspec, yspec,
                  pl.BlockSpec((tm, D_MODEL), lambda i: (i, 0)),
                  pl.BlockSpec((1, D_MODEL), lambda i: (0, 0)),
                  pl.BlockSpec((D_MODEL, 4 * D_MODEL), lambda i: (0, 0), pipeline_mode=once),
                  pl.BlockSpec((4, BRANCH_W, D_MODEL), lambda i: (0, 0, 0), pipeline_mode=once),
                  pl.BlockSpec((D_MODEL, D_MODEL), lambda i: (0, 0), pipeline_mode=once)],
        out_specs=pl.BlockSpec((tm, D_MODEL), lambda i: (i, 0)),
        out_shape=jax.ShapeDtypeStruct((n, D_MODEL), F32),
        compiler_params=_cparams(("parallel",)),
    )(*[y.reshape(n, BRANCH_W) for y in ys], x2, norm_w.reshape(1, D_MODEL), w_gate,
      w_branch.astype(BF16), w_out.astype(BF16))


def _ffn_kernel(x_ref, nw_ref, w1_ref, w3_ref, w2_ref, o_ref):
    x = x_ref[...]
    ms = jnp.mean(x * x, axis=-1, keepdims=True)
    h = (x * lax.rsqrt(ms + NORM_EPS) * nw_ref[...]).astype(BF16)
    a = _silu(jnp.dot(h, w1_ref[...], preferred_element_type=F32)) * jnp.dot(h, w3_ref[...], preferred_element_type=F32)
    o_ref[...] = x + jnp.dot(a.astype(BF16), w2_ref[...], preferred_element_type=F32)


def _ffn(x2, norm_w, w1, w3, w2, tm):
    n = x2.shape[0]
    dff = w1.shape[1]
    once = pl.Buffered(1)
    return pl.pallas_call(
        _ffn_kernel,
        grid=(n // tm,),
        in_specs=[pl.BlockSpec((tm, D_MODEL), lambda i: (i, 0)),
                  pl.BlockSpec((1, D_MODEL), lambda i: (0, 0)),
                  pl.BlockSpec((D_MODEL, dff), lambda i: (0, 0), pipeline_mode=once),
                  pl.BlockSpec((D_MODEL, dff), lambda i: (0, 0), pipeline_mode=once),
                  pl.BlockSpec((dff, D_MODEL), lambda i: (0, 0), pipeline_mode=once)],
        out_specs=pl.BlockSpec((tm, D_MODEL), lambda i: (i, 0)),
        out_shape=jax.ShapeDtypeStruct((n, D_MODEL), F32),
        compiler_params=_cparams(("parallel",)),
    )(x2, norm_w.reshape(1, D_MODEL), w1.astype(BF16), w3.astype(BF16), w2.astype(BF16))


SC_CORES = 2
SC_SUBCORES = 16
SC_WINDOW = 128
SC_PIECE = 256
MOE_BLOCK = 512
MOE_FF = 1792
ROUTER_ROWS = 256
HALF = D_MODEL // 2


HIGH16 = -65536


def _pack_bf16_pairs(h):
    bits = lax.bitcast_convert_type(h.astype(BF16).astype(F32), jnp.int32)
    return lax.shift_right_logical(bits[:, :HALF], 16) | (bits[:, HALF:] & HIGH16)


def _unpack_bf16_pairs(pieces):
    lo = [lax.bitcast_convert_type(lax.shift_left(w, 16), F32) for w in pieces]
    hi = [lax.bitcast_convert_type(w & HIGH16, F32) for w in pieces]
    return jnp.concatenate(lo + hi, axis=1).astype(BF16)


def _route_kernel(x_ref, nw_ref, wr_ref, br_ref, hp_ref, tok_ref, cnt_ref, seen_ref, *, tm):
    @pl.when(pl.program_id(0) == 0)
    def _():
        seen_ref[...] = jnp.zeros_like(seen_ref)

    x = x_ref[...]
    ms = jnp.mean(x * x, axis=-1, keepdims=True)
    h = x * lax.rsqrt(ms + NORM_EPS) * nw_ref[...]
    hp = _pack_bf16_pairs(h)
    for q in range(HALF // SC_PIECE):
        hp_ref[q] = hp[:, q * SC_PIECE:(q + 1) * SC_PIECE]
    lane = _iota((tm, LANES), 1)
    logits = jnp.where(lane < N_EXPERTS, _dot_hp(h, wr_ref[...]) + br_ref[...], -jnp.inf)
    m1 = jnp.max(logits, axis=1, keepdims=True)
    i1 = jnp.min(jnp.where(logits == m1, lane, LANES), axis=1, keepdims=True)
    rest = jnp.where(lane == i1, -jnp.inf, logits)
    m2 = jnp.max(rest, axis=1, keepdims=True)
    i2 = jnp.min(jnp.where(rest == m2, lane, LANES), axis=1, keepdims=True)
    e2 = jnp.exp(m2 - m1)
    g1 = 1.0 / (1.0 + e2)
    g2 = e2 / (1.0 + e2)
    self32 = ((lane == i1) | (lane == i2)).astype(F32)
    seen = seen_ref[...]
    tri = (_iota((tm, tm), 1) < _iota((tm, tm), 0)).astype(BF16)
    rank = jnp.dot(tri, self32.astype(BF16), preferred_element_type=F32) + seen
    r1 = jnp.sum(jnp.where(lane == i1, rank, 0.0), axis=1, keepdims=True)
    r2 = jnp.sum(jnp.where(lane == i2, rank, 0.0), axis=1, keepdims=True)
    cols = (i1.astype(F32), i2.astype(F32), r1, r2, g1, g2)
    tok = jnp.zeros((tm, LANES), F32)
    for j, col in enumerate(cols):
        tok = jnp.where(lane == j, col, tok)
    tok_ref[...] = tok
    seen = seen + jnp.sum(self32, axis=0, keepdims=True)
    seen_ref[...] = seen
    cnt_ref[...] = jnp.broadcast_to(seen, (8, LANES))


def _route(x2, norm_w, w_router, b_router, tm):
    n = x2.shape[0]
    wr = jnp.pad(w_router, ((0, 0), (0, LANES - N_EXPERTS)))
    br = jnp.pad(b_router, (0, LANES - N_EXPERTS)).reshape(1, LANES)
    return pl.pallas_call(
        functools.partial(_route_kernel, tm=tm),
        grid=(n // tm,),
        in_specs=[pl.BlockSpec((tm, D_MODEL), lambda i: (i, 0)),
                  pl.BlockSpec((1, D_MODEL), lambda i: (0, 0)),
                  pl.BlockSpec((D_MODEL, LANES), lambda i: (0, 0)),
                  pl.BlockSpec((1, LANES), lambda i: (0, 0))],
        out_specs=[pl.BlockSpec((HALF // SC_PIECE, tm, SC_PIECE), lambda i: (0, i, 0)),
                   pl.BlockSpec((tm, LANES), lambda i: (i, 0)),
                   pl.BlockSpec((8, LANES), lambda i: (0, 0))],
        out_shape=[jax.ShapeDtypeStruct((HALF // SC_PIECE, n, SC_PIECE), jnp.int32),
                   jax.ShapeDtypeStruct((n, LANES), F32),
                   jax.ShapeDtypeStruct((8, LANES), F32)],
        scratch_shapes=[pltpu.VMEM((1, LANES), F32)],
        compiler_params=_cparams(("arbitrary",)),
    )(x2, norm_w.reshape(1, D_MODEL), wr, br)


def _sc_rows_multiple(d):
    return SC_CORES * SC_SUBCORES * SC_WINDOW * SC_PIECE // d


def _sc_gather_rows(table, idx):
    f, v, _ = table.shape
    b = idx.shape[0]
    assert b % _sc_rows_multiple(f * SC_PIECE) == 0
    idx_all = (idx[None, :] + (jnp.arange(f, dtype=jnp.int32) * v)[:, None]).reshape(-1)
    return _sc_gather_pieces(table.reshape(f * v, SC_PIECE), idx_all).reshape(f, b, SC_PIECE)


def _sc_gather_pieces(table, idx):
    bp = idx.shape[0]
    d = table.shape[1]
    window = SC_WINDOW
    idx2 = idx.reshape(1, bp)
    mesh = plsc.VectorSubcoreMesh(core_axis_name="core", subcore_axis_name="subcore")

    @functools.partial(pl.kernel, out_type=jax.ShapeDtypeStruct((bp, d), table.dtype), mesh=mesh)
    def gather(x_hbm, i_hbm, o_hbm):
        def body(i_vmem, o_vmem):
            pltpu.sync_copy(x_hbm.at[i_vmem.at[0]], o_vmem)

        pltpu.emit_pipeline(
            body,
            grid=(bp // window,),
            in_specs=[pl.BlockSpec((1, window), index_map=lambda i: (0, i))],
            out_specs=[pl.BlockSpec((window, d), index_map=lambda i: (i, 0))],
            core_axis_name=("core", "subcore"),
            dimension_semantics=(pltpu.PARALLEL,),
        )(i_hbm, o_hbm)

    return gather(table, idx2)


def _sc_scatter_rows(rows, pos, n_slots):
    f, n0, _ = rows.shape
    n = -(-n0 // _sc_rows_multiple(D_MODEL)) * _sc_rows_multiple(D_MODEL)
    if n != n0:
        rows = jnp.pad(rows, ((0, 0), (0, n - n0), (0, 0)))
        pos = jnp.pad(pos.reshape(2, n0), ((0, 0), (0, n - n0)), constant_values=n_slots - 1)
    nb = n // SC_WINDOW
    assert (f * 2 * nb) % (SC_CORES * SC_SUBCORES) == 0
    idx = (pos.reshape(1, 2 * n) + (jnp.arange(f, dtype=jnp.int32) * n_slots)[:, None]).reshape(1, f * 2 * n)
    mesh = plsc.VectorSubcoreMesh(core_axis_name="core", subcore_axis_name="subcore")

    @functools.partial(pl.kernel, out_type=jax.ShapeDtypeStruct((f * n_slots, SC_PIECE), rows.dtype), mesh=mesh)
    def scatter(x_hbm, i_hbm, o_hbm):
        def body(x_vmem, i_vmem):
            pltpu.sync_copy(x_vmem, o_hbm.at[i_vmem.at[0]])

        pltpu.emit_pipeline(
            body,
            grid=(f * 2 * nb,),
            in_specs=[pl.BlockSpec((SC_WINDOW, SC_PIECE), index_map=lambda i: ((i // (2 * nb)) * nb + i % nb, 0)),
                      pl.BlockSpec((1, SC_WINDOW), index_map=lambda i: (0, i))],
            out_specs=[],
            core_axis_name=("core", "subcore"),
            dimension_semantics=(pltpu.PARALLEL,),
        )(x_hbm, i_hbm)

    return scatter(rows.reshape(f * n, SC_PIECE), idx).reshape(f, n_slots, SC_PIECE)


def _experts_kernel(be_ref, nb_ref, nv_ref, xs_ref, w1_ref, w3_ref, w2_ref, o_ref):
    g, c = pl.program_id(0), pl.program_id(1)

    @pl.when(c == 0)
    def _():
        o_ref[...] = jnp.zeros_like(o_ref)

    @pl.when(g < nb_ref[0])
    def _():
        live = _iota((xs_ref.shape[1], 1), 0) < nv_ref[g]
        xb = _unpack_bf16_pairs([jnp.where(live, xs_ref[q], 0) for q in range(HALF // SC_PIECE)])
        a = (_silu(jnp.dot(xb, w1_ref[...], preferred_element_type=F32))
             * jnp.dot(xb, w3_ref[...], preferred_element_type=F32))
        y = jnp.dot(a.astype(BF16), w2_ref[...], preferred_element_type=F32)
        for q in range(D_MODEL // SC_PIECE):
            o_ref[q] += y[:, q * SC_PIECE:(q + 1) * SC_PIECE]


def _experts(xs, block_expert, n_blocks, block_rows, w1, w3, w2, blk, tf):
    n_slots = xs.shape[1]
    dff = w1.shape[2]
    grid_spec = pltpu.PrefetchScalarGridSpec(
        num_scalar_prefetch=3,
        grid=(n_slots // blk, dff // tf),
        in_specs=[pl.BlockSpec((HALF // SC_PIECE, blk, SC_PIECE), lambda g, c, be, nb, nv: (0, g, 0)),
                  pl.BlockSpec((None, D_MODEL, tf), lambda g, c, be, nb, nv: (be[g], 0, c)),
                  pl.BlockSpec((None, D_MODEL, tf), lambda g, c, be, nb, nv: (be[g], 0, c)),
                  pl.BlockSpec((None, tf, D_MODEL), lambda g, c, be, nb, nv: (be[g], c, 0))],
        out_specs=pl.BlockSpec((D_MODEL // SC_PIECE, blk, SC_PIECE), lambda g, c, be, nb, nv: (0, g, 0)),
    )
    return pl.pallas_call(
        _experts_kernel,
        grid_spec=grid_spec,
        out_shape=jax.ShapeDtypeStruct((D_MODEL // SC_PIECE, n_slots, SC_PIECE), F32),
        compiler_params=_cparams(("parallel", "arbitrary")),
    )(block_expert, n_blocks, block_rows, xs, w1, w3, w2)


def _combine_kernel(x_ref, y1_ref, y2_ref, tok_ref, nw_ref, o_ref):
    tok = tok_ref[...]
    rows = lambda y_ref: jnp.concatenate([y_ref[q] for q in range(D_MODEL // SC_PIECE)], axis=1)
    x = x_ref[...] + tok[:, 4:5] * rows(y1_ref) + tok[:, 5:6] * rows(y2_ref)
    ms = jnp.mean(x * x, axis=-1, keepdims=True)
    o_ref[...] = x * lax.rsqrt(ms + NORM_EPS) * nw_ref[...]


def _combine(x2, ys2, tok, norm_w, tm):
    n = x2.shape[0]
    spec = pl.BlockSpec((tm, D_MODEL), lambda i: (i, 0))
    yspec = lambda off: pl.BlockSpec((D_MODEL // SC_PIECE, tm, SC_PIECE), lambda i: (0, i + off, 0))
    return pl.pallas_call(
        _combine_kernel,
        grid=(n // tm,),
        in_specs=[spec, yspec(0), yspec(n // tm),
                  pl.BlockSpec((tm, LANES), lambda i: (i, 0)), pl.BlockSpec((1, D_MODEL), lambda i: (0, 0))],
        out_specs=spec,
        out_shape=jax.ShapeDtypeStruct((n, D_MODEL), F32),
        compiler_params=_cparams(("parallel",)),
    )(x2, ys2, ys2, tok, norm_w.reshape(1, D_MODEL))


def _moe_final(x2, norm_w, w_router, b_router, w1, w3, w2, final_norm_w):
    n = x2.shape[0]
    blk = min(MOE_BLOCK, n)
    hp, tok, cnt = _route(x2, norm_w, w_router, b_router, ROUTER_ROWS)
    counts = cnt[0, :N_EXPERTS].astype(jnp.int32)
    padded = (counts + blk - 1) // blk * blk
    ends = jnp.cumsum(padded)
    base = ends - padded
    round_up = lambda a, m: -(-a // m) * m
    n_slots = round_up((-(-2 * n // blk) + N_EXPERTS + 1) * blk, _sc_rows_multiple(HALF))
    n_blocks = n_slots // blk
    i12 = tok[:, 0:2].astype(jnp.int32)
    pos = (base[i12] + tok[:, 2:4].astype(jnp.int32)).T.reshape(-1)
    block_start = jnp.arange(n_blocks, dtype=jnp.int32) * blk
    block_expert = jnp.minimum(jnp.sum(block_start[:, None] >= ends[None, :], axis=1), N_EXPERTS - 1).astype(jnp.int32)
    block_rows = jnp.clip((base + counts)[block_expert] - block_start, 0, blk).astype(jnp.int32)
    xs = _sc_scatter_rows(hp, pos, n_slots)
    ys = _experts(xs, block_expert, (ends[-1:] // blk).astype(jnp.int32), block_rows,
                  w1.astype(BF16), w3.astype(BF16), w2.astype(BF16), blk, MOE_FF)
    pos = jnp.pad(pos, (0, round_up(2 * n, _sc_rows_multiple(D_MODEL)) - 2 * n))
    ys2 = _sc_gather_rows(ys, pos)
    return _combine(x2, ys2, tok, final_norm_w, min(1024, n))


def _permute_w_in(w):
    s5, rw, sg = w[:, 0:256], w[:, 256:1280], w[:, 1280:1792]
    gd, ab, gates = w[:, 1792:2816], w[:, 2816:2824], w[:, 2824:6920]
    pad = jnp.zeros((D_MODEL, PROJ_COLS - COL_AB - 8), w.dtype)
    return jnp.concatenate([rw, gd, sg, s5, ab, pad], axis=1).astype(BF16), gates.astype(BF16)


def _run_trunk(x, s5_h, rw_s, rw_shift, gd_s, gd_conv, p, w_in_perm):
    bsz, t, _ = x.shape
    n = bsz * t
    tm = min(1024, n)
    l = min(MIX_CHUNK, t)
    tc_s5 = min(256, t)
    new = ([], [], [], [], [], [])
    x2 = x.reshape(n, D_MODEL)
    for layer in range(2):
        g = lambda name: p[name][layer]
        w_mix, w_gate = w_in_perm[layer]
        proj = _norm_proj(x2, g('norm1_w'), w_mix, tm, 1024)
        p3 = proj.reshape(bsz, t, PROJ_COLS)
        y_a, s5_new = _s5_mixer(p3, s5_h[layer], g('s5_lam_re'), g('s5_lam_im'), g('s5_log_dt'), g('s5_b'),
                                g('s5_c'), g('s5_d'), g('s5_w_glu'), g('s5_b_glu'), tc_s5)
        y_b, rw_new = _rwkv_mixer(p3, rw_shift[layer], rw_s[layer], g('rw_mu'), g('rw_w0'), g('rw_w2'),
                                  g('rw_a0'), g('rw_a2'), g('rw_g2'), g('rw_k_k'), g('rw_k_a'), g('rw_r_k'),
                                  g('rw_ln_w'), g('rw_ln_b'), l)
        y_c, sg_v = _sgu_mixer(p3, g('sg_ln_w'), g('sg_ln_b'), g('sg_w_s'), g('sg_b_s'))
        y_d, gd_new = _gdn_mixer(p3, gd_conv[layer], gd_s[layer], g('gd_conv_w'), g('gd_a_log'),
                                 g('gd_dt_bias'), g('gd_norm_w'), l)
        shift_new = p3[:, t - 1, COL_RW:COL_RW + RW_COLS]
        conv_new = p3[:, t - (GD_CONV - 1):, COL_GD:COL_GD + GD_QKV]
        x2 = _merge((y_a, y_b, y_c, y_d), x2, g('norm1_w'), w_gate, g('w_branch'), g('w_out'), min(512, n))
        j = layer // 2
        if layer % 2 == 0:
            x2 = _ffn(x2, g('norm2_w'), p['ffn_w1'][j], p['ffn_w3'][j], p['ffn_w2'][j], min(512, n))
        else:
            y = _moe_final(x2, g('norm2_w'), p['moe_router'][j], p['moe_router_b'][j],
                           p['moe_w1'][j], p['moe_w3'][j], p['moe_w2'][j], p['final_norm_w'])
        for lst, s in zip(new, (s5_new, rw_new, shift_new, gd_new, conv_new, sg_v)):
            lst.append(s)
    return y.reshape(bsz, t, D_MODEL), [jnp.stack(lst) for lst in new]


def kernel(x_prompt, x_sample, state_s5, state_rwkv, state_rwkv_shift, state_gdn, state_gdn_conv, norm1_w, w_in, s5_lam_re, s5_lam_im, s5_log_dt, s5_b, s5_c, s5_d, s5_w_glu, s5_b_glu, rw_mu, rw_w0, rw_w2, rw_a0, rw_a2, rw_g2, rw_k_k, rw_k_a, rw_r_k, rw_ln_w, rw_ln_b, sg_ln_w, sg_ln_b, sg_w_s, sg_b_s, gd_conv_w, gd_a_log, gd_dt_bias, gd_norm_w, w_branch, w_out, norm2_w, ffn_w1, ffn_w3, ffn_w2, moe_router, moe_router_b, moe_w1, moe_w3, moe_w2, final_norm_w):
    p = {
        'norm1_w': norm1_w, 's5_lam_re': s5_lam_re, 's5_lam_im': s5_lam_im, 's5_log_dt': s5_log_dt,
        's5_b': s5_b, 's5_c': s5_c, 's5_d': s5_d, 's5_w_glu': s5_w_glu, 's5_b_glu': s5_b_glu,
        'rw_mu': rw_mu, 'rw_w0': rw_w0, 'rw_w2': rw_w2, 'rw_a0': rw_a0, 'rw_a2': rw_a2, 'rw_g2': rw_g2,
        'rw_k_k': rw_k_k, 'rw_k_a': rw_k_a, 'rw_r_k': rw_r_k, 'rw_ln_w': rw_ln_w, 'rw_ln_b': rw_ln_b,
        'sg_ln_w': sg_ln_w, 'sg_ln_b': sg_ln_b, 'sg_w_s': sg_w_s, 'sg_b_s': sg_b_s,
        'gd_conv_w': gd_conv_w, 'gd_a_log': gd_a_log, 'gd_dt_bias': gd_dt_bias, 'gd_norm_w': gd_norm_w,
        'w_branch': w_branch, 'w_out': w_out, 'norm2_w': norm2_w,
        'ffn_w1': ffn_w1, 'ffn_w3': ffn_w3, 'ffn_w2': ffn_w2,
        'moe_router': moe_router, 'moe_router_b': moe_router_b, 'moe_w1': moe_w1, 'moe_w3': moe_w3, 'moe_w2': moe_w2,
        'final_norm_w': final_norm_w,
    }
    w_in_perm = [_permute_w_in(w_in[layer]) for layer in range(2)]
    bp, dt = x_prompt.shape[0], x_prompt.dtype
    depth = w_in.shape[0]
    y_prompt, (s5_p, rw_p, rwsh_p, gd_p, gdc_p, _) = _run_trunk(
        x_prompt,
        jnp.zeros((depth, bp, S5_GROUPS, S5_STATE, 2), dt),
        jnp.zeros((depth, bp, HEADS, HEAD_W, HEAD_W), dt),
        jnp.zeros((depth, bp, RW_COLS), dt),
        jnp.zeros((depth, bp, HEADS, HEAD_W, HEAD_W), dt),
        jnp.zeros((depth, bp, GD_CONV - 1, GD_QKV), dt),
        p, w_in_perm)
    y_sample, (s5_s, rw_s, rwsh_s, gd_s, gdc_s, sgv_s) = _run_trunk(
        x_sample, state_s5, state_rwkv, state_rwkv_shift, state_gdn, state_gdn_conv, p, w_in_perm)
    return (y_prompt, y_sample, s5_p, rw_p, rwsh_p, gd_p, gdc_p, s5_s, rw_s, rwsh_s, gd_s, gdc_s, sgv_s)
```

```python
import functools

import jax
import jax.numpy as jnp
from jax import lax
from jax.experimental import pallas as pl
from jax.experimental.pallas import tpu as pltpu
from jax.experimental.pallas import tpu_sc as plsc

F32 = jnp.float32
BF16 = jnp.bfloat16

D_MODEL = 1024
BRANCH_W = 256
HEADS = 4
HEAD_W = 64
S5_GROUPS = 16
S5_GROUP = 16
S5_STATE = 64
S5_W = S5_GROUPS * S5_STATE
S5_ROWS = 8
SG_CHUNK = 128
SG_ROWS = 512
GD_CONV = 4
GD_QKV = 3 * BRANCH_W
RW_COLS = 1024
RW_EPS = 64e-5
NORM_EPS = 1e-6
N_EXPERTS = 8
LANES = 128

COL_RW = 0
COL_GD = 1024
COL_SG = 2048
COL_S5 = 2560
COL_AB = 2816
PROJ_COLS = 3072

VMEM_LIMIT = 48 * 1024 * 1024


def _cparams(sem):
    return pltpu.CompilerParams(dimension_semantics=sem, vmem_limit_bytes=VMEM_LIMIT)


def _dot(a, b):
    return jnp.dot(a.astype(BF16), b.astype(BF16), preferred_element_type=F32)


def _dot_nt(a, b):
    return lax.dot_general(a.astype(BF16), b.astype(BF16), (((1,), (1,)), ((), ())),
                           preferred_element_type=F32)


def _split3(a):
    hi = a.astype(BF16)
    r1 = a - hi.astype(F32)
    mid = r1.astype(BF16)
    lo = (r1 - mid.astype(F32)).astype(BF16)
    return hi, mid, lo


def _dot3_left(b_exact, a):
    hi, mid, lo = _split3(a)
    b = b_exact.astype(BF16)
    return (jnp.dot(b, hi, preferred_element_type=F32) + jnp.dot(b, mid, preferred_element_type=F32)
            + jnp.dot(b, lo, preferred_element_type=F32))


def _dot_hp(a, b):
    a0, a1, _ = _split3(a)
    b0, b1, _ = _split3(b)
    d = lambda x, y: jnp.dot(x, y, preferred_element_type=F32)
    return d(a0, b0) + (d(a0, b1) + d(a1, b0))


def _iota(shape, axis):
    return lax.broadcasted_iota(jnp.int32, shape, axis)


def _head_ones():
    r = _iota((BRANCH_W, BRANCH_W), 0) // HEAD_W
    c = _iota((BRANCH_W, BRANCH_W), 1) // HEAD_W
    return (r == c).astype(BF16)


def _head_mask(l):
    r = _iota((HEADS * l, BRANCH_W), 0) // l
    c = _iota((HEADS * l, BRANCH_W), 1) // HEAD_W
    return r == c


def _expand(x, mask):
    return jnp.where(mask, jnp.concatenate([x] * HEADS, axis=0), 0.0)


def _tri_masks(l):
    i = _iota((l, HEADS * l), 0)
    j = _iota((l, HEADS * l), 1) % l
    return j < i, j <= i


def _same_head(l):
    n = HEADS * l
    return (_iota((n, n), 0) // l) == (_iota((n, n), 1) // l)


def _expand_sq(x, same):
    return jnp.where(same, jnp.concatenate([x] * HEADS, axis=0), 0.0)


def _unit_lower_inverse(a_strict, l):
    i = _iota((l, HEADS * l), 0)
    j = _iota((l, HEADS * l), 1) % l
    eye = (i == j).astype(F32)
    same = _same_head(l)
    p = [-a for a in a_strict]
    t = [eye + x for x in p]
    k = 2
    while k < l:
        p = [_dot(x, _expand_sq(x, same)) for x in p]
        t = [y + _dot(y, _expand_sq(x, same)) for y, x in zip(t, p)]
        k *= 2
    return t


def _dot2(a, b_exact):
    hi = a.astype(BF16)
    lo = (a - hi.astype(F32)).astype(BF16)
    b = b_exact.astype(BF16)
    return jnp.dot(hi, b, preferred_element_type=F32) + jnp.dot(lo, b, preferred_element_type=F32)


def _cumsum_rows(x, l, nb=1):
    n = nb * l
    i, j = _iota((n, n), 0), _iota((n, n), 1)
    tri = ((j <= i) & ((i // l) == (j // l))).astype(BF16)
    return _dot3_left(tri, x)


def _softplus(x):
    return jnp.maximum(x, 0.0) + jnp.log(1.0 + jnp.exp(-jnp.abs(x)))


def _silu(x):
    return x * jax.nn.sigmoid(x)


def _norm_cast_kernel(x_ref, nw_ref, o_ref):
    x = x_ref[...]
    ms = jnp.mean(x * x, axis=-1, keepdims=True)
    o_ref[...] = (x * lax.rsqrt(ms + NORM_EPS) * nw_ref[...]).astype(BF16)


def _norm_cast(x2, norm_w, tm):
    n = x2.shape[0]
    spec = pl.BlockSpec((tm, D_MODEL), lambda i: (i, 0))
    return pl.pallas_call(
        _norm_cast_kernel,
        grid=(n // tm,),
        in_specs=[spec, pl.BlockSpec((1, D_MODEL), lambda i: (0, 0))],
        out_specs=spec,
        out_shape=jax.ShapeDtypeStruct((n, D_MODEL), BF16),
        compiler_params=_cparams(("parallel",)),
    )(x2, norm_w.reshape(1, D_MODEL))


def _s5_kernel(hn_ref, wz_ref, h0_ref, wb_ref, wc_ref, a2k_ref, apow_ref, d_ref, wg_ref, bg_ref,
               y_ref, hl_ref, hs_ref, hb_ref, *, tc):
    c = pl.program_id(1)

    @pl.when(c == 0)
    def _():
        hs_ref[...] = h0_ref[...]

    u = jnp.dot(hn_ref[...], wz_ref[...], preferred_element_type=F32)
    x = _dot(u, wb_ref[...])
    xr, xi = x[:, :S5_W], x[:, S5_W:]
    ng = tc // S5_ROWS
    xr = xr.reshape(ng, S5_ROWS, S5_W)
    xi = xi.reshape(ng, S5_ROWS, S5_W)
    row = _iota((S5_ROWS, 1), 0)
    k, d = 0, 1
    while d < S5_ROWS:
        m = row >= d
        ar = jnp.where(m, a2k_ref[k:k + 1, :S5_W], 0.0)
        ai = jnp.where(m, a2k_ref[k:k + 1, S5_W:], 0.0)
        sr = pltpu.roll(xr, d, axis=1)
        si = pltpu.roll(xi, d, axis=1)
        xr, xi = xr + (ar * sr - ai * si), xi + (ar * si + ai * sr)
        k, d = k + 1, d * 2
    hb_ref[:, :S5_W] = xr.reshape(tc, S5_W)
    hb_ref[:, S5_W:] = xi.reshape(tc, S5_W)
    pr, pi_ = apow_ref[:, :S5_W], apow_ref[:, S5_W:]

    def group(gi, carry):
        cr, ci = carry
        rows = pl.ds(pl.multiple_of(gi * S5_ROWS, S5_ROWS), S5_ROWS)
        hr = hb_ref[rows, :S5_W] + (pr * cr - pi_ * ci)
        hi = hb_ref[rows, S5_W:] + (pr * ci + pi_ * cr)
        hb_ref[rows, :S5_W] = hr
        hb_ref[rows, S5_W:] = hi
        return hr[S5_ROWS - 1:S5_ROWS], hi[S5_ROWS - 1:S5_ROWS]

    cr, ci = lax.fori_loop(0, tc // S5_ROWS, group, (hs_ref[:, :S5_W], hs_ref[:, S5_W:]), unroll=4)
    hs_ref[:, :S5_W] = cr
    hs_ref[:, S5_W:] = ci
    y = _dot(hb_ref[:, :S5_W], wc_ref[:S5_W]) + _dot(hb_ref[:, S5_W:], wc_ref[S5_W:])
    y = jax.nn.gelu(y + d_ref[...] * u)
    y = y * jax.nn.sigmoid(_dot(y, wg_ref[...]) + bg_ref[...])
    y_ref[...] = y.astype(y_ref.dtype)

    @pl.when(c == pl.num_programs(1) - 1)
    def _():
        hl_ref[...] = hs_ref[...]


def _s5_tables(lam_re, lam_im, log_dt, b_c, c_c, tc):
    dt = jnp.exp(log_dt)[:, None]
    mag = jnp.exp(lam_re * dt)
    ab_re, ab_im = mag * jnp.cos(lam_im * dt), mag * jnp.sin(lam_im * dt)
    den = lam_re * lam_re + lam_im * lam_im
    nr = ab_re - 1.0
    cf_re = (nr * lam_re + ab_im * lam_im) / den
    cf_im = (ab_im * lam_re - nr * lam_im) / den
    br, bi = b_c[..., 0], b_c[..., 1]
    bb_re = cf_re[..., None] * br - cf_im[..., None] * bi
    bb_im = cf_re[..., None] * bi + cf_im[..., None] * br
    eye = jnp.eye(S5_GROUPS, dtype=F32)
    bd_in = lambda m: jnp.einsum('gph,gk->ghkp', m, eye).reshape(BRANCH_W, S5_W)
    wb = jnp.concatenate([bd_in(bb_re), bd_in(bb_im)], axis=1)
    cr, ci = c_c[..., 0], c_c[..., 1]
    bd_out = lambda m: jnp.einsum('ghp,gk->gpkh', m, eye).reshape(S5_W, BRANCH_W)
    wc = jnp.concatenate([bd_out(cr), -bd_out(ci)], axis=0)
    pr, pi_ = ab_re.reshape(1, S5_W), ab_im.reshape(1, S5_W)
    lv_r, lv_i = [], []
    tr, ti = pr, pi_
    d = 1
    while d < tc:
        lv_r.append(pr)
        lv_i.append(pi_)
        tr, ti = (jnp.concatenate([tr, tr * pr - ti * pi_], axis=0),
                  jnp.concatenate([ti, tr * pi_ + ti * pr], axis=0))
        pr, pi_ = pr * pr - pi_ * pi_, 2.0 * pr * pi_
        d *= 2
    n_lv = len(lv_r)
    pad = (-n_lv) % 8
    a2k = jnp.concatenate([jnp.concatenate(lv_r, axis=0), jnp.concatenate(lv_i, axis=0)], axis=1)
    a2k = jnp.pad(a2k, ((0, pad), (0, 0)))
    apow = jnp.concatenate([tr, ti], axis=1)
    return wb.astype(BF16), wc.astype(BF16), a2k, apow


def _w_cols(col, width):
    return pl.BlockSpec((D_MODEL, width), lambda b, c: (0, col // width))


def _s5_mixer(hn3, w_mix, h0, lam_re, lam_im, log_dt, b_c, c_c, d_skip, w_glu, b_glu, tc):
    bsz, t, _ = hn3.shape
    wb, wc, a2k, apow = _s5_tables(lam_re, lam_im, log_dt, b_c, c_c, S5_ROWS)
    h0f = jnp.concatenate([h0[..., 0].reshape(bsz, 1, S5_W), h0[..., 1].reshape(bsz, 1, S5_W)], axis=-1)
    full = lambda a: pl.BlockSpec(a.shape, lambda b, c: (0,) * a.ndim)
    d2, bg2, wg = d_skip.reshape(1, BRANCH_W), b_glu.reshape(1, BRANCH_W), w_glu.astype(BF16)
    y, hl = pl.pallas_call(
        functools.partial(_s5_kernel, tc=tc),
        grid=(bsz, t // tc),
        in_specs=[pl.BlockSpec((None, tc, D_MODEL), lambda b, c: (b, c, 0)), _w_cols(COL_S5, BRANCH_W),
                  pl.BlockSpec((None, 1, 2 * S5_W), lambda b, c: (b, 0, 0)),
                  full(wb), full(wc), full(a2k), full(apow), full(d2), full(wg), full(bg2)],
        out_specs=[pl.BlockSpec((None, tc, BRANCH_W), lambda b, c: (b, c, 0)),
                   pl.BlockSpec((None, 1, 2 * S5_W), lambda b, c: (b, 0, 0))],
        out_shape=[jax.ShapeDtypeStruct((bsz, t, BRANCH_W), BF16),
                   jax.ShapeDtypeStruct((bsz, 1, 2 * S5_W), F32)],
        scratch_shapes=[pltpu.VMEM((1, 2 * S5_W), F32), pltpu.VMEM((tc, 2 * S5_W), F32)],
        compiler_params=_cparams(("parallel", "arbitrary")),
    )(hn3, w_mix, h0f, wb, wc, a2k, apow, d2, wg, bg2)
    h_last = jnp.stack([hl[:, 0, :S5_W].reshape(bsz, S5_GROUPS, S5_STATE),
                        hl[:, 0, S5_W:].reshape(bsz, S5_GROUPS, S5_STATE)], axis=-1)
    return y, h_last


def _rwkv_kernel(hn_ref, wz_ref, sh0_ref, s0_ref, mu_ref, w0_ref, w2_ref, a0_ref, a2_ref, g2_ref,
                 kk_ref, ka_ref, rk_ref, lnw_ref, lnb_ref,
                 y_ref, sl_ref, sho_ref, st_ref, zp_ref, *, l, bb, nt):
    c = pl.program_id(1)

    @pl.when(c == 0)
    def _():
        st_ref[...] = s0_ref[...]
        zp_ref[...] = sh0_ref[...]

    tl = nt * l
    z_all = jnp.dot(jnp.concatenate([hn_ref[b] for b in range(bb)], axis=0), wz_ref[...],
                    preferred_element_type=F32)
    row = _iota((tl, 1), 0)
    zms = []
    for b in range(bb):
        z = z_all[b * tl:(b + 1) * tl]
        prev = jnp.where(row == 0, zp_ref[b], pltpu.roll(z, 1, axis=0))
        zp_ref[b] = z[tl - 1:tl]
        zms.append(z + (prev - z) * mu_ref[...])
    zm = jnp.concatenate(zms, axis=0)
    r, k, v = zm[:, 0:256], zm[:, 256:512], zm[:, 512:768]
    lo = zm[:, 768:896]
    g_lo = zm[:, 896:1024]
    w_log = -_softplus(-(w0_ref[...] + _dot(jnp.tanh(lo), w2_ref[...]))) - 0.5
    lw = -jnp.exp(w_log)
    a = jax.nn.sigmoid(a0_ref[...] + _dot(lo, a2_ref[...]))
    g = _dot(jax.nn.sigmoid(g_lo), g2_ref[...])
    ones_h = _head_ones()
    kk = k * kk_ref[...]
    kk = kk * lax.rsqrt(_dot2(kk * kk, ones_h) + NORM_EPS)
    k = k * (1.0 + (a - 1.0) * ka_ref[...])
    kka = kk * a

    cum = _cumsum_rows(lw, l, bb * nt)
    p_incl = jnp.exp(cum)
    p_inv = jnp.exp(-cum)
    kt = kk * jnp.exp(cum - lw)
    rt = r * p_incl
    kh = k * p_inv
    ah = kka * p_inv

    hm = _head_mask(l)
    strict, incl = _tri_masks(l)
    chunks = [(b, j) for j in range(nt) for b in range(bb)]
    rows = {s: slice((s[0] * nt + s[1]) * l, (s[0] * nt + s[1] + 1) * l) for s in chunks}
    ex = lambda x: {s: _expand(x[rows[s]], hm) for s in chunks}
    kh_e, ah_e, v_e = ex(kh), ex(ah), ex(v)
    a_aa = [jnp.where(strict, _dot_nt(kt[rows[s]], ah_e[s]), 0.0) for s in chunks]
    a_ak = {s: jnp.where(strict, _dot_nt(kt[rows[s]], kh_e[s]), 0.0) for s in chunks}
    b_ra = {s: jnp.where(incl, _dot_nt(rt[rows[s]], ah_e[s]), 0.0) for s in chunks}
    b_rk = {s: jnp.where(incl, _dot_nt(rt[rows[s]], kh_e[s]), 0.0) for s in chunks}
    t_inv = dict(zip(chunks, _unit_lower_inverse(a_aa, l)))
    av = {s: _dot(a_ak[s], v_e[s]) for s in chunks}
    bv = {s: _dot(b_rk[s], v_e[s]) for s in chunks}
    to_end = {s: jnp.exp(cum[rows[s]][l - 1:l] - cum[rows[s]]) for s in chunks}
    head_blk = _same_head(HEAD_W)
    st = [st_ref[b] for b in range(bb)]
    yb = {}
    for j in range(nt):
        now = [(b, j) for b in range(bb)]
        rhs = {s: _dot_nt(kt[rows[s]], st[s[0]]) + av[s] for s in now}
        ys = {s: _dot_nt(rt[rows[s]], st[s[0]]) + bv[s] for s in now}
        u = {s: _dot(t_inv[s], _expand(rhs[s], hm)) for s in now}
        for s in now:
            yb[s] = ys[s] - _dot(b_ra[s], _expand(u[s], hm))
        lhs_t = {s: jnp.concatenate([v[rows[s]], -u[s]], axis=0).T for s in now}
        rhs_k = {s: jnp.concatenate([k[rows[s]] * to_end[s], kka[rows[s]] * to_end[s]], axis=0) for s in now}
        for s in now:
            p_last = p_incl[rows[s]][l - 1:l]
            st[s[0]] = st[s[0]] * p_last + jnp.where(head_blk, _dot(lhs_t[s], rhs_k[s]), 0.0)
    for b in range(bb):
        st_ref[b] = st[b]
    y = jnp.concatenate([yb[(b, j)] for b in range(bb) for j in range(nt)], axis=0)

    inv_w = 1.0 / HEAD_W
    mean = _dot2(y, ones_h) * inv_w
    yc = y - mean
    var = _dot2(yc * yc, ones_h) * inv_w
    y = yc * lax.rsqrt(var + RW_EPS) * lnw_ref[...] + lnb_ref[...]
    bonus = _dot2(r * k * rk_ref[...], ones_h) * v
    y = ((y + bonus) * g).astype(y_ref.dtype)
    for b in range(bb):
        y_ref[b] = y[b * tl:(b + 1) * tl]

    @pl.when(c == pl.num_programs(1) - 1)
    def _():
        sl_ref[...] = st_ref[...]
        sho_ref[...] = zp_ref[...]


MIX_CHUNK = 64
STACK_STREAMS = 4
STACK_ROWS = 512


def _streams_per_step(bsz, t, l):
    bb = min(bsz, STACK_STREAMS)
    while bsz % bb:
        bb -= 1
    nt = max(1, min(t, STACK_ROWS // bb) // l)
    while (t // l) % nt:
        nt -= 1
    return bb, nt


def _block_diag_heads(s):
    bsz = s.shape[0]
    eye = jnp.eye(HEADS, dtype=s.dtype)
    return jnp.einsum('bhij,hg->bhigj', s, eye).reshape(bsz, BRANCH_W, BRANCH_W)


def _diag_blocks(s):
    bsz = s.shape[0]
    s5 = s.reshape(bsz, HEADS, HEAD_W, HEADS, HEAD_W)
    return jnp.stack([s5[:, h, :, h, :] for h in range(HEADS)], axis=1)


def _rwkv_mixer(hn3, w_mix, shift0, s0, mu, w0, w2, a0, a2, g2, k_k, k_a, r_k, ln_w, ln_b, l):
    bsz, t, _ = hn3.shape
    row = lambda a: a.reshape(1, -1)
    w2p = jnp.concatenate([w2, jnp.zeros_like(w2)], axis=0).astype(BF16)
    a2p = jnp.concatenate([jnp.zeros_like(a2), a2], axis=0).astype(BF16)
    args = (shift0.reshape(bsz, 1, RW_COLS), _block_diag_heads(s0), row(mu), row(w0), w2p, row(a0), a2p,
            g2.astype(BF16), row(k_k), row(k_a), row(r_k), row(ln_w), row(ln_b))
    full = lambda a: pl.BlockSpec(a.shape, lambda b, c: (0,) * a.ndim)
    bb, nt = _streams_per_step(bsz, t, l)
    y, sl, sho = pl.pallas_call(
        functools.partial(_rwkv_kernel, l=l, bb=bb, nt=nt),
        grid=(bsz // bb, t // (nt * l)),
        in_specs=[pl.BlockSpec((bb, nt * l, D_MODEL), lambda b, c: (b, c, 0)), _w_cols(COL_RW, RW_COLS),
                  pl.BlockSpec((bb, 1, RW_COLS), lambda b, c: (b, 0, 0)),
                  pl.BlockSpec((bb, BRANCH_W, BRANCH_W), lambda b, c: (b, 0, 0))]
                 + [full(a) for a in args[2:]],
        out_specs=[pl.BlockSpec((bb, nt * l, BRANCH_W), lambda b, c: (b, c, 0)),
                   pl.BlockSpec((bb, BRANCH_W, BRANCH_W), lambda b, c: (b, 0, 0)),
                   pl.BlockSpec((bb, 1, RW_COLS), lambda b, c: (b, 0, 0))],
        out_shape=[jax.ShapeDtypeStruct((bsz, t, BRANCH_W), BF16),
                   jax.ShapeDtypeStruct((bsz, BRANCH_W, BRANCH_W), F32),
                   jax.ShapeDtypeStruct((bsz, 1, RW_COLS), F32)],
        scratch_shapes=[pltpu.VMEM((bb, BRANCH_W, BRANCH_W), F32), pltpu.VMEM((bb, 1, RW_COLS), F32)],
        compiler_params=_cparams(("parallel", "arbitrary")),
    )(hn3, w_mix, *args)
    return y, _diag_blocks(sl), sho[:, 0]


def _sgu_kernel(hn_ref, wz_ref, lnw_ref, lnb_ref, wm_ref, bias_ref, o_ref, v_ref, *, l, nc):
    z = jnp.dot(hn_ref[...], wz_ref[...], preferred_element_type=F32)
    zg = jax.nn.gelu(z)
    u, v = zg[:, :BRANCH_W], zg[:, BRANCH_W:]
    mean = jnp.mean(v, axis=-1, keepdims=True)
    vc = v - mean
    var = jnp.mean(vc * vc, axis=-1, keepdims=True)
    v = vc * lax.rsqrt(var + NORM_EPS) * lnw_ref[...] + lnb_ref[...]
    v_ref[...] = v
    hm = _head_mask(l)
    wm = wm_ref[...]
    for i in range(nc):
        rows = slice(i * l, (i + 1) * l)
        mixed = bias_ref[...] + _dot(wm, _expand(v[rows], hm))
        o_ref[rows, :] = (u[rows] * mixed).astype(o_ref.dtype)


def _sgu_mixer(hn3, w_mix, ln_w, ln_b, w_s, b_s):
    bsz, t, _ = hn3.shape
    l = min(SG_CHUNK, t)
    nc = max(1, min(SG_ROWS, t) // l)
    tril = jnp.tril(jnp.ones((l, l), F32))
    wm = jnp.transpose(w_s[:, :l, :l] * tril, (1, 0, 2)).reshape(l, HEADS * l).astype(BF16)
    bias = jnp.repeat(jnp.transpose(b_s[:, :l]), HEAD_W, axis=1)
    row = lambda a: a.reshape(1, -1)
    full = lambda a: pl.BlockSpec(a.shape, lambda b, c: (0,) * a.ndim)
    args = (row(ln_w), row(ln_b), wm, bias)
    return pl.pallas_call(
        functools.partial(_sgu_kernel, l=l, nc=nc),
        grid=(bsz, t // (nc * l)),
        in_specs=[pl.BlockSpec((None, nc * l, D_MODEL), lambda b, c: (b, c, 0)), _w_cols(COL_SG, 2 * BRANCH_W)]
                 + [full(a) for a in args],
        out_specs=[pl.BlockSpec((None, nc * l, BRANCH_W), lambda b, c: (b, c, 0)),
                   pl.BlockSpec((None, nc * l, BRANCH_W), lambda b, c: (b, c, 0))],
        out_shape=[jax.ShapeDtypeStruct((bsz, t, BRANCH_W), BF16),
                   jax.ShapeDtypeStruct((bsz, t, BRANCH_W), F32)],
        compiler_params=_cparams(("parallel", "parallel")),
    )(hn3, w_mix, *args)


def _gdn_kernel(hn_ref, wz_ref, wab_ref, cv0_ref, s0_ref, cw_ref, alog_ref, dtb_ref, nw_ref,
                y_ref, sl_ref, cvo_ref, st_ref, cv_ref, *, l, bb, nt):
    c = pl.program_id(1)

    @pl.when(c == 0)
    def _():
        st_ref[...] = s0_ref[...]
        cv_ref[...] = cv0_ref[...]

    tl = nt * l
    hn = jnp.concatenate([hn_ref[b] for b in range(bb)], axis=0)
    z_all = jnp.dot(hn, wz_ref[...], preferred_element_type=F32)
    ab = jnp.dot(hn, wab_ref[...], preferred_element_type=F32)
    row8 = _iota((8, 1), 0)
    convs, gates = [], []
    for b in range(bb):
        z = z_all[b * tl:(b + 1) * tl]
        qkv = z[:, :GD_QKV]
        gates.append(z[:, GD_QKV:])
        carry = cv_ref[b]
        cv_ref[b] = qkv[tl - 8:tl]
        conv = qkv * cw_ref[GD_CONV - 1:GD_CONV]
        for j in range(1, GD_CONV):
            sh = pltpu.roll(qkv, j, axis=0)
            top = jnp.where(row8 < j, pltpu.roll(carry, j, axis=0), sh[:8])
            sh = jnp.concatenate([top, sh[8:]], axis=0) if tl > 8 else top
            conv = conv + sh * cw_ref[GD_CONV - 1 - j:GD_CONV - j]
        convs.append(conv)
    conv = _silu(jnp.concatenate(convs, axis=0))
    gate = jnp.concatenate(gates, axis=0)
    q, k, v = conv[:, :256], conv[:, 256:512], conv[:, 512:768]
    ones_h = _head_ones()
    q = q * lax.rsqrt(_dot2(q * q, ones_h) + NORM_EPS) * (HEAD_W ** -0.5)
    k = k * lax.rsqrt(_dot2(k * k, ones_h) + NORM_EPS)
    lane_h = _iota((bb * tl, BRANCH_W), 1) // HEAD_W
    a_in = jnp.zeros((bb * tl, BRANCH_W), F32)
    b_in = jnp.zeros((bb * tl, BRANCH_W), F32)
    for h in range(HEADS):
        a_in = jnp.where(lane_h == h, ab[:, h:h + 1], a_in)
        b_in = jnp.where(lane_h == h, ab[:, HEADS + h:HEADS + h + 1], b_in)
    beta = jax.nn.sigmoid(b_in)
    g = -jnp.exp(alog_ref[...]) * _softplus(a_in + dtb_ref[...])
    gc = _cumsum_rows(g, l, bb * nt)
    eg = jnp.exp(gc)
    kb = k * beta
    vb = v * beta
    kbg = kb * eg
    qg = q * eg

    hm = _head_mask(l)
    strict, incl = _tri_masks(l)
    n = HEADS * l
    chunks = [(b, j) for j in range(nt) for b in range(bb)]
    rows = {s: slice((s[0] * nt + s[1]) * l, (s[0] * nt + s[1] + 1) * l) for s in chunks}
    ex = lambda x: {s: _expand(x[rows[s]], hm) for s in chunks}
    k_e, vb_e, kbg_e = ex(k), ex(vb), ex(kbg)
    lane_hd = _iota((l, n), 1) // l
    eye = _iota((l, n), 0) == (_iota((l, n), 1) % l)
    decay = {}
    for s in chunks:
        gi = jnp.zeros((l, n), F32)
        for h in range(HEADS):
            gi = jnp.where(lane_hd == h, gc[rows[s]][:, h * HEAD_W:h * HEAD_W + 1], gi)
        gj = jnp.sum(jnp.where(eye, gi, 0.0), axis=0, keepdims=True)
        decay[s] = jnp.where(incl, jnp.exp(jnp.where(incl, gi - gj, 0.0)), 0.0)
    lm = [jnp.where(strict, _dot_nt(kb[rows[s]], k_e[s]) * decay[s], 0.0) for s in chunks]
    qk = {s: _dot_nt(q[rows[s]], k_e[s]) * decay[s] for s in chunks}
    t_inv = dict(zip(chunks, _unit_lower_inverse(lm, l)))
    uc = {s: _dot(t_inv[s], vb_e[s]) for s in chunks}
    wc = {s: _dot(t_inv[s], kbg_e[s]) for s in chunks}
    g_last = {s: gc[rows[s]][l - 1:l] for s in chunks}
    k_dec = {s: (k[rows[s]] * jnp.exp(g_last[s] - gc[rows[s]])).T for s in chunks}
    head_blk = _same_head(HEAD_W)
    st = [st_ref[b] for b in range(bb)]
    ob = {}
    for j in range(nt):
        now = [(b, j) for b in range(bb)]
        o0 = {s: _dot(qg[rows[s]], st[s[0]]) for s in now}
        v_new = {s: uc[s] - _dot(wc[s], st[s[0]]) for s in now}
        for s in now:
            ob[s] = o0[s] + _dot(qk[s], _expand(v_new[s], hm))
        for s in now:
            st[s[0]] = st[s[0]] * jnp.exp(g_last[s]) + jnp.where(head_blk, _dot(k_dec[s], v_new[s]), 0.0)
    for b in range(bb):
        st_ref[b] = st[b]
    o = jnp.concatenate([ob[(b, j)] for b in range(bb) for j in range(nt)], axis=0)
    ms = _dot2(o * o, ones_h) * (1.0 / HEAD_W)
    o = (o * lax.rsqrt(ms + NORM_EPS) * nw_ref[...] * _silu(gate)).astype(y_ref.dtype)
    for b in range(bb):
        y_ref[b] = o[b * tl:(b + 1) * tl]

    @pl.when(c == pl.num_programs(1) - 1)
    def _():
        sl_ref[...] = st_ref[...]
        cvo_ref[...] = cv_ref[...]


def _gdn_mixer(hn3, w_mix, conv0, s0, conv_w, a_log, dt_bias, norm_w, l):
    bsz, t, _ = hn3.shape
    cv0 = jnp.pad(conv0, ((0, 0), (8 - (GD_CONV - 1), 0), (0, 0)))
    cw = jnp.pad(conv_w, ((0, 8 - GD_CONV), (0, 0)))
    per_head = lambda a: jnp.repeat(a, HEAD_W).reshape(1, BRANCH_W)
    args = (cv0, _block_diag_heads(s0), cw, per_head(a_log), per_head(dt_bias),
            jnp.tile(norm_w, HEADS).reshape(1, BRANCH_W))
    full = lambda a: pl.BlockSpec(a.shape, lambda b, c: (0,) * a.ndim)
    bb, nt = _streams_per_step(bsz, t, l)
    y, sl, cvo = pl.pallas_call(
        functools.partial(_gdn_kernel, l=l, bb=bb, nt=nt),
        grid=(bsz // bb, t // (nt * l)),
        in_specs=[pl.BlockSpec((bb, nt * l, D_MODEL), lambda b, c: (b, c, 0)),
                  _w_cols(COL_GD, 1024), _w_cols(COL_AB, LANES),
                  pl.BlockSpec((bb, 8, GD_QKV), lambda b, c: (b, 0, 0)),
                  pl.BlockSpec((bb, BRANCH_W, BRANCH_W), lambda b, c: (b, 0, 0))]
                 + [full(a) for a in args[2:]],
        out_specs=[pl.BlockSpec((bb, nt * l, BRANCH_W), lambda b, c: (b, c, 0)),
                   pl.BlockSpec((bb, BRANCH_W, BRANCH_W), lambda b, c: (b, 0, 0)),
                   pl.BlockSpec((bb, 8, GD_QKV), lambda b, c: (b, 0, 0))],
        out_shape=[jax.ShapeDtypeStruct((bsz, t, BRANCH_W), BF16),
                   jax.ShapeDtypeStruct((bsz, BRANCH_W, BRANCH_W), F32),
                   jax.ShapeDtypeStruct((bsz, 8, GD_QKV), F32)],
        scratch_shapes=[pltpu.VMEM((bb, BRANCH_W, BRANCH_W), F32), pltpu.VMEM((bb, 8, GD_QKV), F32)],
        compiler_params=_cparams(("parallel", "arbitrary")),
    )(hn3, w_mix, w_mix, *args)
    return y, _diag_blocks(sl), cvo[:, 8 - (GD_CONV - 1):]


def _merge_kernel(ya_ref, yb_ref, yc_ref, yd_ref, x_ref, nw_ref, wg_ref, wbr_ref, wout_ref, o_ref):
    x = x_ref[...]
    ms = jnp.mean(x * x, axis=-1, keepdims=True)
    h = (x * lax.rsqrt(ms + NORM_EPS) * nw_ref[...]).astype(BF16)
    m = None
    for b, y_ref in enumerate((ya_ref, yb_ref, yc_ref, yd_ref)):
        gate = jnp.dot(h, wg_ref[:, b * D_MODEL:(b + 1) * D_MODEL], preferred_element_type=F32)
        br = jnp.dot(y_ref[...], wbr_ref[b], preferred_element_type=F32)
        term = (0.5 * jnp.tanh(0.5 * gate) + 0.5) * br
        m = term if m is None else m + term
    o_ref[...] = x + jnp.dot(m.astype(BF16), wout_ref[...], preferred_element_type=F32)


def _merge(ys, x2, norm_w, w_gate, w_branch, w_out, tm):
    n = x2.shape[0]
    yspec = pl.BlockSpec((tm, BRANCH_W), lambda i: (i, 0))
    once = pl.Buffered(1)
    return pl.pallas_call(
        _merge_kernel,
        grid=(n // tm,),
        in_specs=[yspec, yspec, yspec, yspec,
                  pl.BlockSpec((tm, D_MODEL), lambda i: (i, 0)),
                  pl.BlockSpec((1, D_MODEL), lambda i: (0, 0)),
                  pl.BlockSpec((D_MODEL, 4 * D_MODEL), lambda i: (0, 0), pipeline_mode=once),
                  pl.BlockSpec((4, BRANCH_W, D_MODEL), lambda i: (0, 0, 0), pipeline_mode=once),
                  pl.BlockSpec((D_MODEL, D_MODEL), lambda i: (0, 0), pipeline_mode=once)],
        out_specs=pl.BlockSpec((tm, D_MODEL), lambda i: (i, 0)),
        out_shape=jax.ShapeDtypeStruct((n, D_MODEL), F32),
        compiler_params=_cparams(("parallel",)),
    )(*[y.reshape(n, BRANCH_W) for y in ys], x2, norm_w.reshape(1, D_MODEL), w_gate,
      w_branch.astype(BF16), w_out.astype(BF16))


def _ffn_kernel(x_ref, nw_ref, w1_ref, w3_ref, w2_ref, o_ref):
    x = x_ref[...]
    ms = jnp.mean(x * x, axis=-1, keepdims=True)
    h = (x * lax.rsqrt(ms + NORM_EPS) * nw_ref[...]).astype(BF16)
    a = _silu(jnp.dot(h, w1_ref[...], preferred_element_type=F32)) * jnp.dot(h, w3_ref[...], preferred_element_type=F32)
    o_ref[...] = x + jnp.dot(a.astype(BF16), w2_ref[...], preferred_element_type=F32)


def _ffn(x2, norm_w, w1, w3, w2, tm):
    n = x2.shape[0]
    dff = w1.shape[1]
    once = pl.Buffered(1)
    return pl.pallas_call(
        _ffn_kernel,
        grid=(n // tm,),
        in_specs=[pl.BlockSpec((tm, D_MODEL), lambda i: (i, 0)),
                  pl.BlockSpec((1, D_MODEL), lambda i: (0, 0)),
                  pl.BlockSpec((D_MODEL, dff), lambda i: (0, 0), pipeline_mode=once),
                  pl.BlockSpec((D_MODEL, dff), lambda i: (0, 0), pipeline_mode=once),
                  pl.BlockSpec((dff, D_MODEL), lambda i: (0, 0), pipeline_mode=once)],
        out_specs=pl.BlockSpec((tm, D_MODEL), lambda i: (i, 0)),
        out_shape=jax.ShapeDtypeStruct((n, D_MODEL), F32),
        compiler_params=_cparams(("parallel",)),
    )(x2, norm_w.reshape(1, D_MODEL), w1.astype(BF16), w3.astype(BF16), w2.astype(BF16))


SC_CORES = 2
SC_SUBCORES = 16
SC_WINDOW = 128
SC_PIECE = 256
MOE_BLOCK = 512
MOE_FF = 1792
ROUTER_ROWS = 256
HALF = D_MODEL // 2


HIGH16 = -65536


def _pack_bf16_pairs(h):
    bits = lax.bitcast_convert_type(h.astype(BF16).astype(F32), jnp.int32)
    return lax.shift_right_logical(bits[:, :HALF], 16) | (bits[:, HALF:] & HIGH16)


def _unpack_bf16_pairs(pieces):
    lo = [lax.bitcast_convert_type(lax.shift_left(w, 16), F32) for w in pieces]
    hi = [lax.bitcast_convert_type(w & HIGH16, F32) for w in pieces]
    return jnp.concatenate(lo + hi, axis=1).astype(BF16)


def _route_kernel(x_ref, nw_ref, wr_ref, br_ref, hp_ref, tok_ref, cnt_ref, seen_ref, *, tm):
    @pl.when(pl.program_id(0) == 0)
    def _():
        seen_ref[...] = jnp.zeros_like(seen_ref)

    x = x_ref[...]
    ms = jnp.mean(x * x, axis=-1, keepdims=True)
    h = x * lax.rsqrt(ms + NORM_EPS) * nw_ref[...]
    hp = _pack_bf16_pairs(h)
    for q in range(HALF // SC_PIECE):
        hp_ref[q] = hp[:, q * SC_PIECE:(q + 1) * SC_PIECE]
    lane = _iota((tm, LANES), 1)
    logits = jnp.where(lane < N_EXPERTS, _dot_hp(h, wr_ref[...]) + br_ref[...], -jnp.inf)
    m1 = jnp.max(logits, axis=1, keepdims=True)
    i1 = jnp.min(jnp.where(logits == m1, lane, LANES), axis=1, keepdims=True)
    rest = jnp.where(lane == i1, -jnp.inf, logits)
    m2 = jnp.max(rest, axis=1, keepdims=True)
    i2 = jnp.min(jnp.where(rest == m2, lane, LANES), axis=1, keepdims=True)
    e2 = jnp.exp(m2 - m1)
    g1 = 1.0 / (1.0 + e2)
    g2 = e2 / (1.0 + e2)
    self32 = ((lane == i1) | (lane == i2)).astype(F32)
    seen = seen_ref[...]
    tri = (_iota((tm, tm), 1) < _iota((tm, tm), 0)).astype(BF16)
    rank = jnp.dot(tri, self32.astype(BF16), preferred_element_type=F32) + seen
    r1 = jnp.sum(jnp.where(lane == i1, rank, 0.0), axis=1, keepdims=True)
    r2 = jnp.sum(jnp.where(lane == i2, rank, 0.0), axis=1, keepdims=True)
    cols = (i1.astype(F32), i2.astype(F32), r1, r2, g1, g2)
    tok = jnp.zeros((tm, LANES), F32)
    for j, col in enumerate(cols):
        tok = jnp.where(lane == j, col, tok)
    tok_ref[...] = tok
    seen = seen + jnp.sum(self32, axis=0, keepdims=True)
    seen_ref[...] = seen
    cnt_ref[...] = jnp.broadcast_to(seen, (8, LANES))


def _route(x2, norm_w, w_router, b_router, tm):
    n = x2.shape[0]
    wr = jnp.pad(w_router, ((0, 0), (0, LANES - N_EXPERTS)))
    br = jnp.pad(b_router, (0, LANES - N_EXPERTS)).reshape(1, LANES)
    return pl.pallas_call(
        functools.partial(_route_kernel, tm=tm),
        grid=(n // tm,),
        in_specs=[pl.BlockSpec((tm, D_MODEL), lambda i: (i, 0)),
                  pl.BlockSpec((1, D_MODEL), lambda i: (0, 0)),
                  pl.BlockSpec((D_MODEL, LANES), lambda i: (0, 0)),
                  pl.BlockSpec((1, LANES), lambda i: (0, 0))],
        out_specs=[pl.BlockSpec((HALF // SC_PIECE, tm, SC_PIECE), lambda i: (0, i, 0)),
                   pl.BlockSpec((tm, LANES), lambda i: (i, 0)),
                   pl.BlockSpec((8, LANES), lambda i: (0, 0))],
        out_shape=[jax.ShapeDtypeStruct((HALF // SC_PIECE, n, SC_PIECE), jnp.int32),
                   jax.ShapeDtypeStruct((n, LANES), F32),
                   jax.ShapeDtypeStruct((8, LANES), F32)],
        scratch_shapes=[pltpu.VMEM((1, LANES), F32)],
        compiler_params=_cparams(("arbitrary",)),
    )(x2, norm_w.reshape(1, D_MODEL), wr, br)


def _sc_rows_multiple(d):
    return SC_CORES * SC_SUBCORES * SC_WINDOW * SC_PIECE // d


def _sc_gather_rows(table, idx):
    f, v, _ = table.shape
    b = idx.shape[0]
    assert b % _sc_rows_multiple(f * SC_PIECE) == 0
    idx_all = (idx[None, :] + (jnp.arange(f, dtype=jnp.int32) * v)[:, None]).reshape(-1)
    return _sc_gather_pieces(table.reshape(f * v, SC_PIECE), idx_all).reshape(f, b, SC_PIECE)


def _sc_gather_pieces(table, idx):
    bp = idx.shape[0]
    d = table.shape[1]
    window = SC_WINDOW
    idx2 = idx.reshape(1, bp)
    mesh = plsc.VectorSubcoreMesh(core_axis_name="core", subcore_axis_name="subcore")

    @functools.partial(pl.kernel, out_type=jax.ShapeDtypeStruct((bp, d), table.dtype), mesh=mesh)
    def gather(x_hbm, i_hbm, o_hbm):
        def body(i_vmem, o_vmem):
            pltpu.sync_copy(x_hbm.at[i_vmem.at[0]], o_vmem)

        pltpu.emit_pipeline(
            body,
            grid=(bp // window,),
            in_specs=[pl.BlockSpec((1, window), index_map=lambda i: (0, i))],
            out_specs=[pl.BlockSpec((window, d), index_map=lambda i: (i, 0))],
            core_axis_name=("core", "subcore"),
            dimension_semantics=(pltpu.PARALLEL,),
        )(i_hbm, o_hbm)

    return gather(table, idx2)


def _sc_scatter_rows(rows, pos, n_slots):
    f, n0, _ = rows.shape
    n = -(-n0 // _sc_rows_multiple(D_MODEL)) * _sc_rows_multiple(D_MODEL)
    if n != n0:
        rows = jnp.pad(rows, ((0, 0), (0, n - n0), (0, 0)))
        pos = jnp.pad(pos.reshape(2, n0), ((0, 0), (0, n - n0)), constant_values=n_slots - 1)
    nb = n // SC_WINDOW
    assert (f * 2 * nb) % (SC_CORES * SC_SUBCORES) == 0
    idx = (pos.reshape(1, 2 * n) + (jnp.arange(f, dtype=jnp.int32) * n_slots)[:, None]).reshape(1, f * 2 * n)
    mesh = plsc.VectorSubcoreMesh(core_axis_name="core", subcore_axis_name="subcore")

    @functools.partial(pl.kernel, out_type=jax.ShapeDtypeStruct((f * n_slots, SC_PIECE), rows.dtype), mesh=mesh)
    def scatter(x_hbm, i_hbm, o_hbm):
        def body(x_vmem, i_vmem):
            pltpu.sync_copy(x_vmem, o_hbm.at[i_vmem.at[0]])

        pltpu.emit_pipeline(
            body,
            grid=(f * 2 * nb,),
            in_specs=[pl.BlockSpec((SC_WINDOW, SC_PIECE), index_map=lambda i: ((i // (2 * nb)) * nb + i % nb, 0)),
                      pl.BlockSpec((1, SC_WINDOW), index_map=lambda i: (0, i))],
            out_specs=[],
            core_axis_name=("core", "subcore"),
            dimension_semantics=(pltpu.PARALLEL,),
        )(x_hbm, i_hbm)

    return scatter(rows.reshape(f * n, SC_PIECE), idx).reshape(f, n_slots, SC_PIECE)


def _experts_kernel(be_ref, nb_ref, nv_ref, xs_ref, w1_ref, w3_ref, w2_ref, o_ref):
    g, c = pl.program_id(0), pl.program_id(1)

    @pl.when(c == 0)
    def _():
        o_ref[...] = jnp.zeros_like(o_ref)

    @pl.when(g < nb_ref[0])
    def _():
        live = _iota((xs_ref.shape[1], 1), 0) < nv_ref[g]
        xb = _unpack_bf16_pairs([jnp.where(live, xs_ref[q], 0) for q in range(HALF // SC_PIECE)])
        a = (_silu(jnp.dot(xb, w1_ref[...], preferred_element_type=F32))
             * jnp.dot(xb, w3_ref[...], preferred_element_type=F32))
        y = jnp.dot(a.astype(BF16), w2_ref[...], preferred_element_type=F32)
        for q in range(D_MODEL // SC_PIECE):
            o_ref[q] += y[:, q * SC_PIECE:(q + 1) * SC_PIECE]


def _experts(xs, block_expert, n_blocks, block_rows, w1, w3, w2, blk, tf):
    n_slots = xs.shape[1]
    dff = w1.shape[2]
    grid_spec = pltpu.PrefetchScalarGridSpec(
        num_scalar_prefetch=3,
        grid=(n_slots // blk, dff // tf),
        in_specs=[pl.BlockSpec((HALF // SC_PIECE, blk, SC_PIECE), lambda g, c, be, nb, nv: (0, g, 0)),
                  pl.BlockSpec((None, D_MODEL, tf), lambda g, c, be, nb, nv: (be[g], 0, c)),
                  pl.BlockSpec((None, D_MODEL, tf), lambda g, c, be, nb, nv: (be[g], 0, c)),
                  pl.BlockSpec((None, tf, D_MODEL), lambda g, c, be, nb, nv: (be[g], c, 0))],
        out_specs=pl.BlockSpec((D_MODEL // SC_PIECE, blk, SC_PIECE), lambda g, c, be, nb, nv: (0, g, 0)),
    )
    return pl.pallas_call(
        _experts_kernel,
        grid_spec=grid_spec,
        out_shape=jax.ShapeDtypeStruct((D_MODEL // SC_PIECE, n_slots, SC_PIECE), F32),
        compiler_params=_cparams(("parallel", "arbitrary")),
    )(block_expert, n_blocks, block_rows, xs, w1, w3, w2)


def _combine_kernel(x_ref, y1_ref, y2_ref, tok_ref, nw_ref, o_ref):
    tok = tok_ref[...]
    rows = lambda y_ref: jnp.concatenate([y_ref[q] for q in range(D_MODEL // SC_PIECE)], axis=1)
    x = x_ref[...] + tok[:, 4:5] * rows(y1_ref) + tok[:, 5:6] * rows(y2_ref)
    ms = jnp.mean(x * x, axis=-1, keepdims=True)
    o_ref[...] = x * lax.rsqrt(ms + NORM_EPS) * nw_ref[...]


def _combine(x2, ys2, tok, norm_w, tm):
    n = x2.shape[0]
    spec = pl.BlockSpec((tm, D_MODEL), lambda i: (i, 0))
    yspec = lambda off: pl.BlockSpec((D_MODEL // SC_PIECE, tm, SC_PIECE), lambda i: (0, i + off, 0))
    return pl.pallas_call(
        _combine_kernel,
        grid=(n // tm,),
        in_specs=[spec, yspec(0), yspec(n // tm),
                  pl.BlockSpec((tm, LANES), lambda i: (i, 0)), pl.BlockSpec((1, D_MODEL), lambda i: (0, 0))],
        out_specs=spec,
        out_shape=jax.ShapeDtypeStruct((n, D_MODEL), F32),
        compiler_params=_cparams(("parallel",)),
    )(x2, ys2, ys2, tok, norm_w.reshape(1, D_MODEL))


def _moe_final(x2, norm_w, w_router, b_router, w1, w3, w2, final_norm_w):
    n = x2.shape[0]
    blk = min(MOE_BLOCK, n)
    hp, tok, cnt = _route(x2, norm_w, w_router, b_router, ROUTER_ROWS)
    counts = cnt[0, :N_EXPERTS].astype(jnp.int32)
    padded = (counts + blk - 1) // blk * blk
    ends = jnp.cumsum(padded)
    base = ends - padded
    round_up = lambda a, m: -(-a // m) * m
    n_slots = round_up((-(-2 * n // blk) + N_EXPERTS + 1) * blk, _sc_rows_multiple(HALF))
    n_blocks = n_slots // blk
    i12 = tok[:, 0:2].astype(jnp.int32)
    pos = (base[i12] + tok[:, 2:4].astype(jnp.int32)).T.reshape(-1)
    block_start = jnp.arange(n_blocks, dtype=jnp.int32) * blk
    block_expert = jnp.minimum(jnp.sum(block_start[:, None] >= ends[None, :], axis=1), N_EXPERTS - 1).astype(jnp.int32)
    block_rows = jnp.clip((base + counts)[block_expert] - block_start, 0, blk).astype(jnp.int32)
    xs = _sc_scatter_rows(hp, pos, n_slots)
    ys = _experts(xs, block_expert, (ends[-1:] // blk).astype(jnp.int32), block_rows,
                  w1.astype(BF16), w3.astype(BF16), w2.astype(BF16), blk, MOE_FF)
    pos = jnp.pad(pos, (0, round_up(2 * n, _sc_rows_multiple(D_MODEL)) - 2 * n))
    ys2 = _sc_gather_rows(ys, pos)
    return _combine(x2, ys2, tok, final_norm_w, min(1024, n))


def _permute_w_in(w):
    s5, rw, sg = w[:, 0:256], w[:, 256:1280], w[:, 1280:1792]
    gd, ab, gates = w[:, 1792:2816], w[:, 2816:2824], w[:, 2824:6920]
    pad = jnp.zeros((D_MODEL, PROJ_COLS - COL_AB - 8), w.dtype)
    return jnp.concatenate([rw, gd, sg, s5, ab, pad], axis=1).astype(BF16), gates.astype(BF16)


def _run_trunk(x, s5_h, rw_s, rw_shift, gd_s, gd_conv, p, w_in_perm):
    bsz, t, _ = x.shape
    n = bsz * t
    tm = min(1024, n)
    l = min(MIX_CHUNK, t)
    tc_s5 = min(256, t)
    new = ([], [], [], [], [], [])
    x2 = x.reshape(n, D_MODEL)
    for layer in range(2):
        g = lambda name: p[name][layer]
        w_mix, w_gate = w_in_perm[layer]
        hn3 = _norm_cast(x2, g('norm1_w'), tm).reshape(bsz, t, D_MODEL)
        y_a, s5_new = _s5_mixer(hn3, w_mix, s5_h[layer], g('s5_lam_re'), g('s5_lam_im'), g('s5_log_dt'), g('s5_b'),
                                g('s5_c'), g('s5_d'), g('s5_w_glu'), g('s5_b_glu'), tc_s5)
        y_b, rw_new, shift_new = _rwkv_mixer(hn3, w_mix, rw_shift[layer], rw_s[layer], g('rw_mu'), g('rw_w0'),
                                             g('rw_w2'), g('rw_a0'), g('rw_a2'), g('rw_g2'), g('rw_k_k'),
                                             g('rw_k_a'), g('rw_r_k'), g('rw_ln_w'), g('rw_ln_b'), l)
        y_c, sg_v = _sgu_mixer(hn3, w_mix, g('sg_ln_w'), g('sg_ln_b'), g('sg_w_s'), g('sg_b_s'))
        y_d, gd_new, conv_new = _gdn_mixer(hn3, w_mix, gd_conv[layer], gd_s[layer], g('gd_conv_w'), g('gd_a_log'),
                                           g('gd_dt_bias'), g('gd_norm_w'), l)
        x2 = _merge((y_a, y_b, y_c, y_d), x2, g('norm1_w'), w_gate, g('w_branch'), g('w_out'), min(512, n))
        j = layer // 2
        if layer % 2 == 0:
            x2 = _ffn(x2, g('norm2_w'), p['ffn_w1'][j], p['ffn_w3'][j], p['ffn_w2'][j], min(512, n))
        else:
            y = _moe_final(x2, g('norm2_w'), p['moe_router'][j], p['moe_router_b'][j],
                           p['moe_w1'][j], p['moe_w3'][j], p['moe_w2'][j], p['final_norm_w'])
        for lst, s in zip(new, (s5_new, rw_new, shift_new, gd_new, conv_new, sg_v)):
            lst.append(s)
    return y.reshape(bsz, t, D_MODEL), [jnp.stack(lst) for lst in new]


def kernel(x_prompt, x_sample, state_s5, state_rwkv, state_rwkv_shift, state_gdn, state_gdn_conv, norm1_w, w_in, s5_lam_re, s5_lam_im, s5_log_dt, s5_b, s5_c, s5_d, s5_w_glu, s5_b_glu, rw_mu, rw_w0, rw_w2, rw_a0, rw_a2, rw_g2, rw_k_k, rw_k_a, rw_r_k, rw_ln_w, rw_ln_b, sg_ln_w, sg_ln_b, sg_w_s, sg_b_s, gd_conv_w, gd_a_log, gd_dt_bias, gd_norm_w, w_branch, w_out, norm2_w, ffn_w1, ffn_w3, ffn_w2, moe_router, moe_router_b, moe_w1, moe_w3, moe_w2, final_norm_w):
    p = {
        'norm1_w': norm1_w, 's5_lam_re': s5_lam_re, 's5_lam_im': s5_lam_im, 's5_log_dt': s5_log_dt,
        's5_b': s5_b, 's5_c': s5_c, 's5_d': s5_d, 's5_w_glu': s5_w_glu, 's5_b_glu': s5_b_glu,
        'rw_mu': rw_mu, 'rw_w0': rw_w0, 'rw_w2': rw_w2, 'rw_a0': rw_a0, 'rw_a2': rw_a2, 'rw_g2': rw_g2,
        'rw_k_k': rw_k_k, 'rw_k_a': rw_k_a, 'rw_r_k': rw_r_k, 'rw_ln_w': rw_ln_w, 'rw_ln_b': rw_ln_b,
        'sg_ln_w': sg_ln_w, 'sg_ln_b': sg_ln_b, 'sg_w_s': sg_w_s, 'sg_b_s': sg_b_s,
        'gd_conv_w': gd_conv_w, 'gd_a_log': gd_a_log, 'gd_dt_bias': gd_dt_bias, 'gd_norm_w': gd_norm_w,
        'w_branch': w_branch, 'w_out': w_out, 'norm2_w': norm2_w,
        'ffn_w1': ffn_w1, 'ffn_w3': ffn_w3, 'ffn_w2': ffn_w2,
        'moe_router': moe_router, 'moe_router_b': moe_router_b, 'moe_w1': moe_w1, 'moe_w3': moe_w3, 'moe_w2': moe_w2,
        'final_norm_w': final_norm_w,
    }
    w_in_perm = [_permute_w_in(w_in[layer]) for layer in range(2)]
    bp, dt = x_prompt.shape[0], x_prompt.dtype
    depth = w_in.shape[0]
    y_prompt, (s5_p, rw_p, rwsh_p, gd_p, gdc_p, _) = _run_trunk(
        x_prompt,
        jnp.zeros((depth, bp, S5_GROUPS, S5_STATE, 2), dt),
        jnp.zeros((depth, bp, HEADS, HEAD_W, HEAD_W), dt),
        jnp.zeros((depth, bp, RW_COLS), dt),
        jnp.zeros((depth, bp, HEADS, HEAD_W, HEAD_W), dt),
        jnp.zeros((depth, bp, GD_CONV - 1, GD_QKV), dt),
        p, w_in_perm)
    y_sample, (s5_s, rw_s, rwsh_s, gd_s, gdc_s, sgv_s) = _run_trunk(
        x_sample, state_s5, state_rwkv, state_rwkv_shift, state_gdn, state_gdn_conv, p, w_in_perm)
    return (y_prompt, y_sample, s5_p, rw_p, rwsh_p, gd_p, gdc_p, s5_s, rw_s, rwsh_s, gd_s, gdc_s, sgv_s)
```

```python
import functools

import jax
import jax.numpy as jnp
from jax import lax
from jax.experimental import pallas as pl
from jax.experimental.pallas import tpu as pltpu
from jax.experimental.pallas import tpu_sc as plsc

F32 = jnp.float32
BF16 = jnp.bfloat16

D_MODEL = 1024
BRANCH_W = 256
HEADS = 4
HEAD_W = 64
S5_GROUPS = 16
S5_GROUP = 16
S5_STATE = 64
S5_W = S5_GROUPS * S5_STATE
S5_ROWS = 8
SG_CHUNK = 128
SG_ROWS = 512
GD_CONV = 4
GD_QKV = 3 * BRANCH_W
RW_COLS = 1024
RW_EPS = 64e-5
NORM_EPS = 1e-6
N_EXPERTS = 8
LANES = 128

COL_RW = 0
COL_GD = 1024
COL_SG = 2048
COL_S5 = 2560
COL_AB = 2816
PROJ_COLS = 3072

VMEM_LIMIT = 48 * 1024 * 1024


def _cparams(sem):
    return pltpu.CompilerParams(dimension_semantics=sem, vmem_limit_bytes=VMEM_LIMIT)


def _dot(a, b):
    return jnp.dot(a.astype(BF16), b.astype(BF16), preferred_element_type=F32)


def _dot_nt(a, b):
    return lax.dot_general(a.astype(BF16), b.astype(BF16), (((1,), (1,)), ((), ())),
                           preferred_element_type=F32)


def _split3(a):
    hi = a.astype(BF16)
    r1 = a - hi.astype(F32)
    mid = r1.astype(BF16)
    lo = (r1 - mid.astype(F32)).astype(BF16)
    return hi, mid, lo


def _dot3_left(b_exact, a):
    hi, mid, lo = _split3(a)
    b = b_exact.astype(BF16)
    return (jnp.dot(b, hi, preferred_element_type=F32) + jnp.dot(b, mid, preferred_element_type=F32)
            + jnp.dot(b, lo, preferred_element_type=F32))


def _dot_hp(a, b):
    a0, a1, _ = _split3(a)
    b0, b1, _ = _split3(b)
    d = lambda x, y: jnp.dot(x, y, preferred_element_type=F32)
    return d(a0, b0) + (d(a0, b1) + d(a1, b0))


def _iota(shape, axis):
    return lax.broadcasted_iota(jnp.int32, shape, axis)


def _head_ones():
    r = _iota((BRANCH_W, BRANCH_W), 0) // HEAD_W
    c = _iota((BRANCH_W, BRANCH_W), 1) // HEAD_W
    return (r == c).astype(BF16)


def _head_mask(l):
    r = _iota((HEADS * l, BRANCH_W), 0) // l
    c = _iota((HEADS * l, BRANCH_W), 1) // HEAD_W
    return r == c


def _expand(x, mask):
    return jnp.where(mask, jnp.concatenate([x] * HEADS, axis=0), 0.0)


def _tri_masks(l):
    i = _iota((l, HEADS * l), 0)
    j = _iota((l, HEADS * l), 1) % l
    return j < i, j <= i


def _same_head(l):
    n = HEADS * l
    return (_iota((n, n), 0) // l) == (_iota((n, n), 1) // l)


def _expand_sq(x, same):
    return jnp.where(same, jnp.concatenate([x] * HEADS, axis=0), 0.0)


def _unit_lower_inverse(a_strict, l):
    i = _iota((l, HEADS * l), 0)
    j = _iota((l, HEADS * l), 1) % l
    eye = (i == j).astype(F32)
    same = _same_head(l)
    p = [-a for a in a_strict]
    t = [eye + x for x in p]
    k = 2
    while k < l:
        p = [_dot(x, _expand_sq(x, same)) for x in p]
        t = [y + _dot(y, _expand_sq(x, same)) for y, x in zip(t, p)]
        k *= 2
    return t


def _dot2(a, b_exact):
    hi = a.astype(BF16)
    lo = (a - hi.astype(F32)).astype(BF16)
    b = b_exact.astype(BF16)
    return jnp.dot(hi, b, preferred_element_type=F32) + jnp.dot(lo, b, preferred_element_type=F32)


def _cumsum_rows(x, l, nb=1):
    n = nb * l
    i, j = _iota((n, n), 0), _iota((n, n), 1)
    tri = ((j <= i) & ((i // l) == (j // l))).astype(BF16)
    return _dot3_left(tri, x)


def _softplus(x):
    return jnp.maximum(x, 0.0) + jnp.log(1.0 + jnp.exp(-jnp.abs(x)))


def _silu(x):
    return x * jax.nn.sigmoid(x)


def _norm_cast_kernel(x_ref, nw_ref, o_ref):
    x = x_ref[...]
    ms = jnp.mean(x * x, axis=-1, keepdims=True)
    o_ref[...] = (x * lax.rsqrt(ms + NORM_EPS) * nw_ref[...]).astype(BF16)


def _norm_cast(x2, norm_w, tm):
    n = x2.shape[0]
    spec = pl.BlockSpec((tm, D_MODEL), lambda i: (i, 0))
    return pl.pallas_call(
        _norm_cast_kernel,
        grid=(n // tm,),
        in_specs=[spec, pl.BlockSpec((1, D_MODEL), lambda i: (0, 0))],
        out_specs=spec,
        out_shape=jax.ShapeDtypeStruct((n, D_MODEL), BF16),
        compiler_params=_cparams(("parallel",)),
    )(x2, norm_w.reshape(1, D_MODEL))


def _s5_kernel(hn_ref, wz_ref, h0_ref, wb_ref, wc_ref, a2k_ref, apow_ref, d_ref, wg_ref, bg_ref,
               y_ref, hl_ref, hs_ref, hb_ref, *, tc):
    c = pl.program_id(1)

    @pl.when(c == 0)
    def _():
        hs_ref[...] = h0_ref[...]

    u = jnp.dot(hn_ref[...], wz_ref[...], preferred_element_type=F32)
    x = _dot(u, wb_ref[...])
    xr, xi = x[:, :S5_W], x[:, S5_W:]
    ng = tc // S5_ROWS
    xr = xr.reshape(ng, S5_ROWS, S5_W)
    xi = xi.reshape(ng, S5_ROWS, S5_W)
    row = _iota((S5_ROWS, 1), 0)
    k, d = 0, 1
    while d < S5_ROWS:
        m = row >= d
        ar = jnp.where(m, a2k_ref[k:k + 1, :S5_W], 0.0)
        ai = jnp.where(m, a2k_ref[k:k + 1, S5_W:], 0.0)
        sr = pltpu.roll(xr, d, axis=1)
        si = pltpu.roll(xi, d, axis=1)
        xr, xi = xr + (ar * sr - ai * si), xi + (ar * si + ai * sr)
        k, d = k + 1, d * 2
    hb_ref[:, :S5_W] = xr.reshape(tc, S5_W)
    hb_ref[:, S5_W:] = xi.reshape(tc, S5_W)
    pr, pi_ = apow_ref[:, :S5_W], apow_ref[:, S5_W:]

    def group(gi, carry):
        cr, ci = carry
        rows = pl.ds(pl.multiple_of(gi * S5_ROWS, S5_ROWS), S5_ROWS)
        hr = hb_ref[rows, :S5_W] + (pr * cr - pi_ * ci)
        hi = hb_ref[rows, S5_W:] + (pr * ci + pi_ * cr)
        hb_ref[rows, :S5_W] = hr
        hb_ref[rows, S5_W:] = hi
        return hr[S5_ROWS - 1:S5_ROWS], hi[S5_ROWS - 1:S5_ROWS]

    cr, ci = lax.fori_loop(0, tc // S5_ROWS, group, (hs_ref[:, :S5_W], hs_ref[:, S5_W:]), unroll=4)
    hs_ref[:, :S5_W] = cr
    hs_ref[:, S5_W:] = ci
    y = _dot(hb_ref[:, :S5_W], wc_ref[:S5_W]) + _dot(hb_ref[:, S5_W:], wc_ref[S5_W:])
    y = jax.nn.gelu(y + d_ref[...] * u)
    y = y * jax.nn.sigmoid(_dot(y, wg_ref[...]) + bg_ref[...])
    y_ref[...] = y.astype(y_ref.dtype)

    @pl.when(c == pl.num_programs(1) - 1)
    def _():
        hl_ref[...] = hs_ref[...]


def _s5_tables(lam_re, lam_im, log_dt, b_c, c_c, tc):
    dt = jnp.exp(log_dt)[:, None]
    mag = jnp.exp(lam_re * dt)
    ab_re, ab_im = mag * jnp.cos(lam_im * dt), mag * jnp.sin(lam_im * dt)
    den = lam_re * lam_re + lam_im * lam_im
    nr = ab_re - 1.0
    cf_re = (nr * lam_re + ab_im * lam_im) / den
    cf_im = (ab_im * lam_re - nr * lam_im) / den
    br, bi = b_c[..., 0], b_c[..., 1]
    bb_re = cf_re[..., None] * br - cf_im[..., None] * bi
    bb_im = cf_re[..., None] * bi + cf_im[..., None] * br
    eye = jnp.eye(S5_GROUPS, dtype=F32)
    bd_in = lambda m: jnp.einsum('gph,gk->ghkp', m, eye).reshape(BRANCH_W, S5_W)
    wb = jnp.concatenate([bd_in(bb_re), bd_in(bb_im)], axis=1)
    cr, ci = c_c[..., 0], c_c[..., 1]
    bd_out = lambda m: jnp.einsum('ghp,gk->gpkh', m, eye).reshape(S5_W, BRANCH_W)
    wc = jnp.concatenate([bd_out(cr), -bd_out(ci)], axis=0)
    pr, pi_ = ab_re.reshape(1, S5_W), ab_im.reshape(1, S5_W)
    lv_r, lv_i = [], []
    tr, ti = pr, pi_
    d = 1
    while d < tc:
        lv_r.append(pr)
        lv_i.append(pi_)
        tr, ti = (jnp.concatenate([tr, tr * pr - ti * pi_], axis=0),
                  jnp.concatenate([ti, tr * pi_ + ti * pr], axis=0))
        pr, pi_ = pr * pr - pi_ * pi_, 2.0 * pr * pi_
        d *= 2
    n_lv = len(lv_r)
    pad = (-n_lv) % 8
    a2k = jnp.concatenate([jnp.concatenate(lv_r, axis=0), jnp.concatenate(lv_i, axis=0)], axis=1)
    a2k = jnp.pad(a2k, ((0, pad), (0, 0)))
    apow = jnp.concatenate([tr, ti], axis=1)
    return wb.astype(BF16), wc.astype(BF16), a2k, apow


def _w_cols(col, width):
    return pl.BlockSpec((D_MODEL, width), lambda b, c: (0, col // width))


def _s5_mixer(hn3, w_mix, h0, lam_re, lam_im, log_dt, b_c, c_c, d_skip, w_glu, b_glu, tc):
    bsz, t, _ = hn3.shape
    wb, wc, a2k, apow = _s5_tables(lam_re, lam_im, log_dt, b_c, c_c, S5_ROWS)
    h0f = jnp.concatenate([h0[..., 0].reshape(bsz, 1, S5_W), h0[..., 1].reshape(bsz, 1, S5_W)], axis=-1)
    full = lambda a: pl.BlockSpec(a.shape, lambda b, c: (0,) * a.ndim)
    d2, bg2, wg = d_skip.reshape(1, BRANCH_W), b_glu.reshape(1, BRANCH_W), w_glu.astype(BF16)
    y, hl = pl.pallas_call(
        functools.partial(_s5_kernel, tc=tc),
        grid=(bsz, t // tc),
        in_specs=[pl.BlockSpec((None, tc, D_MODEL), lambda b, c: (b, c, 0)), _w_cols(COL_S5, BRANCH_W),
                  pl.BlockSpec((None, 1, 2 * S5_W), lambda b, c: (b, 0, 0)),
                  full(wb), full(wc), full(a2k), full(apow), full(d2), full(wg), full(bg2)],
        out_specs=[pl.BlockSpec((None, tc, BRANCH_W), lambda b, c: (b, c, 0)),
                   pl.BlockSpec((None, 1, 2 * S5_W), lambda b, c: (b, 0, 0))],
        out_shape=[jax.ShapeDtypeStruct((bsz, t, BRANCH_W), BF16),
                   jax.ShapeDtypeStruct((bsz, 1, 2 * S5_W), F32)],
        scratch_shapes=[pltpu.VMEM((1, 2 * S5_W), F32), pltpu.VMEM((tc, 2 * S5_W), F32)],
        compiler_params=_cparams(("parallel", "arbitrary")),
    )(hn3, w_mix, h0f, wb, wc, a2k, apow, d2, wg, bg2)
    h_last = jnp.stack([hl[:, 0, :S5_W].reshape(bsz, S5_GROUPS, S5_STATE),
                        hl[:, 0, S5_W:].reshape(bsz, S5_GROUPS, S5_STATE)], axis=-1)
    return y, h_last


def _rwkv_kernel(hn_ref, wz_ref, sh0_ref, s0_ref, mu_ref, w0_ref, w2_ref, a0_ref, a2_ref, g2_ref,
                 kk_ref, ka_ref, rk_ref, lnw_ref, lnb_ref,
                 y_ref, sl_ref, sho_ref, st_ref, zp_ref, *, l, bb, nt):
    c = pl.program_id(1)

    @pl.when(c == 0)
    def _():
        st_ref[...] = s0_ref[...]
        zp_ref[...] = sh0_ref[...]

    tl = nt * l
    z_all = jnp.dot(jnp.concatenate([hn_ref[b] for b in range(bb)], axis=0), wz_ref[...],
                    preferred_element_type=F32)
    row = _iota((tl, 1), 0)
    zms = []
    for b in range(bb):
        z = z_all[b * tl:(b + 1) * tl]
        prev = jnp.where(row == 0, zp_ref[b], pltpu.roll(z, 1, axis=0))
        zp_ref[b] = z[tl - 1:tl]
        zms.append(z + (prev - z) * mu_ref[...])
    zm = jnp.concatenate(zms, axis=0)
    r, k, v = zm[:, 0:256], zm[:, 256:512], zm[:, 512:768]
    lo = zm[:, 768:896]
    g_lo = zm[:, 896:1024]
    w_log = -_softplus(-(w0_ref[...] + _dot(jnp.tanh(lo), w2_ref[...]))) - 0.5
    lw = -jnp.exp(w_log)
    a = jax.nn.sigmoid(a0_ref[...] + _dot(lo, a2_ref[...]))
    g = _dot(jax.nn.sigmoid(g_lo), g2_ref[...])
    ones_h = _head_ones()
    kk = k * kk_ref[...]
    kk = kk * lax.rsqrt(_dot2(kk * kk, ones_h) + NORM_EPS)
    k = k * (1.0 + (a - 1.0) * ka_ref[...])
    kka = kk * a

    cum = _cumsum_rows(lw, l, bb * nt)
    p_incl = jnp.exp(cum)
    p_inv = jnp.exp(-cum)
    kt = kk * jnp.exp(cum - lw)
    rt = r * p_incl
    kh = k * p_inv
    ah = kka * p_inv

    hm = _head_mask(l)
    strict, incl = _tri_masks(l)
    chunks = [(b, j) for j in range(nt) for b in range(bb)]
    rows = {s: slice((s[0] * nt + s[1]) * l, (s[0] * nt + s[1] + 1) * l) for s in chunks}
    ex = lambda x: {s: _expand(x[rows[s]], hm) for s in chunks}
    kh_e, ah_e, v_e = ex(kh), ex(ah), ex(v)
    a_aa = [jnp.where(strict, _dot_nt(kt[rows[s]], ah_e[s]), 0.0) for s in chunks]
    a_ak = {s: jnp.where(strict, _dot_nt(kt[rows[s]], kh_e[s]), 0.0) for s in chunks}
    b_ra = {s: jnp.where(incl, _dot_nt(rt[rows[s]], ah_e[s]), 0.0) for s in chunks}
    b_rk = {s: jnp.where(incl, _dot_nt(rt[rows[s]], kh_e[s]), 0.0) for s in chunks}
    t_inv = dict(zip(chunks, _unit_lower_inverse(a_aa, l)))
    av = {s: _dot(a_ak[s], v_e[s]) for s in chunks}
    bv = {s: _dot(b_rk[s], v_e[s]) for s in chunks}
    to_end = {s: jnp.exp(cum[rows[s]][l - 1:l] - cum[rows[s]]) for s in chunks}
    head_blk = _same_head(HEAD_W)
    st = [st_ref[b] for b in range(bb)]
    yb = {}
    for j in range(nt):
        now = [(b, j) for b in range(bb)]
        rhs = {s: _dot_nt(kt[rows[s]], st[s[0]]) + av[s] for s in now}
        ys = {s: _dot_nt(rt[rows[s]], st[s[0]]) + bv[s] for s in now}
        u = {s: _dot(t_inv[s], _expand(rhs[s], hm)) for s in now}
        for s in now:
            yb[s] = ys[s] - _dot(b_ra[s], _expand(u[s], hm))
        lhs_t = {s: jnp.concatenate([v[rows[s]], -u[s]], axis=0).T for s in now}
        rhs_k = {s: jnp.concatenate([k[rows[s]] * to_end[s], kka[rows[s]] * to_end[s]], axis=0) for s in now}
        for s in now:
            p_last = p_incl[rows[s]][l - 1:l]
            st[s[0]] = st[s[0]] * p_last + jnp.where(head_blk, _dot(lhs_t[s], rhs_k[s]), 0.0)
    for b in range(bb):
        st_ref[b] = st[b]
    y = jnp.concatenate([yb[(b, j)] for b in range(bb) for j in range(nt)], axis=0)

    inv_w = 1.0 / HEAD_W
    mean = _dot2(y, ones_h) * inv_w
    yc = y - mean
    var = _dot2(yc * yc, ones_h) * inv_w
    y = yc * lax.rsqrt(var + RW_EPS) * lnw_ref[...] + lnb_ref[...]
    bonus = _dot2(r * k * rk_ref[...], ones_h) * v
    y = ((y + bonus) * g).astype(y_ref.dtype)
    for b in range(bb):
        y_ref[b] = y[b * tl:(b + 1) * tl]

    @pl.when(c == pl.num_programs(1) - 1)
    def _():
        sl_ref[...] = st_ref[...]
        sho_ref[...] = zp_ref[...]


MIX_CHUNK = 64
STACK_STREAMS = 4
STACK_ROWS = 512


def _streams_per_step(bsz, t, l):
    bb = min(bsz, STACK_STREAMS)
    while bsz % bb:
        bb -= 1
    nt = max(1, min(t, STACK_ROWS // bb) // l)
    while (t // l) % nt:
        nt -= 1
    return bb, nt


def _block_diag_heads(s):
    bsz = s.shape[0]
    eye = jnp.eye(HEADS, dtype=s.dtype)
    return jnp.einsum('bhij,hg->bhigj', s, eye).reshape(bsz, BRANCH_W, BRANCH_W)


def _diag_blocks(s):
    bsz = s.shape[0]
    s5 = s.reshape(bsz, HEADS, HEAD_W, HEADS, HEAD_W)
    return jnp.stack([s5[:, h, :, h, :] for h in range(HEADS)], axis=1)


def _rwkv_mixer(hn3, w_mix, shift0, s0, mu, w0, w2, a0, a2, g2, k_k, k_a, r_k, ln_w, ln_b, l):
    bsz, t, _ = hn3.shape
    row = lambda a: a.reshape(1, -1)
    w2p = jnp.concatenate([w2, jnp.zeros_like(w2)], axis=0).astype(BF16)
    a2p = jnp.concatenate([jnp.zeros_like(a2), a2], axis=0).astype(BF16)
    args = (shift0.reshape(bsz, 1, RW_COLS), _block_diag_heads(s0), row(mu), row(w0), w2p, row(a0), a2p,
            g2.astype(BF16), row(k_k), row(k_a), row(r_k), row(ln_w), row(ln_b))
    full = lambda a: pl.BlockSpec(a.shape, lambda b, c: (0,) * a.ndim)
    bb, nt = _streams_per_step(bsz, t, l)
    y, sl, sho = pl.pallas_call(
        functools.partial(_rwkv_kernel, l=l, bb=bb, nt=nt),
        grid=(bsz // bb, t // (nt * l)),
        in_specs=[pl.BlockSpec((bb, nt * l, D_MODEL), lambda b, c: (b, c, 0)), _w_cols(COL_RW, RW_COLS),
                  pl.BlockSpec((bb, 1, RW_COLS), lambda b, c: (b, 0, 0)),
                  pl.BlockSpec((bb, BRANCH_W, BRANCH_W), lambda b, c: (b, 0, 0))]
                 + [full(a) for a in args[2:]],
        out_specs=[pl.BlockSpec((bb, nt * l, BRANCH_W), lambda b, c: (b, c, 0)),
                   pl.BlockSpec((bb, BRANCH_W, BRANCH_W), lambda b, c: (b, 0, 0)),
                   pl.BlockSpec((bb, 1, RW_COLS), lambda b, c: (b, 0, 0))],
        out_shape=[jax.ShapeDtypeStruct((bsz, t, BRANCH_W), BF16),
                   jax.ShapeDtypeStruct((bsz, BRANCH_W, BRANCH_W), F32),
                   jax.ShapeDtypeStruct((bsz, 1, RW_COLS), F32)],
        scratch_shapes=[pltpu.VMEM((bb, BRANCH_W, BRANCH_W), F32), pltpu.VMEM((bb, 1, RW_COLS), F32)],
        compiler_params=_cparams(("parallel", "arbitrary")),
    )(hn3, w_mix, *args)
    return y, _diag_blocks(sl), sho[:, 0]


def _sgu_kernel(hn_ref, wz_ref, lnw_ref, lnb_ref, wm_ref, bias_ref, o_ref, v_ref, *, l, nc):
    z = jnp.dot(hn_ref[...], wz_ref[...], preferred_element_type=F32)
    zg = jax.nn.gelu(z)
    u, v = zg[:, :BRANCH_W], zg[:, BRANCH_W:]
    mean = jnp.mean(v, axis=-1, keepdims=True)
    vc = v - mean
    var = jnp.mean(vc * vc, axis=-1, keepdims=True)
    v = vc * lax.rsqrt(var + NORM_EPS) * lnw_ref[...] + lnb_ref[...]
    v_ref[...] = v
    hm = _head_mask(l)
    wm = wm_ref[...]
    for i in range(nc):
        rows = slice(i * l, (i + 1) * l)
        mixed = bias_ref[...] + _dot(wm, _expand(v[rows], hm))
        o_ref[rows, :] = (u[rows] * mixed).astype(o_ref.dtype)


def _sgu_mixer(hn3, w_mix, ln_w, ln_b, w_s, b_s):
    bsz, t, _ = hn3.shape
    l = min(SG_CHUNK, t)
    nc = max(1, min(SG_ROWS, t) // l)
    tril = jnp.tril(jnp.ones((l, l), F32))
    wm = jnp.transpose(w_s[:, :l, :l] * tril, (1, 0, 2)).reshape(l, HEADS * l).astype(BF16)
    bias = jnp.repeat(jnp.transpose(b_s[:, :l]), HEAD_W, axis=1)
    row = lambda a: a.reshape(1, -1)
    full = lambda a: pl.BlockSpec(a.shape, lambda b, c: (0,) * a.ndim)
    args = (row(ln_w), row(ln_b), wm, bias)
    return pl.pallas_call(
        functools.partial(_sgu_kernel, l=l, nc=nc),
        grid=(bsz, t // (nc * l)),
        in_specs=[pl.BlockSpec((None, nc * l, D_MODEL), lambda b, c: (b, c, 0)), _w_cols(COL_SG, 2 * BRANCH_W)]
                 + [full(a) for a in args],
        out_specs=[pl.BlockSpec((None, nc * l, BRANCH_W), lambda b, c: (b, c, 0)),
                   pl.BlockSpec((None, nc * l, BRANCH_W), lambda b, c: (b, c, 0))],
        out_shape=[jax.ShapeDtypeStruct((bsz, t, BRANCH_W), BF16),
                   jax.ShapeDtypeStruct((bsz, t, BRANCH_W), F32)],
        compiler_params=_cparams(("parallel", "parallel")),
    )(hn3, w_mix, *args)


def _gdn_kernel(hn_ref, wz_ref, wab_ref, cv0_ref, s0_ref, cw_ref, alog_ref, dtb_ref, nw_ref,
                y_ref, sl_ref, cvo_ref, st_ref, cv_ref, *, l, bb, nt):
    c = pl.program_id(1)

    @pl.when(c == 0)
    def _():
        st_ref[...] = s0_ref[...]
        cv_ref[...] = cv0_ref[...]

    tl = nt * l
    hn = jnp.concatenate([hn_ref[b] for b in range(bb)], axis=0)
    z_all = jnp.dot(hn, wz_ref[...], preferred_element_type=F32)
    ab = jnp.dot(hn, wab_ref[...], preferred_element_type=F32)
    row8 = _iota((8, 1), 0)
    convs, gates = [], []
    for b in range(bb):
        z = z_all[b * tl:(b + 1) * tl]
        qkv = z[:, :GD_QKV]
        gates.append(z[:, GD_QKV:])
        carry = cv_ref[b]
        cv_ref[b] = qkv[tl - 8:tl]
        conv = qkv * cw_ref[GD_CONV - 1:GD_CONV]
        for j in range(1, GD_CONV):
            sh = pltpu.roll(qkv, j, axis=0)
            top = jnp.where(row8 < j, pltpu.roll(carry, j, axis=0), sh[:8])
            sh = jnp.concatenate([top, sh[8:]], axis=0) if tl > 8 else top
            conv = conv + sh * cw_ref[GD_CONV - 1 - j:GD_CONV - j]
        convs.append(conv)
    conv = _silu(jnp.concatenate(convs, axis=0))
    gate = jnp.concatenate(gates, axis=0)
    q, k, v = conv[:, :256], conv[:, 256:512], conv[:, 512:768]
    ones_h = _head_ones()
    q = q * lax.rsqrt(_dot2(q * q, ones_h) + NORM_EPS) * (HEAD_W ** -0.5)
    k = k * lax.rsqrt(_dot2(k * k, ones_h) + NORM_EPS)
    lane_h = _iota((bb * tl, BRANCH_W), 1) // HEAD_W
    a_in = jnp.zeros((bb * tl, BRANCH_W), F32)
    b_in = jnp.zeros((bb * tl, BRANCH_W), F32)
    for h in range(HEADS):
        a_in = jnp.where(lane_h == h, ab[:, h:h + 1], a_in)
        b_in = jnp.where(lane_h == h, ab[:, HEADS + h:HEADS + h + 1], b_in)
    beta = jax.nn.sigmoid(b_in)
    g = -jnp.exp(alog_ref[...]) * _softplus(a_in + dtb_ref[...])
    gc = _cumsum_rows(g, l, bb * nt)
    eg = jnp.exp(gc)
    kb = k * beta
    vb = v * beta
    kbg = kb * eg
    qg = q * eg

    hm = _head_mask(l)
    strict, incl = _tri_masks(l)
    n = HEADS * l
    chunks = [(b, j) for j in range(nt) for b in range(bb)]
    rows = {s: slice((s[0] * nt + s[1]) * l, (s[0] * nt + s[1] + 1) * l) for s in chunks}
    ex = lambda x: {s: _expand(x[rows[s]], hm) for s in chunks}
    k_e, vb_e, kbg_e = ex(k), ex(vb), ex(kbg)
    lane_hd = _iota((l, n), 1) // l
    eye = _iota((l, n), 0) == (_iota((l, n), 1) % l)
    decay = {}
    for s in chunks:
        gi = jnp.zeros((l, n), F32)
        for h in range(HEADS):
            gi = jnp.where(lane_hd == h, gc[rows[s]][:, h * HEAD_W:h * HEAD_W + 1], gi)
        gj = jnp.sum(jnp.where(eye, gi, 0.0), axis=0, keepdims=True)
        decay[s] = jnp.where(incl, jnp.exp(jnp.where(incl, gi - gj, 0.0)), 0.0)
    lm = [jnp.where(strict, _dot_nt(kb[rows[s]], k_e[s]) * decay[s], 0.0) for s in chunks]
    qk = {s: _dot_nt(q[rows[s]], k_e[s]) * decay[s] for s in chunks}
    t_inv = dict(zip(chunks, _unit_lower_inverse(lm, l)))
    uc = {s: _dot(t_inv[s], vb_e[s]) for s in chunks}
    wc = {s: _dot(t_inv[s], kbg_e[s]) for s in chunks}
    g_last = {s: gc[rows[s]][l - 1:l] for s in chunks}
    k_dec = {s: (k[rows[s]] * jnp.exp(g_last[s] - gc[rows[s]])).T for s in chunks}
    head_blk = _same_head(HEAD_W)
    st = [st_ref[b] for b in range(bb)]
    ob = {}
    for j in range(nt):
        now = [(b, j) for b in range(bb)]
        o0 = {s: _dot(qg[rows[s]], st[s[0]]) for s in now}
        v_new = {s: uc[s] - _dot(wc[s], st[s[0]]) for s in now}
        for s in now:
            ob[s] = o0[s] + _dot(qk[s], _expand(v_new[s], hm))
        for s in now:
            st[s[0]] = st[s[0]] * jnp.exp(g_last[s]) + jnp.where(head_blk, _dot(k_dec[s], v_new[s]), 0.0)
    for b in range(bb):
        st_ref[b] = st[b]
    o = jnp.concatenate([ob[(b, j)] for b in range(bb) for j in range(nt)], axis=0)
    ms = _dot2(o * o, ones_h) * (1.0 / HEAD_W)
    o = (o * lax.rsqrt(ms + NORM_EPS) * nw_ref[...] * _silu(gate)).astype(y_ref.dtype)
    for b in range(bb):
        y_ref[b] = o[b * tl:(b + 1) * tl]

    @pl.when(c == pl.num_programs(1) - 1)
    def _():
        sl_ref[...] = st_ref[...]
        cvo_ref[...] = cv_ref[...]


def _gdn_mixer(hn3, w_mix, conv0, s0, conv_w, a_log, dt_bias, norm_w, l):
    bsz, t, _ = hn3.shape
    cv0 = jnp.pad(conv0, ((0, 0), (8 - (GD_CONV - 1), 0), (0, 0)))
    cw = jnp.pad(conv_w, ((0, 8 - GD_CONV), (0, 0)))
    per_head = lambda a: jnp.repeat(a, HEAD_W).reshape(1, BRANCH_W)
    args = (cv0, _block_diag_heads(s0), cw, per_head(a_log), per_head(dt_bias),
            jnp.tile(norm_w, HEADS).reshape(1, BRANCH_W))
    full = lambda a: pl.BlockSpec(a.shape, lambda b, c: (0,) * a.ndim)
    bb, nt = _streams_per_step(bsz, t, l)
    y, sl, cvo = pl.pallas_call(
        functools.partial(_gdn_kernel, l=l, bb=bb, nt=nt),
        grid=(bsz // bb, t // (nt * l)),
        in_specs=[pl.BlockSpec((bb, nt * l, D_MODEL), lambda b, c: (b, c, 0)),
                  _w_cols(COL_GD, 1024), _w_cols(COL_AB, LANES),
                  pl.BlockSpec((bb, 8, GD_QKV), lambda b, c: (b, 0, 0)),
                  pl.BlockSpec((bb, BRANCH_W, BRANCH_W), lambda b, c: (b, 0, 0))]
                 + [full(a) for a in args[2:]],
        out_specs=[pl.BlockSpec((bb, nt * l, BRANCH_W), lambda b, c: (b, c, 0)),
                   pl.BlockSpec((bb, BRANCH_W, BRANCH_W), lambda b, c: (b, 0, 0)),
                   pl.BlockSpec((bb, 8, GD_QKV), lambda b, c: (b, 0, 0))],
        out_shape=[jax.ShapeDtypeStruct((bsz, t, BRANCH_W), BF16),
                   jax.ShapeDtypeStruct((bsz, BRANCH_W, BRANCH_W), F32),
                   jax.ShapeDtypeStruct((bsz, 8, GD_QKV), F32)],
        scratch_shapes=[pltpu.VMEM((bb, BRANCH_W, BRANCH_W), F32), pltpu.VMEM((bb, 8, GD_QKV), F32)],
        compiler_params=_cparams(("parallel", "arbitrary")),
    )(hn3, w_mix, w_mix, *args)
    return y, _diag_blocks(sl), cvo[:, 8 - (GD_CONV - 1):]


def _merge_kernel(ya_ref, yb_ref, yc_ref, yd_ref, x_ref, nw_ref, wg_ref, wbr_ref, wout_ref, o_ref):
    x = x_ref[...]
    ms = jnp.mean(x * x, axis=-1, keepdims=True)
    h = (x * lax.rsqrt(ms + NORM_EPS) * nw_ref[...]).astype(BF16)
    m = None
    for b, y_ref in enumerate((ya_ref, yb_ref, yc_ref, yd_ref)):
        gate = jnp.dot(h, wg_ref[:, b * D_MODEL:(b + 1) * D_MODEL], preferred_element_type=F32)
        br = jnp.dot(y_ref[...], wbr_ref[b], preferred_element_type=F32)
        term = (0.5 * jnp.tanh(0.5 * gate) + 0.5) * br
        m = term if m is None else m + term
    o_ref[...] = x + jnp.dot(m.astype(BF16), wout_ref[...], preferred_element_type=F32)


def _merge(ys, x2, norm_w, w_gate, w_branch, w_out, tm):
    n = x2.shape[0]
    yspec = pl.BlockSpec((tm, BRANCH_W), lambda i: (i, 0))
    once = pl.Buffered(1)
    return pl.pallas_call(
        _merge_kernel,
        grid=(n // tm,),
        in_specs=[yspec, yspec, yspec, yspec,
                  pl.BlockSpec((tm, D_MODEL), lambda i: (i, 0)),
                  pl.BlockSpec((1, D_MODEL), lambda i: (0, 0)),
                  pl.BlockSpec((D_MODEL, 4 * D_MODEL), lambda i: (0, 0), pipeline_mode=once),
                  pl.BlockSpec((4, BRANCH_W, D_MODEL), lambda i: (0, 0, 0), pipeline_mode=once),
                  pl.BlockSpec((D_MODEL, D_MODEL), lambda i: (0, 0), pipeline_mode=once)],
        out_specs=pl.BlockSpec((tm, D_MODEL), lambda i: (i, 0)),
        out_shape=jax.ShapeDtypeStruct((n, D_MODEL), F32),
        compiler_params=_cparams(("parallel",)),
    )(*[y.reshape(n, BRANCH_W) for y in ys], x2, norm_w.reshape(1, D_MODEL), w_gate,
      w_branch.astype(BF16), w_out.astype(BF16))


def _ffn_kernel(x_ref, nw_ref, w1_ref, w3_ref, w2_ref, nnw_ref, o_ref, hn_ref):
    x = x_ref[...]
    ms = jnp.mean(x * x, axis=-1, keepdims=True)
    h = (x * lax.rsqrt(ms + NORM_EPS) * nw_ref[...]).astype(BF16)
    a = _silu(jnp.dot(h, w1_ref[...], preferred_element_type=F32)) * jnp.dot(h, w3_ref[...], preferred_element_type=F32)
    y = x + jnp.dot(a.astype(BF16), w2_ref[...], preferred_element_type=F32)
    o_ref[...] = y
    ms = jnp.mean(y * y, axis=-1, keepdims=True)
    hn_ref[...] = (y * lax.rsqrt(ms + NORM_EPS) * nnw_ref[...]).astype(BF16)


def _ffn(x2, norm_w, w1, w3, w2, next_norm_w, tm):
    n = x2.shape[0]
    dff = w1.shape[1]
    once = pl.Buffered(1)
    spec = pl.BlockSpec((tm, D_MODEL), lambda i: (i, 0))
    vec = pl.BlockSpec((1, D_MODEL), lambda i: (0, 0))
    return pl.pallas_call(
        _ffn_kernel,
        grid=(n // tm,),
        in_specs=[spec, vec,
                  pl.BlockSpec((D_MODEL, dff), lambda i: (0, 0), pipeline_mode=once),
                  pl.BlockSpec((D_MODEL, dff), lambda i: (0, 0), pipeline_mode=once),
                  pl.BlockSpec((dff, D_MODEL), lambda i: (0, 0), pipeline_mode=once), vec],
        out_specs=[spec, spec],
        out_shape=[jax.ShapeDtypeStruct((n, D_MODEL), F32), jax.ShapeDtypeStruct((n, D_MODEL), BF16)],
        compiler_params=_cparams(("parallel",)),
    )(x2, norm_w.reshape(1, D_MODEL), w1.astype(BF16), w3.astype(BF16), w2.astype(BF16),
      next_norm_w.reshape(1, D_MODEL))


SC_CORES = 2
SC_SUBCORES = 16
SC_WINDOW = 128
SC_PIECE = 256
MOE_BLOCK = 512
MOE_FF = 1792
ROUTER_ROWS = 256
HALF = D_MODEL // 2


HIGH16 = -65536


def _pack_bf16_pairs(h):
    bits = lax.bitcast_convert_type(h.astype(BF16).astype(F32), jnp.int32)
    return lax.shift_right_logical(bits[:, :HALF], 16) | (bits[:, HALF:] & HIGH16)


def _unpack_bf16_pairs(pieces):
    lo = [lax.bitcast_convert_type(lax.shift_left(w, 16), F32) for w in pieces]
    hi = [lax.bitcast_convert_type(w & HIGH16, F32) for w in pieces]
    return jnp.concatenate(lo + hi, axis=1).astype(BF16)


def _route_kernel(x_ref, nw_ref, wr_ref, br_ref, hp_ref, tok_ref, cnt_ref, seen_ref, *, tm):
    @pl.when(pl.program_id(0) == 0)
    def _():
        seen_ref[...] = jnp.zeros_like(seen_ref)

    x = x_ref[...]
    ms = jnp.mean(x * x, axis=-1, keepdims=True)
    h = x * lax.rsqrt(ms + NORM_EPS) * nw_ref[...]
    hp = _pack_bf16_pairs(h)
    for q in range(HALF // SC_PIECE):
        hp_ref[q] = hp[:, q * SC_PIECE:(q + 1) * SC_PIECE]
    lane = _iota((tm, LANES), 1)
    logits = jnp.where(lane < N_EXPERTS, _dot_hp(h, wr_ref[...]) + br_ref[...], -jnp.inf)
    m1 = jnp.max(logits, axis=1, keepdims=True)
    i1 = jnp.min(jnp.where(logits == m1, lane, LANES), axis=1, keepdims=True)
    rest = jnp.where(lane == i1, -jnp.inf, logits)
    m2 = jnp.max(rest, axis=1, keepdims=True)
    i2 = jnp.min(jnp.where(rest == m2, lane, LANES), axis=1, keepdims=True)
    e2 = jnp.exp(m2 - m1)
    g1 = 1.0 / (1.0 + e2)
    g2 = e2 / (1.0 + e2)
    self32 = ((lane == i1) | (lane == i2)).astype(F32)
    seen = seen_ref[...]
    tri = (_iota((tm, tm), 1) < _iota((tm, tm), 0)).astype(BF16)
    rank = jnp.dot(tri, self32.astype(BF16), preferred_element_type=F32) + seen
    r1 = jnp.sum(jnp.where(lane == i1, rank, 0.0), axis=1, keepdims=True)
    r2 = jnp.sum(jnp.where(lane == i2, rank, 0.0), axis=1, keepdims=True)
    cols = (i1.astype(F32), i2.astype(F32), r1, r2, g1, g2)
    tok = jnp.zeros((tm, LANES), F32)
    for j, col in enumerate(cols):
        tok = jnp.where(lane == j, col, tok)
    tok_ref[...] = tok
    seen = seen + jnp.sum(self32, axis=0, keepdims=True)
    seen_ref[...] = seen
    cnt_ref[...] = jnp.broadcast_to(seen, (8, LANES))


def _route(x2, norm_w, w_router, b_router, tm):
    n = x2.shape[0]
    wr = jnp.pad(w_router, ((0, 0), (0, LANES - N_EXPERTS)))
    br = jnp.pad(b_router, (0, LANES - N_EXPERTS)).reshape(1, LANES)
    return pl.pallas_call(
        functools.partial(_route_kernel, tm=tm),
        grid=(n // tm,),
        in_specs=[pl.BlockSpec((tm, D_MODEL), lambda i: (i, 0)),
                  pl.BlockSpec((1, D_MODEL), lambda i: (0, 0)),
                  pl.BlockSpec((D_MODEL, LANES), lambda i: (0, 0)),
                  pl.BlockSpec((1, LANES), lambda i: (0, 0))],
        out_specs=[pl.BlockSpec((HALF // SC_PIECE, tm, SC_PIECE), lambda i: (0, i, 0)),
                   pl.BlockSpec((tm, LANES), lambda i: (i, 0)),
                   pl.BlockSpec((8, LANES), lambda i: (0, 0))],
        out_shape=[jax.ShapeDtypeStruct((HALF // SC_PIECE, n, SC_PIECE), jnp.int32),
                   jax.ShapeDtypeStruct((n, LANES), F32),
                   jax.ShapeDtypeStruct((8, LANES), F32)],
        scratch_shapes=[pltpu.VMEM((1, LANES), F32)],
        compiler_params=_cparams(("arbitrary",)),
    )(x2, norm_w.reshape(1, D_MODEL), wr, br)


def _sc_rows_multiple(d):
    return SC_CORES * SC_SUBCORES * SC_WINDOW * SC_PIECE // d


def _sc_gather_rows(table, idx):
    f, v, _ = table.shape
    b = idx.shape[0]
    assert b % _sc_rows_multiple(f * SC_PIECE) == 0
    idx_all = (idx[None, :] + (jnp.arange(f, dtype=jnp.int32) * v)[:, None]).reshape(-1)
    return _sc_gather_pieces(table.reshape(f * v, SC_PIECE), idx_all).reshape(f, b, SC_PIECE)


def _sc_gather_pieces(table, idx):
    bp = idx.shape[0]
    d = table.shape[1]
    window = SC_WINDOW
    idx2 = idx.reshape(1, bp)
    mesh = plsc.VectorSubcoreMesh(core_axis_name="core", subcore_axis_name="subcore")

    @functools.partial(pl.kernel, out_type=jax.ShapeDtypeStruct((bp, d), table.dtype), mesh=mesh)
    def gather(x_hbm, i_hbm, o_hbm):
        def body(i_vmem, o_vmem):
            pltpu.sync_copy(x_hbm.at[i_vmem.at[0]], o_vmem)

        pltpu.emit_pipeline(
            body,
            grid=(bp // window,),
            in_specs=[pl.BlockSpec((1, window), index_map=lambda i: (0, i))],
            out_specs=[pl.BlockSpec((window, d), index_map=lambda i: (i, 0))],
            core_axis_name=("core", "subcore"),
            dimension_semantics=(pltpu.PARALLEL,),
        )(i_hbm, o_hbm)

    return gather(table, idx2)


def _sc_scatter_rows(rows, pos, n_slots):
    f, n0, _ = rows.shape
    n = -(-n0 // _sc_rows_multiple(D_MODEL)) * _sc_rows_multiple(D_MODEL)
    if n != n0:
        rows = jnp.pad(rows, ((0, 0), (0, n - n0), (0, 0)))
        pos = jnp.pad(pos.reshape(2, n0), ((0, 0), (0, n - n0)), constant_values=n_slots - 1)
    nb = n // SC_WINDOW
    assert (f * 2 * nb) % (SC_CORES * SC_SUBCORES) == 0
    idx = (pos.reshape(1, 2 * n) + (jnp.arange(f, dtype=jnp.int32) * n_slots)[:, None]).reshape(1, f * 2 * n)
    mesh = plsc.VectorSubcoreMesh(core_axis_name="core", subcore_axis_name="subcore")

    @functools.partial(pl.kernel, out_type=jax.ShapeDtypeStruct((f * n_slots, SC_PIECE), rows.dtype), mesh=mesh)
    def scatter(x_hbm, i_hbm, o_hbm):
        def body(x_vmem, i_vmem):
            pltpu.sync_copy(x_vmem, o_hbm.at[i_vmem.at[0]])

        pltpu.emit_pipeline(
            body,
            grid=(f * 2 * nb,),
            in_specs=[pl.BlockSpec((SC_WINDOW, SC_PIECE), index_map=lambda i: ((i // (2 * nb)) * nb + i % nb, 0)),
                      pl.BlockSpec((1, SC_WINDOW), index_map=lambda i: (0, i))],
            out_specs=[],
            core_axis_name=("core", "subcore"),
            dimension_semantics=(pltpu.PARALLEL,),
        )(x_hbm, i_hbm)

    return scatter(rows.reshape(f * n, SC_PIECE), idx).reshape(f, n_slots, SC_PIECE)


def _experts_kernel(be_ref, nb_ref, nv_ref, xs_ref, w1_ref, w3_ref, w2_ref, o_ref):
    g, c = pl.program_id(0), pl.program_id(1)

    @pl.when(c == 0)
    def _():
        o_ref[...] = jnp.zeros_like(o_ref)

    @pl.when(g < nb_ref[0])
    def _():
        live = _iota((xs_ref.shape[1], 1), 0) < nv_ref[g]
        xb = _unpack_bf16_pairs([jnp.where(live, xs_ref[q], 0) for q in range(HALF // SC_PIECE)])
        a = (_silu(jnp.dot(xb, w1_ref[...], preferred_element_type=F32))
             * jnp.dot(xb, w3_ref[...], preferred_element_type=F32))
        y = jnp.dot(a.astype(BF16), w2_ref[...], preferred_element_type=F32)
        for q in range(D_MODEL // SC_PIECE):
            o_ref[q] += y[:, q * SC_PIECE:(q + 1) * SC_PIECE]


def _experts(xs, block_expert, n_blocks, block_rows, w1, w3, w2, blk, tf):
    n_slots = xs.shape[1]
    dff = w1.shape[2]
    grid_spec = pltpu.PrefetchScalarGridSpec(
        num_scalar_prefetch=3,
        grid=(n_slots // blk, dff // tf),
        in_specs=[pl.BlockSpec((HALF // SC_PIECE, blk, SC_PIECE), lambda g, c, be, nb, nv: (0, g, 0)),
                  pl.BlockSpec((None, D_MODEL, tf), lambda g, c, be, nb, nv: (be[g], 0, c)),
                  pl.BlockSpec((None, D_MODEL, tf), lambda g, c, be, nb, nv: (be[g], 0, c)),
                  pl.BlockSpec((None, tf, D_MODEL), lambda g, c, be, nb, nv: (be[g], c, 0))],
        out_specs=pl.BlockSpec((D_MODEL // SC_PIECE, blk, SC_PIECE), lambda g, c, be, nb, nv: (0, g, 0)),
    )
    return pl.pallas_call(
        _experts_kernel,
        grid_spec=grid_spec,
        out_shape=jax.ShapeDtypeStruct((D_MODEL // SC_PIECE, n_slots, SC_PIECE), F32),
        compiler_params=_cparams(("parallel", "arbitrary")),
    )(block_expert, n_blocks, block_rows, xs, w1, w3, w2)


def _combine_kernel(x_ref, y1_ref, y2_ref, tok_ref, nw_ref, o_ref):
    tok = tok_ref[...]
    rows = lambda y_ref: jnp.concatenate([y_ref[q] for q in range(D_MODEL // SC_PIECE)], axis=1)
    x = x_ref[...] + tok[:, 4:5] * rows(y1_ref) + tok[:, 5:6] * rows(y2_ref)
    ms = jnp.mean(x * x, axis=-1, keepdims=True)
    o_ref[...] = x * lax.rsqrt(ms + NORM_EPS) * nw_ref[...]


def _combine(x2, ys2, tok, norm_w, tm):
    n = x2.shape[0]
    spec = pl.BlockSpec((tm, D_MODEL), lambda i: (i, 0))
    yspec = lambda off: pl.BlockSpec((D_MODEL // SC_PIECE, tm, SC_PIECE), lambda i: (0, i + off, 0))
    return pl.pallas_call(
        _combine_kernel,
        grid=(n // tm,),
        in_specs=[spec, yspec(0), yspec(n // tm),
                  pl.BlockSpec((tm, LANES), lambda i: (i, 0)), pl.BlockSpec((1, D_MODEL), lambda i: (0, 0))],
        out_specs=spec,
        out_shape=jax.ShapeDtypeStruct((n, D_MODEL), F32),
        compiler_params=_cparams(("parallel",)),
    )(x2, ys2, ys2, tok, norm_w.reshape(1, D_MODEL))


def _moe_final(x2, norm_w, w_router, b_router, w1, w3, w2, final_norm_w):
    n = x2.shape[0]
    blk = min(MOE_BLOCK, n)
    hp, tok, cnt = _route(x2, norm_w, w_router, b_router, ROUTER_ROWS)
    counts = cnt[0, :N_EXPERTS].astype(jnp.int32)
    padded = (counts + blk - 1) // blk * blk
    ends = jnp.cumsum(padded)
    base = ends - padded
    round_up = lambda a, m: -(-a // m) * m
    n_slots = round_up((-(-2 * n // blk) + N_EXPERTS + 1) * blk, _sc_rows_multiple(HALF))
    n_blocks = n_slots // blk
    i12 = tok[:, 0:2].astype(jnp.int32)
    pos = (base[i12] + tok[:, 2:4].astype(jnp.int32)).T.reshape(-1)
    block_start = jnp.arange(n_blocks, dtype=jnp.int32) * blk
    block_expert = jnp.minimum(jnp.sum(block_start[:, None] >= ends[None, :], axis=1), N_EXPERTS - 1).astype(jnp.int32)
    block_rows = jnp.clip((base + counts)[block_expert] - block_start, 0, blk).astype(jnp.int32)
    xs = _sc_scatter_rows(hp, pos, n_slots)
    ys = _experts(xs, block_expert, (ends[-1:] // blk).astype(jnp.int32), block_rows,
                  w1.astype(BF16), w3.astype(BF16), w2.astype(BF16), blk, MOE_FF)
    pos = jnp.pad(pos, (0, round_up(2 * n, _sc_rows_multiple(D_MODEL)) - 2 * n))
    ys2 = _sc_gather_rows(ys, pos)
    return _combine(x2, ys2, tok, final_norm_w, min(1024, n))


def _permute_w_in(w):
    s5, rw, sg = w[:, 0:256], w[:, 256:1280], w[:, 1280:1792]
    gd, ab, gates = w[:, 1792:2816], w[:, 2816:2824], w[:, 2824:6920]
    pad = jnp.zeros((D_MODEL, PROJ_COLS - COL_AB - 8), w.dtype)
    return jnp.concatenate([rw, gd, sg, s5, ab, pad], axis=1).astype(BF16), gates.astype(BF16)


def _run_trunk(x, s5_h, rw_s, rw_shift, gd_s, gd_conv, p, w_in_perm):
    bsz, t, _ = x.shape
    n = bsz * t
    tm = min(1024, n)
    l = min(MIX_CHUNK, t)
    tc_s5 = min(256, t)
    new = ([], [], [], [], [], [])
    x2 = x.reshape(n, D_MODEL)
    for layer in range(2):
        g = lambda name: p[name][layer]
        w_mix, w_gate = w_in_perm[layer]
        if layer == 0:
            hn = _norm_cast(x2, g('norm1_w'), tm)
        hn3 = hn.reshape(bsz, t, D_MODEL)
        y_a, s5_new = _s5_mixer(hn3, w_mix, s5_h[layer], g('s5_lam_re'), g('s5_lam_im'), g('s5_log_dt'), g('s5_b'),
                                g('s5_c'), g('s5_d'), g('s5_w_glu'), g('s5_b_glu'), tc_s5)
        y_b, rw_new, shift_new = _rwkv_mixer(hn3, w_mix, rw_shift[layer], rw_s[layer], g('rw_mu'), g('rw_w0'),
                                             g('rw_w2'), g('rw_a0'), g('rw_a2'), g('rw_g2'), g('rw_k_k'),
                                             g('rw_k_a'), g('rw_r_k'), g('rw_ln_w'), g('rw_ln_b'), l)
        y_c, sg_v = _sgu_mixer(hn3, w_mix, g('sg_ln_w'), g('sg_ln_b'), g('sg_w_s'), g('sg_b_s'))
        y_d, gd_new, conv_new = _gdn_mixer(hn3, w_mix, gd_conv[layer], gd_s[layer], g('gd_conv_w'), g('gd_a_log'),
                                           g('gd_dt_bias'), g('gd_norm_w'), l)
        x2 = _merge((y_a, y_b, y_c, y_d), x2, g('norm1_w'), w_gate, g('w_branch'), g('w_out'), min(512, n))
        j = layer // 2
        if layer % 2 == 0:
            x2, hn = _ffn(x2, g('norm2_w'), p['ffn_w1'][j], p['ffn_w3'][j], p['ffn_w2'][j],
                          p['norm1_w'][layer + 1], min(512, n))
        else:
            y = _moe_final(x2, g('norm2_w'), p['moe_router'][j], p['moe_router_b'][j],
                           p['moe_w1'][j], p['moe_w3'][j], p['moe_w2'][j], p['final_norm_w'])
        for lst, s in zip(new, (s5_new, rw_new, shift_new, gd_new, conv_new, sg_v)):
            lst.append(s)
    return y.reshape(bsz, t, D_MODEL), [jnp.stack(lst) for lst in new]


def kernel(x_prompt, x_sample, state_s5, state_rwkv, state_rwkv_shift, state_gdn, state_gdn_conv, norm1_w, w_in, s5_lam_re, s5_lam_im, s5_log_dt, s5_b, s5_c, s5_d, s5_w_glu, s5_b_glu, rw_mu, rw_w0, rw_w2, rw_a0, rw_a2, rw_g2, rw_k_k, rw_k_a, rw_r_k, rw_ln_w, rw_ln_b, sg_ln_w, sg_ln_b, sg_w_s, sg_b_s, gd_conv_w, gd_a_log, gd_dt_bias, gd_norm_w, w_branch, w_out, norm2_w, ffn_w1, ffn_w3, ffn_w2, moe_router, moe_router_b, moe_w1, moe_w3, moe_w2, final_norm_w):
    p = {
        'norm1_w': norm1_w, 's5_lam_re': s5_lam_re, 's5_lam_im': s5_lam_im, 's5_log_dt': s5_log_dt,
        's5_b': s5_b, 's5_c': s5_c, 's5_d': s5_d, 's5_w_glu': s5_w_glu, 's5_b_glu': s5_b_glu,
        'rw_mu': rw_mu, 'rw_w0': rw_w0, 'rw_w2': rw_w2, 'rw_a0': rw_a0, 'rw_a2': rw_a2, 'rw_g2': rw_g2,
        'rw_k_k': rw_k_k, 'rw_k_a': rw_k_a, 'rw_r_k': rw_r_k, 'rw_ln_w': rw_ln_w, 'rw_ln_b': rw_ln_b,
        'sg_ln_w': sg_ln_w, 'sg_ln_b': sg_ln_b, 'sg_w_s': sg_w_s, 'sg_b_s': sg_b_s,
        'gd_conv_w': gd_conv_w, 'gd_a_log': gd_a_log, 'gd_dt_bias': gd_dt_bias, 'gd_norm_w': gd_norm_w,
        'w_branch': w_branch, 'w_out': w_out, 'norm2_w': norm2_w,
        'ffn_w1': ffn_w1, 'ffn_w3': ffn_w3, 'ffn_w2': ffn_w2,
        'moe_router': moe_router, 'moe_router_b': moe_router_b, 'moe_w1': moe_w1, 'moe_w3': moe_w3, 'moe_w2': moe_w2,
        'final_norm_w': final_norm_w,
    }
    w_in_perm = [_permute_w_in(w_in[layer]) for layer in range(2)]
    bp, dt = x_prompt.shape[0], x_prompt.dtype
    depth = w_in.shape[0]
    y_prompt, (s5_p, rw_p, rwsh_p, gd_p, gdc_p, _) = _run_trunk(
        x_prompt,
        jnp.zeros((depth, bp, S5_GROUPS, S5_STATE, 2), dt),
        jnp.zeros((depth, bp, HEADS, HEAD_W, HEAD_W), dt),
        jnp.zeros((depth, bp, RW_COLS), dt),
        jnp.zeros((depth, bp, HEADS, HEAD_W, HEAD_W), dt),
        jnp.zeros((depth, bp, GD_CONV - 1, GD_QKV), dt),
        p, w_in_perm)
    y_sample, (s5_s, rw_s, rwsh_s, gd_s, gdc_s, sgv_s) = _run_trunk(
        x_sample, state_s5, state_rwkv, state_rwkv_shift, state_gdn, state_gdn_conv, p, w_in_perm)
    return (y_prompt, y_sample, s5_p, rw_p, rwsh_p, gd_p, gdc_p, s5_s, rw_s, rwsh_s, gd_s, gdc_s, sgv_s)
```

```python
import functools

import jax
import jax.numpy as jnp
from jax import lax
from jax.experimental import pallas as pl
from jax.experimental.pallas import tpu as pltpu
from jax.experimental.pallas import tpu_sc as plsc

F32 = jnp.float32
BF16 = jnp.bfloat16

D_MODEL = 1024
BRANCH_W = 256
HEADS = 4
HEAD_W = 64
S5_GROUPS = 16
S5_GROUP = 16
S5_STATE = 64
S5_W = S5_GROUPS * S5_STATE
S5_ROWS = 8
SG_CHUNK = 128
SG_ROWS = 512
GD_CONV = 4
GD_QKV = 3 * BRANCH_W
RW_COLS = 1024
RW_EPS = 64e-5
NORM_EPS = 1e-6
N_EXPERTS = 8
LANES = 128

COL_RW = 0
COL_GD = 1024
COL_SG = 2048
COL_S5 = 2560
COL_AB = 2816
PROJ_COLS = 3072

VMEM_LIMIT = 48 * 1024 * 1024


def _cparams(sem):
    return pltpu.CompilerParams(dimension_semantics=sem, vmem_limit_bytes=VMEM_LIMIT)


def _dot(a, b):
    return jnp.dot(a.astype(BF16), b.astype(BF16), preferred_element_type=F32)


def _dot_nt(a, b):
    return lax.dot_general(a.astype(BF16), b.astype(BF16), (((1,), (1,)), ((), ())),
                           preferred_element_type=F32)


def _split3(a):
    hi = a.astype(BF16)
    r1 = a - hi.astype(F32)
    mid = r1.astype(BF16)
    lo = (r1 - mid.astype(F32)).astype(BF16)
    return hi, mid, lo


def _dot3_left(b_exact, a):
    hi, mid, lo = _split3(a)
    b = b_exact.astype(BF16)
    return (jnp.dot(b, hi, preferred_element_type=F32) + jnp.dot(b, mid, preferred_element_type=F32)
            + jnp.dot(b, lo, preferred_element_type=F32))


def _dot_hp(a, b):
    a0, a1, _ = _split3(a)
    b0, b1, _ = _split3(b)
    d = lambda x, y: jnp.dot(x, y, preferred_element_type=F32)
    return d(a0, b0) + (d(a0, b1) + d(a1, b0))


def _iota(shape, axis):
    return lax.broadcasted_iota(jnp.int32, shape, axis)


def _head_ones():
    r = _iota((BRANCH_W, BRANCH_W), 0) // HEAD_W
    c = _iota((BRANCH_W, BRANCH_W), 1) // HEAD_W
    return (r == c).astype(BF16)


def _head_mask(l):
    r = _iota((HEADS * l, BRANCH_W), 0) // l
    c = _iota((HEADS * l, BRANCH_W), 1) // HEAD_W
    return r == c


def _expand(x, mask):
    return jnp.where(mask, jnp.concatenate([x] * HEADS, axis=0), 0.0)


def _tri_masks(l):
    i = _iota((l, HEADS * l), 0)
    j = _iota((l, HEADS * l), 1) % l
    return j < i, j <= i


def _same_head(l):
    n = HEADS * l
    return (_iota((n, n), 0) // l) == (_iota((n, n), 1) // l)


def _expand_sq(x, same):
    return jnp.where(same, jnp.concatenate([x] * HEADS, axis=0), 0.0)


def _unit_lower_inverse(a_strict, l):
    i = _iota((l, HEADS * l), 0)
    j = _iota((l, HEADS * l), 1) % l
    eye = (i == j).astype(F32)
    same = _same_head(l)
    p = [-a for a in a_strict]
    t = [eye + x for x in p]
    k = 2
    while k < l:
        p = [_dot(x, _expand_sq(x, same)) for x in p]
        t = [y + _dot(y, _expand_sq(x, same)) for y, x in zip(t, p)]
        k *= 2
    return t


def _dot2(a, b_exact):
    hi = a.astype(BF16)
    lo = (a - hi.astype(F32)).astype(BF16)
    b = b_exact.astype(BF16)
    return jnp.dot(hi, b, preferred_element_type=F32) + jnp.dot(lo, b, preferred_element_type=F32)


def _cumsum_rows(x, l, nb=1):
    n = nb * l
    i, j = _iota((n, n), 0), _iota((n, n), 1)
    tri = ((j <= i) & ((i // l) == (j // l))).astype(BF16)
    return _dot3_left(tri, x)


def _softplus(x):
    return jnp.maximum(x, 0.0) + jnp.log(1.0 + jnp.exp(-jnp.abs(x)))


def _silu(x):
    return x * jax.nn.sigmoid(x)


def _norm_cast_kernel(x_ref, nw_ref, o_ref):
    x = x_ref[...]
    ms = jnp.mean(x * x, axis=-1, keepdims=True)
    o_ref[...] = (x * lax.rsqrt(ms + NORM_EPS) * nw_ref[...]).astype(BF16)


def _norm_cast(x2, norm_w, tm):
    n = x2.shape[0]
    spec = pl.BlockSpec((tm, D_MODEL), lambda i: (i, 0))
    return pl.pallas_call(
        _norm_cast_kernel,
        grid=(n // tm,),
        in_specs=[spec, pl.BlockSpec((1, D_MODEL), lambda i: (0, 0))],
        out_specs=spec,
        out_shape=jax.ShapeDtypeStruct((n, D_MODEL), BF16),
        compiler_params=_cparams(("parallel",)),
    )(x2, norm_w.reshape(1, D_MODEL))


def _s5_kernel(hn_ref, wz_ref, h0_ref, wb_ref, wc_ref, a2k_ref, apow_ref, d_ref, wg_ref, bg_ref,
               y_ref, hl_ref, hs_ref, hb_ref, *, tc, bb):
    c = pl.program_id(1)

    @pl.when(c == 0)
    def _():
        hs_ref[...] = h0_ref[...]

    hn = jnp.concatenate([hn_ref[b] for b in range(bb)], axis=0)
    u = jnp.dot(hn, wz_ref[...], preferred_element_type=F32)
    x = _dot(u, wb_ref[...])
    xr, xi = x[:, :S5_W], x[:, S5_W:]
    ng = bb * tc // S5_ROWS
    xr = xr.reshape(ng, S5_ROWS, S5_W)
    xi = xi.reshape(ng, S5_ROWS, S5_W)
    row = _iota((S5_ROWS, 1), 0)
    k, d = 0, 1
    while d < S5_ROWS:
        m = row >= d
        ar = jnp.where(m, a2k_ref[k:k + 1, :S5_W], 0.0)
        ai = jnp.where(m, a2k_ref[k:k + 1, S5_W:], 0.0)
        sr = pltpu.roll(xr, d, axis=1)
        si = pltpu.roll(xi, d, axis=1)
        xr, xi = xr + (ar * sr - ai * si), xi + (ar * si + ai * sr)
        k, d = k + 1, d * 2
    hb_ref[:, :S5_W] = xr.reshape(bb * tc, S5_W)
    hb_ref[:, S5_W:] = xi.reshape(bb * tc, S5_W)
    pr, pi_ = apow_ref[:, :S5_W], apow_ref[:, S5_W:]

    def group(gi, carry):
        out = []
        for b, (cr, ci) in enumerate(carry):
            rows = pl.ds(pl.multiple_of(b * tc + gi * S5_ROWS, S5_ROWS), S5_ROWS)
            hr = hb_ref[rows, :S5_W] + (pr * cr - pi_ * ci)
            hi = hb_ref[rows, S5_W:] + (pr * ci + pi_ * cr)
            hb_ref[rows, :S5_W] = hr
            hb_ref[rows, S5_W:] = hi
            out.append((hr[S5_ROWS - 1:S5_ROWS], hi[S5_ROWS - 1:S5_ROWS]))
        return tuple(out)

    state = tuple((hs_ref[b, :, :S5_W], hs_ref[b, :, S5_W:]) for b in range(bb))
    state = lax.fori_loop(0, tc // S5_ROWS, group, state, unroll=2)
    for b, (cr, ci) in enumerate(state):
        hs_ref[b, :, :S5_W] = cr
        hs_ref[b, :, S5_W:] = ci
    y = _dot(hb_ref[:, :S5_W], wc_ref[:S5_W]) + _dot(hb_ref[:, S5_W:], wc_ref[S5_W:])
    y = jax.nn.gelu(y + d_ref[...] * u)
    y = (y * jax.nn.sigmoid(_dot(y, wg_ref[...]) + bg_ref[...])).astype(y_ref.dtype)
    for b in range(bb):
        y_ref[b] = y[b * tc:(b + 1) * tc]

    @pl.when(c == pl.num_programs(1) - 1)
    def _():
        hl_ref[...] = hs_ref[...]


def _s5_tables(lam_re, lam_im, log_dt, b_c, c_c, tc):
    dt = jnp.exp(log_dt)[:, None]
    mag = jnp.exp(lam_re * dt)
    ab_re, ab_im = mag * jnp.cos(lam_im * dt), mag * jnp.sin(lam_im * dt)
    den = lam_re * lam_re + lam_im * lam_im
    nr = ab_re - 1.0
    cf_re = (nr * lam_re + ab_im * lam_im) / den
    cf_im = (ab_im * lam_re - nr * lam_im) / den
    br, bi = b_c[..., 0], b_c[..., 1]
    bb_re = cf_re[..., None] * br - cf_im[..., None] * bi
    bb_im = cf_re[..., None] * bi + cf_im[..., None] * br
    eye = jnp.eye(S5_GROUPS, dtype=F32)
    bd_in = lambda m: jnp.einsum('gph,gk->ghkp', m, eye).reshape(BRANCH_W, S5_W)
    wb = jnp.concatenate([bd_in(bb_re), bd_in(bb_im)], axis=1)
    cr, ci = c_c[..., 0], c_c[..., 1]
    bd_out = lambda m: jnp.einsum('ghp,gk->gpkh', m, eye).reshape(S5_W, BRANCH_W)
    wc = jnp.concatenate([bd_out(cr), -bd_out(ci)], axis=0)
    pr, pi_ = ab_re.reshape(1, S5_W), ab_im.reshape(1, S5_W)
    lv_r, lv_i = [], []
    tr, ti = pr, pi_
    d = 1
    while d < tc:
        lv_r.append(pr)
        lv_i.append(pi_)
        tr, ti = (jnp.concatenate([tr, tr * pr - ti * pi_], axis=0),
                  jnp.concatenate([ti, tr * pi_ + ti * pr], axis=0))
        pr, pi_ = pr * pr - pi_ * pi_, 2.0 * pr * pi_
        d *= 2
    n_lv = len(lv_r)
    pad = (-n_lv) % 8
    a2k = jnp.concatenate([jnp.concatenate(lv_r, axis=0), jnp.concatenate(lv_i, axis=0)], axis=1)
    a2k = jnp.pad(a2k, ((0, pad), (0, 0)))
    apow = jnp.concatenate([tr, ti], axis=1)
    return wb.astype(BF16), wc.astype(BF16), a2k, apow


def _w_cols(col, width):
    return pl.BlockSpec((D_MODEL, width), lambda b, c: (0, col // width))


def _s5_mixer(hn3, w_mix, h0, lam_re, lam_im, log_dt, b_c, c_c, d_skip, w_glu, b_glu, tc):
    bsz, t, _ = hn3.shape
    wb, wc, a2k, apow = _s5_tables(lam_re, lam_im, log_dt, b_c, c_c, S5_ROWS)
    h0f = jnp.concatenate([h0[..., 0].reshape(bsz, 1, S5_W), h0[..., 1].reshape(bsz, 1, S5_W)], axis=-1)
    full = lambda a: pl.BlockSpec(a.shape, lambda b, c: (0,) * a.ndim)
    d2, bg2, wg = d_skip.reshape(1, BRANCH_W), b_glu.reshape(1, BRANCH_W), w_glu.astype(BF16)
    bb = min(bsz, STACK_STREAMS)
    while bsz % bb:
        bb -= 1
    y, hl = pl.pallas_call(
        functools.partial(_s5_kernel, tc=tc, bb=bb),
        grid=(bsz // bb, t // tc),
        in_specs=[pl.BlockSpec((bb, tc, D_MODEL), lambda b, c: (b, c, 0)), _w_cols(COL_S5, BRANCH_W),
                  pl.BlockSpec((bb, 1, 2 * S5_W), lambda b, c: (b, 0, 0)),
                  full(wb), full(wc), full(a2k), full(apow), full(d2), full(wg), full(bg2)],
        out_specs=[pl.BlockSpec((bb, tc, BRANCH_W), lambda b, c: (b, c, 0)),
                   pl.BlockSpec((bb, 1, 2 * S5_W), lambda b, c: (b, 0, 0))],
        out_shape=[jax.ShapeDtypeStruct((bsz, t, BRANCH_W), BF16),
                   jax.ShapeDtypeStruct((bsz, 1, 2 * S5_W), F32)],
        scratch_shapes=[pltpu.VMEM((bb, 1, 2 * S5_W), F32), pltpu.VMEM((bb * tc, 2 * S5_W), F32)],
        compiler_params=_cparams(("parallel", "arbitrary")),
    )(hn3, w_mix, h0f, wb, wc, a2k, apow, d2, wg, bg2)
    h_last = jnp.stack([hl[:, 0, :S5_W].reshape(bsz, S5_GROUPS, S5_STATE),
                        hl[:, 0, S5_W:].reshape(bsz, S5_GROUPS, S5_STATE)], axis=-1)
    return y, h_last


def _rwkv_kernel(hn_ref, wz_ref, sh0_ref, s0_ref, mu_ref, w0_ref, w2_ref, a0_ref, a2_ref, g2_ref,
                 kk_ref, ka_ref, rk_ref, lnw_ref, lnb_ref,
                 y_ref, sl_ref, sho_ref, st_ref, zp_ref, *, l, bb, nt):
    c = pl.program_id(1)

    @pl.when(c == 0)
    def _():
        st_ref[...] = s0_ref[...]
        zp_ref[...] = sh0_ref[...]

    tl = nt * l
    z_all = jnp.dot(jnp.concatenate([hn_ref[b] for b in range(bb)], axis=0), wz_ref[...],
                    preferred_element_type=F32)
    row = _iota((tl, 1), 0)
    zms = []
    for b in range(bb):
        z = z_all[b * tl:(b + 1) * tl]
        prev = jnp.where(row == 0, zp_ref[b], pltpu.roll(z, 1, axis=0))
        zp_ref[b] = z[tl - 1:tl]
        zms.append(z + (prev - z) * mu_ref[...])
    zm = jnp.concatenate(zms, axis=0)
    r, k, v = zm[:, 0:256], zm[:, 256:512], zm[:, 512:768]
    lo = zm[:, 768:896]
    g_lo = zm[:, 896:1024]
    w_log = -_softplus(-(w0_ref[...] + _dot(jnp.tanh(lo), w2_ref[...]))) - 0.5
    lw = -jnp.exp(w_log)
    a = jax.nn.sigmoid(a0_ref[...] + _dot(lo, a2_ref[...]))
    g = _dot(jax.nn.sigmoid(g_lo), g2_ref[...])
    ones_h = _head_ones()
    kk = k * kk_ref[...]
    kk = kk * lax.rsqrt(_dot2(kk * kk, ones_h) + NORM_EPS)
    k = k * (1.0 + (a - 1.0) * ka_ref[...])
    kka = kk * a

    cum = _cumsum_rows(lw, l, bb * nt)
    p_incl = jnp.exp(cum)
    p_inv = jnp.exp(-cum)
    kt = kk * jnp.exp(cum - lw)
    rt = r * p_incl
    kh = k * p_inv
    ah = kka * p_inv

    hm = _head_mask(l)
    strict, incl = _tri_masks(l)
    chunks = [(b, j) for j in range(nt) for b in range(bb)]
    rows = {s: slice((s[0] * nt + s[1]) * l, (s[0] * nt + s[1] + 1) * l) for s in chunks}
    ex = lambda x: {s: _expand(x[rows[s]], hm) for s in chunks}
    kh_e, ah_e, v_e = ex(kh), ex(ah), ex(v)
    a_aa = [jnp.where(strict, _dot_nt(kt[rows[s]], ah_e[s]), 0.0) for s in chunks]
    a_ak = {s: jnp.where(strict, _dot_nt(kt[rows[s]], kh_e[s]), 0.0) for s in chunks}
    b_ra = {s: jnp.where(incl, _dot_nt(rt[rows[s]], ah_e[s]), 0.0) for s in chunks}
    b_rk = {s: jnp.where(incl, _dot_nt(rt[rows[s]], kh_e[s]), 0.0) for s in chunks}
    t_inv = dict(zip(chunks, _unit_lower_inverse(a_aa, l)))
    av = {s: _dot(a_ak[s], v_e[s]) for s in chunks}
    bv = {s: _dot(b_rk[s], v_e[s]) for s in chunks}
    to_end = {s: jnp.exp(cum[rows[s]][l - 1:l] - cum[rows[s]]) for s in chunks}
    head_blk = _same_head(HEAD_W)
    st = [st_ref[b] for b in range(bb)]
    yb = {}
    for j in range(nt):
        now = [(b, j) for b in range(bb)]
        rhs = {s: _dot_nt(kt[rows[s]], st[s[0]]) + av[s] for s in now}
        ys = {s: _dot_nt(rt[rows[s]], st[s[0]]) + bv[s] for s in now}
        u = {s: _dot(t_inv[s], _expand(rhs[s], hm)) for s in now}
        for s in now:
            yb[s] = ys[s] - _dot(b_ra[s], _expand(u[s], hm))
        lhs_t = {s: jnp.concatenate([v[rows[s]], -u[s]], axis=0).T for s in now}
        rhs_k = {s: jnp.concatenate([k[rows[s]] * to_end[s], kka[rows[s]] * to_end[s]], axis=0) for s in now}
        for s in now:
            p_last = p_incl[rows[s]][l - 1:l]
            st[s[0]] = st[s[0]] * p_last + jnp.where(head_blk, _dot(lhs_t[s], rhs_k[s]), 0.0)
    for b in range(bb):
        st_ref[b] = st[b]
    y = jnp.concatenate([yb[(b, j)] for b in range(bb) for j in range(nt)], axis=0)

    inv_w = 1.0 / HEAD_W
    mean = _dot2(y, ones_h) * inv_w
    yc = y - mean
    var = _dot2(yc * yc, ones_h) * inv_w
    y = yc * lax.rsqrt(var + RW_EPS) * lnw_ref[...] + lnb_ref[...]
    bonus = _dot2(r * k * rk_ref[...], ones_h) * v
    y = ((y + bonus) * g).astype(y_ref.dtype)
    for b in range(bb):
        y_ref[b] = y[b * tl:(b + 1) * tl]

    @pl.when(c == pl.num_programs(1) - 1)
    def _():
        sl_ref[...] = st_ref[...]
        sho_ref[...] = zp_ref[...]


MIX_CHUNK = 64
STACK_STREAMS = 4
STACK_ROWS = 512


def _streams_per_step(bsz, t, l):
    bb = min(bsz, STACK_STREAMS)
    while bsz % bb:
        bb -= 1
    nt = max(1, min(t, STACK_ROWS // bb) // l)
    while (t // l) % nt:
        nt -= 1
    return bb, nt


def _block_diag_heads(s):
    bsz = s.shape[0]
    eye = jnp.eye(HEADS, dtype=s.dtype)
    return jnp.einsum('bhij,hg->bhigj', s, eye).reshape(bsz, BRANCH_W, BRANCH_W)


def _diag_blocks(s):
    bsz = s.shape[0]
    s5 = s.reshape(bsz, HEADS, HEAD_W, HEADS, HEAD_W)
    return jnp.stack([s5[:, h, :, h, :] for h in range(HEADS)], axis=1)


def _rwkv_mixer(hn3, w_mix, shift0, s0, mu, w0, w2, a0, a2, g2, k_k, k_a, r_k, ln_w, ln_b, l):
    bsz, t, _ = hn3.shape
    row = lambda a: a.reshape(1, -1)
    w2p = jnp.concatenate([w2, jnp.zeros_like(w2)], axis=0).astype(BF16)
    a2p = jnp.concatenate([jnp.zeros_like(a2), a2], axis=0).astype(BF16)
    args = (shift0.reshape(bsz, 1, RW_COLS), _block_diag_heads(s0), row(mu), row(w0), w2p, row(a0), a2p,
            g2.astype(BF16), row(k_k), row(k_a), row(r_k), row(ln_w), row(ln_b))
    full = lambda a: pl.BlockSpec(a.shape, lambda b, c: (0,) * a.ndim)
    bb, nt = _streams_per_step(bsz, t, l)
    y, sl, sho = pl.pallas_call(
        functools.partial(_rwkv_kernel, l=l, bb=bb, nt=nt),
        grid=(bsz // bb, t // (nt * l)),
        in_specs=[pl.BlockSpec((bb, nt * l, D_MODEL), lambda b, c: (b, c, 0)), _w_cols(COL_RW, RW_COLS),
                  pl.BlockSpec((bb, 1, RW_COLS), lambda b, c: (b, 0, 0)),
                  pl.BlockSpec((bb, BRANCH_W, BRANCH_W), lambda b, c: (b, 0, 0))]
                 + [full(a) for a in args[2:]],
        out_specs=[pl.BlockSpec((bb, nt * l, BRANCH_W), lambda b, c: (b, c, 0)),
                   pl.BlockSpec((bb, BRANCH_W, BRANCH_W), lambda b, c: (b, 0, 0)),
                   pl.BlockSpec((bb, 1, RW_COLS), lambda b, c: (b, 0, 0))],
        out_shape=[jax.ShapeDtypeStruct((bsz, t, BRANCH_W), BF16),
                   jax.ShapeDtypeStruct((bsz, BRANCH_W, BRANCH_W), F32),
                   jax.ShapeDtypeStruct((bsz, 1, RW_COLS), F32)],
        scratch_shapes=[pltpu.VMEM((bb, BRANCH_W, BRANCH_W), F32), pltpu.VMEM((bb, 1, RW_COLS), F32)],
        compiler_params=_cparams(("parallel", "arbitrary")),
    )(hn3, w_mix, *args)
    return y, _diag_blocks(sl), sho[:, 0]


def _sgu_kernel(hn_ref, wz_ref, lnw_ref, lnb_ref, wm_ref, bias_ref, o_ref, v_ref, *, l, nc):
    z = jnp.dot(hn_ref[...], wz_ref[...], preferred_element_type=F32)
    zg = jax.nn.gelu(z)
    u, v = zg[:, :BRANCH_W], zg[:, BRANCH_W:]
    mean = jnp.mean(v, axis=-1, keepdims=True)
    vc = v - mean
    var = jnp.mean(vc * vc, axis=-1, keepdims=True)
    v = vc * lax.rsqrt(var + NORM_EPS) * lnw_ref[...] + lnb_ref[...]
    v_ref[...] = v
    hm = _head_mask(l)
    wm = wm_ref[...]
    for i in range(nc):
        rows = slice(i * l, (i + 1) * l)
        mixed = bias_ref[...] + _dot(wm, _expand(v[rows], hm))
        o_ref[rows, :] = (u[rows] * mixed).astype(o_ref.dtype)


def _sgu_mixer(hn3, w_mix, ln_w, ln_b, w_s, b_s):
    bsz, t, _ = hn3.shape
    l = min(SG_CHUNK, t)
    nc = max(1, min(SG_ROWS, t) // l)
    tril = jnp.tril(jnp.ones((l, l), F32))
    wm = jnp.transpose(w_s[:, :l, :l] * tril, (1, 0, 2)).reshape(l, HEADS * l).astype(BF16)
    bias = jnp.repeat(jnp.transpose(b_s[:, :l]), HEAD_W, axis=1)
    row = lambda a: a.reshape(1, -1)
    full = lambda a: pl.BlockSpec(a.shape, lambda b, c: (0,) * a.ndim)
    args = (row(ln_w), row(ln_b), wm, bias)
    return pl.pallas_call(
        functools.partial(_sgu_kernel, l=l, nc=nc),
        grid=(bsz, t // (nc * l)),
        in_specs=[pl.BlockSpec((None, nc * l, D_MODEL), lambda b, c: (b, c, 0)), _w_cols(COL_SG, 2 * BRANCH_W)]
                 + [full(a) for a in args],
        out_specs=[pl.BlockSpec((None, nc * l, BRANCH_W), lambda b, c: (b, c, 0)),
                   pl.BlockSpec((None, nc * l, BRANCH_W), lambda b, c: (b, c, 0))],
        out_shape=[jax.ShapeDtypeStruct((bsz, t, BRANCH_W), BF16),
                   jax.ShapeDtypeStruct((bsz, t, BRANCH_W), F32)],
        compiler_params=_cparams(("parallel", "parallel")),
    )(hn3, w_mix, *args)


def _gdn_kernel(hn_ref, wz_ref, wab_ref, cv0_ref, s0_ref, cw_ref, alog_ref, dtb_ref, nw_ref,
                y_ref, sl_ref, cvo_ref, st_ref, cv_ref, *, l, bb, nt):
    c = pl.program_id(1)

    @pl.when(c == 0)
    def _():
        st_ref[...] = s0_ref[...]
        cv_ref[...] = cv0_ref[...]

    tl = nt * l
    hn = jnp.concatenate([hn_ref[b] for b in range(bb)], axis=0)
    z_all = jnp.dot(hn, wz_ref[...], preferred_element_type=F32)
    ab = jnp.dot(hn, wab_ref[...], preferred_element_type=F32)
    row8 = _iota((8, 1), 0)
    convs, gates = [], []
    for b in range(bb):
        z = z_all[b * tl:(b + 1) * tl]
        qkv = z[:, :GD_QKV]
        gates.append(z[:, GD_QKV:])
        carry = cv_ref[b]
        cv_ref[b] = qkv[tl - 8:tl]
        conv = qkv * cw_ref[GD_CONV - 1:GD_CONV]
        for j in range(1, GD_CONV):
            sh = pltpu.roll(qkv, j, axis=0)
            top = jnp.where(row8 < j, pltpu.roll(carry, j, axis=0), sh[:8])
            sh = jnp.concatenate([top, sh[8:]], axis=0) if tl > 8 else top
            conv = conv + sh * cw_ref[GD_CONV - 1 - j:GD_CONV - j]
        convs.append(conv)
    conv = _silu(jnp.concatenate(convs, axis=0))
    gate = jnp.concatenate(gates, axis=0)
    q, k, v = conv[:, :256], conv[:, 256:512], conv[:, 512:768]
    ones_h = _head_ones()
    q = q * lax.rsqrt(_dot2(q * q, ones_h) + NORM_EPS) * (HEAD_W ** -0.5)
    k = k * lax.rsqrt(_dot2(k * k, ones_h) + NORM_EPS)
    lane_h = _iota((bb * tl, BRANCH_W), 1) // HEAD_W
    a_in = jnp.zeros((bb * tl, BRANCH_W), F32)
    b_in = jnp.zeros((bb * tl, BRANCH_W), F32)
    for h in range(HEADS):
        a_in = jnp.where(lane_h == h, ab[:, h:h + 1], a_in)
        b_in = jnp.where(lane_h == h, ab[:, HEADS + h:HEADS + h + 1], b_in)
    beta = jax.nn.sigmoid(b_in)
    g = -jnp.exp(alog_ref[...]) * _softplus(a_in + dtb_ref[...])
    gc = _cumsum_rows(g, l, bb * nt)
    eg = jnp.exp(gc)
    kb = k * beta
    vb = v * beta
    kbg = kb * eg
    qg = q * eg

    hm = _head_mask(l)
    strict, incl = _tri_masks(l)
    n = HEADS * l
    chunks = [(b, j) for j in range(nt) for b in range(bb)]
    rows = {s: slice((s[0] * nt + s[1]) * l, (s[0] * nt + s[1] + 1) * l) for s in chunks}
    ex = lambda x: {s: _expand(x[rows[s]], hm) for s in chunks}
    k_e, vb_e, kbg_e = ex(k), ex(vb), ex(kbg)
    lane_hd = _iota((l, n), 1) // l
    eye = _iota((l, n), 0) == (_iota((l, n), 1) % l)
    decay = {}
    for s in chunks:
        gi = jnp.zeros((l, n), F32)
        for h in range(HEADS):
            gi = jnp.where(lane_hd == h, gc[rows[s]][:, h * HEAD_W:h * HEAD_W + 1], gi)
        gj = jnp.sum(jnp.where(eye, gi, 0.0), axis=0, keepdims=True)
        decay[s] = jnp.where(incl, jnp.exp(jnp.where(incl, gi - gj, 0.0)), 0.0)
    lm = [jnp.where(strict, _dot_nt(kb[rows[s]], k_e[s]) * decay[s], 0.0) for s in chunks]
    qk = {s: _dot_nt(q[rows[s]], k_e[s]) * decay[s] for s in chunks}
    t_inv = dict(zip(chunks, _unit_lower_inverse(lm, l)))
    uc = {s: _dot(t_inv[s], vb_e[s]) for s in chunks}
    wc = {s: _dot(t_inv[s], kbg_e[s]) for s in chunks}
    g_last = {s: gc[rows[s]][l - 1:l] for s in chunks}
    k_dec = {s: (k[rows[s]] * jnp.exp(g_last[s] - gc[rows[s]])).T for s in chunks}
    head_blk = _same_head(HEAD_W)
    st = [st_ref[b] for b in range(bb)]
    ob = {}
    for j in range(nt):
        now = [(b, j) for b in range(bb)]
        o0 = {s: _dot(qg[rows[s]], st[s[0]]) for s in now}
        v_new = {s: uc[s] - _dot(wc[s], st[s[0]]) for s in now}
        for s in now:
            ob[s] = o0[s] + _dot(qk[s], _expand(v_new[s], hm))
        for s in now:
            st[s[0]] = st[s[0]] * jnp.exp(g_last[s]) + jnp.where(head_blk, _dot(k_dec[s], v_new[s]), 0.0)
    for b in range(bb):
        st_ref[b] = st[b]
    o = jnp.concatenate([ob[(b, j)] for b in range(bb) for j in range(nt)], axis=0)
    ms = _dot2(o * o, ones_h) * (1.0 / HEAD_W)
    o = (o * lax.rsqrt(ms + NORM_EPS) * nw_ref[...] * _silu(gate)).astype(y_ref.dtype)
    for b in range(bb):
        y_ref[b] = o[b * tl:(b + 1) * tl]

    @pl.when(c == pl.num_programs(1) - 1)
    def _():
        sl_ref[...] = st_ref[...]
        cvo_ref[...] = cv_ref[...]


def _gdn_mixer(hn3, w_mix, conv0, s0, conv_w, a_log, dt_bias, norm_w, l):
    bsz, t, _ = hn3.shape
    cv0 = jnp.pad(conv0, ((0, 0), (8 - (GD_CONV - 1), 0), (0, 0)))
    cw = jnp.pad(conv_w, ((0, 8 - GD_CONV), (0, 0)))
    per_head = lambda a: jnp.repeat(a, HEAD_W).reshape(1, BRANCH_W)
    args = (cv0, _block_diag_heads(s0), cw, per_head(a_log), per_head(dt_bias),
            jnp.tile(norm_w, HEADS).reshape(1, BRANCH_W))
    full = lambda a: pl.BlockSpec(a.shape, lambda b, c: (0,) * a.ndim)
    bb, nt = _streams_per_step(bsz, t, l)
    y, sl, cvo = pl.pallas_call(
        functools.partial(_gdn_kernel, l=l, bb=bb, nt=nt),
        grid=(bsz // bb, t // (nt * l)),
        in_specs=[pl.BlockSpec((bb, nt * l, D_MODEL), lambda b, c: (b, c, 0)),
                  _w_cols(COL_GD, 1024), _w_cols(COL_AB, LANES),
                  pl.BlockSpec((bb, 8, GD_QKV), lambda b, c: (b, 0, 0)),
                  pl.BlockSpec((bb, BRANCH_W, BRANCH_W), lambda b, c: (b, 0, 0))]
                 + [full(a) for a in args[2:]],
        out_specs=[pl.BlockSpec((bb, nt * l, BRANCH_W), lambda b, c: (b, c, 0)),
                   pl.BlockSpec((bb, BRANCH_W, BRANCH_W), lambda b, c: (b, 0, 0)),
                   pl.BlockSpec((bb, 8, GD_QKV), lambda b, c: (b, 0, 0))],
        out_shape=[jax.ShapeDtypeStruct((bsz, t, BRANCH_W), BF16),
                   jax.ShapeDtypeStruct((bsz, BRANCH_W, BRANCH_W), F32),
                   jax.ShapeDtypeStruct((bsz, 8, GD_QKV), F32)],
        scratch_shapes=[pltpu.VMEM((bb, BRANCH_W, BRANCH_W), F32), pltpu.VMEM((bb, 8, GD_QKV), F32)],
        compiler_params=_cparams(("parallel", "arbitrary")),
    )(hn3, w_mix, w_mix, *args)
    return y, _diag_blocks(sl), cvo[:, 8 - (GD_CONV - 1):]


def _merge_kernel(ya_ref, yb_ref, yc_ref, yd_ref, x_ref, nw_ref, wg_ref, wbr_ref, wout_ref, o_ref):
    x = x_ref[...]
    ms = jnp.mean(x * x, axis=-1, keepdims=True)
    h = (x * lax.rsqrt(ms + NORM_EPS) * nw_ref[...]).astype(BF16)
    m = None
    for b, y_ref in enumerate((ya_ref, yb_ref, yc_ref, yd_ref)):
        gate = jnp.dot(h, wg_ref[:, b * D_MODEL:(b + 1) * D_MODEL], preferred_element_type=F32)
        br = jnp.dot(y_ref[...], wbr_ref[b], preferred_element_type=F32)
        term = (0.5 * jnp.tanh(0.5 * gate) + 0.5) * br
        m = term if m is None else m + term
    o_ref[...] = x + jnp.dot(m.astype(BF16), wout_ref[...], preferred_element_type=F32)


def _merge(ys, x2, norm_w, w_gate, w_branch, w_out, tm):
    n = x2.shape[0]
    yspec = pl.BlockSpec((tm, BRANCH_W), lambda i: (i, 0))
    once = pl.Buffered(1)
    return pl.pallas_call(
        _merge_kernel,
        grid=(n // tm,),
        in_specs=[yspec, yspec, yspec, yspec,
                  pl.BlockSpec((tm, D_MODEL), lambda i: (i, 0)),
                  pl.BlockSpec((1, D_MODEL), lambda i: (0, 0)),
                  pl.BlockSpec((D_MODEL, 4 * D_MODEL), lambda i: (0, 0), pipeline_mode=once),
                  pl.BlockSpec((4, BRANCH_W, D_MODEL), lambda i: (0, 0, 0), pipeline_mode=once),
                  pl.BlockSpec((D_MODEL, D_MODEL), lambda i: (0, 0), pipeline_mode=once)],
        out_specs=pl.BlockSpec((tm, D_MODEL), lambda i: (i, 0)),
        out_shape=jax.ShapeDtypeStruct((n, D_MODEL), F32),
        compiler_params=_cparams(("parallel",)),
    )(*[y.reshape(n, BRANCH_W) for y in ys], x2, norm_w.reshape(1, D_MODEL), w_gate,
      w_branch.astype(BF16), w_out.astype(BF16))


def _ffn_kernel(x_ref, nw_ref, w1_ref, w3_ref, w2_ref, nnw_ref, o_ref, hn_ref):
    x = x_ref[...]
    ms = jnp.mean(x * x, axis=-1, keepdims=True)
    h = (x * lax.rsqrt(ms + NORM_EPS) * nw_ref[...]).astype(BF16)
    a = _silu(jnp.dot(h, w1_ref[...], preferred_element_type=F32)) * jnp.dot(h, w3_ref[...], preferred_element_type=F32)
    y = x + jnp.dot(a.astype(BF16), w2_ref[...], preferred_element_type=F32)
    o_ref[...] = y
    ms = jnp.mean(y * y, axis=-1, keepdims=True)
    hn_ref[...] = (y * lax.rsqrt(ms + NORM_EPS) * nnw_ref[...]).astype(BF16)


def _ffn(x2, norm_w, w1, w3, w2, next_norm_w, tm):
    n = x2.shape[0]
    dff = w1.shape[1]
    once = pl.Buffered(1)
    spec = pl.BlockSpec((tm, D_MODEL), lambda i: (i, 0))
    vec = pl.BlockSpec((1, D_MODEL), lambda i: (0, 0))
    return pl.pallas_call(
        _ffn_kernel,
        grid=(n // tm,),
        in_specs=[spec, vec,
                  pl.BlockSpec((D_MODEL, dff), lambda i: (0, 0), pipeline_mode=once),
                  pl.BlockSpec((D_MODEL, dff), lambda i: (0, 0), pipeline_mode=once),
                  pl.BlockSpec((dff, D_MODEL), lambda i: (0, 0), pipeline_mode=once), vec],
        out_specs=[spec, spec],
        out_shape=[jax.ShapeDtypeStruct((n, D_MODEL), F32), jax.ShapeDtypeStruct((n, D_MODEL), BF16)],
        compiler_params=_cparams(("parallel",)),
    )(x2, norm_w.reshape(1, D_MODEL), w1.astype(BF16), w3.astype(BF16), w2.astype(BF16),
      next_norm_w.reshape(1, D_MODEL))


SC_CORES = 2
SC_SUBCORES = 16
SC_WINDOW = 128
SC_PIECE = 256
MOE_BLOCK = 512
MOE_FF = 1792
ROUTER_ROWS = 256
HALF = D_MODEL // 2


HIGH16 = -65536


def _pack_bf16_pairs(h):
    bits = lax.bitcast_convert_type(h.astype(BF16).astype(F32), jnp.int32)
    return lax.shift_right_logical(bits[:, :HALF], 16) | (bits[:, HALF:] & HIGH16)


def _unpack_bf16_pairs(pieces):
    lo = [lax.bitcast_convert_type(lax.shift_left(w, 16), F32) for w in pieces]
    hi = [lax.bitcast_convert_type(w & HIGH16, F32) for w in pieces]
    return jnp.concatenate(lo + hi, axis=1).astype(BF16)


def _route_kernel(x_ref, nw_ref, wr_ref, br_ref, hp_ref, tok_ref, cnt_ref, seen_ref, *, tm):
    @pl.when(pl.program_id(0) == 0)
    def _():
        seen_ref[...] = jnp.zeros_like(seen_ref)

    x = x_ref[...]
    ms = jnp.mean(x * x, axis=-1, keepdims=True)
    h = x * lax.rsqrt(ms + NORM_EPS) * nw_ref[...]
    hp = _pack_bf16_pairs(h)
    for q in range(HALF // SC_PIECE):
        hp_ref[q] = hp[:, q * SC_PIECE:(q + 1) * SC_PIECE]
    lane = _iota((tm, LANES), 1)
    logits = jnp.where(lane < N_EXPERTS, _dot_hp(h, wr_ref[...]) + br_ref[...], -jnp.inf)
    m1 = jnp.max(logits, axis=1, keepdims=True)
    i1 = jnp.min(jnp.where(logits == m1, lane, LANES), axis=1, keepdims=True)
    rest = jnp.where(lane == i1, -jnp.inf, logits)
    m2 = jnp.max(rest, axis=1, keepdims=True)
    i2 = jnp.min(jnp.where(rest == m2, lane, LANES), axis=1, keepdims=True)
    e2 = jnp.exp(m2 - m1)
    g1 = 1.0 / (1.0 + e2)
    g2 = e2 / (1.0 + e2)
    self32 = ((lane == i1) | (lane == i2)).astype(F32)
    seen = seen_ref[...]
    tri = (_iota((tm, tm), 1) < _iota((tm, tm), 0)).astype(BF16)
    rank = jnp.dot(tri, self32.astype(BF16), preferred_element_type=F32) + seen
    r1 = jnp.sum(jnp.where(lane == i1, rank, 0.0), axis=1, keepdims=True)
    r2 = jnp.sum(jnp.where(lane == i2, rank, 0.0), axis=1, keepdims=True)
    cols = (i1.astype(F32), i2.astype(F32), r1, r2, g1, g2)
    tok = jnp.zeros((tm, LANES), F32)
    for j, col in enumerate(cols):
        tok = jnp.where(lane == j, col, tok)
    tok_ref[...] = tok
    seen = seen + jnp.sum(self32, axis=0, keepdims=True)
    seen_ref[...] = seen
    cnt_ref[...] = jnp.broadcast_to(seen, (8, LANES))


def _route(x2, norm_w, w_router, b_router, tm):
    n = x2.shape[0]
    wr = jnp.pad(w_router, ((0, 0), (0, LANES - N_EXPERTS)))
    br = jnp.pad(b_router, (0, LANES - N_EXPERTS)).reshape(1, LANES)
    return pl.pallas_call(
        functools.partial(_route_kernel, tm=tm),
        grid=(n // tm,),
        in_specs=[pl.BlockSpec((tm, D_MODEL), lambda i: (i, 0)),
                  pl.BlockSpec((1, D_MODEL), lambda i: (0, 0)),
                  pl.BlockSpec((D_MODEL, LANES), lambda i: (0, 0)),
                  pl.BlockSpec((1, LANES), lambda i: (0, 0))],
        out_specs=[pl.BlockSpec((HALF // SC_PIECE, tm, SC_PIECE), lambda i: (0, i, 0)),
                   pl.BlockSpec((tm, LANES), lambda i: (i, 0)),
                   pl.BlockSpec((8, LANES), lambda i: (0, 0))],
        out_shape=[jax.ShapeDtypeStruct((HALF // SC_PIECE, n, SC_PIECE), jnp.int32),
                   jax.ShapeDtypeStruct((n, LANES), F32),
                   jax.ShapeDtypeStruct((8, LANES), F32)],
        scratch_shapes=[pltpu.VMEM((1, LANES), F32)],
        compiler_params=_cparams(("arbitrary",)),
    )(x2, norm_w.reshape(1, D_MODEL), wr, br)


def _sc_rows_multiple(d):
    return SC_CORES * SC_SUBCORES * SC_WINDOW * SC_PIECE // d


def _sc_gather_rows(table, idx):
    f, v, _ = table.shape
    b = idx.shape[0]
    assert b % _sc_rows_multiple(f * SC_PIECE) == 0
    idx_all = (idx[None, :] + (jnp.arange(f, dtype=jnp.int32) * v)[:, None]).reshape(-1)
    return _sc_gather_pieces(table.reshape(f * v, SC_PIECE), idx_all).reshape(f, b, SC_PIECE)


def _sc_gather_pieces(table, idx):
    bp = idx.shape[0]
    d = table.shape[1]
    window = SC_WINDOW
    idx2 = idx.reshape(1, bp)
    mesh = plsc.VectorSubcoreMesh(core_axis_name="core", subcore_axis_name="subcore")

    @functools.partial(pl.kernel, out_type=jax.ShapeDtypeStruct((bp, d), table.dtype), mesh=mesh)
    def gather(x_hbm, i_hbm, o_hbm):
        def body(i_vmem, o_vmem):
            pltpu.sync_copy(x_hbm.at[i_vmem.at[0]], o_vmem)

        pltpu.emit_pipeline(
            body,
            grid=(bp // window,),
            in_specs=[pl.BlockSpec((1, window), index_map=lambda i: (0, i))],
            out_specs=[pl.BlockSpec((window, d), index_map=lambda i: (i, 0))],
            core_axis_name=("core", "subcore"),
            dimension_semantics=(pltpu.PARALLEL,),
        )(i_hbm, o_hbm)

    return gather(table, idx2)


def _sc_scatter_rows(rows, pos, n_slots):
    f, n0, _ = rows.shape
    n = -(-n0 // _sc_rows_multiple(D_MODEL)) * _sc_rows_multiple(D_MODEL)
    if n != n0:
        rows = jnp.pad(rows, ((0, 0), (0, n - n0), (0, 0)))
        pos = jnp.pad(pos.reshape(2, n0), ((0, 0), (0, n - n0)), constant_values=n_slots - 1)
    nb = n // SC_WINDOW
    assert (f * 2 * nb) % (SC_CORES * SC_SUBCORES) == 0
    idx = (pos.reshape(1, 2 * n) + (jnp.arange(f, dtype=jnp.int32) * n_slots)[:, None]).reshape(1, f * 2 * n)
    mesh = plsc.VectorSubcoreMesh(core_axis_name="core", subcore_axis_name="subcore")

    @functools.partial(pl.kernel, out_type=jax.ShapeDtypeStruct((f * n_slots, SC_PIECE), rows.dtype), mesh=mesh)
    def scatter(x_hbm, i_hbm, o_hbm):
        def body(x_vmem, i_vmem):
            pltpu.sync_copy(x_vmem, o_hbm.at[i_vmem.at[0]])

        pltpu.emit_pipeline(
            body,
            grid=(f * 2 * nb,),
            in_specs=[pl.BlockSpec((SC_WINDOW, SC_PIECE), index_map=lambda i: ((i // (2 * nb)) * nb + i % nb, 0)),
                      pl.BlockSpec((1, SC_WINDOW), index_map=lambda i: (0, i))],
            out_specs=[],
            core_axis_name=("core", "subcore"),
            dimension_semantics=(pltpu.PARALLEL,),
        )(x_hbm, i_hbm)

    return scatter(rows.reshape(f * n, SC_PIECE), idx).reshape(f, n_slots, SC_PIECE)


def _experts_kernel(be_ref, nb_ref, nv_ref, xs_ref, w1_ref, w3_ref, w2_ref, o_ref):
    g, c = pl.program_id(0), pl.program_id(1)

    @pl.when(c == 0)
    def _():
        o_ref[...] = jnp.zeros_like(o_ref)

    @pl.when(g < nb_ref[0])
    def _():
        live = _iota((xs_ref.shape[1], 1), 0) < nv_ref[g]
        xb = _unpack_bf16_pairs([jnp.where(live, xs_ref[q], 0) for q in range(HALF // SC_PIECE)])
        a = (_silu(jnp.dot(xb, w1_ref[...], preferred_element_type=F32))
             * jnp.dot(xb, w3_ref[...], preferred_element_type=F32))
        y = jnp.dot(a.astype(BF16), w2_ref[...], preferred_element_type=F32)
        for q in range(D_MODEL // SC_PIECE):
            o_ref[q] += y[:, q * SC_PIECE:(q + 1) * SC_PIECE]


def _experts(xs, block_expert, n_blocks, block_rows, w1, w3, w2, blk, tf):
    n_slots = xs.shape[1]
    dff = w1.shape[2]
    grid_spec = pltpu.PrefetchScalarGridSpec(
        num_scalar_prefetch=3,
        grid=(n_slots // blk, dff // tf),
        in_specs=[pl.BlockSpec((HALF // SC_PIECE, blk, SC_PIECE), lambda g, c, be, nb, nv: (0, g, 0)),
                  pl.BlockSpec((None, D_MODEL, tf), lambda g, c, be, nb, nv: (be[g], 0, c)),
                  pl.BlockSpec((None, D_MODEL, tf), lambda g, c, be, nb, nv: (be[g], 0, c)),
                  pl.BlockSpec((None, tf, D_MODEL), lambda g, c, be, nb, nv: (be[g], c, 0))],
        out_specs=pl.BlockSpec((D_MODEL // SC_PIECE, blk, SC_PIECE), lambda g, c, be, nb, nv: (0, g, 0)),
    )
    return pl.pallas_call(
        _experts_kernel,
        grid_spec=grid_spec,
        out_shape=jax.ShapeDtypeStruct((D_MODEL // SC_PIECE, n_slots, SC_PIECE), F32),
        compiler_params=_cparams(("parallel", "arbitrary")),
    )(block_expert, n_blocks, block_rows, xs, w1, w3, w2)


def _combine_kernel(x_ref, y1_ref, y2_ref, tok_ref, nw_ref, o_ref):
    tok = tok_ref[...]
    rows = lambda y_ref: jnp.concatenate([y_ref[q] for q in range(D_MODEL // SC_PIECE)], axis=1)
    x = x_ref[...] + tok[:, 4:5] * rows(y1_ref) + tok[:, 5:6] * rows(y2_ref)
    ms = jnp.mean(x * x, axis=-1, keepdims=True)
    o_ref[...] = x * lax.rsqrt(ms + NORM_EPS) * nw_ref[...]


def _combine(x2, ys2, tok, norm_w, tm):
    n = x2.shape[0]
    spec = pl.BlockSpec((tm, D_MODEL), lambda i: (i, 0))
    yspec = lambda off: pl.BlockSpec((D_MODEL // SC_PIECE, tm, SC_PIECE), lambda i: (0, i + off, 0))
    return pl.pallas_call(
        _combine_kernel,
        grid=(n // tm,),
        in_specs=[spec, yspec(0), yspec(n // tm),
                  pl.BlockSpec((tm, LANES), lambda i: (i, 0)), pl.BlockSpec((1, D_MODEL), lambda i: (0, 0))],
        out_specs=spec,
        out_shape=jax.ShapeDtypeStruct((n, D_MODEL), F32),
        compiler_params=_cparams(("parallel",)),
    )(x2, ys2, ys2, tok, norm_w.reshape(1, D_MODEL))


def _moe_final(x2, norm_w, w_router, b_router, w1, w3, w2, final_norm_w):
    n = x2.shape[0]
    blk = min(MOE_BLOCK, n)
    hp, tok, cnt = _route(x2, norm_w, w_router, b_router, ROUTER_ROWS)
    counts = cnt[0, :N_EXPERTS].astype(jnp.int32)
    padded = (counts + blk - 1) // blk * blk
    ends = jnp.cumsum(padded)
    base = ends - padded
    round_up = lambda a, m: -(-a // m) * m
    n_slots = round_up((-(-2 * n // blk) + N_EXPERTS + 1) * blk, _sc_rows_multiple(HALF))
    n_blocks = n_slots // blk
    i12 = tok[:, 0:2].astype(jnp.int32)
    pos = (base[i12] + tok[:, 2:4].astype(jnp.int32)).T.reshape(-1)
    block_start = jnp.arange(n_blocks, dtype=jnp.int32) * blk
    block_expert = jnp.minimum(jnp.sum(block_start[:, None] >= ends[None, :], axis=1), N_EXPERTS - 1).astype(jnp.int32)
    block_rows = jnp.clip((base + counts)[block_expert] - block_start, 0, blk).astype(jnp.int32)
    xs = _sc_scatter_rows(hp, pos, n_slots)
    ys = _experts(xs, block_expert, (ends[-1:] // blk).astype(jnp.int32), block_rows,
                  w1.astype(BF16), w3.astype(BF16), w2.astype(BF16), blk, MOE_FF)
    pos = jnp.pad(pos, (0, round_up(2 * n, _sc_rows_multiple(D_MODEL)) - 2 * n))
    ys2 = _sc_gather_rows(ys, pos)
    return _combine(x2, ys2, tok, final_norm_w, min(1024, n))


def _permute_w_in(w):
    s5, rw, sg = w[:, 0:256], w[:, 256:1280], w[:, 1280:1792]
    gd, ab, gates = w[:, 1792:2816], w[:, 2816:2824], w[:, 2824:6920]
    pad = jnp.zeros((D_MODEL, PROJ_COLS - COL_AB - 8), w.dtype)
    return jnp.concatenate([rw, gd, sg, s5, ab, pad], axis=1).astype(BF16), gates.astype(BF16)


def _run_trunk(x, s5_h, rw_s, rw_shift, gd_s, gd_conv, p, w_in_perm):
    bsz, t, _ = x.shape
    n = bsz * t
    tm = min(1024, n)
    l = min(MIX_CHUNK, t)
    tc_s5 = min(256, t)
    new = ([], [], [], [], [], [])
    x2 = x.reshape(n, D_MODEL)
    for layer in range(2):
        g = lambda name: p[name][layer]
        w_mix, w_gate = w_in_perm[layer]
        if layer == 0:
            hn = _norm_cast(x2, g('norm1_w'), tm)
        hn3 = hn.reshape(bsz, t, D_MODEL)
        y_a, s5_new = _s5_mixer(hn3, w_mix, s5_h[layer], g('s5_lam_re'), g('s5_lam_im'), g('s5_log_dt'), g('s5_b'),
                                g('s5_c'), g('s5_d'), g('s5_w_glu'), g('s5_b_glu'), tc_s5)
        y_b, rw_new, shift_new = _rwkv_mixer(hn3, w_mix, rw_shift[layer], rw_s[layer], g('rw_mu'), g('rw_w0'),
                                             g('rw_w2'), g('rw_a0'), g('rw_a2'), g('rw_g2'), g('rw_k_k'),
                                             g('rw_k_a'), g('rw_r_k'), g('rw_ln_w'), g('rw_ln_b'), l)
        y_c, sg_v = _sgu_mixer(hn3, w_mix, g('sg_ln_w'), g('sg_ln_b'), g('sg_w_s'), g('sg_b_s'))
        y_d, gd_new, conv_new = _gdn_mixer(hn3, w_mix, gd_conv[layer], gd_s[layer], g('gd_conv_w'), g('gd_a_log'),
                                           g('gd_dt_bias'), g('gd_norm_w'), l)
        x2 = _merge((y_a, y_b, y_c, y_d), x2, g('norm1_w'), w_gate, g('w_branch'), g('w_out'), min(512, n))
        j = layer // 2
        if layer % 2 == 0:
            x2, hn = _ffn(x2, g('norm2_w'), p['ffn_w1'][j], p['ffn_w3'][j], p['ffn_w2'][j],
                          p['norm1_w'][layer + 1], min(512, n))
        else:
            y = _moe_final(x2, g('norm2_w'), p['moe_router'][j], p['moe_router_b'][j],
                           p['moe_w1'][j], p['moe_w3'][j], p['moe_w2'][j], p['final_norm_w'])
        for lst, s in zip(new, (s5_new, rw_new, shift_new, gd_new, conv_new, sg_v)):
            lst.append(s)
    return y.reshape(bsz, t, D_MODEL), [jnp.stack(lst) for lst in new]


def kernel(x_prompt, x_sample, state_s5, state_rwkv, state_rwkv_shift, state_gdn, state_gdn_conv, norm1_w, w_in, s5_lam_re, s5_lam_im, s5_log_dt, s5_b, s5_c, s5_d, s5_w_glu, s5_b_glu, rw_mu, rw_w0, rw_w2, rw_a0, rw_a2, rw_g2, rw_k_k, rw_k_a, rw_r_k, rw_ln_w, rw_ln_b, sg_ln_w, sg_ln_b, sg_w_s, sg_b_s, gd_conv_w, gd_a_log, gd_dt_bias, gd_norm_w, w_branch, w_out, norm2_w, ffn_w1, ffn_w3, ffn_w2, moe_router, moe_router_b, moe_w1, moe_w3, moe_w2, final_norm_w):
    p = {
        'norm1_w': norm1_w, 's5_lam_re': s5_lam_re, 's5_lam_im': s5_lam_im, 's5_log_dt': s5_log_dt,
        's5_b': s5_b, 's5_c': s5_c, 's5_d': s5_d, 's5_w_glu': s5_w_glu, 's5_b_glu': s5_b_glu,
        'rw_mu': rw_mu, 'rw_w0': rw_w0, 'rw_w2': rw_w2, 'rw_a0': rw_a0, 'rw_a2': rw_a2, 'rw_g2': rw_g2,
        'rw_k_k': rw_k_k, 'rw_k_a': rw_k_a, 'rw_r_k': rw_r_k, 'rw_ln_w': rw_ln_w, 'rw_ln_b': rw_ln_b,
        'sg_ln_w': sg_ln_w, 'sg_ln_b': sg_ln_b, 'sg_w_s': sg_w_s, 'sg_b_s': sg_b_s,
        'gd_conv_w': gd_conv_w, 'gd_a_log': gd_a_log, 'gd_dt_bias': gd_dt_bias, 'gd_norm_w': gd_norm_w,
        'w_branch': w_branch, 'w_out': w_out, 'norm2_w': norm2_w,
        'ffn_w1': ffn_w1, 'ffn_w3': ffn_w3, 'ffn_w2': ffn_w2,
        'moe_router': moe_router, 'moe_router_b': moe_router_b, 'moe_w1': moe_w1, 'moe_w3': moe_w3, 'moe_w2': moe_w2,
        'final_norm_w': final_norm_w,
    }
    w_in_perm = [_permute_w_in(w_in[layer]) for layer in range(2)]
    bp, dt = x_prompt.shape[0], x_prompt.dtype
    depth = w_in.shape[0]
    y_prompt, (s5_p, rw_p, rwsh_p, gd_p, gdc_p, _) = _run_trunk(
        x_prompt,
        jnp.zeros((depth, bp, S5_GROUPS, S5_STATE, 2), dt),
        jnp.zeros((depth, bp, HEADS, HEAD_W, HEAD_W), dt),
        jnp.zeros((depth, bp, RW_COLS), dt),
        jnp.zeros((depth, bp, HEADS, HEAD_W, HEAD_W), dt),
        jnp.zeros((depth, bp, GD_CONV - 1, GD_QKV), dt),
        p, w_in_perm)
    y_sample, (s5_s, rw_s, rwsh_s, gd_s, gdc_s, sgv_s) = _run_trunk(
        x_sample, state_s5, state_rwkv, state_rwkv_shift, state_gdn, state_gdn_conv, p, w_in_perm)
    return (y_prompt, y_sample, s5_p, rw_p, rwsh_p, gd_p, gdc_p, s5_s, rw_s, rwsh_s, gd_s, gdc_s, sgv_s)
```

```python
import functools

import jax
import jax.numpy as jnp
from jax import lax
from jax.experimental import pallas as pl
from jax.experimental.pallas import tpu as pltpu
from jax.experimental.pallas import tpu_sc as plsc

F32 = jnp.float32
BF16 = jnp.bfloat16

D_MODEL = 1024
BRANCH_W = 256
HEADS = 4
HEAD_W = 64
S5_GROUPS = 16
S5_STATE = 64
S5_W = S5_GROUPS * S5_STATE
S5_ROWS = 8
SG_CHUNK = 128
SG_ROWS = 512
GD_CONV = 4
GD_QKV = 3 * BRANCH_W
RW_COLS = 1024
RW_EPS = 64e-5
NORM_EPS = 1e-6
N_EXPERTS = 8
LANES = 128

COL_RW = 0
COL_GD = 1024
COL_SG = 2048
COL_S5 = 2560
COL_AB = 2816
PROJ_COLS = 3072

VMEM_LIMIT = 48 * 1024 * 1024


def _cparams(sem):
    return pltpu.CompilerParams(dimension_semantics=sem, vmem_limit_bytes=VMEM_LIMIT)


def _dot(a, b):
    return jnp.dot(a.astype(BF16), b.astype(BF16), preferred_element_type=F32)


def _dot_nt(a, b):
    return lax.dot_general(a.astype(BF16), b.astype(BF16), (((1,), (1,)), ((), ())),
                           preferred_element_type=F32)


def _split3(a):
    hi = a.astype(BF16)
    r1 = a - hi.astype(F32)
    mid = r1.astype(BF16)
    lo = (r1 - mid.astype(F32)).astype(BF16)
    return hi, mid, lo


def _dot3_left(b_exact, a):
    hi, mid, lo = _split3(a)
    b = b_exact.astype(BF16)
    return (jnp.dot(b, hi, preferred_element_type=F32) + jnp.dot(b, mid, preferred_element_type=F32)
            + jnp.dot(b, lo, preferred_element_type=F32))


def _dot_hp(a, b):
    a0, a1, _ = _split3(a)
    b0, b1, _ = _split3(b)
    d = lambda x, y: jnp.dot(x, y, preferred_element_type=F32)
    return d(a0, b0) + (d(a0, b1) + d(a1, b0))


def _iota(shape, axis):
    return lax.broadcasted_iota(jnp.int32, shape, axis)


def _head_ones():
    r = _iota((BRANCH_W, BRANCH_W), 0) // HEAD_W
    c = _iota((BRANCH_W, BRANCH_W), 1) // HEAD_W
    return (r == c).astype(BF16)


def _head_mask(l):
    r = _iota((HEADS * l, BRANCH_W), 0) // l
    c = _iota((HEADS * l, BRANCH_W), 1) // HEAD_W
    return r == c


def _expand(x, mask):
    return jnp.where(mask, jnp.concatenate([x] * HEADS, axis=0), 0.0)


def _tri_masks(l):
    i = _iota((l, HEADS * l), 0)
    j = _iota((l, HEADS * l), 1) % l
    return j < i, j <= i


def _same_head(l):
    n = HEADS * l
    return (_iota((n, n), 0) // l) == (_iota((n, n), 1) // l)


def _expand_sq(x, same):
    return jnp.where(same, jnp.concatenate([x] * HEADS, axis=0), 0.0)


def _unit_lower_inverse(a_strict, l):
    i = _iota((l, HEADS * l), 0)
    j = _iota((l, HEADS * l), 1) % l
    eye = (i == j).astype(F32)
    same = _same_head(l)
    p = [-a for a in a_strict]
    t = [eye + x for x in p]
    k = 2
    while k < l:
        p = [_dot(x, _expand_sq(x, same)) for x in p]
        t = [y + _dot(y, _expand_sq(x, same)) for y, x in zip(t, p)]
        k *= 2
    return t


def _dot2(a, b_exact):
    hi = a.astype(BF16)
    lo = (a - hi.astype(F32)).astype(BF16)
    b = b_exact.astype(BF16)
    return jnp.dot(hi, b, preferred_element_type=F32) + jnp.dot(lo, b, preferred_element_type=F32)


def _cumsum_rows(x, l, nb=1):
    n = nb * l
    i, j = _iota((n, n), 0), _iota((n, n), 1)
    tri = ((j <= i) & ((i // l) == (j // l))).astype(BF16)
    return _dot3_left(tri, x)


def _softplus(x):
    return jnp.maximum(x, 0.0) + jnp.log(1.0 + jnp.exp(-jnp.abs(x)))


def _silu(x):
    return x * jax.nn.sigmoid(x)


def _norm_cast_kernel(x_ref, nw_ref, o_ref):
    x = x_ref[...]
    ms = jnp.mean(x * x, axis=-1, keepdims=True)
    o_ref[...] = (x * lax.rsqrt(ms + NORM_EPS) * nw_ref[...]).astype(BF16)


def _norm_cast(x2, norm_w, tm):
    n = x2.shape[0]
    spec = pl.BlockSpec((tm, D_MODEL), lambda i: (i, 0))
    return pl.pallas_call(
        _norm_cast_kernel,
        grid=(n // tm,),
        in_specs=[spec, pl.BlockSpec((1, D_MODEL), lambda i: (0, 0))],
        out_specs=spec,
        out_shape=jax.ShapeDtypeStruct((n, D_MODEL), BF16),
        compiler_params=_cparams(("parallel",)),
    )(x2, norm_w.reshape(1, D_MODEL))


def _s5_kernel(hn_ref, wz_ref, h0_ref, wb_ref, wc_ref, a2k_ref, apow_ref, d_ref, wg_ref, bg_ref,
               y_ref, hl_ref, hs_ref, hb_ref, *, tc, bb):
    c = pl.program_id(1)

    @pl.when(c == 0)
    def _():
        hs_ref[...] = h0_ref[...]

    hn = jnp.concatenate([hn_ref[b] for b in range(bb)], axis=0)
    u = jnp.dot(hn, wz_ref[...], preferred_element_type=F32)
    x = _dot(u, wb_ref[...])
    xr, xi = x[:, :S5_W], x[:, S5_W:]
    ng = bb * tc // S5_ROWS
    xr = xr.reshape(ng, S5_ROWS, S5_W)
    xi = xi.reshape(ng, S5_ROWS, S5_W)
    row = _iota((S5_ROWS, 1), 0)
    k, d = 0, 1
    while d < S5_ROWS:
        m = row >= d
        ar = jnp.where(m, a2k_ref[k:k + 1, :S5_W], 0.0)
        ai = jnp.where(m, a2k_ref[k:k + 1, S5_W:], 0.0)
        sr = pltpu.roll(xr, d, axis=1)
        si = pltpu.roll(xi, d, axis=1)
        xr, xi = xr + (ar * sr - ai * si), xi + (ar * si + ai * sr)
        k, d = k + 1, d * 2
    hb_ref[:, :S5_W] = xr.reshape(bb * tc, S5_W)
    hb_ref[:, S5_W:] = xi.reshape(bb * tc, S5_W)
    pr, pi_ = apow_ref[:, :S5_W], apow_ref[:, S5_W:]

    def group(gi, carry):
        out = []
        for b, (cr, ci) in enumerate(carry):
            rows = pl.ds(pl.multiple_of(b * tc + gi * S5_ROWS, S5_ROWS), S5_ROWS)
            hr = hb_ref[rows, :S5_W] + (pr * cr - pi_ * ci)
            hi = hb_ref[rows, S5_W:] + (pr * ci + pi_ * cr)
            hb_ref[rows, :S5_W] = hr
            hb_ref[rows, S5_W:] = hi
            out.append((hr[S5_ROWS - 1:S5_ROWS], hi[S5_ROWS - 1:S5_ROWS]))
        return tuple(out)

    state = tuple((hs_ref[b, :, :S5_W], hs_ref[b, :, S5_W:]) for b in range(bb))
    state = lax.fori_loop(0, tc // S5_ROWS, group, state, unroll=2)
    for b, (cr, ci) in enumerate(state):
        hs_ref[b, :, :S5_W] = cr
        hs_ref[b, :, S5_W:] = ci
    y = _dot(hb_ref[:, :S5_W], wc_ref[:S5_W]) + _dot(hb_ref[:, S5_W:], wc_ref[S5_W:])
    y = jax.nn.gelu(y + d_ref[...] * u)
    y = (y * jax.nn.sigmoid(_dot(y, wg_ref[...]) + bg_ref[...])).astype(y_ref.dtype)
    for b in range(bb):
        y_ref[b] = y[b * tc:(b + 1) * tc]

    @pl.when(c == pl.num_programs(1) - 1)
    def _():
        hl_ref[...] = hs_ref[...]


def _s5_tables(lam_re, lam_im, log_dt, b_c, c_c, tc):
    dt = jnp.exp(log_dt)[:, None]
    mag = jnp.exp(lam_re * dt)
    ab_re, ab_im = mag * jnp.cos(lam_im * dt), mag * jnp.sin(lam_im * dt)
    den = lam_re * lam_re + lam_im * lam_im
    nr = ab_re - 1.0
    cf_re = (nr * lam_re + ab_im * lam_im) / den
    cf_im = (ab_im * lam_re - nr * lam_im) / den
    br, bi = b_c[..., 0], b_c[..., 1]
    bb_re = cf_re[..., None] * br - cf_im[..., None] * bi
    bb_im = cf_re[..., None] * bi + cf_im[..., None] * br
    eye = jnp.eye(S5_GROUPS, dtype=F32)
    bd_in = lambda m: jnp.einsum('gph,gk->ghkp', m, eye).reshape(BRANCH_W, S5_W)
    wb = jnp.concatenate([bd_in(bb_re), bd_in(bb_im)], axis=1)
    cr, ci = c_c[..., 0], c_c[..., 1]
    bd_out = lambda m: jnp.einsum('ghp,gk->gpkh', m, eye).reshape(S5_W, BRANCH_W)
    wc = jnp.concatenate([bd_out(cr), -bd_out(ci)], axis=0)
    pr, pi_ = ab_re.reshape(1, S5_W), ab_im.reshape(1, S5_W)
    lv_r, lv_i = [], []
    tr, ti = pr, pi_
    d = 1
    while d < tc:
        lv_r.append(pr)
        lv_i.append(pi_)
        tr, ti = (jnp.concatenate([tr, tr * pr - ti * pi_], axis=0),
                  jnp.concatenate([ti, tr * pi_ + ti * pr], axis=0))
        pr, pi_ = pr * pr - pi_ * pi_, 2.0 * pr * pi_
        d *= 2
    n_lv = len(lv_r)
    pad = (-n_lv) % 8
    a2k = jnp.concatenate([jnp.concatenate(lv_r, axis=0), jnp.concatenate(lv_i, axis=0)], axis=1)
    a2k = jnp.pad(a2k, ((0, pad), (0, 0)))
    apow = jnp.concatenate([tr, ti], axis=1)
    return wb.astype(BF16), wc.astype(BF16), a2k, apow


def _w_cols(col, width):
    return pl.BlockSpec((D_MODEL, width), lambda b, c: (0, col // width))


def _s5_mixer(hn3, w_mix, h0, lam_re, lam_im, log_dt, b_c, c_c, d_skip, w_glu, b_glu, tc):
    bsz, t, _ = hn3.shape
    wb, wc, a2k, apow = _s5_tables(lam_re, lam_im, log_dt, b_c, c_c, S5_ROWS)
    h0f = jnp.concatenate([h0[..., 0].reshape(bsz, 1, S5_W), h0[..., 1].reshape(bsz, 1, S5_W)], axis=-1)
    full = lambda a: pl.BlockSpec(a.shape, lambda b, c: (0,) * a.ndim)
    d2, bg2, wg = d_skip.reshape(1, BRANCH_W), b_glu.reshape(1, BRANCH_W), w_glu.astype(BF16)
    bb = min(bsz, STACK_STREAMS)
    while bsz % bb:
        bb -= 1
    y, hl = pl.pallas_call(
        functools.partial(_s5_kernel, tc=tc, bb=bb),
        grid=(bsz // bb, t // tc),
        in_specs=[pl.BlockSpec((bb, tc, D_MODEL), lambda b, c: (b, c, 0)), _w_cols(COL_S5, BRANCH_W),
                  pl.BlockSpec((bb, 1, 2 * S5_W), lambda b, c: (b, 0, 0)),
                  full(wb), full(wc), full(a2k), full(apow), full(d2), full(wg), full(bg2)],
        out_specs=[pl.BlockSpec((bb, tc, BRANCH_W), lambda b, c: (b, c, 0)),
                   pl.BlockSpec((bb, 1, 2 * S5_W), lambda b, c: (b, 0, 0))],
        out_shape=[jax.ShapeDtypeStruct((bsz, t, BRANCH_W), BF16),
                   jax.ShapeDtypeStruct((bsz, 1, 2 * S5_W), F32)],
        scratch_shapes=[pltpu.VMEM((bb, 1, 2 * S5_W), F32), pltpu.VMEM((bb * tc, 2 * S5_W), F32)],
        compiler_params=_cparams(("parallel", "arbitrary")),
    )(hn3, w_mix, h0f, wb, wc, a2k, apow, d2, wg, bg2)
    h_last = jnp.stack([hl[:, 0, :S5_W].reshape(bsz, S5_GROUPS, S5_STATE),
                        hl[:, 0, S5_W:].reshape(bsz, S5_GROUPS, S5_STATE)], axis=-1)
    return y, h_last


def _rwkv_kernel(hn_ref, wz_ref, sh0_ref, s0_ref, mu_ref, w0_ref, w2_ref, a0_ref, a2_ref, g2_ref,
                 kk_ref, ka_ref, rk_ref, lnw_ref, lnb_ref,
                 y_ref, sl_ref, sho_ref, st_ref, zp_ref, *, l, bb, nt):
    c = pl.program_id(1)

    @pl.when(c == 0)
    def _():
        st_ref[...] = s0_ref[...]
        zp_ref[...] = sh0_ref[...]

    tl = nt * l
    z_all = jnp.dot(jnp.concatenate([hn_ref[b] for b in range(bb)], axis=0), wz_ref[...],
                    preferred_element_type=F32)
    row = _iota((tl, 1), 0)
    zms = []
    for b in range(bb):
        z = z_all[b * tl:(b + 1) * tl]
        prev = jnp.where(row == 0, zp_ref[b], pltpu.roll(z, 1, axis=0))
        zp_ref[b] = z[tl - 1:tl]
        zms.append(z + (prev - z) * mu_ref[...])
    zm = jnp.concatenate(zms, axis=0)
    r, k, v = zm[:, 0:256], zm[:, 256:512], zm[:, 512:768]
    lo = zm[:, 768:896]
    g_lo = zm[:, 896:1024]
    w_log = -_softplus(-(w0_ref[...] + _dot(jnp.tanh(lo), w2_ref[...]))) - 0.5
    lw = -jnp.exp(w_log)
    a = jax.nn.sigmoid(a0_ref[...] + _dot(lo, a2_ref[...]))
    g = _dot(jax.nn.sigmoid(g_lo), g2_ref[...])
    ones_h = _head_ones()
    kk = k * kk_ref[...]
    kk = kk * lax.rsqrt(_dot2(kk * kk, ones_h) + NORM_EPS)
    k = k * (1.0 + (a - 1.0) * ka_ref[...])
    kka = kk * a

    cum = _cumsum_rows(lw, l, bb * nt)
    p_incl = jnp.exp(cum)
    p_inv = jnp.exp(-cum)
    kt = kk * jnp.exp(cum - lw)
    rt = r * p_incl
    kh = k * p_inv
    ah = kka * p_inv

    hm = _head_mask(l)
    strict, incl = _tri_masks(l)
    chunks = [(b, j) for j in range(nt) for b in range(bb)]
    rows = {s: slice((s[0] * nt + s[1]) * l, (s[0] * nt + s[1] + 1) * l) for s in chunks}
    ex = lambda x: {s: _expand(x[rows[s]], hm) for s in chunks}
    kh_e, ah_e, v_e = ex(kh), ex(ah), ex(v)
    a_aa = [jnp.where(strict, _dot_nt(kt[rows[s]], ah_e[s]), 0.0) for s in chunks]
    a_ak = {s: jnp.where(strict, _dot_nt(kt[rows[s]], kh_e[s]), 0.0) for s in chunks}
    b_ra = {s: jnp.where(incl, _dot_nt(rt[rows[s]], ah_e[s]), 0.0) for s in chunks}
    b_rk = {s: jnp.where(incl, _dot_nt(rt[rows[s]], kh_e[s]), 0.0) for s in chunks}
    t_inv = dict(zip(chunks, _unit_lower_inverse(a_aa, l)))
    av = {s: _dot(a_ak[s], v_e[s]) for s in chunks}
    bv = {s: _dot(b_rk[s], v_e[s]) for s in chunks}
    to_end = {s: jnp.exp(cum[rows[s]][l - 1:l] - cum[rows[s]]) for s in chunks}
    head_blk = _same_head(HEAD_W)
    st = [st_ref[b] for b in range(bb)]
    yb = {}
    for j in range(nt):
        now = [(b, j) for b in range(bb)]
        rhs = {s: _dot_nt(kt[rows[s]], st[s[0]]) + av[s] for s in now}
        ys = {s: _dot_nt(rt[rows[s]], st[s[0]]) + bv[s] for s in now}
        u = {s: _dot(t_inv[s], _expand(rhs[s], hm)) for s in now}
        for s in now:
            yb[s] = ys[s] - _dot(b_ra[s], _expand(u[s], hm))
        lhs_t = {s: jnp.concatenate([v[rows[s]], -u[s]], axis=0).T for s in now}
        rhs_k = {s: jnp.concatenate([k[rows[s]] * to_end[s], kka[rows[s]] * to_end[s]], axis=0) for s in now}
        for s in now:
            p_last = p_incl[rows[s]][l - 1:l]
            st[s[0]] = st[s[0]] * p_last + jnp.where(head_blk, _dot(lhs_t[s], rhs_k[s]), 0.0)
    for b in range(bb):
        st_ref[b] = st[b]
    y = jnp.concatenate([yb[(b, j)] for b in range(bb) for j in range(nt)], axis=0)

    inv_w = 1.0 / HEAD_W
    mean = _dot2(y, ones_h) * inv_w
    yc = y - mean
    var = _dot2(yc * yc, ones_h) * inv_w
    y = yc * lax.rsqrt(var + RW_EPS) * lnw_ref[...] + lnb_ref[...]
    bonus = _dot2(r * k * rk_ref[...], ones_h) * v
    y = ((y + bonus) * g).astype(y_ref.dtype)
    for b in range(bb):
        y_ref[b] = y[b * tl:(b + 1) * tl]

    @pl.when(c == pl.num_programs(1) - 1)
    def _():
        sl_ref[...] = st_ref[...]
        sho_ref[...] = zp_ref[...]


MIX_CHUNK = 64
STACK_STREAMS = 4
STACK_ROWS = 512


def _streams_per_step(bsz, t, l):
    bb = min(bsz, STACK_STREAMS)
    while bsz % bb:
        bb -= 1
    nt = max(1, min(t, STACK_ROWS // bb) // l)
    while (t // l) % nt:
        nt -= 1
    return bb, nt


def _block_diag_heads(s):
    bsz = s.shape[0]
    eye = jnp.eye(HEADS, dtype=s.dtype)
    return jnp.einsum('bhij,hg->bhigj', s, eye).reshape(bsz, BRANCH_W, BRANCH_W)


def _diag_blocks(s):
    bsz = s.shape[0]
    s5 = s.reshape(bsz, HEADS, HEAD_W, HEADS, HEAD_W)
    return jnp.stack([s5[:, h, :, h, :] for h in range(HEADS)], axis=1)


def _rwkv_mixer(hn3, w_mix, shift0, s0, mu, w0, w2, a0, a2, g2, k_k, k_a, r_k, ln_w, ln_b, l):
    bsz, t, _ = hn3.shape
    row = lambda a: a.reshape(1, -1)
    w2p = jnp.concatenate([w2, jnp.zeros_like(w2)], axis=0).astype(BF16)
    a2p = jnp.concatenate([jnp.zeros_like(a2), a2], axis=0).astype(BF16)
    args = (shift0.reshape(bsz, 1, RW_COLS), _block_diag_heads(s0), row(mu), row(w0), w2p, row(a0), a2p,
            g2.astype(BF16), row(k_k), row(k_a), row(r_k), row(ln_w), row(ln_b))
    full = lambda a: pl.BlockSpec(a.shape, lambda b, c: (0,) * a.ndim)
    bb, nt = _streams_per_step(bsz, t, l)
    y, sl, sho = pl.pallas_call(
        functools.partial(_rwkv_kernel, l=l, bb=bb, nt=nt),
        grid=(bsz // bb, t // (nt * l)),
        in_specs=[pl.BlockSpec((bb, nt * l, D_MODEL), lambda b, c: (b, c, 0)), _w_cols(COL_RW, RW_COLS),
                  pl.BlockSpec((bb, 1, RW_COLS), lambda b, c: (b, 0, 0)),
                  pl.BlockSpec((bb, BRANCH_W, BRANCH_W), lambda b, c: (b, 0, 0))]
                 + [full(a) for a in args[2:]],
        out_specs=[pl.BlockSpec((bb, nt * l, BRANCH_W), lambda b, c: (b, c, 0)),
                   pl.BlockSpec((bb, BRANCH_W, BRANCH_W), lambda b, c: (b, 0, 0)),
                   pl.BlockSpec((bb, 1, RW_COLS), lambda b, c: (b, 0, 0))],
        out_shape=[jax.ShapeDtypeStruct((bsz, t, BRANCH_W), BF16),
                   jax.ShapeDtypeStruct((bsz, BRANCH_W, BRANCH_W), F32),
                   jax.ShapeDtypeStruct((bsz, 1, RW_COLS), F32)],
        scratch_shapes=[pltpu.VMEM((bb, BRANCH_W, BRANCH_W), F32), pltpu.VMEM((bb, 1, RW_COLS), F32)],
        compiler_params=_cparams(("parallel", "arbitrary")),
    )(hn3, w_mix, *args)
    return y, _diag_blocks(sl), sho[:, 0]


def _sgu_kernel(hn_ref, wz_ref, lnw_ref, lnb_ref, wm_ref, bias_ref, o_ref, v_ref, *, l, nc):
    z = jnp.dot(hn_ref[...], wz_ref[...], preferred_element_type=F32)
    zg = jax.nn.gelu(z)
    u, v = zg[:, :BRANCH_W], zg[:, BRANCH_W:]
    mean = jnp.mean(v, axis=-1, keepdims=True)
    vc = v - mean
    var = jnp.mean(vc * vc, axis=-1, keepdims=True)
    v = vc * lax.rsqrt(var + NORM_EPS) * lnw_ref[...] + lnb_ref[...]
    v_ref[...] = v
    hm = _head_mask(l)
    wm = wm_ref[...]
    for i in range(nc):
        rows = slice(i * l, (i + 1) * l)
        mixed = bias_ref[...] + _dot(wm, _expand(v[rows], hm))
        o_ref[rows, :] = (u[rows] * mixed).astype(o_ref.dtype)


def _sgu_mixer(hn3, w_mix, ln_w, ln_b, w_s, b_s):
    bsz, t, _ = hn3.shape
    l = min(SG_CHUNK, t)
    nc = max(1, min(SG_ROWS, t) // l)
    tril = jnp.tril(jnp.ones((l, l), F32))
    wm = jnp.transpose(w_s[:, :l, :l] * tril, (1, 0, 2)).reshape(l, HEADS * l).astype(BF16)
    bias = jnp.repeat(jnp.transpose(b_s[:, :l]), HEAD_W, axis=1)
    row = lambda a: a.reshape(1, -1)
    full = lambda a: pl.BlockSpec(a.shape, lambda b, c: (0,) * a.ndim)
    args = (row(ln_w), row(ln_b), wm, bias)
    return pl.pallas_call(
        functools.partial(_sgu_kernel, l=l, nc=nc),
        grid=(bsz, t // (nc * l)),
        in_specs=[pl.BlockSpec((None, nc * l, D_MODEL), lambda b, c: (b, c, 0)), _w_cols(COL_SG, 2 * BRANCH_W)]
                 + [full(a) for a in args],
        out_specs=[pl.BlockSpec((None, nc * l, BRANCH_W), lambda b, c: (b, c, 0)),
                   pl.BlockSpec((None, nc * l, BRANCH_W), lambda b, c: (b, c, 0))],
        out_shape=[jax.ShapeDtypeStruct((bsz, t, BRANCH_W), BF16),
                   jax.ShapeDtypeStruct((bsz, t, BRANCH_W), F32)],
        compiler_params=_cparams(("parallel", "parallel")),
    )(hn3, w_mix, *args)


def _gdn_kernel(hn_ref, wz_ref, wab_ref, cv0_ref, s0_ref, cw_ref, alog_ref, dtb_ref, nw_ref,
                y_ref, sl_ref, cvo_ref, st_ref, cv_ref, *, l, bb, nt):
    c = pl.program_id(1)

    @pl.when(c == 0)
    def _():
        st_ref[...] = s0_ref[...]
        cv_ref[...] = cv0_ref[...]

    tl = nt * l
    hn = jnp.concatenate([hn_ref[b] for b in range(bb)], axis=0)
    z_all = jnp.dot(hn, wz_ref[...], preferred_element_type=F32)
    ab = jnp.dot(hn, wab_ref[...], preferred_element_type=F32)
    row8 = _iota((8, 1), 0)
    convs, gates = [], []
    for b in range(bb):
        z = z_all[b * tl:(b + 1) * tl]
        qkv = z[:, :GD_QKV]
        gates.append(z[:, GD_QKV:])
        carry = cv_ref[b]
        cv_ref[b] = qkv[tl - 8:tl]
        conv = qkv * cw_ref[GD_CONV - 1:GD_CONV]
        for j in range(1, GD_CONV):
            sh = pltpu.roll(qkv, j, axis=0)
            top = jnp.where(row8 < j, pltpu.roll(carry, j, axis=0), sh[:8])
            sh = jnp.concatenate([top, sh[8:]], axis=0) if tl > 8 else top
            conv = conv + sh * cw_ref[GD_CONV - 1 - j:GD_CONV - j]
        convs.append(conv)
    conv = _silu(jnp.concatenate(convs, axis=0))
    gate = jnp.concatenate(gates, axis=0)
    q, k, v = conv[:, :256], conv[:, 256:512], conv[:, 512:768]
    ones_h = _head_ones()
    q = q * lax.rsqrt(_dot2(q * q, ones_h) + NORM_EPS) * (HEAD_W ** -0.5)
    k = k * lax.rsqrt(_dot2(k * k, ones_h) + NORM_EPS)
    lane_h = _iota((bb * tl, BRANCH_W), 1) // HEAD_W
    a_in = jnp.zeros((bb * tl, BRANCH_W), F32)
    b_in = jnp.zeros((bb * tl, BRANCH_W), F32)
    for h in range(HEADS):
        a_in = jnp.where(lane_h == h, ab[:, h:h + 1], a_in)
        b_in = jnp.where(lane_h == h, ab[:, HEADS + h:HEADS + h + 1], b_in)
    beta = jax.nn.sigmoid(b_in)
    g = -jnp.exp(alog_ref[...]) * _softplus(a_in + dtb_ref[...])
    gc = _cumsum_rows(g, l, bb * nt)
    eg = jnp.exp(gc)
    kb = k * beta
    vb = v * beta
    kbg = kb * eg
    qg = q * eg

    hm = _head_mask(l)
    strict, incl = _tri_masks(l)
    n = HEADS * l
    chunks = [(b, j) for j in range(nt) for b in range(bb)]
    rows = {s: slice((s[0] * nt + s[1]) * l, (s[0] * nt + s[1] + 1) * l) for s in chunks}
    ex = lambda x: {s: _expand(x[rows[s]], hm) for s in chunks}
    k_e, vb_e, kbg_e = ex(k), ex(vb), ex(kbg)
    lane_hd = _iota((l, n), 1) // l
    eye = _iota((l, n), 0) == (_iota((l, n), 1) % l)
    decay = {}
    for s in chunks:
        gi = jnp.zeros((l, n), F32)
        for h in range(HEADS):
            gi = jnp.where(lane_hd == h, gc[rows[s]][:, h * HEAD_W:h * HEAD_W + 1], gi)
        gj = jnp.sum(jnp.where(eye, gi, 0.0), axis=0, keepdims=True)
        decay[s] = jnp.where(incl, jnp.exp(jnp.where(incl, gi - gj, 0.0)), 0.0)
    lm = [jnp.where(strict, _dot_nt(kb[rows[s]], k_e[s]) * decay[s], 0.0) for s in chunks]
    qk = {s: _dot_nt(q[rows[s]], k_e[s]) * decay[s] for s in chunks}
    t_inv = dict(zip(chunks, _unit_lower_inverse(lm, l)))
    uc = {s: _dot(t_inv[s], vb_e[s]) for s in chunks}
    wc = {s: _dot(t_inv[s], kbg_e[s]) for s in chunks}
    g_last = {s: gc[rows[s]][l - 1:l] for s in chunks}
    k_dec = {s: (k[rows[s]] * jnp.exp(g_last[s] - gc[rows[s]])).T for s in chunks}
    head_blk = _same_head(HEAD_W)
    st = [st_ref[b] for b in range(bb)]
    ob = {}
    for j in range(nt):
        now = [(b, j) for b in range(bb)]
        o0 = {s: _dot(qg[rows[s]], st[s[0]]) for s in now}
        v_new = {s: uc[s] - _dot(wc[s], st[s[0]]) for s in now}
        for s in now:
            ob[s] = o0[s] + _dot(qk[s], _expand(v_new[s], hm))
        for s in now:
            st[s[0]] = st[s[0]] * jnp.exp(g_last[s]) + jnp.where(head_blk, _dot(k_dec[s], v_new[s]), 0.0)
    for b in range(bb):
        st_ref[b] = st[b]
    o = jnp.concatenate([ob[(b, j)] for b in range(bb) for j in range(nt)], axis=0)
    ms = _dot2(o * o, ones_h) * (1.0 / HEAD_W)
    o = (o * lax.rsqrt(ms + NORM_EPS) * nw_ref[...] * _silu(gate)).astype(y_ref.dtype)
    for b in range(bb):
        y_ref[b] = o[b * tl:(b + 1) * tl]

    @pl.when(c == pl.num_programs(1) - 1)
    def _():
        sl_ref[...] = st_ref[...]
        cvo_ref[...] = cv_ref[...]


def _gdn_mixer(hn3, w_mix, conv0, s0, conv_w, a_log, dt_bias, norm_w, l):
    bsz, t, _ = hn3.shape
    cv0 = jnp.pad(conv0, ((0, 0), (8 - (GD_CONV - 1), 0), (0, 0)))
    cw = jnp.pad(conv_w, ((0, 8 - GD_CONV), (0, 0)))
    per_head = lambda a: jnp.repeat(a, HEAD_W).reshape(1, BRANCH_W)
    args = (cv0, _block_diag_heads(s0), cw, per_head(a_log), per_head(dt_bias),
            jnp.tile(norm_w, HEADS).reshape(1, BRANCH_W))
    full = lambda a: pl.BlockSpec(a.shape, lambda b, c: (0,) * a.ndim)
    bb, nt = _streams_per_step(bsz, t, l)
    y, sl, cvo = pl.pallas_call(
        functools.partial(_gdn_kernel, l=l, bb=bb, nt=nt),
        grid=(bsz // bb, t // (nt * l)),
        in_specs=[pl.BlockSpec((bb, nt * l, D_MODEL), lambda b, c: (b, c, 0)),
                  _w_cols(COL_GD, 1024), _w_cols(COL_AB, LANES),
                  pl.BlockSpec((bb, 8, GD_QKV), lambda b, c: (b, 0, 0)),
                  pl.BlockSpec((bb, BRANCH_W, BRANCH_W), lambda b, c: (b, 0, 0))]
                 + [full(a) for a in args[2:]],
        out_specs=[pl.BlockSpec((bb, nt * l, BRANCH_W), lambda b, c: (b, c, 0)),
                   pl.BlockSpec((bb, BRANCH_W, BRANCH_W), lambda b, c: (b, 0, 0)),
                   pl.BlockSpec((bb, 8, GD_QKV), lambda b, c: (b, 0, 0))],
        out_shape=[jax.ShapeDtypeStruct((bsz, t, BRANCH_W), BF16),
                   jax.ShapeDtypeStruct((bsz, BRANCH_W, BRANCH_W), F32),
                   jax.ShapeDtypeStruct((bsz, 8, GD_QKV), F32)],
        scratch_shapes=[pltpu.VMEM((bb, BRANCH_W, BRANCH_W), F32), pltpu.VMEM((bb, 8, GD_QKV), F32)],
        compiler_params=_cparams(("parallel", "arbitrary")),
    )(hn3, w_mix, w_mix, *args)
    return y, _diag_blocks(sl), cvo[:, 8 - (GD_CONV - 1):]


def _merge_kernel(ya_ref, yb_ref, yc_ref, yd_ref, x_ref, nw_ref, wg_ref, wbr_ref, wout_ref, o_ref):
    x = x_ref[...]
    ms = jnp.mean(x * x, axis=-1, keepdims=True)
    h = (x * lax.rsqrt(ms + NORM_EPS) * nw_ref[...]).astype(BF16)
    m = None
    for b, y_ref in enumerate((ya_ref, yb_ref, yc_ref, yd_ref)):
        gate = jnp.dot(h, wg_ref[:, b * D_MODEL:(b + 1) * D_MODEL], preferred_element_type=F32)
        br = jnp.dot(y_ref[...], wbr_ref[b], preferred_element_type=F32)
        term = (0.5 * jnp.tanh(0.5 * gate) + 0.5) * br
        m = term if m is None else m + term
    o_ref[...] = x + jnp.dot(m.astype(BF16), wout_ref[...], preferred_element_type=F32)


def _merge(ys, x2, norm_w, w_gate, w_branch, w_out, tm):
    n = x2.shape[0]
    yspec = pl.BlockSpec((tm, BRANCH_W), lambda i: (i, 0))
    once = pl.Buffered(1)
    return pl.pallas_call(
        _merge_kernel,
        grid=(n // tm,),
        in_specs=[yspec, yspec, yspec, yspec,
                  pl.BlockSpec((tm, D_MODEL), lambda i: (i, 0)),
                  pl.BlockSpec((1, D_MODEL), lambda i: (0, 0)),
                  pl.BlockSpec((D_MODEL, 4 * D_MODEL), lambda i: (0, 0), pipeline_mode=once),
                  pl.BlockSpec((4, BRANCH_W, D_MODEL), lambda i: (0, 0, 0), pipeline_mode=once),
                  pl.BlockSpec((D_MODEL, D_MODEL), lambda i: (0, 0), pipeline_mode=once)],
        out_specs=pl.BlockSpec((tm, D_MODEL), lambda i: (i, 0)),
        out_shape=jax.ShapeDtypeStruct((n, D_MODEL), F32),
        compiler_params=_cparams(("parallel",)),
    )(*[y.reshape(n, BRANCH_W) for y in ys], x2, norm_w.reshape(1, D_MODEL), w_gate,
      w_branch.astype(BF16), w_out.astype(BF16))


def _ffn_kernel(x_ref, nw_ref, w1_ref, w3_ref, w2_ref, nnw_ref, o_ref, hn_ref):
    x = x_ref[...]
    ms = jnp.mean(x * x, axis=-1, keepdims=True)
    h = (x * lax.rsqrt(ms + NORM_EPS) * nw_ref[...]).astype(BF16)
    a = _silu(jnp.dot(h, w1_ref[...], preferred_element_type=F32)) * jnp.dot(h, w3_ref[...], preferred_element_type=F32)
    y = x + jnp.dot(a.astype(BF16), w2_ref[...], preferred_element_type=F32)
    o_ref[...] = y
    ms = jnp.mean(y * y, axis=-1, keepdims=True)
    hn_ref[...] = (y * lax.rsqrt(ms + NORM_EPS) * nnw_ref[...]).astype(BF16)


def _ffn(x2, norm_w, w1, w3, w2, next_norm_w, tm):
    n = x2.shape[0]
    dff = w1.shape[1]
    once = pl.Buffered(1)
    spec = pl.BlockSpec((tm, D_MODEL), lambda i: (i, 0))
    vec = pl.BlockSpec((1, D_MODEL), lambda i: (0, 0))
    return pl.pallas_call(
        _ffn_kernel,
        grid=(n // tm,),
        in_specs=[spec, vec,
                  pl.BlockSpec((D_MODEL, dff), lambda i: (0, 0), pipeline_mode=once),
                  pl.BlockSpec((D_MODEL, dff), lambda i: (0, 0), pipeline_mode=once),
                  pl.BlockSpec((dff, D_MODEL), lambda i: (0, 0), pipeline_mode=once), vec],
        out_specs=[spec, spec],
        out_shape=[jax.ShapeDtypeStruct((n, D_MODEL), F32), jax.ShapeDtypeStruct((n, D_MODEL), BF16)],
        compiler_params=_cparams(("parallel",)),
    )(x2, norm_w.reshape(1, D_MODEL), w1.astype(BF16), w3.astype(BF16), w2.astype(BF16),
      next_norm_w.reshape(1, D_MODEL))


SC_CORES = 2
SC_SUBCORES = 16
SC_WINDOW = 128
SC_PIECE = 256
MOE_BLOCK = 512
MOE_FF = 1792
ROUTER_ROWS = 256
HALF = D_MODEL // 2


HIGH16 = -65536


def _pack_bf16_pairs(h):
    bits = lax.bitcast_convert_type(h.astype(BF16).astype(F32), jnp.int32)
    return lax.shift_right_logical(bits[:, :HALF], 16) | (bits[:, HALF:] & HIGH16)


def _unpack_bf16_pairs(pieces):
    lo = [lax.bitcast_convert_type(lax.shift_left(w, 16), F32) for w in pieces]
    hi = [lax.bitcast_convert_type(w & HIGH16, F32) for w in pieces]
    return jnp.concatenate(lo + hi, axis=1).astype(BF16)


def _route_kernel(x_ref, nw_ref, wr_ref, br_ref, hp_ref, tok_ref, cnt_ref, seen_ref, *, tm):
    @pl.when(pl.program_id(0) == 0)
    def _():
        seen_ref[...] = jnp.zeros_like(seen_ref)

    x = x_ref[...]
    ms = jnp.mean(x * x, axis=-1, keepdims=True)
    h = x * lax.rsqrt(ms + NORM_EPS) * nw_ref[...]
    hp = _pack_bf16_pairs(h)
    for q in range(HALF // SC_PIECE):
        hp_ref[q] = hp[:, q * SC_PIECE:(q + 1) * SC_PIECE]
    lane = _iota((tm, LANES), 1)
    logits = jnp.where(lane < N_EXPERTS, _dot_hp(h, wr_ref[...]) + br_ref[...], -jnp.inf)
    m1 = jnp.max(logits, axis=1, keepdims=True)
    i1 = jnp.min(jnp.where(logits == m1, lane, LANES), axis=1, keepdims=True)
    rest = jnp.where(lane == i1, -jnp.inf, logits)
    m2 = jnp.max(rest, axis=1, keepdims=True)
    i2 = jnp.min(jnp.where(rest == m2, lane, LANES), axis=1, keepdims=True)
    e2 = jnp.exp(m2 - m1)
    g1 = 1.0 / (1.0 + e2)
    g2 = e2 / (1.0 + e2)
    self32 = ((lane == i1) | (lane == i2)).astype(F32)
    seen = seen_ref[...]
    tri = (_iota((tm, tm), 1) < _iota((tm, tm), 0)).astype(BF16)
    rank = jnp.dot(tri, self32.astype(BF16), preferred_element_type=F32) + seen
    r1 = jnp.sum(jnp.where(lane == i1, rank, 0.0), axis=1, keepdims=True)
    r2 = jnp.sum(jnp.where(lane == i2, rank, 0.0), axis=1, keepdims=True)
    cols = (i1.astype(F32), i2.astype(F32), r1, r2, g1, g2)
    tok = jnp.zeros((tm, LANES), F32)
    for j, col in enumerate(cols):
        tok = jnp.where(lane == j, col, tok)
    tok_ref[...] = tok
    seen = seen + jnp.sum(self32, axis=0, keepdims=True)
    seen_ref[...] = seen
    cnt_ref[...] = jnp.broadcast_to(seen, (8, LANES))


def _route(x2, norm_w, w_router, b_router, tm):
    n = x2.shape[0]
    wr = jnp.pad(w_router, ((0, 0), (0, LANES - N_EXPERTS)))
    br = jnp.pad(b_router, (0, LANES - N_EXPERTS)).reshape(1, LANES)
    return pl.pallas_call(
        functools.partial(_route_kernel, tm=tm),
        grid=(n // tm,),
        in_specs=[pl.BlockSpec((tm, D_MODEL), lambda i: (i, 0)),
                  pl.BlockSpec((1, D_MODEL), lambda i: (0, 0)),
                  pl.BlockSpec((D_MODEL, LANES), lambda i: (0, 0)),
                  pl.BlockSpec((1, LANES), lambda i: (0, 0))],
        out_specs=[pl.BlockSpec((HALF // SC_PIECE, tm, SC_PIECE), lambda i: (0, i, 0)),
                   pl.BlockSpec((tm, LANES), lambda i: (i, 0)),
                   pl.BlockSpec((8, LANES), lambda i: (0, 0))],
        out_shape=[jax.ShapeDtypeStruct((HALF // SC_PIECE, n, SC_PIECE), jnp.int32),
                   jax.ShapeDtypeStruct((n, LANES), F32),
                   jax.ShapeDtypeStruct((8, LANES), F32)],
        scratch_shapes=[pltpu.VMEM((1, LANES), F32)],
        compiler_params=_cparams(("arbitrary",)),
    )(x2, norm_w.reshape(1, D_MODEL), wr, br)


def _sc_rows_multiple(d):
    return SC_CORES * SC_SUBCORES * SC_WINDOW * SC_PIECE // d


def _sc_gather_rows(table, idx):
    f, v, _ = table.shape
    b = idx.shape[0]
    assert b % _sc_rows_multiple(f * SC_PIECE) == 0
    idx_all = (idx[None, :] + (jnp.arange(f, dtype=jnp.int32) * v)[:, None]).reshape(-1)
    return _sc_gather_pieces(table.reshape(f * v, SC_PIECE), idx_all).reshape(f, b, SC_PIECE)


def _sc_gather_pieces(table, idx):
    bp = idx.shape[0]
    d = table.shape[1]
    window = SC_WINDOW
    idx2 = idx.reshape(1, bp)
    mesh = plsc.VectorSubcoreMesh(core_axis_name="core", subcore_axis_name="subcore")

    @functools.partial(pl.kernel, out_type=jax.ShapeDtypeStruct((bp, d), table.dtype), mesh=mesh)
    def gather(x_hbm, i_hbm, o_hbm):
        def body(i_vmem, o_vmem):
            pltpu.sync_copy(x_hbm.at[i_vmem.at[0]], o_vmem)

        pltpu.emit_pipeline(
            body,
            grid=(bp // window,),
            in_specs=[pl.BlockSpec((1, window), index_map=lambda i: (0, i))],
            out_specs=[pl.BlockSpec((window, d), index_map=lambda i: (i, 0))],
            core_axis_name=("core", "subcore"),
            dimension_semantics=(pltpu.PARALLEL,),
        )(i_hbm, o_hbm)

    return gather(table, idx2)


def _sc_scatter_rows(rows, pos, n_slots):
    f, n0, _ = rows.shape
    n = -(-n0 // _sc_rows_multiple(D_MODEL)) * _sc_rows_multiple(D_MODEL)
    if n != n0:
        rows = jnp.pad(rows, ((0, 0), (0, n - n0), (0, 0)))
        pos = jnp.pad(pos.reshape(2, n0), ((0, 0), (0, n - n0)), constant_values=n_slots - 1)
    nb = n // SC_WINDOW
    assert (f * 2 * nb) % (SC_CORES * SC_SUBCORES) == 0
    idx = (pos.reshape(1, 2 * n) + (jnp.arange(f, dtype=jnp.int32) * n_slots)[:, None]).reshape(1, f * 2 * n)
    mesh = plsc.VectorSubcoreMesh(core_axis_name="core", subcore_axis_name="subcore")

    @functools.partial(pl.kernel, out_type=jax.ShapeDtypeStruct((f * n_slots, SC_PIECE), rows.dtype), mesh=mesh)
    def scatter(x_hbm, i_hbm, o_hbm):
        def body(x_vmem, i_vmem):
            pltpu.sync_copy(x_vmem, o_hbm.at[i_vmem.at[0]])

        pltpu.emit_pipeline(
            body,
            grid=(f * 2 * nb,),
            in_specs=[pl.BlockSpec((SC_WINDOW, SC_PIECE), index_map=lambda i: ((i // (2 * nb)) * nb + i % nb, 0)),
                      pl.BlockSpec((1, SC_WINDOW), index_map=lambda i: (0, i))],
            out_specs=[],
            core_axis_name=("core", "subcore"),
            dimension_semantics=(pltpu.PARALLEL,),
        )(x_hbm, i_hbm)

    return scatter(rows.reshape(f * n, SC_PIECE), idx).reshape(f, n_slots, SC_PIECE)


def _experts_kernel(be_ref, nb_ref, nv_ref, xs_ref, w1_ref, w3_ref, w2_ref, o_ref):
    g, c = pl.program_id(0), pl.program_id(1)

    @pl.when(c == 0)
    def _():
        o_ref[...] = jnp.zeros_like(o_ref)

    @pl.when(g < nb_ref[0])
    def _():
        live = _iota((xs_ref.shape[1], 1), 0) < nv_ref[g]
        xb = _unpack_bf16_pairs([jnp.where(live, xs_ref[q], 0) for q in range(HALF // SC_PIECE)])
        a = (_silu(jnp.dot(xb, w1_ref[...], preferred_element_type=F32))
             * jnp.dot(xb, w3_ref[...], preferred_element_type=F32))
        y = jnp.dot(a.astype(BF16), w2_ref[...], preferred_element_type=F32)
        for q in range(D_MODEL // SC_PIECE):
            o_ref[q] += y[:, q * SC_PIECE:(q + 1) * SC_PIECE]


def _experts(xs, block_expert, n_blocks, block_rows, w1, w3, w2, blk, tf):
    n_slots = xs.shape[1]
    dff = w1.shape[2]
    grid_spec = pltpu.PrefetchScalarGridSpec(
        num_scalar_prefetch=3,
        grid=(n_slots // blk, dff // tf),
        in_specs=[pl.BlockSpec((HALF // SC_PIECE, blk, SC_PIECE), lambda g, c, be, nb, nv: (0, g, 0)),
                  pl.BlockSpec((None, D_MODEL, tf), lambda g, c, be, nb, nv: (be[g], 0, c)),
                  pl.BlockSpec((None, D_MODEL, tf), lambda g, c, be, nb, nv: (be[g], 0, c)),
                  pl.BlockSpec((None, tf, D_MODEL), lambda g, c, be, nb, nv: (be[g], c, 0))],
        out_specs=pl.BlockSpec((D_MODEL // SC_PIECE, blk, SC_PIECE), lambda g, c, be, nb, nv: (0, g, 0)),
    )
    return pl.pallas_call(
        _experts_kernel,
        grid_spec=grid_spec,
        out_shape=jax.ShapeDtypeStruct((D_MODEL // SC_PIECE, n_slots, SC_PIECE), F32),
        compiler_params=_cparams(("parallel", "arbitrary")),
    )(block_expert, n_blocks, block_rows, xs, w1, w3, w2)


def _combine_kernel(x_ref, y1_ref, y2_ref, tok_ref, nw_ref, o_ref):
    tok = tok_ref[...]
    rows = lambda y_ref: jnp.concatenate([y_ref[q] for q in range(D_MODEL // SC_PIECE)], axis=1)
    x = x_ref[...] + tok[:, 4:5] * rows(y1_ref) + tok[:, 5:6] * rows(y2_ref)
    ms = jnp.mean(x * x, axis=-1, keepdims=True)
    o_ref[...] = x * lax.rsqrt(ms + NORM_EPS) * nw_ref[...]


def _combine(x2, ys2, tok, norm_w, tm):
    n = x2.shape[0]
    spec = pl.BlockSpec((tm, D_MODEL), lambda i: (i, 0))
    yspec = lambda off: pl.BlockSpec((D_MODEL // SC_PIECE, tm, SC_PIECE), lambda i: (0, i + off, 0))
    return pl.pallas_call(
        _combine_kernel,
        grid=(n // tm,),
        in_specs=[spec, yspec(0), yspec(n // tm),
                  pl.BlockSpec((tm, LANES), lambda i: (i, 0)), pl.BlockSpec((1, D_MODEL), lambda i: (0, 0))],
        out_specs=spec,
        out_shape=jax.ShapeDtypeStruct((n, D_MODEL), F32),
        compiler_params=_cparams(("parallel",)),
    )(x2, ys2, ys2, tok, norm_w.reshape(1, D_MODEL))


def _moe_final(x2, norm_w, w_router, b_router, w1, w3, w2, final_norm_w):
    n = x2.shape[0]
    blk = min(MOE_BLOCK, n)
    hp, tok, cnt = _route(x2, norm_w, w_router, b_router, ROUTER_ROWS)
    counts = cnt[0, :N_EXPERTS].astype(jnp.int32)
    padded = (counts + blk - 1) // blk * blk
    ends = jnp.cumsum(padded)
    base = ends - padded
    round_up = lambda a, m: -(-a // m) * m
    n_slots = round_up((-(-2 * n // blk) + N_EXPERTS + 1) * blk, _sc_rows_multiple(HALF))
    n_blocks = n_slots // blk
    i12 = tok[:, 0:2].astype(jnp.int32)
    pos = (base[i12] + tok[:, 2:4].astype(jnp.int32)).T.reshape(-1)
    block_start = jnp.arange(n_blocks, dtype=jnp.int32) * blk
    block_expert = jnp.minimum(jnp.sum(block_start[:, None] >= ends[None, :], axis=1), N_EXPERTS - 1).astype(jnp.int32)
    block_rows = jnp.clip((base + counts)[block_expert] - block_start, 0, blk).astype(jnp.int32)
    xs = _sc_scatter_rows(hp, pos, n_slots)
    ys = _experts(xs, block_expert, (ends[-1:] // blk).astype(jnp.int32), block_rows,
                  w1.astype(BF16), w3.astype(BF16), w2.astype(BF16), blk, MOE_FF)
    pos = jnp.pad(pos, (0, round_up(2 * n, _sc_rows_multiple(D_MODEL)) - 2 * n))
    ys2 = _sc_gather_rows(ys, pos)
    return _combine(x2, ys2, tok, final_norm_w, min(1024, n))


def _permute_w_in(w):
    s5, rw, sg = w[:, 0:256], w[:, 256:1280], w[:, 1280:1792]
    gd, ab, gates = w[:, 1792:2816], w[:, 2816:2824], w[:, 2824:6920]
    pad = jnp.zeros((D_MODEL, PROJ_COLS - COL_AB - 8), w.dtype)
    return jnp.concatenate([rw, gd, sg, s5, ab, pad], axis=1).astype(BF16), gates.astype(BF16)


def _run_trunk(x, s5_h, rw_s, rw_shift, gd_s, gd_conv, p, w_in_perm):
    bsz, t, _ = x.shape
    n = bsz * t
    l = min(MIX_CHUNK, t)
    tc_s5 = min(256, t)
    new = ([], [], [], [], [], [])
    x2 = x.reshape(n, D_MODEL)
    for layer in range(2):
        g = lambda name: p[name][layer]
        w_mix, w_gate = w_in_perm[layer]
        if layer == 0:
            hn = _norm_cast(x2, g('norm1_w'), min(512, n))
        hn3 = hn.reshape(bsz, t, D_MODEL)
        y_a, s5_new = _s5_mixer(hn3, w_mix, s5_h[layer], g('s5_lam_re'), g('s5_lam_im'), g('s5_log_dt'), g('s5_b'),
                                g('s5_c'), g('s5_d'), g('s5_w_glu'), g('s5_b_glu'), tc_s5)
        y_b, rw_new, shift_new = _rwkv_mixer(hn3, w_mix, rw_shift[layer], rw_s[layer], g('rw_mu'), g('rw_w0'),
                                             g('rw_w2'), g('rw_a0'), g('rw_a2'), g('rw_g2'), g('rw_k_k'),
                                             g('rw_k_a'), g('rw_r_k'), g('rw_ln_w'), g('rw_ln_b'), l)
        y_c, sg_v = _sgu_mixer(hn3, w_mix, g('sg_ln_w'), g('sg_ln_b'), g('sg_w_s'), g('sg_b_s'))
        y_d, gd_new, conv_new = _gdn_mixer(hn3, w_mix, gd_conv[layer], gd_s[layer], g('gd_conv_w'), g('gd_a_log'),
                                           g('gd_dt_bias'), g('gd_norm_w'), l)
        x2 = _merge((y_a, y_b, y_c, y_d), x2, g('norm1_w'), w_gate, g('w_branch'), g('w_out'), min(512, n))
        j = layer // 2
        if layer % 2 == 0:
            x2, hn = _ffn(x2, g('norm2_w'), p['ffn_w1'][j], p['ffn_w3'][j], p['ffn_w2'][j],
                          p['norm1_w'][layer + 1], min(512, n))
        else:
            y = _moe_final(x2, g('norm2_w'), p['moe_router'][j], p['moe_router_b'][j],
                           p['moe_w1'][j], p['moe_w3'][j], p['moe_w2'][j], p['final_norm_w'])
        for lst, s in zip(new, (s5_new, rw_new, shift_new, gd_new, conv_new, sg_v)):
            lst.append(s)
    return y.reshape(bsz, t, D_MODEL), [jnp.stack(lst) for lst in new]


def kernel(x_prompt, x_sample, state_s5, state_rwkv, state_rwkv_shift, state_gdn, state_gdn_conv, norm1_w, w_in, s5_lam_re, s5_lam_im, s5_log_dt, s5_b, s5_c, s5_d, s5_w_glu, s5_b_glu, rw_mu, rw_w0, rw_w2, rw_a0, rw_a2, rw_g2, rw_k_k, rw_k_a, rw_r_k, rw_ln_w, rw_ln_b, sg_ln_w, sg_ln_b, sg_w_s, sg_b_s, gd_conv_w, gd_a_log, gd_dt_bias, gd_norm_w, w_branch, w_out, norm2_w, ffn_w1, ffn_w3, ffn_w2, moe_router, moe_router_b, moe_w1, moe_w3, moe_w2, final_norm_w):
    p = {
        'norm1_w': norm1_w, 's5_lam_re': s5_lam_re, 's5_lam_im': s5_lam_im, 's5_log_dt': s5_log_dt,
        's5_b': s5_b, 's5_c': s5_c, 's5_d': s5_d, 's5_w_glu': s5_w_glu, 's5_b_glu': s5_b_glu,
        'rw_mu': rw_mu, 'rw_w0': rw_w0, 'rw_w2': rw_w2, 'rw_a0': rw_a0, 'rw_a2': rw_a2, 'rw_g2': rw_g2,
        'rw_k_k': rw_k_k, 'rw_k_a': rw_k_a, 'rw_r_k': rw_r_k, 'rw_ln_w': rw_ln_w, 'rw_ln_b': rw_ln_b,
        'sg_ln_w': sg_ln_w, 'sg_ln_b': sg_ln_b, 'sg_w_s': sg_w_s, 'sg_b_s': sg_b_s,
        'gd_conv_w': gd_conv_w, 'gd_a_log': gd_a_log, 'gd_dt_bias': gd_dt_bias, 'gd_norm_w': gd_norm_w,
        'w_branch': w_branch, 'w_out': w_out, 'norm2_w': norm2_w,
        'ffn_w1': ffn_w1, 'ffn_w3': ffn_w3, 'ffn_w2': ffn_w2,
        'moe_router': moe_router, 'moe_router_b': moe_router_b, 'moe_w1': moe_w1, 'moe_w3': moe_w3, 'moe_w2': moe_w2,
        'final_norm_w': final_norm_w,
    }
    w_in_perm = [_permute_w_in(w_in[layer]) for layer in range(2)]
    bp, dt = x_prompt.shape[0], x_prompt.dtype
    depth = w_in.shape[0]
    y_prompt, (s5_p, rw_p, rwsh_p, gd_p, gdc_p, _) = _run_trunk(
        x_prompt,
        jnp.zeros((depth, bp, S5_GROUPS, S5_STATE, 2), dt),
        jnp.zeros((depth, bp, HEADS, HEAD_W, HEAD_W), dt),
        jnp.zeros((depth, bp, RW_COLS), dt),
        jnp.zeros((depth, bp, HEADS, HEAD_W, HEAD_W), dt),
        jnp.zeros((depth, bp, GD_CONV - 1, GD_QKV), dt),
        p, w_in_perm)
    y_sample, (s5_s, rw_s, rwsh_s, gd_s, gdc_s, sgv_s) = _run_trunk(
        x_sample, state_s5, state_rwkv, state_rwkv_shift, state_gdn, state_gdn_conv, p, w_in_perm)
    return (y_prompt, y_sample, s5_p, rw_p, rwsh_p, gd_p, gdc_p, s5_s, rw_s, rwsh_s, gd_s, gdc_s, sgv_s)
```

```python
import functools

import jax
import jax.numpy as jnp
from jax import lax
from jax.experimental import pallas as pl
from jax.experimental.pallas import tpu as pltpu
from jax.experimental.pallas import tpu_sc as plsc

F32 = jnp.float32
BF16 = jnp.bfloat16

D_MODEL = 1024
BRANCH_W = 256
HEADS = 4
HEAD_W = 64
S5_GROUPS = 16
S5_STATE = 64
S5_W = S5_GROUPS * S5_STATE
S5_ROWS = 8
SG_CHUNK = 128
SG_ROWS = 512
GD_CONV = 4
GD_QKV = 3 * BRANCH_W
RW_COLS = 1024
RW_EPS = 64e-5
NORM_EPS = 1e-6
N_EXPERTS = 8
LANES = 128

COL_RW = 0
COL_GD = 1024
COL_SG = 2048
COL_S5 = 2560
COL_AB = 2816
PROJ_COLS = 3072

VMEM_LIMIT = 48 * 1024 * 1024


def _cparams(sem):
    return pltpu.CompilerParams(dimension_semantics=sem, vmem_limit_bytes=VMEM_LIMIT)


def _dot(a, b):
    return jnp.dot(a.astype(BF16), b.astype(BF16), preferred_element_type=F32)


def _dot_nt(a, b):
    return lax.dot_general(a.astype(BF16), b.astype(BF16), (((1,), (1,)), ((), ())),
                           preferred_element_type=F32)


def _split3(a):
    hi = a.astype(BF16)
    r1 = a - hi.astype(F32)
    mid = r1.astype(BF16)
    lo = (r1 - mid.astype(F32)).astype(BF16)
    return hi, mid, lo


def _dot3_left(b_exact, a):
    hi, mid, lo = _split3(a)
    b = b_exact.astype(BF16)
    return (jnp.dot(b, hi, preferred_element_type=F32) + jnp.dot(b, mid, preferred_element_type=F32)
            + jnp.dot(b, lo, preferred_element_type=F32))


def _dot_hp(a, b):
    a0, a1, _ = _split3(a)
    b0, b1, _ = _split3(b)
    d = lambda x, y: jnp.dot(x, y, preferred_element_type=F32)
    return d(a0, b0) + (d(a0, b1) + d(a1, b0))


def _iota(shape, axis):
    return lax.broadcasted_iota(jnp.int32, shape, axis)


def _head_ones():
    r = _iota((BRANCH_W, BRANCH_W), 0) // HEAD_W
    c = _iota((BRANCH_W, BRANCH_W), 1) // HEAD_W
    return (r == c).astype(BF16)


def _head_mask(l):
    r = _iota((HEADS * l, BRANCH_W), 0) // l
    c = _iota((HEADS * l, BRANCH_W), 1) // HEAD_W
    return r == c


def _expand(x, mask):
    return jnp.where(mask, jnp.concatenate([x] * HEADS, axis=0), 0.0)


def _tri_masks(l):
    i = _iota((l, HEADS * l), 0)
    j = _iota((l, HEADS * l), 1) % l
    return j < i, j <= i


def _same_head(l):
    n = HEADS * l
    return (_iota((n, n), 0) // l) == (_iota((n, n), 1) // l)


def _expand_sq(x, same):
    return jnp.where(same, jnp.concatenate([x] * HEADS, axis=0), 0.0)


def _unit_lower_inverse(a_strict, l):
    i = _iota((l, HEADS * l), 0)
    j = _iota((l, HEADS * l), 1) % l
    eye = (i == j).astype(F32)
    same = _same_head(l)
    p = [-a for a in a_strict]
    t = [eye + x for x in p]
    k = 2
    while k < l:
        p = [_dot(x, _expand_sq(x, same)) for x in p]
        t = [y + _dot(y, _expand_sq(x, same)) for y, x in zip(t, p)]
        k *= 2
    return t


def _dot2(a, b_exact):
    hi = a.astype(BF16)
    lo = (a - hi.astype(F32)).astype(BF16)
    b = b_exact.astype(BF16)
    return jnp.dot(hi, b, preferred_element_type=F32) + jnp.dot(lo, b, preferred_element_type=F32)


def _cumsum_rows(x, l, nb=1):
    n = nb * l
    i, j = _iota((n, n), 0), _iota((n, n), 1)
    tri = ((j <= i) & ((i // l) == (j // l))).astype(BF16)
    return _dot3_left(tri, x)


def _softplus(x):
    return jnp.maximum(x, 0.0) + jnp.log(1.0 + jnp.exp(-jnp.abs(x)))


def _silu(x):
    return x * jax.nn.sigmoid(x)


def _norm_cast_kernel(x_ref, nw_ref, o_ref):
    x = x_ref[...]
    ms = jnp.mean(x * x, axis=-1, keepdims=True)
    o_ref[...] = (x * lax.rsqrt(ms + NORM_EPS) * nw_ref[...]).astype(BF16)


def _norm_cast(x2, norm_w, tm):
    n = x2.shape[0]
    spec = pl.BlockSpec((tm, D_MODEL), lambda i: (i, 0))
    return pl.pallas_call(
        _norm_cast_kernel,
        grid=(n // tm,),
        in_specs=[spec, pl.BlockSpec((1, D_MODEL), lambda i: (0, 0))],
        out_specs=spec,
        out_shape=jax.ShapeDtypeStruct((n, D_MODEL), BF16),
        compiler_params=_cparams(("parallel",)),
    )(x2, norm_w.reshape(1, D_MODEL))


def _s5_kernel(hn_ref, wz_ref, h0_ref, wb_ref, wc_ref, a2k_ref, apow_ref, d_ref, wg_ref, bg_ref,
               y_ref, hl_ref, hs_ref, hb_ref, *, tc, bb):
    c = pl.program_id(1)

    @pl.when(c == 0)
    def _():
        hs_ref[...] = h0_ref[...]

    hn = jnp.concatenate([hn_ref[b] for b in range(bb)], axis=0)
    u = jnp.dot(hn, wz_ref[...], preferred_element_type=F32)
    x = _dot(u, wb_ref[...])
    xr, xi = x[:, :S5_W], x[:, S5_W:]
    ng = bb * tc // S5_ROWS
    xr = xr.reshape(ng, S5_ROWS, S5_W)
    xi = xi.reshape(ng, S5_ROWS, S5_W)
    row = _iota((S5_ROWS, 1), 0)
    k, d = 0, 1
    while d < S5_ROWS:
        m = row >= d
        ar = jnp.where(m, a2k_ref[k:k + 1, :S5_W], 0.0)
        ai = jnp.where(m, a2k_ref[k:k + 1, S5_W:], 0.0)
        sr = pltpu.roll(xr, d, axis=1)
        si = pltpu.roll(xi, d, axis=1)
        xr, xi = xr + (ar * sr - ai * si), xi + (ar * si + ai * sr)
        k, d = k + 1, d * 2
    hb_ref[:, :S5_W] = xr.reshape(bb * tc, S5_W)
    hb_ref[:, S5_W:] = xi.reshape(bb * tc, S5_W)
    pr, pi_ = apow_ref[:, :S5_W], apow_ref[:, S5_W:]

    def group(gi, carry):
        out = []
        for b, (cr, ci) in enumerate(carry):
            rows = pl.ds(pl.multiple_of(b * tc + gi * S5_ROWS, S5_ROWS), S5_ROWS)
            hr = hb_ref[rows, :S5_W] + (pr * cr - pi_ * ci)
            hi = hb_ref[rows, S5_W:] + (pr * ci + pi_ * cr)
            hb_ref[rows, :S5_W] = hr
            hb_ref[rows, S5_W:] = hi
            out.append((hr[S5_ROWS - 1:S5_ROWS], hi[S5_ROWS - 1:S5_ROWS]))
        return tuple(out)

    state = tuple((hs_ref[b, :, :S5_W], hs_ref[b, :, S5_W:]) for b in range(bb))
    state = lax.fori_loop(0, tc // S5_ROWS, group, state, unroll=2)
    for b, (cr, ci) in enumerate(state):
        hs_ref[b, :, :S5_W] = cr
        hs_ref[b, :, S5_W:] = ci
    y = _dot(hb_ref[:, :S5_W], wc_ref[:S5_W]) + _dot(hb_ref[:, S5_W:], wc_ref[S5_W:])
    y = jax.nn.gelu(y + d_ref[...] * u)
    y = (y * jax.nn.sigmoid(_dot(y, wg_ref[...]) + bg_ref[...])).astype(y_ref.dtype)
    for b in range(bb):
        y_ref[b] = y[b * tc:(b + 1) * tc]

    @pl.when(c == pl.num_programs(1) - 1)
    def _():
        hl_ref[...] = hs_ref[...]


def _s5_tables(lam_re, lam_im, log_dt, b_c, c_c, tc):
    dt = jnp.exp(log_dt)[:, None]
    mag = jnp.exp(lam_re * dt)
    ab_re, ab_im = mag * jnp.cos(lam_im * dt), mag * jnp.sin(lam_im * dt)
    den = lam_re * lam_re + lam_im * lam_im
    nr = ab_re - 1.0
    cf_re = (nr * lam_re + ab_im * lam_im) / den
    cf_im = (ab_im * lam_re - nr * lam_im) / den
    br, bi = b_c[..., 0], b_c[..., 1]
    bb_re = cf_re[..., None] * br - cf_im[..., None] * bi
    bb_im = cf_re[..., None] * bi + cf_im[..., None] * br
    eye = jnp.eye(S5_GROUPS, dtype=F32)
    bd_in = lambda m: jnp.einsum('gph,gk->ghkp', m, eye).reshape(BRANCH_W, S5_W)
    wb = jnp.concatenate([bd_in(bb_re), bd_in(bb_im)], axis=1)
    cr, ci = c_c[..., 0], c_c[..., 1]
    bd_out = lambda m: jnp.einsum('ghp,gk->gpkh', m, eye).reshape(S5_W, BRANCH_W)
    wc = jnp.concatenate([bd_out(cr), -bd_out(ci)], axis=0)
    pr, pi_ = ab_re.reshape(1, S5_W), ab_im.reshape(1, S5_W)
    lv_r, lv_i = [], []
    tr, ti = pr, pi_
    d = 1
    while d < tc:
        lv_r.append(pr)
        lv_i.append(pi_)
        tr, ti = (jnp.concatenate([tr, tr * pr - ti * pi_], axis=0),
                  jnp.concatenate([ti, tr * pi_ + ti * pr], axis=0))
        pr, pi_ = pr * pr - pi_ * pi_, 2.0 * pr * pi_
        d *= 2
    n_lv = len(lv_r)
    pad = (-n_lv) % 8
    a2k = jnp.concatenate([jnp.concatenate(lv_r, axis=0), jnp.concatenate(lv_i, axis=0)], axis=1)
    a2k = jnp.pad(a2k, ((0, pad), (0, 0)))
    apow = jnp.concatenate([tr, ti], axis=1)
    return wb.astype(BF16), wc.astype(BF16), a2k, apow


def _w_cols(col, width):
    return pl.BlockSpec((D_MODEL, width), lambda b, c: (0, col // width))


def _s5_mixer(hn3, w_mix, h0, lam_re, lam_im, log_dt, b_c, c_c, d_skip, w_glu, b_glu, tc):
    bsz, t, _ = hn3.shape
    wb, wc, a2k, apow = _s5_tables(lam_re, lam_im, log_dt, b_c, c_c, S5_ROWS)
    h0f = jnp.concatenate([h0[..., 0].reshape(bsz, 1, S5_W), h0[..., 1].reshape(bsz, 1, S5_W)], axis=-1)
    full = lambda a: pl.BlockSpec(a.shape, lambda b, c: (0,) * a.ndim)
    d2, bg2, wg = d_skip.reshape(1, BRANCH_W), b_glu.reshape(1, BRANCH_W), w_glu.astype(BF16)
    bb = min(bsz, STACK_STREAMS)
    while bsz % bb:
        bb -= 1
    y, hl = pl.pallas_call(
        functools.partial(_s5_kernel, tc=tc, bb=bb),
        grid=(bsz // bb, t // tc),
        in_specs=[pl.BlockSpec((bb, tc, D_MODEL), lambda b, c: (b, c, 0)), _w_cols(COL_S5, BRANCH_W),
                  pl.BlockSpec((bb, 1, 2 * S5_W), lambda b, c: (b, 0, 0)),
                  full(wb), full(wc), full(a2k), full(apow), full(d2), full(wg), full(bg2)],
        out_specs=[pl.BlockSpec((bb, tc, BRANCH_W), lambda b, c: (b, c, 0)),
                   pl.BlockSpec((bb, 1, 2 * S5_W), lambda b, c: (b, 0, 0))],
        out_shape=[jax.ShapeDtypeStruct((bsz, t, BRANCH_W), BF16),
                   jax.ShapeDtypeStruct((bsz, 1, 2 * S5_W), F32)],
        scratch_shapes=[pltpu.VMEM((bb, 1, 2 * S5_W), F32), pltpu.VMEM((bb * tc, 2 * S5_W), F32)],
        compiler_params=_cparams(("parallel", "arbitrary")),
    )(hn3, w_mix, h0f, wb, wc, a2k, apow, d2, wg, bg2)
    h_last = jnp.stack([hl[:, 0, :S5_W].reshape(bsz, S5_GROUPS, S5_STATE),
                        hl[:, 0, S5_W:].reshape(bsz, S5_GROUPS, S5_STATE)], axis=-1)
    return y, h_last


def _rwkv_kernel(hn_ref, wz_ref, sh0_ref, s0_ref, mu_ref, w0_ref, w2_ref, a0_ref, a2_ref, g2_ref,
                 kk_ref, ka_ref, rk_ref, lnw_ref, lnb_ref,
                 y_ref, sl_ref, sho_ref, st_ref, zp_ref, *, l, bb, nt):
    c = pl.program_id(1)

    @pl.when(c == 0)
    def _():
        st_ref[...] = s0_ref[...]
        zp_ref[...] = sh0_ref[...]

    tl = nt * l
    z_all = jnp.dot(jnp.concatenate([hn_ref[b] for b in range(bb)], axis=0), wz_ref[...],
                    preferred_element_type=F32)
    row = _iota((tl, 1), 0)
    zms = []
    for b in range(bb):
        z = z_all[b * tl:(b + 1) * tl]
        prev = jnp.where(row == 0, zp_ref[b], pltpu.roll(z, 1, axis=0))
        zp_ref[b] = z[tl - 1:tl]
        zms.append(z + (prev - z) * mu_ref[...])
    zm = jnp.concatenate(zms, axis=0)
    r, k, v = zm[:, 0:256], zm[:, 256:512], zm[:, 512:768]
    lo = zm[:, 768:896]
    g_lo = zm[:, 896:1024]
    w_log = -_softplus(-(w0_ref[...] + _dot(jnp.tanh(lo), w2_ref[...]))) - 0.5
    lw = -jnp.exp(w_log)
    a = jax.nn.sigmoid(a0_ref[...] + _dot(lo, a2_ref[...]))
    g = _dot(jax.nn.sigmoid(g_lo), g2_ref[...])
    ones_h = _head_ones()
    kk = k * kk_ref[...]
    kk = kk * lax.rsqrt(_dot2(kk * kk, ones_h) + NORM_EPS)
    k = k * (1.0 + (a - 1.0) * ka_ref[...])
    kka = kk * a

    cum = _cumsum_rows(lw, l, bb * nt)
    p_incl = jnp.exp(cum)
    p_inv = jnp.exp(-cum)
    kt = kk * jnp.exp(cum - lw)
    rt = r * p_incl
    kh = k * p_inv
    ah = kka * p_inv

    hm = _head_mask(l)
    strict, incl = _tri_masks(l)
    chunks = [(b, j) for j in range(nt) for b in range(bb)]
    rows = {s: slice((s[0] * nt + s[1]) * l, (s[0] * nt + s[1] + 1) * l) for s in chunks}
    ex = lambda x: {s: _expand(x[rows[s]], hm) for s in chunks}
    kh_e, ah_e, v_e = ex(kh), ex(ah), ex(v)
    a_aa = [jnp.where(strict, _dot_nt(kt[rows[s]], ah_e[s]), 0.0) for s in chunks]
    a_ak = {s: jnp.where(strict, _dot_nt(kt[rows[s]], kh_e[s]), 0.0) for s in chunks}
    b_ra = {s: jnp.where(incl, _dot_nt(rt[rows[s]], ah_e[s]), 0.0) for s in chunks}
    b_rk = {s: jnp.where(incl, _dot_nt(rt[rows[s]], kh_e[s]), 0.0) for s in chunks}
    t_inv = dict(zip(chunks, _unit_lower_inverse(a_aa, l)))
    av = {s: _dot(a_ak[s], v_e[s]) for s in chunks}
    bv = {s: _dot(b_rk[s], v_e[s]) for s in chunks}
    to_end = {s: jnp.exp(cum[rows[s]][l - 1:l] - cum[rows[s]]) for s in chunks}
    head_blk = _same_head(HEAD_W)
    st = [st_ref[b] for b in range(bb)]
    yb = {}
    for j in range(nt):
        now = [(b, j) for b in range(bb)]
        rhs = {s: _dot_nt(kt[rows[s]], st[s[0]]) + av[s] for s in now}
        ys = {s: _dot_nt(rt[rows[s]], st[s[0]]) + bv[s] for s in now}
        u = {s: _dot(t_inv[s], _expand(rhs[s], hm)) for s in now}
        for s in now:
            yb[s] = ys[s] - _dot(b_ra[s], _expand(u[s], hm))
        lhs_t = {s: jnp.concatenate([v[rows[s]], -u[s]], axis=0).T for s in now}
        rhs_k = {s: jnp.concatenate([k[rows[s]] * to_end[s], kka[rows[s]] * to_end[s]], axis=0) for s in now}
        for s in now:
            p_last = p_incl[rows[s]][l - 1:l]
            st[s[0]] = st[s[0]] * p_last + jnp.where(head_blk, _dot(lhs_t[s], rhs_k[s]), 0.0)
    for b in range(bb):
        st_ref[b] = st[b]
    y = jnp.concatenate([yb[(b, j)] for b in range(bb) for j in range(nt)], axis=0)

    inv_w = 1.0 / HEAD_W
    mean = _dot2(y, ones_h) * inv_w
    yc = y - mean
    var = _dot2(yc * yc, ones_h) * inv_w
    y = yc * lax.rsqrt(var + RW_EPS) * lnw_ref[...] + lnb_ref[...]
    bonus = _dot2(r * k * rk_ref[...], ones_h) * v
    y = ((y + bonus) * g).astype(y_ref.dtype)
    for b in range(bb):
        y_ref[b] = y[b * tl:(b + 1) * tl]

    @pl.when(c == pl.num_programs(1) - 1)
    def _():
        sl_ref[...] = st_ref[...]
        sho_ref[...] = zp_ref[...]


MIX_CHUNK = 64
STACK_STREAMS = 4
STACK_ROWS = 512


def _streams_per_step(bsz, t, l):
    bb = min(bsz, STACK_STREAMS)
    while bsz % bb:
        bb -= 1
    nt = max(1, min(t, STACK_ROWS // bb) // l)
    while (t // l) % nt:
        nt -= 1
    return bb, nt


def _block_diag_heads(s):
    bsz = s.shape[0]
    eye = jnp.eye(HEADS, dtype=s.dtype)
    return jnp.einsum('bhij,hg->bhigj', s, eye).reshape(bsz, BRANCH_W, BRANCH_W)


def _diag_blocks(s):
    bsz = s.shape[0]
    s5 = s.reshape(bsz, HEADS, HEAD_W, HEADS, HEAD_W)
    return jnp.stack([s5[:, h, :, h, :] for h in range(HEADS)], axis=1)


def _rwkv_mixer(hn3, w_mix, shift0, s0, mu, w0, w2, a0, a2, g2, k_k, k_a, r_k, ln_w, ln_b, l):
    bsz, t, _ = hn3.shape
    row = lambda a: a.reshape(1, -1)
    w2p = jnp.concatenate([w2, jnp.zeros_like(w2)], axis=0).astype(BF16)
    a2p = jnp.concatenate([jnp.zeros_like(a2), a2], axis=0).astype(BF16)
    args = (shift0.reshape(bsz, 1, RW_COLS), _block_diag_heads(s0), row(mu), row(w0), w2p, row(a0), a2p,
            g2.astype(BF16), row(k_k), row(k_a), row(r_k), row(ln_w), row(ln_b))
    full = lambda a: pl.BlockSpec(a.shape, lambda b, c: (0,) * a.ndim)
    bb, nt = _streams_per_step(bsz, t, l)
    y, sl, sho = pl.pallas_call(
        functools.partial(_rwkv_kernel, l=l, bb=bb, nt=nt),
        grid=(bsz // bb, t // (nt * l)),
        in_specs=[pl.BlockSpec((bb, nt * l, D_MODEL), lambda b, c: (b, c, 0)), _w_cols(COL_RW, RW_COLS),
                  pl.BlockSpec((bb, 1, RW_COLS), lambda b, c: (b, 0, 0)),
                  pl.BlockSpec((bb, BRANCH_W, BRANCH_W), lambda b, c: (b, 0, 0))]
                 + [full(a) for a in args[2:]],
        out_specs=[pl.BlockSpec((bb, nt * l, BRANCH_W), lambda b, c: (b, c, 0)),
                   pl.BlockSpec((bb, BRANCH_W, BRANCH_W), lambda b, c: (b, 0, 0)),
                   pl.BlockSpec((bb, 1, RW_COLS), lambda b, c: (b, 0, 0))],
        out_shape=[jax.ShapeDtypeStruct((bsz, t, BRANCH_W), BF16),
                   jax.ShapeDtypeStruct((bsz, BRANCH_W, BRANCH_W), F32),
                   jax.ShapeDtypeStruct((bsz, 1, RW_COLS), F32)],
        scratch_shapes=[pltpu.VMEM((bb, BRANCH_W, BRANCH_W), F32), pltpu.VMEM((bb, 1, RW_COLS), F32)],
        compiler_params=_cparams(("parallel", "arbitrary")),
    )(hn3, w_mix, *args)
    return y, _diag_blocks(sl), sho[:, 0]


def _sgu_kernel(hn_ref, wz_ref, lnw_ref, lnb_ref, wm_ref, bias_ref, o_ref, v_ref, *, l, nc):
    z = jnp.dot(hn_ref[...], wz_ref[...], preferred_element_type=F32)
    zg = jax.nn.gelu(z)
    u, v = zg[:, :BRANCH_W], zg[:, BRANCH_W:]
    mean = jnp.mean(v, axis=-1, keepdims=True)
    vc = v - mean
    var = jnp.mean(vc * vc, axis=-1, keepdims=True)
    v = vc * lax.rsqrt(var + NORM_EPS) * lnw_ref[...] + lnb_ref[...]
    v_ref[...] = v
    hm = _head_mask(l)
    wm = wm_ref[...]
    for i in range(nc):
        rows = slice(i * l, (i + 1) * l)
        mixed = bias_ref[...] + _dot(wm, _expand(v[rows], hm))
        o_ref[rows, :] = (u[rows] * mixed).astype(o_ref.dtype)


def _sgu_mixer(hn3, w_mix, ln_w, ln_b, w_s, b_s):
    bsz, t, _ = hn3.shape
    l = min(SG_CHUNK, t)
    nc = max(1, min(SG_ROWS, t) // l)
    tril = jnp.tril(jnp.ones((l, l), F32))
    wm = jnp.transpose(w_s[:, :l, :l] * tril, (1, 0, 2)).reshape(l, HEADS * l).astype(BF16)
    bias = jnp.repeat(jnp.transpose(b_s[:, :l]), HEAD_W, axis=1)
    row = lambda a: a.reshape(1, -1)
    full = lambda a: pl.BlockSpec(a.shape, lambda b, c: (0,) * a.ndim)
    args = (row(ln_w), row(ln_b), wm, bias)
    return pl.pallas_call(
        functools.partial(_sgu_kernel, l=l, nc=nc),
        grid=(bsz, t // (nc * l)),
        in_specs=[pl.BlockSpec((None, nc * l, D_MODEL), lambda b, c: (b, c, 0)), _w_cols(COL_SG, 2 * BRANCH_W)]
                 + [full(a) for a in args],
        out_specs=[pl.BlockSpec((None, nc * l, BRANCH_W), lambda b, c: (b, c, 0)),
                   pl.BlockSpec((None, nc * l, BRANCH_W), lambda b, c: (b, c, 0))],
        out_shape=[jax.ShapeDtypeStruct((bsz, t, BRANCH_W), BF16),
                   jax.ShapeDtypeStruct((bsz, t, BRANCH_W), F32)],
        compiler_params=_cparams(("parallel", "parallel")),
    )(hn3, w_mix, *args)


def _gdn_kernel(hn_ref, wz_ref, wab_ref, cv0_ref, s0_ref, cw_ref, alog_ref, dtb_ref, nw_ref,
                y_ref, sl_ref, cvo_ref, st_ref, cv_ref, *, l, bb, nt):
    c = pl.program_id(1)

    @pl.when(c == 0)
    def _():
        st_ref[...] = s0_ref[...]
        cv_ref[...] = cv0_ref[...]

    tl = nt * l
    hn = jnp.concatenate([hn_ref[b] for b in range(bb)], axis=0)
    z_all = jnp.dot(hn, wz_ref[...], preferred_element_type=F32)
    ab = jnp.dot(hn, wab_ref[...], preferred_element_type=F32)
    row8 = _iota((8, 1), 0)
    convs, gates = [], []
    for b in range(bb):
        z = z_all[b * tl:(b + 1) * tl]
        qkv = z[:, :GD_QKV]
        gates.append(z[:, GD_QKV:])
        carry = cv_ref[b]
        cv_ref[b] = qkv[tl - 8:tl]
        conv = qkv * cw_ref[GD_CONV - 1:GD_CONV]
        for j in range(1, GD_CONV):
            sh = pltpu.roll(qkv, j, axis=0)
            top = jnp.where(row8 < j, pltpu.roll(carry, j, axis=0), sh[:8])
            sh = jnp.concatenate([top, sh[8:]], axis=0) if tl > 8 else top
            conv = conv + sh * cw_ref[GD_CONV - 1 - j:GD_CONV - j]
        convs.append(conv)
    conv = _silu(jnp.concatenate(convs, axis=0))
    gate = jnp.concatenate(gates, axis=0)
    q, k, v = conv[:, :256], conv[:, 256:512], conv[:, 512:768]
    ones_h = _head_ones()
    q = q * lax.rsqrt(_dot2(q * q, ones_h) + NORM_EPS) * (HEAD_W ** -0.5)
    k = k * lax.rsqrt(_dot2(k * k, ones_h) + NORM_EPS)
    lane_h = _iota((bb * tl, BRANCH_W), 1) // HEAD_W
    a_in = jnp.zeros((bb * tl, BRANCH_W), F32)
    b_in = jnp.zeros((bb * tl, BRANCH_W), F32)
    for h in range(HEADS):
        a_in = jnp.where(lane_h == h, ab[:, h:h + 1], a_in)
        b_in = jnp.where(lane_h == h, ab[:, HEADS + h:HEADS + h + 1], b_in)
    beta = jax.nn.sigmoid(b_in)
    g = -jnp.exp(alog_ref[...]) * _softplus(a_in + dtb_ref[...])
    gc = _cumsum_rows(g, l, bb * nt)
    eg = jnp.exp(gc)
    kb = k * beta
    vb = v * beta
    kbg = kb * eg
    qg = q * eg

    hm = _head_mask(l)
    strict, incl = _tri_masks(l)
    n = HEADS * l
    chunks = [(b, j) for j in range(nt) for b in range(bb)]
    rows = {s: slice((s[0] * nt + s[1]) * l, (s[0] * nt + s[1] + 1) * l) for s in chunks}
    ex = lambda x: {s: _expand(x[rows[s]], hm) for s in chunks}
    k_e, vb_e, kbg_e = ex(k), ex(vb), ex(kbg)
    lane_hd = _iota((l, n), 1) // l
    eye = _iota((l, n), 0) == (_iota((l, n), 1) % l)
    decay = {}
    for s in chunks:
        gi = jnp.zeros((l, n), F32)
        for h in range(HEADS):
            gi = jnp.where(lane_hd == h, gc[rows[s]][:, h * HEAD_W:h * HEAD_W + 1], gi)
        gj = jnp.sum(jnp.where(eye, gi, 0.0), axis=0, keepdims=True)
        decay[s] = jnp.where(incl, jnp.exp(jnp.where(incl, gi - gj, 0.0)), 0.0)
    lm = [jnp.where(strict, _dot_nt(kb[rows[s]], k_e[s]) * decay[s], 0.0) for s in chunks]
    qk = {s: _dot_nt(q[rows[s]], k_e[s]) * decay[s] for s in chunks}
    t_inv = dict(zip(chunks, _unit_lower_inverse(lm, l)))
    uc = {s: _dot(t_inv[s], vb_e[s]) for s in chunks}
    wc = {s: _dot(t_inv[s], kbg_e[s]) for s in chunks}
    g_last = {s: gc[rows[s]][l - 1:l] for s in chunks}
    k_dec = {s: (k[rows[s]] * jnp.exp(g_last[s] - gc[rows[s]])).T for s in chunks}
    head_blk = _same_head(HEAD_W)
    st = [st_ref[b] for b in range(bb)]
    ob = {}
    for j in range(nt):
        now = [(b, j) for b in range(bb)]
        o0 = {s: _dot(qg[rows[s]], st[s[0]]) for s in now}
        v_new = {s: uc[s] - _dot(wc[s], st[s[0]]) for s in now}
        for s in now:
            ob[s] = o0[s] + _dot(qk[s], _expand(v_new[s], hm))
        for s in now:
            st[s[0]] = st[s[0]] * jnp.exp(g_last[s]) + jnp.where(head_blk, _dot(k_dec[s], v_new[s]), 0.0)
    for b in range(bb):
        st_ref[b] = st[b]
    o = jnp.concatenate([ob[(b, j)] for b in range(bb) for j in range(nt)], axis=0)
    ms = _dot2(o * o, ones_h) * (1.0 / HEAD_W)
    o = (o * lax.rsqrt(ms + NORM_EPS) * nw_ref[...] * _silu(gate)).astype(y_ref.dtype)
    for b in range(bb):
        y_ref[b] = o[b * tl:(b + 1) * tl]

    @pl.when(c == pl.num_programs(1) - 1)
    def _():
        sl_ref[...] = st_ref[...]
        cvo_ref[...] = cv_ref[...]


def _gdn_mixer(hn3, w_mix, conv0, s0, conv_w, a_log, dt_bias, norm_w, l):
    bsz, t, _ = hn3.shape
    cv0 = jnp.pad(conv0, ((0, 0), (8 - (GD_CONV - 1), 0), (0, 0)))
    cw = jnp.pad(conv_w, ((0, 8 - GD_CONV), (0, 0)))
    per_head = lambda a: jnp.repeat(a, HEAD_W).reshape(1, BRANCH_W)
    args = (cv0, _block_diag_heads(s0), cw, per_head(a_log), per_head(dt_bias),
            jnp.tile(norm_w, HEADS).reshape(1, BRANCH_W))
    full = lambda a: pl.BlockSpec(a.shape, lambda b, c: (0,) * a.ndim)
    bb, nt = _streams_per_step(bsz, t, l)
    y, sl, cvo = pl.pallas_call(
        functools.partial(_gdn_kernel, l=l, bb=bb, nt=nt),
        grid=(bsz // bb, t // (nt * l)),
        in_specs=[pl.BlockSpec((bb, nt * l, D_MODEL), lambda b, c: (b, c, 0)),
                  _w_cols(COL_GD, 1024), _w_cols(COL_AB, LANES),
                  pl.BlockSpec((bb, 8, GD_QKV), lambda b, c: (b, 0, 0)),
                  pl.BlockSpec((bb, BRANCH_W, BRANCH_W), lambda b, c: (b, 0, 0))]
                 + [full(a) for a in args[2:]],
        out_specs=[pl.BlockSpec((bb, nt * l, BRANCH_W), lambda b, c: (b, c, 0)),
                   pl.BlockSpec((bb, BRANCH_W, BRANCH_W), lambda b, c: (b, 0, 0)),
                   pl.BlockSpec((bb, 8, GD_QKV), lambda b, c: (b, 0, 0))],
        out_shape=[jax.ShapeDtypeStruct((bsz, t, BRANCH_W), BF16),
                   jax.ShapeDtypeStruct((bsz, BRANCH_W, BRANCH_W), F32),
                   jax.ShapeDtypeStruct((bsz, 8, GD_QKV), F32)],
        scratch_shapes=[pltpu.VMEM((bb, BRANCH_W, BRANCH_W), F32), pltpu.VMEM((bb, 8, GD_QKV), F32)],
        compiler_params=_cparams(("parallel", "arbitrary")),
    )(hn3, w_mix, w_mix, *args)
    return y, _diag_blocks(sl), cvo[:, 8 - (GD_CONV - 1):]


def _merge_kernel(ya_ref, yb_ref, yc_ref, yd_ref, x_ref, nw_ref, wg_ref, wbr_ref, wout_ref, o_ref):
    x = x_ref[...]
    ms = jnp.mean(x * x, axis=-1, keepdims=True)
    h = (x * lax.rsqrt(ms + NORM_EPS) * nw_ref[...]).astype(BF16)
    m = None
    for b, y_ref in enumerate((ya_ref, yb_ref, yc_ref, yd_ref)):
        gate = jnp.dot(h, wg_ref[:, b * D_MODEL:(b + 1) * D_MODEL], preferred_element_type=F32)
        br = jnp.dot(y_ref[...], wbr_ref[b], preferred_element_type=F32)
        term = (0.5 * jnp.tanh(0.5 * gate) + 0.5) * br
        m = term if m is None else m + term
    o_ref[...] = x + jnp.dot(m.astype(BF16), wout_ref[...], preferred_element_type=F32)


def _merge(ys, x2, norm_w, w_gate, w_branch, w_out, tm):
    n = x2.shape[0]
    yspec = pl.BlockSpec((tm, BRANCH_W), lambda i: (i, 0))
    once = pl.Buffered(1)
    return pl.pallas_call(
        _merge_kernel,
        grid=(n // tm,),
        in_specs=[yspec, yspec, yspec, yspec,
                  pl.BlockSpec((tm, D_MODEL), lambda i: (i, 0)),
                  pl.BlockSpec((1, D_MODEL), lambda i: (0, 0)),
                  pl.BlockSpec((D_MODEL, 4 * D_MODEL), lambda i: (0, 0), pipeline_mode=once),
                  pl.BlockSpec((4, BRANCH_W, D_MODEL), lambda i: (0, 0, 0), pipeline_mode=once),
                  pl.BlockSpec((D_MODEL, D_MODEL), lambda i: (0, 0), pipeline_mode=once)],
        out_specs=pl.BlockSpec((tm, D_MODEL), lambda i: (i, 0)),
        out_shape=jax.ShapeDtypeStruct((n, D_MODEL), F32),
        compiler_params=_cparams(("parallel",)),
    )(*[y.reshape(n, BRANCH_W) for y in ys], x2, norm_w.reshape(1, D_MODEL), w_gate,
      w_branch.astype(BF16), w_out.astype(BF16))


def _ffn_kernel(x_ref, nw_ref, w1_ref, w3_ref, w2_ref, nnw_ref, o_ref, hn_ref):
    x = x_ref[...]
    ms = jnp.mean(x * x, axis=-1, keepdims=True)
    h = (x * lax.rsqrt(ms + NORM_EPS) * nw_ref[...]).astype(BF16)
    a = _silu(jnp.dot(h, w1_ref[...], preferred_element_type=F32)) * jnp.dot(h, w3_ref[...], preferred_element_type=F32)
    y = x + jnp.dot(a.astype(BF16), w2_ref[...], preferred_element_type=F32)
    o_ref[...] = y
    ms = jnp.mean(y * y, axis=-1, keepdims=True)
    hn_ref[...] = (y * lax.rsqrt(ms + NORM_EPS) * nnw_ref[...]).astype(BF16)


def _ffn(x2, norm_w, w1, w3, w2, next_norm_w, tm):
    n = x2.shape[0]
    dff = w1.shape[1]
    once = pl.Buffered(1)
    spec = pl.BlockSpec((tm, D_MODEL), lambda i: (i, 0))
    vec = pl.BlockSpec((1, D_MODEL), lambda i: (0, 0))
    return pl.pallas_call(
        _ffn_kernel,
        grid=(n // tm,),
        in_specs=[spec, vec,
                  pl.BlockSpec((D_MODEL, dff), lambda i: (0, 0), pipeline_mode=once),
                  pl.BlockSpec((D_MODEL, dff), lambda i: (0, 0), pipeline_mode=once),
                  pl.BlockSpec((dff, D_MODEL), lambda i: (0, 0), pipeline_mode=once), vec],
        out_specs=[spec, spec],
        out_shape=[jax.ShapeDtypeStruct((n, D_MODEL), F32), jax.ShapeDtypeStruct((n, D_MODEL), BF16)],
        compiler_params=_cparams(("parallel",)),
    )(x2, norm_w.reshape(1, D_MODEL), w1.astype(BF16), w3.astype(BF16), w2.astype(BF16),
      next_norm_w.reshape(1, D_MODEL))


SC_CORES = 2
SC_SUBCORES = 16
SC_WINDOW = 128
SC_PIECE = 256
MOE_BLOCK = 512
MOE_FF = 1792
ROUTER_ROWS = 256
HALF = D_MODEL // 2


HIGH16 = -65536


def _pack_bf16_pairs(h):
    bits = lax.bitcast_convert_type(h.astype(BF16).astype(F32), jnp.int32)
    return lax.shift_right_logical(bits[:, :HALF], 16) | (bits[:, HALF:] & HIGH16)


def _unpack_bf16_pairs(pieces):
    lo = [lax.bitcast_convert_type(lax.shift_left(w, 16), F32) for w in pieces]
    hi = [lax.bitcast_convert_type(w & HIGH16, F32) for w in pieces]
    return jnp.concatenate(lo + hi, axis=1).astype(BF16)


def _route_kernel(x_ref, nw_ref, wr_ref, br_ref, hp_ref, tok_ref, cnt_ref, seen_ref, *, tm):
    @pl.when(pl.program_id(0) == 0)
    def _():
        seen_ref[...] = jnp.zeros_like(seen_ref)

    x = x_ref[...]
    ms = jnp.mean(x * x, axis=-1, keepdims=True)
    h = x * lax.rsqrt(ms + NORM_EPS) * nw_ref[...]
    hp = _pack_bf16_pairs(h)
    for q in range(HALF // SC_PIECE):
        hp_ref[q] = hp[:, q * SC_PIECE:(q + 1) * SC_PIECE]
    lane = _iota((tm, LANES), 1)
    logits = jnp.where(lane < N_EXPERTS, _dot_hp(h, wr_ref[...]) + br_ref[...], -jnp.inf)
    m1 = jnp.max(logits, axis=1, keepdims=True)
    i1 = jnp.min(jnp.where(logits == m1, lane, LANES), axis=1, keepdims=True)
    rest = jnp.where(lane == i1, -jnp.inf, logits)
    m2 = jnp.max(rest, axis=1, keepdims=True)
    i2 = jnp.min(jnp.where(rest == m2, lane, LANES), axis=1, keepdims=True)
    e2 = jnp.exp(m2 - m1)
    g1 = 1.0 / (1.0 + e2)
    g2 = e2 / (1.0 + e2)
    self32 = ((lane == i1) | (lane == i2)).astype(F32)
    seen = seen_ref[...]
    tri = (_iota((tm, tm), 1) < _iota((tm, tm), 0)).astype(BF16)
    rank = jnp.dot(tri, self32.astype(BF16), preferred_element_type=F32) + seen
    r1 = jnp.sum(jnp.where(lane == i1, rank, 0.0), axis=1, keepdims=True)
    r2 = jnp.sum(jnp.where(lane == i2, rank, 0.0), axis=1, keepdims=True)
    cols = (i1.astype(F32), i2.astype(F32), r1, r2, g1, g2)
    tok = jnp.zeros((tm, LANES), F32)
    for j, col in enumerate(cols):
        tok = jnp.where(lane == j, col, tok)
    tok_ref[...] = tok
    seen = seen + jnp.sum(self32, axis=0, keepdims=True)
    seen_ref[...] = seen
    cnt_ref[...] = jnp.broadcast_to(seen, (8, LANES))


def _route(x2, norm_w, w_router, b_router, tm):
    n = x2.shape[0]
    wr = jnp.pad(w_router, ((0, 0), (0, LANES - N_EXPERTS)))
    br = jnp.pad(b_router, (0, LANES - N_EXPERTS)).reshape(1, LANES)
    return pl.pallas_call(
        functools.partial(_route_kernel, tm=tm),
        grid=(n // tm,),
        in_specs=[pl.BlockSpec((tm, D_MODEL), lambda i: (i, 0)),
                  pl.BlockSpec((1, D_MODEL), lambda i: (0, 0)),
                  pl.BlockSpec((D_MODEL, LANES), lambda i: (0, 0)),
                  pl.BlockSpec((1, LANES), lambda i: (0, 0))],
        out_specs=[pl.BlockSpec((HALF // SC_PIECE, tm, SC_PIECE), lambda i: (0, i, 0)),
                   pl.BlockSpec((tm, LANES), lambda i: (i, 0)),
                   pl.BlockSpec((8, LANES), lambda i: (0, 0))],
        out_shape=[jax.ShapeDtypeStruct((HALF // SC_PIECE, n, SC_PIECE), jnp.int32),
                   jax.ShapeDtypeStruct((n, LANES), F32),
                   jax.ShapeDtypeStruct((8, LANES), F32)],
        scratch_shapes=[pltpu.VMEM((1, LANES), F32)],
        compiler_params=_cparams(("arbitrary",)),
    )(x2, norm_w.reshape(1, D_MODEL), wr, br)


def _sc_rows_multiple(d):
    return SC_CORES * SC_SUBCORES * SC_WINDOW * SC_PIECE // d


def _sc_gather_rows(table, idx):
    f, v, _ = table.shape
    b = idx.shape[0]
    assert b % _sc_rows_multiple(f * SC_PIECE) == 0
    idx_all = (idx[None, :] + (jnp.arange(f, dtype=jnp.int32) * v)[:, None]).reshape(-1)
    return _sc_gather_pieces(table.reshape(f * v, SC_PIECE), idx_all).reshape(f, b, SC_PIECE)


def _sc_gather_pieces(table, idx):
    bp = idx.shape[0]
    d = table.shape[1]
    window = SC_WINDOW
    idx2 = idx.reshape(1, bp)
    mesh = plsc.VectorSubcoreMesh(core_axis_name="core", subcore_axis_name="subcore")

    @functools.partial(pl.kernel, out_type=jax.ShapeDtypeStruct((bp, d), table.dtype), mesh=mesh)
    def gather(x_hbm, i_hbm, o_hbm):
        def body(i_vmem, o_vmem):
            pltpu.sync_copy(x_hbm.at[i_vmem.at[0]], o_vmem)

        pltpu.emit_pipeline(
            body,
            grid=(bp // window,),
            in_specs=[pl.BlockSpec((1, window), index_map=lambda i: (0, i))],
            out_specs=[pl.BlockSpec((window, d), index_map=lambda i: (i, 0))],
            core_axis_name=("core", "subcore"),
            dimension_semantics=(pltpu.PARALLEL,),
        )(i_hbm, o_hbm)

    return gather(table, idx2)


def _sc_scatter_rows(rows, pos, n_slots):
    f, n0, _ = rows.shape
    n = -(-n0 // _sc_rows_multiple(D_MODEL)) * _sc_rows_multiple(D_MODEL)
    if n != n0:
        rows = jnp.pad(rows, ((0, 0), (0, n - n0), (0, 0)))
        pos = jnp.pad(pos.reshape(2, n0), ((0, 0), (0, n - n0)), constant_values=n_slots - 1)
    nb = n // SC_WINDOW
    assert (f * 2 * nb) % (SC_CORES * SC_SUBCORES) == 0
    idx = (pos.reshape(1, 2 * n) + (jnp.arange(f, dtype=jnp.int32) * n_slots)[:, None]).reshape(1, f * 2 * n)
    mesh = plsc.VectorSubcoreMesh(core_axis_name="core", subcore_axis_name="subcore")

    @functools.partial(pl.kernel, out_type=jax.ShapeDtypeStruct((f * n_slots, SC_PIECE), rows.dtype), mesh=mesh)
    def scatter(x_hbm, i_hbm, o_hbm):
        def body(x_vmem, i_vmem):
            pltpu.sync_copy(x_vmem, o_hbm.at[i_vmem.at[0]])

        pltpu.emit_pipeline(
            body,
            grid=(f * 2 * nb,),
            in_specs=[pl.BlockSpec((SC_WINDOW, SC_PIECE), index_map=lambda i: ((i // (2 * nb)) * nb + i % nb, 0)),
                      pl.BlockSpec((1, SC_WINDOW), index_map=lambda i: (0, i))],
            out_specs=[],
            core_axis_name=("core", "subcore"),
            dimension_semantics=(pltpu.PARALLEL,),
        )(x_hbm, i_hbm)

    return scatter(rows.reshape(f * n, SC_PIECE), idx).reshape(f, n_slots, SC_PIECE)


def _experts_kernel(be_ref, nb_ref, nv_ref, xs_ref, w1_ref, w3_ref, w2_ref, o_ref):
    g, c = pl.program_id(0), pl.program_id(1)

    @pl.when(c == 0)
    def _():
        o_ref[...] = jnp.zeros_like(o_ref)

    @pl.when(g < nb_ref[0])
    def _():
        live = _iota((xs_ref.shape[1], 1), 0) < nv_ref[g]
        xb = _unpack_bf16_pairs([jnp.where(live, xs_ref[q], 0) for q in range(HALF // SC_PIECE)])
        a = (_silu(jnp.dot(xb, w1_ref[...], preferred_element_type=F32))
             * jnp.dot(xb, w3_ref[...], preferred_element_type=F32))
        y = jnp.dot(a.astype(BF16), w2_ref[...], preferred_element_type=F32)
        for q in range(D_MODEL // SC_PIECE):
            o_ref[q] += y[:, q * SC_PIECE:(q + 1) * SC_PIECE]


def _experts(xs, block_expert, n_blocks, block_rows, w1, w3, w2, blk, tf):
    n_slots = xs.shape[1]
    dff = w1.shape[2]
    grid_spec = pltpu.PrefetchScalarGridSpec(
        num_scalar_prefetch=3,
        grid=(n_slots // blk, dff // tf),
        in_specs=[pl.BlockSpec((HALF // SC_PIECE, blk, SC_PIECE), lambda g, c, be, nb, nv: (0, g, 0)),
                  pl.BlockSpec((None, D_MODEL, tf), lambda g, c, be, nb, nv: (be[g], 0, c)),
                  pl.BlockSpec((None, D_MODEL, tf), lambda g, c, be, nb, nv: (be[g], 0, c)),
                  pl.BlockSpec((None, tf, D_MODEL), lambda g, c, be, nb, nv: (be[g], c, 0))],
        out_specs=pl.BlockSpec((D_MODEL // SC_PIECE, blk, SC_PIECE), lambda g, c, be, nb, nv: (0, g, 0)),
    )
    return pl.pallas_call(
        _experts_kernel,
        grid_spec=grid_spec,
        out_shape=jax.ShapeDtypeStruct((D_MODEL // SC_PIECE, n_slots, SC_PIECE), F32),
        compiler_params=_cparams(("parallel", "arbitrary")),
    )(block_expert, n_blocks, block_rows, xs, w1, w3, w2)


def _combine_kernel(x_ref, y1_ref, y2_ref, tok_ref, nw_ref, o_ref):
    tok = tok_ref[...]
    rows = lambda y_ref: jnp.concatenate([y_ref[q] for q in range(D_MODEL // SC_PIECE)], axis=1)
    x = x_ref[...] + tok[:, 4:5] * rows(y1_ref) + tok[:, 5:6] * rows(y2_ref)
    ms = jnp.mean(x * x, axis=-1, keepdims=True)
    o_ref[...] = x * lax.rsqrt(ms + NORM_EPS) * nw_ref[...]


def _combine(x2, ys2, tok, norm_w, tm):
    n = x2.shape[0]
    spec = pl.BlockSpec((tm, D_MODEL), lambda i: (i, 0))
    yspec = lambda off: pl.BlockSpec((D_MODEL // SC_PIECE, tm, SC_PIECE), lambda i: (0, i + off, 0))
    return pl.pallas_call(
        _combine_kernel,
        grid=(n // tm,),
        in_specs=[spec, yspec(0), yspec(n // tm),
                  pl.BlockSpec((tm, LANES), lambda i: (i, 0)), pl.BlockSpec((1, D_MODEL), lambda i: (0, 0))],
        out_specs=spec,
        out_shape=jax.ShapeDtypeStruct((n, D_MODEL), F32),
        compiler_params=_cparams(("parallel",)),
    )(x2, ys2, ys2, tok, norm_w.reshape(1, D_MODEL))


def _moe_final(x2, norm_w, w_router, b_router, w1, w3, w2, final_norm_w):
    n = x2.shape[0]
    blk = min(MOE_BLOCK, n)
    hp, tok, cnt = _route(x2, norm_w, w_router, b_router, ROUTER_ROWS)
    counts = cnt[0, :N_EXPERTS].astype(jnp.int32)
    padded = (counts + blk - 1) // blk * blk
    ends = jnp.cumsum(padded)
    base = ends - padded
    round_up = lambda a, m: -(-a // m) * m
    n_slots = round_up((-(-2 * n // blk) + N_EXPERTS + 1) * blk, _sc_rows_multiple(HALF))
    n_blocks = n_slots // blk
    i12 = tok[:, 0:2].astype(jnp.int32)
    pos = (base[i12] + tok[:, 2:4].astype(jnp.int32)).T.reshape(-1)
    block_start = jnp.arange(n_blocks, dtype=jnp.int32) * blk
    block_expert = jnp.minimum(jnp.sum(block_start[:, None] >= ends[None, :], axis=1), N_EXPERTS - 1).astype(jnp.int32)
    block_rows = jnp.clip((base + counts)[block_expert] - block_start, 0, blk).astype(jnp.int32)
    xs = _sc_scatter_rows(hp, pos, n_slots)
    ys = _experts(xs, block_expert, (ends[-1:] // blk).astype(jnp.int32), block_rows,
                  w1.astype(BF16), w3.astype(BF16), w2.astype(BF16), blk, MOE_FF)
    pos = jnp.pad(pos, (0, round_up(2 * n, _sc_rows_multiple(D_MODEL)) - 2 * n))
    ys2 = _sc_gather_rows(ys, pos)
    return _combine(x2, ys2, tok, final_norm_w, min(1024, n))


def _permute_w_in(w):
    s5, rw, sg = w[:, 0:256], w[:, 256:1280], w[:, 1280:1792]
    gd, ab, gates = w[:, 1792:2816], w[:, 2816:2824], w[:, 2824:6920]
    pad = jnp.zeros((D_MODEL, PROJ_COLS - COL_AB - 8), w.dtype)
    return jnp.concatenate([rw, gd, sg, s5, ab, pad], axis=1).astype(BF16), gates.astype(BF16)


def _run_trunk(x, s5_h, rw_s, rw_shift, gd_s, gd_conv, p, w_in_perm):
    bsz, t, _ = x.shape
    n = bsz * t
    l = min(MIX_CHUNK, t)
    tc_s5 = min(256, t)
    new = ([], [], [], [], [], [])
    x2 = x.reshape(n, D_MODEL)
    for layer in range(2):
        g = lambda name: p[name][layer]
        w_mix, w_gate = w_in_perm[layer]
        if layer == 0:
            hn = _norm_cast(x2, g('norm1_w'), min(1024, n))
        hn3 = hn.reshape(bsz, t, D_MODEL)
        y_a, s5_new = _s5_mixer(hn3, w_mix, s5_h[layer], g('s5_lam_re'), g('s5_lam_im'), g('s5_log_dt'), g('s5_b'),
                                g('s5_c'), g('s5_d'), g('s5_w_glu'), g('s5_b_glu'), tc_s5)
        y_b, rw_new, shift_new = _rwkv_mixer(hn3, w_mix, rw_shift[layer], rw_s[layer], g('rw_mu'), g('rw_w0'),
                                             g('rw_w2'), g('rw_a0'), g('rw_a2'), g('rw_g2'), g('rw_k_k'),
                                             g('rw_k_a'), g('rw_r_k'), g('rw_ln_w'), g('rw_ln_b'), l)
        y_c, sg_v = _sgu_mixer(hn3, w_mix, g('sg_ln_w'), g('sg_ln_b'), g('sg_w_s'), g('sg_b_s'))
        y_d, gd_new, conv_new = _gdn_mixer(hn3, w_mix, gd_conv[layer], gd_s[layer], g('gd_conv_w'), g('gd_a_log'),
                                           g('gd_dt_bias'), g('gd_norm_w'), l)
        x2 = _merge((y_a, y_b, y_c, y_d), x2, g('norm1_w'), w_gate, g('w_branch'), g('w_out'), min(512, n))
        j = layer // 2
        if layer % 2 == 0:
            x2, hn = _ffn(x2, g('norm2_w'), p['ffn_w1'][j], p['ffn_w3'][j], p['ffn_w2'][j],
                          p['norm1_w'][layer + 1], min(512, n))
        else:
            y = _moe_final(x2, g('norm2_w'), p['moe_router'][j], p['moe_router_b'][j],
                           p['moe_w1'][j], p['moe_w3'][j], p['moe_w2'][j], p['final_norm_w'])
        for lst, s in zip(new, (s5_new, rw_new, shift_new, gd_new, conv_new, sg_v)):
            lst.append(s)
    return y.reshape(bsz, t, D_MODEL), [jnp.stack(lst) for lst in new]


def kernel(x_prompt, x_sample, state_s5, state_rwkv, state_rwkv_shift, state_gdn, state_gdn_conv, norm1_w, w_in, s5_lam_re, s5_lam_im, s5_log_dt, s5_b, s5_c, s5_d, s5_w_glu, s5_b_glu, rw_mu, rw_w0, rw_w2, rw_a0, rw_a2, rw_g2, rw_k_k, rw_k_a, rw_r_k, rw_ln_w, rw_ln_b, sg_ln_w, sg_ln_b, sg_w_s, sg_b_s, gd_conv_w, gd_a_log, gd_dt_bias, gd_norm_w, w_branch, w_out, norm2_w, ffn_w1, ffn_w3, ffn_w2, moe_router, moe_router_b, moe_w1, moe_w3, moe_w2, final_norm_w):
    p = {
        'norm1_w': norm1_w, 's5_lam_re': s5_lam_re, 's5_lam_im': s5_lam_im, 's5_log_dt': s5_log_dt,
        's5_b': s5_b, 's5_c': s5_c, 's5_d': s5_d, 's5_w_glu': s5_w_glu, 's5_b_glu': s5_b_glu,
        'rw_mu': rw_mu, 'rw_w0': rw_w0, 'rw_w2': rw_w2, 'rw_a0': rw_a0, 'rw_a2': rw_a2, 'rw_g2': rw_g2,
        'rw_k_k': rw_k_k, 'rw_k_a': rw_k_a, 'rw_r_k': rw_r_k, 'rw_ln_w': rw_ln_w, 'rw_ln_b': rw_ln_b,
        'sg_ln_w': sg_ln_w, 'sg_ln_b': sg_ln_b, 'sg_w_s': sg_w_s, 'sg_b_s': sg_b_s,
        'gd_conv_w': gd_conv_w, 'gd_a_log': gd_a_log, 'gd_dt_bias': gd_dt_bias, 'gd_norm_w': gd_norm_w,
        'w_branch': w_branch, 'w_out': w_out, 'norm2_w': norm2_w,
        'ffn_w1': ffn_w1, 'ffn_w3': ffn_w3, 'ffn_w2': ffn_w2,
        'moe_router': moe_router, 'moe_router_b': moe_router_b, 'moe_w1': moe_w1, 'moe_w3': moe_w3, 'moe_w2': moe_w2,
        'final_norm_w': final_norm_w,
    }
    w_in_perm = [_permute_w_in(w_in[layer]) for layer in range(2)]
    bp, dt = x_prompt.shape[0], x_prompt.dtype
    depth = w_in.shape[0]
    y_prompt, (s5_p, rw_p, rwsh_p, gd_p, gdc_p, _) = _run_trunk(
        x_prompt,
        jnp.zeros((depth, bp, S5_GROUPS, S5_STATE, 2), dt),
        jnp.zeros((depth, bp, HEADS, HEAD_W, HEAD_W), dt),
        jnp.zeros((depth, bp, RW_COLS), dt),
        jnp.zeros((depth, bp, HEADS, HEAD_W, HEAD_W), dt),
        jnp.zeros((depth, bp, GD_CONV - 1, GD_QKV), dt),
        p, w_in_perm)
    y_sample, (s5_s, rw_s, rwsh_s, gd_s, gdc_s, sgv_s) = _run_trunk(
        x_sample, state_s5, state_rwkv, state_rwkv_shift, state_gdn, state_gdn_conv, p, w_in_perm)
    return (y_prompt, y_sample, s5_p, rw_p, rwsh_p, gd_p, gdc_p, s5_s, rw_s, rwsh_s, gd_s, gdc_s, sgv_s)
```

```python
import functools

import jax
import jax.numpy as jnp
from jax import lax
from jax.experimental import pallas as pl
from jax.experimental.pallas import tpu as pltpu
from jax.experimental.pallas import tpu_sc as plsc

F32 = jnp.float32
BF16 = jnp.bfloat16

D_MODEL = 1024
BRANCH_W = 256
HEADS = 4
HEAD_W = 64
S5_GROUPS = 16
S5_STATE = 64
S5_W = S5_GROUPS * S5_STATE
S5_ROWS = 8
SG_CHUNK = 128
SG_ROWS = 512
GD_CONV = 4
GD_QKV = 3 * BRANCH_W
RW_COLS = 1024
RW_EPS = 64e-5
NORM_EPS = 1e-6
N_EXPERTS = 8
LANES = 128

COL_RW = 0
COL_GD = 1024
COL_SG = 2048
COL_S5 = 2560
COL_AB = 2816
PROJ_COLS = 3072

VMEM_LIMIT = 48 * 1024 * 1024


def _cparams(sem):
    return pltpu.CompilerParams(dimension_semantics=sem, vmem_limit_bytes=VMEM_LIMIT)


def _dot(a, b):
    return jnp.dot(a.astype(BF16), b.astype(BF16), preferred_element_type=F32)


def _dot_nt(a, b):
    return lax.dot_general(a.astype(BF16), b.astype(BF16), (((1,), (1,)), ((), ())),
                           preferred_element_type=F32)


def _split3(a):
    hi = a.astype(BF16)
    r1 = a - hi.astype(F32)
    mid = r1.astype(BF16)
    lo = (r1 - mid.astype(F32)).astype(BF16)
    return hi, mid, lo


def _dot3_left(b_exact, a):
    hi, mid, lo = _split3(a)
    b = b_exact.astype(BF16)
    return (jnp.dot(b, hi, preferred_element_type=F32) + jnp.dot(b, mid, preferred_element_type=F32)
            + jnp.dot(b, lo, preferred_element_type=F32))


def _dot_hp(a, b):
    a0, a1, _ = _split3(a)
    b0, b1, _ = _split3(b)
    d = lambda x, y: jnp.dot(x, y, preferred_element_type=F32)
    return d(a0, b0) + (d(a0, b1) + d(a1, b0))


def _iota(shape, axis):
    return lax.broadcasted_iota(jnp.int32, shape, axis)


def _head_ones():
    r = _iota((BRANCH_W, BRANCH_W), 0) // HEAD_W
    c = _iota((BRANCH_W, BRANCH_W), 1) // HEAD_W
    return (r == c).astype(BF16)


def _head_mask(l):
    r = _iota((HEADS * l, BRANCH_W), 0) // l
    c = _iota((HEADS * l, BRANCH_W), 1) // HEAD_W
    return r == c


def _expand(x, mask):
    return jnp.where(mask, jnp.concatenate([x] * HEADS, axis=0), 0.0)


def _tri_masks(l):
    i = _iota((l, HEADS * l), 0)
    j = _iota((l, HEADS * l), 1) % l
    return j < i, j <= i


def _same_head(l):
    n = HEADS * l
    return (_iota((n, n), 0) // l) == (_iota((n, n), 1) // l)


def _expand_sq(x, same):
    return jnp.where(same, jnp.concatenate([x] * HEADS, axis=0), 0.0)


def _unit_lower_inverse(a_strict, l):
    i = _iota((l, HEADS * l), 0)
    j = _iota((l, HEADS * l), 1) % l
    eye = (i == j).astype(F32)
    same = _same_head(l)
    p = [-a for a in a_strict]
    t = [eye + x for x in p]
    k = 2
    while k < l:
        p = [_dot(x, _expand_sq(x, same)) for x in p]
        t = [y + _dot(y, _expand_sq(x, same)) for y, x in zip(t, p)]
        k *= 2
    return t


def _dot2(a, b_exact):
    hi = a.astype(BF16)
    lo = (a - hi.astype(F32)).astype(BF16)
    b = b_exact.astype(BF16)
    return jnp.dot(hi, b, preferred_element_type=F32) + jnp.dot(lo, b, preferred_element_type=F32)


def _cumsum_rows(x, l, nb=1):
    n = nb * l
    i, j = _iota((n, n), 0), _iota((n, n), 1)
    tri = ((j <= i) & ((i // l) == (j // l))).astype(BF16)
    return _dot3_left(tri, x)


def _softplus(x):
    return jnp.maximum(x, 0.0) + jnp.log(1.0 + jnp.exp(-jnp.abs(x)))


def _silu(x):
    return x * jax.nn.sigmoid(x)


def _norm_cast_kernel(x_ref, nw_ref, o_ref):
    x = x_ref[...]
    ms = jnp.mean(x * x, axis=-1, keepdims=True)
    o_ref[...] = (x * lax.rsqrt(ms + NORM_EPS) * nw_ref[...]).astype(BF16)


def _norm_cast(x2, norm_w, tm):
    n = x2.shape[0]
    spec = pl.BlockSpec((tm, D_MODEL), lambda i: (i, 0))
    return pl.pallas_call(
        _norm_cast_kernel,
        grid=(n // tm,),
        in_specs=[spec, pl.BlockSpec((1, D_MODEL), lambda i: (0, 0))],
        out_specs=spec,
        out_shape=jax.ShapeDtypeStruct((n, D_MODEL), BF16),
        compiler_params=_cparams(("parallel",)),
    )(x2, norm_w.reshape(1, D_MODEL))


def _s5_kernel(hn_ref, wz_ref, h0_ref, wb_ref, wc_ref, a2k_ref, apow_ref, d_ref, wg_ref, bg_ref,
               y_ref, hl_ref, hs_ref, hb_ref, *, tc, bb):
    c = pl.program_id(1)

    @pl.when(c == 0)
    def _():
        hs_ref[...] = h0_ref[...]

    hn = jnp.concatenate([hn_ref[b] for b in range(bb)], axis=0)
    u = jnp.dot(hn, wz_ref[...], preferred_element_type=F32)
    x = _dot(u, wb_ref[...])
    xr, xi = x[:, :S5_W], x[:, S5_W:]
    ng = bb * tc // S5_ROWS
    xr = xr.reshape(ng, S5_ROWS, S5_W)
    xi = xi.reshape(ng, S5_ROWS, S5_W)
    row = _iota((S5_ROWS, 1), 0)
    k, d = 0, 1
    while d < S5_ROWS:
        m = row >= d
        ar = jnp.where(m, a2k_ref[k:k + 1, :S5_W], 0.0)
        ai = jnp.where(m, a2k_ref[k:k + 1, S5_W:], 0.0)
        sr = pltpu.roll(xr, d, axis=1)
        si = pltpu.roll(xi, d, axis=1)
        xr, xi = xr + (ar * sr - ai * si), xi + (ar * si + ai * sr)
        k, d = k + 1, d * 2
    hb_ref[:, :S5_W] = xr.reshape(bb * tc, S5_W)
    hb_ref[:, S5_W:] = xi.reshape(bb * tc, S5_W)
    pr, pi_ = apow_ref[:, :S5_W], apow_ref[:, S5_W:]

    def group(gi, carry):
        out = []
        for b, (cr, ci) in enumerate(carry):
            rows = pl.ds(pl.multiple_of(b * tc + gi * S5_ROWS, S5_ROWS), S5_ROWS)
            hr = hb_ref[rows, :S5_W] + (pr * cr - pi_ * ci)
            hi = hb_ref[rows, S5_W:] + (pr * ci + pi_ * cr)
            hb_ref[rows, :S5_W] = hr
            hb_ref[rows, S5_W:] = hi
            out.append((hr[S5_ROWS - 1:S5_ROWS], hi[S5_ROWS - 1:S5_ROWS]))
        return tuple(out)

    state = tuple((hs_ref[b, :, :S5_W], hs_ref[b, :, S5_W:]) for b in range(bb))
    state = lax.fori_loop(0, tc // S5_ROWS, group, state, unroll=2)
    for b, (cr, ci) in enumerate(state):
        hs_ref[b, :, :S5_W] = cr
        hs_ref[b, :, S5_W:] = ci
    y = _dot(hb_ref[:, :S5_W], wc_ref[:S5_W]) + _dot(hb_ref[:, S5_W:], wc_ref[S5_W:])
    y = jax.nn.gelu(y + d_ref[...] * u)
    y = (y * jax.nn.sigmoid(_dot(y, wg_ref[...]) + bg_ref[...])).astype(y_ref.dtype)
    for b in range(bb):
        y_ref[b] = y[b * tc:(b + 1) * tc]

    @pl.when(c == pl.num_programs(1) - 1)
    def _():
        hl_ref[...] = hs_ref[...]


def _s5_tables(lam_re, lam_im, log_dt, b_c, c_c, tc):
    dt = jnp.exp(log_dt)[:, None]
    mag = jnp.exp(lam_re * dt)
    ab_re, ab_im = mag * jnp.cos(lam_im * dt), mag * jnp.sin(lam_im * dt)
    den = lam_re * lam_re + lam_im * lam_im
    nr = ab_re - 1.0
    cf_re = (nr * lam_re + ab_im * lam_im) / den
    cf_im = (ab_im * lam_re - nr * lam_im) / den
    br, bi = b_c[..., 0], b_c[..., 1]
    bb_re = cf_re[..., None] * br - cf_im[..., None] * bi
    bb_im = cf_re[..., None] * bi + cf_im[..., None] * br
    eye = jnp.eye(S5_GROUPS, dtype=F32)
    bd_in = lambda m: jnp.einsum('gph,gk->ghkp', m, eye).reshape(BRANCH_W, S5_W)
    wb = jnp.concatenate([bd_in(bb_re), bd_in(bb_im)], axis=1)
    cr, ci = c_c[..., 0], c_c[..., 1]
    bd_out = lambda m: jnp.einsum('ghp,gk->gpkh', m, eye).reshape(S5_W, BRANCH_W)
    wc = jnp.concatenate([bd_out(cr), -bd_out(ci)], axis=0)
    pr, pi_ = ab_re.reshape(1, S5_W), ab_im.reshape(1, S5_W)
    lv_r, lv_i = [], []
    tr, ti = pr, pi_
    d = 1
    while d < tc:
        lv_r.append(pr)
        lv_i.append(pi_)
        tr, ti = (jnp.concatenate([tr, tr * pr - ti * pi_], axis=0),
                  jnp.concatenate([ti, tr * pi_ + ti * pr], axis=0))
        pr, pi_ = pr * pr - pi_ * pi_, 2.0 * pr * pi_
        d *= 2
    n_lv = len(lv_r)
    pad = (-n_lv) % 8
    a2k = jnp.concatenate([jnp.concatenate(lv_r, axis=0), jnp.concatenate(lv_i, axis=0)], axis=1)
    a2k = jnp.pad(a2k, ((0, pad), (0, 0)))
    apow = jnp.concatenate([tr, ti], axis=1)
    return wb.astype(BF16), wc.astype(BF16), a2k, apow


def _w_cols(col, width):
    return pl.BlockSpec((D_MODEL, width), lambda b, c: (0, col // width))


def _s5_mixer(hn3, w_mix, h0, lam_re, lam_im, log_dt, b_c, c_c, d_skip, w_glu, b_glu, tc):
    bsz, t, _ = hn3.shape
    wb, wc, a2k, apow = _s5_tables(lam_re, lam_im, log_dt, b_c, c_c, S5_ROWS)
    h0f = jnp.concatenate([h0[..., 0].reshape(bsz, 1, S5_W), h0[..., 1].reshape(bsz, 1, S5_W)], axis=-1)
    full = lambda a: pl.BlockSpec(a.shape, lambda b, c: (0,) * a.ndim)
    d2, bg2, wg = d_skip.reshape(1, BRANCH_W), b_glu.reshape(1, BRANCH_W), w_glu.astype(BF16)
    bb = min(bsz, STACK_STREAMS)
    while bsz % bb:
        bb -= 1
    y, hl = pl.pallas_call(
        functools.partial(_s5_kernel, tc=tc, bb=bb),
        grid=(bsz // bb, t // tc),
        in_specs=[pl.BlockSpec((bb, tc, D_MODEL), lambda b, c: (b, c, 0)), _w_cols(COL_S5, BRANCH_W),
                  pl.BlockSpec((bb, 1, 2 * S5_W), lambda b, c: (b, 0, 0)),
                  full(wb), full(wc), full(a2k), full(apow), full(d2), full(wg), full(bg2)],
        out_specs=[pl.BlockSpec((bb, tc, BRANCH_W), lambda b, c: (b, c, 0)),
                   pl.BlockSpec((bb, 1, 2 * S5_W), lambda b, c: (b, 0, 0))],
        out_shape=[jax.ShapeDtypeStruct((bsz, t, BRANCH_W), BF16),
                   jax.ShapeDtypeStruct((bsz, 1, 2 * S5_W), F32)],
        scratch_shapes=[pltpu.VMEM((bb, 1, 2 * S5_W), F32), pltpu.VMEM((bb * tc, 2 * S5_W), F32)],
        compiler_params=_cparams(("parallel", "arbitrary")),
    )(hn3, w_mix, h0f, wb, wc, a2k, apow, d2, wg, bg2)
    h_last = jnp.stack([hl[:, 0, :S5_W].reshape(bsz, S5_GROUPS, S5_STATE),
                        hl[:, 0, S5_W:].reshape(bsz, S5_GROUPS, S5_STATE)], axis=-1)
    return y, h_last


def _rwkv_kernel(hn_ref, wz_ref, sh0_ref, s0_ref, mu_ref, w0_ref, w2_ref, a0_ref, a2_ref, g2_ref,
                 kk_ref, ka_ref, rk_ref, lnw_ref, lnb_ref,
                 y_ref, sl_ref, sho_ref, st_ref, zp_ref, *, l, bb, nt):
    c = pl.program_id(1)

    @pl.when(c == 0)
    def _():
        st_ref[...] = s0_ref[...]
        zp_ref[...] = sh0_ref[...]

    tl = nt * l
    z_all = jnp.dot(jnp.concatenate([hn_ref[b] for b in range(bb)], axis=0), wz_ref[...],
                    preferred_element_type=F32)
    row = _iota((tl, 1), 0)
    zms = []
    for b in range(bb):
        z = z_all[b * tl:(b + 1) * tl]
        prev = jnp.where(row == 0, zp_ref[b], pltpu.roll(z, 1, axis=0))
        zp_ref[b] = z[tl - 1:tl]
        zms.append(z + (prev - z) * mu_ref[...])
    zm = jnp.concatenate(zms, axis=0)
    r, k, v = zm[:, 0:256], zm[:, 256:512], zm[:, 512:768]
    lo = zm[:, 768:896]
    g_lo = zm[:, 896:1024]
    w_log = -_softplus(-(w0_ref[...] + _dot(jnp.tanh(lo), w2_ref[...]))) - 0.5
    lw = -jnp.exp(w_log)
    a = jax.nn.sigmoid(a0_ref[...] + _dot(lo, a2_ref[...]))
    g = _dot(jax.nn.sigmoid(g_lo), g2_ref[...])
    ones_h = _head_ones()
    kk = k * kk_ref[...]
    kk = kk * lax.rsqrt(_dot2(kk * kk, ones_h) + NORM_EPS)
    k = k * (1.0 + (a - 1.0) * ka_ref[...])
    kka = kk * a

    cum = _cumsum_rows(lw, l, bb * nt)
    p_incl = jnp.exp(cum)
    p_inv = jnp.exp(-cum)
    kt = kk * jnp.exp(cum - lw)
    rt = r * p_incl
    kh = k * p_inv
    ah = kka * p_inv

    hm = _head_mask(l)
    strict, incl = _tri_masks(l)
    chunks = [(b, j) for j in range(nt) for b in range(bb)]
    rows = {s: slice((s[0] * nt + s[1]) * l, (s[0] * nt + s[1] + 1) * l) for s in chunks}
    ex = lambda x: {s: _expand(x[rows[s]], hm) for s in chunks}
    kh_e, ah_e, v_e = ex(kh), ex(ah), ex(v)
    a_aa = [jnp.where(strict, _dot_nt(kt[rows[s]], ah_e[s]), 0.0) for s in chunks]
    a_ak = {s: jnp.where(strict, _dot_nt(kt[rows[s]], kh_e[s]), 0.0) for s in chunks}
    b_ra = {s: jnp.where(incl, _dot_nt(rt[rows[s]], ah_e[s]), 0.0) for s in chunks}
    b_rk = {s: jnp.where(incl, _dot_nt(rt[rows[s]], kh_e[s]), 0.0) for s in chunks}
    t_inv = dict(zip(chunks, _unit_lower_inverse(a_aa, l)))
    av = {s: _dot(a_ak[s], v_e[s]) for s in chunks}
    bv = {s: _dot(b_rk[s], v_e[s]) for s in chunks}
    to_end = {s: jnp.exp(cum[rows[s]][l - 1:l] - cum[rows[s]]) for s in chunks}
    head_blk = _same_head(HEAD_W)
    st = [st_ref[b] for b in range(bb)]
    yb = {}
    for j in range(nt):
        now = [(b, j) for b in range(bb)]
        rhs = {s: _dot_nt(kt[rows[s]], st[s[0]]) + av[s] for s in now}
        ys = {s: _dot_nt(rt[rows[s]], st[s[0]]) + bv[s] for s in now}
        u = {s: _dot(t_inv[s], _expand(rhs[s], hm)) for s in now}
        for s in now:
            yb[s] = ys[s] - _dot(b_ra[s], _expand(u[s], hm))
        lhs_t = {s: jnp.concatenate([v[rows[s]], -u[s]], axis=0).T for s in now}
        rhs_k = {s: jnp.concatenate([k[rows[s]] * to_end[s], kka[rows[s]] * to_end[s]], axis=0) for s in now}
        for s in now:
            p_last = p_incl[rows[s]][l - 1:l]
            st[s[0]] = st[s[0]] * p_last + jnp.where(head_blk, _dot(lhs_t[s], rhs_k[s]), 0.0)
    for b in range(bb):
        st_ref[b] = st[b]
    y = jnp.concatenate([yb[(b, j)] for b in range(bb) for j in range(nt)], axis=0)

    inv_w = 1.0 / HEAD_W
    mean = _dot2(y, ones_h) * inv_w
    yc = y - mean
    var = _dot2(yc * yc, ones_h) * inv_w
    y = yc * lax.rsqrt(var + RW_EPS) * lnw_ref[...] + lnb_ref[...]
    bonus = _dot2(r * k * rk_ref[...], ones_h) * v
    y = ((y + bonus) * g).astype(y_ref.dtype)
    for b in range(bb):
        y_ref[b] = y[b * tl:(b + 1) * tl]

    @pl.when(c == pl.num_programs(1) - 1)
    def _():
        sl_ref[...] = st_ref[...]
        sho_ref[...] = zp_ref[...]


MIX_CHUNK = 64
STACK_STREAMS = 4
STACK_ROWS = 512


def _streams_per_step(bsz, t, l):
    bb = min(bsz, STACK_STREAMS)
    while bsz % bb:
        bb -= 1
    nt = max(1, min(t, STACK_ROWS // bb) // l)
    while (t // l) % nt:
        nt -= 1
    return bb, nt


def _block_diag_heads(s):
    bsz = s.shape[0]
    eye = jnp.eye(HEADS, dtype=s.dtype)
    return jnp.einsum('bhij,hg->bhigj', s, eye).reshape(bsz, BRANCH_W, BRANCH_W)


def _diag_blocks(s):
    bsz = s.shape[0]
    s5 = s.reshape(bsz, HEADS, HEAD_W, HEADS, HEAD_W)
    return jnp.stack([s5[:, h, :, h, :] for h in range(HEADS)], axis=1)


def _rwkv_mixer(hn3, w_mix, shift0, s0, mu, w0, w2, a0, a2, g2, k_k, k_a, r_k, ln_w, ln_b, l):
    bsz, t, _ = hn3.shape
    row = lambda a: a.reshape(1, -1)
    w2p = jnp.concatenate([w2, jnp.zeros_like(w2)], axis=0).astype(BF16)
    a2p = jnp.concatenate([jnp.zeros_like(a2), a2], axis=0).astype(BF16)
    args = (shift0.reshape(bsz, 1, RW_COLS), _block_diag_heads(s0), row(mu), row(w0), w2p, row(a0), a2p,
            g2.astype(BF16), row(k_k), row(k_a), row(r_k), row(ln_w), row(ln_b))
    full = lambda a: pl.BlockSpec(a.shape, lambda b, c: (0,) * a.ndim)
    bb, nt = _streams_per_step(bsz, t, l)
    y, sl, sho = pl.pallas_call(
        functools.partial(_rwkv_kernel, l=l, bb=bb, nt=nt),
        grid=(bsz // bb, t // (nt * l)),
        in_specs=[pl.BlockSpec((bb, nt * l, D_MODEL), lambda b, c: (b, c, 0)), _w_cols(COL_RW, RW_COLS),
                  pl.BlockSpec((bb, 1, RW_COLS), lambda b, c: (b, 0, 0)),
                  pl.BlockSpec((bb, BRANCH_W, BRANCH_W), lambda b, c: (b, 0, 0))]
                 + [full(a) for a in args[2:]],
        out_specs=[pl.BlockSpec((bb, nt * l, BRANCH_W), lambda b, c: (b, c, 0)),
                   pl.BlockSpec((bb, BRANCH_W, BRANCH_W), lambda b, c: (b, 0, 0)),
                   pl.BlockSpec((bb, 1, RW_COLS), lambda b, c: (b, 0, 0))],
        out_shape=[jax.ShapeDtypeStruct((bsz, t, BRANCH_W), BF16),
                   jax.ShapeDtypeStruct((bsz, BRANCH_W, BRANCH_W), F32),
                   jax.ShapeDtypeStruct((bsz, 1, RW_COLS), F32)],
        scratch_shapes=[pltpu.VMEM((bb, BRANCH_W, BRANCH_W), F32), pltpu.VMEM((bb, 1, RW_COLS), F32)],
        compiler_params=_cparams(("parallel", "arbitrary")),
    )(hn3, w_mix, *args)
    return y, _diag_blocks(sl), sho[:, 0]


def _sgu_kernel(hn_ref, wz_ref, lnw_ref, lnb_ref, wm_ref, bias_ref, o_ref, v_ref, *, l, nc):
    z = jnp.dot(hn_ref[...], wz_ref[...], preferred_element_type=F32)
    zg = jax.nn.gelu(z)
    u, v = zg[:, :BRANCH_W], zg[:, BRANCH_W:]
    mean = jnp.mean(v, axis=-1, keepdims=True)
    vc = v - mean
    var = jnp.mean(vc * vc, axis=-1, keepdims=True)
    v = vc * lax.rsqrt(var + NORM_EPS) * lnw_ref[...] + lnb_ref[...]
    v_ref[...] = v
    hm = _head_mask(l)
    wm = wm_ref[...]
    for i in range(nc):
        rows = slice(i * l, (i + 1) * l)
        mixed = bias_ref[...] + _dot(wm, _expand(v[rows], hm))
        o_ref[rows, :] = (u[rows] * mixed).astype(o_ref.dtype)


def _sgu_mixer(hn3, w_mix, ln_w, ln_b, w_s, b_s):
    bsz, t, _ = hn3.shape
    l = min(SG_CHUNK, t)
    nc = max(1, min(SG_ROWS, t) // l)
    tril = jnp.tril(jnp.ones((l, l), F32))
    wm = jnp.transpose(w_s[:, :l, :l] * tril, (1, 0, 2)).reshape(l, HEADS * l).astype(BF16)
    bias = jnp.repeat(jnp.transpose(b_s[:, :l]), HEAD_W, axis=1)
    row = lambda a: a.reshape(1, -1)
    full = lambda a: pl.BlockSpec(a.shape, lambda b, c: (0,) * a.ndim)
    args = (row(ln_w), row(ln_b), wm, bias)
    return pl.pallas_call(
        functools.partial(_sgu_kernel, l=l, nc=nc),
        grid=(bsz, t // (nc * l)),
        in_specs=[pl.BlockSpec((None, nc * l, D_MODEL), lambda b, c: (b, c, 0)), _w_cols(COL_SG, 2 * BRANCH_W)]
                 + [full(a) for a in args],
        out_specs=[pl.BlockSpec((None, nc * l, BRANCH_W), lambda b, c: (b, c, 0)),
                   pl.BlockSpec((None, nc * l, BRANCH_W), lambda b, c: (b, c, 0))],
        out_shape=[jax.ShapeDtypeStruct((bsz, t, BRANCH_W), BF16),
                   jax.ShapeDtypeStruct((bsz, t, BRANCH_W), F32)],
        compiler_params=_cparams(("parallel", "parallel")),
    )(hn3, w_mix, *args)


def _gdn_kernel(hn_ref, wz_ref, wab_ref, cv0_ref, s0_ref, cw_ref, alog_ref, dtb_ref, nw_ref,
                y_ref, sl_ref, cvo_ref, st_ref, cv_ref, *, l, bb, nt):
    c = pl.program_id(1)

    @pl.when(c == 0)
    def _():
        st_ref[...] = s0_ref[...]
        cv_ref[...] = cv0_ref[...]

    tl = nt * l
    hn = jnp.concatenate([hn_ref[b] for b in range(bb)], axis=0)
    z_all = jnp.dot(hn, wz_ref[...], preferred_element_type=F32)
    ab = jnp.dot(hn, wab_ref[...], preferred_element_type=F32)
    row8 = _iota((8, 1), 0)
    convs, gates = [], []
    for b in range(bb):
        z = z_all[b * tl:(b + 1) * tl]
        qkv = z[:, :GD_QKV]
        gates.append(z[:, GD_QKV:])
        carry = cv_ref[b]
        cv_ref[b] = qkv[tl - 8:tl]
        conv = qkv * cw_ref[GD_CONV - 1:GD_CONV]
        for j in range(1, GD_CONV):
            sh = pltpu.roll(qkv, j, axis=0)
            top = jnp.where(row8 < j, pltpu.roll(carry, j, axis=0), sh[:8])
            sh = jnp.concatenate([top, sh[8:]], axis=0) if tl > 8 else top
            conv = conv + sh * cw_ref[GD_CONV - 1 - j:GD_CONV - j]
        convs.append(conv)
    conv = _silu(jnp.concatenate(convs, axis=0))
    gate = jnp.concatenate(gates, axis=0)
    q, k, v = conv[:, :256], conv[:, 256:512], conv[:, 512:768]
    ones_h = _head_ones()
    q = q * lax.rsqrt(_dot2(q * q, ones_h) + NORM_EPS) * (HEAD_W ** -0.5)
    k = k * lax.rsqrt(_dot2(k * k, ones_h) + NORM_EPS)
    lane_h = _iota((bb * tl, BRANCH_W), 1) // HEAD_W
    a_in = jnp.zeros((bb * tl, BRANCH_W), F32)
    b_in = jnp.zeros((bb * tl, BRANCH_W), F32)
    for h in range(HEADS):
        a_in = jnp.where(lane_h == h, ab[:, h:h + 1], a_in)
        b_in = jnp.where(lane_h == h, ab[:, HEADS + h:HEADS + h + 1], b_in)
    beta = jax.nn.sigmoid(b_in)
    g = -jnp.exp(alog_ref[...]) * _softplus(a_in + dtb_ref[...])
    gc = _cumsum_rows(g, l, bb * nt)
    eg = jnp.exp(gc)
    kb = k * beta
    vb = v * beta
    kbg = kb * eg
    qg = q * eg

    hm = _head_mask(l)
    strict, incl = _tri_masks(l)
    n = HEADS * l
    chunks = [(b, j) for j in range(nt) for b in range(bb)]
    rows = {s: slice((s[0] * nt + s[1]) * l, (s[0] * nt + s[1] + 1) * l) for s in chunks}
    ex = lambda x: {s: _expand(x[rows[s]], hm) for s in chunks}
    k_e, vb_e, kbg_e = ex(k), ex(vb), ex(kbg)
    lane_hd = _iota((l, n), 1) // l
    eye = _iota((l, n), 0) == (_iota((l, n), 1) % l)
    decay = {}
    for s in chunks:
        gi = jnp.zeros((l, n), F32)
        for h in range(HEADS):
            gi = jnp.where(lane_hd == h, gc[rows[s]][:, h * HEAD_W:h * HEAD_W + 1], gi)
        gj = jnp.sum(jnp.where(eye, gi, 0.0), axis=0, keepdims=True)
        decay[s] = jnp.where(incl, jnp.exp(jnp.where(incl, gi - gj, 0.0)), 0.0)
    lm = [jnp.where(strict, _dot_nt(kb[rows[s]], k_e[s]) * decay[s], 0.0) for s in chunks]
    qk = {s: _dot_nt(q[rows[s]], k_e[s]) * decay[s] for s in chunks}
    t_inv = dict(zip(chunks, _unit_lower_inverse(lm, l)))
    uc = {s: _dot(t_inv[s], vb_e[s]) for s in chunks}
    wc = {s: _dot(t_inv[s], kbg_e[s]) for s in chunks}
    g_last = {s: gc[rows[s]][l - 1:l] for s in chunks}
    k_dec = {s: (k[rows[s]] * jnp.exp(g_last[s] - gc[rows[s]])).T for s in chunks}
    head_blk = _same_head(HEAD_W)
    st = [st_ref[b] for b in range(bb)]
    ob = {}
    for j in range(nt):
        now = [(b, j) for b in range(bb)]
        o0 = {s: _dot(qg[rows[s]], st[s[0]]) for s in now}
        v_new = {s: uc[s] - _dot(wc[s], st[s[0]]) for s in now}
        for s in now:
            ob[s] = o0[s] + _dot(qk[s], _expand(v_new[s], hm))
        for s in now:
            st[s[0]] = st[s[0]] * jnp.exp(g_last[s]) + jnp.where(head_blk, _dot(k_dec[s], v_new[s]), 0.0)
    for b in range(bb):
        st_ref[b] = st[b]
    o = jnp.concatenate([ob[(b, j)] for b in range(bb) for j in range(nt)], axis=0)
    ms = _dot2(o * o, ones_h) * (1.0 / HEAD_W)
    o = (o * lax.rsqrt(ms + NORM_EPS) * nw_ref[...] * _silu(gate)).astype(y_ref.dtype)
    for b in range(bb):
        y_ref[b] = o[b * tl:(b + 1) * tl]

    @pl.when(c == pl.num_programs(1) - 1)
    def _():
        sl_ref[...] = st_ref[...]
        cvo_ref[...] = cv_ref[...]


def _gdn_mixer(hn3, w_mix, conv0, s0, conv_w, a_log, dt_bias, norm_w, l):
    bsz, t, _ = hn3.shape
    cv0 = jnp.pad(conv0, ((0, 0), (8 - (GD_CONV - 1), 0), (0, 0)))
    cw = jnp.pad(conv_w, ((0, 8 - GD_CONV), (0, 0)))
    per_head = lambda a: jnp.repeat(a, HEAD_W).reshape(1, BRANCH_W)
    args = (cv0, _block_diag_heads(s0), cw, per_head(a_log), per_head(dt_bias),
            jnp.tile(norm_w, HEADS).reshape(1, BRANCH_W))
    full = lambda a: pl.BlockSpec(a.shape, lambda b, c: (0,) * a.ndim)
    bb, nt = _streams_per_step(bsz, t, l)
    y, sl, cvo = pl.pallas_call(
        functools.partial(_gdn_kernel, l=l, bb=bb, nt=nt),
        grid=(bsz // bb, t // (nt * l)),
        in_specs=[pl.BlockSpec((bb, nt * l, D_MODEL), lambda b, c: (b, c, 0)),
                  _w_cols(COL_GD, 1024), _w_cols(COL_AB, LANES),
                  pl.BlockSpec((bb, 8, GD_QKV), lambda b, c: (b, 0, 0)),
                  pl.BlockSpec((bb, BRANCH_W, BRANCH_W), lambda b, c: (b, 0, 0))]
                 + [full(a) for a in args[2:]],
        out_specs=[pl.BlockSpec((bb, nt * l, BRANCH_W), lambda b, c: (b, c, 0)),
                   pl.BlockSpec((bb, BRANCH_W, BRANCH_W), lambda b, c: (b, 0, 0)),
                   pl.BlockSpec((bb, 8, GD_QKV), lambda b, c: (b, 0, 0))],
        out_shape=[jax.ShapeDtypeStruct((bsz, t, BRANCH_W), BF16),
                   jax.ShapeDtypeStruct((bsz, BRANCH_W, BRANCH_W), F32),
                   jax.ShapeDtypeStruct((bsz, 8, GD_QKV), F32)],
        scratch_shapes=[pltpu.VMEM((bb, BRANCH_W, BRANCH_W), F32), pltpu.VMEM((bb, 8, GD_QKV), F32)],
        compiler_params=_cparams(("parallel", "arbitrary")),
    )(hn3, w_mix, w_mix, *args)
    return y, _diag_blocks(sl), cvo[:, 8 - (GD_CONV - 1):]


def _merge_kernel(ya_ref, yb_ref, yc_ref, yd_ref, x_ref, nw_ref, wg_ref, wbr_ref, wout_ref, o_ref):
    x = x_ref[...]
    ms = jnp.mean(x * x, axis=-1, keepdims=True)
    h = (x * lax.rsqrt(ms + NORM_EPS) * nw_ref[...]).astype(BF16)
    m = None
    for b, y_ref in enumerate((ya_ref, yb_ref, yc_ref, yd_ref)):
        gate = jnp.dot(h, wg_ref[:, b * D_MODEL:(b + 1) * D_MODEL], preferred_element_type=F32)
        br = jnp.dot(y_ref[...], wbr_ref[b], preferred_element_type=F32)
        term = (0.5 * jnp.tanh(0.5 * gate) + 0.5) * br
        m = term if m is None else m + term
    o_ref[...] = x + jnp.dot(m.astype(BF16), wout_ref[...], preferred_element_type=F32)


def _merge(ys, x2, norm_w, w_gate, w_branch, w_out, tm):
    n = x2.shape[0]
    yspec = pl.BlockSpec((tm, BRANCH_W), lambda i: (i, 0))
    once = pl.Buffered(1)
    return pl.pallas_call(
        _merge_kernel,
        grid=(n // tm,),
        in_specs=[yspec, yspec, yspec, yspec,
                  pl.BlockSpec((tm, D_MODEL), lambda i: (i, 0)),
                  pl.BlockSpec((1, D_MODEL), lambda i: (0, 0)),
                  pl.BlockSpec((D_MODEL, 4 * D_MODEL), lambda i: (0, 0), pipeline_mode=once),
                  pl.BlockSpec((4, BRANCH_W, D_MODEL), lambda i: (0, 0, 0), pipeline_mode=once),
                  pl.BlockSpec((D_MODEL, D_MODEL), lambda i: (0, 0), pipeline_mode=once)],
        out_specs=pl.BlockSpec((tm, D_MODEL), lambda i: (i, 0)),
        out_shape=jax.ShapeDtypeStruct((n, D_MODEL), F32),
        compiler_params=_cparams(("parallel",)),
    )(*[y.reshape(n, BRANCH_W) for y in ys], x2, norm_w.reshape(1, D_MODEL), w_gate,
      w_branch.astype(BF16), w_out.astype(BF16))


def _ffn_kernel(x_ref, nw_ref, w1_ref, w3_ref, w2_ref, nnw_ref, o_ref, hn_ref):
    x = x_ref[...]
    ms = jnp.mean(x * x, axis=-1, keepdims=True)
    h = (x * lax.rsqrt(ms + NORM_EPS) * nw_ref[...]).astype(BF16)
    a = _silu(jnp.dot(h, w1_ref[...], preferred_element_type=F32)) * jnp.dot(h, w3_ref[...], preferred_element_type=F32)
    y = x + jnp.dot(a.astype(BF16), w2_ref[...], preferred_element_type=F32)
    o_ref[...] = y
    ms = jnp.mean(y * y, axis=-1, keepdims=True)
    hn_ref[...] = (y * lax.rsqrt(ms + NORM_EPS) * nnw_ref[...]).astype(BF16)


def _ffn(x2, norm_w, w1, w3, w2, next_norm_w, tm):
    n = x2.shape[0]
    dff = w1.shape[1]
    once = pl.Buffered(1)
    spec = pl.BlockSpec((tm, D_MODEL), lambda i: (i, 0))
    vec = pl.BlockSpec((1, D_MODEL), lambda i: (0, 0))
    return pl.pallas_call(
        _ffn_kernel,
        grid=(n // tm,),
        in_specs=[spec, vec,
                  pl.BlockSpec((D_MODEL, dff), lambda i: (0, 0), pipeline_mode=once),
                  pl.BlockSpec((D_MODEL, dff), lambda i: (0, 0), pipeline_mode=once),
                  pl.BlockSpec((dff, D_MODEL), lambda i: (0, 0), pipeline_mode=once), vec],
        out_specs=[spec, spec],
        out_shape=[jax.ShapeDtypeStruct((n, D_MODEL), F32), jax.ShapeDtypeStruct((n, D_MODEL), BF16)],
        compiler_params=_cparams(("parallel",)),
    )(x2, norm_w.reshape(1, D_MODEL), w1.astype(BF16), w3.astype(BF16), w2.astype(BF16),
      next_norm_w.reshape(1, D_MODEL))


SC_CORES = 2
SC_SUBCORES = 16
SC_WINDOW = 128
SC_PIECE = 256
MOE_BLOCK = 512
MOE_FF = 1792
ROUTER_ROWS = 512
HALF = D_MODEL // 2


HIGH16 = -65536


def _pack_bf16_pairs(h):
    bits = lax.bitcast_convert_type(h.astype(BF16).astype(F32), jnp.int32)
    return lax.shift_right_logical(bits[:, :HALF], 16) | (bits[:, HALF:] & HIGH16)


def _unpack_bf16_pairs(pieces):
    lo = [lax.bitcast_convert_type(lax.shift_left(w, 16), F32) for w in pieces]
    hi = [lax.bitcast_convert_type(w & HIGH16, F32) for w in pieces]
    return jnp.concatenate(lo + hi, axis=1).astype(BF16)


def _route_kernel(x_ref, nw_ref, wr_ref, br_ref, hp_ref, tok_ref, cnt_ref, seen_ref, *, tm):
    @pl.when(pl.program_id(0) == 0)
    def _():
        seen_ref[...] = jnp.zeros_like(seen_ref)

    x = x_ref[...]
    ms = jnp.mean(x * x, axis=-1, keepdims=True)
    h = x * lax.rsqrt(ms + NORM_EPS) * nw_ref[...]
    hp = _pack_bf16_pairs(h)
    for q in range(HALF // SC_PIECE):
        hp_ref[q] = hp[:, q * SC_PIECE:(q + 1) * SC_PIECE]
    lane = _iota((tm, LANES), 1)
    logits = jnp.where(lane < N_EXPERTS, _dot_hp(h, wr_ref[...]) + br_ref[...], -jnp.inf)
    m1 = jnp.max(logits, axis=1, keepdims=True)
    i1 = jnp.min(jnp.where(logits == m1, lane, LANES), axis=1, keepdims=True)
    rest = jnp.where(lane == i1, -jnp.inf, logits)
    m2 = jnp.max(rest, axis=1, keepdims=True)
    i2 = jnp.min(jnp.where(rest == m2, lane, LANES), axis=1, keepdims=True)
    e2 = jnp.exp(m2 - m1)
    g1 = 1.0 / (1.0 + e2)
    g2 = e2 / (1.0 + e2)
    self32 = ((lane == i1) | (lane == i2)).astype(F32)
    seen = seen_ref[...]
    tri = (_iota((tm, tm), 1) < _iota((tm, tm), 0)).astype(BF16)
    rank = jnp.dot(tri, self32.astype(BF16), preferred_element_type=F32) + seen
    r1 = jnp.sum(jnp.where(lane == i1, rank, 0.0), axis=1, keepdims=True)
    r2 = jnp.sum(jnp.where(lane == i2, rank, 0.0), axis=1, keepdims=True)
    cols = (i1.astype(F32), i2.astype(F32), r1, r2, g1, g2)
    tok = jnp.zeros((tm, LANES), F32)
    for j, col in enumerate(cols):
        tok = jnp.where(lane == j, col, tok)
    tok_ref[...] = tok
    seen = seen + jnp.sum(self32, axis=0, keepdims=True)
    seen_ref[...] = seen
    cnt_ref[...] = jnp.broadcast_to(seen, (8, LANES))


def _route(x2, norm_w, w_router, b_router, tm):
    n = x2.shape[0]
    wr = jnp.pad(w_router, ((0, 0), (0, LANES - N_EXPERTS)))
    br = jnp.pad(b_router, (0, LANES - N_EXPERTS)).reshape(1, LANES)
    return pl.pallas_call(
        functools.partial(_route_kernel, tm=tm),
        grid=(n // tm,),
        in_specs=[pl.BlockSpec((tm, D_MODEL), lambda i: (i, 0)),
                  pl.BlockSpec((1, D_MODEL), lambda i: (0, 0)),
                  pl.BlockSpec((D_MODEL, LANES), lambda i: (0, 0)),
                  pl.BlockSpec((1, LANES), lambda i: (0, 0))],
        out_specs=[pl.BlockSpec((HALF // SC_PIECE, tm, SC_PIECE), lambda i: (0, i, 0)),
                   pl.BlockSpec((tm, LANES), lambda i: (i, 0)),
                   pl.BlockSpec((8, LANES), lambda i: (0, 0))],
        out_shape=[jax.ShapeDtypeStruct((HALF // SC_PIECE, n, SC_PIECE), jnp.int32),
                   jax.ShapeDtypeStruct((n, LANES), F32),
                   jax.ShapeDtypeStruct((8, LANES), F32)],
        scratch_shapes=[pltpu.VMEM((1, LANES), F32)],
        compiler_params=_cparams(("arbitrary",)),
    )(x2, norm_w.reshape(1, D_MODEL), wr, br)


def _sc_rows_multiple(d):
    return SC_CORES * SC_SUBCORES * SC_WINDOW * SC_PIECE // d


def _sc_gather_rows(table, idx):
    f, v, _ = table.shape
    b = idx.shape[0]
    assert b % _sc_rows_multiple(f * SC_PIECE) == 0
    idx_all = (idx[None, :] + (jnp.arange(f, dtype=jnp.int32) * v)[:, None]).reshape(-1)
    return _sc_gather_pieces(table.reshape(f * v, SC_PIECE), idx_all).reshape(f, b, SC_PIECE)


def _sc_gather_pieces(table, idx):
    bp = idx.shape[0]
    d = table.shape[1]
    window = SC_WINDOW
    idx2 = idx.reshape(1, bp)
    mesh = plsc.VectorSubcoreMesh(core_axis_name="core", subcore_axis_name="subcore")

    @functools.partial(pl.kernel, out_type=jax.ShapeDtypeStruct((bp, d), table.dtype), mesh=mesh)
    def gather(x_hbm, i_hbm, o_hbm):
        def body(i_vmem, o_vmem):
            pltpu.sync_copy(x_hbm.at[i_vmem.at[0]], o_vmem)

        pltpu.emit_pipeline(
            body,
            grid=(bp // window,),
            in_specs=[pl.BlockSpec((1, window), index_map=lambda i: (0, i))],
            out_specs=[pl.BlockSpec((window, d), index_map=lambda i: (i, 0))],
            core_axis_name=("core", "subcore"),
            dimension_semantics=(pltpu.PARALLEL,),
        )(i_hbm, o_hbm)

    return gather(table, idx2)


def _sc_scatter_rows(rows, pos, n_slots):
    f, n0, _ = rows.shape
    n = -(-n0 // _sc_rows_multiple(D_MODEL)) * _sc_rows_multiple(D_MODEL)
    if n != n0:
        rows = jnp.pad(rows, ((0, 0), (0, n - n0), (0, 0)))
        pos = jnp.pad(pos.reshape(2, n0), ((0, 0), (0, n - n0)), constant_values=n_slots - 1)
    nb = n // SC_WINDOW
    assert (f * 2 * nb) % (SC_CORES * SC_SUBCORES) == 0
    idx = (pos.reshape(1, 2 * n) + (jnp.arange(f, dtype=jnp.int32) * n_slots)[:, None]).reshape(1, f * 2 * n)
    mesh = plsc.VectorSubcoreMesh(core_axis_name="core", subcore_axis_name="subcore")

    @functools.partial(pl.kernel, out_type=jax.ShapeDtypeStruct((f * n_slots, SC_PIECE), rows.dtype), mesh=mesh)
    def scatter(x_hbm, i_hbm, o_hbm):
        def body(x_vmem, i_vmem):
            pltpu.sync_copy(x_vmem, o_hbm.at[i_vmem.at[0]])

        pltpu.emit_pipeline(
            body,
            grid=(f * 2 * nb,),
            in_specs=[pl.BlockSpec((SC_WINDOW, SC_PIECE), index_map=lambda i: ((i // (2 * nb)) * nb + i % nb, 0)),
                      pl.BlockSpec((1, SC_WINDOW), index_map=lambda i: (0, i))],
            out_specs=[],
            core_axis_name=("core", "subcore"),
            dimension_semantics=(pltpu.PARALLEL,),
        )(x_hbm, i_hbm)

    return scatter(rows.reshape(f * n, SC_PIECE), idx).reshape(f, n_slots, SC_PIECE)


def _experts_kernel(be_ref, nb_ref, nv_ref, xs_ref, w1_ref, w3_ref, w2_ref, o_ref):
    g, c = pl.program_id(0), pl.program_id(1)

    @pl.when(c == 0)
    def _():
        o_ref[...] = jnp.zeros_like(o_ref)

    @pl.when(g < nb_ref[0])
    def _():
        live = _iota((xs_ref.shape[1], 1), 0) < nv_ref[g]
        xb = _unpack_bf16_pairs([jnp.where(live, xs_ref[q], 0) for q in range(HALF // SC_PIECE)])
        a = (_silu(jnp.dot(xb, w1_ref[...], preferred_element_type=F32))
             * jnp.dot(xb, w3_ref[...], preferred_element_type=F32))
        y = jnp.dot(a.astype(BF16), w2_ref[...], preferred_element_type=F32)
        for q in range(D_MODEL // SC_PIECE):
            o_ref[q] += y[:, q * SC_PIECE:(q + 1) * SC_PIECE]


def _experts(xs, block_expert, n_blocks, block_rows, w1, w3, w2, blk, tf):
    n_slots = xs.shape[1]
    dff = w1.shape[2]
    grid_spec = pltpu.PrefetchScalarGridSpec(
        num_scalar_prefetch=3,
        grid=(n_slots // blk, dff // tf),
        in_specs=[pl.BlockSpec((HALF // SC_PIECE, blk, SC_PIECE), lambda g, c, be, nb, nv: (0, g, 0)),
                  pl.BlockSpec((None, D_MODEL, tf), lambda g, c, be, nb, nv: (be[g], 0, c)),
                  pl.BlockSpec((None, D_MODEL, tf), lambda g, c, be, nb, nv: (be[g], 0, c)),
                  pl.BlockSpec((None, tf, D_MODEL), lambda g, c, be, nb, nv: (be[g], c, 0))],
        out_specs=pl.BlockSpec((D_MODEL // SC_PIECE, blk, SC_PIECE), lambda g, c, be, nb, nv: (0, g, 0)),
    )
    return pl.pallas_call(
        _experts_kernel,
        grid_spec=grid_spec,
        out_shape=jax.ShapeDtypeStruct((D_MODEL // SC_PIECE, n_slots, SC_PIECE), F32),
        compiler_params=_cparams(("parallel", "arbitrary")),
    )(block_expert, n_blocks, block_rows, xs, w1, w3, w2)


def _combine_kernel(x_ref, y1_ref, y2_ref, tok_ref, nw_ref, o_ref):
    tok = tok_ref[...]
    rows = lambda y_ref: jnp.concatenate([y_ref[q] for q in range(D_MODEL // SC_PIECE)], axis=1)
    x = x_ref[...] + tok[:, 4:5] * rows(y1_ref) + tok[:, 5:6] * rows(y2_ref)
    ms = jnp.mean(x * x, axis=-1, keepdims=True)
    o_ref[...] = x * lax.rsqrt(ms + NORM_EPS) * nw_ref[...]


def _combine(x2, ys2, tok, norm_w, tm):
    n = x2.shape[0]
    spec = pl.BlockSpec((tm, D_MODEL), lambda i: (i, 0))
    yspec = lambda off: pl.BlockSpec((D_MODEL // SC_PIECE, tm, SC_PIECE), lambda i: (0, i + off, 0))
    return pl.pallas_call(
        _combine_kernel,
        grid=(n // tm,),
        in_specs=[spec, yspec(0), yspec(n // tm),
                  pl.BlockSpec((tm, LANES), lambda i: (i, 0)), pl.BlockSpec((1, D_MODEL), lambda i: (0, 0))],
        out_specs=spec,
        out_shape=jax.ShapeDtypeStruct((n, D_MODEL), F32),
        compiler_params=_cparams(("parallel",)),
    )(x2, ys2, ys2, tok, norm_w.reshape(1, D_MODEL))


def _moe_final(x2, norm_w, w_router, b_router, w1, w3, w2, final_norm_w):
    n = x2.shape[0]
    blk = min(MOE_BLOCK, n)
    hp, tok, cnt = _route(x2, norm_w, w_router, b_router, min(ROUTER_ROWS, n))
    counts = cnt[0, :N_EXPERTS].astype(jnp.int32)
    padded = (counts + blk - 1) // blk * blk
    ends = jnp.cumsum(padded)
    base = ends - padded
    round_up = lambda a, m: -(-a // m) * m
    n_slots = round_up((-(-2 * n // blk) + N_EXPERTS + 1) * blk, _sc_rows_multiple(HALF))
    n_blocks = n_slots // blk
    i12 = tok[:, 0:2].astype(jnp.int32)
    pos = (base[i12] + tok[:, 2:4].astype(jnp.int32)).T.reshape(-1)
    block_start = jnp.arange(n_blocks, dtype=jnp.int32) * blk
    block_expert = jnp.minimum(jnp.sum(block_start[:, None] >= ends[None, :], axis=1), N_EXPERTS - 1).astype(jnp.int32)
    block_rows = jnp.clip((base + counts)[block_expert] - block_start, 0, blk).astype(jnp.int32)
    xs = _sc_scatter_rows(hp, pos, n_slots)
    ys = _experts(xs, block_expert, (ends[-1:] // blk).astype(jnp.int32), block_rows,
                  w1.astype(BF16), w3.astype(BF16), w2.astype(BF16), blk, MOE_FF)
    pos = jnp.pad(pos, (0, round_up(2 * n, _sc_rows_multiple(D_MODEL)) - 2 * n))
    ys2 = _sc_gather_rows(ys, pos)
    return _combine(x2, ys2, tok, final_norm_w, min(1024, n))


def _permute_w_in(w):
    s5, rw, sg = w[:, 0:256], w[:, 256:1280], w[:, 1280:1792]
    gd, ab, gates = w[:, 1792:2816], w[:, 2816:2824], w[:, 2824:6920]
    pad = jnp.zeros((D_MODEL, PROJ_COLS - COL_AB - 8), w.dtype)
    return jnp.concatenate([rw, gd, sg, s5, ab, pad], axis=1).astype(BF16), gates.astype(BF16)


def _run_trunk(x, s5_h, rw_s, rw_shift, gd_s, gd_conv, p, w_in_perm):
    bsz, t, _ = x.shape
    n = bsz * t
    l = min(MIX_CHUNK, t)
    tc_s5 = min(256, t)
    new = ([], [], [], [], [], [])
    x2 = x.reshape(n, D_MODEL)
    for layer in range(2):
        g = lambda name: p[name][layer]
        w_mix, w_gate = w_in_perm[layer]
        if layer == 0:
            hn = _norm_cast(x2, g('norm1_w'), min(1024, n))
        hn3 = hn.reshape(bsz, t, D_MODEL)
        y_a, s5_new = _s5_mixer(hn3, w_mix, s5_h[layer], g('s5_lam_re'), g('s5_lam_im'), g('s5_log_dt'), g('s5_b'),
                                g('s5_c'), g('s5_d'), g('s5_w_glu'), g('s5_b_glu'), tc_s5)
        y_b, rw_new, shift_new = _rwkv_mixer(hn3, w_mix, rw_shift[layer], rw_s[layer], g('rw_mu'), g('rw_w0'),
                                             g('rw_w2'), g('rw_a0'), g('rw_a2'), g('rw_g2'), g('rw_k_k'),
                                             g('rw_k_a'), g('rw_r_k'), g('rw_ln_w'), g('rw_ln_b'), l)
        y_c, sg_v = _sgu_mixer(hn3, w_mix, g('sg_ln_w'), g('sg_ln_b'), g('sg_w_s'), g('sg_b_s'))
        y_d, gd_new, conv_new = _gdn_mixer(hn3, w_mix, gd_conv[layer], gd_s[layer], g('gd_conv_w'), g('gd_a_log'),
                                           g('gd_dt_bias'), g('gd_norm_w'), l)
        x2 = _merge((y_a, y_b, y_c, y_d), x2, g('norm1_w'), w_gate, g('w_branch'), g('w_out'), min(512, n))
        j = layer // 2
        if layer % 2 == 0:
            x2, hn = _ffn(x2, g('norm2_w'), p['ffn_w1'][j], p['ffn_w3'][j], p['ffn_w2'][j],
                          p['norm1_w'][layer + 1], min(512, n))
        else:
            y = _moe_final(x2, g('norm2_w'), p['moe_router'][j], p['moe_router_b'][j],
                           p['moe_w1'][j], p['moe_w3'][j], p['moe_w2'][j], p['final_norm_w'])
        for lst, s in zip(new, (s5_new, rw_new, shift_new, gd_new, conv_new, sg_v)):
            lst.append(s)
    return y.reshape(bsz, t, D_MODEL), [jnp.stack(lst) for lst in new]


def kernel(x_prompt, x_sample, state_s5, state_rwkv, state_rwkv_shift, state_gdn, state_gdn_conv, norm1_w, w_in, s5_lam_re, s5_lam_im, s5_log_dt, s5_b, s5_c, s5_d, s5_w_glu, s5_b_glu, rw_mu, rw_w0, rw_w2, rw_a0, rw_a2, rw_g2, rw_k_k, rw_k_a, rw_r_k, rw_ln_w, rw_ln_b, sg_ln_w, sg_ln_b, sg_w_s, sg_b_s, gd_conv_w, gd_a_log, gd_dt_bias, gd_norm_w, w_branch, w_out, norm2_w, ffn_w1, ffn_w3, ffn_w2, moe_router, moe_router_b, moe_w1, moe_w3, moe_w2, final_norm_w):
    p = {
        'norm1_w': norm1_w, 's5_lam_re': s5_lam_re, 's5_lam_im': s5_lam_im, 's5_log_dt': s5_log_dt,
        's5_b': s5_b, 's5_c': s5_c, 's5_d': s5_d, 's5_w_glu': s5_w_glu, 's5_b_glu': s5_b_glu,
        'rw_mu': rw_mu, 'rw_w0': rw_w0, 'rw_w2': rw_w2, 'rw_a0': rw_a0, 'rw_a2': rw_a2, 'rw_g2': rw_g2,
        'rw_k_k': rw_k_k, 'rw_k_a': rw_k_a, 'rw_r_k': rw_r_k, 'rw_ln_w': rw_ln_w, 'rw_ln_b': rw_ln_b,
        'sg_ln_w': sg_ln_w, 'sg_ln_b': sg_ln_b, 'sg_w_s': sg_w_s, 'sg_b_s': sg_b_s,
        'gd_conv_w': gd_conv_w, 'gd_a_log': gd_a_log, 'gd_dt_bias': gd_dt_bias, 'gd_norm_w': gd_norm_w,
        'w_branch': w_branch, 'w_out': w_out, 'norm2_w': norm2_w,
        'ffn_w1': ffn_w1, 'ffn_w3': ffn_w3, 'ffn_w2': ffn_w2,
        'moe_router': moe_router, 'moe_router_b': moe_router_b, 'moe_w1': moe_w1, 'moe_w3': moe_w3, 'moe_w2': moe_w2,
        'final_norm_w': final_norm_w,
    }
    w_in_perm = [_permute_w_in(w_in[layer]) for layer in range(2)]
    bp, dt = x_prompt.shape[0], x_prompt.dtype
    depth = w_in.shape[0]
    y_prompt, (s5_p, rw_p, rwsh_p, gd_p, gdc_p, _) = _run_trunk(
        x_prompt,
        jnp.zeros((depth, bp, S5_GROUPS, S5_STATE, 2), dt),
        jnp.zeros((depth, bp, HEADS, HEAD_W, HEAD_W), dt),
        jnp.zeros((depth, bp, RW_COLS), dt),
        jnp.zeros((depth, bp, HEADS, HEAD_W, HEAD_W), dt),
        jnp.zeros((depth, bp, GD_CONV - 1, GD_QKV), dt),
        p, w_in_perm)
    y_sample, (s5_s, rw_s, rwsh_s, gd_s, gdc_s, sgv_s) = _run_trunk(
        x_sample, state_s5, state_rwkv, state_rwkv_shift, state_gdn, state_gdn_conv, p, w_in_perm)
    return (y_prompt, y_sample, s5_p, rw_p, rwsh_p, gd_p, gdc_p, s5_s, rw_s, rwsh_s, gd_s, gdc_s, sgv_s)
```

```python
import functools

import jax
import jax.numpy as jnp
from jax import lax
from jax.experimental import pallas as pl
from jax.experimental.pallas import tpu as pltpu
from jax.experimental.pallas import tpu_sc as plsc

F32 = jnp.float32
BF16 = jnp.bfloat16

D_MODEL = 1024
BRANCH_W = 256
HEADS = 4
HEAD_W = 64
S5_GROUPS = 16
S5_STATE = 64
S5_W = S5_GROUPS * S5_STATE
S5_ROWS = 8
SG_CHUNK = 128
SG_ROWS = 1024
GD_CONV = 4
GD_QKV = 3 * BRANCH_W
RW_COLS = 1024
RW_EPS = 64e-5
NORM_EPS = 1e-6
N_EXPERTS = 8
LANES = 128

COL_RW = 0
COL_GD = 1024
COL_SG = 2048
COL_S5 = 2560
COL_AB = 2816
PROJ_COLS = 3072

VMEM_LIMIT = 48 * 1024 * 1024


def _cparams(sem):
    return pltpu.CompilerParams(dimension_semantics=sem, vmem_limit_bytes=VMEM_LIMIT)


def _dot(a, b):
    return jnp.dot(a.astype(BF16), b.astype(BF16), preferred_element_type=F32)


def _dot_nt(a, b):
    return lax.dot_general(a.astype(BF16), b.astype(BF16), (((1,), (1,)), ((), ())),
                           preferred_element_type=F32)


def _split3(a):
    hi = a.astype(BF16)
    r1 = a - hi.astype(F32)
    mid = r1.astype(BF16)
    lo = (r1 - mid.astype(F32)).astype(BF16)
    return hi, mid, lo


def _dot3_left(b_exact, a):
    hi, mid, lo = _split3(a)
    b = b_exact.astype(BF16)
    return (jnp.dot(b, hi, preferred_element_type=F32) + jnp.dot(b, mid, preferred_element_type=F32)
            + jnp.dot(b, lo, preferred_element_type=F32))


def _dot_hp(a, b):
    a0, a1, _ = _split3(a)
    b0, b1, _ = _split3(b)
    d = lambda x, y: jnp.dot(x, y, preferred_element_type=F32)
    return d(a0, b0) + (d(a0, b1) + d(a1, b0))


def _iota(shape, axis):
    return lax.broadcasted_iota(jnp.int32, shape, axis)


def _head_ones():
    r = _iota((BRANCH_W, BRANCH_W), 0) // HEAD_W
    c = _iota((BRANCH_W, BRANCH_W), 1) // HEAD_W
    return (r == c).astype(BF16)


def _head_mask(l):
    r = _iota((HEADS * l, BRANCH_W), 0) // l
    c = _iota((HEADS * l, BRANCH_W), 1) // HEAD_W
    return r == c


def _expand(x, mask):
    return jnp.where(mask, jnp.concatenate([x] * HEADS, axis=0), 0.0)


def _tri_masks(l):
    i = _iota((l, HEADS * l), 0)
    j = _iota((l, HEADS * l), 1) % l
    return j < i, j <= i


def _same_head(l):
    n = HEADS * l
    return (_iota((n, n), 0) // l) == (_iota((n, n), 1) // l)


def _expand_sq(x, same):
    return jnp.where(same, jnp.concatenate([x] * HEADS, axis=0), 0.0)


def _unit_lower_inverse(a_strict, l):
    i = _iota((l, HEADS * l), 0)
    j = _iota((l, HEADS * l), 1) % l
    eye = (i == j).astype(F32)
    same = _same_head(l)
    p = [-a for a in a_strict]
    t = [eye + x for x in p]
    k = 2
    while k < l:
        p = [_dot(x, _expand_sq(x, same)) for x in p]
        t = [y + _dot(y, _expand_sq(x, same)) for y, x in zip(t, p)]
        k *= 2
    return t


def _dot2(a, b_exact):
    hi = a.astype(BF16)
    lo = (a - hi.astype(F32)).astype(BF16)
    b = b_exact.astype(BF16)
    return jnp.dot(hi, b, preferred_element_type=F32) + jnp.dot(lo, b, preferred_element_type=F32)


def _cumsum_rows(x, l, nb=1):
    n = nb * l
    i, j = _iota((n, n), 0), _iota((n, n), 1)
    tri = ((j <= i) & ((i // l) == (j // l))).astype(BF16)
    return _dot3_left(tri, x)


def _softplus(x):
    return jnp.maximum(x, 0.0) + jnp.log(1.0 + jnp.exp(-jnp.abs(x)))


def _silu(x):
    return x * jax.nn.sigmoid(x)


def _norm_cast_kernel(x_ref, nw_ref, o_ref):
    x = x_ref[...]
    ms = jnp.mean(x * x, axis=-1, keepdims=True)
    o_ref[...] = (x * lax.rsqrt(ms + NORM_EPS) * nw_ref[...]).astype(BF16)


def _norm_cast(x2, norm_w, tm):
    n = x2.shape[0]
    spec = pl.BlockSpec((tm, D_MODEL), lambda i: (i, 0))
    return pl.pallas_call(
        _norm_cast_kernel,
        grid=(n // tm,),
        in_specs=[spec, pl.BlockSpec((1, D_MODEL), lambda i: (0, 0))],
        out_specs=spec,
        out_shape=jax.ShapeDtypeStruct((n, D_MODEL), BF16),
        compiler_params=_cparams(("parallel",)),
    )(x2, norm_w.reshape(1, D_MODEL))


def _s5_kernel(hn_ref, wz_ref, h0_ref, wb_ref, wc_ref, a2k_ref, apow_ref, d_ref, wg_ref, bg_ref,
               y_ref, hl_ref, hs_ref, hb_ref, *, tc, bb):
    c = pl.program_id(1)

    @pl.when(c == 0)
    def _():
        hs_ref[...] = h0_ref[...]

    hn = jnp.concatenate([hn_ref[b] for b in range(bb)], axis=0)
    u = jnp.dot(hn, wz_ref[...], preferred_element_type=F32)
    x = _dot(u, wb_ref[...])
    xr, xi = x[:, :S5_W], x[:, S5_W:]
    ng = bb * tc // S5_ROWS
    xr = xr.reshape(ng, S5_ROWS, S5_W)
    xi = xi.reshape(ng, S5_ROWS, S5_W)
    row = _iota((S5_ROWS, 1), 0)
    k, d = 0, 1
    while d < S5_ROWS:
        m = row >= d
        ar = jnp.where(m, a2k_ref[k:k + 1, :S5_W], 0.0)
        ai = jnp.where(m, a2k_ref[k:k + 1, S5_W:], 0.0)
        sr = pltpu.roll(xr, d, axis=1)
        si = pltpu.roll(xi, d, axis=1)
        xr, xi = xr + (ar * sr - ai * si), xi + (ar * si + ai * sr)
        k, d = k + 1, d * 2
    hb_ref[:, :S5_W] = xr.reshape(bb * tc, S5_W)
    hb_ref[:, S5_W:] = xi.reshape(bb * tc, S5_W)
    pr, pi_ = apow_ref[:, :S5_W], apow_ref[:, S5_W:]

    def group(gi, carry):
        out = []
        for b, (cr, ci) in enumerate(carry):
            rows = pl.ds(pl.multiple_of(b * tc + gi * S5_ROWS, S5_ROWS), S5_ROWS)
            hr = hb_ref[rows, :S5_W] + (pr * cr - pi_ * ci)
            hi = hb_ref[rows, S5_W:] + (pr * ci + pi_ * cr)
            hb_ref[rows, :S5_W] = hr
            hb_ref[rows, S5_W:] = hi
            out.append((hr[S5_ROWS - 1:S5_ROWS], hi[S5_ROWS - 1:S5_ROWS]))
        return tuple(out)

    state = tuple((hs_ref[b, :, :S5_W], hs_ref[b, :, S5_W:]) for b in range(bb))
    state = lax.fori_loop(0, tc // S5_ROWS, group, state, unroll=2)
    for b, (cr, ci) in enumerate(state):
        hs_ref[b, :, :S5_W] = cr
        hs_ref[b, :, S5_W:] = ci
    y = _dot(hb_ref[:, :S5_W], wc_ref[:S5_W]) + _dot(hb_ref[:, S5_W:], wc_ref[S5_W:])
    y = jax.nn.gelu(y + d_ref[...] * u)
    y = (y * jax.nn.sigmoid(_dot(y, wg_ref[...]) + bg_ref[...])).astype(y_ref.dtype)
    for b in range(bb):
        y_ref[b] = y[b * tc:(b + 1) * tc]

    @pl.when(c == pl.num_programs(1) - 1)
    def _():
        hl_ref[...] = hs_ref[...]


def _s5_tables(lam_re, lam_im, log_dt, b_c, c_c, tc):
    dt = jnp.exp(log_dt)[:, None]
    mag = jnp.exp(lam_re * dt)
    ab_re, ab_im = mag * jnp.cos(lam_im * dt), mag * jnp.sin(lam_im * dt)
    den = lam_re * lam_re + lam_im * lam_im
    nr = ab_re - 1.0
    cf_re = (nr * lam_re + ab_im * lam_im) / den
    cf_im = (ab_im * lam_re - nr * lam_im) / den
    br, bi = b_c[..., 0], b_c[..., 1]
    bb_re = cf_re[..., None] * br - cf_im[..., None] * bi
    bb_im = cf_re[..., None] * bi + cf_im[..., None] * br
    eye = jnp.eye(S5_GROUPS, dtype=F32)
    bd_in = lambda m: jnp.einsum('gph,gk->ghkp', m, eye).reshape(BRANCH_W, S5_W)
    wb = jnp.concatenate([bd_in(bb_re), bd_in(bb_im)], axis=1)
    cr, ci = c_c[..., 0], c_c[..., 1]
    bd_out = lambda m: jnp.einsum('ghp,gk->gpkh', m, eye).reshape(S5_W, BRANCH_W)
    wc = jnp.concatenate([bd_out(cr), -bd_out(ci)], axis=0)
    pr, pi_ = ab_re.reshape(1, S5_W), ab_im.reshape(1, S5_W)
    lv_r, lv_i = [], []
    tr, ti = pr, pi_
    d = 1
    while d < tc:
        lv_r.append(pr)
        lv_i.append(pi_)
        tr, ti = (jnp.concatenate([tr, tr * pr - ti * pi_], axis=0),
                  jnp.concatenate([ti, tr * pi_ + ti * pr], axis=0))
        pr, pi_ = pr * pr - pi_ * pi_, 2.0 * pr * pi_
        d *= 2
    n_lv = len(lv_r)
    pad = (-n_lv) % 8
    a2k = jnp.concatenate([jnp.concatenate(lv_r, axis=0), jnp.concatenate(lv_i, axis=0)], axis=1)
    a2k = jnp.pad(a2k, ((0, pad), (0, 0)))
    apow = jnp.concatenate([tr, ti], axis=1)
    return wb.astype(BF16), wc.astype(BF16), a2k, apow


def _w_cols(col, width):
    return pl.BlockSpec((D_MODEL, width), lambda b, c: (0, col // width))


def _s5_mixer(hn3, w_mix, h0, lam_re, lam_im, log_dt, b_c, c_c, d_skip, w_glu, b_glu, tc):
    bsz, t, _ = hn3.shape
    wb, wc, a2k, apow = _s5_tables(lam_re, lam_im, log_dt, b_c, c_c, S5_ROWS)
    h0f = jnp.concatenate([h0[..., 0].reshape(bsz, 1, S5_W), h0[..., 1].reshape(bsz, 1, S5_W)], axis=-1)
    full = lambda a: pl.BlockSpec(a.shape, lambda b, c: (0,) * a.ndim)
    d2, bg2, wg = d_skip.reshape(1, BRANCH_W), b_glu.reshape(1, BRANCH_W), w_glu.astype(BF16)
    bb = min(bsz, STACK_STREAMS)
    while bsz % bb:
        bb -= 1
    y, hl = pl.pallas_call(
        functools.partial(_s5_kernel, tc=tc, bb=bb),
        grid=(bsz // bb, t // tc),
        in_specs=[pl.BlockSpec((bb, tc, D_MODEL), lambda b, c: (b, c, 0)), _w_cols(COL_S5, BRANCH_W),
                  pl.BlockSpec((bb, 1, 2 * S5_W), lambda b, c: (b, 0, 0)),
                  full(wb), full(wc), full(a2k), full(apow), full(d2), full(wg), full(bg2)],
        out_specs=[pl.BlockSpec((bb, tc, BRANCH_W), lambda b, c: (b, c, 0)),
                   pl.BlockSpec((bb, 1, 2 * S5_W), lambda b, c: (b, 0, 0))],
        out_shape=[jax.ShapeDtypeStruct((bsz, t, BRANCH_W), BF16),
                   jax.ShapeDtypeStruct((bsz, 1, 2 * S5_W), F32)],
        scratch_shapes=[pltpu.VMEM((bb, 1, 2 * S5_W), F32), pltpu.VMEM((bb * tc, 2 * S5_W), F32)],
        compiler_params=_cparams(("parallel", "arbitrary")),
    )(hn3, w_mix, h0f, wb, wc, a2k, apow, d2, wg, bg2)
    h_last = jnp.stack([hl[:, 0, :S5_W].reshape(bsz, S5_GROUPS, S5_STATE),
                        hl[:, 0, S5_W:].reshape(bsz, S5_GROUPS, S5_STATE)], axis=-1)
    return y, h_last


def _rwkv_kernel(hn_ref, wz_ref, sh0_ref, s0_ref, mu_ref, w0_ref, w2_ref, a0_ref, a2_ref, g2_ref,
                 kk_ref, ka_ref, rk_ref, lnw_ref, lnb_ref,
                 y_ref, sl_ref, sho_ref, st_ref, zp_ref, *, l, bb, nt):
    c = pl.program_id(1)

    @pl.when(c == 0)
    def _():
        st_ref[...] = s0_ref[...]
        zp_ref[...] = sh0_ref[...]

    tl = nt * l
    z_all = jnp.dot(jnp.concatenate([hn_ref[b] for b in range(bb)], axis=0), wz_ref[...],
                    preferred_element_type=F32)
    row = _iota((tl, 1), 0)
    zms = []
    for b in range(bb):
        z = z_all[b * tl:(b + 1) * tl]
        prev = jnp.where(row == 0, zp_ref[b], pltpu.roll(z, 1, axis=0))
        zp_ref[b] = z[tl - 1:tl]
        zms.append(z + (prev - z) * mu_ref[...])
    zm = jnp.concatenate(zms, axis=0)
    r, k, v = zm[:, 0:256], zm[:, 256:512], zm[:, 512:768]
    lo = zm[:, 768:896]
    g_lo = zm[:, 896:1024]
    w_log = -_softplus(-(w0_ref[...] + _dot(jnp.tanh(lo), w2_ref[...]))) - 0.5
    lw = -jnp.exp(w_log)
    a = jax.nn.sigmoid(a0_ref[...] + _dot(lo, a2_ref[...]))
    g = _dot(jax.nn.sigmoid(g_lo), g2_ref[...])
    ones_h = _head_ones()
    kk = k * kk_ref[...]
    kk = kk * lax.rsqrt(_dot2(kk * kk, ones_h) + NORM_EPS)
    k = k * (1.0 + (a - 1.0) * ka_ref[...])
    kka = kk * a

    cum = _cumsum_rows(lw, l, bb * nt)
    p_incl = jnp.exp(cum)
    p_inv = jnp.exp(-cum)
    kt = kk * jnp.exp(cum - lw)
    rt = r * p_incl
    kh = k * p_inv
    ah = kka * p_inv

    hm = _head_mask(l)
    strict, incl = _tri_masks(l)
    chunks = [(b, j) for j in range(nt) for b in range(bb)]
    rows = {s: slice((s[0] * nt + s[1]) * l, (s[0] * nt + s[1] + 1) * l) for s in chunks}
    ex = lambda x: {s: _expand(x[rows[s]], hm) for s in chunks}
    kh_e, ah_e, v_e = ex(kh), ex(ah), ex(v)
    a_aa = [jnp.where(strict, _dot_nt(kt[rows[s]], ah_e[s]), 0.0) for s in chunks]
    a_ak = {s: jnp.where(strict, _dot_nt(kt[rows[s]], kh_e[s]), 0.0) for s in chunks}
    b_ra = {s: jnp.where(incl, _dot_nt(rt[rows[s]], ah_e[s]), 0.0) for s in chunks}
    b_rk = {s: jnp.where(incl, _dot_nt(rt[rows[s]], kh_e[s]), 0.0) for s in chunks}
    t_inv = dict(zip(chunks, _unit_lower_inverse(a_aa, l)))
    av = {s: _dot(a_ak[s], v_e[s]) for s in chunks}
    bv = {s: _dot(b_rk[s], v_e[s]) for s in chunks}
    to_end = {s: jnp.exp(cum[rows[s]][l - 1:l] - cum[rows[s]]) for s in chunks}
    head_blk = _same_head(HEAD_W)
    st = [st_ref[b] for b in range(bb)]
    yb = {}
    for j in range(nt):
        now = [(b, j) for b in range(bb)]
        rhs = {s: _dot_nt(kt[rows[s]], st[s[0]]) + av[s] for s in now}
        ys = {s: _dot_nt(rt[rows[s]], st[s[0]]) + bv[s] for s in now}
        u = {s: _dot(t_inv[s], _expand(rhs[s], hm)) for s in now}
        for s in now:
            yb[s] = ys[s] - _dot(b_ra[s], _expand(u[s], hm))
        lhs_t = {s: jnp.concatenate([v[rows[s]], -u[s]], axis=0).T for s in now}
        rhs_k = {s: jnp.concatenate([k[rows[s]] * to_end[s], kka[rows[s]] * to_end[s]], axis=0) for s in now}
        for s in now:
            p_last = p_incl[rows[s]][l - 1:l]
            st[s[0]] = st[s[0]] * p_last + jnp.where(head_blk, _dot(lhs_t[s], rhs_k[s]), 0.0)
    for b in range(bb):
        st_ref[b] = st[b]
    y = jnp.concatenate([yb[(b, j)] for b in range(bb) for j in range(nt)], axis=0)

    inv_w = 1.0 / HEAD_W
    mean = _dot2(y, ones_h) * inv_w
    yc = y - mean
    var = _dot2(yc * yc, ones_h) * inv_w
    y = yc * lax.rsqrt(var + RW_EPS) * lnw_ref[...] + lnb_ref[...]
    bonus = _dot2(r * k * rk_ref[...], ones_h) * v
    y = ((y + bonus) * g).astype(y_ref.dtype)
    for b in range(bb):
        y_ref[b] = y[b * tl:(b + 1) * tl]

    @pl.when(c == pl.num_programs(1) - 1)
    def _():
        sl_ref[...] = st_ref[...]
        sho_ref[...] = zp_ref[...]


MIX_CHUNK = 64
STACK_STREAMS = 4
STACK_ROWS = 512


def _streams_per_step(bsz, t, l):
    bb = min(bsz, STACK_STREAMS)
    while bsz % bb:
        bb -= 1
    nt = max(1, min(t, STACK_ROWS // bb) // l)
    while (t // l) % nt:
        nt -= 1
    return bb, nt


def _block_diag_heads(s):
    bsz = s.shape[0]
    eye = jnp.eye(HEADS, dtype=s.dtype)
    return jnp.einsum('bhij,hg->bhigj', s, eye).reshape(bsz, BRANCH_W, BRANCH_W)


def _diag_blocks(s):
    bsz = s.shape[0]
    s5 = s.reshape(bsz, HEADS, HEAD_W, HEADS, HEAD_W)
    return jnp.stack([s5[:, h, :, h, :] for h in range(HEADS)], axis=1)


def _rwkv_mixer(hn3, w_mix, shift0, s0, mu, w0, w2, a0, a2, g2, k_k, k_a, r_k, ln_w, ln_b, l):
    bsz, t, _ = hn3.shape
    row = lambda a: a.reshape(1, -1)
    w2p = jnp.concatenate([w2, jnp.zeros_like(w2)], axis=0).astype(BF16)
    a2p = jnp.concatenate([jnp.zeros_like(a2), a2], axis=0).astype(BF16)
    args = (shift0.reshape(bsz, 1, RW_COLS), _block_diag_heads(s0), row(mu), row(w0), w2p, row(a0), a2p,
            g2.astype(BF16), row(k_k), row(k_a), row(r_k), row(ln_w), row(ln_b))
    full = lambda a: pl.BlockSpec(a.shape, lambda b, c: (0,) * a.ndim)
    bb, nt = _streams_per_step(bsz, t, l)
    y, sl, sho = pl.pallas_call(
        functools.partial(_rwkv_kernel, l=l, bb=bb, nt=nt),
        grid=(bsz // bb, t // (nt * l)),
        in_specs=[pl.BlockSpec((bb, nt * l, D_MODEL), lambda b, c: (b, c, 0)), _w_cols(COL_RW, RW_COLS),
                  pl.BlockSpec((bb, 1, RW_COLS), lambda b, c: (b, 0, 0)),
                  pl.BlockSpec((bb, BRANCH_W, BRANCH_W), lambda b, c: (b, 0, 0))]
                 + [full(a) for a in args[2:]],
        out_specs=[pl.BlockSpec((bb, nt * l, BRANCH_W), lambda b, c: (b, c, 0)),
                   pl.BlockSpec((bb, BRANCH_W, BRANCH_W), lambda b, c: (b, 0, 0)),
                   pl.BlockSpec((bb, 1, RW_COLS), lambda b, c: (b, 0, 0))],
        out_shape=[jax.ShapeDtypeStruct((bsz, t, BRANCH_W), BF16),
                   jax.ShapeDtypeStruct((bsz, BRANCH_W, BRANCH_W), F32),
                   jax.ShapeDtypeStruct((bsz, 1, RW_COLS), F32)],
        scratch_shapes=[pltpu.VMEM((bb, BRANCH_W, BRANCH_W), F32), pltpu.VMEM((bb, 1, RW_COLS), F32)],
        compiler_params=_cparams(("parallel", "arbitrary")),
    )(hn3, w_mix, *args)
    return y, _diag_blocks(sl), sho[:, 0]


def _sgu_kernel(hn_ref, wz_ref, lnw_ref, lnb_ref, wm_ref, bias_ref, o_ref, v_ref, *, l, nc):
    z = jnp.dot(hn_ref[...], wz_ref[...], preferred_element_type=F32)
    zg = jax.nn.gelu(z)
    u, v = zg[:, :BRANCH_W], zg[:, BRANCH_W:]
    mean = jnp.mean(v, axis=-1, keepdims=True)
    vc = v - mean
    var = jnp.mean(vc * vc, axis=-1, keepdims=True)
    v = vc * lax.rsqrt(var + NORM_EPS) * lnw_ref[...] + lnb_ref[...]
    v_ref[...] = v
    hm = _head_mask(l)
    wm = wm_ref[...]
    for i in range(nc):
        rows = slice(i * l, (i + 1) * l)
        mixed = bias_ref[...] + _dot(wm, _expand(v[rows], hm))
        o_ref[rows, :] = (u[rows] * mixed).astype(o_ref.dtype)


def _sgu_mixer(hn3, w_mix, ln_w, ln_b, w_s, b_s):
    bsz, t, _ = hn3.shape
    l = min(SG_CHUNK, t)
    nc = max(1, min(SG_ROWS, t) // l)
    tril = jnp.tril(jnp.ones((l, l), F32))
    wm = jnp.transpose(w_s[:, :l, :l] * tril, (1, 0, 2)).reshape(l, HEADS * l).astype(BF16)
    bias = jnp.repeat(jnp.transpose(b_s[:, :l]), HEAD_W, axis=1)
    row = lambda a: a.reshape(1, -1)
    full = lambda a: pl.BlockSpec(a.shape, lambda b, c: (0,) * a.ndim)
    args = (row(ln_w), row(ln_b), wm, bias)
    return pl.pallas_call(
        functools.partial(_sgu_kernel, l=l, nc=nc),
        grid=(bsz, t // (nc * l)),
        in_specs=[pl.BlockSpec((None, nc * l, D_MODEL), lambda b, c: (b, c, 0)), _w_cols(COL_SG, 2 * BRANCH_W)]
                 + [full(a) for a in args],
        out_specs=[pl.BlockSpec((None, nc * l, BRANCH_W), lambda b, c: (b, c, 0)),
                   pl.BlockSpec((None, nc * l, BRANCH_W), lambda b, c: (b, c, 0))],
        out_shape=[jax.ShapeDtypeStruct((bsz, t, BRANCH_W), BF16),
                   jax.ShapeDtypeStruct((bsz, t, BRANCH_W), F32)],
        compiler_params=_cparams(("parallel", "parallel")),
    )(hn3, w_mix, *args)


def _gdn_kernel(hn_ref, wz_ref, wab_ref, cv0_ref, s0_ref, cw_ref, alog_ref, dtb_ref, nw_ref,
                y_ref, sl_ref, cvo_ref, st_ref, cv_ref, *, l, bb, nt):
    c = pl.program_id(1)

    @pl.when(c == 0)
    def _():
        st_ref[...] = s0_ref[...]
        cv_ref[...] = cv0_ref[...]

    tl = nt * l
    hn = jnp.concatenate([hn_ref[b] for b in range(bb)], axis=0)
    z_all = jnp.dot(hn, wz_ref[...], preferred_element_type=F32)
    ab = jnp.dot(hn, wab_ref[...], preferred_element_type=F32)
    row8 = _iota((8, 1), 0)
    convs, gates = [], []
    for b in range(bb):
        z = z_all[b * tl:(b + 1) * tl]
        qkv = z[:, :GD_QKV]
        gates.append(z[:, GD_QKV:])
        carry = cv_ref[b]
        cv_ref[b] = qkv[tl - 8:tl]
        conv = qkv * cw_ref[GD_CONV - 1:GD_CONV]
        for j in range(1, GD_CONV):
            sh = pltpu.roll(qkv, j, axis=0)
            top = jnp.where(row8 < j, pltpu.roll(carry, j, axis=0), sh[:8])
            sh = jnp.concatenate([top, sh[8:]], axis=0) if tl > 8 else top
            conv = conv + sh * cw_ref[GD_CONV - 1 - j:GD_CONV - j]
        convs.append(conv)
    conv = _silu(jnp.concatenate(convs, axis=0))
    gate = jnp.concatenate(gates, axis=0)
    q, k, v = conv[:, :256], conv[:, 256:512], conv[:, 512:768]
    ones_h = _head_ones()
    q = q * lax.rsqrt(_dot2(q * q, ones_h) + NORM_EPS) * (HEAD_W ** -0.5)
    k = k * lax.rsqrt(_dot2(k * k, ones_h) + NORM_EPS)
    lane_h = _iota((bb * tl, BRANCH_W), 1) // HEAD_W
    a_in = jnp.zeros((bb * tl, BRANCH_W), F32)
    b_in = jnp.zeros((bb * tl, BRANCH_W), F32)
    for h in range(HEADS):
        a_in = jnp.where(lane_h == h, ab[:, h:h + 1], a_in)
        b_in = jnp.where(lane_h == h, ab[:, HEADS + h:HEADS + h + 1], b_in)
    beta = jax.nn.sigmoid(b_in)
    g = -jnp.exp(alog_ref[...]) * _softplus(a_in + dtb_ref[...])
    gc = _cumsum_rows(g, l, bb * nt)
    eg = jnp.exp(gc)
    kb = k * beta
    vb = v * beta
    kbg = kb * eg
    qg = q * eg

    hm = _head_mask(l)
    strict, incl = _tri_masks(l)
    n = HEADS * l
    chunks = [(b, j) for j in range(nt) for b in range(bb)]
    rows = {s: slice((s[0] * nt + s[1]) * l, (s[0] * nt + s[1] + 1) * l) for s in chunks}
    ex = lambda x: {s: _expand(x[rows[s]], hm) for s in chunks}
    k_e, vb_e, kbg_e = ex(k), ex(vb), ex(kbg)
    lane_hd = _iota((l, n), 1) // l
    eye = _iota((l, n), 0) == (_iota((l, n), 1) % l)
    decay = {}
    for s in chunks:
        gi = jnp.zeros((l, n), F32)
        for h in range(HEADS):
            gi = jnp.where(lane_hd == h, gc[rows[s]][:, h * HEAD_W:h * HEAD_W + 1], gi)
        gj = jnp.sum(jnp.where(eye, gi, 0.0), axis=0, keepdims=True)
        decay[s] = jnp.where(incl, jnp.exp(jnp.where(incl, gi - gj, 0.0)), 0.0)
    lm = [jnp.where(strict, _dot_nt(kb[rows[s]], k_e[s]) * decay[s], 0.0) for s in chunks]
    qk = {s: _dot_nt(q[rows[s]], k_e[s]) * decay[s] for s in chunks}
    t_inv = dict(zip(chunks, _unit_lower_inverse(lm, l)))
    uc = {s: _dot(t_inv[s], vb_e[s]) for s in chunks}
    wc = {s: _dot(t_inv[s], kbg_e[s]) for s in chunks}
    g_last = {s: gc[rows[s]][l - 1:l] for s in chunks}
    k_dec = {s: (k[rows[s]] * jnp.exp(g_last[s] - gc[rows[s]])).T for s in chunks}
    head_blk = _same_head(HEAD_W)
    st = [st_ref[b] for b in range(bb)]
    ob = {}
    for j in range(nt):
        now = [(b, j) for b in range(bb)]
        o0 = {s: _dot(qg[rows[s]], st[s[0]]) for s in now}
        v_new = {s: uc[s] - _dot(wc[s], st[s[0]]) for s in now}
        for s in now:
            ob[s] = o0[s] + _dot(qk[s], _expand(v_new[s], hm))
        for s in now:
            st[s[0]] = st[s[0]] * jnp.exp(g_last[s]) + jnp.where(head_blk, _dot(k_dec[s], v_new[s]), 0.0)
    for b in range(bb):
        st_ref[b] = st[b]
    o = jnp.concatenate([ob[(b, j)] for b in range(bb) for j in range(nt)], axis=0)
    ms = _dot2(o * o, ones_h) * (1.0 / HEAD_W)
    o = (o * lax.rsqrt(ms + NORM_EPS) * nw_ref[...] * _silu(gate)).astype(y_ref.dtype)
    for b in range(bb):
        y_ref[b] = o[b * tl:(b + 1) * tl]

    @pl.when(c == pl.num_programs(1) - 1)
    def _():
        sl_ref[...] = st_ref[...]
        cvo_ref[...] = cv_ref[...]


def _gdn_mixer(hn3, w_mix, conv0, s0, conv_w, a_log, dt_bias, norm_w, l):
    bsz, t, _ = hn3.shape
    cv0 = jnp.pad(conv0, ((0, 0), (8 - (GD_CONV - 1), 0), (0, 0)))
    cw = jnp.pad(conv_w, ((0, 8 - GD_CONV), (0, 0)))
    per_head = lambda a: jnp.repeat(a, HEAD_W).reshape(1, BRANCH_W)
    args = (cv0, _block_diag_heads(s0), cw, per_head(a_log), per_head(dt_bias),
            jnp.tile(norm_w, HEADS).reshape(1, BRANCH_W))
    full = lambda a: pl.BlockSpec(a.shape, lambda b, c: (0,) * a.ndim)
    bb, nt = _streams_per_step(bsz, t, l)
    y, sl, cvo = pl.pallas_call(
        functools.partial(_gdn_kernel, l=l, bb=bb, nt=nt),
        grid=(bsz // bb, t // (nt * l)),
        in_specs=[pl.BlockSpec((bb, nt * l, D_MODEL), lambda b, c: (b, c, 0)),
                  _w_cols(COL_GD, 1024), _w_cols(COL_AB, LANES),
                  pl.BlockSpec((bb, 8, GD_QKV), lambda b, c: (b, 0, 0)),
                  pl.BlockSpec((bb, BRANCH_W, BRANCH_W), lambda b, c: (b, 0, 0))]
                 + [full(a) for a in args[2:]],
        out_specs=[pl.BlockSpec((bb, nt * l, BRANCH_W), lambda b, c: (b, c, 0)),
                   pl.BlockSpec((bb, BRANCH_W, BRANCH_W), lambda b, c: (b, 0, 0)),
                   pl.BlockSpec((bb, 8, GD_QKV), lambda b, c: (b, 0, 0))],
        out_shape=[jax.ShapeDtypeStruct((bsz, t, BRANCH_W), BF16),
                   jax.ShapeDtypeStruct((bsz, BRANCH_W, BRANCH_W), F32),
                   jax.ShapeDtypeStruct((bsz, 8, GD_QKV), F32)],
        scratch_shapes=[pltpu.VMEM((bb, BRANCH_W, BRANCH_W), F32), pltpu.VMEM((bb, 8, GD_QKV), F32)],
        compiler_params=_cparams(("parallel", "arbitrary")),
    )(hn3, w_mix, w_mix, *args)
    return y, _diag_blocks(sl), cvo[:, 8 - (GD_CONV - 1):]


def _merge_kernel(ya_ref, yb_ref, yc_ref, yd_ref, x_ref, nw_ref, wg_ref, wbr_ref, wout_ref, o_ref):
    x = x_ref[...]
    ms = jnp.mean(x * x, axis=-1, keepdims=True)
    h = (x * lax.rsqrt(ms + NORM_EPS) * nw_ref[...]).astype(BF16)
    m = None
    for b, y_ref in enumerate((ya_ref, yb_ref, yc_ref, yd_ref)):
        gate = jnp.dot(h, wg_ref[:, b * D_MODEL:(b + 1) * D_MODEL], preferred_element_type=F32)
        br = jnp.dot(y_ref[...], wbr_ref[b], preferred_element_type=F32)
        term = (0.5 * jnp.tanh(0.5 * gate) + 0.5) * br
        m = term if m is None else m + term
    o_ref[...] = x + jnp.dot(m.astype(BF16), wout_ref[...], preferred_element_type=F32)


def _merge(ys, x2, norm_w, w_gate, w_branch, w_out, tm):
    n = x2.shape[0]
    yspec = pl.BlockSpec((tm, BRANCH_W), lambda i: (i, 0))
    once = pl.Buffered(1)
    return pl.pallas_call(
        _merge_kernel,
        grid=(n // tm,),
        in_specs=[yspec, yspec, yspec, yspec,
                  pl.BlockSpec((tm, D_MODEL), lambda i: (i, 0)),
                  pl.BlockSpec((1, D_MODEL), lambda i: (0, 0)),
                  pl.BlockSpec((D_MODEL, 4 * D_MODEL), lambda i: (0, 0), pipeline_mode=once),
                  pl.BlockSpec((4, BRANCH_W, D_MODEL), lambda i: (0, 0, 0), pipeline_mode=once),
                  pl.BlockSpec((D_MODEL, D_MODEL), lambda i: (0, 0), pipeline_mode=once)],
        out_specs=pl.BlockSpec((tm, D_MODEL), lambda i: (i, 0)),
        out_shape=jax.ShapeDtypeStruct((n, D_MODEL), F32),
        compiler_params=_cparams(("parallel",)),
    )(*[y.reshape(n, BRANCH_W) for y in ys], x2, norm_w.reshape(1, D_MODEL), w_gate,
      w_branch.astype(BF16), w_out.astype(BF16))


def _ffn_kernel(x_ref, nw_ref, w1_ref, w3_ref, w2_ref, nnw_ref, o_ref, hn_ref):
    x = x_ref[...]
    ms = jnp.mean(x * x, axis=-1, keepdims=True)
    h = (x * lax.rsqrt(ms + NORM_EPS) * nw_ref[...]).astype(BF16)
    a = _silu(jnp.dot(h, w1_ref[...], preferred_element_type=F32)) * jnp.dot(h, w3_ref[...], preferred_element_type=F32)
    y = x + jnp.dot(a.astype(BF16), w2_ref[...], preferred_element_type=F32)
    o_ref[...] = y
    ms = jnp.mean(y * y, axis=-1, keepdims=True)
    hn_ref[...] = (y * lax.rsqrt(ms + NORM_EPS) * nnw_ref[...]).astype(BF16)


def _ffn(x2, norm_w, w1, w3, w2, next_norm_w, tm):
    n = x2.shape[0]
    dff = w1.shape[1]
    once = pl.Buffered(1)
    spec = pl.BlockSpec((tm, D_MODEL), lambda i: (i, 0))
    vec = pl.BlockSpec((1, D_MODEL), lambda i: (0, 0))
    return pl.pallas_call(
        _ffn_kernel,
        grid=(n // tm,),
        in_specs=[spec, vec,
                  pl.BlockSpec((D_MODEL, dff), lambda i: (0, 0), pipeline_mode=once),
                  pl.BlockSpec((D_MODEL, dff), lambda i: (0, 0), pipeline_mode=once),
                  pl.BlockSpec((dff, D_MODEL), lambda i: (0, 0), pipeline_mode=once), vec],
        out_specs=[spec, spec],
        out_shape=[jax.ShapeDtypeStruct((n, D_MODEL), F32), jax.ShapeDtypeStruct((n, D_MODEL), BF16)],
        compiler_params=_cparams(("parallel",)),
    )(x2, norm_w.reshape(1, D_MODEL), w1.astype(BF16), w3.astype(BF16), w2.astype(BF16),
      next_norm_w.reshape(1, D_MODEL))


SC_CORES = 2
SC_SUBCORES = 16
SC_WINDOW = 128
SC_PIECE = 256
MOE_BLOCK = 512
MOE_FF = 1792
ROUTER_ROWS = 1024
HALF = D_MODEL // 2


HIGH16 = -65536


def _pack_bf16_pairs(h):
    bits = lax.bitcast_convert_type(h.astype(BF16).astype(F32), jnp.int32)
    return lax.shift_right_logical(bits[:, :HALF], 16) | (bits[:, HALF:] & HIGH16)


def _unpack_bf16_pairs(pieces):
    lo = [lax.bitcast_convert_type(lax.shift_left(w, 16), F32) for w in pieces]
    hi = [lax.bitcast_convert_type(w & HIGH16, F32) for w in pieces]
    return jnp.concatenate(lo + hi, axis=1).astype(BF16)


def _route_kernel(x_ref, nw_ref, wr_ref, br_ref, hp_ref, tok_ref, cnt_ref, seen_ref, *, tm):
    @pl.when(pl.program_id(0) == 0)
    def _():
        seen_ref[...] = jnp.zeros_like(seen_ref)

    x = x_ref[...]
    ms = jnp.mean(x * x, axis=-1, keepdims=True)
    h = x * lax.rsqrt(ms + NORM_EPS) * nw_ref[...]
    hp = _pack_bf16_pairs(h)
    for q in range(HALF // SC_PIECE):
        hp_ref[q] = hp[:, q * SC_PIECE:(q + 1) * SC_PIECE]
    lane = _iota((tm, LANES), 1)
    logits = jnp.where(lane < N_EXPERTS, _dot_hp(h, wr_ref[...]) + br_ref[...], -jnp.inf)
    m1 = jnp.max(logits, axis=1, keepdims=True)
    i1 = jnp.min(jnp.where(logits == m1, lane, LANES), axis=1, keepdims=True)
    rest = jnp.where(lane == i1, -jnp.inf, logits)
    m2 = jnp.max(rest, axis=1, keepdims=True)
    i2 = jnp.min(jnp.where(rest == m2, lane, LANES), axis=1, keepdims=True)
    e2 = jnp.exp(m2 - m1)
    g1 = 1.0 / (1.0 + e2)
    g2 = e2 / (1.0 + e2)
    self32 = ((lane == i1) | (lane == i2)).astype(F32)
    seen = seen_ref[...]
    tri = (_iota((tm, tm), 1) < _iota((tm, tm), 0)).astype(BF16)
    rank = jnp.dot(tri, self32.astype(BF16), preferred_element_type=F32) + seen
    r1 = jnp.sum(jnp.where(lane == i1, rank, 0.0), axis=1, keepdims=True)
    r2 = jnp.sum(jnp.where(lane == i2, rank, 0.0), axis=1, keepdims=True)
    cols = (i1.astype(F32), i2.astype(F32), r1, r2, g1, g2)
    tok = jnp.zeros((tm, LANES), F32)
    for j, col in enumerate(cols):
        tok = jnp.where(lane == j, col, tok)
    tok_ref[...] = tok
    seen = seen + jnp.sum(self32, axis=0, keepdims=True)
    seen_ref[...] = seen
    cnt_ref[...] = jnp.broadcast_to(seen, (8, LANES))


def _route(x2, norm_w, w_router, b_router, tm):
    n = x2.shape[0]
    wr = jnp.pad(w_router, ((0, 0), (0, LANES - N_EXPERTS)))
    br = jnp.pad(b_router, (0, LANES - N_EXPERTS)).reshape(1, LANES)
    return pl.pallas_call(
        functools.partial(_route_kernel, tm=tm),
        grid=(n // tm,),
        in_specs=[pl.BlockSpec((tm, D_MODEL), lambda i: (i, 0)),
                  pl.BlockSpec((1, D_MODEL), lambda i: (0, 0)),
                  pl.BlockSpec((D_MODEL, LANES), lambda i: (0, 0)),
                  pl.BlockSpec((1, LANES), lambda i: (0, 0))],
        out_specs=[pl.BlockSpec((HALF // SC_PIECE, tm, SC_PIECE), lambda i: (0, i, 0)),
                   pl.BlockSpec((tm, LANES), lambda i: (i, 0)),
                   pl.BlockSpec((8, LANES), lambda i: (0, 0))],
        out_shape=[jax.ShapeDtypeStruct((HALF // SC_PIECE, n, SC_PIECE), jnp.int32),
                   jax.ShapeDtypeStruct((n, LANES), F32),
                   jax.ShapeDtypeStruct((8, LANES), F32)],
        scratch_shapes=[pltpu.VMEM((1, LANES), F32)],
        compiler_params=_cparams(("arbitrary",)),
    )(x2, norm_w.reshape(1, D_MODEL), wr, br)


def _sc_rows_multiple(d):
    return SC_CORES * SC_SUBCORES * SC_WINDOW * SC_PIECE // d


def _sc_gather_rows(table, idx):
    f, v, _ = table.shape
    b = idx.shape[0]
    assert b % _sc_rows_multiple(f * SC_PIECE) == 0
    idx_all = (idx[None, :] + (jnp.arange(f, dtype=jnp.int32) * v)[:, None]).reshape(-1)
    return _sc_gather_pieces(table.reshape(f * v, SC_PIECE), idx_all).reshape(f, b, SC_PIECE)


def _sc_gather_pieces(table, idx):
    bp = idx.shape[0]
    d = table.shape[1]
    window = SC_WINDOW
    idx2 = idx.reshape(1, bp)
    mesh = plsc.VectorSubcoreMesh(core_axis_name="core", subcore_axis_name="subcore")

    @functools.partial(pl.kernel, out_type=jax.ShapeDtypeStruct((bp, d), table.dtype), mesh=mesh)
    def gather(x_hbm, i_hbm, o_hbm):
        def body(i_vmem, o_vmem):
            pltpu.sync_copy(x_hbm.at[i_vmem.at[0]], o_vmem)

        pltpu.emit_pipeline(
            body,
            grid=(bp // window,),
            in_specs=[pl.BlockSpec((1, window), index_map=lambda i: (0, i))],
            out_specs=[pl.BlockSpec((window, d), index_map=lambda i: (i, 0))],
            core_axis_name=("core", "subcore"),
            dimension_semantics=(pltpu.PARALLEL,),
        )(i_hbm, o_hbm)

    return gather(table, idx2)


def _sc_scatter_rows(rows, pos, n_slots):
    f, n0, _ = rows.shape
    n = -(-n0 // _sc_rows_multiple(D_MODEL)) * _sc_rows_multiple(D_MODEL)
    if n != n0:
        rows = jnp.pad(rows, ((0, 0), (0, n - n0), (0, 0)))
        pos = jnp.pad(pos.reshape(2, n0), ((0, 0), (0, n - n0)), constant_values=n_slots - 1)
    nb = n // SC_WINDOW
    assert (f * 2 * nb) % (SC_CORES * SC_SUBCORES) == 0
    idx = (pos.reshape(1, 2 * n) + (jnp.arange(f, dtype=jnp.int32) * n_slots)[:, None]).reshape(1, f * 2 * n)
    mesh = plsc.VectorSubcoreMesh(core_axis_name="core", subcore_axis_name="subcore")

    @functools.partial(pl.kernel, out_type=jax.ShapeDtypeStruct((f * n_slots, SC_PIECE), rows.dtype), mesh=mesh)
    def scatter(x_hbm, i_hbm, o_hbm):
        def body(x_vmem, i_vmem):
            pltpu.sync_copy(x_vmem, o_hbm.at[i_vmem.at[0]])

        pltpu.emit_pipeline(
            body,
            grid=(f * 2 * nb,),
            in_specs=[pl.BlockSpec((SC_WINDOW, SC_PIECE), index_map=lambda i: ((i // (2 * nb)) * nb + i % nb, 0)),
                      pl.BlockSpec((1, SC_WINDOW), index_map=lambda i: (0, i))],
            out_specs=[],
            core_axis_name=("core", "subcore"),
            dimension_semantics=(pltpu.PARALLEL,),
        )(x_hbm, i_hbm)

    return scatter(rows.reshape(f * n, SC_PIECE), idx).reshape(f, n_slots, SC_PIECE)


def _experts_kernel(be_ref, nb_ref, nv_ref, xs_ref, w1_ref, w3_ref, w2_ref, o_ref):
    g, c = pl.program_id(0), pl.program_id(1)

    @pl.when(c == 0)
    def _():
        o_ref[...] = jnp.zeros_like(o_ref)

    @pl.when(g < nb_ref[0])
    def _():
        live = _iota((xs_ref.shape[1], 1), 0) < nv_ref[g]
        xb = _unpack_bf16_pairs([jnp.where(live, xs_ref[q], 0) for q in range(HALF // SC_PIECE)])
        a = (_silu(jnp.dot(xb, w1_ref[...], preferred_element_type=F32))
             * jnp.dot(xb, w3_ref[...], preferred_element_type=F32))
        y = jnp.dot(a.astype(BF16), w2_ref[...], preferred_element_type=F32)
        for q in range(D_MODEL // SC_PIECE):
            o_ref[q] += y[:, q * SC_PIECE:(q + 1) * SC_PIECE]


def _experts(xs, block_expert, n_blocks, block_rows, w1, w3, w2, blk, tf):
    n_slots = xs.shape[1]
    dff = w1.shape[2]
    grid_spec = pltpu.PrefetchScalarGridSpec(
        num_scalar_prefetch=3,
        grid=(n_slots // blk, dff // tf),
        in_specs=[pl.BlockSpec((HALF // SC_PIECE, blk, SC_PIECE), lambda g, c, be, nb, nv: (0, g, 0)),
                  pl.BlockSpec((None, D_MODEL, tf), lambda g, c, be, nb, nv: (be[g], 0, c)),
                  pl.BlockSpec((None, D_MODEL, tf), lambda g, c, be, nb, nv: (be[g], 0, c)),
                  pl.BlockSpec((None, tf, D_MODEL), lambda g, c, be, nb, nv: (be[g], c, 0))],
        out_specs=pl.BlockSpec((D_MODEL // SC_PIECE, blk, SC_PIECE), lambda g, c, be, nb, nv: (0, g, 0)),
    )
    return pl.pallas_call(
        _experts_kernel,
        grid_spec=grid_spec,
        out_shape=jax.ShapeDtypeStruct((D_MODEL // SC_PIECE, n_slots, SC_PIECE), F32),
        compiler_params=_cparams(("parallel", "arbitrary")),
    )(block_expert, n_blocks, block_rows, xs, w1, w3, w2)


def _combine_kernel(x_ref, y1_ref, y2_ref, tok_ref, nw_ref, o_ref):
    tok = tok_ref[...]
    rows = lambda y_ref: jnp.concatenate([y_ref[q] for q in range(D_MODEL // SC_PIECE)], axis=1)
    x = x_ref[...] + tok[:, 4:5] * rows(y1_ref) + tok[:, 5:6] * rows(y2_ref)
    ms = jnp.mean(x * x, axis=-1, keepdims=True)
    o_ref[...] = x * lax.rsqrt(ms + NORM_EPS) * nw_ref[...]


def _combine(x2, ys2, tok, norm_w, tm):
    n = x2.shape[0]
    spec = pl.BlockSpec((tm, D_MODEL), lambda i: (i, 0))
    yspec = lambda off: pl.BlockSpec((D_MODEL // SC_PIECE, tm, SC_PIECE), lambda i: (0, i + off, 0))
    return pl.pallas_call(
        _combine_kernel,
        grid=(n // tm,),
        in_specs=[spec, yspec(0), yspec(n // tm),
                  pl.BlockSpec((tm, LANES), lambda i: (i, 0)), pl.BlockSpec((1, D_MODEL), lambda i: (0, 0))],
        out_specs=spec,
        out_shape=jax.ShapeDtypeStruct((n, D_MODEL), F32),
        compiler_params=_cparams(("parallel",)),
    )(x2, ys2, ys2, tok, norm_w.reshape(1, D_MODEL))


def _moe_final(x2, norm_w, w_router, b_router, w1, w3, w2, final_norm_w):
    n = x2.shape[0]
    blk = min(MOE_BLOCK, n)
    hp, tok, cnt = _route(x2, norm_w, w_router, b_router, min(ROUTER_ROWS, n))
    counts = cnt[0, :N_EXPERTS].astype(jnp.int32)
    padded = (counts + blk - 1) // blk * blk
    ends = jnp.cumsum(padded)
    base = ends - padded
    round_up = lambda a, m: -(-a // m) * m
    n_slots = round_up((-(-2 * n // blk) + N_EXPERTS + 1) * blk, _sc_rows_multiple(HALF))
    n_blocks = n_slots // blk
    i12 = tok[:, 0:2].astype(jnp.int32)
    pos = (base[i12] + tok[:, 2:4].astype(jnp.int32)).T.reshape(-1)
    block_start = jnp.arange(n_blocks, dtype=jnp.int32) * blk
    block_expert = jnp.minimum(jnp.sum(block_start[:, None] >= ends[None, :], axis=1), N_EXPERTS - 1).astype(jnp.int32)
    block_rows = jnp.clip((base + counts)[block_expert] - block_start, 0, blk).astype(jnp.int32)
    xs = _sc_scatter_rows(hp, pos, n_slots)
    ys = _experts(xs, block_expert, (ends[-1:] // blk).astype(jnp.int32), block_rows,
                  w1.astype(BF16), w3.astype(BF16), w2.astype(BF16), blk, MOE_FF)
    pos = jnp.pad(pos, (0, round_up(2 * n, _sc_rows_multiple(D_MODEL)) - 2 * n))
    ys2 = _sc_gather_rows(ys, pos)
    return _combine(x2, ys2, tok, final_norm_w, min(1024, n))


def _permute_w_in(w):
    s5, rw, sg = w[:, 0:256], w[:, 256:1280], w[:, 1280:1792]
    gd, ab, gates = w[:, 1792:2816], w[:, 2816:2824], w[:, 2824:6920]
    pad = jnp.zeros((D_MODEL, PROJ_COLS - COL_AB - 8), w.dtype)
    return jnp.concatenate([rw, gd, sg, s5, ab, pad], axis=1).astype(BF16), gates.astype(BF16)


def _run_trunk(x, s5_h, rw_s, rw_shift, gd_s, gd_conv, p, w_in_perm):
    bsz, t, _ = x.shape
    n = bsz * t
    l = min(MIX_CHUNK, t)
    tc_s5 = min(256, t)
    new = ([], [], [], [], [], [])
    x2 = x.reshape(n, D_MODEL)
    for layer in range(2):
        g = lambda name: p[name][layer]
        w_mix, w_gate = w_in_perm[layer]
        if layer == 0:
            hn = _norm_cast(x2, g('norm1_w'), min(1024, n))
        hn3 = hn.reshape(bsz, t, D_MODEL)
        y_a, s5_new = _s5_mixer(hn3, w_mix, s5_h[layer], g('s5_lam_re'), g('s5_lam_im'), g('s5_log_dt'), g('s5_b'),
                                g('s5_c'), g('s5_d'), g('s5_w_glu'), g('s5_b_glu'), tc_s5)
        y_b, rw_new, shift_new = _rwkv_mixer(hn3, w_mix, rw_shift[layer], rw_s[layer], g('rw_mu'), g('rw_w0'),
                                             g('rw_w2'), g('rw_a0'), g('rw_a2'), g('rw_g2'), g('rw_k_k'),
                                             g('rw_k_a'), g('rw_r_k'), g('rw_ln_w'), g('rw_ln_b'), l)
        y_c, sg_v = _sgu_mixer(hn3, w_mix, g('sg_ln_w'), g('sg_ln_b'), g('sg_w_s'), g('sg_b_s'))
        y_d, gd_new, conv_new = _gdn_mixer(hn3, w_mix, gd_conv[layer], gd_s[layer], g('gd_conv_w'), g('gd_a_log'),
                                           g('gd_dt_bias'), g('gd_norm_w'), l)
        x2 = _merge((y_a, y_b, y_c, y_d), x2, g('norm1_w'), w_gate, g('w_branch'), g('w_out'), min(512, n))
        j = layer // 2
        if layer % 2 == 0:
            x2, hn = _ffn(x2, g('norm2_w'), p['ffn_w1'][j], p['ffn_w3'][j], p['ffn_w2'][j],
                          p['norm1_w'][layer + 1], min(512, n))
        else:
            y = _moe_final(x2, g('norm2_w'), p['moe_router'][j], p['moe_router_b'][j],
                           p['moe_w1'][j], p['moe_w3'][j], p['moe_w2'][j], p['final_norm_w'])
        for lst, s in zip(new, (s5_new, rw_new, shift_new, gd_new, conv_new, sg_v)):
            lst.append(s)
    return y.reshape(bsz, t, D_MODEL), [jnp.stack(lst) for lst in new]


def kernel(x_prompt, x_sample, state_s5, state_rwkv, state_rwkv_shift, state_gdn, state_gdn_conv, norm1_w, w_in, s5_lam_re, s5_lam_im, s5_log_dt, s5_b, s5_c, s5_d, s5_w_glu, s5_b_glu, rw_mu, rw_w0, rw_w2, rw_a0, rw_a2, rw_g2, rw_k_k, rw_k_a, rw_r_k, rw_ln_w, rw_ln_b, sg_ln_w, sg_ln_b, sg_w_s, sg_b_s, gd_conv_w, gd_a_log, gd_dt_bias, gd_norm_w, w_branch, w_out, norm2_w, ffn_w1, ffn_w3, ffn_w2, moe_router, moe_router_b, moe_w1, moe_w3, moe_w2, final_norm_w):
    p = {
        'norm1_w': norm1_w, 's5_lam_re': s5_lam_re, 's5_lam_im': s5_lam_im, 's5_log_dt': s5_log_dt,
        's5_b': s5_b, 's5_c': s5_c, 's5_d': s5_d, 's5_w_glu': s5_w_glu, 's5_b_glu': s5_b_glu,
        'rw_mu': rw_mu, 'rw_w0': rw_w0, 'rw_w2': rw_w2, 'rw_a0': rw_a0, 'rw_a2': rw_a2, 'rw_g2': rw_g2,
        'rw_k_k': rw_k_k, 'rw_k_a': rw_k_a, 'rw_r_k': rw_r_k, 'rw_ln_w': rw_ln_w, 'rw_ln_b': rw_ln_b,
        'sg_ln_w': sg_ln_w, 'sg_ln_b': sg_ln_b, 'sg_w_s': sg_w_s, 'sg_b_s': sg_b_s,
        'gd_conv_w': gd_conv_w, 'gd_a_log': gd_a_log, 'gd_dt_bias': gd_dt_bias, 'gd_norm_w': gd_norm_w,
        'w_branch': w_branch, 'w_out': w_out, 'norm2_w': norm2_w,
        'ffn_w1': ffn_w1, 'ffn_w3': ffn_w3, 'ffn_w2': ffn_w2,
        'moe_router': moe_router, 'moe_router_b': moe_router_b, 'moe_w1': moe_w1, 'moe_w3': moe_w3, 'moe_w2': moe_w2,
        'final_norm_w': final_norm_w,
    }
    w_in_perm = [_permute_w_in(w_in[layer]) for layer in range(2)]
    bp, dt = x_prompt.shape[0], x_prompt.dtype
    depth = w_in.shape[0]
    y_prompt, (s5_p, rw_p, rwsh_p, gd_p, gdc_p, _) = _run_trunk(
        x_prompt,
        jnp.zeros((depth, bp, S5_GROUPS, S5_STATE, 2), dt),
        jnp.zeros((depth, bp, HEADS, HEAD_W, HEAD_W), dt),
        jnp.zeros((depth, bp, RW_COLS), dt),
        jnp.zeros((depth, bp, HEADS, HEAD_W, HEAD_W), dt),
        jnp.zeros((depth, bp, GD_CONV - 1, GD_QKV), dt),
        p, w_in_perm)
    y_sample, (s5_s, rw_s, rwsh_s, gd_s, gdc_s, sgv_s) = _run_trunk(
        x_sample, state_s5, state_rwkv, state_rwkv_shift, state_gdn, state_gdn_conv, p, w_in_perm)
    return (y_prompt, y_sample, s5_p, rw_p, rwsh_p, gd_p, gdc_p, s5_s, rw_s, rwsh_s, gd_s, gdc_s, sgv_s)
```
